```python
import math
import jax
import jax.numpy as jnp
from jax import lax
import numpy as np

D_MODEL = 1024
BATCH = 8
SEQ = 4096
DEPTH = 2

ATT_HEADS = 4
ATT_HEAD_DIM = 64
ATT_WIDTH = ATT_HEADS * ATT_HEAD_DIM
DILATED_PATTERNS = ((128, 1), (512, 4), (2048, 16))
POOL_WINDOWS = (2, 4, 8, 16)
POOL_GROUP_DIM = 64
POOL_WIDTH = len(POOL_WINDOWS) * POOL_GROUP_DIM
DN_HEADS = 4
DN_HEAD_DIM = 128
DN_WIDTH = DN_HEADS * DN_HEAD_DIM
DN_CONV = 4
DN_CHUNK = 64
MIX_WIDTH = ATT_WIDTH + POOL_WIDTH + DN_WIDTH
IN_SPLITS = (ATT_WIDTH, ATT_WIDTH, ATT_WIDTH, POOL_WIDTH,
             DN_WIDTH, DN_WIDTH, DN_WIDTH, DN_WIDTH, DN_HEADS, DN_HEADS)
IN_WIDTH = sum(IN_SPLITS)
D_FF = 2816
ROPE_THETA = 10000.0
EPS = 1e-6

kernel_name = "hybrid_dilated_pool_deltanet_macaron"


def rms_norm(x, w):
    xf = x.astype(jnp.float32)
    y = xf * lax.rsqrt(jnp.mean(xf * xf, axis=-1, keepdims=True) + EPS)
    return (y * w).astype(x.dtype)


def swiglu(h, w_gate, w_up, w_down):
    return (jax.nn.silu(h @ w_gate) * (h @ w_up)) @ w_down


def rotary_tables(positions, dim):
    inv_freq = ROPE_THETA ** (-jnp.arange(0, dim, 2, dtype=jnp.float32) / dim)
    ang = positions.astype(jnp.float32)[..., None] * inv_freq
    return jnp.cos(ang)[:, :, None, :], jnp.sin(ang)[:, :, None, :]


def apply_rope(t, cos, sin):
    t1, t2 = jnp.split(t.astype(jnp.float32), 2, axis=-1)
    return jnp.concatenate([t1 * cos - t2 * sin, t2 * cos + t1 * sin], axis=-1).astype(t.dtype)


def dilated_window_attention(q, k, v, window, dilation):
    B, S, H, E = q.shape
    n = window // dilation
    span = n * dilation
    L = -(-S // span) * span
    M = L // dilation
    nb = M // n

    def to_blocks(t):
        t = jnp.pad(t, ((0, 0), (0, L - S), (0, 0), (0, 0)))
        return t.reshape(B, nb, n, dilation, H, E)

    def with_prev(t):
        prev = jnp.pad(t, ((0, 0), (1, 0), (0, 0), (0, 0), (0, 0), (0, 0)))[:, :-1]
        return jnp.concatenate([prev, t], axis=2)

    qb = to_blocks(q)
    kk = with_prev(to_blocks(k))
    vv = with_prev(to_blocks(v))
    s = jnp.einsum('bcqrhe,bckrhe->bcrhqk', qb, kk).astype(jnp.float32) / math.sqrt(E)
    qi = jnp.arange(n)[:, None]
    ki = jnp.arange(2 * n)[None, :]
    dist = n + qi - ki
    blk = jnp.arange(nb)[:, None, None]
    valid = (dist >= 0) & (dist <= n) & ((blk - 1) * n + ki >= 0)
    s = jnp.where(valid[None, :, None, None], s, -jnp.inf)
    m = jnp.max(s, axis=-1, keepdims=True)
    p = jnp.exp(s - m)
    den = jnp.sum(p, axis=-1, keepdims=True)
    o = jnp.einsum('bcrhqk,bckrhe->bcqrhe', (p / den).astype(vv.dtype), vv)
    lse = (m + jnp.log(den))[..., 0]
    o = o.reshape(B, L, H, E)[:, :S]
    lse = lse.transpose(0, 1, 4, 2, 3).reshape(B, L, H)[:, :S]
    return o, lse


def dilated_attention(q, k, v):
    outs, lses = [], []
    for window, dilation in DILATED_PATTERNS:
        o, lse = dilated_window_attention(q, k, v, window, dilation)
        outs.append(o)
        lses.append(lse)
    wts = jax.nn.softmax(jnp.stack(lses, axis=0), axis=0)
    y = sum(wts[i][..., None] * outs[i].astype(jnp.float32) for i in range(len(outs)))
    return y.astype(q.dtype)


def multiscale_pool(u, pool_w, pool_scale):
    B, S, _ = u.shape
    G = len(POOL_WINDOWS)
    ug = u.reshape(B, S, G, POOL_GROUP_DIM)
    cs = jnp.cumsum(ug.astype(jnp.float32), axis=1)
    t = jnp.arange(S)
    pooled = []
    for g, w in enumerate(POOL_WINDOWS):
        csg = cs[:, :, g]
        lower = jnp.pad(csg, ((0, 0), (w, 0), (0, 0)))[:, :S]
        count = jnp.minimum(t + 1, w).astype(jnp.float32)[None, :, None]
        pooled.append((csg - lower) / count)
    pooled = jnp.stack(pooled, axis=2) - ug.astype(jnp.float32)
    y = jnp.einsum('bsgc,gcd->bsgd', pooled.astype(u.dtype), pool_w)
    return y.reshape(B, S, POOL_WIDTH) * pool_scale


def causal_depthwise_conv(u, w):
    K = w.shape[0]
    up = jnp.pad(u, ((0, 0), (K - 1, 0), (0, 0)))
    return lax.conv_general_dilated(up, w[:, None, :], window_strides=(1,), padding='VALID',
                                    dimension_numbers=('NWC', 'WIO', 'NWC'),
                                    feature_group_count=u.shape[-1])


def l2_normalize(t):
    tf = t.astype(jnp.float32)
    return tf * lax.rsqrt(jnp.sum(tf * tf, axis=-1, keepdims=True) + EPS)


def chunk_gated_delta_rule(q, k, v, g, beta):
    B, S, H, Dk = q.shape
    Dv = v.shape[-1]
    C = DN_CHUNK
    N = S // C
    to_c = lambda t: t.astype(jnp.float32).reshape(B, N, C, H, -1).transpose(1, 0, 3, 2, 4)
    qc, kc, vc = to_c(q), to_c(k), to_c(v)
    beta = beta.astype(jnp.float32).reshape(B, N, C, H).transpose(1, 0, 3, 2)
    g = jnp.cumsum(g.astype(jnp.float32).reshape(B, N, C, H).transpose(1, 0, 3, 2), axis=-1)
    kb = kc * beta[..., None]
    vb = vc * beta[..., None]
    lower = jnp.tril(jnp.ones((C, C), dtype=bool))
    strict = jnp.tril(jnp.ones((C, C), dtype=bool), -1)
    diff = g[..., :, None] - g[..., None, :]
    decay = jnp.where(lower, jnp.exp(jnp.where(lower, diff, 0.0)), 0.0)
    A = jnp.where(strict, jnp.einsum('nbhid,nbhjd->nbhij', kb, kc) * decay, 0.0)
    eye = jnp.eye(C, dtype=jnp.float32)
    T = lax.linalg.triangular_solve(eye + A, jnp.broadcast_to(eye, A.shape),
                                    left_side=True, lower=True)
    u = T @ vb
    w = T @ (kb * jnp.exp(g)[..., None])
    intra = jnp.where(lower, jnp.einsum('nbhid,nbhjd->nbhij', qc, kc) * decay, 0.0)

    def step(state, xs):
        q_i, k_i, u_i, w_i, g_i, a_i = xs
        v_new = u_i - w_i @ state
        o = (q_i * jnp.exp(g_i)[..., None]) @ state + a_i @ v_new
        g_last = g_i[..., -1]
        k_dec = k_i * jnp.exp(g_last[..., None] - g_i)[..., None]
        state = state * jnp.exp(g_last)[..., None, None] + jnp.einsum('bhck,bhcv->bhkv', k_dec, v_new)
        return state, o

    state0 = jnp.zeros((B, H, Dk, Dv), dtype=jnp.float32)
    _, o = lax.scan(step, state0, (qc, kc, u, w, g, intra))
    return o.transpose(1, 0, 3, 2, 4).reshape(B, S, H, Dv)


def gated_deltanet(q, k, v, z, b, a, conv_w, a_log, dt_bias, norm_w):
    B, S, _ = q.shape
    qkv = jax.nn.silu(causal_depthwise_conv(jnp.concatenate([q, k, v], axis=-1), conv_w))
    q, k, v = jnp.split(qkv, 3, axis=-1)
    q = l2_normalize(q.reshape(B, S, DN_HEADS, DN_HEAD_DIM)) * (DN_HEAD_DIM ** -0.5)
    k = l2_normalize(k.reshape(B, S, DN_HEADS, DN_HEAD_DIM))
    v = v.reshape(B, S, DN_HEADS, DN_HEAD_DIM)
    beta = jax.nn.sigmoid(b.astype(jnp.float32))
    g = -jnp.exp(a_log.astype(jnp.float32)) * jax.nn.softplus(a.astype(jnp.float32) + dt_bias)
    o = chunk_gated_delta_rule(q, k, v, g, beta)
    o = rms_norm(o, norm_w) * jax.nn.silu(z.reshape(B, S, DN_HEADS, DN_HEAD_DIM).astype(jnp.float32))
    return o.reshape(B, S, DN_WIDTH).astype(z.dtype)


def _fwd_setup_inputs(seed: int = 0) -> dict:
    key = jax.random.key(seed)
    ks = jax.random.split(key, 24)
    nrm = lambda kk, shape, fan_in: jax.random.normal(kk, shape, jnp.float32) * (fan_in ** -0.5)
    gain = lambda kk, shape: 1.0 + 0.05 * jax.random.normal(kk, shape, jnp.float32)
    L = DEPTH
    x = jax.random.normal(ks[0], (BATCH, SEQ, D_MODEL), jnp.float32)
    positions = jnp.broadcast_to(jnp.arange(SEQ, dtype=jnp.int32), (BATCH, SEQ))
    dt = jnp.exp(jax.random.uniform(ks[12], (L, DN_HEADS), jnp.float32,
                                    math.log(1e-3), math.log(1e-1)))
    dt_bias = dt + jnp.log(-jnp.expm1(-dt))
    a_log = jnp.log(jax.random.uniform(ks[11], (L, DN_HEADS), jnp.float32, 1.0, 16.0))
    return {
        "x": x,
        "positions": positions,
        "ffn1_norm": gain(ks[1], (L, D_MODEL)),
        "ffn1_w_gate": nrm(ks[2], (L, D_MODEL, D_FF), D_MODEL),
        "ffn1_w_up": nrm(ks[3], (L, D_MODEL, D_FF), D_MODEL),
        "ffn1_w_down": nrm(ks[4], (L, D_FF, D_MODEL), D_FF),
        "mix_norm": gain(ks[5], (L, D_MODEL)),
        "w_in": nrm(ks[6], (L, D_MODEL, IN_WIDTH), D_MODEL),
        "pool_w": nrm(ks[7], (L, len(POOL_WINDOWS), POOL_GROUP_DIM, POOL_GROUP_DIM), POOL_GROUP_DIM),
        "pool_scale": gain(ks[8], (L, POOL_WIDTH)),
        "dn_conv_w": nrm(ks[9], (L, DN_CONV, 3 * DN_WIDTH), DN_CONV),
        "dn_a_log": a_log,
        "dn_dt_bias": dt_bias,
        "dn_out_norm": gain(ks[13], (L, DN_HEAD_DIM)),
        "w_out": nrm(ks[14], (L, MIX_WIDTH, D_MODEL), MIX_WIDTH),
        "ffn2_norm": gain(ks[15], (L, D_MODEL)),
        "ffn2_w_gate": nrm(ks[16], (L, D_MODEL, D_FF), D_MODEL),
        "ffn2_w_up": nrm(ks[17], (L, D_MODEL, D_FF), D_MODEL),
        "ffn2_w_down": nrm(ks[18], (L, D_FF, D_MODEL), D_FF),
        "final_norm": gain(ks[19], (D_MODEL,)),
    }


def _fwd_reference(x, positions, ffn1_norm, ffn1_w_gate, ffn1_w_up, ffn1_w_down, mix_norm, w_in,
              pool_w, pool_scale, dn_conv_w, dn_a_log, dn_dt_bias, dn_out_norm, w_out,
              ffn2_norm, ffn2_w_gate, ffn2_w_up, ffn2_w_down, final_norm):
    B, S, _ = x.shape
    cos, sin = rotary_tables(positions, ATT_HEAD_DIM)
    split_at = np.cumsum(IN_SPLITS)[:-1].tolist()
    for l in range(DEPTH):
        h = rms_norm(x, ffn1_norm[l])
        x = x + 0.5 * swiglu(h, ffn1_w_gate[l], ffn1_w_up[l], ffn1_w_down[l])
        h = rms_norm(x, mix_norm[l])
        proj = h @ w_in[l]
        aq, ak, av, pu, dq, dk, dv, dz, db, da = jnp.split(proj, split_at, axis=-1)
        aq = apply_rope(aq.reshape(B, S, ATT_HEADS, ATT_HEAD_DIM), cos, sin)
        ak = apply_rope(ak.reshape(B, S, ATT_HEADS, ATT_HEAD_DIM), cos, sin)
        av = av.reshape(B, S, ATT_HEADS, ATT_HEAD_DIM)
        ya = dilated_attention(aq, ak, av).reshape(B, S, ATT_WIDTH)
        yb = multiscale_pool(pu, pool_w[l], pool_scale[l])
        yc = gated_deltanet(dq, dk, dv, dz, db, da, dn_conv_w[l], dn_a_log[l],
                            dn_dt_bias[l], dn_out_norm[l])
        x = x + jnp.concatenate([ya, yb, yc], axis=-1) @ w_out[l]
        h = rms_norm(x, ffn2_norm[l])
        x = x + 0.5 * swiglu(h, ffn2_w_gate[l], ffn2_w_up[l], ffn2_w_down[l])
    return rms_norm(x, final_norm)


import jax as _jax
import jax.numpy as _jnp

TWIN_FORMAT = 'train_step'
FWD_PARAMS = ['x', 'positions', 'ffn1_norm', 'ffn1_w_gate', 'ffn1_w_up', 'ffn1_w_down', 'mix_norm', 'w_in', 'pool_w', 'pool_scale', 'dn_conv_w', 'dn_a_log', 'dn_dt_bias', 'dn_out_norm', 'w_out', 'ffn2_norm', 'ffn2_w_gate', 'ffn2_w_up', 'ffn2_w_down', 'final_norm']
TWIN_WEIGHTS = ['ffn1_norm', 'ffn1_w_gate', 'ffn1_w_up', 'ffn1_w_down', 'mix_norm', 'w_in', 'pool_w', 'pool_scale', 'dn_conv_w', 'dn_a_log', 'dn_dt_bias', 'dn_out_norm', 'w_out', 'ffn2_norm', 'ffn2_w_gate', 'ffn2_w_up', 'ffn2_w_down', 'final_norm']
TWIN_DIFF_INPUT = 'x'
TWIN_INPUTS = ['x', 'positions', 'ffn1_norm', 'ffn1_w_gate', 'ffn1_w_up', 'ffn1_w_down', 'mix_norm', 'w_in', 'pool_w', 'pool_scale', 'dn_conv_w', 'dn_a_log', 'dn_dt_bias', 'dn_out_norm', 'w_out', 'ffn2_norm', 'ffn2_w_gate', 'ffn2_w_up', 'ffn2_w_down', 'final_norm', 'loss_target', 'm_ffn1_norm', 'm_ffn1_w_gate', 'm_ffn1_w_up', 'm_ffn1_w_down', 'm_mix_norm', 'm_w_in', 'm_pool_w', 'm_pool_scale', 'm_dn_conv_w', 'm_dn_a_log', 'm_dn_dt_bias', 'm_dn_out_norm', 'm_w_out', 'm_ffn2_norm', 'm_ffn2_w_gate', 'm_ffn2_w_up', 'm_ffn2_w_down', 'm_final_norm', 'v_ffn1_norm', 'v_ffn1_w_gate', 'v_ffn1_w_up', 'v_ffn1_w_down', 'v_mix_norm', 'v_w_in', 'v_pool_w', 'v_pool_scale', 'v_dn_conv_w', 'v_dn_a_log', 'v_dn_dt_bias', 'v_dn_out_norm', 'v_w_out', 'v_ffn2_norm', 'v_ffn2_w_gate', 'v_ffn2_w_up', 'v_ffn2_w_down', 'v_final_norm']
TWIN_OUTPUTS = ['loss', 'grad_x', 'grad_ffn1_norm', 'grad_ffn1_w_gate', 'grad_ffn1_w_up', 'grad_ffn1_w_down', 'grad_mix_norm', 'grad_w_in', 'grad_pool_w', 'grad_pool_scale', 'grad_dn_conv_w', 'grad_dn_a_log', 'grad_dn_dt_bias', 'grad_dn_out_norm', 'grad_w_out', 'grad_ffn2_norm', 'grad_ffn2_w_gate', 'grad_ffn2_w_up', 'grad_ffn2_w_down', 'grad_final_norm', 'delta_ffn1_norm', 'delta_ffn1_w_gate', 'delta_ffn1_w_up', 'delta_ffn1_w_down', 'delta_mix_norm', 'delta_w_in', 'delta_pool_w', 'delta_pool_scale', 'delta_dn_conv_w', 'delta_dn_a_log', 'delta_dn_dt_bias', 'delta_dn_out_norm', 'delta_w_out', 'delta_ffn2_norm', 'delta_ffn2_w_gate', 'delta_ffn2_w_up', 'delta_ffn2_w_down', 'delta_final_norm', 'new_m_ffn1_norm', 'new_m_ffn1_w_gate', 'new_m_ffn1_w_up', 'new_m_ffn1_w_down', 'new_m_mix_norm', 'new_m_w_in', 'new_m_pool_w', 'new_m_pool_scale', 'new_m_dn_conv_w', 'new_m_dn_a_log', 'new_m_dn_dt_bias', 'new_m_dn_out_norm', 'new_m_w_out', 'new_m_ffn2_norm', 'new_m_ffn2_w_gate', 'new_m_ffn2_w_up', 'new_m_ffn2_w_down', 'new_m_final_norm', 'new_v_ffn1_norm', 'new_v_ffn1_w_gate', 'new_v_ffn1_w_up', 'new_v_ffn1_w_down', 'new_v_mix_norm', 'new_v_w_in', 'new_v_pool_w', 'new_v_pool_scale', 'new_v_dn_conv_w', 'new_v_dn_a_log', 'new_v_dn_dt_bias', 'new_v_dn_out_norm', 'new_v_w_out', 'new_v_ffn2_norm', 'new_v_ffn2_w_gate', 'new_v_ffn2_w_up', 'new_v_ffn2_w_down', 'new_v_final_norm']
TWIN_LEAF_KINDS = {'loss': 'loss', 'grad_x': 'grad_x', 'grad_ffn1_norm': 'grad_w', 'grad_ffn1_w_gate': 'grad_w', 'grad_ffn1_w_up': 'grad_w', 'grad_ffn1_w_down': 'grad_w', 'grad_mix_norm': 'grad_w', 'grad_w_in': 'grad_w', 'grad_pool_w': 'grad_w', 'grad_pool_scale': 'grad_w', 'grad_dn_conv_w': 'grad_w', 'grad_dn_a_log': 'grad_w', 'grad_dn_dt_bias': 'grad_w', 'grad_dn_out_norm': 'grad_w', 'grad_w_out': 'grad_w', 'grad_ffn2_norm': 'grad_w', 'grad_ffn2_w_gate': 'grad_w', 'grad_ffn2_w_up': 'grad_w', 'grad_ffn2_w_down': 'grad_w', 'grad_final_norm': 'grad_w', 'delta_ffn1_norm': 'delta_w', 'delta_ffn1_w_gate': 'delta_w', 'delta_ffn1_w_up': 'delta_w', 'delta_ffn1_w_down': 'delta_w', 'delta_mix_norm': 'delta_w', 'delta_w_in': 'delta_w', 'delta_pool_w': 'delta_w', 'delta_pool_scale': 'delta_w', 'delta_dn_conv_w': 'delta_w', 'delta_dn_a_log': 'delta_w', 'delta_dn_dt_bias': 'delta_w', 'delta_dn_out_norm': 'delta_w', 'delta_w_out': 'delta_w', 'delta_ffn2_norm': 'delta_w', 'delta_ffn2_w_gate': 'delta_w', 'delta_ffn2_w_up': 'delta_w', 'delta_ffn2_w_down': 'delta_w', 'delta_final_norm': 'delta_w', 'new_m_ffn1_norm': 'new_m', 'new_m_ffn1_w_gate': 'new_m', 'new_m_ffn1_w_up': 'new_m', 'new_m_ffn1_w_down': 'new_m', 'new_m_mix_norm': 'new_m', 'new_m_w_in': 'new_m', 'new_m_pool_w': 'new_m', 'new_m_pool_scale': 'new_m', 'new_m_dn_conv_w': 'new_m', 'new_m_dn_a_log': 'new_m', 'new_m_dn_dt_bias': 'new_m', 'new_m_dn_out_norm': 'new_m', 'new_m_w_out': 'new_m', 'new_m_ffn2_norm': 'new_m', 'new_m_ffn2_w_gate': 'new_m', 'new_m_ffn2_w_up': 'new_m', 'new_m_ffn2_w_down': 'new_m', 'new_m_final_norm': 'new_m', 'new_v_ffn1_norm': 'new_v', 'new_v_ffn1_w_gate': 'new_v', 'new_v_ffn1_w_up': 'new_v', 'new_v_ffn1_w_down': 'new_v', 'new_v_mix_norm': 'new_v', 'new_v_w_in': 'new_v', 'new_v_pool_w': 'new_v', 'new_v_pool_scale': 'new_v', 'new_v_dn_conv_w': 'new_v', 'new_v_dn_a_log': 'new_v', 'new_v_dn_dt_bias': 'new_v', 'new_v_dn_out_norm': 'new_v', 'new_v_w_out': 'new_v', 'new_v_ffn2_norm': 'new_v', 'new_v_ffn2_w_gate': 'new_v', 'new_v_ffn2_w_up': 'new_v', 'new_v_ffn2_w_down': 'new_v', 'new_v_final_norm': 'new_v'}


def _forward(args):
    return _fwd_reference(*[args[k] for k in FWD_PARAMS])


def _output_shape():
    out = _jax.eval_shape(lambda: _forward(_fwd_setup_inputs(0)))
    return out.shape, out.dtype

N_MICROBATCH = 1
ADAM_LR = 0.001
ADAM_B1 = 0.9
ADAM_B2 = 0.999
ADAM_EPS = 1e-08
ADAM_WD = 0.01
ADAM_STEP = 10
PER_EXAMPLE_BATCH_AXIS = {'x': 0, 'positions': 0, 'loss_target': 0}
SHARED_INPUTS = []
_WEIGHT_DTYPES = {'ffn1_norm': _jnp.float32, 'ffn1_w_gate': _jnp.float32, 'ffn1_w_up': _jnp.float32, 'ffn1_w_down': _jnp.float32, 'mix_norm': _jnp.float32, 'w_in': _jnp.float32, 'pool_w': _jnp.float32, 'pool_scale': _jnp.float32, 'dn_conv_w': _jnp.float32, 'dn_a_log': _jnp.float32, 'dn_dt_bias': _jnp.float32, 'dn_out_norm': _jnp.float32, 'w_out': _jnp.float32, 'ffn2_norm': _jnp.float32, 'ffn2_w_gate': _jnp.float32, 'ffn2_w_up': _jnp.float32, 'ffn2_w_down': _jnp.float32, 'final_norm': _jnp.float32}
MOMENT_SCALE = {'ffn1_norm': 8.155664e-02, 'ffn1_w_gate': 3.522631e-02, 'ffn1_w_up': 3.408078e-02, 'ffn1_w_down': 5.650708e-02, 'mix_norm': 1.275753e-01, 'w_in': 7.301734e-02, 'pool_w': 1.357826e-01, 'pool_scale': 1.352987e-01, 'dn_conv_w': 6.683826e-02, 'dn_a_log': 2.696511e-01, 'dn_dt_bias': 2.656407e-01, 'dn_out_norm': 1.858054e-01, 'w_out': 9.420711e-02, 'ffn2_norm': 6.262122e-02, 'ffn2_w_gate': 2.666071e-02, 'ffn2_w_up': 2.599475e-02, 'ffn2_w_down': 4.312175e-02, 'final_norm': 3.195787e+01}


def _to_microbatches(a, axis):
    t = _jnp.moveaxis(a, axis, 0)
    t = t.reshape((N_MICROBATCH, t.shape[0] // N_MICROBATCH) + t.shape[1:])
    return _jnp.moveaxis(t, 1, axis + 1)


def setup_inputs(seed: int = 0) -> dict:
    inp = _fwd_setup_inputs(seed)
    key = _jax.random.fold_in(_jax.random.key(seed), 7919)
    shape, _ = _output_shape()
    out = dict(inp)
    out["loss_target"] = _jax.random.normal(_jax.random.fold_in(key, 0), shape, _jnp.float32)
    for i, name in enumerate(TWIN_WEIGHTS):
        w = inp[name].astype(_jnp.float32)
        if MOMENT_SCALE is None:
            s = _jnp.sqrt(_jnp.mean(_jnp.square(w)) + 1e-30)
        else:
            s = MOMENT_SCALE[name]
        km, kv = _jax.random.split(_jax.random.fold_in(key, i + 1))
        out[name] = w
        out["m_" + name] = s * _jax.random.normal(km, w.shape, _jnp.float32)
        out["v_" + name] = (s * s) * _jax.random.uniform(kv, w.shape, _jnp.float32, 0.5, 1.5)
    if N_MICROBATCH > 1:
        for name, axis in PER_EXAMPLE_BATCH_AXIS.items():
            out[name] = _to_microbatches(out[name], axis)
    return {'x': out['x'], 'positions': out['positions'], 'ffn1_norm': out['ffn1_norm'], 'ffn1_w_gate': out['ffn1_w_gate'], 'ffn1_w_up': out['ffn1_w_up'], 'ffn1_w_down': out['ffn1_w_down'], 'mix_norm': out['mix_norm'], 'w_in': out['w_in'], 'pool_w': out['pool_w'], 'pool_scale': out['pool_scale'], 'dn_conv_w': out['dn_conv_w'], 'dn_a_log': out['dn_a_log'], 'dn_dt_bias': out['dn_dt_bias'], 'dn_out_norm': out['dn_out_norm'], 'w_out': out['w_out'], 'ffn2_norm': out['ffn2_norm'], 'ffn2_w_gate': out['ffn2_w_gate'], 'ffn2_w_up': out['ffn2_w_up'], 'ffn2_w_down': out['ffn2_w_down'], 'final_norm': out['final_norm'], 'loss_target': out['loss_target'], 'm_ffn1_norm': out['m_ffn1_norm'], 'm_ffn1_w_gate': out['m_ffn1_w_gate'], 'm_ffn1_w_up': out['m_ffn1_w_up'], 'm_ffn1_w_down': out['m_ffn1_w_down'], 'm_mix_norm': out['m_mix_norm'], 'm_w_in': out['m_w_in'], 'm_pool_w': out['m_pool_w'], 'm_pool_scale': out['m_pool_scale'], 'm_dn_conv_w': out['m_dn_conv_w'], 'm_dn_a_log': out['m_dn_a_log'], 'm_dn_dt_bias': out['m_dn_dt_bias'], 'm_dn_out_norm': out['m_dn_out_norm'], 'm_w_out': out['m_w_out'], 'm_ffn2_norm': out['m_ffn2_norm'], 'm_ffn2_w_gate': out['m_ffn2_w_gate'], 'm_ffn2_w_up': out['m_ffn2_w_up'], 'm_ffn2_w_down': out['m_ffn2_w_down'], 'm_final_norm': out['m_final_norm'], 'v_ffn1_norm': out['v_ffn1_norm'], 'v_ffn1_w_gate': out['v_ffn1_w_gate'], 'v_ffn1_w_up': out['v_ffn1_w_up'], 'v_ffn1_w_down': out['v_ffn1_w_down'], 'v_mix_norm': out['v_mix_norm'], 'v_w_in': out['v_w_in'], 'v_pool_w': out['v_pool_w'], 'v_pool_scale': out['v_pool_scale'], 'v_dn_conv_w': out['v_dn_conv_w'], 'v_dn_a_log': out['v_dn_a_log'], 'v_dn_dt_bias': out['v_dn_dt_bias'], 'v_dn_out_norm': out['v_dn_out_norm'], 'v_w_out': out['v_w_out'], 'v_ffn2_norm': out['v_ffn2_norm'], 'v_ffn2_w_gate': out['v_ffn2_w_gate'], 'v_ffn2_w_up': out['v_ffn2_w_up'], 'v_ffn2_w_down': out['v_ffn2_w_down'], 'v_final_norm': out['v_final_norm']}


def _loss(weights, diff, rest, loss_target):
    with _jax.named_scope("forward"):
        args = {**rest, TWIN_DIFF_INPUT: diff, **{k: w.astype(_WEIGHT_DTYPES[k]) for k, w in weights.items()}}
        y = _forward(args)
    with _jax.named_scope("loss_head"):
        err = _jnp.square(y.astype(_jnp.float32) - loss_target)
        return 0.5 * _jnp.sum(_jnp.mean(err, axis=-1)) if err.ndim else 0.5 * err


def _adamw(w, g, m, v):
    m = ADAM_B1 * m + (1.0 - ADAM_B1) * g
    v = ADAM_B2 * v + (1.0 - ADAM_B2) * _jnp.square(g)
    m_hat = m / (1.0 - ADAM_B1 ** ADAM_STEP)
    v_hat = v / (1.0 - ADAM_B2 ** ADAM_STEP)
    delta = -ADAM_LR * (m_hat / (_jnp.sqrt(v_hat) + ADAM_EPS) + ADAM_WD * w)
    return delta, m, v


def reference(x, positions, ffn1_norm, ffn1_w_gate, ffn1_w_up, ffn1_w_down, mix_norm, w_in, pool_w, pool_scale, dn_conv_w, dn_a_log, dn_dt_bias, dn_out_norm, w_out, ffn2_norm, ffn2_w_gate, ffn2_w_up, ffn2_w_down, final_norm, loss_target, m_ffn1_norm, m_ffn1_w_gate, m_ffn1_w_up, m_ffn1_w_down, m_mix_norm, m_w_in, m_pool_w, m_pool_scale, m_dn_conv_w, m_dn_a_log, m_dn_dt_bias, m_dn_out_norm, m_w_out, m_ffn2_norm, m_ffn2_w_gate, m_ffn2_w_up, m_ffn2_w_down, m_final_norm, v_ffn1_norm, v_ffn1_w_gate, v_ffn1_w_up, v_ffn1_w_down, v_mix_norm, v_w_in, v_pool_w, v_pool_scale, v_dn_conv_w, v_dn_a_log, v_dn_dt_bias, v_dn_out_norm, v_w_out, v_ffn2_norm, v_ffn2_w_gate, v_ffn2_w_up, v_ffn2_w_down, v_final_norm):
    given = dict(x=x, positions=positions, ffn1_norm=ffn1_norm, ffn1_w_gate=ffn1_w_gate, ffn1_w_up=ffn1_w_up, ffn1_w_down=ffn1_w_down, mix_norm=mix_norm, w_in=w_in, pool_w=pool_w, pool_scale=pool_scale, dn_conv_w=dn_conv_w, dn_a_log=dn_a_log, dn_dt_bias=dn_dt_bias, dn_out_norm=dn_out_norm, w_out=w_out, ffn2_norm=ffn2_norm, ffn2_w_gate=ffn2_w_gate, ffn2_w_up=ffn2_w_up, ffn2_w_down=ffn2_w_down, final_norm=final_norm, loss_target=loss_target, m_ffn1_norm=m_ffn1_norm, m_ffn1_w_gate=m_ffn1_w_gate, m_ffn1_w_up=m_ffn1_w_up, m_ffn1_w_down=m_ffn1_w_down, m_mix_norm=m_mix_norm, m_w_in=m_w_in, m_pool_w=m_pool_w, m_pool_scale=m_pool_scale, m_dn_conv_w=m_dn_conv_w, m_dn_a_log=m_dn_a_log, m_dn_dt_bias=m_dn_dt_bias, m_dn_out_norm=m_dn_out_norm, m_w_out=m_w_out, m_ffn2_norm=m_ffn2_norm, m_ffn2_w_gate=m_ffn2_w_gate, m_ffn2_w_up=m_ffn2_w_up, m_ffn2_w_down=m_ffn2_w_down, m_final_norm=m_final_norm, v_ffn1_norm=v_ffn1_norm, v_ffn1_w_gate=v_ffn1_w_gate, v_ffn1_w_up=v_ffn1_w_up, v_ffn1_w_down=v_ffn1_w_down, v_mix_norm=v_mix_norm, v_w_in=v_w_in, v_pool_w=v_pool_w, v_pool_scale=v_pool_scale, v_dn_conv_w=v_dn_conv_w, v_dn_a_log=v_dn_a_log, v_dn_dt_bias=v_dn_dt_bias, v_dn_out_norm=v_dn_out_norm, v_w_out=v_w_out, v_ffn2_norm=v_ffn2_norm, v_ffn2_w_gate=v_ffn2_w_gate, v_ffn2_w_up=v_ffn2_w_up, v_ffn2_w_down=v_ffn2_w_down, v_final_norm=v_final_norm)
    weights = {n: given[n] for n in TWIN_WEIGHTS}
    shared = {n: given[n] for n in SHARED_INPUTS}
    per_example = {n: given[n] for n in ['x', 'positions']}
    grad_fn = _jax.value_and_grad(_loss, argnums=(0, 1))

    def one_microbatch(ex, loss_target):
        ex = dict(ex)
        diff = ex.pop(TWIN_DIFF_INPUT)
        return grad_fn(weights, diff, {**shared, **ex}, loss_target)

    if N_MICROBATCH == 1:
        loss, (grad_w, grad_x) = one_microbatch(per_example, given["loss_target"])
    else:
        def body(carry, xs):
            loss_sum, grad_sum = carry
            l_k, (gw_k, gx_k) = one_microbatch(xs[0], xs[1])
            with _jax.named_scope("update"):
                return (loss_sum + l_k, _jax.tree.map(_jnp.add, grad_sum, gw_k)), gx_k

        init = (_jnp.zeros((), _jnp.float32), _jax.tree.map(_jnp.zeros_like, weights))
        (loss, grad_w), grad_x = _jax.lax.scan(body, init, (per_example, given["loss_target"]))
    with _jax.named_scope("update"):
        delta_w, new_m, new_v = {}, {}, {}
        for n in TWIN_WEIGHTS:
            delta_w[n], new_m[n], new_v[n] = _adamw(weights[n], grad_w[n], given["m_" + n], given["v_" + n])
    return (loss, grad_x, *[grad_w[n] for n in TWIN_WEIGHTS], *[delta_w[n] for n in TWIN_WEIGHTS],
            *[new_m[n] for n in TWIN_WEIGHTS], *[new_v[n] for n in TWIN_WEIGHTS])
```

```python
import functools
import math

import jax
import jax.numpy as jnp
import numpy as np
from jax import lax
from jax.experimental import pallas as pl
from jax.experimental.pallas import tpu as pltpu

f32 = jnp.float32
bf16 = jnp.bfloat16
SDS = jax.ShapeDtypeStruct
MESH = pl.DeviceIdType.MESH

S = 4096
D = 1024
DEPTH = 2
FF = 2816
NCH = 4
FC = FF // NCH
INW = 3080
INC = INW // NCH
INP = 3200
ATT = 256
EH = 64
NBLK = 128
DNW = 512
DH = 128
CH = 64
NCHUNK = S // CH
EPS = 1e-6
ROPE_THETA = 10000.0
PATTERN_DIL = (1, 4, 16)
ADAM_LR, ADAM_B1, ADAM_B2, ADAM_EPS, ADAM_WD, ADAM_STEP = 0.001, 0.9, 0.999, 1e-08, 0.01, 10
VMEM_BYTES_V7X = 64 * 1024 * 1024
NEG = -1e30

TM = 512
RA, RB, RC = 8 * D, 4 * FC + 2 * 256, 2 * D


def _cp(vmem_mb=48, sem=None):
    kw = dict(vmem_limit_bytes=vmem_mb * 1024 * 1024)
    if sem is not None:
        kw["dimension_semantics"] = sem
    return pltpu.CompilerParams(**kw)


def _dot(a, b):
    return jnp.dot(a, b, preferred_element_type=f32)


def _dot_nt(a, b):
    return lax.dot_general(a, b, (((1,), (1,)), ((), ())), preferred_element_type=f32)


def _dot_tn(a, b):
    return lax.dot_general(a, b, (((0,), (0,)), ((), ())), preferred_element_type=f32)


def _rms(x, w):
    r = lax.rsqrt(jnp.mean(x * x, axis=-1, keepdims=True) + EPS)
    return x * r * w, r


def _rms_bwd(x, w, r, dh):
    xhat = x * r
    dw = jnp.sum(dh * xhat, axis=0, keepdims=True)
    dxh = dh * w
    dx = r * (dxh - xhat * jnp.mean(dxh * xhat, axis=-1, keepdims=True))
    return dx, dw


def _ffn_fwd(x, nw, blob_a, blob_b, kg, ku, kd):
    def body(x_ref, nw_ref, wg_ref, wu_ref, wd_ref, o_ref, h_scr, acc_scr):
        j = pl.program_id(1)

        @pl.when(j == 0)
        def _():
            h, _ = _rms(x_ref[...], nw_ref[...])
            h_scr[...] = h.astype(bf16)
            acc_scr[...] = jnp.zeros_like(acc_scr)

        h = h_scr[...]
        g = _dot(h, wg_ref[0])
        u = _dot(h, wu_ref[0])
        a = (g * jax.nn.sigmoid(g) * u).astype(bf16)
        acc_scr[...] += _dot(a, wd_ref[0])

        @pl.when(j == NCH - 1)
        def _():
            o_ref[...] = x_ref[...] + 0.5 * acc_scr[...]

    return pl.pallas_call(
        body, grid=(S // TM, NCH), name="ffn_fwd",
        in_specs=[pl.BlockSpec((TM, D), lambda i, j: (i, 0)),
                  pl.BlockSpec((1, D), lambda i, j: (0, 0)),
                  pl.BlockSpec((1, D, FC), lambda i, j: (j, kg, 0)),
                  pl.BlockSpec((1, D, FC), lambda i, j: (j, ku, 0)),
                  pl.BlockSpec((1, FC, D), lambda i, j: (j, kd, 0))],
        out_specs=pl.BlockSpec((TM, D), lambda i, j: (i, 0)),
        out_shape=SDS((S, D), f32),
        scratch_shapes=[pltpu.VMEM((TM, D), bf16), pltpu.VMEM((TM, D), f32)],
        compiler_params=_cp(40),
    )(x, nw, blob_a, blob_a, blob_b)


def _ffn_bwd(x, nw, blob_a, blob_b, kg, ku, kd, dy):
    nt = S // TM

    def body(x_ref, nw_ref, wg_ref, wu_ref, wd_ref, dy_ref, dwg_ref, dwu_ref, dwd_ref, dh_ref, ag, au, ad):
        i = pl.program_id(1)

        @pl.when(i == 0)
        def _():
            ag[...] = jnp.zeros_like(ag)
            au[...] = jnp.zeros_like(au)
            ad[...] = jnp.zeros_like(ad)

        hf, _ = _rms(x_ref[...], nw_ref[...])
        h = hf.astype(bf16)
        g = _dot(h, wg_ref[0])
        u = _dot(h, wu_ref[0])
        sg = jax.nn.sigmoid(g)
        s = g * sg
        a = (s * u).astype(bf16)
        dyb = (0.5 * dy_ref[...]).astype(bf16)
        da = _dot_nt(dyb, wd_ref[0])
        ad[...] += _dot_tn(a, dyb)
        du = (da * s).astype(bf16)
        dg = (da * u * (sg * (1.0 + g * (1.0 - sg)))).astype(bf16)
        ag[...] += _dot_tn(h, dg)
        au[...] += _dot_tn(h, du)
        dh_ref[0] = (_dot_nt(dg, wg_ref[0]) + _dot_nt(du, wu_ref[0])).astype(bf16)

        @pl.when(i == nt - 1)
        def _():
            dwg_ref[0] = ag[...].astype(bf16)
            dwu_ref[0] = au[...].astype(bf16)
            dwd_ref[0] = ad[...].astype(bf16)

    return pl.pallas_call(
        body, grid=(NCH, nt), name="ffn_bwd",
        in_specs=[pl.BlockSpec((TM, D), lambda j, i: (i, 0)),
                  pl.BlockSpec((1, D), lambda j, i: (0, 0)),
                  pl.BlockSpec((1, D, FC), lambda j, i: (j, kg, 0)),
                  pl.BlockSpec((1, D, FC), lambda j, i: (j, ku, 0)),
                  pl.BlockSpec((1, FC, D), lambda j, i: (j, kd, 0)),
                  pl.BlockSpec((TM, D), lambda j, i: (i, 0))],
        out_specs=[pl.BlockSpec((1, D, FC), lambda j, i: (j, 0, 0)),
                   pl.BlockSpec((1, D, FC), lambda j, i: (j, 0, 0)),
                   pl.BlockSpec((1, FC, D), lambda j, i: (j, 0, 0)),
                   pl.BlockSpec((1, TM, D), lambda j, i: (j, i, 0))],
        out_shape=[SDS((NCH, D, FC), bf16), SDS((NCH, D, FC), bf16), SDS((NCH, FC, D), bf16), SDS((NCH, S, D), bf16)],
        scratch_shapes=[pltpu.VMEM((D, FC), f32), pltpu.VMEM((D, FC), f32), pltpu.VMEM((FC, D), f32)],
        compiler_params=_cp(56),
    )(x, nw, blob_a, blob_a, blob_b, dy)


def _norm_bwd(x, nw, dres, dh4):
    nt = S // TM
    nparts = dh4.shape[0]

    def body(x_ref, nw_ref, dres_ref, dh_ref, dx_ref, dnw_ref):
        i = pl.program_id(0)
        dh = dh_ref[0].astype(f32)
        for p in range(1, nparts):
            dh = dh + dh_ref[p].astype(f32)
        xv = x_ref[...]
        _, r = _rms(xv, nw_ref[...])
        dx, dw = _rms_bwd(xv, nw_ref[...], r, dh)
        dx_ref[...] = dres_ref[...] + dx

        @pl.when(i == 0)
        def _():
            dnw_ref[...] = jnp.zeros_like(dnw_ref)

        dnw_ref[...] += dw

    return pl.pallas_call(
        body, grid=(nt,), name="norm_bwd",
        in_specs=[pl.BlockSpec((TM, D), lambda i: (i, 0)),
                  pl.BlockSpec((1, D), lambda i: (0, 0)),
                  pl.BlockSpec((TM, D), lambda i: (i, 0)),
                  pl.BlockSpec((nparts, TM, D), lambda i: (0, i, 0))],
        out_specs=[pl.BlockSpec((TM, D), lambda i: (i, 0)), pl.BlockSpec((1, D), lambda i: (0, 0))],
        out_shape=[SDS((S, D), f32), SDS((1, D), f32)],
        compiler_params=_cp(40),
    )(x, nw, dres, dh4)


def _final(x, nw, target):
    nt = S // TM

    def body(x_ref, nw_ref, t_ref, dx_ref, dnw_ref, loss_ref):
        i = pl.program_id(0)
        xv = x_ref[...]
        y, r = _rms(xv, nw_ref[...])
        err = y - t_ref[...]
        part = 0.5 * jnp.sum(jnp.mean(err * err, axis=-1, keepdims=True), axis=0, keepdims=True)
        dx, dw = _rms_bwd(xv, nw_ref[...], r, err * (1.0 / D))
        dx_ref[...] = dx

        @pl.when(i == 0)
        def _():
            dnw_ref[...] = jnp.zeros_like(dnw_ref)
            loss_ref[...] = jnp.zeros_like(loss_ref)

        dnw_ref[...] += dw
        loss_ref[...] += jnp.broadcast_to(part, loss_ref.shape)

    return pl.pallas_call(
        body, grid=(nt,), name="final_loss",
        in_specs=[pl.BlockSpec((TM, D), lambda i: (i, 0)),
                  pl.BlockSpec((1, D), lambda i: (0, 0)),
                  pl.BlockSpec((TM, D), lambda i: (i, 0))],
        out_specs=[pl.BlockSpec((TM, D), lambda i: (i, 0)), pl.BlockSpec((1, D), lambda i: (0, 0)),
                   pl.BlockSpec((1, 128), lambda i: (0, 0))],
        out_shape=[SDS((S, D), f32), SDS((1, D), f32), SDS((1, 128), f32)],
        compiler_params=_cp(40),
    )(x, nw, target)


def _rot_half(t):
    lane = lax.broadcasted_iota(jnp.int32, t.shape, 1)
    first = (lane % EH) < (EH // 2)
    return jnp.where(first, -pltpu.roll(t, ATT - EH // 2, 1), pltpu.roll(t, EH // 2, 1))


def _rope_tables(pos_ref, freq_ref):
    ang = pos_ref[...].astype(f32) * freq_ref[...]
    return jnp.cos(ang), jnp.sin(ang)


def _inproj_fwd(x, nw, w_aug, pos, freq):
    TI = 256

    def body(x_ref, nw_ref, w_hbm, pos_ref, freq_ref, att_ref, pu_ref, dq_ref, dz_ref, dba_ref, w_scr):
        @pl.when(pl.program_id(0) == 0)
        def _():
            pltpu.sync_copy(w_hbm, w_scr)

        h, _ = _rms(x_ref[...], nw_ref[...])
        proj = _dot(h.astype(bf16), w_scr[...])
        cos, sin = _rope_tables(pos_ref, freq_ref)
        q = proj[:, 0:ATT]
        k = proj[:, ATT:2 * ATT]
        att_ref[:, 0:ATT] = q * cos + _rot_half(q) * sin
        att_ref[:, ATT:2 * ATT] = k * cos + _rot_half(k) * sin
        att_ref[:, 2 * ATT:3 * ATT] = proj[:, 2 * ATT:3 * ATT]
        pu_ref[...] = proj[:, 768:1024]
        dq_ref[...] = proj[:, 1024:2560]
        dz_ref[...] = proj[:, 2560:3072]
        dba_ref[...] = proj[:, 3072:3200]

    return pl.pallas_call(
        body, grid=(S // TI,), name="inproj_fwd",
        in_specs=[pl.BlockSpec((TI, D), lambda i: (i, 0)),
                  pl.BlockSpec((1, D), lambda i: (0, 0)),
                  pl.BlockSpec(memory_space=pl.ANY),
                  pl.BlockSpec((TI, 1), lambda i: (i, 0)),
                  pl.BlockSpec((1, ATT), lambda i: (0, 0))],
        out_specs=[pl.BlockSpec((TI, 768), lambda i: (i, 0)), pl.BlockSpec((TI, 256), lambda i: (i, 0)),
                   pl.BlockSpec((TI, 1536), lambda i: (i, 0)), pl.BlockSpec((TI, 512), lambda i: (i, 0)),
                   pl.BlockSpec((TI, 128), lambda i: (i, 0))],
        out_shape=[SDS((S, 768), f32), SDS((S, 256), f32), SDS((S, 1536), f32), SDS((S, 512), f32), SDS((S, 128), f32)],
        scratch_shapes=[pltpu.VMEM((D, INP), bf16)],
        compiler_params=_cp(48),
    )(x, nw, w_aug, pos, freq)


def _inproj_bwd(x, nw, w_aug, pos, freq, dres, datt3, dpu, ddq, ddz, ddba):
    TI = 256
    nt = S // TI

    def body(x_ref, nw_ref, w_hbm, pos_ref, freq_ref, dres_ref, datt_ref, dpu_ref, ddq_ref, ddz_ref, ddba_ref,
             dx_ref, dnw_ref, dw_hbm, w_scr, acc):
        i = pl.program_id(0)

        @pl.when(i == 0)
        def _():
            pltpu.sync_copy(w_hbm, w_scr)
            acc[...] = jnp.zeros_like(acc)
            dnw_ref[...] = jnp.zeros_like(dnw_ref)

        xv = x_ref[...]
        hf, r = _rms(xv, nw_ref[...])
        h = hf.astype(bf16)
        cos, sin = _rope_tables(pos_ref, freq_ref)
        datt = datt_ref[0] + datt_ref[1] + datt_ref[2]
        dq = datt[:, 0:ATT]
        dk = datt[:, ATT:2 * ATT]
        dq = dq * cos - _rot_half(dq) * sin
        dk = dk * cos - _rot_half(dk) * sin
        dproj = jnp.concatenate([dq, dk, datt[:, 2 * ATT:3 * ATT], dpu_ref[...], ddq_ref[...], ddz_ref[...], ddba_ref[...]],
                                axis=1).astype(bf16)
        acc[...] += _dot_tn(h, dproj)
        dh = _dot_nt(dproj, w_scr[...])
        dx, dw = _rms_bwd(xv, nw_ref[...], r, dh)
        dx_ref[...] = dres_ref[...] + dx
        dnw_ref[...] += dw

        @pl.when(i == nt - 1)
        def _():
            pltpu.sync_copy(acc, dw_hbm)

    return pl.pallas_call(
        body, grid=(nt,), name="inproj_bwd",
        in_specs=[pl.BlockSpec((TI, D), lambda i: (i, 0)),
                  pl.BlockSpec((1, D), lambda i: (0, 0)),
                  pl.BlockSpec(memory_space=pl.ANY),
                  pl.BlockSpec((TI, 1), lambda i: (i, 0)),
                  pl.BlockSpec((1, ATT), lambda i: (0, 0)),
                  pl.BlockSpec((TI, D), lambda i: (i, 0)),
                  pl.BlockSpec((3, TI, 768), lambda i: (0, i, 0)),
                  pl.BlockSpec((TI, 256), lambda i: (i, 0)),
                  pl.BlockSpec((TI, 1536), lambda i: (i, 0)),
                  pl.BlockSpec((TI, 512), lambda i: (i, 0)),
                  pl.BlockSpec((TI, 128), lambda i: (i, 0))],
        out_specs=[pl.BlockSpec((TI, D), lambda i: (i, 0)), pl.BlockSpec((1, D), lambda i: (0, 0)),
                   pl.BlockSpec(memory_space=pl.ANY)],
        out_shape=[SDS((S, D), f32), SDS((1, D), f32), SDS((D, INP), f32)],
        scratch_shapes=[pltpu.VMEM((D, INP), bf16), pltpu.VMEM((D, INP), f32)],
        compiler_params=_cp(56),
    )(x, nw, w_aug, pos, freq, dres, datt3, dpu, ddq, ddz, ddba)


def _outproj_fwd(x, ya, yb, yc, blob_b, kw):
    def body(x_ref, ya_ref, yb_ref, yc_ref, w_ref, o_ref):
        ycat = jnp.concatenate([ya_ref[...], yb_ref[...], yc_ref[...]], axis=1).astype(bf16)
        o_ref[...] = x_ref[...] + _dot(ycat, w_ref[...].reshape(D, D))

    return pl.pallas_call(
        body, grid=(S // TM,), name="outproj_fwd",
        in_specs=[pl.BlockSpec((TM, D), lambda i: (i, 0)),
                  pl.BlockSpec((TM, 256), lambda i: (i, 0)),
                  pl.BlockSpec((TM, 256), lambda i: (i, 0)),
                  pl.BlockSpec((TM, 512), lambda i: (i, 0)),
                  pl.BlockSpec((NCH, 256, D), lambda i: (0, kw, 0))],
        out_specs=pl.BlockSpec((TM, D), lambda i: (i, 0)),
        out_shape=SDS((S, D), f32),
        compiler_params=_cp(40),
    )(x, ya, yb, yc, blob_b)


def _outproj_bwd(dy, ya, yb, yc, blob_b, kw):
    nt = S // TM

    def body(dy_ref, ya_ref, yb_ref, yc_ref, w_ref, dya_ref, dyb_ref, dyc_ref, dw_ref):
        i = pl.program_id(0)

        @pl.when(i == 0)
        def _():
            dw_ref[...] = jnp.zeros_like(dw_ref)

        dyv = dy_ref[...].astype(bf16)
        ycat = jnp.concatenate([ya_ref[...], yb_ref[...], yc_ref[...]], axis=1).astype(bf16)
        dw_ref[...] += _dot_tn(ycat, dyv)
        dcat = _dot_nt(dyv, w_ref[...].reshape(D, D))
        dya_ref[...] = dcat[:, 0:256]
        dyb_ref[...] = dcat[:, 256:512]
        dyc_ref[...] = dcat[:, 512:1024]

    return pl.pallas_call(
        body, grid=(nt,), name="outproj_bwd",
        in_specs=[pl.BlockSpec((TM, D), lambda i: (i, 0)),
                  pl.BlockSpec((TM, 256), lambda i: (i, 0)),
                  pl.BlockSpec((TM, 256), lambda i: (i, 0)),
                  pl.BlockSpec((TM, 512), lambda i: (i, 0)),
                  pl.BlockSpec((NCH, 256, D), lambda i: (0, kw, 0))],
        out_specs=[pl.BlockSpec((TM, 256), lambda i: (i, 0)), pl.BlockSpec((TM, 256), lambda i: (i, 0)),
                   pl.BlockSpec((TM, 512), lambda i: (i, 0)), pl.BlockSpec((D, D), lambda i: (0, 0))],
        out_shape=[SDS((S, 256), f32), SDS((S, 256), f32), SDS((S, 512), f32), SDS((D, D), f32)],
        compiler_params=_cp(40),
    )(dy, ya, yb, yc, blob_b)


NB3 = 3 * S // NBLK


def _first_of_sequence(b):
    return jnp.where(b < 32, b == 0, jnp.where(b < 64, b % 8 == 0, b % 2 == 0))


def _attn_block(q, kp, kc, vp, vc, first):
    kk = jnp.concatenate([kp, kc], axis=0).astype(bf16)
    vv = jnp.concatenate([vp, vc], axis=0).astype(bf16)
    qi = lax.broadcasted_iota(jnp.int32, (NBLK, 2 * NBLK), 0)
    ki = lax.broadcasted_iota(jnp.int32, (NBLK, 2 * NBLK), 1)
    dist = NBLK + qi - ki
    valid = (dist >= 0) & (dist <= NBLK) & (jnp.logical_not(first) | (ki >= NBLK))
    head = lax.broadcasted_iota(jnp.int32, (1, ATT), 1) // EH
    o = jnp.zeros((NBLK, ATT), f32)
    l = jnp.zeros((NBLK, ATT), f32)
    for h in range(4):
        mh = (head == h).astype(f32)
        s = _dot_nt((q * mh).astype(bf16), kk) * (1.0 / math.sqrt(EH))
        s = jnp.where(valid, s, NEG)
        m = lax.stop_gradient(jnp.max(s, axis=-1, keepdims=True))
        p = jnp.exp(s - m)
        den = jnp.sum(p, axis=-1, keepdims=True)
        o = o + _dot((p / den).astype(bf16), vv) * mh
        l = l + (m + jnp.log(den)) * mh
    return o, l


def _attn_specs():
    prev = lambda b: jnp.maximum(b - 1, 0)
    return [pl.BlockSpec((NBLK, ATT), lambda b: (b, 0)),
            pl.BlockSpec((NBLK, ATT), lambda b: (prev(b), 1)),
            pl.BlockSpec((NBLK, ATT), lambda b: (b, 1)),
            pl.BlockSpec((NBLK, ATT), lambda b: (prev(b), 2)),
            pl.BlockSpec((NBLK, ATT), lambda b: (b, 2))]


def _attn_fwd(qkv3):
    def body(q_ref, kp_ref, kc_ref, vp_ref, vc_ref, o_ref, l_ref):
        first = _first_of_sequence(pl.program_id(0))
        o, l = _attn_block(q_ref[...], kp_ref[...], kc_ref[...], vp_ref[...], vc_ref[...], first)
        o_ref[...] = o
        l_ref[...] = l

    blk = pl.BlockSpec((NBLK, ATT), lambda b: (b, 0))
    return pl.pallas_call(
        body, grid=(NB3,), name="attn_fwd", in_specs=_attn_specs(), out_specs=[blk, blk],
        out_shape=[SDS((3 * S, ATT), f32), SDS((3 * S, ATT), f32)], compiler_params=_cp(32),
    )(qkv3, qkv3, qkv3, qkv3, qkv3)


def _attn_bwd(qkv3, do3, dl3):
    def body(q_ref, kp_ref, kc_ref, vp_ref, vc_ref, do_ref, dl_ref, dq_ref, dkp_ref, dkc_ref, dvp_ref, dvc_ref):
        first = _first_of_sequence(pl.program_id(0))
        fn = lambda q, kp, kc, vp, vc: _attn_block(q, kp, kc, vp, vc, first)
        _, vjp = jax.vjp(fn, q_ref[...], kp_ref[...], kc_ref[...], vp_ref[...], vc_ref[...])
        dq, dkp, dkc, dvp, dvc = vjp((do_ref[...], dl_ref[...]))
        dq_ref[...] = dq
        dkp_ref[...] = dkp
        dkc_ref[...] = dkc
        dvp_ref[...] = dvp
        dvc_ref[...] = dvc

    blk = pl.BlockSpec((NBLK, ATT), lambda b: (b, 0))
    return pl.pallas_call(
        body, grid=(NB3,), name="attn_bwd", in_specs=_attn_specs() + [blk, blk], out_specs=[blk] * 5,
        out_shape=[SDS((3 * S, ATT), f32)] * 5, compiler_params=_cp(32),
    )(qkv3, qkv3, qkv3, qkv3, qkv3, do3, dl3)


def _attn_fold(dq, dkp, dkc, dvp, dvc):
    def body(dq_ref, kc_ref, kn_ref, vc_ref, vn_ref, o_ref):
        b = pl.program_id(0)
        take = jnp.logical_and(b < NB3 - 1, jnp.logical_not(_first_of_sequence(b + 1))).astype(f32)
        o_ref[:, 0:ATT] = dq_ref[...]
        o_ref[:, ATT:2 * ATT] = kc_ref[...] + take * kn_ref[...]
        o_ref[:, 2 * ATT:3 * ATT] = vc_ref[...] + take * vn_ref[...]

    cur = pl.BlockSpec((NBLK, ATT), lambda b: (b, 0))
    nxt = pl.BlockSpec((NBLK, ATT), lambda b: (jnp.minimum(b + 1, NB3 - 1), 0))
    return pl.pallas_call(
        body, grid=(NB3,), name="attn_fold", in_specs=[cur, cur, nxt, cur, nxt],
        out_specs=pl.BlockSpec((NBLK, 768), lambda b: (b, 0)), out_shape=SDS((3 * S, 768), f32), compiler_params=_cp(32),
    )(dq, dkc, dkp, dvc, dvp)


def _merge_weights(l_ref):
    l0, l1, l2 = l_ref[0], l_ref[1], l_ref[2]
    m = jnp.maximum(jnp.maximum(l0, l1), l2)
    e0, e1, e2 = jnp.exp(l0 - m), jnp.exp(l1 - m), jnp.exp(l2 - m)
    tot = e0 + e1 + e2
    return e0 / tot, e1 / tot, e2 / tot


def _merge_fwd(o3, l3):
    def body(o_ref, l_ref, y_ref):
        w0, w1, w2 = _merge_weights(l_ref)
        y_ref[...] = w0 * o_ref[0] + w1 * o_ref[1] + w2 * o_ref[2]

    b3 = pl.BlockSpec((3, TM, ATT), lambda i: (0, i, 0))
    return pl.pallas_call(body, grid=(S // TM,), name="merge_fwd", in_specs=[b3, b3],
                          out_specs=pl.BlockSpec((TM, ATT), lambda i: (i, 0)), out_shape=SDS((S, ATT), f32),
                          compiler_params=_cp(32))(o3, l3)


def _merge_bwd(o3, l3, dy):
    def body(o_ref, l_ref, dy_ref, do_ref, dl_ref):
        ws = _merge_weights(l_ref)
        y = ws[0] * o_ref[0] + ws[1] * o_ref[1] + ws[2] * o_ref[2]
        dyv = dy_ref[...]
        for p in range(3):
            do_ref[p] = ws[p] * dyv
            dl_ref[p] = ws[p] * (o_ref[p] - y) * dyv

    b3 = pl.BlockSpec((3, TM, ATT), lambda i: (0, i, 0))
    return pl.pallas_call(body, grid=(S // TM,), name="merge_bwd",
                          in_specs=[b3, b3, pl.BlockSpec((TM, ATT), lambda i: (i, 0))], out_specs=[b3, b3],
                          out_shape=[SDS((3, S, ATT), f32), SDS((3, S, ATT), f32)], compiler_params=_cp(32))(o3, l3, dy)


HALO = 16


def _pool_consts(i, rows):
    grp = lax.broadcasted_iota(jnp.int32, (rows, 256), 1) // 64
    t = i * TM + lax.broadcasted_iota(jnp.int32, (rows, 256), 0)
    win = jnp.where(grp == 0, 2, jnp.where(grp == 1, 4, jnp.where(grp == 2, 8, 16)))
    cnt = jnp.minimum(t + 1, win).astype(f32)
    return grp, cnt


def _pool_select(grp, s2, s4, s8, s16):
    return jnp.where(grp == 0, s2, jnp.where(grp == 1, s4, jnp.where(grp == 2, s8, s16)))


def _pooled(i, cur, halo):
    xx = jnp.concatenate([halo, cur], axis=0)
    s2 = xx + pltpu.roll(xx, 1, 0)
    s4 = s2 + pltpu.roll(s2, 2, 0)
    s8 = s4 + pltpu.roll(s4, 4, 0)
    s16 = s8 + pltpu.roll(s8, 8, 0)
    grp, cnt = _pool_consts(i, TM)
    tot = _pool_select(grp, s2[HALO:], s4[HALO:], s8[HALO:], s16[HALO:])
    return tot / cnt - cur


def _pool_fwd(u, wp, scale):
    def body(u_ref, halo_ref, wp_ref, sc_ref, y_ref):
        i = pl.program_id(0)
        halo = halo_ref[...] * (i > 0).astype(f32)
        pooled = _pooled(i, u_ref[...], halo)
        y_ref[...] = _dot(pooled.astype(bf16), wp_ref[...]) * sc_ref[...]

    return pl.pallas_call(
        body, grid=(S // TM,), name="pool_fwd",
        in_specs=[pl.BlockSpec((TM, 256), lambda i: (i, 0)),
                  pl.BlockSpec((HALO, 256), lambda i: (jnp.maximum(i * (TM // HALO) - 1, 0), 0)),
                  pl.BlockSpec((256, 256), lambda i: (0, 0)),
                  pl.BlockSpec((1, 256), lambda i: (0, 0))],
        out_specs=pl.BlockSpec((TM, 256), lambda i: (i, 0)), out_shape=SDS((S, 256), f32), compiler_params=_cp(32),
    )(u, u, wp, scale)


def _pool_bwd(u, wp, scale, dy):
    nt = S // TM

    def body(u_ref, halo_ref, wp_ref, sc_ref, dy_ref, dyn_ref, du_ref, dwp_ref, dsc_ref):
        i = pl.program_id(0)

        @pl.when(i == 0)
        def _():
            dwp_ref[...] = jnp.zeros_like(dwp_ref)
            dsc_ref[...] = jnp.zeros_like(dsc_ref)

        halo = halo_ref[...] * (i > 0).astype(f32)
        pooled = _pooled(i, u_ref[...], halo).astype(bf16)
        dyv = dy_ref[...]
        dsc_ref[...] += jnp.sum(dyv * _dot(pooled, wp_ref[...]), axis=0, keepdims=True)
        dys = (dyv * sc_ref[...]).astype(bf16)
        dwp_ref[...] += _dot_tn(pooled, dys)
        dpool = _dot_nt(dys, wp_ref[...])
        grp, cnt = _pool_consts(i, TM)
        dyn = ((dyn_ref[...] * (i < nt - 1).astype(f32)) * sc_ref[...]).astype(bf16)
        _, cntn = _pool_consts(i + 1, HALO)
        zn = _dot_nt(dyn, wp_ref[...]) / cntn
        zz = jnp.concatenate([dpool / cnt, zn], axis=0)
        n = TM + HALO
        a2 = zz + pltpu.roll(zz, n - 1, 0)
        a4 = a2 + pltpu.roll(a2, n - 2, 0)
        a8 = a4 + pltpu.roll(a4, n - 4, 0)
        a16 = a8 + pltpu.roll(a8, n - 8, 0)
        du_ref[...] = _pool_select(grp, a2[:TM], a4[:TM], a8[:TM], a16[:TM]) - dpool

    return pl.pallas_call(
        body, grid=(nt,), name="pool_bwd",
        in_specs=[pl.BlockSpec((TM, 256), lambda i: (i, 0)),
                  pl.BlockSpec((HALO, 256), lambda i: (jnp.maximum(i * (TM // HALO) - 1, 0), 0)),
                  pl.BlockSpec((256, 256), lambda i: (0, 0)),
                  pl.BlockSpec((1, 256), lambda i: (0, 0)),
                  pl.BlockSpec((TM, 256), lambda i: (i, 0)),
                  pl.BlockSpec((HALO, 256), lambda i: (jnp.minimum((i + 1) * (TM // HALO), S // HALO - 1), 0))],
        out_specs=[pl.BlockSpec((TM, 256), lambda i: (i, 0)), pl.BlockSpec((256, 256), lambda i: (0, 0)),
                   pl.BlockSpec((1, 256), lambda i: (0, 0))],
        out_shape=[SDS((S, 256), f32), SDS((256, 256), f32), SDS((1, 256), f32)], compiler_params=_cp(32),
    )(u, u, wp, scale, dy, dy)


CW = 3 * DNW
CHALO = 8
TC = 256


def _conv_fwd(u, w):
    def body(u_ref, halo_ref, w_ref, c_ref):
        i = pl.program_id(0)
        xx = jnp.concatenate([halo_ref[...] * (i > 0).astype(f32), u_ref[...]], axis=0)
        c = (w_ref[3:4, :] * xx + w_ref[2:3, :] * pltpu.roll(xx, 1, 0) + w_ref[1:2, :] * pltpu.roll(xx, 2, 0)
             + w_ref[0:1, :] * pltpu.roll(xx, 3, 0))
        c_ref[...] = c[CHALO:]

    return pl.pallas_call(
        body, grid=(S // TC,), name="conv_fwd",
        in_specs=[pl.BlockSpec((TC, CW), lambda i: (i, 0)),
                  pl.BlockSpec((CHALO, CW), lambda i: (jnp.maximum(i * (TC // CHALO) - 1, 0), 0)),
                  pl.BlockSpec((8, CW), lambda i: (0, 0))],
        out_specs=pl.BlockSpec((TC, CW), lambda i: (i, 0)), out_shape=SDS((S, CW), f32), compiler_params=_cp(32),
    )(u, u, w)


def _conv_bwd(u, w, dc):
    nt = S // TC

    def body(u_ref, halo_ref, w_ref, dc_ref, dcn_ref, du_ref, dw_ref):
        i = pl.program_id(0)

        @pl.when(i == 0)
        def _():
            dw_ref[...] = jnp.zeros_like(dw_ref)

        dcv = dc_ref[...]
        zz = jnp.concatenate([dcv, dcn_ref[...] * (i < nt - 1).astype(f32)], axis=0)
        n = TC + CHALO
        du = (w_ref[3:4, :] * zz + w_ref[2:3, :] * pltpu.roll(zz, n - 1, 0) + w_ref[1:2, :] * pltpu.roll(zz, n - 2, 0)
              + w_ref[0:1, :] * pltpu.roll(zz, n - 3, 0))
        du_ref[...] = du[:TC]
        xx = jnp.concatenate([halo_ref[...] * (i > 0).astype(f32), u_ref[...]], axis=0)
        for j in range(4):
            shifted = xx if j == 3 else pltpu.roll(xx, 3 - j, 0)
            dw_ref[j:j + 1, :] += jnp.sum(dcv * shifted[CHALO:], axis=0, keepdims=True)

    return pl.pallas_call(
        body, grid=(nt,), name="conv_bwd",
        in_specs=[pl.BlockSpec((TC, CW), lambda i: (i, 0)),
                  pl.BlockSpec((CHALO, CW), lambda i: (jnp.maximum(i * (TC // CHALO) - 1, 0), 0)),
                  pl.BlockSpec((8, CW), lambda i: (0, 0)),
                  pl.BlockSpec((TC, CW), lambda i: (i, 0)),
                  pl.BlockSpec((CHALO, CW), lambda i: (jnp.minimum((i + 1) * (TC // CHALO), S // CHALO - 1), 0))],
        out_specs=[pl.BlockSpec((TC, CW), lambda i: (i, 0)), pl.BlockSpec((8, CW), lambda i: (0, 0))],
        out_shape=[SDS((S, CW), f32), SDS((8, CW), f32)], compiler_params=_cp(32),
    )(u, u, w, dc, dc)


TL = 256
NCL = TL // CH


def _bdot(a, b):
    return jnp.einsum('nik,nkj->nij', a.astype(bf16), b.astype(bf16), preferred_element_type=f32)


def _bdot_nt(a, b):
    return jnp.einsum('nik,njk->nij', a.astype(bf16), b.astype(bf16), preferred_element_type=f32)


def _dn_local(c, dba, a_row, b_row):
    act = c * jax.nn.sigmoid(c)
    lane = lax.broadcasted_iota(jnp.int32, (TL, 128), 1)
    beta_all = jax.nn.sigmoid(dba)
    xs = dba + b_row
    softplus = jnp.maximum(xs, 0.0) + jnp.log(1.0 + jnp.exp(-jnp.abs(xs)))
    g_all = -jnp.exp(a_row) * softplus
    ii = lax.broadcasted_iota(jnp.int32, (1, CH, CH), 1)
    jj = lax.broadcasted_iota(jnp.int32, (1, CH, CH), 2)
    lower = jj <= ii
    strict = jj < ii
    eye = (ii == jj).astype(f32)
    us, ws, qgs, kds, intras = [], [], [], [], []
    aux = jnp.zeros((TL, 128), f32)
    for h in range(4):
        q = act[:, DH * h:DH * (h + 1)]
        k = act[:, DNW + DH * h:DNW + DH * (h + 1)]
        v = act[:, 2 * DNW + DH * h:2 * DNW + DH * (h + 1)]
        q = q * lax.rsqrt(jnp.sum(q * q, axis=-1, keepdims=True) + EPS) * (DH ** -0.5)
        k = k * lax.rsqrt(jnp.sum(k * k, axis=-1, keepdims=True) + EPS)
        beta = jnp.sum(jnp.where(lane == h, beta_all, 0.0), axis=1, keepdims=True)
        g = jnp.sum(jnp.where(lane == 4 + h, g_all, 0.0), axis=1, keepdims=True)
        q3, k3, v3 = q.reshape(NCL, CH, DH), k.reshape(NCL, CH, DH), v.reshape(NCL, CH, DH)
        beta3, g3 = beta.reshape(NCL, CH, 1), g.reshape(NCL, CH, 1)
        g_row = jnp.sum(eye * g3, axis=1, keepdims=True)
        gc_col = jnp.sum(jnp.where(lower, g_row, 0.0), axis=2, keepdims=True)
        gc_row = jnp.sum(jnp.where(ii <= jj, g3, 0.0), axis=1, keepdims=True)
        diff = gc_col - gc_row
        decay = jnp.where(lower, jnp.exp(jnp.where(lower, diff, 0.0)), 0.0)
        kb = k3 * beta3
        vb = v3 * beta3
        a = jnp.where(strict, _bdot_nt(kb, k3) * decay, 0.0)
        t = eye - a
        p = a
        for _ in range(5):
            p = _bdot(p, p)
            t = t + _bdot(t, p)
        u3 = _bdot(t, vb)
        w3 = _bdot(t, kb * jnp.exp(gc_col))
        intra = jnp.where(lower, _bdot_nt(q3, k3) * decay, 0.0)
        g_last = jnp.sum(g3, axis=1, keepdims=True)
        us.append(u3.reshape(TL, DH))
        ws.append(w3.reshape(TL, DH))
        qgs.append((q3 * jnp.exp(gc_col)).reshape(TL, DH))
        kds.append((k3 * jnp.exp(g_last - gc_col)).reshape(TL, DH))
        intras.append(intra.reshape(TL, CH))
        e_last = jnp.broadcast_to(jnp.exp(g_last), (NCL, CH, 1)).reshape(TL, 1)
        aux = aux + jnp.where(lane == h, e_last, 0.0)
    cat = lambda xs: jnp.concatenate(xs, axis=1)
    return cat(us), cat(ws), cat(qgs), cat(kds), jnp.stack(intras, axis=0), aux


def _dn_local_fwd(c, dba, par):
    def body(c_ref, dba_ref, par_ref, u_ref, w_ref, qg_ref, kd_ref, in_ref, aux_ref):
        u, w, qg, kd, intra, aux = _dn_local(c_ref[...], dba_ref[...], par_ref[0:1, :], par_ref[1:2, :])
        u_ref[...] = u
        w_ref[...] = w
        qg_ref[...] = qg
        kd_ref[...] = kd
        in_ref[...] = intra
        aux_ref[...] = aux

    wide = pl.BlockSpec((TL, DNW), lambda i: (i, 0))
    return pl.pallas_call(
        body, grid=(S // TL,), name="dn_local_fwd",
        in_specs=[pl.BlockSpec((TL, CW), lambda i: (i, 0)), pl.BlockSpec((TL, 128), lambda i: (i, 0)),
                  pl.BlockSpec((8, 128), lambda i: (0, 0))],
        out_specs=[wide, wide, wide, wide, pl.BlockSpec((4, TL, CH), lambda i: (0, i, 0)),
                   pl.BlockSpec((TL, 128), lambda i: (i, 0))],
        out_shape=[SDS((S, DNW), f32)] * 4 + [SDS((4, S, CH), f32), SDS((S, 128), f32)], compiler_params=_cp(48),
    )(c, dba, par)


def _dn_local_bwd(c, dba, par, du, dw, dqg, dkd, dintra, daux):
    def body(c_ref, dba_ref, par_ref, du_ref, dw_ref, dqg_ref, dkd_ref, din_ref, daux_ref, dc_ref, ddba_ref, dpar_ref):
        @pl.when(pl.program_id(0) == 0)
        def _():
            dpar_ref[...] = jnp.zeros_like(dpar_ref)

        _, vjp = jax.vjp(_dn_local, c_ref[...], dba_ref[...], par_ref[0:1, :], par_ref[1:2, :])
        dc, ddba, da_row, db_row = vjp((du_ref[...], dw_ref[...], dqg_ref[...], dkd_ref[...], din_ref[...], daux_ref[...]))
        dc_ref[...] = dc
        ddba_ref[...] = ddba
        dpar_ref[0:1, :] += da_row
        dpar_ref[1:2, :] += db_row

    wide = pl.BlockSpec((TL, DNW), lambda i: (i, 0))
    return pl.pallas_call(
        body, grid=(S // TL,), name="dn_local_bwd",
        in_specs=[pl.BlockSpec((TL, CW), lambda i: (i, 0)), pl.BlockSpec((TL, 128), lambda i: (i, 0)),
                  pl.BlockSpec((8, 128), lambda i: (0, 0)), wide, wide, wide, wide,
                  pl.BlockSpec((4, TL, CH), lambda i: (0, i, 0)), pl.BlockSpec((TL, 128), lambda i: (i, 0))],
        out_specs=[pl.BlockSpec((TL, CW), lambda i: (i, 0)), pl.BlockSpec((TL, 128), lambda i: (i, 0)),
                   pl.BlockSpec((8, 128), lambda i: (0, 0))],
        out_shape=[SDS((S, CW), f32), SDS((S, 128), f32), SDS((8, 128), f32)], compiler_params=_cp(56),
    )(c, dba, par, du, dw, dqg, dkd, dintra, daux)


def _dn_step(state, u, w, qg, kd, intra, aux):
    lane = lax.broadcasted_iota(jnp.int32, (CH, 128), 1)
    row = lax.broadcasted_iota(jnp.int32, (CH, 128), 0)
    outs, states = [], []
    for h in range(4):
        sl = slice(DH * h, DH * (h + 1))
        st = state[h]
        e = jnp.sum(jnp.sum(jnp.where((lane == h) & (row == 0), aux, 0.0), axis=1, keepdims=True), axis=0, keepdims=True)
        v_new = u[:, sl] - _dot(w[:, sl].astype(bf16), st.astype(bf16))
        vb = v_new.astype(bf16)
        outs.append(_dot(qg[:, sl].astype(bf16), st.astype(bf16)) + _dot(intra[h].astype(bf16), vb))
        states.append(st * e + _dot_tn(kd[:, sl].astype(bf16), vb))
    return jnp.concatenate(outs, axis=1), jnp.stack(states, axis=0)


def _dn_rec_fwd(u, w, qg, kd, intra, aux):
    def body(u_ref, w_ref, qg_ref, kd_ref, in_ref, aux_ref, o_ref, st_ref, st_scr):
        @pl.when(pl.program_id(0) == 0)
        def _():
            st_scr[...] = jnp.zeros_like(st_scr)

        st = st_scr[...]
        st_ref[0] = st
        o, new = _dn_step(st, u_ref[...], w_ref[...], qg_ref[...], kd_ref[...], in_ref[...], aux_ref[...])
        o_ref[...] = o
        st_scr[...] = new

    wide = pl.BlockSpec((CH, DNW), lambda n: (n, 0))
    return pl.pallas_call(
        body, grid=(NCHUNK,), name="dn_rec_fwd",
        in_specs=[wide, wide, wide, wide, pl.BlockSpec((4, CH, CH), lambda n: (0, n, 0)),
                  pl.BlockSpec((CH, 128), lambda n: (n, 0))],
        out_specs=[wide, pl.BlockSpec((1, 4, DH, DH), lambda n: (n, 0, 0, 0))],
        out_shape=[SDS((S, DNW), f32), SDS((NCHUNK, 4, DH, DH), f32)],
        scratch_shapes=[pltpu.VMEM((4, DH, DH), f32)], compiler_params=_cp(32),
    )(u, w, qg, kd, intra, aux)


def _dn_rec_bwd(u, w, qg, kd, intra, aux, states, do):
    def body(u_ref, w_ref, qg_ref, kd_ref, in_ref, aux_ref, st_ref, do_ref,
             du_ref, dw_ref, dqg_ref, dkd_ref, din_ref, daux_ref, ds_scr):
        @pl.when(pl.program_id(0) == 0)
        def _():
            ds_scr[...] = jnp.zeros_like(ds_scr)

        _, vjp = jax.vjp(_dn_step, st_ref[0], u_ref[...], w_ref[...], qg_ref[...], kd_ref[...], in_ref[...], aux_ref[...])
        dst, du, dw, dqg, dkd, din, daux = vjp((do_ref[...], ds_scr[...]))
        du_ref[...] = du
        dw_ref[...] = dw
        dqg_ref[...] = dqg
        dkd_ref[...] = dkd
        din_ref[...] = din
        daux_ref[...] = daux
        ds_scr[...] = dst

    rev = lambda n: NCHUNK - 1 - n
    wide = pl.BlockSpec((CH, DNW), lambda n: (rev(n), 0))
    inb = pl.BlockSpec((4, CH, CH), lambda n: (0, rev(n), 0))
    auxb = pl.BlockSpec((CH, 128), lambda n: (rev(n), 0))
    return pl.pallas_call(
        body, grid=(NCHUNK,), name="dn_rec_bwd",
        in_specs=[wide, wide, wide, wide, inb, auxb, pl.BlockSpec((1, 4, DH, DH), lambda n: (rev(n), 0, 0, 0)), wide],
        out_specs=[wide, wide, wide, wide, inb, auxb],
        out_shape=[SDS((S, DNW), f32)] * 4 + [SDS((4, S, CH), f32), SDS((S, 128), f32)],
        scratch_shapes=[pltpu.VMEM((4, DH, DH), f32)], compiler_params=_cp(32),
    )(u, w, qg, kd, intra, aux, states, do)


def _dn_post(o, z, nw):
    parts = []
    for h in range(4):
        sl = slice(DH * h, DH * (h + 1))
        oh = o[:, sl]
        y = oh * lax.rsqrt(jnp.mean(oh * oh, axis=-1, keepdims=True) + EPS) * nw
        zh = z[:, sl]
        parts.append(y * (zh * jax.nn.sigmoid(zh)))
    return jnp.concatenate(parts, axis=1)


def _dn_post_fwd(o, z, nw):
    def body(o_ref, z_ref, nw_ref, y_ref):
        y_ref[...] = _dn_post(o_ref[...], z_ref[...], nw_ref[...])

    wide = pl.BlockSpec((TM, DNW), lambda i: (i, 0))
    return pl.pallas_call(body, grid=(S // TM,), name="dn_post_fwd",
                          in_specs=[wide, wide, pl.BlockSpec((1, 128), lambda i: (0, 0))], out_specs=wide,
                          out_shape=SDS((S, DNW), f32), compiler_params=_cp(32))(o, z, nw)


def _dn_post_bwd(o, z, nw, dy):
    def body(o_ref, z_ref, nw_ref, dy_ref, do_ref, dz_ref, dnw_ref):
        @pl.when(pl.program_id(0) == 0)
        def _():
            dnw_ref[...] = jnp.zeros_like(dnw_ref)

        _, vjp = jax.vjp(_dn_post, o_ref[...], z_ref[...], nw_ref[...])
        do, dz, dnw = vjp(dy_ref[...])
        do_ref[...] = do
        dz_ref[...] = dz
        dnw_ref[...] += dnw

    wide = pl.BlockSpec((TM, DNW), lambda i: (i, 0))
    one = pl.BlockSpec((1, 128), lambda i: (0, 0))
    return pl.pallas_call(body, grid=(S // TM,), name="dn_post_bwd", in_specs=[wide, wide, one, wide],
                          out_specs=[wide, wide, one], out_shape=[SDS((S, DNW), f32), SDS((S, DNW), f32), SDS((1, 128), f32)],
                          compiler_params=_cp(32))(o, z, nw, dy)


def _row_tile(rows, width, itemsize=4, target=2 * 1024 * 1024):
    best = None
    for t in range(16, rows + 1, 16):
        if rows % t == 0 and t * width * itemsize <= target:
            best = t
    return best if best is not None else rows


def _sum_pieces(pieces, out_dtype, name):
    n, rows, width = pieces.shape
    tr = _row_tile(rows, width * n)

    def body(p_ref, o_ref):
        acc = p_ref[0].astype(f32)
        for s in range(1, n):
            acc = acc + p_ref[s].astype(f32)
        o_ref[...] = acc.astype(out_dtype)

    return pl.pallas_call(body, grid=(rows // tr,), name=name,
                          in_specs=[pl.BlockSpec((n, tr, width), lambda i: (0, i, 0))],
                          out_specs=pl.BlockSpec((tr, width), lambda i: (i, 0)),
                          out_shape=SDS((rows, width), out_dtype), compiler_params=_cp(32))(pieces)


def _adamw(w, g, m, v, name):
    rows, width = w.shape
    tr = _row_tile(rows, width * 7, target=12 * 1024 * 1024)

    def body(w_ref, g_ref, m_ref, v_ref, d_ref, nm_ref, nv_ref):
        gv = g_ref[...]
        mn = ADAM_B1 * m_ref[...] + (1.0 - ADAM_B1) * gv
        vn = ADAM_B2 * v_ref[...] + (1.0 - ADAM_B2) * (gv * gv)
        m_hat = mn / (1.0 - ADAM_B1 ** ADAM_STEP)
        v_hat = vn / (1.0 - ADAM_B2 ** ADAM_STEP)
        d_ref[...] = -ADAM_LR * (m_hat / (jnp.sqrt(v_hat) + ADAM_EPS) + ADAM_WD * w_ref[...])
        nm_ref[...] = mn
        nv_ref[...] = vn

    blk = pl.BlockSpec((tr, width), lambda i: (i, 0))
    return pl.pallas_call(body, grid=(rows // tr,), name=name, in_specs=[blk] * 4, out_specs=[blk] * 3,
                          out_shape=[SDS((rows, width), f32)] * 3, compiler_params=_cp(40))(w, g, m, v)


ANY = pl.BlockSpec(memory_space=pl.ANY)


def _place():
    x, y, c = lax.axis_index("x"), lax.axis_index("y"), lax.axis_index("c")
    chips = [(1 - x, y), (x, 1 - y), (1 - x, 1 - y)]
    return x, y, c, chips


def _remote(src, dst, ssem, rsem, dev):
    return pltpu.make_async_remote_copy(src_ref=src, dst_ref=dst, send_sem=ssem, recv_sem=rsem, device_id=dev,
                                        device_id_type=MESH)


def _all_gather_weights(shards):
    nb = len(shards)

    def body(*refs):
        ins, outs = refs[:nb], refs[nb:2 * nb]
        ssem, rsem, lsem = refs[2 * nb:]
        x, y, c, chips = _place()
        me, sib = (x, y, c), (x, y, 1 - c)
        waits = []
        for b in range(nb):
            half = ins[b].shape[0] // 2
            mine_rows = pl.ds(c * half, half)
            other_rows = pl.ds((1 - c) * half, half)
            slot = lambda px, py, rows: outs[b].at[2 * px + py, rows, :]
            local = pltpu.make_async_copy(ins[b].at[mine_rows, :], slot(x, y, mine_rows), lsem.at[b])
            local.start()
            first = [_remote(ins[b].at[mine_rows, :], slot(x, y, mine_rows), ssem.at[b, 0], rsem.at[b, 0], sib)]
            first += [_remote(ins[b].at[mine_rows, :], slot(x, y, mine_rows), ssem.at[b, 1 + j], rsem.at[b, 1 + j], (*chip, c))
                      for j, chip in enumerate(chips)]
            for cp in first:
                cp.start()
            waits.append((b, local, first, mine_rows, other_rows, slot))
        for b, local, first, mine_rows, other_rows, slot in waits:
            passed = []
            for j, chip in enumerate(chips):
                _remote(slot(*chip, mine_rows), slot(*chip, mine_rows), ssem.at[b, 1 + j], rsem.at[b, 1 + j], me).wait_recv()
                cp = _remote(slot(*chip, mine_rows), slot(*chip, mine_rows), ssem.at[b, 4 + j], rsem.at[b, 4 + j], sib)
                cp.start()
                passed.append(cp)
            _remote(slot(x, y, other_rows), slot(x, y, other_rows), ssem.at[b, 0], rsem.at[b, 0], me).wait_recv()
            for j, chip in enumerate(chips):
                _remote(slot(*chip, other_rows), slot(*chip, other_rows), ssem.at[b, 4 + j], rsem.at[b, 4 + j], me).wait_recv()
            for cp in first + passed:
                cp.wait_send()
            local.wait()

    return pl.pallas_call(
        body, name="all_gather_weights", in_specs=[ANY] * nb, out_specs=[ANY] * nb,
        out_shape=[SDS((NCH,) + s.shape, s.dtype) for s in shards],
        scratch_shapes=[pltpu.SemaphoreType.DMA((nb, 7)), pltpu.SemaphoreType.DMA((nb, 7)), pltpu.SemaphoreType.DMA((nb,))],
    )(*shards)


def _swap_halves(parts):
    nb = len(parts)

    def body(*refs):
        ins, owns, gots = refs[:nb], refs[nb:2 * nb], refs[2 * nb:3 * nb]
        ssem, rsem, lsem = refs[3 * nb:]
        x, y, c, _ = _place()
        sib = (x, y, 1 - c)
        todo = []
        for b in range(nb):
            half = ins[b].shape[1] // 2
            local = pltpu.make_async_copy(ins[b].at[:, pl.ds(c * half, half), :], owns[b], lsem.at[b])
            local.start()
            cp = _remote(ins[b].at[:, pl.ds((1 - c) * half, half), :], gots[b], ssem.at[b], rsem.at[b], sib)
            cp.start()
            todo.append((local, cp))
        for local, cp in todo:
            cp.wait()
            local.wait()

    halves = [SDS((p.shape[0], p.shape[1] // 2, p.shape[2]), p.dtype) for p in parts]
    return pl.pallas_call(
        body, name="swap_halves", in_specs=[ANY] * nb, out_specs=[ANY] * (2 * nb), out_shape=halves + halves,
        scratch_shapes=[pltpu.SemaphoreType.DMA((nb,)), pltpu.SemaphoreType.DMA((nb,)), pltpu.SemaphoreType.DMA((nb,))],
    )(*parts)


def _scatter_to_chips(parts):
    nb = len(parts)

    def body(*refs):
        ins, outs = refs[:nb], refs[nb:2 * nb]
        ssem, rsem, lsem = refs[2 * nb:]
        x, y, c, chips = _place()
        me = (x, y, c)
        todo = []
        for b in range(nb):
            local = pltpu.make_async_copy(ins[b].at[2 * x + y], outs[b].at[2 * x + y], lsem.at[b])
            local.start()
            sends = [_remote(ins[b].at[2 * chip[0] + chip[1]], outs[b].at[2 * x + y], ssem.at[b, j], rsem.at[b, j], (*chip, c))
                     for j, chip in enumerate(chips)]
            for cp in sends:
                cp.start()
            todo.append((b, local, sends))
        for b, local, sends in todo:
            for j, chip in enumerate(chips):
                slot = outs[b].at[2 * chip[0] + chip[1]]
                _remote(slot, slot, ssem.at[b, j], rsem.at[b, j], me).wait_recv()
            for cp in sends:
                cp.wait_send()
            local.wait()

    return pl.pallas_call(
        body, name="scatter_to_chips", in_specs=[ANY] * nb, out_specs=[ANY] * nb,
        out_shape=[SDS(p.shape, p.dtype) for p in parts],
        scratch_shapes=[pltpu.SemaphoreType.DMA((nb, 3)), pltpu.SemaphoreType.DMA((nb, 3)), pltpu.SemaphoreType.DMA((nb,))],
    )(*parts)


def _join_halves(halves):
    nb = len(halves)

    def body(*refs):
        ins, outs = refs[:nb], refs[nb:2 * nb]
        ssem, rsem, lsem = refs[2 * nb:]
        x, y, c, _ = _place()
        sib = (x, y, 1 - c)
        todo = []
        for b in range(nb):
            half = ins[b].shape[0]
            mine = outs[b].at[pl.ds(c * half, half), :]
            local = pltpu.make_async_copy(ins[b], mine, lsem.at[b])
            local.start()
            cp = _remote(ins[b], mine, ssem.at[b], rsem.at[b], sib)
            cp.start()
            todo.append((b, local, cp, half))
        for b, local, cp, half in todo:
            other = outs[b].at[pl.ds((1 - c) * half, half), :]
            _remote(other, other, ssem.at[b], rsem.at[b], sib).wait_recv()
            cp.wait_send()
            local.wait()

    return pl.pallas_call(
        body, name="join_halves", in_specs=[ANY] * nb, out_specs=[ANY] * nb,
        out_shape=[SDS((2 * h.shape[0], h.shape[1]), h.dtype) for h in halves],
        scratch_shapes=[pltpu.SemaphoreType.DMA((nb,)), pltpu.SemaphoreType.DMA((nb,)), pltpu.SemaphoreType.DMA((nb,))],
    )(*halves)


def _gather_small(vec):
    def body(v_ref, o_ref, ssem, rsem, lsem):
        x, y, c, _ = _place()
        mine = o_ref.at[4 * x + 2 * y + c]
        local = pltpu.make_async_copy(v_ref, mine, lsem)
        local.start()
        sends = []
        for k in range(1, 8):
            peer = (x ^ (k >> 2), y ^ ((k >> 1) & 1), c ^ (k & 1))
            cp = _remote(v_ref, mine, ssem.at[k - 1], rsem.at[k - 1], peer)
            cp.start()
            sends.append(cp)
        for k in range(1, 8):
            px, py, pc = x ^ (k >> 2), y ^ ((k >> 1) & 1), c ^ (k & 1)
            slot = o_ref.at[4 * px + 2 * py + pc]
            _remote(slot, slot, ssem.at[k - 1], rsem.at[k - 1], (x, y, c)).wait_recv()
        for cp in sends:
            cp.wait_send()
        local.wait()

    return pl.pallas_call(
        body, name="gather_small", in_specs=[ANY], out_specs=ANY, out_shape=SDS((8,) + vec.shape, vec.dtype),
        scratch_shapes=[pltpu.SemaphoreType.DMA((7,)), pltpu.SemaphoreType.DMA((7,)), pltpu.SemaphoreType.DMA],
    )(vec)


def _perm(t, d):
    return t.reshape(S // d, d, t.shape[-1]).transpose(1, 0, 2).reshape(S, t.shape[-1])


def _unperm(t, d):
    return t.reshape(d, S // d, t.shape[-1]).transpose(1, 0, 2).reshape(S, t.shape[-1])


def _block_diag(pw):
    out = jnp.zeros((256, 256), pw.dtype)
    for g in range(4):
        out = lax.dynamic_update_slice(out, pw[g], (64 * g, 64 * g))
    return out


def _pad_rows(a, rows):
    return jnp.concatenate([a, jnp.zeros((rows - a.shape[0],) + a.shape[1:], a.dtype)], axis=0)


def _layer_fwd(l, x0, pos, freq, wts, blob_a, blob_b):
    sv = {"x0": x0}
    x1 = _ffn_fwd(x0, wts["ffn1_norm"][l:l + 1], blob_a, blob_b, 4 * l + 0, 4 * l + 1, 2 * l + 0)
    att, pu, dq, dz, dba = _inproj_fwd(x1, wts["mix_norm"][l:l + 1], wts["w_aug"][l], pos, freq)
    qkv3 = jnp.concatenate([_perm(att, d) for d in PATTERN_DIL], axis=0)
    o3p, l3p = _attn_fwd(qkv3)
    o3 = jnp.stack([_unperm(o3p[S * p:S * (p + 1)], d) for p, d in enumerate(PATTERN_DIL)])
    l3 = jnp.stack([_unperm(l3p[S * p:S * (p + 1)], d) for p, d in enumerate(PATTERN_DIL)])
    ya = _merge_fwd(o3, l3)
    yb = _pool_fwd(pu, wts["pool_bd"][l], wts["pool_scale"][l:l + 1])
    c = _conv_fwd(dq, wts["conv_w"][l])
    u, w, qg, kd, intra, aux = _dn_local_fwd(c, dba, wts["dn_par"][l])
    o_dn, states = _dn_rec_fwd(u, w, qg, kd, intra, aux)
    yc = _dn_post_fwd(o_dn, dz, wts["dn_out_norm"][l:l + 1])
    x2 = _outproj_fwd(x1, ya, yb, yc, blob_b, 11 + l)
    x3 = _ffn_fwd(x2, wts["ffn2_norm"][l:l + 1], blob_a, blob_b, 4 * l + 2, 4 * l + 3, 2 * l + 1)
    sv.update(x1=x1, x2=x2, qkv3=qkv3, o3=o3, l3=l3, ya=ya, yb=yb, yc=yc, pu=pu, dq=dq, dz=dz, dba=dba, c=c,
              u=u, w=w, qg=qg, kd=kd, intra=intra, aux=aux, states=states, o_dn=o_dn)
    return x3, sv


def _layer_bwd(l, dx3, sv, pos, freq, wts, blob_a, blob_b):
    gr = {}
    g2, u2, d2, dh4 = _ffn_bwd(sv["x2"], wts["ffn2_norm"][l:l + 1], blob_a, blob_b, 4 * l + 2, 4 * l + 3, 2 * l + 1, dx3)
    dx2, gr["ffn2_norm"] = _norm_bwd(sv["x2"], wts["ffn2_norm"][l:l + 1], dx3, dh4)
    gr.update(ffn2_w_gate=g2, ffn2_w_up=u2, ffn2_w_down=d2)
    dya, dyb, dyc, gr["w_out"] = _outproj_bwd(dx2, sv["ya"], sv["yb"], sv["yc"], blob_b, 11 + l)
    do_dn, ddz, gr["dn_out_norm"] = _dn_post_bwd(sv["o_dn"], sv["dz"], wts["dn_out_norm"][l:l + 1], dyc)
    du, dw, dqg, dkd, dintra, daux = _dn_rec_bwd(sv["u"], sv["w"], sv["qg"], sv["kd"], sv["intra"], sv["aux"], sv["states"], do_dn)
    dc, ddba, gr["dn_par"] = _dn_local_bwd(sv["c"], sv["dba"], wts["dn_par"][l], du, dw, dqg, dkd, dintra, daux)
    ddq, gr["conv_w"] = _conv_bwd(sv["dq"], wts["conv_w"][l], dc)
    dpu, gr["pool_bd"], gr["pool_scale"] = _pool_bwd(sv["pu"], wts["pool_bd"][l], wts["pool_scale"][l:l + 1], dyb)
    do3, dl3 = _merge_bwd(sv["o3"], sv["l3"], dya)
    do3p = jnp.concatenate([_perm(do3[p], d) for p, d in enumerate(PATTERN_DIL)], axis=0)
    dl3p = jnp.concatenate([_perm(dl3[p], d) for p, d in enumerate(PATTERN_DIL)], axis=0)
    dqp, dkp, dkc, dvp, dvc = _attn_bwd(sv["qkv3"], do3p, dl3p)
    dqkv3 = _attn_fold(dqp, dkp, dkc, dvp, dvc)
    datt3 = jnp.stack([_unperm(dqkv3[S * p:S * (p + 1)], d) for p, d in enumerate(PATTERN_DIL)])
    dx1, gr["mix_norm"], gr["w_aug"] = _inproj_bwd(sv["x1"], wts["mix_norm"][l:l + 1], wts["w_aug"][l], pos, freq, dx2,
                                                    datt3, dpu, ddq, ddz, ddba)
    g1, u1, d1, dh4 = _ffn_bwd(sv["x0"], wts["ffn1_norm"][l:l + 1], blob_a, blob_b, 4 * l + 0, 4 * l + 1, 2 * l + 0, dx1)
    dx0, gr["ffn1_norm"] = _norm_bwd(sv["x0"], wts["ffn1_norm"][l:l + 1], dx1, dh4)
    gr.update(ffn1_w_gate=g1, ffn1_w_up=u1, ffn1_w_down=d1)
    return dx0, gr


def _device_step(x, pos, target, wts, blob_a, blob_b):
    freq = jnp.tile(ROPE_THETA ** (-jnp.arange(0, EH, 2, dtype=f32) / EH), 2 * ATT // EH).reshape(1, ATT)
    saved = []
    h = x
    for l in range(DEPTH):
        h, sv = _layer_fwd(l, h, pos, freq, wts, blob_a, blob_b)
        saved.append(sv)
    dh, g_final, loss = _final(h, wts["final_norm"], target)
    grads = [None] * DEPTH
    for l in reversed(range(DEPTH)):
        dh, grads[l] = _layer_bwd(l, dh, saved[l], pos, freq, wts, blob_a, blob_b)
    return loss, dh, g_final, grads


_SMALL = (("ffn1_norm", (DEPTH, D)), ("mix_norm", (DEPTH, D)), ("pool_w", (DEPTH, 4, 64, 64)), ("pool_scale", (DEPTH, 256)),
          ("dn_conv_w", (DEPTH, 4, CW)), ("dn_a_log", (DEPTH, 4)), ("dn_dt_bias", (DEPTH, 4)), ("dn_out_norm", (DEPTH, 128)),
          ("ffn2_norm", (DEPTH, D)), ("final_norm", (D,)), ("loss", (1,)))


def _pack_small(vals):
    rows = []
    for name, shape in _SMALL:
        flat = vals[name].astype(f32).reshape(-1)
        n = -(-flat.shape[0] // 128) * 128
        rows.append(jnp.concatenate([flat, jnp.zeros((n - flat.shape[0],), f32)]).reshape(-1, 128))
    out = jnp.concatenate(rows, axis=0)
    return _pad_rows(out, -(-out.shape[0] // 16) * 16)


def _unpack_small(packed):
    vals, r = {}, 0
    for name, shape in _SMALL:
        size = int(np.prod(shape))
        n = -(-size // 128)
        vals[name] = packed[r:r + n].reshape(-1)[:size].reshape(shape)
        r += n
    return vals


def kernel(x, positions, ffn1_norm, ffn1_w_gate, ffn1_w_up, ffn1_w_down, mix_norm, w_in, pool_w, pool_scale, dn_conv_w, dn_a_log, dn_dt_bias, dn_out_norm, w_out, ffn2_norm, ffn2_w_gate, ffn2_w_up, ffn2_w_down, final_norm, loss_target, m_ffn1_norm, m_ffn1_w_gate, m_ffn1_w_up, m_ffn1_w_down, m_mix_norm, m_w_in, m_pool_w, m_pool_scale, m_dn_conv_w, m_dn_a_log, m_dn_dt_bias, m_dn_out_norm, m_w_out, m_ffn2_norm, m_ffn2_w_gate, m_ffn2_w_up, m_ffn2_w_down, m_final_norm, v_ffn1_norm, v_ffn1_w_gate, v_ffn1_w_up, v_ffn1_w_down, v_mix_norm, v_w_in, v_pool_w, v_pool_scale, v_dn_conv_w, v_dn_a_log, v_dn_dt_bias, v_dn_out_norm, v_w_out, v_ffn2_norm, v_ffn2_w_gate, v_ffn2_w_up, v_ffn2_w_down, v_final_norm):
    names = ["ffn1_norm", "ffn1_w_gate", "ffn1_w_up", "ffn1_w_down", "mix_norm", "w_in", "pool_w", "pool_scale", "dn_conv_w",
             "dn_a_log", "dn_dt_bias", "dn_out_norm", "w_out", "ffn2_norm", "ffn2_w_gate", "ffn2_w_up", "ffn2_w_down", "final_norm"]
    W = dict(zip(names, [ffn1_norm, ffn1_w_gate, ffn1_w_up, ffn1_w_down, mix_norm, w_in, pool_w, pool_scale, dn_conv_w,
                         dn_a_log, dn_dt_bias, dn_out_norm, w_out, ffn2_norm, ffn2_w_gate, ffn2_w_up, ffn2_w_down, final_norm]))
    M = dict(zip(names, [m_ffn1_norm, m_ffn1_w_gate, m_ffn1_w_up, m_ffn1_w_down, m_mix_norm, m_w_in, m_pool_w, m_pool_scale,
                         m_dn_conv_w, m_dn_a_log, m_dn_dt_bias, m_dn_out_norm, m_w_out, m_ffn2_norm, m_ffn2_w_gate, m_ffn2_w_up,
                         m_ffn2_w_down, m_final_norm]))
    V = dict(zip(names, [v_ffn1_norm, v_ffn1_w_gate, v_ffn1_w_up, v_ffn1_w_down, v_mix_norm, v_w_in, v_pool_w, v_pool_scale,
                         v_dn_conv_w, v_dn_a_log, v_dn_dt_bias, v_dn_out_norm, v_w_out, v_ffn2_norm, v_ffn2_w_gate, v_ffn2_w_up,
                         v_ffn2_w_down, v_final_norm]))
    chip = 2 * lax.axis_index("x") + lax.axis_index("y")

    def blobs_of(T, dtype):
        a = jnp.concatenate([T[n][l] for l in range(DEPTH) for n in ("ffn1_w_gate", "ffn1_w_up", "ffn2_w_gate", "ffn2_w_up")], axis=0)
        b = jnp.concatenate([T[n][l] for l in range(DEPTH) for n in ("ffn1_w_down", "ffn2_w_down")]
                            + [T["w_out"][l] for l in range(DEPTH)], axis=0)
        c = jnp.concatenate([T["w_in"][l] for l in range(DEPTH)], axis=0)
        return a.astype(dtype), b.astype(dtype), c.astype(dtype)

    sh_a, sh_b, sh_c = blobs_of(W, bf16)
    blob_a, blob_b, blob_c = _all_gather_weights([sh_a, sh_b, sh_c])
    conv_all = _gather_small(_pad_rows(dn_conv_w.reshape(DEPTH * 4 * (CW // NCH) // 128, 128), 32))
    conv_full = jnp.concatenate([conv_all[2 * j, :DEPTH * 4 * (CW // NCH) // 128].reshape(DEPTH, 4, CW // NCH) for j in range(NCH)],
                                axis=-1)

    win = blob_c.reshape(NCH, DEPTH, D, INC).transpose(1, 2, 0, 3).reshape(DEPTH, D, INW)
    w_aug = jnp.concatenate([win, jnp.zeros((DEPTH, D, INP - INW), bf16)], axis=-1)
    par = jnp.zeros((DEPTH, 8, 128), f32)
    par = par.at[:, 0, 4:8].set(dn_a_log).at[:, 1, 4:8].set(dn_dt_bias)
    wts = dict(ffn1_norm=ffn1_norm, mix_norm=mix_norm, ffn2_norm=ffn2_norm, final_norm=final_norm.reshape(1, D),
               w_aug=w_aug, pool_bd=jnp.stack([_block_diag(pool_w[l]) for l in range(DEPTH)]).astype(bf16),
               pool_scale=pool_scale, conv_w=jnp.concatenate([conv_full, jnp.zeros((DEPTH, 4, CW), f32)], axis=1),
               dn_par=par, dn_out_norm=dn_out_norm)

    loss, dx, g_final, grads = _device_step(x[0], positions.reshape(S, 1), loss_target[0], wts, blob_a, blob_b)

    G = {n: [grads[l][n] for l in range(DEPTH)] for n in ("ffn1_w_gate", "ffn1_w_up", "ffn2_w_gate", "ffn2_w_up",
                                                           "ffn1_w_down", "ffn2_w_down")}
    part_a = jnp.concatenate([G[n][l] for l in range(DEPTH) for n in ("ffn1_w_gate", "ffn1_w_up", "ffn2_w_gate", "ffn2_w_up")], axis=1)
    wout_parts = [grads[l]["w_out"].astype(bf16).reshape(NCH, 256, D) for l in range(DEPTH)]
    part_b = jnp.concatenate([G[n][l] for l in range(DEPTH) for n in ("ffn1_w_down", "ffn2_w_down")] + wout_parts, axis=1)
    win_parts = [grads[l]["w_aug"][:, :INW].astype(bf16).reshape(D, NCH, INC).transpose(1, 0, 2) for l in range(DEPTH)]
    part_c = jnp.concatenate(win_parts, axis=1)
    owns_gots = _swap_halves([part_a, part_b, part_c])
    owns, gots = owns_gots[:3], owns_gots[3:]
    chip_sums = []
    for k, (o, g) in enumerate(zip(owns, gots)):
        n, rows, width = o.shape
        pair = jnp.stack([o.reshape(n * rows, width), g.reshape(n * rows, width)])
        chip_sums.append(_sum_pieces(pair, bf16, "sum_core_pair").reshape(n, rows, width))
    pieces = _scatter_to_chips(chip_sums)
    halves = [_sum_pieces(p, f32, "sum_chips") for p in pieces]
    full_a, full_b, full_c = _join_halves(halves)

    small = {"loss": loss[0, 0:1], "final_norm": g_final.reshape(D)}
    for n in ("ffn1_norm", "mix_norm", "ffn2_norm", "pool_scale", "dn_out_norm"):
        small[n] = jnp.stack([grads[l][n].reshape(-1) for l in range(DEPTH)])
    small["pool_w"] = jnp.stack([jnp.stack([grads[l]["pool_bd"][64 * g:64 * (g + 1), 64 * g:64 * (g + 1)] for g in range(4)])
                                 for l in range(DEPTH)])
    small["dn_conv_w"] = jnp.stack([grads[l]["conv_w"][0:4] for l in range(DEPTH)])
    small["dn_a_log"] = jnp.stack([grads[l]["dn_par"][0, 4:8] for l in range(DEPTH)])
    small["dn_dt_bias"] = jnp.stack([grads[l]["dn_par"][1, 4:8] for l in range(DEPTH)])
    packed = _pack_small(small)
    g_small = _sum_pieces(_gather_small(packed), f32, "sum_small")
    gs = _unpack_small(g_small)
    conv_cols = lax.dynamic_slice_in_dim(gs["dn_conv_w"], chip * (CW // NCH), CW // NCH, axis=2)

    def shard_blobs(T):
        a, b, c = blobs_of(T, f32)
        return a, b, c

    wa, wb, wc = shard_blobs(W)
    ma, mb, mc = shard_blobs(M)
    va, vb, vc = shard_blobs(V)
    res_a = _adamw(wa, full_a, ma, va, "adamw_a")
    res_b = _adamw(wb, full_b, mb, vb, "adamw_b")
    res_c = _adamw(wc, full_c, mc, vc, "adamw_c")

    def small_of(T):
        d = {n: T[n] for n, _ in _SMALL if n not in ("loss", "dn_conv_w")}
        d["loss"] = jnp.zeros((1,), f32)
        full = jnp.zeros((DEPTH, 4, CW), f32)
        d["dn_conv_w"] = lax.dynamic_update_slice_in_dim(full, T["dn_conv_w"], chip * (CW // NCH), axis=2)
        return _pack_small(d)

    res_s = _adamw(small_of(W), g_small, small_of(M), small_of(V), "adamw_small")
    small_out = [_unpack_small(r) for r in res_s]

    def split_blobs(a, b, c):
        out = {}
        a4 = a.reshape(DEPTH, 4, D, FC)
        for k, n in enumerate(("ffn1_w_gate", "ffn1_w_up", "ffn2_w_gate", "ffn2_w_up")):
            out[n] = a4[:, k]
        b4 = b[:4 * FC].reshape(DEPTH, 2, FC, D)
        out["ffn1_w_down"], out["ffn2_w_down"] = b4[:, 0], b4[:, 1]
        out["w_out"] = b[4 * FC:].reshape(DEPTH, 256, D)
        out["w_in"] = c.reshape(DEPTH, D, INC)
        return out

    def assemble(big, sm):
        out = []
        for n in names:
            if n in big:
                out.append(big[n])
            elif n == "dn_conv_w":
                out.append(lax.dynamic_slice_in_dim(sm[n], chip * (CW // NCH), CW // NCH, axis=2))
            else:
                out.append(sm[n])
        return out

    grad_list = assemble(split_blobs(full_a, full_b, full_c), dict(gs, dn_conv_w=gs["dn_conv_w"]))
    del conv_cols
    outs = [gs["loss"].reshape(()), dx.reshape(1, S, D)] + grad_list
    for k in range(3):
        outs += assemble(split_blobs(res_a[k], res_b[k], res_c[k]), small_out[k])
    return tuple(outs)
```

```python
import functools
import math

import jax
import jax.numpy as jnp
import numpy as np
from jax import lax
from jax.experimental import pallas as pl
from jax.experimental.pallas import tpu as pltpu

f32 = jnp.float32
bf16 = jnp.bfloat16
SDS = jax.ShapeDtypeStruct
MESH = pl.DeviceIdType.MESH

S = 4096
D = 1024
DEPTH = 2
FF = 2816
NCH = 4
FC = FF // NCH
INW = 3080
INC = INW // NCH
INP = 3200
ATT = 256
EH = 64
NBLK = 128
DNW = 512
DH = 128
CH = 64
NCHUNK = S // CH
EPS = 1e-6
ROPE_THETA = 10000.0
PATTERN_DIL = (1, 4, 16)
ADAM_LR, ADAM_B1, ADAM_B2, ADAM_EPS, ADAM_WD, ADAM_STEP = 0.001, 0.9, 0.999, 1e-08, 0.01, 10
VMEM_BYTES_V7X = 64 * 1024 * 1024
NEG = -1e30

TM = 512
RA, RB, RC = 8 * D, 4 * FC + 2 * 256, 2 * D


def _cp(vmem_mb=48, sem=None):
    kw = dict(vmem_limit_bytes=vmem_mb * 1024 * 1024)
    if sem is not None:
        kw["dimension_semantics"] = sem
    return pltpu.CompilerParams(**kw)


def _dot(a, b):
    return jnp.dot(a, b, preferred_element_type=f32)


def _dot_nt(a, b):
    return lax.dot_general(a, b, (((1,), (1,)), ((), ())), preferred_element_type=f32)


def _dot_tn(a, b):
    return lax.dot_general(a, b, (((0,), (0,)), ((), ())), preferred_element_type=f32)


def _rms(x, w):
    r = lax.rsqrt(jnp.mean(x * x, axis=-1, keepdims=True) + EPS)
    return x * r * w, r


def _rms_bwd(x, w, r, dh):
    xhat = x * r
    dw = jnp.sum(dh * xhat, axis=0, keepdims=True)
    dxh = dh * w
    dx = r * (dxh - xhat * jnp.mean(dxh * xhat, axis=-1, keepdims=True))
    return dx, dw


def _ffn_fwd(x, nw, blob_a, blob_b, kg, ku, kd):
    def body(x_ref, nw_ref, wg_ref, wu_ref, wd_ref, o_ref, h_scr, acc_scr):
        j = pl.program_id(1)

        @pl.when(j == 0)
        def _():
            h, _ = _rms(x_ref[...], nw_ref[...])
            h_scr[...] = h.astype(bf16)
            acc_scr[...] = jnp.zeros_like(acc_scr)

        h = h_scr[...]
        g = _dot(h, wg_ref[0])
        u = _dot(h, wu_ref[0])
        a = (g * jax.nn.sigmoid(g) * u).astype(bf16)
        acc_scr[...] += _dot(a, wd_ref[0])

        @pl.when(j == NCH - 1)
        def _():
            o_ref[...] = x_ref[...] + 0.5 * acc_scr[...]

    return pl.pallas_call(
        body, grid=(S // TM, NCH), name="ffn_fwd",
        in_specs=[pl.BlockSpec((TM, D), lambda i, j: (i, 0)),
                  pl.BlockSpec((1, D), lambda i, j: (0, 0)),
                  pl.BlockSpec((1, D, FC), lambda i, j: (j, kg, 0)),
                  pl.BlockSpec((1, D, FC), lambda i, j: (j, ku, 0)),
                  pl.BlockSpec((1, FC, D), lambda i, j: (j, kd, 0))],
        out_specs=pl.BlockSpec((TM, D), lambda i, j: (i, 0)),
        out_shape=SDS((S, D), f32),
        scratch_shapes=[pltpu.VMEM((TM, D), bf16), pltpu.VMEM((TM, D), f32)],
        compiler_params=_cp(40),
    )(x, nw, blob_a, blob_a, blob_b)


def _ffn_bwd(x, nw, blob_a, blob_b, kg, ku, kd, dy):
    nt = S // TM

    def body(x_ref, nw_ref, wg_ref, wu_ref, wd_ref, dy_ref, dwg_ref, dwu_ref, dwd_ref, dh_ref, ag, au, ad):
        i = pl.program_id(1)

        @pl.when(i == 0)
        def _():
            ag[...] = jnp.zeros_like(ag)
            au[...] = jnp.zeros_like(au)
            ad[...] = jnp.zeros_like(ad)

        hf, _ = _rms(x_ref[...], nw_ref[...])
        h = hf.astype(bf16)
        g = _dot(h, wg_ref[0])
        u = _dot(h, wu_ref[0])
        sg = jax.nn.sigmoid(g)
        s = g * sg
        a = (s * u).astype(bf16)
        dyb = (0.5 * dy_ref[...]).astype(bf16)
        da = _dot_nt(dyb, wd_ref[0])
        ad[...] += _dot_tn(a, dyb)
        du = (da * s).astype(bf16)
        dg = (da * u * (sg * (1.0 + g * (1.0 - sg)))).astype(bf16)
        ag[...] += _dot_tn(h, dg)
        au[...] += _dot_tn(h, du)
        dh_ref[0] = (_dot_nt(dg, wg_ref[0]) + _dot_nt(du, wu_ref[0])).astype(bf16)

        @pl.when(i == nt - 1)
        def _():
            dwg_ref[0] = ag[...].astype(bf16)
            dwu_ref[0] = au[...].astype(bf16)
            dwd_ref[0] = ad[...].astype(bf16)

    return pl.pallas_call(
        body, grid=(NCH, nt), name="ffn_bwd",
        in_specs=[pl.BlockSpec((TM, D), lambda j, i: (i, 0)),
                  pl.BlockSpec((1, D), lambda j, i: (0, 0)),
                  pl.BlockSpec((1, D, FC), lambda j, i: (j, kg, 0)),
                  pl.BlockSpec((1, D, FC), lambda j, i: (j, ku, 0)),
                  pl.BlockSpec((1, FC, D), lambda j, i: (j, kd, 0)),
                  pl.BlockSpec((TM, D), lambda j, i: (i, 0))],
        out_specs=[pl.BlockSpec((1, D, FC), lambda j, i: (j, 0, 0)),
                   pl.BlockSpec((1, D, FC), lambda j, i: (j, 0, 0)),
                   pl.BlockSpec((1, FC, D), lambda j, i: (j, 0, 0)),
                   pl.BlockSpec((1, TM, D), lambda j, i: (j, i, 0))],
        out_shape=[SDS((NCH, D, FC), bf16), SDS((NCH, D, FC), bf16), SDS((NCH, FC, D), bf16), SDS((NCH, S, D), bf16)],
        scratch_shapes=[pltpu.VMEM((D, FC), f32), pltpu.VMEM((D, FC), f32), pltpu.VMEM((FC, D), f32)],
        compiler_params=_cp(56),
    )(x, nw, blob_a, blob_a, blob_b, dy)


def _norm_bwd(x, nw, dres, dh4):
    nt = S // TM
    nparts = dh4.shape[0]

    def body(x_ref, nw_ref, dres_ref, dh_ref, dx_ref, dnw_ref):
        i = pl.program_id(0)
        dh = dh_ref[0].astype(f32)
        for p in range(1, nparts):
            dh = dh + dh_ref[p].astype(f32)
        xv = x_ref[...]
        _, r = _rms(xv, nw_ref[...])
        dx, dw = _rms_bwd(xv, nw_ref[...], r, dh)
        dx_ref[...] = dres_ref[...] + dx

        @pl.when(i == 0)
        def _():
            dnw_ref[...] = jnp.zeros_like(dnw_ref)

        dnw_ref[...] += dw

    return pl.pallas_call(
        body, grid=(nt,), name="norm_bwd",
        in_specs=[pl.BlockSpec((TM, D), lambda i: (i, 0)),
                  pl.BlockSpec((1, D), lambda i: (0, 0)),
                  pl.BlockSpec((TM, D), lambda i: (i, 0)),
                  pl.BlockSpec((nparts, TM, D), lambda i: (0, i, 0))],
        out_specs=[pl.BlockSpec((TM, D), lambda i: (i, 0)), pl.BlockSpec((1, D), lambda i: (0, 0))],
        out_shape=[SDS((S, D), f32), SDS((1, D), f32)],
        compiler_params=_cp(40),
    )(x, nw, dres, dh4)


def _final(x, nw, target):
    nt = S // TM

    def body(x_ref, nw_ref, t_ref, dx_ref, dnw_ref, loss_ref):
        i = pl.program_id(0)
        xv = x_ref[...]
        y, r = _rms(xv, nw_ref[...])
        err = y - t_ref[...]
        part = 0.5 * jnp.sum(jnp.mean(err * err, axis=-1, keepdims=True), axis=0, keepdims=True)
        dx, dw = _rms_bwd(xv, nw_ref[...], r, err * (1.0 / D))
        dx_ref[...] = dx

        @pl.when(i == 0)
        def _():
            dnw_ref[...] = jnp.zeros_like(dnw_ref)
            loss_ref[...] = jnp.zeros_like(loss_ref)

        dnw_ref[...] += dw
        loss_ref[...] += jnp.broadcast_to(part, loss_ref.shape)

    return pl.pallas_call(
        body, grid=(nt,), name="final_loss",
        in_specs=[pl.BlockSpec((TM, D), lambda i: (i, 0)),
                  pl.BlockSpec((1, D), lambda i: (0, 0)),
                  pl.BlockSpec((TM, D), lambda i: (i, 0))],
        out_specs=[pl.BlockSpec((TM, D), lambda i: (i, 0)), pl.BlockSpec((1, D), lambda i: (0, 0)),
                   pl.BlockSpec((1, 128), lambda i: (0, 0))],
        out_shape=[SDS((S, D), f32), SDS((1, D), f32), SDS((1, 128), f32)],
        compiler_params=_cp(40),
    )(x, nw, target)


def _rot_half(t):
    lane = lax.broadcasted_iota(jnp.int32, t.shape, 1)
    first = (lane % EH) < (EH // 2)
    return jnp.where(first, -pltpu.roll(t, ATT - EH // 2, 1), pltpu.roll(t, EH // 2, 1))


def _rope_tables(pos_ref, freq_ref):
    ang = pos_ref[...].astype(f32) * freq_ref[...]
    return jnp.cos(ang), jnp.sin(ang)


def _inproj_fwd(x, nw, w_aug, pos, freq):
    TI = 256

    def body(x_ref, nw_ref, w_hbm, pos_ref, freq_ref, att_ref, pu_ref, dq_ref, dz_ref, dba_ref, w_scr):
        @pl.when(pl.program_id(0) == 0)
        def _():
            pltpu.sync_copy(w_hbm, w_scr)

        h, _ = _rms(x_ref[...], nw_ref[...])
        proj = _dot(h.astype(bf16), w_scr[...])
        cos, sin = _rope_tables(pos_ref, freq_ref)
        q = proj[:, 0:ATT]
        k = proj[:, ATT:2 * ATT]
        att_ref[:, 0:ATT] = q * cos + _rot_half(q) * sin
        att_ref[:, ATT:2 * ATT] = k * cos + _rot_half(k) * sin
        att_ref[:, 2 * ATT:3 * ATT] = proj[:, 2 * ATT:3 * ATT]
        pu_ref[...] = proj[:, 768:1024]
        dq_ref[...] = proj[:, 1024:2560]
        dz_ref[...] = proj[:, 2560:3072]
        dba_ref[...] = proj[:, 3072:3200]

    return pl.pallas_call(
        body, grid=(S // TI,), name="inproj_fwd",
        in_specs=[pl.BlockSpec((TI, D), lambda i: (i, 0)),
                  pl.BlockSpec((1, D), lambda i: (0, 0)),
                  pl.BlockSpec(memory_space=pl.ANY),
                  pl.BlockSpec((TI, 1), lambda i: (i, 0)),
                  pl.BlockSpec((1, ATT), lambda i: (0, 0))],
        out_specs=[pl.BlockSpec((TI, 768), lambda i: (i, 0)), pl.BlockSpec((TI, 256), lambda i: (i, 0)),
                   pl.BlockSpec((TI, 1536), lambda i: (i, 0)), pl.BlockSpec((TI, 512), lambda i: (i, 0)),
                   pl.BlockSpec((TI, 128), lambda i: (i, 0))],
        out_shape=[SDS((S, 768), f32), SDS((S, 256), f32), SDS((S, 1536), f32), SDS((S, 512), f32), SDS((S, 128), f32)],
        scratch_shapes=[pltpu.VMEM((D, INP), bf16)],
        compiler_params=_cp(48),
    )(x, nw, w_aug, pos, freq)


def _inproj_bwd(x, nw, w_aug, pos, freq, dres, datt3, dpu, ddq, ddz, ddba):
    TI = 256
    nt = S // TI

    def body(x_ref, nw_ref, w_hbm, pos_ref, freq_ref, dres_ref, datt_ref, dpu_ref, ddq_ref, ddz_ref, ddba_ref,
             dx_ref, dnw_ref, dw_hbm, w_scr, acc):
        i = pl.program_id(0)

        @pl.when(i == 0)
        def _():
            pltpu.sync_copy(w_hbm, w_scr)
            acc[...] = jnp.zeros_like(acc)
            dnw_ref[...] = jnp.zeros_like(dnw_ref)

        xv = x_ref[...]
        hf, r = _rms(xv, nw_ref[...])
        h = hf.astype(bf16)
        cos, sin = _rope_tables(pos_ref, freq_ref)
        datt = datt_ref[0] + datt_ref[1] + datt_ref[2]
        dq = datt[:, 0:ATT]
        dk = datt[:, ATT:2 * ATT]
        dq = dq * cos - _rot_half(dq) * sin
        dk = dk * cos - _rot_half(dk) * sin
        dproj = jnp.concatenate([dq, dk, datt[:, 2 * ATT:3 * ATT], dpu_ref[...], ddq_ref[...], ddz_ref[...], ddba_ref[...]],
                                axis=1).astype(bf16)
        acc[...] += _dot_tn(h, dproj)
        dh = _dot_nt(dproj, w_scr[...])
        dx, dw = _rms_bwd(xv, nw_ref[...], r, dh)
        dx_ref[...] = dres_ref[...] + dx
        dnw_ref[...] += dw

        @pl.when(i == nt - 1)
        def _():
            pltpu.sync_copy(acc, dw_hbm)

    return pl.pallas_call(
        body, grid=(nt,), name="inproj_bwd",
        in_specs=[pl.BlockSpec((TI, D), lambda i: (i, 0)),
                  pl.BlockSpec((1, D), lambda i: (0, 0)),
                  pl.BlockSpec(memory_space=pl.ANY),
                  pl.BlockSpec((TI, 1), lambda i: (i, 0)),
                  pl.BlockSpec((1, ATT), lambda i: (0, 0)),
                  pl.BlockSpec((TI, D), lambda i: (i, 0)),
                  pl.BlockSpec((3, TI, 768), lambda i: (0, i, 0)),
                  pl.BlockSpec((TI, 256), lambda i: (i, 0)),
                  pl.BlockSpec((TI, 1536), lambda i: (i, 0)),
                  pl.BlockSpec((TI, 512), lambda i: (i, 0)),
                  pl.BlockSpec((TI, 128), lambda i: (i, 0))],
        out_specs=[pl.BlockSpec((TI, D), lambda i: (i, 0)), pl.BlockSpec((1, D), lambda i: (0, 0)),
                   pl.BlockSpec(memory_space=pl.ANY)],
        out_shape=[SDS((S, D), f32), SDS((1, D), f32), SDS((D, INP), f32)],
        scratch_shapes=[pltpu.VMEM((D, INP), bf16), pltpu.VMEM((D, INP), f32)],
        compiler_params=_cp(56),
    )(x, nw, w_aug, pos, freq, dres, datt3, dpu, ddq, ddz, ddba)


def _outproj_fwd(x, ya, yb, yc, blob_b, kw):
    def body(x_ref, ya_ref, yb_ref, yc_ref, w_ref, o_ref):
        ycat = jnp.concatenate([ya_ref[...], yb_ref[...], yc_ref[...]], axis=1).astype(bf16)
        o_ref[...] = x_ref[...] + _dot(ycat, w_ref[...].reshape(D, D))

    return pl.pallas_call(
        body, grid=(S // TM,), name="outproj_fwd",
        in_specs=[pl.BlockSpec((TM, D), lambda i: (i, 0)),
                  pl.BlockSpec((TM, 256), lambda i: (i, 0)),
                  pl.BlockSpec((TM, 256), lambda i: (i, 0)),
                  pl.BlockSpec((TM, 512), lambda i: (i, 0)),
                  pl.BlockSpec((NCH, 256, D), lambda i: (0, kw, 0))],
        out_specs=pl.BlockSpec((TM, D), lambda i: (i, 0)),
        out_shape=SDS((S, D), f32),
        compiler_params=_cp(40),
    )(x, ya, yb, yc, blob_b)


def _outproj_bwd(dy, ya, yb, yc, blob_b, kw):
    nt = S // TM

    def body(dy_ref, ya_ref, yb_ref, yc_ref, w_ref, dya_ref, dyb_ref, dyc_ref, dw_ref):
        i = pl.program_id(0)

        @pl.when(i == 0)
        def _():
            dw_ref[...] = jnp.zeros_like(dw_ref)

        dyv = dy_ref[...].astype(bf16)
        ycat = jnp.concatenate([ya_ref[...], yb_ref[...], yc_ref[...]], axis=1).astype(bf16)
        dw_ref[...] += _dot_tn(ycat, dyv)
        dcat = _dot_nt(dyv, w_ref[...].reshape(D, D))
        dya_ref[...] = dcat[:, 0:256]
        dyb_ref[...] = dcat[:, 256:512]
        dyc_ref[...] = dcat[:, 512:1024]

    return pl.pallas_call(
        body, grid=(nt,), name="outproj_bwd",
        in_specs=[pl.BlockSpec((TM, D), lambda i: (i, 0)),
                  pl.BlockSpec((TM, 256), lambda i: (i, 0)),
                  pl.BlockSpec((TM, 256), lambda i: (i, 0)),
                  pl.BlockSpec((TM, 512), lambda i: (i, 0)),
                  pl.BlockSpec((NCH, 256, D), lambda i: (0, kw, 0))],
        out_specs=[pl.BlockSpec((TM, 256), lambda i: (i, 0)), pl.BlockSpec((TM, 256), lambda i: (i, 0)),
                   pl.BlockSpec((TM, 512), lambda i: (i, 0)), pl.BlockSpec((D, D), lambda i: (0, 0))],
        out_shape=[SDS((S, 256), f32), SDS((S, 256), f32), SDS((S, 512), f32), SDS((D, D), f32)],
        compiler_params=_cp(40),
    )(dy, ya, yb, yc, blob_b)


NB3 = 3 * S // NBLK


def _first_of_sequence(b):
    return jnp.where(b < 32, b == 0, jnp.where(b < 64, b % 8 == 0, b % 2 == 0))


def _attn_block(q, kp, kc, vp, vc, first):
    kk = jnp.concatenate([kp, kc], axis=0).astype(bf16)
    vv = jnp.concatenate([vp, vc], axis=0).astype(bf16)
    qi = lax.broadcasted_iota(jnp.int32, (NBLK, 2 * NBLK), 0)
    ki = lax.broadcasted_iota(jnp.int32, (NBLK, 2 * NBLK), 1)
    dist = NBLK + qi - ki
    valid = (dist >= 0) & (dist <= NBLK) & (jnp.logical_not(first) | (ki >= NBLK))
    head = lax.broadcasted_iota(jnp.int32, (1, ATT), 1) // EH
    o = jnp.zeros((NBLK, ATT), f32)
    l = jnp.zeros((NBLK, ATT), f32)
    for h in range(4):
        mh = (head == h).astype(f32)
        s = _dot_nt((q * mh).astype(bf16), kk) * (1.0 / math.sqrt(EH))
        s = jnp.where(valid, s, NEG)
        m = lax.stop_gradient(jnp.max(s, axis=-1, keepdims=True))
        p = jnp.exp(s - m)
        den = jnp.sum(p, axis=-1, keepdims=True)
        o = o + _dot((p / den).astype(bf16), vv) * mh
        l = l + (m + jnp.log(den)) * mh
    return o, l


def _attn_specs():
    prev = lambda b: jnp.maximum(b - 1, 0)
    return [pl.BlockSpec((NBLK, ATT), lambda b: (b, 0)),
            pl.BlockSpec((NBLK, ATT), lambda b: (prev(b), 1)),
            pl.BlockSpec((NBLK, ATT), lambda b: (b, 1)),
            pl.BlockSpec((NBLK, ATT), lambda b: (prev(b), 2)),
            pl.BlockSpec((NBLK, ATT), lambda b: (b, 2))]


def _attn_fwd(qkv3):
    def body(q_ref, kp_ref, kc_ref, vp_ref, vc_ref, o_ref, l_ref):
        first = _first_of_sequence(pl.program_id(0))
        o, l = _attn_block(q_ref[...], kp_ref[...], kc_ref[...], vp_ref[...], vc_ref[...], first)
        o_ref[...] = o
        l_ref[...] = l

    blk = pl.BlockSpec((NBLK, ATT), lambda b: (b, 0))
    return pl.pallas_call(
        body, grid=(NB3,), name="attn_fwd", in_specs=_attn_specs(), out_specs=[blk, blk],
        out_shape=[SDS((3 * S, ATT), f32), SDS((3 * S, ATT), f32)], compiler_params=_cp(32),
    )(qkv3, qkv3, qkv3, qkv3, qkv3)


def _attn_bwd(qkv3, do3, dl3):
    def body(q_ref, kp_ref, kc_ref, vp_ref, vc_ref, do_ref, dl_ref, dq_ref, dkp_ref, dkc_ref, dvp_ref, dvc_ref):
        first = _first_of_sequence(pl.program_id(0))
        fn = lambda q, kp, kc, vp, vc: _attn_block(q, kp, kc, vp, vc, first)
        _, vjp = jax.vjp(fn, q_ref[...], kp_ref[...], kc_ref[...], vp_ref[...], vc_ref[...])
        dq, dkp, dkc, dvp, dvc = vjp((do_ref[...], dl_ref[...]))
        dq_ref[...] = dq
        dkp_ref[...] = dkp
        dkc_ref[...] = dkc
        dvp_ref[...] = dvp
        dvc_ref[...] = dvc

    blk = pl.BlockSpec((NBLK, ATT), lambda b: (b, 0))
    return pl.pallas_call(
        body, grid=(NB3,), name="attn_bwd", in_specs=_attn_specs() + [blk, blk], out_specs=[blk] * 5,
        out_shape=[SDS((3 * S, ATT), f32)] * 5, compiler_params=_cp(32),
    )(qkv3, qkv3, qkv3, qkv3, qkv3, do3, dl3)


def _attn_fold(dq, dkp, dkc, dvp, dvc):
    def body(dq_ref, kc_ref, kn_ref, vc_ref, vn_ref, o_ref):
        b = pl.program_id(0)
        take = jnp.logical_and(b < NB3 - 1, jnp.logical_not(_first_of_sequence(b + 1))).astype(f32)
        o_ref[:, 0:ATT] = dq_ref[...]
        o_ref[:, ATT:2 * ATT] = kc_ref[...] + take * kn_ref[...]
        o_ref[:, 2 * ATT:3 * ATT] = vc_ref[...] + take * vn_ref[...]

    cur = pl.BlockSpec((NBLK, ATT), lambda b: (b, 0))
    nxt = pl.BlockSpec((NBLK, ATT), lambda b: (jnp.minimum(b + 1, NB3 - 1), 0))
    return pl.pallas_call(
        body, grid=(NB3,), name="attn_fold", in_specs=[cur, cur, nxt, cur, nxt],
        out_specs=pl.BlockSpec((NBLK, 768), lambda b: (b, 0)), out_shape=SDS((3 * S, 768), f32), compiler_params=_cp(32),
    )(dq, dkc, dkp, dvc, dvp)


def _merge_weights(l_ref):
    l0, l1, l2 = l_ref[0], l_ref[1], l_ref[2]
    m = jnp.maximum(jnp.maximum(l0, l1), l2)
    e0, e1, e2 = jnp.exp(l0 - m), jnp.exp(l1 - m), jnp.exp(l2 - m)
    tot = e0 + e1 + e2
    return e0 / tot, e1 / tot, e2 / tot


def _merge_fwd(o3, l3):
    def body(o_ref, l_ref, y_ref):
        w0, w1, w2 = _merge_weights(l_ref)
        y_ref[...] = w0 * o_ref[0] + w1 * o_ref[1] + w2 * o_ref[2]

    b3 = pl.BlockSpec((3, TM, ATT), lambda i: (0, i, 0))
    return pl.pallas_call(body, grid=(S // TM,), name="merge_fwd", in_specs=[b3, b3],
                          out_specs=pl.BlockSpec((TM, ATT), lambda i: (i, 0)), out_shape=SDS((S, ATT), f32),
                          compiler_params=_cp(32))(o3, l3)


def _merge_bwd(o3, l3, dy):
    def body(o_ref, l_ref, dy_ref, do_ref, dl_ref):
        ws = _merge_weights(l_ref)
        y = ws[0] * o_ref[0] + ws[1] * o_ref[1] + ws[2] * o_ref[2]
        dyv = dy_ref[...]
        for p in range(3):
            do_ref[p] = ws[p] * dyv
            dl_ref[p] = ws[p] * (o_ref[p] - y) * dyv

    b3 = pl.BlockSpec((3, TM, ATT), lambda i: (0, i, 0))
    return pl.pallas_call(body, grid=(S // TM,), name="merge_bwd",
                          in_specs=[b3, b3, pl.BlockSpec((TM, ATT), lambda i: (i, 0))], out_specs=[b3, b3],
                          out_shape=[SDS((3, S, ATT), f32), SDS((3, S, ATT), f32)], compiler_params=_cp(32))(o3, l3, dy)


HALO = 16


def _pool_consts(i, rows):
    grp = lax.broadcasted_iota(jnp.int32, (rows, 256), 1) // 64
    t = i * TM + lax.broadcasted_iota(jnp.int32, (rows, 256), 0)
    win = jnp.where(grp == 0, 2, jnp.where(grp == 1, 4, jnp.where(grp == 2, 8, 16)))
    cnt = jnp.minimum(t + 1, win).astype(f32)
    return grp, cnt


def _pool_select(grp, s2, s4, s8, s16):
    return jnp.where(grp == 0, s2, jnp.where(grp == 1, s4, jnp.where(grp == 2, s8, s16)))


def _pooled(i, cur, halo):
    xx = jnp.concatenate([halo, cur], axis=0)
    s2 = xx + pltpu.roll(xx, 1, 0)
    s4 = s2 + pltpu.roll(s2, 2, 0)
    s8 = s4 + pltpu.roll(s4, 4, 0)
    s16 = s8 + pltpu.roll(s8, 8, 0)
    grp, cnt = _pool_consts(i, TM)
    tot = _pool_select(grp, s2[HALO:], s4[HALO:], s8[HALO:], s16[HALO:])
    return tot / cnt - cur


def _pool_fwd(u, wp, scale):
    def body(u_ref, halo_ref, wp_ref, sc_ref, y_ref):
        i = pl.program_id(0)
        halo = halo_ref[...] * (i > 0).astype(f32)
        pooled = _pooled(i, u_ref[...], halo)
        y_ref[...] = _dot(pooled.astype(bf16), wp_ref[...]) * sc_ref[...]

    return pl.pallas_call(
        body, grid=(S // TM,), name="pool_fwd",
        in_specs=[pl.BlockSpec((TM, 256), lambda i: (i, 0)),
                  pl.BlockSpec((HALO, 256), lambda i: (jnp.maximum(i * (TM // HALO) - 1, 0), 0)),
                  pl.BlockSpec((256, 256), lambda i: (0, 0)),
                  pl.BlockSpec((1, 256), lambda i: (0, 0))],
        out_specs=pl.BlockSpec((TM, 256), lambda i: (i, 0)), out_shape=SDS((S, 256), f32), compiler_params=_cp(32),
    )(u, u, wp, scale)


def _pool_bwd(u, wp, scale, dy):
    nt = S // TM

    def body(u_ref, halo_ref, wp_ref, sc_ref, dy_ref, dyn_ref, du_ref, dwp_ref, dsc_ref):
        i = pl.program_id(0)

        @pl.when(i == 0)
        def _():
            dwp_ref[...] = jnp.zeros_like(dwp_ref)
            dsc_ref[...] = jnp.zeros_like(dsc_ref)

        halo = halo_ref[...] * (i > 0).astype(f32)
        pooled = _pooled(i, u_ref[...], halo).astype(bf16)
        dyv = dy_ref[...]
        dsc_ref[...] += jnp.sum(dyv * _dot(pooled, wp_ref[...]), axis=0, keepdims=True)
        dys = (dyv * sc_ref[...]).astype(bf16)
        dwp_ref[...] += _dot_tn(pooled, dys)
        dpool = _dot_nt(dys, wp_ref[...])
        grp, cnt = _pool_consts(i, TM)
        dyn = ((dyn_ref[...] * (i < nt - 1).astype(f32)) * sc_ref[...]).astype(bf16)
        _, cntn = _pool_consts(i + 1, HALO)
        zn = _dot_nt(dyn, wp_ref[...]) / cntn
        zz = jnp.concatenate([dpool / cnt, zn], axis=0)
        n = TM + HALO
        a2 = zz + pltpu.roll(zz, n - 1, 0)
        a4 = a2 + pltpu.roll(a2, n - 2, 0)
        a8 = a4 + pltpu.roll(a4, n - 4, 0)
        a16 = a8 + pltpu.roll(a8, n - 8, 0)
        du_ref[...] = _pool_select(grp, a2[:TM], a4[:TM], a8[:TM], a16[:TM]) - dpool

    return pl.pallas_call(
        body, grid=(nt,), name="pool_bwd",
        in_specs=[pl.BlockSpec((TM, 256), lambda i: (i, 0)),
                  pl.BlockSpec((HALO, 256), lambda i: (jnp.maximum(i * (TM // HALO) - 1, 0), 0)),
                  pl.BlockSpec((256, 256), lambda i: (0, 0)),
                  pl.BlockSpec((1, 256), lambda i: (0, 0)),
                  pl.BlockSpec((TM, 256), lambda i: (i, 0)),
                  pl.BlockSpec((HALO, 256), lambda i: (jnp.minimum((i + 1) * (TM // HALO), S // HALO - 1), 0))],
        out_specs=[pl.BlockSpec((TM, 256), lambda i: (i, 0)), pl.BlockSpec((256, 256), lambda i: (0, 0)),
                   pl.BlockSpec((1, 256), lambda i: (0, 0))],
        out_shape=[SDS((S, 256), f32), SDS((256, 256), f32), SDS((1, 256), f32)], compiler_params=_cp(32),
    )(u, u, wp, scale, dy, dy)


CW = 3 * DNW
CHALO = 8
TC = 256


def _conv_fwd(u, w):
    def body(u_ref, halo_ref, w_ref, c_ref):
        i = pl.program_id(0)
        xx = jnp.concatenate([halo_ref[...] * (i > 0).astype(f32), u_ref[...]], axis=0)
        c = (w_ref[3:4, :] * xx + w_ref[2:3, :] * pltpu.roll(xx, 1, 0) + w_ref[1:2, :] * pltpu.roll(xx, 2, 0)
             + w_ref[0:1, :] * pltpu.roll(xx, 3, 0))
        c_ref[...] = c[CHALO:]

    return pl.pallas_call(
        body, grid=(S // TC,), name="conv_fwd",
        in_specs=[pl.BlockSpec((TC, CW), lambda i: (i, 0)),
                  pl.BlockSpec((CHALO, CW), lambda i: (jnp.maximum(i * (TC // CHALO) - 1, 0), 0)),
                  pl.BlockSpec((8, CW), lambda i: (0, 0))],
        out_specs=pl.BlockSpec((TC, CW), lambda i: (i, 0)), out_shape=SDS((S, CW), f32), compiler_params=_cp(32),
    )(u, u, w)


def _conv_bwd(u, w, dc):
    nt = S // TC

    def body(u_ref, halo_ref, w_ref, dc_ref, dcn_ref, du_ref, dw_ref):
        i = pl.program_id(0)

        @pl.when(i == 0)
        def _():
            dw_ref[...] = jnp.zeros_like(dw_ref)

        dcv = dc_ref[...]
        zz = jnp.concatenate([dcv, dcn_ref[...] * (i < nt - 1).astype(f32)], axis=0)
        n = TC + CHALO
        du = (w_ref[3:4, :] * zz + w_ref[2:3, :] * pltpu.roll(zz, n - 1, 0) + w_ref[1:2, :] * pltpu.roll(zz, n - 2, 0)
              + w_ref[0:1, :] * pltpu.roll(zz, n - 3, 0))
        du_ref[...] = du[:TC]
        xx = jnp.concatenate([halo_ref[...] * (i > 0).astype(f32), u_ref[...]], axis=0)
        for j in range(4):
            shifted = xx if j == 3 else pltpu.roll(xx, 3 - j, 0)
            dw_ref[j:j + 1, :] += jnp.sum(dcv * shifted[CHALO:], axis=0, keepdims=True)

    return pl.pallas_call(
        body, grid=(nt,), name="conv_bwd",
        in_specs=[pl.BlockSpec((TC, CW), lambda i: (i, 0)),
                  pl.BlockSpec((CHALO, CW), lambda i: (jnp.maximum(i * (TC // CHALO) - 1, 0), 0)),
                  pl.BlockSpec((8, CW), lambda i: (0, 0)),
                  pl.BlockSpec((TC, CW), lambda i: (i, 0)),
                  pl.BlockSpec((CHALO, CW), lambda i: (jnp.minimum((i + 1) * (TC // CHALO), S // CHALO - 1), 0))],
        out_specs=[pl.BlockSpec((TC, CW), lambda i: (i, 0)), pl.BlockSpec((8, CW), lambda i: (0, 0))],
        out_shape=[SDS((S, CW), f32), SDS((8, CW), f32)], compiler_params=_cp(32),
    )(u, u, w, dc, dc)


TL = 256
NCL = TL // CH


def _bdot(a, b):
    return jnp.einsum('nik,nkj->nij', a.astype(bf16), b.astype(bf16), preferred_element_type=f32)


def _bdot_nt(a, b):
    return jnp.einsum('nik,njk->nij', a.astype(bf16), b.astype(bf16), preferred_element_type=f32)


def _dn_local(c, dba, a_row, b_row):
    act = c * jax.nn.sigmoid(c)
    lane = lax.broadcasted_iota(jnp.int32, (TL, 128), 1)
    beta_all = jax.nn.sigmoid(dba)
    xs = dba + b_row
    softplus = jnp.maximum(xs, 0.0) + jnp.log(1.0 + jnp.exp(-jnp.abs(xs)))
    g_all = -jnp.exp(a_row) * softplus
    ii = lax.broadcasted_iota(jnp.int32, (1, CH, CH), 1)
    jj = lax.broadcasted_iota(jnp.int32, (1, CH, CH), 2)
    lower = jj <= ii
    strict = jj < ii
    eye = (ii == jj).astype(f32)
    us, ws, qgs, kds, intras = [], [], [], [], []
    aux = jnp.zeros((TL, 128), f32)
    for h in range(4):
        q = act[:, DH * h:DH * (h + 1)]
        k = act[:, DNW + DH * h:DNW + DH * (h + 1)]
        v = act[:, 2 * DNW + DH * h:2 * DNW + DH * (h + 1)]
        q = q * lax.rsqrt(jnp.sum(q * q, axis=-1, keepdims=True) + EPS) * (DH ** -0.5)
        k = k * lax.rsqrt(jnp.sum(k * k, axis=-1, keepdims=True) + EPS)
        beta = jnp.sum(jnp.where(lane == h, beta_all, 0.0), axis=1, keepdims=True)
        g = jnp.sum(jnp.where(lane == 4 + h, g_all, 0.0), axis=1, keepdims=True)
        q3, k3, v3 = q.reshape(NCL, CH, DH), k.reshape(NCL, CH, DH), v.reshape(NCL, CH, DH)
        beta3, g3 = beta.reshape(NCL, CH, 1), g.reshape(NCL, CH, 1)
        g_row = jnp.sum(eye * g3, axis=1, keepdims=True)
        gc_col = jnp.sum(jnp.where(lower, g_row, 0.0), axis=2, keepdims=True)
        gc_row = jnp.sum(jnp.where(ii <= jj, g3, 0.0), axis=1, keepdims=True)
        diff = gc_col - gc_row
        decay = jnp.where(lower, jnp.exp(jnp.where(lower, diff, 0.0)), 0.0)
        kb = k3 * beta3
        vb = v3 * beta3
        a = jnp.where(strict, _bdot_nt(kb, k3) * decay, 0.0)
        t = eye - a
        p = a
        for _ in range(5):
            p = _bdot(p, p)
            t = t + _bdot(t, p)
        u3 = _bdot(t, vb)
        w3 = _bdot(t, kb * jnp.exp(gc_col))
        intra = jnp.where(lower, _bdot_nt(q3, k3) * decay, 0.0)
        g_last = jnp.sum(g3, axis=1, keepdims=True)
        us.append(u3.reshape(TL, DH))
        ws.append(w3.reshape(TL, DH))
        qgs.append((q3 * jnp.exp(gc_col)).reshape(TL, DH))
        kds.append((k3 * jnp.exp(g_last - gc_col)).reshape(TL, DH))
        intras.append(intra.reshape(TL, CH))
        e_last = jnp.broadcast_to(jnp.exp(g_last), (NCL, CH, 1)).reshape(TL, 1)
        aux = aux + jnp.where(lane == h, e_last, 0.0)
    cat = lambda xs: jnp.concatenate(xs, axis=1)
    return cat(us), cat(ws), cat(qgs), cat(kds), jnp.stack(intras, axis=0), aux


def _dn_local_fwd(c, dba, par):
    def body(c_ref, dba_ref, par_ref, u_ref, w_ref, qg_ref, kd_ref, in_ref, aux_ref):
        u, w, qg, kd, intra, aux = _dn_local(c_ref[...], dba_ref[...], par_ref[0:1, :], par_ref[1:2, :])
        u_ref[...] = u
        w_ref[...] = w
        qg_ref[...] = qg
        kd_ref[...] = kd
        in_ref[...] = intra
        aux_ref[...] = aux

    wide = pl.BlockSpec((TL, DNW), lambda i: (i, 0))
    return pl.pallas_call(
        body, grid=(S // TL,), name="dn_local_fwd",
        in_specs=[pl.BlockSpec((TL, CW), lambda i: (i, 0)), pl.BlockSpec((TL, 128), lambda i: (i, 0)),
                  pl.BlockSpec((8, 128), lambda i: (0, 0))],
        out_specs=[wide, wide, wide, wide, pl.BlockSpec((4, TL, CH), lambda i: (0, i, 0)),
                   pl.BlockSpec((TL, 128), lambda i: (i, 0))],
        out_shape=[SDS((S, DNW), f32)] * 4 + [SDS((4, S, CH), f32), SDS((S, 128), f32)], compiler_params=_cp(48),
    )(c, dba, par)


def _dn_local_bwd(c, dba, par, du, dw, dqg, dkd, dintra, daux):
    def body(c_ref, dba_ref, par_ref, du_ref, dw_ref, dqg_ref, dkd_ref, din_ref, daux_ref, dc_ref, ddba_ref, dpar_ref):
        @pl.when(pl.program_id(0) == 0)
        def _():
            dpar_ref[...] = jnp.zeros_like(dpar_ref)

        _, vjp = jax.vjp(_dn_local, c_ref[...], dba_ref[...], par_ref[0:1, :], par_ref[1:2, :])
        dc, ddba, da_row, db_row = vjp((du_ref[...], dw_ref[...], dqg_ref[...], dkd_ref[...], din_ref[...], daux_ref[...]))
        dc_ref[...] = dc
        ddba_ref[...] = ddba
        dpar_ref[0:1, :] += da_row
        dpar_ref[1:2, :] += db_row

    wide = pl.BlockSpec((TL, DNW), lambda i: (i, 0))
    return pl.pallas_call(
        body, grid=(S // TL,), name="dn_local_bwd",
        in_specs=[pl.BlockSpec((TL, CW), lambda i: (i, 0)), pl.BlockSpec((TL, 128), lambda i: (i, 0)),
                  pl.BlockSpec((8, 128), lambda i: (0, 0)), wide, wide, wide, wide,
                  pl.BlockSpec((4, TL, CH), lambda i: (0, i, 0)), pl.BlockSpec((TL, 128), lambda i: (i, 0))],
        out_specs=[pl.BlockSpec((TL, CW), lambda i: (i, 0)), pl.BlockSpec((TL, 128), lambda i: (i, 0)),
                   pl.BlockSpec((8, 128), lambda i: (0, 0))],
        out_shape=[SDS((S, CW), f32), SDS((S, 128), f32), SDS((8, 128), f32)], compiler_params=_cp(56),
    )(c, dba, par, du, dw, dqg, dkd, dintra, daux)


def _dn_step(state, u, w, qg, kd, intra, aux):
    lane = lax.broadcasted_iota(jnp.int32, (CH, 128), 1)
    row = lax.broadcasted_iota(jnp.int32, (CH, 128), 0)
    outs, states = [], []
    for h in range(4):
        sl = slice(DH * h, DH * (h + 1))
        st = state[h]
        e = jnp.sum(jnp.sum(jnp.where((lane == h) & (row == 0), aux, 0.0), axis=1, keepdims=True), axis=0, keepdims=True)
        v_new = u[:, sl] - _dot(w[:, sl].astype(bf16), st.astype(bf16))
        vb = v_new.astype(bf16)
        outs.append(_dot(qg[:, sl].astype(bf16), st.astype(bf16)) + _dot(intra[h].astype(bf16), vb))
        states.append(st * e + _dot_tn(kd[:, sl].astype(bf16), vb))
    return jnp.concatenate(outs, axis=1), jnp.stack(states, axis=0)


def _dn_rec_fwd(u, w, qg, kd, intra, aux):
    def body(u_ref, w_ref, qg_ref, kd_ref, in_ref, aux_ref, o_ref, st_ref, st_scr):
        @pl.when(pl.program_id(0) == 0)
        def _():
            st_scr[...] = jnp.zeros_like(st_scr)

        st = st_scr[...]
        st_ref[0] = st
        o, new = _dn_step(st, u_ref[...], w_ref[...], qg_ref[...], kd_ref[...], in_ref[...], aux_ref[...])
        o_ref[...] = o
        st_scr[...] = new

    wide = pl.BlockSpec((CH, DNW), lambda n: (n, 0))
    return pl.pallas_call(
        body, grid=(NCHUNK,), name="dn_rec_fwd",
        in_specs=[wide, wide, wide, wide, pl.BlockSpec((4, CH, CH), lambda n: (0, n, 0)),
                  pl.BlockSpec((CH, 128), lambda n: (n, 0))],
        out_specs=[wide, pl.BlockSpec((1, 4, DH, DH), lambda n: (n, 0, 0, 0))],
        out_shape=[SDS((S, DNW), f32), SDS((NCHUNK, 4, DH, DH), f32)],
        scratch_shapes=[pltpu.VMEM((4, DH, DH), f32)], compiler_params=_cp(32),
    )(u, w, qg, kd, intra, aux)


def _dn_rec_bwd(u, w, qg, kd, intra, aux, states, do):
    def body(u_ref, w_ref, qg_ref, kd_ref, in_ref, aux_ref, st_ref, do_ref,
             du_ref, dw_ref, dqg_ref, dkd_ref, din_ref, daux_ref, ds_scr):
        @pl.when(pl.program_id(0) == 0)
        def _():
            ds_scr[...] = jnp.zeros_like(ds_scr)

        _, vjp = jax.vjp(_dn_step, st_ref[0], u_ref[...], w_ref[...], qg_ref[...], kd_ref[...], in_ref[...], aux_ref[...])
        dst, du, dw, dqg, dkd, din, daux = vjp((do_ref[...], ds_scr[...]))
        du_ref[...] = du
        dw_ref[...] = dw
        dqg_ref[...] = dqg
        dkd_ref[...] = dkd
        din_ref[...] = din
        daux_ref[...] = daux
        ds_scr[...] = dst

    rev = lambda n: NCHUNK - 1 - n
    wide = pl.BlockSpec((CH, DNW), lambda n: (rev(n), 0))
    inb = pl.BlockSpec((4, CH, CH), lambda n: (0, rev(n), 0))
    auxb = pl.BlockSpec((CH, 128), lambda n: (rev(n), 0))
    return pl.pallas_call(
        body, grid=(NCHUNK,), name="dn_rec_bwd",
        in_specs=[wide, wide, wide, wide, inb, auxb, pl.BlockSpec((1, 4, DH, DH), lambda n: (rev(n), 0, 0, 0)), wide],
        out_specs=[wide, wide, wide, wide, inb, auxb],
        out_shape=[SDS((S, DNW), f32)] * 4 + [SDS((4, S, CH), f32), SDS((S, 128), f32)],
        scratch_shapes=[pltpu.VMEM((4, DH, DH), f32)], compiler_params=_cp(32),
    )(u, w, qg, kd, intra, aux, states, do)


def _dn_post(o, z, nw):
    parts = []
    for h in range(4):
        sl = slice(DH * h, DH * (h + 1))
        oh = o[:, sl]
        y = oh * lax.rsqrt(jnp.mean(oh * oh, axis=-1, keepdims=True) + EPS) * nw
        zh = z[:, sl]
        parts.append(y * (zh * jax.nn.sigmoid(zh)))
    return jnp.concatenate(parts, axis=1)


def _dn_post_fwd(o, z, nw):
    def body(o_ref, z_ref, nw_ref, y_ref):
        y_ref[...] = _dn_post(o_ref[...], z_ref[...], nw_ref[...])

    wide = pl.BlockSpec((TM, DNW), lambda i: (i, 0))
    return pl.pallas_call(body, grid=(S // TM,), name="dn_post_fwd",
                          in_specs=[wide, wide, pl.BlockSpec((1, 128), lambda i: (0, 0))], out_specs=wide,
                          out_shape=SDS((S, DNW), f32), compiler_params=_cp(32))(o, z, nw)


def _dn_post_bwd(o, z, nw, dy):
    def body(o_ref, z_ref, nw_ref, dy_ref, do_ref, dz_ref, dnw_ref):
        @pl.when(pl.program_id(0) == 0)
        def _():
            dnw_ref[...] = jnp.zeros_like(dnw_ref)

        _, vjp = jax.vjp(_dn_post, o_ref[...], z_ref[...], nw_ref[...])
        do, dz, dnw = vjp(dy_ref[...])
        do_ref[...] = do
        dz_ref[...] = dz
        dnw_ref[...] += dnw

    wide = pl.BlockSpec((TM, DNW), lambda i: (i, 0))
    one = pl.BlockSpec((1, 128), lambda i: (0, 0))
    return pl.pallas_call(body, grid=(S // TM,), name="dn_post_bwd", in_specs=[wide, wide, one, wide],
                          out_specs=[wide, wide, one], out_shape=[SDS((S, DNW), f32), SDS((S, DNW), f32), SDS((1, 128), f32)],
                          compiler_params=_cp(32))(o, z, nw, dy)


def _row_tile(rows, width, itemsize=4, target=2 * 1024 * 1024):
    best = None
    for t in range(16, rows + 1, 16):
        if rows % t == 0 and t * width * itemsize <= target:
            best = t
    return best if best is not None else rows


def _sum_pieces(pieces, out_dtype, name):
    n, rows, width = pieces.shape
    tr = _row_tile(rows, width * n)

    def body(p_ref, o_ref):
        acc = p_ref[0].astype(f32)
        for s in range(1, n):
            acc = acc + p_ref[s].astype(f32)
        o_ref[...] = acc.astype(out_dtype)

    return pl.pallas_call(body, grid=(rows // tr,), name=name,
                          in_specs=[pl.BlockSpec((n, tr, width), lambda i: (0, i, 0))],
                          out_specs=pl.BlockSpec((tr, width), lambda i: (i, 0)),
                          out_shape=SDS((rows, width), out_dtype), compiler_params=_cp(32))(pieces)


def _sum_core_pair(part, got, c_arr):
    n, rows, width = part.shape
    half = rows // 2
    tr = _row_tile(half, width, itemsize=2)
    nt = half // tr

    def body(c_ref, p_ref, g_ref, o_ref):
        o_ref[...] = (p_ref[...].astype(f32) + g_ref[...].astype(f32)).astype(bf16)

    gs = pltpu.PrefetchScalarGridSpec(
        num_scalar_prefetch=1, grid=(n, nt),
        in_specs=[pl.BlockSpec((1, tr, width), lambda j, i, c: (j, c[0] * nt + i, 0)),
                  pl.BlockSpec((1, tr, width), lambda j, i, c: (j, i, 0))],
        out_specs=pl.BlockSpec((1, tr, width), lambda j, i, c: (j, i, 0)))
    return pl.pallas_call(body, grid_spec=gs, name="sum_core_pair", out_shape=SDS((n, half, width), bf16),
                          compiler_params=_cp(32))(c_arr, part, got)


def _sum_chips(pieces, c_arr):
    n, half, width = pieces.shape
    tr = _row_tile(half, width * n, itemsize=2)
    nt = half // tr

    def body(c_ref, p_ref, o_ref):
        acc = p_ref[0].astype(f32)
        for s in range(1, n):
            acc = acc + p_ref[s].astype(f32)
        o_ref[...] = acc

    gs = pltpu.PrefetchScalarGridSpec(
        num_scalar_prefetch=1, grid=(nt,),
        in_specs=[pl.BlockSpec((n, tr, width), lambda i, c: (0, i, 0))],
        out_specs=pl.BlockSpec((tr, width), lambda i, c: (c[0] * nt + i, 0)))
    return pl.pallas_call(body, grid_spec=gs, name="sum_chips", out_shape=SDS((2 * half, width), f32),
                          compiler_params=_cp(32))(c_arr, pieces)


def _adamw_math(w, g, m, v):
    mn = ADAM_B1 * m + (1.0 - ADAM_B1) * g
    vn = ADAM_B2 * v + (1.0 - ADAM_B2) * (g * g)
    m_hat = mn / (1.0 - ADAM_B1 ** ADAM_STEP)
    v_hat = vn / (1.0 - ADAM_B2 ** ADAM_STEP)
    return -ADAM_LR * (m_hat / (jnp.sqrt(v_hat) + ADAM_EPS) + ADAM_WD * w), mn, vn


def _adamw(w, g, m, v, name):
    rows, width = w.shape
    tr = _row_tile(rows, width * 7, target=12 * 1024 * 1024)

    def body(w_ref, g_ref, m_ref, v_ref, d_ref, nm_ref, nv_ref):
        d_ref[...], nm_ref[...], nv_ref[...] = _adamw_math(w_ref[...], g_ref[...], m_ref[...], v_ref[...])

    blk = pl.BlockSpec((tr, width), lambda i: (i, 0))
    return pl.pallas_call(body, grid=(rows // tr,), name=name, in_specs=[blk] * 4, out_specs=[blk] * 3,
                          out_shape=[SDS((rows, width), f32)] * 3, compiler_params=_cp(40))(w, g, m, v)


def _adamw_rows(w, m, v, gblob, first_block, name):
    layers, rows, width = w.shape
    tr = rows // 2

    def body(w_ref, g_ref, m_ref, v_ref, d_ref, nm_ref, nv_ref):
        d_ref[0], nm_ref[0], nv_ref[0] = _adamw_math(w_ref[0], g_ref[...], m_ref[0], v_ref[0])

    blk = pl.BlockSpec((1, tr, width), lambda l, i: (l, i, 0))
    gblk = pl.BlockSpec((tr, width), lambda l, i: (2 * first_block(l) + i, 0))
    return pl.pallas_call(body, grid=(layers, 2), name=name, in_specs=[blk, gblk, blk, blk], out_specs=[blk] * 3,
                          out_shape=[SDS(w.shape, f32)] * 3, compiler_params=_cp(40))(w, gblob, m, v)


ANY = pl.BlockSpec(memory_space=pl.ANY)


def _place():
    x, y, c = lax.axis_index("x"), lax.axis_index("y"), lax.axis_index("c")
    chips = [(1 - x, y), (x, 1 - y), (1 - x, 1 - y)]
    return x, y, c, chips


NQ_ICI = 4
NQ_D2D = 8


def _chunks(rows, n):
    step = rows // n
    assert step * n == rows and step % 16 == 0, (rows, n)
    return [(q * step, step) for q in range(n)]


def _remote(src, dst, ssem, rsem, dev):
    return pltpu.make_async_remote_copy(src_ref=src, dst_ref=dst, send_sem=ssem, recv_sem=rsem, device_id=dev,
                                        device_id_type=MESH)


def _all_gather_weights(shards):
    nb = len(shards)

    def body(*refs):
        ins, outs = refs[:nb], refs[nb:2 * nb]
        ssem, rsem, lsem = refs[2 * nb:]
        x, y, c, chips = _place()
        me, sib = (x, y, c), (x, y, 1 - c)
        started, locals_ = [], []
        for b in range(nb):
            half = ins[b].shape[0] // 2
            for q, (off, n) in enumerate(_chunks(half, NQ_ICI)):
                mine = pl.ds(c * half + off, n)
                own = outs[b].at[2 * x + y, mine, :]
                local = pltpu.make_async_copy(ins[b].at[mine, :], own, lsem.at[b, q])
                local.start()
                first = [_remote(ins[b].at[mine, :], own, ssem.at[b, 0, q], rsem.at[b, 0, q], sib)]
                first += [_remote(ins[b].at[mine, :], own, ssem.at[b, 1 + j, q], rsem.at[b, 1 + j, q], (*chip, c))
                          for j, chip in enumerate(chips)]
                for cp in first:
                    cp.start()
                started += first
                locals_.append(local)
        for b in range(nb):
            half = ins[b].shape[0] // 2
            for q, (off, n) in enumerate(_chunks(half, NQ_ICI)):
                mine = pl.ds(c * half + off, n)
                for j, chip in enumerate(chips):
                    landed = outs[b].at[2 * chip[0] + chip[1], mine, :]
                    _remote(landed, landed, ssem.at[b, 1 + j, q], rsem.at[b, 1 + j, q], me).wait_recv()
                    cp = _remote(landed, landed, ssem.at[b, 4 + j, q], rsem.at[b, 4 + j, q], sib)
                    cp.start()
                    started.append(cp)
        for b in range(nb):
            half = ins[b].shape[0] // 2
            for q, (off, n) in enumerate(_chunks(half, NQ_ICI)):
                other = pl.ds((1 - c) * half + off, n)
                theirs = outs[b].at[2 * x + y, other, :]
                _remote(theirs, theirs, ssem.at[b, 0, q], rsem.at[b, 0, q], me).wait_recv()
                for j, chip in enumerate(chips):
                    fwd = outs[b].at[2 * chip[0] + chip[1], other, :]
                    _remote(fwd, fwd, ssem.at[b, 4 + j, q], rsem.at[b, 4 + j, q], me).wait_recv()
        for cp in started:
            cp.wait_send()
        for cp in locals_:
            cp.wait()

    return pl.pallas_call(
        body, name="all_gather_weights", in_specs=[ANY] * nb, out_specs=[ANY] * nb,
        out_shape=[SDS((NCH,) + s.shape, s.dtype) for s in shards],
        scratch_shapes=[pltpu.SemaphoreType.DMA((nb, 7, NQ_ICI)), pltpu.SemaphoreType.DMA((nb, 7, NQ_ICI)),
                        pltpu.SemaphoreType.DMA((nb, NQ_ICI))],
    )(*shards)


def _send_sibling_half(parts):
    nb = len(parts)

    def body(*refs):
        ins, gots = refs[:nb], refs[nb:2 * nb]
        ssem, rsem = refs[2 * nb:]
        x, y, c, _ = _place()
        sib = (x, y, 1 - c)
        todo = []
        for b in range(nb):
            half = ins[b].shape[1] // 2
            for q, (off, n) in enumerate(_chunks(half, NQ_D2D)):
                cp = _remote(ins[b].at[:, pl.ds((1 - c) * half + off, n), :], gots[b].at[:, pl.ds(off, n), :],
                             ssem.at[b, q], rsem.at[b, q], sib)
                cp.start()
                todo.append(cp)
        for cp in todo:
            cp.wait()

    return pl.pallas_call(
        body, name="send_sibling_half", in_specs=[ANY] * nb, out_specs=[ANY] * nb,
        out_shape=[SDS((p.shape[0], p.shape[1] // 2, p.shape[2]), p.dtype) for p in parts],
        scratch_shapes=[pltpu.SemaphoreType.DMA((nb, NQ_D2D)), pltpu.SemaphoreType.DMA((nb, NQ_D2D))],
    )(*parts)


def _scatter_to_chips(parts):
    nb = len(parts)

    def body(*refs):
        ins, outs = refs[:nb], refs[nb:2 * nb]
        ssem, rsem, lsem = refs[2 * nb:]
        x, y, c, chips = _place()
        me = (x, y, c)
        sends, locals_ = [], []
        for b in range(nb):
            for q, (off, n) in enumerate(_chunks(ins[b].shape[1], NQ_ICI)):
                rows = pl.ds(off, n)
                local = pltpu.make_async_copy(ins[b].at[2 * x + y, rows, :], outs[b].at[2 * x + y, rows, :], lsem.at[b, q])
                local.start()
                locals_.append(local)
                for j, chip in enumerate(chips):
                    cp = _remote(ins[b].at[2 * chip[0] + chip[1], rows, :], outs[b].at[2 * x + y, rows, :],
                                 ssem.at[b, j, q], rsem.at[b, j, q], (*chip, c))
                    cp.start()
                    sends.append(cp)
        for b in range(nb):
            for q, (off, n) in enumerate(_chunks(ins[b].shape[1], NQ_ICI)):
                for j, chip in enumerate(chips):
                    slot = outs[b].at[2 * chip[0] + chip[1], pl.ds(off, n), :]
                    _remote(slot, slot, ssem.at[b, j, q], rsem.at[b, j, q], me).wait_recv()
        for cp in sends:
            cp.wait_send()
        for cp in locals_:
            cp.wait()

    return pl.pallas_call(
        body, name="scatter_to_chips", in_specs=[ANY] * nb, out_specs=[ANY] * nb,
        out_shape=[SDS(p.shape, p.dtype) for p in parts],
        scratch_shapes=[pltpu.SemaphoreType.DMA((nb, 3, NQ_ICI)), pltpu.SemaphoreType.DMA((nb, 3, NQ_ICI)),
                        pltpu.SemaphoreType.DMA((nb, NQ_ICI))],
    )(*parts)


def _join_halves(fulls):
    nb = len(fulls)

    def body(*refs):
        ins, outs = refs[:nb], refs[nb:2 * nb]
        ssem, rsem = refs[2 * nb:]
        x, y, c, _ = _place()
        sib = (x, y, 1 - c)
        sends = []
        for b in range(nb):
            half = ins[b].shape[0] // 2
            for q, (off, n) in enumerate(_chunks(half, NQ_D2D)):
                mine = pl.ds(c * half + off, n)
                cp = _remote(ins[b].at[mine, :], outs[b].at[mine, :], ssem.at[b, q], rsem.at[b, q], sib)
                cp.start()
                sends.append(cp)
        for b in range(nb):
            half = ins[b].shape[0] // 2
            for q, (off, n) in enumerate(_chunks(half, NQ_D2D)):
                other = outs[b].at[pl.ds((1 - c) * half + off, n), :]
                _remote(other, other, ssem.at[b, q], rsem.at[b, q], sib).wait_recv()
        for cp in sends:
            cp.wait_send()

    return pl.pallas_call(
        body, name="join_halves", in_specs=[ANY] * nb, out_specs=[ANY] * nb,
        out_shape=[SDS(h.shape, h.dtype) for h in fulls], input_output_aliases={b: b for b in range(nb)},
        scratch_shapes=[pltpu.SemaphoreType.DMA((nb, NQ_D2D)), pltpu.SemaphoreType.DMA((nb, NQ_D2D))],
    )(*fulls)


def _gather_small(vec):
    def body(v_ref, o_ref, ssem, rsem, lsem):
        x, y, c, _ = _place()
        mine = o_ref.at[4 * x + 2 * y + c]
        local = pltpu.make_async_copy(v_ref, mine, lsem)
        local.start()
        sends = []
        for k in range(1, 8):
            peer = (x ^ (k >> 2), y ^ ((k >> 1) & 1), c ^ (k & 1))
            cp = _remote(v_ref, mine, ssem.at[k - 1], rsem.at[k - 1], peer)
            cp.start()
            sends.append(cp)
        for k in range(1, 8):
            px, py, pc = x ^ (k >> 2), y ^ ((k >> 1) & 1), c ^ (k & 1)
            slot = o_ref.at[4 * px + 2 * py + pc]
            _remote(slot, slot, ssem.at[k - 1], rsem.at[k - 1], (x, y, c)).wait_recv()
        for cp in sends:
            cp.wait_send()
        local.wait()

    return pl.pallas_call(
        body, name="gather_small", in_specs=[ANY], out_specs=ANY, out_shape=SDS((8,) + vec.shape, vec.dtype),
        scratch_shapes=[pltpu.SemaphoreType.DMA((7,)), pltpu.SemaphoreType.DMA((7,)), pltpu.SemaphoreType.DMA],
    )(vec)


def _perm(t, d):
    return t.reshape(S // d, d, t.shape[-1]).transpose(1, 0, 2).reshape(S, t.shape[-1])


def _unperm(t, d):
    return t.reshape(d, S // d, t.shape[-1]).transpose(1, 0, 2).reshape(S, t.shape[-1])


def _block_diag(pw):
    return jnp.concatenate([jnp.pad(pw[g], ((0, 0), (64 * g, 192 - 64 * g))) for g in range(4)], axis=0)


def _own_columns(full, chip):
    n = full.shape[-1] // NCH
    parts = full.reshape(full.shape[:-1] + (NCH, n))
    sel = (lax.broadcasted_iota(jnp.int32, (NCH, 1), 0) == chip)
    return jnp.sum(jnp.where(sel, parts, 0.0), axis=-2)


def _at_own_columns(shard, chip):
    n = shard.shape[-1]
    sel = (lax.broadcasted_iota(jnp.int32, (NCH * n,), 0) // n == chip)
    return jnp.where(sel, jnp.tile(shard, NCH), 0.0)


def _pad_rows(a, rows):
    return jnp.concatenate([a, jnp.zeros((rows - a.shape[0],) + a.shape[1:], a.dtype)], axis=0)


def _layer_fwd(l, x0, pos, freq, wts, blob_a, blob_b):
    sv = {"x0": x0}
    x1 = _ffn_fwd(x0, wts["ffn1_norm"][l:l + 1], blob_a, blob_b, 4 * l + 0, 4 * l + 1, 2 * l + 0)
    att, pu, dq, dz, dba = _inproj_fwd(x1, wts["mix_norm"][l:l + 1], wts["w_aug"][l], pos, freq)
    qkv3 = jnp.concatenate([_perm(att, d) for d in PATTERN_DIL], axis=0)
    o3p, l3p = _attn_fwd(qkv3)
    o3 = jnp.stack([_unperm(o3p[S * p:S * (p + 1)], d) for p, d in enumerate(PATTERN_DIL)])
    l3 = jnp.stack([_unperm(l3p[S * p:S * (p + 1)], d) for p, d in enumerate(PATTERN_DIL)])
    ya = _merge_fwd(o3, l3)
    yb = _pool_fwd(pu, wts["pool_bd"][l], wts["pool_scale"][l:l + 1])
    c = _conv_fwd(dq, wts["conv_w"][l])
    u, w, qg, kd, intra, aux = _dn_local_fwd(c, dba, wts["dn_par"][l])
    o_dn, states = _dn_rec_fwd(u, w, qg, kd, intra, aux)
    yc = _dn_post_fwd(o_dn, dz, wts["dn_out_norm"][l:l + 1])
    x2 = _outproj_fwd(x1, ya, yb, yc, blob_b, 11 + l)
    x3 = _ffn_fwd(x2, wts["ffn2_norm"][l:l + 1], blob_a, blob_b, 4 * l + 2, 4 * l + 3, 2 * l + 1)
    sv.update(x1=x1, x2=x2, qkv3=qkv3, o3=o3, l3=l3, ya=ya, yb=yb, yc=yc, pu=pu, dq=dq, dz=dz, dba=dba, c=c,
              u=u, w=w, qg=qg, kd=kd, intra=intra, aux=aux, states=states, o_dn=o_dn)
    return x3, sv


def _layer_bwd(l, dx3, sv, pos, freq, wts, blob_a, blob_b):
    gr = {}
    g2, u2, d2, dh4 = _ffn_bwd(sv["x2"], wts["ffn2_norm"][l:l + 1], blob_a, blob_b, 4 * l + 2, 4 * l + 3, 2 * l + 1, dx3)
    dx2, gr["ffn2_norm"] = _norm_bwd(sv["x2"], wts["ffn2_norm"][l:l + 1], dx3, dh4)
    gr.update(ffn2_w_gate=g2, ffn2_w_up=u2, ffn2_w_down=d2)
    dya, dyb, dyc, gr["w_out"] = _outproj_bwd(dx2, sv["ya"], sv["yb"], sv["yc"], blob_b, 11 + l)
    do_dn, ddz, gr["dn_out_norm"] = _dn_post_bwd(sv["o_dn"], sv["dz"], wts["dn_out_norm"][l:l + 1], dyc)
    du, dw, dqg, dkd, dintra, daux = _dn_rec_bwd(sv["u"], sv["w"], sv["qg"], sv["kd"], sv["intra"], sv["aux"], sv["states"], do_dn)
    dc, ddba, gr["dn_par"] = _dn_local_bwd(sv["c"], sv["dba"], wts["dn_par"][l], du, dw, dqg, dkd, dintra, daux)
    ddq, gr["conv_w"] = _conv_bwd(sv["dq"], wts["conv_w"][l], dc)
    dpu, gr["pool_bd"], gr["pool_scale"] = _pool_bwd(sv["pu"], wts["pool_bd"][l], wts["pool_scale"][l:l + 1], dyb)
    do3, dl3 = _merge_bwd(sv["o3"], sv["l3"], dya)
    do3p = jnp.concatenate([_perm(do3[p], d) for p, d in enumerate(PATTERN_DIL)], axis=0)
    dl3p = jnp.concatenate([_perm(dl3[p], d) for p, d in enumerate(PATTERN_DIL)], axis=0)
    dqp, dkp, dkc, dvp, dvc = _attn_bwd(sv["qkv3"], do3p, dl3p)
    dqkv3 = _attn_fold(dqp, dkp, dkc, dvp, dvc)
    datt3 = jnp.stack([_unperm(dqkv3[S * p:S * (p + 1)], d) for p, d in enumerate(PATTERN_DIL)])
    dx1, gr["mix_norm"], gr["w_aug"] = _inproj_bwd(sv["x1"], wts["mix_norm"][l:l + 1], wts["w_aug"][l], pos, freq, dx2,
                                                    datt3, dpu, ddq, ddz, ddba)
    g1, u1, d1, dh4 = _ffn_bwd(sv["x0"], wts["ffn1_norm"][l:l + 1], blob_a, blob_b, 4 * l + 0, 4 * l + 1, 2 * l + 0, dx1)
    dx0, gr["ffn1_norm"] = _norm_bwd(sv["x0"], wts["ffn1_norm"][l:l + 1], dx1, dh4)
    gr.update(ffn1_w_gate=g1, ffn1_w_up=u1, ffn1_w_down=d1)
    return dx0, gr


def _device_step(x, pos, target, wts, blob_a, blob_b):
    freq = jnp.tile(ROPE_THETA ** (-jnp.arange(0, EH, 2, dtype=f32) / EH), 2 * ATT // EH).reshape(1, ATT)
    saved = []
    h = x
    for l in range(DEPTH):
        h, sv = _layer_fwd(l, h, pos, freq, wts, blob_a, blob_b)
        saved.append(sv)
    dh, g_final, loss = _final(h, wts["final_norm"], target)
    grads = [None] * DEPTH
    for l in reversed(range(DEPTH)):
        dh, grads[l] = _layer_bwd(l, dh, saved[l], pos, freq, wts, blob_a, blob_b)
    return loss, dh, g_final, grads


_SMALL = (("ffn1_norm", (DEPTH, D)), ("mix_norm", (DEPTH, D)), ("pool_w", (DEPTH, 4, 64, 64)), ("pool_scale", (DEPTH, 256)),
          ("dn_conv_w", (DEPTH, 4, CW)), ("dn_a_log", (DEPTH, 4)), ("dn_dt_bias", (DEPTH, 4)), ("dn_out_norm", (DEPTH, 128)),
          ("ffn2_norm", (DEPTH, D)), ("final_norm", (D,)), ("loss", (1,)))


def _pack_small(vals):
    rows = []
    for name, shape in _SMALL:
        flat = vals[name].astype(f32).reshape(-1)
        n = -(-flat.shape[0] // 128) * 128
        rows.append(jnp.concatenate([flat, jnp.zeros((n - flat.shape[0],), f32)]).reshape(-1, 128))
    out = jnp.concatenate(rows, axis=0)
    return _pad_rows(out, -(-out.shape[0] // 16) * 16)


def _unpack_small(packed):
    vals, r = {}, 0
    for name, shape in _SMALL:
        size = int(np.prod(shape))
        n = -(-size // 128)
        vals[name] = packed[r:r + n].reshape(-1)[:size].reshape(shape)
        r += n
    return vals


def kernel(x, positions, ffn1_norm, ffn1_w_gate, ffn1_w_up, ffn1_w_down, mix_norm, w_in, pool_w, pool_scale, dn_conv_w, dn_a_log, dn_dt_bias, dn_out_norm, w_out, ffn2_norm, ffn2_w_gate, ffn2_w_up, ffn2_w_down, final_norm, loss_target, m_ffn1_norm, m_ffn1_w_gate, m_ffn1_w_up, m_ffn1_w_down, m_mix_norm, m_w_in, m_pool_w, m_pool_scale, m_dn_conv_w, m_dn_a_log, m_dn_dt_bias, m_dn_out_norm, m_w_out, m_ffn2_norm, m_ffn2_w_gate, m_ffn2_w_up, m_ffn2_w_down, m_final_norm, v_ffn1_norm, v_ffn1_w_gate, v_ffn1_w_up, v_ffn1_w_down, v_mix_norm, v_w_in, v_pool_w, v_pool_scale, v_dn_conv_w, v_dn_a_log, v_dn_dt_bias, v_dn_out_norm, v_w_out, v_ffn2_norm, v_ffn2_w_gate, v_ffn2_w_up, v_ffn2_w_down, v_final_norm):
    names = ["ffn1_norm", "ffn1_w_gate", "ffn1_w_up", "ffn1_w_down", "mix_norm", "w_in", "pool_w", "pool_scale", "dn_conv_w",
             "dn_a_log", "dn_dt_bias", "dn_out_norm", "w_out", "ffn2_norm", "ffn2_w_gate", "ffn2_w_up", "ffn2_w_down", "final_norm"]
    W = dict(zip(names, [ffn1_norm, ffn1_w_gate, ffn1_w_up, ffn1_w_down, mix_norm, w_in, pool_w, pool_scale, dn_conv_w,
                         dn_a_log, dn_dt_bias, dn_out_norm, w_out, ffn2_norm, ffn2_w_gate, ffn2_w_up, ffn2_w_down, final_norm]))
    M = dict(zip(names, [m_ffn1_norm, m_ffn1_w_gate, m_ffn1_w_up, m_ffn1_w_down, m_mix_norm, m_w_in, m_pool_w, m_pool_scale,
                         m_dn_conv_w, m_dn_a_log, m_dn_dt_bias, m_dn_out_norm, m_w_out, m_ffn2_norm, m_ffn2_w_gate, m_ffn2_w_up,
                         m_ffn2_w_down, m_final_norm]))
    V = dict(zip(names, [v_ffn1_norm, v_ffn1_w_gate, v_ffn1_w_up, v_ffn1_w_down, v_mix_norm, v_w_in, v_pool_w, v_pool_scale,
                         v_dn_conv_w, v_dn_a_log, v_dn_dt_bias, v_dn_out_norm, v_w_out, v_ffn2_norm, v_ffn2_w_gate, v_ffn2_w_up,
                         v_ffn2_w_down, v_final_norm]))
    chip = 2 * lax.axis_index("x") + lax.axis_index("y")

    def blobs_of(T, dtype):
        a = jnp.concatenate([T[n][l] for l in range(DEPTH) for n in ("ffn1_w_gate", "ffn1_w_up", "ffn2_w_gate", "ffn2_w_up")], axis=0)
        b = jnp.concatenate([T[n][l] for l in range(DEPTH) for n in ("ffn1_w_down", "ffn2_w_down")]
                            + [T["w_out"][l] for l in range(DEPTH)], axis=0)
        c = jnp.concatenate([T["w_in"][l] for l in range(DEPTH)], axis=0)
        return a.astype(dtype), b.astype(dtype), c.astype(dtype)

    sh_a, sh_b, sh_c = blobs_of(W, bf16)
    blob_a, blob_b, blob_c = _all_gather_weights([sh_a, sh_b, sh_c])
    conv_all = _gather_small(_pad_rows(dn_conv_w.reshape(DEPTH * 4 * (CW // NCH) // 128, 128), 32))
    conv_full = jnp.concatenate([conv_all[2 * j, :DEPTH * 4 * (CW // NCH) // 128].reshape(DEPTH, 4, CW // NCH) for j in range(NCH)],
                                axis=-1)

    win = blob_c.reshape(NCH, DEPTH, D, INC).transpose(1, 2, 0, 3).reshape(DEPTH, D, INW)
    w_aug = jnp.concatenate([win, jnp.zeros((DEPTH, D, INP - INW), bf16)], axis=-1)
    par = jnp.pad(jnp.stack([dn_a_log, dn_dt_bias], axis=1), ((0, 0), (0, 6), (4, 120)))
    wts = dict(ffn1_norm=ffn1_norm, mix_norm=mix_norm, ffn2_norm=ffn2_norm, final_norm=final_norm.reshape(1, D),
               w_aug=w_aug, pool_bd=jnp.stack([_block_diag(pool_w[l]) for l in range(DEPTH)]).astype(bf16),
               pool_scale=pool_scale, conv_w=jnp.concatenate([conv_full, jnp.zeros((DEPTH, 4, CW), f32)], axis=1),
               dn_par=par, dn_out_norm=dn_out_norm)

    loss, dx, g_final, grads = _device_step(x[0], positions.reshape(S, 1), loss_target[0], wts, blob_a, blob_b)

    G = {n: [grads[l][n] for l in range(DEPTH)] for n in ("ffn1_w_gate", "ffn1_w_up", "ffn2_w_gate", "ffn2_w_up",
                                                           "ffn1_w_down", "ffn2_w_down")}
    part_a = jnp.concatenate([G[n][l] for l in range(DEPTH) for n in ("ffn1_w_gate", "ffn1_w_up", "ffn2_w_gate", "ffn2_w_up")], axis=1)
    wout_parts = [grads[l]["w_out"].astype(bf16).reshape(NCH, 256, D) for l in range(DEPTH)]
    part_b = jnp.concatenate([G[n][l] for l in range(DEPTH) for n in ("ffn1_w_down", "ffn2_w_down")] + wout_parts, axis=1)
    win_parts = [grads[l]["w_aug"][:, :INW].astype(bf16).reshape(D, NCH, INC).transpose(1, 0, 2) for l in range(DEPTH)]
    part_c = jnp.concatenate(win_parts, axis=1)
    c_arr = lax.axis_index("c").astype(jnp.int32).reshape(1)
    parts = [part_a, part_b, part_c]
    gots = _send_sibling_half(parts)
    chip_sums = [_sum_core_pair(p, g, c_arr) for p, g in zip(parts, gots)]
    pieces = _scatter_to_chips(chip_sums)
    full_a, full_b, full_c = _join_halves([_sum_chips(p, c_arr) for p in pieces])

    small = {"loss": loss[0, 0:1], "final_norm": g_final.reshape(D)}
    for n in ("ffn1_norm", "mix_norm", "ffn2_norm", "pool_scale", "dn_out_norm"):
        small[n] = jnp.stack([grads[l][n].reshape(-1) for l in range(DEPTH)])
    small["pool_w"] = jnp.stack([jnp.stack([grads[l]["pool_bd"][64 * g:64 * (g + 1), 64 * g:64 * (g + 1)] for g in range(4)])
                                 for l in range(DEPTH)])
    small["dn_conv_w"] = jnp.stack([grads[l]["conv_w"][0:4] for l in range(DEPTH)])
    small["dn_a_log"] = jnp.stack([grads[l]["dn_par"][0, 4:8] for l in range(DEPTH)])
    small["dn_dt_bias"] = jnp.stack([grads[l]["dn_par"][1, 4:8] for l in range(DEPTH)])
    packed = _pack_small(small)
    g_small = _sum_pieces(_gather_small(packed), f32, "sum_small")
    gs = _unpack_small(g_small)

    where = {"ffn1_w_gate": (full_a, lambda l: 4 * l), "ffn1_w_up": (full_a, lambda l: 4 * l + 1),
             "ffn2_w_gate": (full_a, lambda l: 4 * l + 2), "ffn2_w_up": (full_a, lambda l: 4 * l + 3),
             "ffn1_w_down": (full_b, lambda l: 2 * l), "ffn2_w_down": (full_b, lambda l: 2 * l + 1),
             "w_out": (full_b, lambda l: 4 * FC // 256 + l), "w_in": (full_c, lambda l: l)}
    big_res = {n: _adamw_rows(W[n], M[n], V[n], blob, first, "adamw_" + n) for n, (blob, first) in where.items()}

    def small_of(T):
        d = {n: T[n] for n, _ in _SMALL if n not in ("loss", "dn_conv_w")}
        d["loss"] = jnp.zeros((1,), f32)
        d["dn_conv_w"] = _at_own_columns(T["dn_conv_w"], chip)
        return _pack_small(d)

    res_s = _adamw(small_of(W), g_small, small_of(M), small_of(V), "adamw_small")
    small_out = [_unpack_small(r) for r in res_s]

    def split_blobs(a, b, c):
        out = {}
        a4 = a.reshape(DEPTH, 4, D, FC)
        for k, n in enumerate(("ffn1_w_gate", "ffn1_w_up", "ffn2_w_gate", "ffn2_w_up")):
            out[n] = a4[:, k]
        b4 = b[:4 * FC].reshape(DEPTH, 2, FC, D)
        out["ffn1_w_down"], out["ffn2_w_down"] = b4[:, 0], b4[:, 1]
        out["w_out"] = b[4 * FC:].reshape(DEPTH, 256, D)
        out["w_in"] = c.reshape(DEPTH, D, INC)
        return out

    def assemble(big, sm):
        out = []
        for n in names:
            if n in big:
                out.append(big[n])
            elif n == "dn_conv_w":
                out.append(_own_columns(sm[n], chip))
            else:
                out.append(sm[n])
        return out

    grad_list = assemble(split_blobs(full_a, full_b, full_c), gs)
    outs = [gs["loss"].reshape(()), dx.reshape(1, S, D)] + grad_list
    for k in range(3):
        outs += assemble({n: r[k] for n, r in big_res.items()}, small_out[k])
    return tuple(outs)
```

```python
import functools
import math

import jax
import jax.numpy as jnp
import numpy as np
from jax import lax
from jax.experimental import pallas as pl
from jax.experimental.pallas import tpu as pltpu

f32 = jnp.float32
bf16 = jnp.bfloat16
SDS = jax.ShapeDtypeStruct
MESH = pl.DeviceIdType.MESH

S = 4096
D = 1024
DEPTH = 2
FF = 2816
NCH = 4
FC = FF // NCH
INW = 3080
INC = INW // NCH
INP = 3200
ATT = 256
EH = 64
NBLK = 128
DNW = 512
DH = 128
CH = 64
NCHUNK = S // CH
EPS = 1e-6
ROPE_THETA = 10000.0
PATTERN_DIL = (1, 4, 16)
ADAM_LR, ADAM_B1, ADAM_B2, ADAM_EPS, ADAM_WD, ADAM_STEP = 0.001, 0.9, 0.999, 1e-08, 0.01, 10
VMEM_BYTES_V7X = 64 * 1024 * 1024
NEG = -1e30

TM = 512
RB, RC = 12 * FC + 2 * 256, 2 * D


def _cp(vmem_mb=48, sem=None):
    kw = dict(vmem_limit_bytes=vmem_mb * 1024 * 1024)
    if sem is not None:
        kw["dimension_semantics"] = sem
    return pltpu.CompilerParams(**kw)


def _dot(a, b):
    return jnp.dot(a, b, preferred_element_type=f32)


def _dot_nt(a, b):
    return lax.dot_general(a, b, (((1,), (1,)), ((), ())), preferred_element_type=f32)


def _dot_tn(a, b):
    return lax.dot_general(a, b, (((0,), (0,)), ((), ())), preferred_element_type=f32)


def _rms(x, w):
    r = lax.rsqrt(jnp.mean(x * x, axis=-1, keepdims=True) + EPS)
    return x * r * w, r


def _rms_bwd(x, w, r, dh):
    xhat = x * r
    dw = jnp.sum(dh * xhat, axis=0, keepdims=True)
    dxh = dh * w
    dx = r * (dxh - xhat * jnp.mean(dxh * xhat, axis=-1, keepdims=True))
    return dx, dw


def _ffn_fwd(x, nw, blob, k0):
    kg, ku, kd = k0, k0 + 1, k0 + 2

    def body(x_ref, nw_ref, wg_ref, wu_ref, wd_ref, o_ref, h_scr, acc_scr):
        j = pl.program_id(1)

        @pl.when(j == 0)
        def _():
            h, _ = _rms(x_ref[...], nw_ref[...])
            h_scr[...] = h.astype(bf16)
            acc_scr[...] = jnp.zeros_like(acc_scr)

        h = h_scr[...]
        g = _dot_nt(h, wg_ref[0])
        u = _dot_nt(h, wu_ref[0])
        a = (g * jax.nn.sigmoid(g) * u).astype(bf16)
        acc_scr[...] += _dot(a, wd_ref[0])

        @pl.when(j == NCH - 1)
        def _():
            o_ref[...] = x_ref[...] + 0.5 * acc_scr[...]

    wspec = lambda k: pl.BlockSpec((1, FC, D), lambda i, j: (j, k, 0))
    return pl.pallas_call(
        body, grid=(S // TM, NCH), name="ffn_fwd",
        in_specs=[pl.BlockSpec((TM, D), lambda i, j: (i, 0)),
                  pl.BlockSpec((1, D), lambda i, j: (0, 0)),
                  wspec(kg), wspec(ku), wspec(kd)],
        out_specs=pl.BlockSpec((TM, D), lambda i, j: (i, 0)),
        out_shape=SDS((S, D), f32),
        scratch_shapes=[pltpu.VMEM((TM, D), bf16), pltpu.VMEM((TM, D), f32)],
        compiler_params=_cp(40),
    )(x, nw, blob, blob, blob)


def _ffn_bwd(x, nw, blob, k0, dy):
    nt = S // TM
    kg, ku, kd = k0, k0 + 1, k0 + 2

    def body(x_ref, nw_ref, wg_ref, wu_ref, wd_ref, dy_ref, dwg_ref, dwu_ref, dwd_ref, dh_ref, ag, au, ad):
        i = pl.program_id(1)

        @pl.when(i == 0)
        def _():
            ag[...] = jnp.zeros_like(ag)
            au[...] = jnp.zeros_like(au)
            ad[...] = jnp.zeros_like(ad)

        hf, _ = _rms(x_ref[...], nw_ref[...])
        h = hf.astype(bf16)
        g = _dot_nt(h, wg_ref[0])
        u = _dot_nt(h, wu_ref[0])
        sg = jax.nn.sigmoid(g)
        s = g * sg
        a = (s * u).astype(bf16)
        dyb = (0.5 * dy_ref[...]).astype(bf16)
        da = _dot_nt(dyb, wd_ref[0])
        ad[...] += _dot_tn(a, dyb)
        du = (da * s).astype(bf16)
        dg = (da * u * (sg * (1.0 + g * (1.0 - sg)))).astype(bf16)
        ag[...] += _dot_tn(dg, h)
        au[...] += _dot_tn(du, h)
        dh_ref[0] = (_dot(dg, wg_ref[0]) + _dot(du, wu_ref[0])).astype(bf16)

        @pl.when(i == nt - 1)
        def _():
            dwg_ref[0] = ag[...].astype(bf16)
            dwu_ref[0] = au[...].astype(bf16)
            dwd_ref[0] = ad[...].astype(bf16)

    wspec = lambda k: pl.BlockSpec((1, FC, D), lambda j, i: (j, k, 0))
    gspec = pl.BlockSpec((1, FC, D), lambda j, i: (j, 0, 0))
    return pl.pallas_call(
        body, grid=(NCH, nt), name="ffn_bwd",
        in_specs=[pl.BlockSpec((TM, D), lambda j, i: (i, 0)),
                  pl.BlockSpec((1, D), lambda j, i: (0, 0)),
                  wspec(kg), wspec(ku), wspec(kd),
                  pl.BlockSpec((TM, D), lambda j, i: (i, 0))],
        out_specs=[gspec, gspec, gspec, pl.BlockSpec((1, TM, D), lambda j, i: (j, i, 0))],
        out_shape=[SDS((NCH, FC, D), bf16)] * 3 + [SDS((NCH, S, D), bf16)],
        scratch_shapes=[pltpu.VMEM((FC, D), f32)] * 3,
        compiler_params=_cp(56),
    )(x, nw, blob, blob, blob, dy)


def _norm_bwd(x, nw, dres, dh4):
    nt = S // TM
    nparts = dh4.shape[0]

    def body(x_ref, nw_ref, dres_ref, dh_ref, dx_ref, dnw_ref):
        i = pl.program_id(0)
        dh = dh_ref[0].astype(f32)
        for p in range(1, nparts):
            dh = dh + dh_ref[p].astype(f32)
        xv = x_ref[...]
        _, r = _rms(xv, nw_ref[...])
        dx, dw = _rms_bwd(xv, nw_ref[...], r, dh)
        dx_ref[...] = dres_ref[...] + dx

        @pl.when(i == 0)
        def _():
            dnw_ref[...] = jnp.zeros_like(dnw_ref)

        dnw_ref[...] += dw

    return pl.pallas_call(
        body, grid=(nt,), name="norm_bwd",
        in_specs=[pl.BlockSpec((TM, D), lambda i: (i, 0)),
                  pl.BlockSpec((1, D), lambda i: (0, 0)),
                  pl.BlockSpec((TM, D), lambda i: (i, 0)),
                  pl.BlockSpec((nparts, TM, D), lambda i: (0, i, 0))],
        out_specs=[pl.BlockSpec((TM, D), lambda i: (i, 0)), pl.BlockSpec((1, D), lambda i: (0, 0))],
        out_shape=[SDS((S, D), f32), SDS((1, D), f32)],
        compiler_params=_cp(40),
    )(x, nw, dres, dh4)


def _final(x, nw, target):
    nt = S // TM

    def body(x_ref, nw_ref, t_ref, dx_ref, dnw_ref, loss_ref):
        i = pl.program_id(0)
        xv = x_ref[...]
        y, r = _rms(xv, nw_ref[...])
        err = y - t_ref[...]
        part = 0.5 * jnp.sum(jnp.mean(err * err, axis=-1, keepdims=True), axis=0, keepdims=True)
        dx, dw = _rms_bwd(xv, nw_ref[...], r, err * (1.0 / D))
        dx_ref[...] = dx

        @pl.when(i == 0)
        def _():
            dnw_ref[...] = jnp.zeros_like(dnw_ref)
            loss_ref[...] = jnp.zeros_like(loss_ref)

        dnw_ref[...] += dw
        loss_ref[...] += jnp.broadcast_to(part, loss_ref.shape)

    return pl.pallas_call(
        body, grid=(nt,), name="final_loss",
        in_specs=[pl.BlockSpec((TM, D), lambda i: (i, 0)),
                  pl.BlockSpec((1, D), lambda i: (0, 0)),
                  pl.BlockSpec((TM, D), lambda i: (i, 0))],
        out_specs=[pl.BlockSpec((TM, D), lambda i: (i, 0)), pl.BlockSpec((1, D), lambda i: (0, 0)),
                   pl.BlockSpec((1, 128), lambda i: (0, 0))],
        out_shape=[SDS((S, D), f32), SDS((1, D), f32), SDS((1, 128), f32)],
        compiler_params=_cp(40),
    )(x, nw, target)


def _rot_half(t):
    lane = lax.broadcasted_iota(jnp.int32, t.shape, 1)
    first = (lane % EH) < (EH // 2)
    return jnp.where(first, -pltpu.roll(t, ATT - EH // 2, 1), pltpu.roll(t, EH // 2, 1))


def _rope_tables(pos_ref, freq_ref):
    ang = pos_ref[...].astype(f32) * freq_ref[...]
    return jnp.cos(ang), jnp.sin(ang)


def _split_residues(val, scr, outs):
    rows, cols = val.shape
    for j in range(cols // 128):
        scr[j] = val[:, 128 * j:128 * (j + 1)]
    for ref, d in outs:
        for j in range(cols // 128):
            for r in range(d):
                ref.at[r][:, 128 * j:128 * (j + 1)] = scr.at[j][pl.ds(r, rows // d, stride=d), :]


def _join_residues(ref, d, scr):
    rows, cols = scr.shape[1], ref.shape[2]
    for j in range(cols // 128):
        for r in range(d):
            scr.at[j][pl.ds(r, rows // d, stride=d), :] = ref.at[r][:, 128 * j:128 * (j + 1)]
    return jnp.concatenate([scr[j] for j in range(cols // 128)], axis=1)


def _res_spec(d, tile, cols):
    return pl.BlockSpec((d, tile // d, cols), lambda i: (0, i, 0))


def _inproj_fwd(x, nw, w_aug, pos, freq):
    TI = 256

    def body(x_ref, nw_ref, w_hbm, pos_ref, freq_ref, att_ref, att4_ref, att16_ref, pu_ref, dq_ref, dz_ref, dba_ref,
             w_scr, r_scr):
        @pl.when(pl.program_id(0) == 0)
        def _():
            pltpu.sync_copy(w_hbm, w_scr)

        h, _ = _rms(x_ref[...], nw_ref[...])
        proj = _dot(h.astype(bf16), w_scr[...])
        cos, sin = _rope_tables(pos_ref, freq_ref)
        q = proj[:, 0:ATT]
        k = proj[:, ATT:2 * ATT]
        att = jnp.concatenate([q * cos + _rot_half(q) * sin, k * cos + _rot_half(k) * sin, proj[:, 2 * ATT:3 * ATT]], axis=1)
        att_ref[...] = att
        _split_residues(att, r_scr, [(att4_ref, 4), (att16_ref, 16)])
        pu_ref[...] = proj[:, 768:1024]
        dq_ref[...] = proj[:, 1024:2560]
        dz_ref[...] = proj[:, 2560:3072]
        dba_ref[...] = proj[:, 3072:3200]

    return pl.pallas_call(
        body, grid=(S // TI,), name="inproj_fwd",
        in_specs=[pl.BlockSpec((TI, D), lambda i: (i, 0)),
                  pl.BlockSpec((1, D), lambda i: (0, 0)),
                  pl.BlockSpec(memory_space=pl.ANY),
                  pl.BlockSpec((TI, 1), lambda i: (i, 0)),
                  pl.BlockSpec((1, ATT), lambda i: (0, 0))],
        out_specs=[pl.BlockSpec((TI, 768), lambda i: (i, 0)), _res_spec(4, TI, 768), _res_spec(16, TI, 768),
                   pl.BlockSpec((TI, 256), lambda i: (i, 0)),
                   pl.BlockSpec((TI, 1536), lambda i: (i, 0)), pl.BlockSpec((TI, 512), lambda i: (i, 0)),
                   pl.BlockSpec((TI, 128), lambda i: (i, 0))],
        out_shape=[SDS((S, 768), f32), SDS((4, S // 4, 768), f32), SDS((16, S // 16, 768), f32), SDS((S, 256), f32),
                   SDS((S, 1536), f32), SDS((S, 512), f32), SDS((S, 128), f32)],
        scratch_shapes=[pltpu.VMEM((D, INP), bf16), pltpu.VMEM((6, TI, 128), f32)],
        compiler_params=_cp(48),
    )(x, nw, w_aug, pos, freq)


def _inproj_bwd(x, nw, w_aug, pos, freq, dres, datt, datt4, datt16, dpu, ddq, ddz, ddba):
    TI = 256
    nt = S // TI

    def body(x_ref, nw_ref, w_hbm, pos_ref, freq_ref, dres_ref, datt_ref, datt4_ref, datt16_ref, dpu_ref, ddq_ref, ddz_ref,
             ddba_ref, dx_ref, dnw_ref, dw_hbm, w_scr, acc, r_scr):
        i = pl.program_id(0)

        @pl.when(i == 0)
        def _():
            pltpu.sync_copy(w_hbm, w_scr)
            acc[...] = jnp.zeros_like(acc)
            dnw_ref[...] = jnp.zeros_like(dnw_ref)

        xv = x_ref[...]
        hf, r = _rms(xv, nw_ref[...])
        h = hf.astype(bf16)
        cos, sin = _rope_tables(pos_ref, freq_ref)
        datt = datt_ref[...] + _join_residues(datt4_ref, 4, r_scr)
        datt = datt + _join_residues(datt16_ref, 16, r_scr)
        dq = datt[:, 0:ATT]
        dk = datt[:, ATT:2 * ATT]
        dq = dq * cos - _rot_half(dq) * sin
        dk = dk * cos - _rot_half(dk) * sin
        dproj = jnp.concatenate([dq, dk, datt[:, 2 * ATT:3 * ATT], dpu_ref[...], ddq_ref[...], ddz_ref[...], ddba_ref[...]],
                                axis=1).astype(bf16)
        acc[...] += _dot_tn(h, dproj)
        dh = _dot_nt(dproj, w_scr[...])
        dx, dw = _rms_bwd(xv, nw_ref[...], r, dh)
        dx_ref[...] = dres_ref[...] + dx
        dnw_ref[...] += dw

        @pl.when(i == nt - 1)
        def _():
            pltpu.sync_copy(acc, dw_hbm)

    return pl.pallas_call(
        body, grid=(nt,), name="inproj_bwd",
        in_specs=[pl.BlockSpec((TI, D), lambda i: (i, 0)),
                  pl.BlockSpec((1, D), lambda i: (0, 0)),
                  pl.BlockSpec(memory_space=pl.ANY),
                  pl.BlockSpec((TI, 1), lambda i: (i, 0)),
                  pl.BlockSpec((1, ATT), lambda i: (0, 0)),
                  pl.BlockSpec((TI, D), lambda i: (i, 0)),
                  pl.BlockSpec((TI, 768), lambda i: (i, 0)), _res_spec(4, TI, 768), _res_spec(16, TI, 768),
                  pl.BlockSpec((TI, 256), lambda i: (i, 0)),
                  pl.BlockSpec((TI, 1536), lambda i: (i, 0)),
                  pl.BlockSpec((TI, 512), lambda i: (i, 0)),
                  pl.BlockSpec((TI, 128), lambda i: (i, 0))],
        out_specs=[pl.BlockSpec((TI, D), lambda i: (i, 0)), pl.BlockSpec((1, D), lambda i: (0, 0)),
                   pl.BlockSpec(memory_space=pl.ANY)],
        out_shape=[SDS((S, D), f32), SDS((1, D), f32), SDS((D, INP), f32)],
        scratch_shapes=[pltpu.VMEM((D, INP), bf16), pltpu.VMEM((D, INP), f32), pltpu.VMEM((6, TI, 128), f32)],
        compiler_params=_cp(56),
    )(x, nw, w_aug, pos, freq, dres, datt, datt4, datt16, dpu, ddq, ddz, ddba)


def _outproj_fwd(x, ya, yb, yc, blob_b, kw):
    def body(x_ref, ya_ref, yb_ref, yc_ref, w_ref, o_ref):
        ycat = jnp.concatenate([ya_ref[...], yb_ref[...], yc_ref[...]], axis=1).astype(bf16)
        o_ref[...] = x_ref[...] + _dot(ycat, w_ref[...].reshape(D, D))

    return pl.pallas_call(
        body, grid=(S // TM,), name="outproj_fwd",
        in_specs=[pl.BlockSpec((TM, D), lambda i: (i, 0)),
                  pl.BlockSpec((TM, 256), lambda i: (i, 0)),
                  pl.BlockSpec((TM, 256), lambda i: (i, 0)),
                  pl.BlockSpec((TM, 512), lambda i: (i, 0)),
                  pl.BlockSpec((NCH, 256, D), lambda i: (0, kw, 0))],
        out_specs=pl.BlockSpec((TM, D), lambda i: (i, 0)),
        out_shape=SDS((S, D), f32),
        compiler_params=_cp(40),
    )(x, ya, yb, yc, blob_b)


def _outproj_bwd(dy, ya, yb, yc, blob_b, kw):
    nt = S // TM

    def body(dy_ref, ya_ref, yb_ref, yc_ref, w_ref, dya_ref, dyb_ref, dyc_ref, dw_ref):
        i = pl.program_id(0)

        @pl.when(i == 0)
        def _():
            dw_ref[...] = jnp.zeros_like(dw_ref)

        dyv = dy_ref[...].astype(bf16)
        ycat = jnp.concatenate([ya_ref[...], yb_ref[...], yc_ref[...]], axis=1).astype(bf16)
        dw_ref[...] += _dot_tn(ycat, dyv)
        dcat = _dot_nt(dyv, w_ref[...].reshape(D, D))
        dya_ref[...] = dcat[:, 0:256]
        dyb_ref[...] = dcat[:, 256:512]
        dyc_ref[...] = dcat[:, 512:1024]

    return pl.pallas_call(
        body, grid=(nt,), name="outproj_bwd",
        in_specs=[pl.BlockSpec((TM, D), lambda i: (i, 0)),
                  pl.BlockSpec((TM, 256), lambda i: (i, 0)),
                  pl.BlockSpec((TM, 256), lambda i: (i, 0)),
                  pl.BlockSpec((TM, 512), lambda i: (i, 0)),
                  pl.BlockSpec((NCH, 256, D), lambda i: (0, kw, 0))],
        out_specs=[pl.BlockSpec((TM, 256), lambda i: (i, 0)), pl.BlockSpec((TM, 256), lambda i: (i, 0)),
                   pl.BlockSpec((TM, 512), lambda i: (i, 0)), pl.BlockSpec((D, D), lambda i: (0, 0))],
        out_shape=[SDS((S, 256), f32), SDS((S, 256), f32), SDS((S, 512), f32), SDS((D, D), f32)],
        compiler_params=_cp(40),
    )(dy, ya, yb, yc, blob_b)


NB = S // NBLK


def _attn_block(q, kp, kc, vp, vc, first):
    kk = jnp.concatenate([kp, kc], axis=0).astype(bf16)
    vv = jnp.concatenate([vp, vc], axis=0).astype(bf16)
    qi = lax.broadcasted_iota(jnp.int32, (4 * NBLK, 2 * NBLK), 0) % NBLK
    ki = lax.broadcasted_iota(jnp.int32, (4 * NBLK, 2 * NBLK), 1)
    dist = NBLK + qi - ki
    valid = (dist >= 0) & (dist <= NBLK) & (jnp.logical_not(first) | (ki >= NBLK))
    head = lax.broadcasted_iota(jnp.int32, (1, ATT), 1) // EH
    masks = [(head == h).astype(f32) for h in range(4)]
    qs = jnp.concatenate([q * mh for mh in masks], axis=0).astype(bf16)
    s = _dot_nt(qs, kk) * (1.0 / math.sqrt(EH))
    s = jnp.where(valid, s, NEG)
    m = lax.stop_gradient(jnp.max(s, axis=-1, keepdims=True))
    p = jnp.exp(s - m)
    den = jnp.sum(p, axis=-1, keepdims=True)
    po = _dot((p / den).astype(bf16), vv)
    lse = m + jnp.log(den)
    o = jnp.zeros((NBLK, ATT), f32)
    l = jnp.zeros((NBLK, ATT), f32)
    for h, mh in enumerate(masks):
        o = o + po[NBLK * h:NBLK * (h + 1)] * mh
        l = l + lse[NBLK * h:NBLK * (h + 1)] * mh
    return o, l


def _attn_specs():
    prev = lambda b: jnp.maximum(b - 1, 0)
    return [pl.BlockSpec((NBLK, ATT), lambda b: (b, 0)),
            pl.BlockSpec((NBLK, ATT), lambda b: (prev(b), 1)),
            pl.BlockSpec((NBLK, ATT), lambda b: (b, 1)),
            pl.BlockSpec((NBLK, ATT), lambda b: (prev(b), 2)),
            pl.BlockSpec((NBLK, ATT), lambda b: (b, 2))]


def _attn_fwd(qkv, per_seq):
    def body(q_ref, kp_ref, kc_ref, vp_ref, vc_ref, o_ref, l_ref):
        first = pl.program_id(0) % per_seq == 0
        o, l = _attn_block(q_ref[...], kp_ref[...], kc_ref[...], vp_ref[...], vc_ref[...], first)
        o_ref[...] = o
        l_ref[...] = l

    blk = pl.BlockSpec((NBLK, ATT), lambda b: (b, 0))
    return pl.pallas_call(
        body, grid=(NB,), name="attn_fwd", in_specs=_attn_specs(), out_specs=[blk, blk],
        out_shape=[SDS((S, ATT), f32), SDS((S, ATT), f32)], compiler_params=_cp(32),
    )(qkv, qkv, qkv, qkv, qkv)


def _attn_bwd(qkv, do, dl, per_seq):
    def body(q_ref, kp_ref, kc_ref, vp_ref, vc_ref, do_ref, dl_ref, o_ref, k_carry, v_carry):
        step = pl.program_id(0)

        @pl.when(step == 0)
        def _():
            k_carry[...] = jnp.zeros_like(k_carry)
            v_carry[...] = jnp.zeros_like(v_carry)

        b = NB - 1 - step
        first = b % per_seq == 0
        last = b % per_seq == per_seq - 1
        fn = lambda q, kp, kc, vp, vc: _attn_block(q, kp, kc, vp, vc, first)
        _, vjp = jax.vjp(fn, q_ref[...], kp_ref[...], kc_ref[...], vp_ref[...], vc_ref[...])
        dq, dkp, dkc, dvp, dvc = vjp((do_ref[...], dl_ref[...]))
        o_ref[:, 0:ATT] = dq
        o_ref[:, ATT:2 * ATT] = dkc + jnp.where(last, 0.0, k_carry[...])
        o_ref[:, 2 * ATT:3 * ATT] = dvc + jnp.where(last, 0.0, v_carry[...])
        k_carry[...] = dkp
        v_carry[...] = dvp

    rev = lambda s: NB - 1 - s
    prev = lambda s: jnp.maximum(rev(s) - 1, 0)
    specs = [pl.BlockSpec((NBLK, ATT), lambda s: (rev(s), 0)),
             pl.BlockSpec((NBLK, ATT), lambda s: (prev(s), 1)),
             pl.BlockSpec((NBLK, ATT), lambda s: (rev(s), 1)),
             pl.BlockSpec((NBLK, ATT), lambda s: (prev(s), 2)),
             pl.BlockSpec((NBLK, ATT), lambda s: (rev(s), 2)),
             pl.BlockSpec((NBLK, ATT), lambda s: (rev(s), 0)),
             pl.BlockSpec((NBLK, ATT), lambda s: (rev(s), 0))]
    return pl.pallas_call(
        body, grid=(NB,), name="attn_bwd", in_specs=specs, out_specs=pl.BlockSpec((NBLK, 768), lambda s: (rev(s), 0)),
        out_shape=SDS((S, 768), f32), scratch_shapes=[pltpu.VMEM((NBLK, ATT), f32)] * 2, compiler_params=_cp(32),
    )(qkv, qkv, qkv, qkv, qkv, do, dl)


def _merge_weights(l0, l1, l2):
    m = jnp.maximum(jnp.maximum(l0, l1), l2)
    e0, e1, e2 = jnp.exp(l0 - m), jnp.exp(l1 - m), jnp.exp(l2 - m)
    tot = e0 + e1 + e2
    return e0 / tot, e1 / tot, e2 / tot


def _merge_specs():
    nat = pl.BlockSpec((TM, ATT), lambda i: (i, 0))
    return nat, _res_spec(4, TM, ATT), _res_spec(16, TM, ATT)


def _merge_fwd(o1, l1, o4, l4, o16, l16):
    def body(o1_ref, l1_ref, o4_ref, l4_ref, o16_ref, l16_ref, y_ref, scr):
        o4v, l4v = _join_residues(o4_ref, 4, scr), _join_residues(l4_ref, 4, scr)
        o16v, l16v = _join_residues(o16_ref, 16, scr), _join_residues(l16_ref, 16, scr)
        w0, w1, w2 = _merge_weights(l1_ref[...], l4v, l16v)
        y_ref[...] = w0 * o1_ref[...] + w1 * o4v + w2 * o16v

    nat, r4, r16 = _merge_specs()
    return pl.pallas_call(body, grid=(S // TM,), name="merge_fwd", in_specs=[nat, nat, r4, r4, r16, r16],
                          out_specs=nat, out_shape=SDS((S, ATT), f32), scratch_shapes=[pltpu.VMEM((2, TM, 128), f32)],
                          compiler_params=_cp(32))(o1, l1, o4, l4, o16, l16)


def _merge_bwd(o1, l1, o4, l4, o16, l16, dy):
    def body(o1_ref, l1_ref, o4_ref, l4_ref, o16_ref, l16_ref, dy_ref, do1_ref, dl1_ref, do4_ref, dl4_ref, do16_ref, dl16_ref, scr):
        o4v, l4v = _join_residues(o4_ref, 4, scr), _join_residues(l4_ref, 4, scr)
        o16v, l16v = _join_residues(o16_ref, 16, scr), _join_residues(l16_ref, 16, scr)
        o1v = o1_ref[...]
        w0, w1, w2 = _merge_weights(l1_ref[...], l4v, l16v)
        y = w0 * o1v + w1 * o4v + w2 * o16v
        dyv = dy_ref[...]
        do1_ref[...] = w0 * dyv
        dl1_ref[...] = w0 * (o1v - y) * dyv
        _split_residues(w1 * dyv, scr, [(do4_ref, 4)])
        _split_residues(w1 * (o4v - y) * dyv, scr, [(dl4_ref, 4)])
        _split_residues(w2 * dyv, scr, [(do16_ref, 16)])
        _split_residues(w2 * (o16v - y) * dyv, scr, [(dl16_ref, 16)])

    nat, r4, r16 = _merge_specs()
    return pl.pallas_call(body, grid=(S // TM,), name="merge_bwd", in_specs=[nat, nat, r4, r4, r16, r16, nat],
                          out_specs=[nat, nat, r4, r4, r16, r16],
                          out_shape=[SDS((S, ATT), f32)] * 2 + [SDS((4, S // 4, ATT), f32)] * 2 + [SDS((16, S // 16, ATT), f32)] * 2,
                          scratch_shapes=[pltpu.VMEM((2, TM, 128), f32)], compiler_params=_cp(32))(o1, l1, o4, l4, o16, l16, dy)


HALO = 16


def _pool_consts(i, rows):
    grp = lax.broadcasted_iota(jnp.int32, (rows, 256), 1) // 64
    t = i * TM + lax.broadcasted_iota(jnp.int32, (rows, 256), 0)
    win = jnp.where(grp == 0, 2, jnp.where(grp == 1, 4, jnp.where(grp == 2, 8, 16)))
    cnt = jnp.minimum(t + 1, win).astype(f32)
    return grp, cnt


def _pool_select(grp, s2, s4, s8, s16):
    return jnp.where(grp == 0, s2, jnp.where(grp == 1, s4, jnp.where(grp == 2, s8, s16)))


def _pooled(i, cur, halo):
    xx = jnp.concatenate([halo, cur], axis=0)
    s2 = xx + pltpu.roll(xx, 1, 0)
    s4 = s2 + pltpu.roll(s2, 2, 0)
    s8 = s4 + pltpu.roll(s4, 4, 0)
    s16 = s8 + pltpu.roll(s8, 8, 0)
    grp, cnt = _pool_consts(i, TM)
    tot = _pool_select(grp, s2[HALO:], s4[HALO:], s8[HALO:], s16[HALO:])
    return tot / cnt - cur


def _pool_fwd(u, wp, scale):
    def body(u_ref, halo_ref, wp_ref, sc_ref, y_ref):
        i = pl.program_id(0)
        halo = halo_ref[...] * (i > 0).astype(f32)
        pooled = _pooled(i, u_ref[...], halo)
        y_ref[...] = _dot(pooled.astype(bf16), wp_ref[...]) * sc_ref[...]

    return pl.pallas_call(
        body, grid=(S // TM,), name="pool_fwd",
        in_specs=[pl.BlockSpec((TM, 256), lambda i: (i, 0)),
                  pl.BlockSpec((HALO, 256), lambda i: (jnp.maximum(i * (TM // HALO) - 1, 0), 0)),
                  pl.BlockSpec((256, 256), lambda i: (0, 0)),
                  pl.BlockSpec((1, 256), lambda i: (0, 0))],
        out_specs=pl.BlockSpec((TM, 256), lambda i: (i, 0)), out_shape=SDS((S, 256), f32), compiler_params=_cp(32),
    )(u, u, wp, scale)


def _pool_bwd(u, wp, scale, dy):
    nt = S // TM

    def body(u_ref, halo_ref, wp_ref, sc_ref, dy_ref, dyn_ref, du_ref, dwp_ref, dsc_ref):
        i = pl.program_id(0)

        @pl.when(i == 0)
        def _():
            dwp_ref[...] = jnp.zeros_like(dwp_ref)
            dsc_ref[...] = jnp.zeros_like(dsc_ref)

        halo = halo_ref[...] * (i > 0).astype(f32)
        pooled = _pooled(i, u_ref[...], halo).astype(bf16)
        dyv = dy_ref[...]
        dsc_ref[...] += jnp.sum(dyv * _dot(pooled, wp_ref[...]), axis=0, keepdims=True)
        dys = (dyv * sc_ref[...]).astype(bf16)
        dwp_ref[...] += _dot_tn(pooled, dys)
        dpool = _dot_nt(dys, wp_ref[...])
        grp, cnt = _pool_consts(i, TM)
        dyn = ((dyn_ref[...] * (i < nt - 1).astype(f32)) * sc_ref[...]).astype(bf16)
        _, cntn = _pool_consts(i + 1, HALO)
        zn = _dot_nt(dyn, wp_ref[...]) / cntn
        zz = jnp.concatenate([dpool / cnt, zn], axis=0)
        n = TM + HALO
        a2 = zz + pltpu.roll(zz, n - 1, 0)
        a4 = a2 + pltpu.roll(a2, n - 2, 0)
        a8 = a4 + pltpu.roll(a4, n - 4, 0)
        a16 = a8 + pltpu.roll(a8, n - 8, 0)
        du_ref[...] = _pool_select(grp, a2[:TM], a4[:TM], a8[:TM], a16[:TM]) - dpool

    return pl.pallas_call(
        body, grid=(nt,), name="pool_bwd",
        in_specs=[pl.BlockSpec((TM, 256), lambda i: (i, 0)),
                  pl.BlockSpec((HALO, 256), lambda i: (jnp.maximum(i * (TM // HALO) - 1, 0), 0)),
                  pl.BlockSpec((256, 256), lambda i: (0, 0)),
                  pl.BlockSpec((1, 256), lambda i: (0, 0)),
                  pl.BlockSpec((TM, 256), lambda i: (i, 0)),
                  pl.BlockSpec((HALO, 256), lambda i: (jnp.minimum((i + 1) * (TM // HALO), S // HALO - 1), 0))],
        out_specs=[pl.BlockSpec((TM, 256), lambda i: (i, 0)), pl.BlockSpec((256, 256), lambda i: (0, 0)),
                   pl.BlockSpec((1, 256), lambda i: (0, 0))],
        out_shape=[SDS((S, 256), f32), SDS((256, 256), f32), SDS((1, 256), f32)], compiler_params=_cp(32),
    )(u, u, wp, scale, dy, dy)


CW = 3 * DNW
CHALO = 8
TC = 256


def _conv_fwd(u, w):
    def body(u_ref, halo_ref, w_ref, c_ref):
        i = pl.program_id(0)
        xx = jnp.concatenate([halo_ref[...] * (i > 0).astype(f32), u_ref[...]], axis=0)
        c = (w_ref[3:4, :] * xx + w_ref[2:3, :] * pltpu.roll(xx, 1, 0) + w_ref[1:2, :] * pltpu.roll(xx, 2, 0)
             + w_ref[0:1, :] * pltpu.roll(xx, 3, 0))
        c_ref[...] = c[CHALO:]

    return pl.pallas_call(
        body, grid=(S // TC,), name="conv_fwd",
        in_specs=[pl.BlockSpec((TC, CW), lambda i: (i, 0)),
                  pl.BlockSpec((CHALO, CW), lambda i: (jnp.maximum(i * (TC // CHALO) - 1, 0), 0)),
                  pl.BlockSpec((8, CW), lambda i: (0, 0))],
        out_specs=pl.BlockSpec((TC, CW), lambda i: (i, 0)), out_shape=SDS((S, CW), f32), compiler_params=_cp(32),
    )(u, u, w)


def _conv_bwd(u, w, dc):
    nt = S // TC

    def body(u_ref, halo_ref, w_ref, dc_ref, dcn_ref, du_ref, dw_ref):
        i = pl.program_id(0)

        @pl.when(i == 0)
        def _():
            dw_ref[...] = jnp.zeros_like(dw_ref)

        dcv = dc_ref[...]
        zz = jnp.concatenate([dcv, dcn_ref[...] * (i < nt - 1).astype(f32)], axis=0)
        n = TC + CHALO
        du = (w_ref[3:4, :] * zz + w_ref[2:3, :] * pltpu.roll(zz, n - 1, 0) + w_ref[1:2, :] * pltpu.roll(zz, n - 2, 0)
              + w_ref[0:1, :] * pltpu.roll(zz, n - 3, 0))
        du_ref[...] = du[:TC]
        xx = jnp.concatenate([halo_ref[...] * (i > 0).astype(f32), u_ref[...]], axis=0)
        for j in range(4):
            shifted = xx if j == 3 else pltpu.roll(xx, 3 - j, 0)
            dw_ref[j:j + 1, :] += jnp.sum(dcv * shifted[CHALO:], axis=0, keepdims=True)

    return pl.pallas_call(
        body, grid=(nt,), name="conv_bwd",
        in_specs=[pl.BlockSpec((TC, CW), lambda i: (i, 0)),
                  pl.BlockSpec((CHALO, CW), lambda i: (jnp.maximum(i * (TC // CHALO) - 1, 0), 0)),
                  pl.BlockSpec((8, CW), lambda i: (0, 0)),
                  pl.BlockSpec((TC, CW), lambda i: (i, 0)),
                  pl.BlockSpec((CHALO, CW), lambda i: (jnp.minimum((i + 1) * (TC // CHALO), S // CHALO - 1), 0))],
        out_specs=[pl.BlockSpec((TC, CW), lambda i: (i, 0)), pl.BlockSpec((8, CW), lambda i: (0, 0))],
        out_shape=[SDS((S, CW), f32), SDS((8, CW), f32)], compiler_params=_cp(32),
    )(u, u, w, dc, dc)


TL = 256
NCL = TL // CH


def _bdot(a, b):
    return jnp.einsum('nik,nkj->nij', a.astype(bf16), b.astype(bf16), preferred_element_type=f32)


def _bdot_nt(a, b):
    return jnp.einsum('nik,njk->nij', a.astype(bf16), b.astype(bf16), preferred_element_type=f32)


def _dn_local(c, dba, a_row, b_row):
    act = c * jax.nn.sigmoid(c)
    lane = lax.broadcasted_iota(jnp.int32, (TL, 128), 1)
    beta_all = jax.nn.sigmoid(dba)
    xs = dba + b_row
    softplus = jnp.maximum(xs, 0.0) + jnp.log(1.0 + jnp.exp(-jnp.abs(xs)))
    g_all = -jnp.exp(a_row) * softplus
    ii = lax.broadcasted_iota(jnp.int32, (1, CH, CH), 1)
    jj = lax.broadcasted_iota(jnp.int32, (1, CH, CH), 2)
    lower = jj <= ii
    strict = jj < ii
    eye = (ii == jj).astype(f32)
    us, ws, qgs, kds, intras = [], [], [], [], []
    aux = jnp.zeros((TL, 128), f32)
    for h in range(4):
        q = act[:, DH * h:DH * (h + 1)]
        k = act[:, DNW + DH * h:DNW + DH * (h + 1)]
        v = act[:, 2 * DNW + DH * h:2 * DNW + DH * (h + 1)]
        q = q * lax.rsqrt(jnp.sum(q * q, axis=-1, keepdims=True) + EPS) * (DH ** -0.5)
        k = k * lax.rsqrt(jnp.sum(k * k, axis=-1, keepdims=True) + EPS)
        beta = jnp.sum(jnp.where(lane == h, beta_all, 0.0), axis=1, keepdims=True)
        g = jnp.sum(jnp.where(lane == 4 + h, g_all, 0.0), axis=1, keepdims=True)
        q3, k3, v3 = q.reshape(NCL, CH, DH), k.reshape(NCL, CH, DH), v.reshape(NCL, CH, DH)
        beta3, g3 = beta.reshape(NCL, CH, 1), g.reshape(NCL, CH, 1)
        g_row = jnp.sum(eye * g3, axis=1, keepdims=True)
        gc_col = jnp.sum(jnp.where(lower, g_row, 0.0), axis=2, keepdims=True)
        gc_row = jnp.sum(jnp.where(ii <= jj, g3, 0.0), axis=1, keepdims=True)
        diff = gc_col - gc_row
        decay = jnp.where(lower, jnp.exp(jnp.where(lower, diff, 0.0)), 0.0)
        kb = k3 * beta3
        vb = v3 * beta3
        a = jnp.where(strict, _bdot_nt(kb, k3) * decay, 0.0)
        t = eye - a
        p = a
        for _ in range(5):
            p = _bdot(p, p)
            t = t + _bdot(t, p)
        u3 = _bdot(t, vb)
        w3 = _bdot(t, kb * jnp.exp(gc_col))
        intra = jnp.where(lower, _bdot_nt(q3, k3) * decay, 0.0)
        g_last = jnp.sum(g3, axis=1, keepdims=True)
        us.append(u3.reshape(TL, DH))
        ws.append(w3.reshape(TL, DH))
        qgs.append((q3 * jnp.exp(gc_col)).reshape(TL, DH))
        kds.append((k3 * jnp.exp(g_last - gc_col)).reshape(TL, DH))
        intras.append(intra.reshape(TL, CH))
        e_last = jnp.broadcast_to(jnp.exp(g_last), (NCL, CH, 1)).reshape(TL, 1)
        aux = aux + jnp.where(lane == h, e_last, 0.0)
    cat = lambda xs: jnp.concatenate(xs, axis=1)
    return cat(us), cat(ws), cat(qgs), cat(kds), jnp.stack(intras, axis=0), aux


def _dn_local_fwd(c, dba, par):
    def body(c_ref, dba_ref, par_ref, u_ref, w_ref, qg_ref, kd_ref, in_ref, aux_ref):
        u, w, qg, kd, intra, aux = _dn_local(c_ref[...], dba_ref[...], par_ref[0:1, :], par_ref[1:2, :])
        u_ref[...] = u
        w_ref[...] = w
        qg_ref[...] = qg
        kd_ref[...] = kd
        in_ref[...] = intra
        aux_ref[...] = aux

    wide = pl.BlockSpec((TL, DNW), lambda i: (i, 0))
    return pl.pallas_call(
        body, grid=(S // TL,), name="dn_local_fwd",
        in_specs=[pl.BlockSpec((TL, CW), lambda i: (i, 0)), pl.BlockSpec((TL, 128), lambda i: (i, 0)),
                  pl.BlockSpec((8, 128), lambda i: (0, 0))],
        out_specs=[wide, wide, wide, wide, pl.BlockSpec((4, TL, CH), lambda i: (0, i, 0)),
                   pl.BlockSpec((TL, 128), lambda i: (i, 0))],
        out_shape=[SDS((S, DNW), f32)] * 4 + [SDS((4, S, CH), f32), SDS((S, 128), f32)], compiler_params=_cp(48),
    )(c, dba, par)


def _dn_local_bwd(c, dba, par, du, dw, dqg, dkd, dintra, daux):
    def body(c_ref, dba_ref, par_ref, du_ref, dw_ref, dqg_ref, dkd_ref, din_ref, daux_ref, dc_ref, ddba_ref, dpar_ref):
        @pl.when(pl.program_id(0) == 0)
        def _():
            dpar_ref[...] = jnp.zeros_like(dpar_ref)

        _, vjp = jax.vjp(_dn_local, c_ref[...], dba_ref[...], par_ref[0:1, :], par_ref[1:2, :])
        dc, ddba, da_row, db_row = vjp((du_ref[...], dw_ref[...], dqg_ref[...], dkd_ref[...], din_ref[...], daux_ref[...]))
        dc_ref[...] = dc
        ddba_ref[...] = ddba
        dpar_ref[0:1, :] += da_row
        dpar_ref[1:2, :] += db_row

    wide = pl.BlockSpec((TL, DNW), lambda i: (i, 0))
    return pl.pallas_call(
        body, grid=(S // TL,), name="dn_local_bwd",
        in_specs=[pl.BlockSpec((TL, CW), lambda i: (i, 0)), pl.BlockSpec((TL, 128), lambda i: (i, 0)),
                  pl.BlockSpec((8, 128), lambda i: (0, 0)), wide, wide, wide, wide,
                  pl.BlockSpec((4, TL, CH), lambda i: (0, i, 0)), pl.BlockSpec((TL, 128), lambda i: (i, 0))],
        out_specs=[pl.BlockSpec((TL, CW), lambda i: (i, 0)), pl.BlockSpec((TL, 128), lambda i: (i, 0)),
                   pl.BlockSpec((8, 128), lambda i: (0, 0))],
        out_shape=[SDS((S, CW), f32), SDS((S, 128), f32), SDS((8, 128), f32)], compiler_params=_cp(56),
    )(c, dba, par, du, dw, dqg, dkd, dintra, daux)


def _dn_step(state, u, w, qg, kd, intra, aux):
    lane = lax.broadcasted_iota(jnp.int32, (CH, 128), 1)
    row = lax.broadcasted_iota(jnp.int32, (CH, 128), 0)
    outs, states = [], []
    for h in range(4):
        sl = slice(DH * h, DH * (h + 1))
        st = state[h]
        e = jnp.sum(jnp.sum(jnp.where((lane == h) & (row == 0), aux, 0.0), axis=1, keepdims=True), axis=0, keepdims=True)
        v_new = u[:, sl] - _dot(w[:, sl].astype(bf16), st.astype(bf16))
        vb = v_new.astype(bf16)
        outs.append(_dot(qg[:, sl].astype(bf16), st.astype(bf16)) + _dot(intra[h].astype(bf16), vb))
        states.append(st * e + _dot_tn(kd[:, sl].astype(bf16), vb))
    return jnp.concatenate(outs, axis=1), jnp.stack(states, axis=0)


def _dn_rec_fwd(u, w, qg, kd, intra, aux):
    def body(u_ref, w_ref, qg_ref, kd_ref, in_ref, aux_ref, o_ref, st_ref, st_scr):
        @pl.when(pl.program_id(0) == 0)
        def _():
            st_scr[...] = jnp.zeros_like(st_scr)

        st = st_scr[...]
        st_ref[0] = st
        o, new = _dn_step(st, u_ref[...], w_ref[...], qg_ref[...], kd_ref[...], in_ref[...], aux_ref[...])
        o_ref[...] = o
        st_scr[...] = new

    wide = pl.BlockSpec((CH, DNW), lambda n: (n, 0))
    return pl.pallas_call(
        body, grid=(NCHUNK,), name="dn_rec_fwd",
        in_specs=[wide, wide, wide, wide, pl.BlockSpec((4, CH, CH), lambda n: (0, n, 0)),
                  pl.BlockSpec((CH, 128), lambda n: (n, 0))],
        out_specs=[wide, pl.BlockSpec((1, 4, DH, DH), lambda n: (n, 0, 0, 0))],
        out_shape=[SDS((S, DNW), f32), SDS((NCHUNK, 4, DH, DH), f32)],
        scratch_shapes=[pltpu.VMEM((4, DH, DH), f32)], compiler_params=_cp(32),
    )(u, w, qg, kd, intra, aux)


def _dn_rec_bwd(u, w, qg, kd, intra, aux, states, do):
    def body(u_ref, w_ref, qg_ref, kd_ref, in_ref, aux_ref, st_ref, do_ref,
             du_ref, dw_ref, dqg_ref, dkd_ref, din_ref, daux_ref, ds_scr):
        @pl.when(pl.program_id(0) == 0)
        def _():
            ds_scr[...] = jnp.zeros_like(ds_scr)

        _, vjp = jax.vjp(_dn_step, st_ref[0], u_ref[...], w_ref[...], qg_ref[...], kd_ref[...], in_ref[...], aux_ref[...])
        dst, du, dw, dqg, dkd, din, daux = vjp((do_ref[...], ds_scr[...]))
        du_ref[...] = du
        dw_ref[...] = dw
        dqg_ref[...] = dqg
        dkd_ref[...] = dkd
        din_ref[...] = din
        daux_ref[...] = daux
        ds_scr[...] = dst

    rev = lambda n: NCHUNK - 1 - n
    wide = pl.BlockSpec((CH, DNW), lambda n: (rev(n), 0))
    inb = pl.BlockSpec((4, CH, CH), lambda n: (0, rev(n), 0))
    auxb = pl.BlockSpec((CH, 128), lambda n: (rev(n), 0))
    return pl.pallas_call(
        body, grid=(NCHUNK,), name="dn_rec_bwd",
        in_specs=[wide, wide, wide, wide, inb, auxb, pl.BlockSpec((1, 4, DH, DH), lambda n: (rev(n), 0, 0, 0)), wide],
        out_specs=[wide, wide, wide, wide, inb, auxb],
        out_shape=[SDS((S, DNW), f32)] * 4 + [SDS((4, S, CH), f32), SDS((S, 128), f32)],
        scratch_shapes=[pltpu.VMEM((4, DH, DH), f32)], compiler_params=_cp(32),
    )(u, w, qg, kd, intra, aux, states, do)


def _dn_post(o, z, nw):
    parts = []
    for h in range(4):
        sl = slice(DH * h, DH * (h + 1))
        oh = o[:, sl]
        y = oh * lax.rsqrt(jnp.mean(oh * oh, axis=-1, keepdims=True) + EPS) * nw
        zh = z[:, sl]
        parts.append(y * (zh * jax.nn.sigmoid(zh)))
    return jnp.concatenate(parts, axis=1)


def _dn_post_fwd(o, z, nw):
    def body(o_ref, z_ref, nw_ref, y_ref):
        y_ref[...] = _dn_post(o_ref[...], z_ref[...], nw_ref[...])

    wide = pl.BlockSpec((TM, DNW), lambda i: (i, 0))
    return pl.pallas_call(body, grid=(S // TM,), name="dn_post_fwd",
                          in_specs=[wide, wide, pl.BlockSpec((1, 128), lambda i: (0, 0))], out_specs=wide,
                          out_shape=SDS((S, DNW), f32), compiler_params=_cp(32))(o, z, nw)


def _dn_post_bwd(o, z, nw, dy):
    def body(o_ref, z_ref, nw_ref, dy_ref, do_ref, dz_ref, dnw_ref):
        @pl.when(pl.program_id(0) == 0)
        def _():
            dnw_ref[...] = jnp.zeros_like(dnw_ref)

        _, vjp = jax.vjp(_dn_post, o_ref[...], z_ref[...], nw_ref[...])
        do, dz, dnw = vjp(dy_ref[...])
        do_ref[...] = do
        dz_ref[...] = dz
        dnw_ref[...] += dnw

    wide = pl.BlockSpec((TM, DNW), lambda i: (i, 0))
    one = pl.BlockSpec((1, 128), lambda i: (0, 0))
    return pl.pallas_call(body, grid=(S // TM,), name="dn_post_bwd", in_specs=[wide, wide, one, wide],
                          out_specs=[wide, wide, one], out_shape=[SDS((S, DNW), f32), SDS((S, DNW), f32), SDS((1, 128), f32)],
                          compiler_params=_cp(32))(o, z, nw, dy)


def _row_tile(rows, width, itemsize=4, target=2 * 1024 * 1024):
    best = None
    for t in range(16, rows + 1, 16):
        if rows % t == 0 and t * width * itemsize <= target:
            best = t
    return best if best is not None else rows


def _sum_pieces(pieces, out_dtype, name):
    n, rows, width = pieces.shape
    tr = _row_tile(rows, width * n)

    def body(p_ref, o_ref):
        acc = p_ref[0].astype(f32)
        for s in range(1, n):
            acc = acc + p_ref[s].astype(f32)
        o_ref[...] = acc.astype(out_dtype)

    return pl.pallas_call(body, grid=(rows // tr,), name=name,
                          in_specs=[pl.BlockSpec((n, tr, width), lambda i: (0, i, 0))],
                          out_specs=pl.BlockSpec((tr, width), lambda i: (i, 0)),
                          out_shape=SDS((rows, width), out_dtype), compiler_params=_cp(32))(pieces)


def _sum_core_pair(part, got, c_arr):
    n, rows, width = part.shape
    half = rows // 2
    tr = _row_tile(half, width, itemsize=2)
    nt = half // tr

    def body(c_ref, p_ref, g_ref, o_ref):
        o_ref[...] = (p_ref[...].astype(f32) + g_ref[...].astype(f32)).astype(bf16)

    gs = pltpu.PrefetchScalarGridSpec(
        num_scalar_prefetch=1, grid=(n, nt),
        in_specs=[pl.BlockSpec((1, tr, width), lambda j, i, c: (j, c[0] * nt + i, 0)),
                  pl.BlockSpec((1, tr, width), lambda j, i, c: (j, i, 0))],
        out_specs=pl.BlockSpec((1, tr, width), lambda j, i, c: (j, i, 0)))
    return pl.pallas_call(body, grid_spec=gs, name="sum_core_pair", out_shape=SDS((n, half, width), bf16),
                          compiler_params=_cp(32))(c_arr, part, got)


def _sum_chips(pieces, c_arr):
    n, half, width = pieces.shape
    tr = _row_tile(half, width * n, itemsize=2)
    nt = half // tr

    def body(c_ref, p_ref, o_ref):
        acc = p_ref[0].astype(f32)
        for s in range(1, n):
            acc = acc + p_ref[s].astype(f32)
        o_ref[...] = acc

    gs = pltpu.PrefetchScalarGridSpec(
        num_scalar_prefetch=1, grid=(nt,),
        in_specs=[pl.BlockSpec((n, tr, width), lambda i, c: (0, i, 0))],
        out_specs=pl.BlockSpec((tr, width), lambda i, c: (c[0] * nt + i, 0)))
    return pl.pallas_call(body, grid_spec=gs, name="sum_chips", out_shape=SDS((2 * half, width), f32),
                          compiler_params=_cp(32))(c_arr, pieces)


def _adamw_math(w, g, m, v):
    mn = ADAM_B1 * m + (1.0 - ADAM_B1) * g
    vn = ADAM_B2 * v + (1.0 - ADAM_B2) * (g * g)
    m_hat = mn / (1.0 - ADAM_B1 ** ADAM_STEP)
    v_hat = vn / (1.0 - ADAM_B2 ** ADAM_STEP)
    return -ADAM_LR * (m_hat / (jnp.sqrt(v_hat) + ADAM_EPS) + ADAM_WD * w), mn, vn


def _adamw(w, g, m, v, name):
    rows, width = w.shape
    tr = _row_tile(rows, width * 7, target=12 * 1024 * 1024)

    def body(w_ref, g_ref, m_ref, v_ref, d_ref, nm_ref, nv_ref):
        d_ref[...], nm_ref[...], nv_ref[...] = _adamw_math(w_ref[...], g_ref[...], m_ref[...], v_ref[...])

    blk = pl.BlockSpec((tr, width), lambda i: (i, 0))
    return pl.pallas_call(body, grid=(rows // tr,), name=name, in_specs=[blk] * 4, out_specs=[blk] * 3,
                          out_shape=[SDS((rows, width), f32)] * 3, compiler_params=_cp(40))(w, g, m, v)


def _adamw_rows(w, m, v, gblob, first_block, name):
    layers, rows, width = w.shape
    tr = rows // 2

    def body(w_ref, g_ref, m_ref, v_ref, d_ref, nm_ref, nv_ref):
        d_ref[0], nm_ref[0], nv_ref[0] = _adamw_math(w_ref[0], g_ref[...], m_ref[0], v_ref[0])

    blk = pl.BlockSpec((1, tr, width), lambda l, i: (l, i, 0))
    gblk = pl.BlockSpec((tr, width), lambda l, i: (2 * first_block(l) + i, 0))
    return pl.pallas_call(body, grid=(layers, 2), name=name, in_specs=[blk, gblk, blk, blk], out_specs=[blk] * 3,
                          out_shape=[SDS(w.shape, f32)] * 3, compiler_params=_cp(40))(w, gblob, m, v)


ANY = pl.BlockSpec(memory_space=pl.ANY)


def _place():
    x, y, c = lax.axis_index("x"), lax.axis_index("y"), lax.axis_index("c")
    chips = [(1 - x, y), (x, 1 - y), (1 - x, 1 - y)]
    return x, y, c, chips


NQ_ICI = 4
NQ_D2D = 8


def _chunks(rows, n):
    step = rows // n
    assert step * n == rows and step % 16 == 0, (rows, n)
    return [(q * step, step) for q in range(n)]


def _remote(src, dst, ssem, rsem, dev):
    return pltpu.make_async_remote_copy(src_ref=src, dst_ref=dst, send_sem=ssem, recv_sem=rsem, device_id=dev,
                                        device_id_type=MESH)


def _all_gather_weights(shards):
    nb = len(shards)

    def body(*refs):
        ins, outs = refs[:nb], refs[nb:2 * nb]
        ssem, rsem, lsem = refs[2 * nb:]
        x, y, c, chips = _place()
        me, sib = (x, y, c), (x, y, 1 - c)
        started, locals_ = [], []
        for b in range(nb):
            half = ins[b].shape[0] // 2
            for q, (off, n) in enumerate(_chunks(half, NQ_ICI)):
                mine = pl.ds(c * half + off, n)
                own = outs[b].at[2 * x + y, mine, :]
                local = pltpu.make_async_copy(ins[b].at[mine, :], own, lsem.at[b, q])
                local.start()
                first = [_remote(ins[b].at[mine, :], own, ssem.at[b, 0, q], rsem.at[b, 0, q], sib)]
                first += [_remote(ins[b].at[mine, :], own, ssem.at[b, 1 + j, q], rsem.at[b, 1 + j, q], (*chip, c))
                          for j, chip in enumerate(chips)]
                for cp in first:
                    cp.start()
                started += first
                locals_.append(local)
        for b in range(nb):
            half = ins[b].shape[0] // 2
            for q, (off, n) in enumerate(_chunks(half, NQ_ICI)):
                mine = pl.ds(c * half + off, n)
                for j, chip in enumerate(chips):
                    landed = outs[b].at[2 * chip[0] + chip[1], mine, :]
                    _remote(landed, landed, ssem.at[b, 1 + j, q], rsem.at[b, 1 + j, q], me).wait_recv()
                    cp = _remote(landed, landed, ssem.at[b, 4 + j, q], rsem.at[b, 4 + j, q], sib)
                    cp.start()
                    started.append(cp)
        for b in range(nb):
            half = ins[b].shape[0] // 2
            for q, (off, n) in enumerate(_chunks(half, NQ_ICI)):
                other = pl.ds((1 - c) * half + off, n)
                theirs = outs[b].at[2 * x + y, other, :]
                _remote(theirs, theirs, ssem.at[b, 0, q], rsem.at[b, 0, q], me).wait_recv()
                for j, chip in enumerate(chips):
                    fwd = outs[b].at[2 * chip[0] + chip[1], other, :]
                    _remote(fwd, fwd, ssem.at[b, 4 + j, q], rsem.at[b, 4 + j, q], me).wait_recv()
        for cp in started:
            cp.wait_send()
        for cp in locals_:
            cp.wait()

    return pl.pallas_call(
        body, name="all_gather_weights", in_specs=[ANY] * nb, out_specs=[ANY] * nb,
        out_shape=[SDS((NCH,) + s.shape, s.dtype) for s in shards],
        scratch_shapes=[pltpu.SemaphoreType.DMA((nb, 7, NQ_ICI)), pltpu.SemaphoreType.DMA((nb, 7, NQ_ICI)),
                        pltpu.SemaphoreType.DMA((nb, NQ_ICI))],
    )(*shards)


def _send_sibling_half(parts):
    nb = len(parts)

    def body(*refs):
        ins, gots = refs[:nb], refs[nb:2 * nb]
        ssem, rsem = refs[2 * nb:]
        x, y, c, _ = _place()
        sib = (x, y, 1 - c)
        todo = []
        for b in range(nb):
            half = ins[b].shape[1] // 2
            for q, (off, n) in enumerate(_chunks(half, NQ_D2D)):
                cp = _remote(ins[b].at[:, pl.ds((1 - c) * half + off, n), :], gots[b].at[:, pl.ds(off, n), :],
                             ssem.at[b, q], rsem.at[b, q], sib)
                cp.start()
                todo.append(cp)
        for cp in todo:
            cp.wait()

    return pl.pallas_call(
        body, name="send_sibling_half", in_specs=[ANY] * nb, out_specs=[ANY] * nb,
        out_shape=[SDS((p.shape[0], p.shape[1] // 2, p.shape[2]), p.dtype) for p in parts],
        scratch_shapes=[pltpu.SemaphoreType.DMA((nb, NQ_D2D)), pltpu.SemaphoreType.DMA((nb, NQ_D2D))],
    )(*parts)


def _scatter_to_chips(parts):
    nb = len(parts)

    def body(*refs):
        ins, outs = refs[:nb], refs[nb:2 * nb]
        ssem, rsem, lsem = refs[2 * nb:]
        x, y, c, chips = _place()
        me = (x, y, c)
        sends, locals_ = [], []
        for b in range(nb):
            for q, (off, n) in enumerate(_chunks(ins[b].shape[1], NQ_ICI)):
                rows = pl.ds(off, n)
                local = pltpu.make_async_copy(ins[b].at[2 * x + y, rows, :], outs[b].at[2 * x + y, rows, :], lsem.at[b, q])
                local.start()
                locals_.append(local)
                for j, chip in enumerate(chips):
                    cp = _remote(ins[b].at[2 * chip[0] + chip[1], rows, :], outs[b].at[2 * x + y, rows, :],
                                 ssem.at[b, j, q], rsem.at[b, j, q], (*chip, c))
                    cp.start()
                    sends.append(cp)
        for b in range(nb):
            for q, (off, n) in enumerate(_chunks(ins[b].shape[1], NQ_ICI)):
                for j, chip in enumerate(chips):
                    slot = outs[b].at[2 * chip[0] + chip[1], pl.ds(off, n), :]
                    _remote(slot, slot, ssem.at[b, j, q], rsem.at[b, j, q], me).wait_recv()
        for cp in sends:
            cp.wait_send()
        for cp in locals_:
            cp.wait()

    return pl.pallas_call(
        body, name="scatter_to_chips", in_specs=[ANY] * nb, out_specs=[ANY] * nb,
        out_shape=[SDS(p.shape, p.dtype) for p in parts],
        scratch_shapes=[pltpu.SemaphoreType.DMA((nb, 3, NQ_ICI)), pltpu.SemaphoreType.DMA((nb, 3, NQ_ICI)),
                        pltpu.SemaphoreType.DMA((nb, NQ_ICI))],
    )(*parts)


def _join_halves(fulls):
    nb = len(fulls)

    def body(*refs):
        ins, outs = refs[:nb], refs[nb:2 * nb]
        ssem, rsem = refs[2 * nb:]
        x, y, c, _ = _place()
        sib = (x, y, 1 - c)
        sends = []
        for b in range(nb):
            half = ins[b].shape[0] // 2
            for q, (off, n) in enumerate(_chunks(half, NQ_D2D)):
                mine = pl.ds(c * half + off, n)
                cp = _remote(ins[b].at[mine, :], outs[b].at[mine, :], ssem.at[b, q], rsem.at[b, q], sib)
                cp.start()
                sends.append(cp)
        for b in range(nb):
            half = ins[b].shape[0] // 2
            for q, (off, n) in enumerate(_chunks(half, NQ_D2D)):
                other = outs[b].at[pl.ds((1 - c) * half + off, n), :]
                _remote(other, other, ssem.at[b, q], rsem.at[b, q], sib).wait_recv()
        for cp in sends:
            cp.wait_send()

    return pl.pallas_call(
        body, name="join_halves", in_specs=[ANY] * nb, out_specs=[ANY] * nb,
        out_shape=[SDS(h.shape, h.dtype) for h in fulls], input_output_aliases={b: b for b in range(nb)},
        scratch_shapes=[pltpu.SemaphoreType.DMA((nb, NQ_D2D)), pltpu.SemaphoreType.DMA((nb, NQ_D2D))],
    )(*fulls)


def _gather_small(vec):
    def body(v_ref, o_ref, ssem, rsem, lsem):
        x, y, c, _ = _place()
        mine = o_ref.at[4 * x + 2 * y + c]
        local = pltpu.make_async_copy(v_ref, mine, lsem)
        local.start()
        sends = []
        for k in range(1, 8):
            peer = (x ^ (k >> 2), y ^ ((k >> 1) & 1), c ^ (k & 1))
            cp = _remote(v_ref, mine, ssem.at[k - 1], rsem.at[k - 1], peer)
            cp.start()
            sends.append(cp)
        for k in range(1, 8):
            px, py, pc = x ^ (k >> 2), y ^ ((k >> 1) & 1), c ^ (k & 1)
            slot = o_ref.at[4 * px + 2 * py + pc]
            _remote(slot, slot, ssem.at[k - 1], rsem.at[k - 1], (x, y, c)).wait_recv()
        for cp in sends:
            cp.wait_send()
        local.wait()

    return pl.pallas_call(
        body, name="gather_small", in_specs=[ANY], out_specs=ANY, out_shape=SDS((8,) + vec.shape, vec.dtype),
        scratch_shapes=[pltpu.SemaphoreType.DMA((7,)), pltpu.SemaphoreType.DMA((7,)), pltpu.SemaphoreType.DMA],
    )(vec)


def _block_diag(pw):
    return jnp.concatenate([jnp.pad(pw[g], ((0, 0), (64 * g, 192 - 64 * g))) for g in range(4)], axis=0)


def _own_columns(full, chip):
    n = full.shape[-1] // NCH
    parts = full.reshape(full.shape[:-1] + (NCH, n))
    sel = (lax.broadcasted_iota(jnp.int32, (NCH, 1), 0) == chip)
    return jnp.sum(jnp.where(sel, parts, 0.0), axis=-2)


def _at_own_columns(shard, chip):
    n = shard.shape[-1]
    sel = (lax.broadcasted_iota(jnp.int32, (NCH * n,), 0) // n == chip)
    return jnp.where(sel, jnp.tile(shard, NCH), 0.0)


def _pad_rows(a, rows):
    return jnp.concatenate([a, jnp.zeros((rows - a.shape[0],) + a.shape[1:], a.dtype)], axis=0)


def _ffn_block(l, which):
    return 6 * l + 3 * which


def _wout_block(l):
    return 12 * FC // 256 + l


def _layer_fwd(l, x0, pos, freq, wts, blob):
    sv = {"x0": x0}
    x1 = _ffn_fwd(x0, wts["ffn1_norm"][l:l + 1], blob, _ffn_block(l, 0))
    att, att4, att16, pu, dq, dz, dba = _inproj_fwd(x1, wts["mix_norm"][l:l + 1], wts["w_aug"][l], pos, freq)
    qkvs = [att, att4.reshape(S, 768), att16.reshape(S, 768)]
    (o1, l1), (o4, l4), (o16, l16) = [_attn_fwd(q, NB // d) for q, d in zip(qkvs, PATTERN_DIL)]
    ols = (o1, l1, o4.reshape(4, S // 4, ATT), l4.reshape(4, S // 4, ATT), o16.reshape(16, S // 16, ATT),
           l16.reshape(16, S // 16, ATT))
    ya = _merge_fwd(*ols)
    yb = _pool_fwd(pu, wts["pool_bd"][l], wts["pool_scale"][l:l + 1])
    c = _conv_fwd(dq, wts["conv_w"][l])
    u, w, qg, kd, intra, aux = _dn_local_fwd(c, dba, wts["dn_par"][l])
    o_dn, states = _dn_rec_fwd(u, w, qg, kd, intra, aux)
    yc = _dn_post_fwd(o_dn, dz, wts["dn_out_norm"][l:l + 1])
    x2 = _outproj_fwd(x1, ya, yb, yc, blob, _wout_block(l))
    x3 = _ffn_fwd(x2, wts["ffn2_norm"][l:l + 1], blob, _ffn_block(l, 1))
    sv.update(x1=x1, x2=x2, qkvs=qkvs, ols=ols, ya=ya, yb=yb, yc=yc, pu=pu, dq=dq, dz=dz, dba=dba, c=c,
              u=u, w=w, qg=qg, kd=kd, intra=intra, aux=aux, states=states, o_dn=o_dn)
    return x3, sv


def _layer_bwd(l, dx3, sv, pos, freq, wts, blob):
    gr = {}
    g2, u2, d2, dh4 = _ffn_bwd(sv["x2"], wts["ffn2_norm"][l:l + 1], blob, _ffn_block(l, 1), dx3)
    dx2, gr["ffn2_norm"] = _norm_bwd(sv["x2"], wts["ffn2_norm"][l:l + 1], dx3, dh4)
    gr.update(ffn2_w_gate=g2, ffn2_w_up=u2, ffn2_w_down=d2)
    dya, dyb, dyc, gr["w_out"] = _outproj_bwd(dx2, sv["ya"], sv["yb"], sv["yc"], blob, _wout_block(l))
    do_dn, ddz, gr["dn_out_norm"] = _dn_post_bwd(sv["o_dn"], sv["dz"], wts["dn_out_norm"][l:l + 1], dyc)
    du, dw, dqg, dkd, dintra, daux = _dn_rec_bwd(sv["u"], sv["w"], sv["qg"], sv["kd"], sv["intra"], sv["aux"], sv["states"], do_dn)
    dc, ddba, gr["dn_par"] = _dn_local_bwd(sv["c"], sv["dba"], wts["dn_par"][l], du, dw, dqg, dkd, dintra, daux)
    ddq, gr["conv_w"] = _conv_bwd(sv["dq"], wts["conv_w"][l], dc)
    dpu, gr["pool_bd"], gr["pool_scale"] = _pool_bwd(sv["pu"], wts["pool_bd"][l], wts["pool_scale"][l:l + 1], dyb)
    dols = _merge_bwd(*sv["ols"], dya)
    datts = [_attn_bwd(q, dols[2 * p].reshape(S, ATT), dols[2 * p + 1].reshape(S, ATT), NB // d)
             for p, (q, d) in enumerate(zip(sv["qkvs"], PATTERN_DIL))]
    dx1, gr["mix_norm"], gr["w_aug"] = _inproj_bwd(sv["x1"], wts["mix_norm"][l:l + 1], wts["w_aug"][l], pos, freq, dx2,
                                                    datts[0], datts[1].reshape(4, S // 4, 768),
                                                    datts[2].reshape(16, S // 16, 768), dpu, ddq, ddz, ddba)
    g1, u1, d1, dh4 = _ffn_bwd(sv["x0"], wts["ffn1_norm"][l:l + 1], blob, _ffn_block(l, 0), dx1)
    dx0, gr["ffn1_norm"] = _norm_bwd(sv["x0"], wts["ffn1_norm"][l:l + 1], dx1, dh4)
    gr.update(ffn1_w_gate=g1, ffn1_w_up=u1, ffn1_w_down=d1)
    return dx0, gr


def _device_step(x, pos, target, wts, blob):
    freq = jnp.tile(ROPE_THETA ** (-jnp.arange(0, EH, 2, dtype=f32) / EH), 2 * ATT // EH).reshape(1, ATT)
    saved = []
    h = x
    for l in range(DEPTH):
        h, sv = _layer_fwd(l, h, pos, freq, wts, blob)
        saved.append(sv)
    dh, g_final, loss = _final(h, wts["final_norm"], target)
    grads = [None] * DEPTH
    for l in reversed(range(DEPTH)):
        dh, grads[l] = _layer_bwd(l, dh, saved[l], pos, freq, wts, blob)
    return loss, dh, g_final, grads


_SMALL = (("ffn1_norm", (DEPTH, D)), ("mix_norm", (DEPTH, D)), ("pool_w", (DEPTH, 4, 64, 64)), ("pool_scale", (DEPTH, 256)),
          ("dn_conv_w", (DEPTH, 4, CW)), ("dn_a_log", (DEPTH, 4)), ("dn_dt_bias", (DEPTH, 4)), ("dn_out_norm", (DEPTH, 128)),
          ("ffn2_norm", (DEPTH, D)), ("final_norm", (D,)), ("loss", (1,)))


def _pack_small(vals):
    rows = []
    for name, shape in _SMALL:
        flat = vals[name].astype(f32).reshape(-1)
        n = -(-flat.shape[0] // 128) * 128
        rows.append(jnp.concatenate([flat, jnp.zeros((n - flat.shape[0],), f32)]).reshape(-1, 128))
    out = jnp.concatenate(rows, axis=0)
    return _pad_rows(out, -(-out.shape[0] // 16) * 16)


def _unpack_small(packed):
    vals, r = {}, 0
    for name, shape in _SMALL:
        size = int(np.prod(shape))
        n = -(-size // 128)
        vals[name] = packed[r:r + n].reshape(-1)[:size].reshape(shape)
        r += n
    return vals


def kernel(x, positions, ffn1_norm, ffn1_w_gate, ffn1_w_up, ffn1_w_down, mix_norm, w_in, pool_w, pool_scale, dn_conv_w, dn_a_log, dn_dt_bias, dn_out_norm, w_out, ffn2_norm, ffn2_w_gate, ffn2_w_up, ffn2_w_down, final_norm, loss_target, m_ffn1_norm, m_ffn1_w_gate, m_ffn1_w_up, m_ffn1_w_down, m_mix_norm, m_w_in, m_pool_w, m_pool_scale, m_dn_conv_w, m_dn_a_log, m_dn_dt_bias, m_dn_out_norm, m_w_out, m_ffn2_norm, m_ffn2_w_gate, m_ffn2_w_up, m_ffn2_w_down, m_final_norm, v_ffn1_norm, v_ffn1_w_gate, v_ffn1_w_up, v_ffn1_w_down, v_mix_norm, v_w_in, v_pool_w, v_pool_scale, v_dn_conv_w, v_dn_a_log, v_dn_dt_bias, v_dn_out_norm, v_w_out, v_ffn2_norm, v_ffn2_w_gate, v_ffn2_w_up, v_ffn2_w_down, v_final_norm):
    names = ["ffn1_norm", "ffn1_w_gate", "ffn1_w_up", "ffn1_w_down", "mix_norm", "w_in", "pool_w", "pool_scale", "dn_conv_w",
             "dn_a_log", "dn_dt_bias", "dn_out_norm", "w_out", "ffn2_norm", "ffn2_w_gate", "ffn2_w_up", "ffn2_w_down", "final_norm"]
    W = dict(zip(names, [ffn1_norm, ffn1_w_gate, ffn1_w_up, ffn1_w_down, mix_norm, w_in, pool_w, pool_scale, dn_conv_w,
                         dn_a_log, dn_dt_bias, dn_out_norm, w_out, ffn2_norm, ffn2_w_gate, ffn2_w_up, ffn2_w_down, final_norm]))
    M = dict(zip(names, [m_ffn1_norm, m_ffn1_w_gate, m_ffn1_w_up, m_ffn1_w_down, m_mix_norm, m_w_in, m_pool_w, m_pool_scale,
                         m_dn_conv_w, m_dn_a_log, m_dn_dt_bias, m_dn_out_norm, m_w_out, m_ffn2_norm, m_ffn2_w_gate, m_ffn2_w_up,
                         m_ffn2_w_down, m_final_norm]))
    V = dict(zip(names, [v_ffn1_norm, v_ffn1_w_gate, v_ffn1_w_up, v_ffn1_w_down, v_mix_norm, v_w_in, v_pool_w, v_pool_scale,
                         v_dn_conv_w, v_dn_a_log, v_dn_dt_bias, v_dn_out_norm, v_w_out, v_ffn2_norm, v_ffn2_w_gate, v_ffn2_w_up,
                         v_ffn2_w_down, v_final_norm]))
    chip = 2 * lax.axis_index("x") + lax.axis_index("y")

    ffn_names = [(f"ffn{f}_w_gate", f"ffn{f}_w_up", f"ffn{f}_w_down") for f in (1, 2)]
    tr = lambda t: jnp.swapaxes(t, -1, -2)
    rows = []
    for l in range(DEPTH):
        for g, u, dn in ffn_names:
            rows += [tr(W[g][l]), tr(W[u][l]), W[dn][l]]
    sh_b = jnp.concatenate(rows + [W["w_out"][l] for l in range(DEPTH)], axis=0).astype(bf16)
    sh_c = jnp.concatenate([W["w_in"][l] for l in range(DEPTH)], axis=0).astype(bf16)
    blob, blob_c = _all_gather_weights([sh_b, sh_c])
    conv_all = _gather_small(_pad_rows(dn_conv_w.reshape(DEPTH * 4 * (CW // NCH) // 128, 128), 32))
    conv_full = jnp.concatenate([conv_all[2 * j, :DEPTH * 4 * (CW // NCH) // 128].reshape(DEPTH, 4, CW // NCH) for j in range(NCH)],
                                axis=-1)

    win = blob_c.reshape(NCH, DEPTH, D, INC).transpose(1, 2, 0, 3).reshape(DEPTH, D, INW)
    w_aug = jnp.concatenate([win, jnp.zeros((DEPTH, D, INP - INW), bf16)], axis=-1)
    par = jnp.pad(jnp.stack([dn_a_log, dn_dt_bias], axis=1), ((0, 0), (0, 6), (4, 120)))
    wts = dict(ffn1_norm=ffn1_norm, mix_norm=mix_norm, ffn2_norm=ffn2_norm, final_norm=final_norm.reshape(1, D),
               w_aug=w_aug, pool_bd=jnp.stack([_block_diag(pool_w[l]) for l in range(DEPTH)]).astype(bf16),
               pool_scale=pool_scale, conv_w=jnp.pad(conv_full, ((0, 0), (0, 4), (0, 0))),
               dn_par=par, dn_out_norm=dn_out_norm)

    loss, dx, g_final, grads = _device_step(x[0], positions.reshape(S, 1), loss_target[0], wts, blob)

    ffn_parts = [grads[l][n] for l in range(DEPTH) for names3 in ffn_names for n in names3]
    wout_parts = [grads[l]["w_out"].astype(bf16).reshape(NCH, 256, D) for l in range(DEPTH)]
    part_b = jnp.concatenate(ffn_parts + wout_parts, axis=1)
    win_parts = [grads[l]["w_aug"][:, :INW].astype(bf16).reshape(D, NCH, INC).transpose(1, 0, 2) for l in range(DEPTH)]
    part_c = jnp.concatenate(win_parts, axis=1)
    c_arr = lax.axis_index("c").astype(jnp.int32).reshape(1)
    parts = [part_b, part_c]
    gots = _send_sibling_half(parts)
    chip_sums = [_sum_core_pair(p, g, c_arr) for p, g in zip(parts, gots)]
    pieces = _scatter_to_chips(chip_sums)
    full_b, full_c = _join_halves([_sum_chips(p, c_arr) for p in pieces])

    small = {"loss": loss[0, 0:1], "final_norm": g_final.reshape(D)}
    for n in ("ffn1_norm", "mix_norm", "ffn2_norm", "pool_scale", "dn_out_norm"):
        small[n] = jnp.stack([grads[l][n].reshape(-1) for l in range(DEPTH)])
    small["pool_w"] = jnp.stack([jnp.stack([grads[l]["pool_bd"][64 * g:64 * (g + 1), 64 * g:64 * (g + 1)] for g in range(4)])
                                 for l in range(DEPTH)])
    small["dn_conv_w"] = jnp.stack([grads[l]["conv_w"][0:4] for l in range(DEPTH)])
    small["dn_a_log"] = jnp.stack([grads[l]["dn_par"][0, 4:8] for l in range(DEPTH)])
    small["dn_dt_bias"] = jnp.stack([grads[l]["dn_par"][1, 4:8] for l in range(DEPTH)])
    packed = _pack_small(small)
    g_small = _sum_pieces(_gather_small(packed), f32, "sum_small")
    gs = _unpack_small(g_small)

    transposed = ("ffn1_w_gate", "ffn1_w_up", "ffn2_w_gate", "ffn2_w_up")
    where = {"ffn1_w_gate": (full_b, lambda l: 6 * l), "ffn1_w_up": (full_b, lambda l: 6 * l + 1),
             "ffn1_w_down": (full_b, lambda l: 6 * l + 2), "ffn2_w_gate": (full_b, lambda l: 6 * l + 3),
             "ffn2_w_up": (full_b, lambda l: 6 * l + 4), "ffn2_w_down": (full_b, lambda l: 6 * l + 5),
             "w_out": (full_b, _wout_block), "w_in": (full_c, lambda l: l)}
    big_res = {}
    for n, (gblob, first) in where.items():
        t = tr if n in transposed else (lambda a: a)
        big_res[n] = [t(r) for r in _adamw_rows(t(W[n]), t(M[n]), t(V[n]), gblob, first, "adamw_" + n)]

    def small_of(T):
        d = {n: T[n] for n, _ in _SMALL if n not in ("loss", "dn_conv_w")}
        d["loss"] = jnp.zeros((1,), f32)
        d["dn_conv_w"] = _at_own_columns(T["dn_conv_w"], chip)
        return _pack_small(d)

    res_s = _adamw(small_of(W), g_small, small_of(M), small_of(V), "adamw_small")
    small_out = [_unpack_small(r) for r in res_s]

    def split_blobs(b, c):
        out = {}
        b6 = b[:12 * FC].reshape(DEPTH, 6, FC, D)
        for k, n in enumerate(n for names3 in ffn_names for n in names3):
            out[n] = tr(b6[:, k]) if n in transposed else b6[:, k]
        out["w_out"] = b[12 * FC:].reshape(DEPTH, 256, D)
        out["w_in"] = c.reshape(DEPTH, D, INC)
        return out

    def assemble(big, sm):
        out = []
        for n in names:
            if n in big:
                out.append(big[n])
            elif n == "dn_conv_w":
                out.append(_own_columns(sm[n], chip))
            else:
                out.append(sm[n])
        return out

    grad_list = assemble(split_blobs(full_b, full_c), gs)
    outs = [gs["loss"].reshape(()), dx.reshape(1, S, D)] + grad_list
    for k in range(3):
        outs += assemble({n: r[k] for n, r in big_res.items()}, small_out[k])
    return tuple(outs)
```

```python
import functools
import math

import jax
import jax.numpy as jnp
import numpy as np
from jax import lax
from jax.experimental import pallas as pl
from jax.experimental.pallas import tpu as pltpu

f32 = jnp.float32
bf16 = jnp.bfloat16
SDS = jax.ShapeDtypeStruct
MESH = pl.DeviceIdType.MESH

S = 4096
D = 1024
DEPTH = 2
FF = 2816
NCH = 4
FC = FF // NCH
INW = 3080
INC = INW // NCH
INP = 3200
ATT = 256
EH = 64
NBLK = 128
DNW = 512
DH = 128
CH = 64
NCHUNK = S // CH
EPS = 1e-6
ROPE_THETA = 10000.0
PATTERN_DIL = (1, 4, 16)
ADAM_LR, ADAM_B1, ADAM_B2, ADAM_EPS, ADAM_WD, ADAM_STEP = 0.001, 0.9, 0.999, 1e-08, 0.01, 10
VMEM_BYTES_V7X = 64 * 1024 * 1024
NEG = -1e30

TM = 512
RB, RC = 14 * FC, 2 * D


def _cp(vmem_mb=48, sem=None):
    kw = dict(vmem_limit_bytes=vmem_mb * 1024 * 1024)
    if sem is not None:
        kw["dimension_semantics"] = sem
    return pltpu.CompilerParams(**kw)


def _dot(a, b):
    return jnp.dot(a, b, preferred_element_type=f32)


def _dot_nt(a, b):
    return lax.dot_general(a, b, (((1,), (1,)), ((), ())), preferred_element_type=f32)


def _dot_tn(a, b):
    return lax.dot_general(a, b, (((0,), (0,)), ((), ())), preferred_element_type=f32)


def _rms(x, w):
    r = lax.rsqrt(jnp.mean(x * x, axis=-1, keepdims=True) + EPS)
    return x * r * w, r


def _rms_bwd(x, w, r, dh):
    xhat = x * r
    dw = jnp.sum(dh * xhat, axis=0, keepdims=True)
    dxh = dh * w
    dx = r * (dxh - xhat * jnp.mean(dxh * xhat, axis=-1, keepdims=True))
    return dx, dw


def _ffn_fwd(x, nw, blob, k0):
    kg, ku, kd = k0, k0 + 1, k0 + 2

    def body(x_ref, nw_ref, wg_ref, wu_ref, wd_ref, o_ref, h_scr, acc_scr):
        j = pl.program_id(1)

        @pl.when(j == 0)
        def _():
            h, _ = _rms(x_ref[...], nw_ref[...])
            h_scr[...] = h.astype(bf16)
            acc_scr[...] = jnp.zeros_like(acc_scr)

        h = h_scr[...]
        g = _dot_nt(h, wg_ref[0])
        u = _dot_nt(h, wu_ref[0])
        a = (g * jax.nn.sigmoid(g) * u).astype(bf16)
        acc_scr[...] += _dot(a, wd_ref[0])

        @pl.when(j == NCH - 1)
        def _():
            o_ref[...] = x_ref[...] + 0.5 * acc_scr[...]

    wspec = lambda k: pl.BlockSpec((1, FC, D), lambda i, j: (j, k, 0))
    return pl.pallas_call(
        body, grid=(S // TM, NCH), name="ffn_fwd",
        in_specs=[pl.BlockSpec((TM, D), lambda i, j: (i, 0)),
                  pl.BlockSpec((1, D), lambda i, j: (0, 0)),
                  wspec(kg), wspec(ku), wspec(kd)],
        out_specs=pl.BlockSpec((TM, D), lambda i, j: (i, 0)),
        out_shape=SDS((S, D), f32),
        scratch_shapes=[pltpu.VMEM((TM, D), bf16), pltpu.VMEM((TM, D), f32)],
        compiler_params=_cp(40),
    )(x, nw, blob, blob, blob)


def _ffn_bwd(x, nw, blob, k0, dy, ride=None):
    nt = S // TM
    kg, ku, kd = k0, k0 + 1, k0 + 2
    nr = 0 if ride is None else len(ride)

    def body(*refs):
        x_ref, nw_ref, wg_ref, wu_ref, wd_ref, dy_ref = refs[:6]
        ride_in = refs[6:6 + nr]
        dwg_ref, dwu_ref, dwd_ref, dh_ref = refs[6 + nr:10 + nr]
        ride_out = refs[10 + nr:10 + 2 * nr]
        ag, au, ad = refs[10 + 2 * nr:13 + 2 * nr]
        sems = refs[13 + 2 * nr:]
        j = pl.program_id(0)
        i = pl.program_id(1)

        if nr:
            @pl.when(jnp.logical_and(j == 0, i == 0))
            def _():
                _scatter_start(ride_in, ride_out, *sems)

        @pl.when(i == 0)
        def _():
            ag[...] = jnp.zeros_like(ag)
            au[...] = jnp.zeros_like(au)
            ad[...] = jnp.zeros_like(ad)

        hf, _ = _rms(x_ref[...], nw_ref[...])
        h = hf.astype(bf16)
        g = _dot_nt(h, wg_ref[0])
        u = _dot_nt(h, wu_ref[0])
        sg = jax.nn.sigmoid(g)
        s = g * sg
        a = (s * u).astype(bf16)
        dyb = (0.5 * dy_ref[...]).astype(bf16)
        da = _dot_nt(dyb, wd_ref[0])
        ad[...] += _dot_tn(a, dyb)
        du = (da * s).astype(bf16)
        dg = (da * u * (sg * (1.0 + g * (1.0 - sg)))).astype(bf16)
        ag[...] += _dot_tn(dg, h)
        au[...] += _dot_tn(du, h)
        dh_ref[0] = (_dot(dg, wg_ref[0]) + _dot(du, wu_ref[0])).astype(bf16)

        @pl.when(i == nt - 1)
        def _():
            dwg_ref[0] = ag[...].astype(bf16)
            dwu_ref[0] = au[...].astype(bf16)
            dwd_ref[0] = ad[...].astype(bf16)

        if nr:
            @pl.when(jnp.logical_and(j == NCH - 1, i == nt - 1))
            def _():
                _scatter_finish(ride_in, ride_out, *sems)

    wspec = lambda k: pl.BlockSpec((1, FC, D), lambda j, i: (j, k, 0))
    gspec = pl.BlockSpec((1, FC, D), lambda j, i: (j, 0, 0))
    rides = [] if ride is None else list(ride)
    return pl.pallas_call(
        body, grid=(NCH, nt), name="ffn_bwd_ride" if nr else "ffn_bwd",
        in_specs=[pl.BlockSpec((TM, D), lambda j, i: (i, 0)),
                  pl.BlockSpec((1, D), lambda j, i: (0, 0)),
                  wspec(kg), wspec(ku), wspec(kd),
                  pl.BlockSpec((TM, D), lambda j, i: (i, 0))] + [ANY] * nr,
        out_specs=[gspec, gspec, gspec, pl.BlockSpec((1, TM, D), lambda j, i: (j, i, 0))] + [ANY] * nr,
        out_shape=[SDS((NCH, FC, D), bf16)] * 3 + [SDS((NCH, S, D), bf16)] + [SDS(r.shape, r.dtype) for r in rides],
        scratch_shapes=[pltpu.VMEM((FC, D), f32)] * 3 + (_scatter_sems(nr) if nr else []),
        compiler_params=_cp(56),
    )(x, nw, blob, blob, blob, dy, *rides)


def _norm_bwd(x, nw, dres, dh4):
    nt = S // TM
    nparts = dh4.shape[0]

    def body(x_ref, nw_ref, dres_ref, dh_ref, dx_ref, dnw_ref):
        i = pl.program_id(0)
        dh = dh_ref[0].astype(f32)
        for p in range(1, nparts):
            dh = dh + dh_ref[p].astype(f32)
        xv = x_ref[...]
        _, r = _rms(xv, nw_ref[...])
        dx, dw = _rms_bwd(xv, nw_ref[...], r, dh)
        dx_ref[...] = dres_ref[...] + dx

        @pl.when(i == 0)
        def _():
            dnw_ref[...] = jnp.zeros_like(dnw_ref)

        dnw_ref[...] += dw

    return pl.pallas_call(
        body, grid=(nt,), name="norm_bwd",
        in_specs=[pl.BlockSpec((TM, D), lambda i: (i, 0)),
                  pl.BlockSpec((1, D), lambda i: (0, 0)),
                  pl.BlockSpec((TM, D), lambda i: (i, 0)),
                  pl.BlockSpec((nparts, TM, D), lambda i: (0, i, 0))],
        out_specs=[pl.BlockSpec((TM, D), lambda i: (i, 0)), pl.BlockSpec((1, D), lambda i: (0, 0))],
        out_shape=[SDS((S, D), f32), SDS((1, D), f32)],
        compiler_params=_cp(40),
    )(x, nw, dres, dh4)


def _final(x, nw, target):
    nt = S // TM

    def body(x_ref, nw_ref, t_ref, dx_ref, dnw_ref, loss_ref):
        i = pl.program_id(0)
        xv = x_ref[...]
        y, r = _rms(xv, nw_ref[...])
        err = y - t_ref[...]
        part = 0.5 * jnp.sum(jnp.mean(err * err, axis=-1, keepdims=True), axis=0, keepdims=True)
        dx, dw = _rms_bwd(xv, nw_ref[...], r, err * (1.0 / D))
        dx_ref[...] = dx

        @pl.when(i == 0)
        def _():
            dnw_ref[...] = jnp.zeros_like(dnw_ref)
            loss_ref[...] = jnp.zeros_like(loss_ref)

        dnw_ref[...] += dw
        loss_ref[...] += jnp.broadcast_to(part, loss_ref.shape)

    return pl.pallas_call(
        body, grid=(nt,), name="final_loss",
        in_specs=[pl.BlockSpec((TM, D), lambda i: (i, 0)),
                  pl.BlockSpec((1, D), lambda i: (0, 0)),
                  pl.BlockSpec((TM, D), lambda i: (i, 0))],
        out_specs=[pl.BlockSpec((TM, D), lambda i: (i, 0)), pl.BlockSpec((1, D), lambda i: (0, 0)),
                   pl.BlockSpec((1, 128), lambda i: (0, 0))],
        out_shape=[SDS((S, D), f32), SDS((1, D), f32), SDS((1, 128), f32)],
        compiler_params=_cp(40),
    )(x, nw, target)


def _rot_half(t):
    lane = lax.broadcasted_iota(jnp.int32, t.shape, 1)
    first = (lane % EH) < (EH // 2)
    return jnp.where(first, -pltpu.roll(t, ATT - EH // 2, 1), pltpu.roll(t, EH // 2, 1))


def _rope_tables(pos_ref, freq_ref):
    ang = pos_ref[...].astype(f32) * freq_ref[...]
    return jnp.cos(ang), jnp.sin(ang)


def _split_residues(val, scr, outs):
    rows, cols = val.shape
    for j in range(cols // 128):
        scr[j] = val[:, 128 * j:128 * (j + 1)]
    for ref, d in outs:
        for j in range(cols // 128):
            for r in range(d):
                ref.at[r][:, 128 * j:128 * (j + 1)] = scr.at[j][pl.ds(r, rows // d, stride=d), :]


def _join_residues(ref, d, scr):
    rows, cols = scr.shape[1], ref.shape[2]
    for j in range(cols // 128):
        for r in range(d):
            scr.at[j][pl.ds(r, rows // d, stride=d), :] = ref.at[r][:, 128 * j:128 * (j + 1)]
    return jnp.concatenate([scr[j] for j in range(cols // 128)], axis=1)


def _res_spec(d, tile, cols):
    return pl.BlockSpec((d, tile // d, cols), lambda i: (0, i, 0))


def _inproj_fwd(x, nw, w_aug, pos, freq):
    TI = 256

    def body(x_ref, nw_ref, w_hbm, pos_ref, freq_ref, att_ref, att4_ref, att16_ref, pu_ref, dq_ref, dz_ref, dba_ref,
             w_scr, r_scr):
        @pl.when(pl.program_id(0) == 0)
        def _():
            pltpu.sync_copy(w_hbm, w_scr)

        h, _ = _rms(x_ref[...], nw_ref[...])
        proj = _dot(h.astype(bf16), w_scr[...])
        cos, sin = _rope_tables(pos_ref, freq_ref)
        q = proj[:, 0:ATT]
        k = proj[:, ATT:2 * ATT]
        att = jnp.concatenate([q * cos + _rot_half(q) * sin, k * cos + _rot_half(k) * sin, proj[:, 2 * ATT:3 * ATT]], axis=1)
        att_ref[...] = att
        _split_residues(att, r_scr, [(att4_ref, 4), (att16_ref, 16)])
        pu_ref[...] = proj[:, 768:1024]
        dq_ref[...] = proj[:, 1024:2560]
        dz_ref[...] = proj[:, 2560:3072]
        dba_ref[...] = proj[:, 3072:3200]

    return pl.pallas_call(
        body, grid=(S // TI,), name="inproj_fwd",
        in_specs=[pl.BlockSpec((TI, D), lambda i: (i, 0)),
                  pl.BlockSpec((1, D), lambda i: (0, 0)),
                  pl.BlockSpec(memory_space=pl.ANY),
                  pl.BlockSpec((TI, 1), lambda i: (i, 0)),
                  pl.BlockSpec((1, ATT), lambda i: (0, 0))],
        out_specs=[pl.BlockSpec((TI, 768), lambda i: (i, 0)), _res_spec(4, TI, 768), _res_spec(16, TI, 768),
                   pl.BlockSpec((TI, 256), lambda i: (i, 0)),
                   pl.BlockSpec((TI, 1536), lambda i: (i, 0)), pl.BlockSpec((TI, 512), lambda i: (i, 0)),
                   pl.BlockSpec((TI, 128), lambda i: (i, 0))],
        out_shape=[SDS((S, 768), f32), SDS((4, S // 4, 768), f32), SDS((16, S // 16, 768), f32), SDS((S, 256), f32),
                   SDS((S, 1536), f32), SDS((S, 512), f32), SDS((S, 128), f32)],
        scratch_shapes=[pltpu.VMEM((D, INP), bf16), pltpu.VMEM((6, TI, 128), f32)],
        compiler_params=_cp(48),
    )(x, nw, w_aug, pos, freq)


def _inproj_bwd(x, nw, w_aug, pos, freq, dres, datt, datt4, datt16, dpu, ddq, ddz, ddba):
    TI = 256
    nt = S // TI

    def body(x_ref, nw_ref, w_hbm, pos_ref, freq_ref, dres_ref, datt_ref, datt4_ref, datt16_ref, dpu_ref, ddq_ref, ddz_ref,
             ddba_ref, dx_ref, dnw_ref, dw_hbm, w_scr, acc, r_scr):
        i = pl.program_id(0)

        @pl.when(i == 0)
        def _():
            pltpu.sync_copy(w_hbm, w_scr)
            acc[...] = jnp.zeros_like(acc)
            dnw_ref[...] = jnp.zeros_like(dnw_ref)

        xv = x_ref[...]
        hf, r = _rms(xv, nw_ref[...])
        h = hf.astype(bf16)
        cos, sin = _rope_tables(pos_ref, freq_ref)
        datt = datt_ref[...] + _join_residues(datt4_ref, 4, r_scr)
        datt = datt + _join_residues(datt16_ref, 16, r_scr)
        dq = datt[:, 0:ATT]
        dk = datt[:, ATT:2 * ATT]
        dq = dq * cos - _rot_half(dq) * sin
        dk = dk * cos - _rot_half(dk) * sin
        dproj = jnp.concatenate([dq, dk, datt[:, 2 * ATT:3 * ATT], dpu_ref[...], ddq_ref[...], ddz_ref[...], ddba_ref[...]],
                                axis=1).astype(bf16)
        acc[...] += _dot_tn(h, dproj)
        dh = _dot_nt(dproj, w_scr[...])
        dx, dw = _rms_bwd(xv, nw_ref[...], r, dh)
        dx_ref[...] = dres_ref[...] + dx
        dnw_ref[...] += dw

        @pl.when(i == nt - 1)
        def _():
            pltpu.sync_copy(acc, dw_hbm)

    return pl.pallas_call(
        body, grid=(nt,), name="inproj_bwd",
        in_specs=[pl.BlockSpec((TI, D), lambda i: (i, 0)),
                  pl.BlockSpec((1, D), lambda i: (0, 0)),
                  pl.BlockSpec(memory_space=pl.ANY),
                  pl.BlockSpec((TI, 1), lambda i: (i, 0)),
                  pl.BlockSpec((1, ATT), lambda i: (0, 0)),
                  pl.BlockSpec((TI, D), lambda i: (i, 0)),
                  pl.BlockSpec((TI, 768), lambda i: (i, 0)), _res_spec(4, TI, 768), _res_spec(16, TI, 768),
                  pl.BlockSpec((TI, 256), lambda i: (i, 0)),
                  pl.BlockSpec((TI, 1536), lambda i: (i, 0)),
                  pl.BlockSpec((TI, 512), lambda i: (i, 0)),
                  pl.BlockSpec((TI, 128), lambda i: (i, 0))],
        out_specs=[pl.BlockSpec((TI, D), lambda i: (i, 0)), pl.BlockSpec((1, D), lambda i: (0, 0)),
                   pl.BlockSpec(memory_space=pl.ANY)],
        out_shape=[SDS((S, D), f32), SDS((1, D), f32), SDS((D, INP), f32)],
        scratch_shapes=[pltpu.VMEM((D, INP), bf16), pltpu.VMEM((D, INP), f32), pltpu.VMEM((6, TI, 128), f32)],
        compiler_params=_cp(56),
    )(x, nw, w_aug, pos, freq, dres, datt, datt4, datt16, dpu, ddq, ddz, ddba)


def _outproj_fwd(x, ya, yb, yc, blob_b, kw):
    def body(x_ref, ya_ref, yb_ref, yc_ref, w_ref, o_ref):
        ycat = jnp.concatenate([ya_ref[...], yb_ref[...], yc_ref[...]], axis=1).astype(bf16)
        o_ref[...] = x_ref[...] + _dot(ycat, w_ref[:, 0:256, :].reshape(D, D))

    return pl.pallas_call(
        body, grid=(S // TM,), name="outproj_fwd",
        in_specs=[pl.BlockSpec((TM, D), lambda i: (i, 0)),
                  pl.BlockSpec((TM, 256), lambda i: (i, 0)),
                  pl.BlockSpec((TM, 256), lambda i: (i, 0)),
                  pl.BlockSpec((TM, 512), lambda i: (i, 0)),
                  pl.BlockSpec((NCH, FC, D), lambda i: (0, kw, 0))],
        out_specs=pl.BlockSpec((TM, D), lambda i: (i, 0)),
        out_shape=SDS((S, D), f32),
        compiler_params=_cp(40),
    )(x, ya, yb, yc, blob_b)


def _outproj_bwd(dy, ya, yb, yc, blob_b, kw):
    nt = S // TM

    def body(dy_ref, ya_ref, yb_ref, yc_ref, w_ref, dya_ref, dyb_ref, dyc_ref, dw_ref):
        i = pl.program_id(0)

        @pl.when(i == 0)
        def _():
            dw_ref[...] = jnp.zeros_like(dw_ref)

        dyv = dy_ref[...].astype(bf16)
        ycat = jnp.concatenate([ya_ref[...], yb_ref[...], yc_ref[...]], axis=1).astype(bf16)
        dw_ref[...] += _dot_tn(ycat, dyv)
        dcat = _dot_nt(dyv, w_ref[:, 0:256, :].reshape(D, D))
        dya_ref[...] = dcat[:, 0:256]
        dyb_ref[...] = dcat[:, 256:512]
        dyc_ref[...] = dcat[:, 512:1024]

    return pl.pallas_call(
        body, grid=(nt,), name="outproj_bwd",
        in_specs=[pl.BlockSpec((TM, D), lambda i: (i, 0)),
                  pl.BlockSpec((TM, 256), lambda i: (i, 0)),
                  pl.BlockSpec((TM, 256), lambda i: (i, 0)),
                  pl.BlockSpec((TM, 512), lambda i: (i, 0)),
                  pl.BlockSpec((NCH, FC, D), lambda i: (0, kw, 0))],
        out_specs=[pl.BlockSpec((TM, 256), lambda i: (i, 0)), pl.BlockSpec((TM, 256), lambda i: (i, 0)),
                   pl.BlockSpec((TM, 512), lambda i: (i, 0)), pl.BlockSpec((D, D), lambda i: (0, 0))],
        out_shape=[SDS((S, 256), f32), SDS((S, 256), f32), SDS((S, 512), f32), SDS((D, D), f32)],
        compiler_params=_cp(40),
    )(dy, ya, yb, yc, blob_b)


NB = S // NBLK


def _attn_block(q, kp, kc, vp, vc, first):
    kk = jnp.concatenate([kp, kc], axis=0).astype(bf16)
    vv = jnp.concatenate([vp, vc], axis=0).astype(bf16)
    qi = lax.broadcasted_iota(jnp.int32, (4 * NBLK, 2 * NBLK), 0) % NBLK
    ki = lax.broadcasted_iota(jnp.int32, (4 * NBLK, 2 * NBLK), 1)
    dist = NBLK + qi - ki
    valid = (dist >= 0) & (dist <= NBLK) & (jnp.logical_not(first) | (ki >= NBLK))
    head = lax.broadcasted_iota(jnp.int32, (1, ATT), 1) // EH
    masks = [(head == h).astype(f32) for h in range(4)]
    qs = jnp.concatenate([q * mh for mh in masks], axis=0).astype(bf16)
    s = _dot_nt(qs, kk) * (1.0 / math.sqrt(EH))
    s = jnp.where(valid, s, NEG)
    m = lax.stop_gradient(jnp.max(s, axis=-1, keepdims=True))
    p = jnp.exp(s - m)
    den = jnp.sum(p, axis=-1, keepdims=True)
    po = _dot((p / den).astype(bf16), vv)
    lse = m + jnp.log(den)
    o = jnp.zeros((NBLK, ATT), f32)
    l = jnp.zeros((NBLK, ATT), f32)
    for h, mh in enumerate(masks):
        o = o + po[NBLK * h:NBLK * (h + 1)] * mh
        l = l + lse[NBLK * h:NBLK * (h + 1)] * mh
    return o, l


def _attn_specs():
    prev = lambda b: jnp.maximum(b - 1, 0)
    return [pl.BlockSpec((NBLK, ATT), lambda b: (b, 0)),
            pl.BlockSpec((NBLK, ATT), lambda b: (prev(b), 1)),
            pl.BlockSpec((NBLK, ATT), lambda b: (b, 1)),
            pl.BlockSpec((NBLK, ATT), lambda b: (prev(b), 2)),
            pl.BlockSpec((NBLK, ATT), lambda b: (b, 2))]


def _attn_fwd(qkv, per_seq):
    def body(q_ref, kp_ref, kc_ref, vp_ref, vc_ref, o_ref, l_ref):
        first = pl.program_id(0) % per_seq == 0
        o, l = _attn_block(q_ref[...], kp_ref[...], kc_ref[...], vp_ref[...], vc_ref[...], first)
        o_ref[...] = o
        l_ref[...] = l

    blk = pl.BlockSpec((NBLK, ATT), lambda b: (b, 0))
    return pl.pallas_call(
        body, grid=(NB,), name="attn_fwd", in_specs=_attn_specs(), out_specs=[blk, blk],
        out_shape=[SDS((S, ATT), f32), SDS((S, ATT), f32)], compiler_params=_cp(32),
    )(qkv, qkv, qkv, qkv, qkv)


def _attn_bwd(qkv, do, dl, per_seq):
    def body(q_ref, kp_ref, kc_ref, vp_ref, vc_ref, do_ref, dl_ref, o_ref, k_carry, v_carry):
        step = pl.program_id(0)

        @pl.when(step == 0)
        def _():
            k_carry[...] = jnp.zeros_like(k_carry)
            v_carry[...] = jnp.zeros_like(v_carry)

        b = NB - 1 - step
        first = b % per_seq == 0
        last = b % per_seq == per_seq - 1
        fn = lambda q, kp, kc, vp, vc: _attn_block(q, kp, kc, vp, vc, first)
        _, vjp = jax.vjp(fn, q_ref[...], kp_ref[...], kc_ref[...], vp_ref[...], vc_ref[...])
        dq, dkp, dkc, dvp, dvc = vjp((do_ref[...], dl_ref[...]))
        o_ref[:, 0:ATT] = dq
        o_ref[:, ATT:2 * ATT] = dkc + jnp.where(last, 0.0, k_carry[...])
        o_ref[:, 2 * ATT:3 * ATT] = dvc + jnp.where(last, 0.0, v_carry[...])
        k_carry[...] = dkp
        v_carry[...] = dvp

    rev = lambda s: NB - 1 - s
    prev = lambda s: jnp.maximum(rev(s) - 1, 0)
    specs = [pl.BlockSpec((NBLK, ATT), lambda s: (rev(s), 0)),
             pl.BlockSpec((NBLK, ATT), lambda s: (prev(s), 1)),
             pl.BlockSpec((NBLK, ATT), lambda s: (rev(s), 1)),
             pl.BlockSpec((NBLK, ATT), lambda s: (prev(s), 2)),
             pl.BlockSpec((NBLK, ATT), lambda s: (rev(s), 2)),
             pl.BlockSpec((NBLK, ATT), lambda s: (rev(s), 0)),
             pl.BlockSpec((NBLK, ATT), lambda s: (rev(s), 0))]
    return pl.pallas_call(
        body, grid=(NB,), name="attn_bwd", in_specs=specs, out_specs=pl.BlockSpec((NBLK, 768), lambda s: (rev(s), 0)),
        out_shape=SDS((S, 768), f32), scratch_shapes=[pltpu.VMEM((NBLK, ATT), f32)] * 2, compiler_params=_cp(32),
    )(qkv, qkv, qkv, qkv, qkv, do, dl)


def _merge_weights(l0, l1, l2):
    m = jnp.maximum(jnp.maximum(l0, l1), l2)
    e0, e1, e2 = jnp.exp(l0 - m), jnp.exp(l1 - m), jnp.exp(l2 - m)
    tot = e0 + e1 + e2
    return e0 / tot, e1 / tot, e2 / tot


def _merge_specs():
    nat = pl.BlockSpec((TM, ATT), lambda i: (i, 0))
    return nat, _res_spec(4, TM, ATT), _res_spec(16, TM, ATT)


def _merge_fwd(o1, l1, o4, l4, o16, l16):
    def body(o1_ref, l1_ref, o4_ref, l4_ref, o16_ref, l16_ref, y_ref, scr):
        o4v, l4v = _join_residues(o4_ref, 4, scr), _join_residues(l4_ref, 4, scr)
        o16v, l16v = _join_residues(o16_ref, 16, scr), _join_residues(l16_ref, 16, scr)
        w0, w1, w2 = _merge_weights(l1_ref[...], l4v, l16v)
        y_ref[...] = w0 * o1_ref[...] + w1 * o4v + w2 * o16v

    nat, r4, r16 = _merge_specs()
    return pl.pallas_call(body, grid=(S // TM,), name="merge_fwd", in_specs=[nat, nat, r4, r4, r16, r16],
                          out_specs=nat, out_shape=SDS((S, ATT), f32), scratch_shapes=[pltpu.VMEM((2, TM, 128), f32)],
                          compiler_params=_cp(32))(o1, l1, o4, l4, o16, l16)


def _merge_bwd(o1, l1, o4, l4, o16, l16, dy):
    def body(o1_ref, l1_ref, o4_ref, l4_ref, o16_ref, l16_ref, dy_ref, do1_ref, dl1_ref, do4_ref, dl4_ref, do16_ref, dl16_ref, scr):
        o4v, l4v = _join_residues(o4_ref, 4, scr), _join_residues(l4_ref, 4, scr)
        o16v, l16v = _join_residues(o16_ref, 16, scr), _join_residues(l16_ref, 16, scr)
        o1v = o1_ref[...]
        w0, w1, w2 = _merge_weights(l1_ref[...], l4v, l16v)
        y = w0 * o1v + w1 * o4v + w2 * o16v
        dyv = dy_ref[...]
        do1_ref[...] = w0 * dyv
        dl1_ref[...] = w0 * (o1v - y) * dyv
        _split_residues(w1 * dyv, scr, [(do4_ref, 4)])
        _split_residues(w1 * (o4v - y) * dyv, scr, [(dl4_ref, 4)])
        _split_residues(w2 * dyv, scr, [(do16_ref, 16)])
        _split_residues(w2 * (o16v - y) * dyv, scr, [(dl16_ref, 16)])

    nat, r4, r16 = _merge_specs()
    return pl.pallas_call(body, grid=(S // TM,), name="merge_bwd", in_specs=[nat, nat, r4, r4, r16, r16, nat],
                          out_specs=[nat, nat, r4, r4, r16, r16],
                          out_shape=[SDS((S, ATT), f32)] * 2 + [SDS((4, S // 4, ATT), f32)] * 2 + [SDS((16, S // 16, ATT), f32)] * 2,
                          scratch_shapes=[pltpu.VMEM((2, TM, 128), f32)], compiler_params=_cp(32))(o1, l1, o4, l4, o16, l16, dy)


HALO = 16


def _pool_consts(i, rows):
    grp = lax.broadcasted_iota(jnp.int32, (rows, 256), 1) // 64
    t = i * TM + lax.broadcasted_iota(jnp.int32, (rows, 256), 0)
    win = jnp.where(grp == 0, 2, jnp.where(grp == 1, 4, jnp.where(grp == 2, 8, 16)))
    cnt = jnp.minimum(t + 1, win).astype(f32)
    return grp, cnt


def _pool_select(grp, s2, s4, s8, s16):
    return jnp.where(grp == 0, s2, jnp.where(grp == 1, s4, jnp.where(grp == 2, s8, s16)))


def _pooled(i, cur, halo):
    xx = jnp.concatenate([halo, cur], axis=0)
    s2 = xx + pltpu.roll(xx, 1, 0)
    s4 = s2 + pltpu.roll(s2, 2, 0)
    s8 = s4 + pltpu.roll(s4, 4, 0)
    s16 = s8 + pltpu.roll(s8, 8, 0)
    grp, cnt = _pool_consts(i, TM)
    tot = _pool_select(grp, s2[HALO:], s4[HALO:], s8[HALO:], s16[HALO:])
    return tot / cnt - cur


def _pool_fwd(u, wp, scale):
    def body(u_ref, halo_ref, wp_ref, sc_ref, y_ref):
        i = pl.program_id(0)
        halo = halo_ref[...] * (i > 0).astype(f32)
        pooled = _pooled(i, u_ref[...], halo)
        y_ref[...] = _dot(pooled.astype(bf16), wp_ref[...]) * sc_ref[...]

    return pl.pallas_call(
        body, grid=(S // TM,), name="pool_fwd",
        in_specs=[pl.BlockSpec((TM, 256), lambda i: (i, 0)),
                  pl.BlockSpec((HALO, 256), lambda i: (jnp.maximum(i * (TM // HALO) - 1, 0), 0)),
                  pl.BlockSpec((256, 256), lambda i: (0, 0)),
                  pl.BlockSpec((1, 256), lambda i: (0, 0))],
        out_specs=pl.BlockSpec((TM, 256), lambda i: (i, 0)), out_shape=SDS((S, 256), f32), compiler_params=_cp(32),
    )(u, u, wp, scale)


def _pool_bwd(u, wp, scale, dy):
    nt = S // TM

    def body(u_ref, halo_ref, wp_ref, sc_ref, dy_ref, dyn_ref, du_ref, dwp_ref, dsc_ref):
        i = pl.program_id(0)

        @pl.when(i == 0)
        def _():
            dwp_ref[...] = jnp.zeros_like(dwp_ref)
            dsc_ref[...] = jnp.zeros_like(dsc_ref)

        halo = halo_ref[...] * (i > 0).astype(f32)
        pooled = _pooled(i, u_ref[...], halo).astype(bf16)
        dyv = dy_ref[...]
        dsc_ref[...] += jnp.sum(dyv * _dot(pooled, wp_ref[...]), axis=0, keepdims=True)
        dys = (dyv * sc_ref[...]).astype(bf16)
        dwp_ref[...] += _dot_tn(pooled, dys)
        dpool = _dot_nt(dys, wp_ref[...])
        grp, cnt = _pool_consts(i, TM)
        dyn = ((dyn_ref[...] * (i < nt - 1).astype(f32)) * sc_ref[...]).astype(bf16)
        _, cntn = _pool_consts(i + 1, HALO)
        zn = _dot_nt(dyn, wp_ref[...]) / cntn
        zz = jnp.concatenate([dpool / cnt, zn], axis=0)
        n = TM + HALO
        a2 = zz + pltpu.roll(zz, n - 1, 0)
        a4 = a2 + pltpu.roll(a2, n - 2, 0)
        a8 = a4 + pltpu.roll(a4, n - 4, 0)
        a16 = a8 + pltpu.roll(a8, n - 8, 0)
        du_ref[...] = _pool_select(grp, a2[:TM], a4[:TM], a8[:TM], a16[:TM]) - dpool

    return pl.pallas_call(
        body, grid=(nt,), name="pool_bwd",
        in_specs=[pl.BlockSpec((TM, 256), lambda i: (i, 0)),
                  pl.BlockSpec((HALO, 256), lambda i: (jnp.maximum(i * (TM // HALO) - 1, 0), 0)),
                  pl.BlockSpec((256, 256), lambda i: (0, 0)),
                  pl.BlockSpec((1, 256), lambda i: (0, 0)),
                  pl.BlockSpec((TM, 256), lambda i: (i, 0)),
                  pl.BlockSpec((HALO, 256), lambda i: (jnp.minimum((i + 1) * (TM // HALO), S // HALO - 1), 0))],
        out_specs=[pl.BlockSpec((TM, 256), lambda i: (i, 0)), pl.BlockSpec((256, 256), lambda i: (0, 0)),
                   pl.BlockSpec((1, 256), lambda i: (0, 0))],
        out_shape=[SDS((S, 256), f32), SDS((256, 256), f32), SDS((1, 256), f32)], compiler_params=_cp(32),
    )(u, u, wp, scale, dy, dy)


CW = 3 * DNW
CHALO = 8
TC = 256


def _conv_fwd(u, w):
    def body(u_ref, halo_ref, w_ref, c_ref):
        i = pl.program_id(0)
        xx = jnp.concatenate([halo_ref[...] * (i > 0).astype(f32), u_ref[...]], axis=0)
        c = (w_ref[3:4, :] * xx + w_ref[2:3, :] * pltpu.roll(xx, 1, 0) + w_ref[1:2, :] * pltpu.roll(xx, 2, 0)
             + w_ref[0:1, :] * pltpu.roll(xx, 3, 0))
        c_ref[...] = c[CHALO:]

    return pl.pallas_call(
        body, grid=(S // TC,), name="conv_fwd",
        in_specs=[pl.BlockSpec((TC, CW), lambda i: (i, 0)),
                  pl.BlockSpec((CHALO, CW), lambda i: (jnp.maximum(i * (TC // CHALO) - 1, 0), 0)),
                  pl.BlockSpec((8, CW), lambda i: (0, 0))],
        out_specs=pl.BlockSpec((TC, CW), lambda i: (i, 0)), out_shape=SDS((S, CW), f32), compiler_params=_cp(32),
    )(u, u, w)


def _conv_bwd(u, w, dc):
    nt = S // TC

    def body(u_ref, halo_ref, w_ref, dc_ref, dcn_ref, du_ref, dw_ref):
        i = pl.program_id(0)

        @pl.when(i == 0)
        def _():
            dw_ref[...] = jnp.zeros_like(dw_ref)

        dcv = dc_ref[...]
        zz = jnp.concatenate([dcv, dcn_ref[...] * (i < nt - 1).astype(f32)], axis=0)
        n = TC + CHALO
        du = (w_ref[3:4, :] * zz + w_ref[2:3, :] * pltpu.roll(zz, n - 1, 0) + w_ref[1:2, :] * pltpu.roll(zz, n - 2, 0)
              + w_ref[0:1, :] * pltpu.roll(zz, n - 3, 0))
        du_ref[...] = du[:TC]
        xx = jnp.concatenate([halo_ref[...] * (i > 0).astype(f32), u_ref[...]], axis=0)
        for j in range(4):
            shifted = xx if j == 3 else pltpu.roll(xx, 3 - j, 0)
            dw_ref[j:j + 1, :] += jnp.sum(dcv * shifted[CHALO:], axis=0, keepdims=True)

    return pl.pallas_call(
        body, grid=(nt,), name="conv_bwd",
        in_specs=[pl.BlockSpec((TC, CW), lambda i: (i, 0)),
                  pl.BlockSpec((CHALO, CW), lambda i: (jnp.maximum(i * (TC // CHALO) - 1, 0), 0)),
                  pl.BlockSpec((8, CW), lambda i: (0, 0)),
                  pl.BlockSpec((TC, CW), lambda i: (i, 0)),
                  pl.BlockSpec((CHALO, CW), lambda i: (jnp.minimum((i + 1) * (TC // CHALO), S // CHALO - 1), 0))],
        out_specs=[pl.BlockSpec((TC, CW), lambda i: (i, 0)), pl.BlockSpec((8, CW), lambda i: (0, 0))],
        out_shape=[SDS((S, CW), f32), SDS((8, CW), f32)], compiler_params=_cp(32),
    )(u, u, w, dc, dc)


TL = 256
NCL = TL // CH


def _bdot(a, b):
    return jnp.einsum('nik,nkj->nij', a.astype(bf16), b.astype(bf16), preferred_element_type=f32)


def _bdot_nt(a, b):
    return jnp.einsum('nik,njk->nij', a.astype(bf16), b.astype(bf16), preferred_element_type=f32)


def _bdot_tn(a, b):
    return jnp.einsum('nki,nkj->nij', a.astype(bf16), b.astype(bf16), preferred_element_type=f32)


@jax.custom_vjp
def _inv_unit_lower(a):
    ii = lax.broadcasted_iota(jnp.int32, (1, CH, CH), 1)
    jj = lax.broadcasted_iota(jnp.int32, (1, CH, CH), 2)
    t = (ii == jj).astype(f32) - a
    p = a
    for _ in range(5):
        p = _bdot(p, p)
        t = t + _bdot(t, p)
    return t


def _inv_unit_lower_fwd(a):
    t = _inv_unit_lower(a)
    return t, t


def _inv_unit_lower_bwd(t, dt):
    return (-_bdot_tn(t, _bdot_nt(dt, t)),)


_inv_unit_lower.defvjp(_inv_unit_lower_fwd, _inv_unit_lower_bwd)


def _dn_local(c, dba, a_row, b_row):
    act = c * jax.nn.sigmoid(c)
    lane = lax.broadcasted_iota(jnp.int32, (TL, 128), 1)
    beta_all = jax.nn.sigmoid(dba)
    xs = dba + b_row
    softplus = jnp.maximum(xs, 0.0) + jnp.log(1.0 + jnp.exp(-jnp.abs(xs)))
    g_all = -jnp.exp(a_row) * softplus
    ii = lax.broadcasted_iota(jnp.int32, (1, CH, CH), 1)
    jj = lax.broadcasted_iota(jnp.int32, (1, CH, CH), 2)
    lower = jj <= ii
    strict = jj < ii
    eye = (ii == jj).astype(f32)
    us, ws, qgs, kds, intras = [], [], [], [], []
    aux = jnp.zeros((TL, 128), f32)
    for h in range(4):
        q = act[:, DH * h:DH * (h + 1)]
        k = act[:, DNW + DH * h:DNW + DH * (h + 1)]
        v = act[:, 2 * DNW + DH * h:2 * DNW + DH * (h + 1)]
        q = q * lax.rsqrt(jnp.sum(q * q, axis=-1, keepdims=True) + EPS) * (DH ** -0.5)
        k = k * lax.rsqrt(jnp.sum(k * k, axis=-1, keepdims=True) + EPS)
        beta = jnp.sum(jnp.where(lane == h, beta_all, 0.0), axis=1, keepdims=True)
        g = jnp.sum(jnp.where(lane == 4 + h, g_all, 0.0), axis=1, keepdims=True)
        q3, k3, v3 = q.reshape(NCL, CH, DH), k.reshape(NCL, CH, DH), v.reshape(NCL, CH, DH)
        beta3, g3 = beta.reshape(NCL, CH, 1), g.reshape(NCL, CH, 1)
        g_row = jnp.sum(eye * g3, axis=1, keepdims=True)
        gc_col = jnp.sum(jnp.where(lower, g_row, 0.0), axis=2, keepdims=True)
        gc_row = jnp.sum(jnp.where(ii <= jj, g3, 0.0), axis=1, keepdims=True)
        diff = gc_col - gc_row
        decay = jnp.where(lower, jnp.exp(jnp.where(lower, diff, 0.0)), 0.0)
        kb = k3 * beta3
        vb = v3 * beta3
        a = jnp.where(strict, _bdot_nt(kb, k3) * decay, 0.0)
        t = _inv_unit_lower(a)
        u3 = _bdot(t, vb)
        w3 = _bdot(t, kb * jnp.exp(gc_col))
        intra = jnp.where(lower, _bdot_nt(q3, k3) * decay, 0.0)
        g_last = jnp.sum(g3, axis=1, keepdims=True)
        us.append(u3.reshape(TL, DH))
        ws.append(w3.reshape(TL, DH))
        qgs.append((q3 * jnp.exp(gc_col)).reshape(TL, DH))
        kds.append((k3 * jnp.exp(g_last - gc_col)).reshape(TL, DH))
        intras.append(intra.reshape(TL, CH))
        e_last = jnp.broadcast_to(jnp.exp(g_last), (NCL, CH, 1)).reshape(TL, 1)
        aux = aux + jnp.where(lane == h, e_last, 0.0)
    cat = lambda xs: jnp.concatenate(xs, axis=1)
    return cat(us), cat(ws), cat(qgs), cat(kds), jnp.stack(intras, axis=0), aux


def _dn_local_fwd(c, dba, par):
    def body(c_ref, dba_ref, par_ref, u_ref, w_ref, qg_ref, kd_ref, in_ref, aux_ref):
        u, w, qg, kd, intra, aux = _dn_local(c_ref[...], dba_ref[...], par_ref[0:1, :], par_ref[1:2, :])
        u_ref[...] = u
        w_ref[...] = w
        qg_ref[...] = qg
        kd_ref[...] = kd
        in_ref[...] = intra
        aux_ref[...] = aux

    wide = pl.BlockSpec((TL, DNW), lambda i: (i, 0))
    return pl.pallas_call(
        body, grid=(S // TL,), name="dn_local_fwd",
        in_specs=[pl.BlockSpec((TL, CW), lambda i: (i, 0)), pl.BlockSpec((TL, 128), lambda i: (i, 0)),
                  pl.BlockSpec((8, 128), lambda i: (0, 0))],
        out_specs=[wide, wide, wide, wide, pl.BlockSpec((4, TL, CH), lambda i: (0, i, 0)),
                   pl.BlockSpec((TL, 128), lambda i: (i, 0))],
        out_shape=[SDS((S, DNW), f32)] * 4 + [SDS((4, S, CH), f32), SDS((S, 128), f32)], compiler_params=_cp(48),
    )(c, dba, par)


def _dn_local_bwd(c, dba, par, du, dw, dqg, dkd, dintra, daux):
    def body(c_ref, dba_ref, par_ref, du_ref, dw_ref, dqg_ref, dkd_ref, din_ref, daux_ref, dc_ref, ddba_ref, dpar_ref):
        @pl.when(pl.program_id(0) == 0)
        def _():
            dpar_ref[...] = jnp.zeros_like(dpar_ref)

        _, vjp = jax.vjp(_dn_local, c_ref[...], dba_ref[...], par_ref[0:1, :], par_ref[1:2, :])
        dc, ddba, da_row, db_row = vjp((du_ref[...], dw_ref[...], dqg_ref[...], dkd_ref[...], din_ref[...], daux_ref[...]))
        dc_ref[...] = dc
        ddba_ref[...] = ddba
        dpar_ref[0:1, :] += da_row
        dpar_ref[1:2, :] += db_row

    wide = pl.BlockSpec((TL, DNW), lambda i: (i, 0))
    return pl.pallas_call(
        body, grid=(S // TL,), name="dn_local_bwd",
        in_specs=[pl.BlockSpec((TL, CW), lambda i: (i, 0)), pl.BlockSpec((TL, 128), lambda i: (i, 0)),
                  pl.BlockSpec((8, 128), lambda i: (0, 0)), wide, wide, wide, wide,
                  pl.BlockSpec((4, TL, CH), lambda i: (0, i, 0)), pl.BlockSpec((TL, 128), lambda i: (i, 0))],
        out_specs=[pl.BlockSpec((TL, CW), lambda i: (i, 0)), pl.BlockSpec((TL, 128), lambda i: (i, 0)),
                   pl.BlockSpec((8, 128), lambda i: (0, 0))],
        out_shape=[SDS((S, CW), f32), SDS((S, 128), f32), SDS((8, 128), f32)], compiler_params=_cp(56),
    )(c, dba, par, du, dw, dqg, dkd, dintra, daux)


def _dn_step(state, u, w, qg, kd, intra, aux):
    lane = lax.broadcasted_iota(jnp.int32, (CH, 128), 1)
    row = lax.broadcasted_iota(jnp.int32, (CH, 128), 0)
    outs, states = [], []
    for h in range(4):
        sl = slice(DH * h, DH * (h + 1))
        st = state[h]
        e = jnp.sum(jnp.sum(jnp.where((lane == h) & (row == 0), aux, 0.0), axis=1, keepdims=True), axis=0, keepdims=True)
        v_new = u[:, sl] - _dot(w[:, sl].astype(bf16), st.astype(bf16))
        vb = v_new.astype(bf16)
        outs.append(_dot(qg[:, sl].astype(bf16), st.astype(bf16)) + _dot(intra[h].astype(bf16), vb))
        states.append(st * e + _dot_tn(kd[:, sl].astype(bf16), vb))
    return jnp.concatenate(outs, axis=1), jnp.stack(states, axis=0)


def _dn_rec_fwd(u, w, qg, kd, intra, aux):
    def body(u_ref, w_ref, qg_ref, kd_ref, in_ref, aux_ref, o_ref, st_ref, st_scr):
        @pl.when(pl.program_id(0) == 0)
        def _():
            st_scr[...] = jnp.zeros_like(st_scr)

        st = st_scr[...]
        st_ref[0] = st
        o, new = _dn_step(st, u_ref[...], w_ref[...], qg_ref[...], kd_ref[...], in_ref[...], aux_ref[...])
        o_ref[...] = o
        st_scr[...] = new

    wide = pl.BlockSpec((CH, DNW), lambda n: (n, 0))
    return pl.pallas_call(
        body, grid=(NCHUNK,), name="dn_rec_fwd",
        in_specs=[wide, wide, wide, wide, pl.BlockSpec((4, CH, CH), lambda n: (0, n, 0)),
                  pl.BlockSpec((CH, 128), lambda n: (n, 0))],
        out_specs=[wide, pl.BlockSpec((1, 4, DH, DH), lambda n: (n, 0, 0, 0))],
        out_shape=[SDS((S, DNW), f32), SDS((NCHUNK, 4, DH, DH), f32)],
        scratch_shapes=[pltpu.VMEM((4, DH, DH), f32)], compiler_params=_cp(32),
    )(u, w, qg, kd, intra, aux)


def _dn_rec_bwd(u, w, qg, kd, intra, aux, states, do):
    def body(u_ref, w_ref, qg_ref, kd_ref, in_ref, aux_ref, st_ref, do_ref,
             du_ref, dw_ref, dqg_ref, dkd_ref, din_ref, daux_ref, ds_scr):
        @pl.when(pl.program_id(0) == 0)
        def _():
            ds_scr[...] = jnp.zeros_like(ds_scr)

        _, vjp = jax.vjp(_dn_step, st_ref[0], u_ref[...], w_ref[...], qg_ref[...], kd_ref[...], in_ref[...], aux_ref[...])
        dst, du, dw, dqg, dkd, din, daux = vjp((do_ref[...], ds_scr[...]))
        du_ref[...] = du
        dw_ref[...] = dw
        dqg_ref[...] = dqg
        dkd_ref[...] = dkd
        din_ref[...] = din
        daux_ref[...] = daux
        ds_scr[...] = dst

    rev = lambda n: NCHUNK - 1 - n
    wide = pl.BlockSpec((CH, DNW), lambda n: (rev(n), 0))
    inb = pl.BlockSpec((4, CH, CH), lambda n: (0, rev(n), 0))
    auxb = pl.BlockSpec((CH, 128), lambda n: (rev(n), 0))
    return pl.pallas_call(
        body, grid=(NCHUNK,), name="dn_rec_bwd",
        in_specs=[wide, wide, wide, wide, inb, auxb, pl.BlockSpec((1, 4, DH, DH), lambda n: (rev(n), 0, 0, 0)), wide],
        out_specs=[wide, wide, wide, wide, inb, auxb],
        out_shape=[SDS((S, DNW), f32)] * 4 + [SDS((4, S, CH), f32), SDS((S, 128), f32)],
        scratch_shapes=[pltpu.VMEM((4, DH, DH), f32)], compiler_params=_cp(32),
    )(u, w, qg, kd, intra, aux, states, do)


def _dn_post(o, z, nw):
    parts = []
    for h in range(4):
        sl = slice(DH * h, DH * (h + 1))
        oh = o[:, sl]
        y = oh * lax.rsqrt(jnp.mean(oh * oh, axis=-1, keepdims=True) + EPS) * nw
        zh = z[:, sl]
        parts.append(y * (zh * jax.nn.sigmoid(zh)))
    return jnp.concatenate(parts, axis=1)


def _dn_post_fwd(o, z, nw):
    def body(o_ref, z_ref, nw_ref, y_ref):
        y_ref[...] = _dn_post(o_ref[...], z_ref[...], nw_ref[...])

    wide = pl.BlockSpec((TM, DNW), lambda i: (i, 0))
    return pl.pallas_call(body, grid=(S // TM,), name="dn_post_fwd",
                          in_specs=[wide, wide, pl.BlockSpec((1, 128), lambda i: (0, 0))], out_specs=wide,
                          out_shape=SDS((S, DNW), f32), compiler_params=_cp(32))(o, z, nw)


def _dn_post_bwd(o, z, nw, dy):
    def body(o_ref, z_ref, nw_ref, dy_ref, do_ref, dz_ref, dnw_ref):
        @pl.when(pl.program_id(0) == 0)
        def _():
            dnw_ref[...] = jnp.zeros_like(dnw_ref)

        _, vjp = jax.vjp(_dn_post, o_ref[...], z_ref[...], nw_ref[...])
        do, dz, dnw = vjp(dy_ref[...])
        do_ref[...] = do
        dz_ref[...] = dz
        dnw_ref[...] += dnw

    wide = pl.BlockSpec((TM, DNW), lambda i: (i, 0))
    one = pl.BlockSpec((1, 128), lambda i: (0, 0))
    return pl.pallas_call(body, grid=(S // TM,), name="dn_post_bwd", in_specs=[wide, wide, one, wide],
                          out_specs=[wide, wide, one], out_shape=[SDS((S, DNW), f32), SDS((S, DNW), f32), SDS((1, 128), f32)],
                          compiler_params=_cp(32))(o, z, nw, dy)


def _row_tile(rows, width, itemsize=4, target=2 * 1024 * 1024):
    best = None
    for t in range(16, rows + 1, 16):
        if rows % t == 0 and t * width * itemsize <= target:
            best = t
    return best if best is not None else rows


def _sum_pieces(pieces, out_dtype, name):
    n, rows, width = pieces.shape
    tr = _row_tile(rows, width * n)

    def body(p_ref, o_ref):
        acc = p_ref[0].astype(f32)
        for s in range(1, n):
            acc = acc + p_ref[s].astype(f32)
        o_ref[...] = acc.astype(out_dtype)

    return pl.pallas_call(body, grid=(rows // tr,), name=name,
                          in_specs=[pl.BlockSpec((n, tr, width), lambda i: (0, i, 0))],
                          out_specs=pl.BlockSpec((tr, width), lambda i: (i, 0)),
                          out_shape=SDS((rows, width), out_dtype), compiler_params=_cp(32))(pieces)


def _sum_core_pair(part, got, c_arr):
    n, rows, width = part.shape
    half = rows // 2
    tr = _row_tile(half, width, itemsize=2)
    nt = half // tr

    def body(c_ref, p_ref, g_ref, o_ref):
        o_ref[...] = (p_ref[...].astype(f32) + g_ref[...].astype(f32)).astype(bf16)

    gs = pltpu.PrefetchScalarGridSpec(
        num_scalar_prefetch=1, grid=(n, nt),
        in_specs=[pl.BlockSpec((1, tr, width), lambda j, i, c: (j, c[0] * nt + i, 0)),
                  pl.BlockSpec((1, tr, width), lambda j, i, c: (j, i, 0))],
        out_specs=pl.BlockSpec((1, tr, width), lambda j, i, c: (j, i, 0)))
    return pl.pallas_call(body, grid_spec=gs, name="sum_core_pair", out_shape=SDS((n, half, width), bf16),
                          compiler_params=_cp(32))(c_arr, part, got)


def _sum_chips(pieces, c_arr, full, row0, total_rows):
    n, half, width = pieces.shape
    tr = max(t for t in range(16, 257, 16) if half % t == 0 and row0 % t == 0)
    nt = half // tr

    def body(c_ref, p_ref, *rest):
        o_ref = rest[-1]
        acc = p_ref[0].astype(f32)
        for s in range(1, n):
            acc = acc + p_ref[s].astype(f32)
        o_ref[...] = acc

    gs = pltpu.PrefetchScalarGridSpec(
        num_scalar_prefetch=1, grid=(nt,),
        in_specs=[pl.BlockSpec((n, tr, width), lambda i, c: (0, i, 0))] + ([] if full is None else [ANY]),
        out_specs=pl.BlockSpec((tr, width), lambda i, c: (row0 // tr + c[0] * nt + i, 0)))
    args = (c_arr, pieces) if full is None else (c_arr, pieces, full)
    return pl.pallas_call(body, grid_spec=gs, name="sum_chips", out_shape=SDS((total_rows, width), f32),
                          input_output_aliases={} if full is None else {2: 0}, compiler_params=_cp(32))(*args)


def _adamw_math(w, g, m, v):
    mn = ADAM_B1 * m + (1.0 - ADAM_B1) * g
    vn = ADAM_B2 * v + (1.0 - ADAM_B2) * (g * g)
    m_hat = mn / (1.0 - ADAM_B1 ** ADAM_STEP)
    v_hat = vn / (1.0 - ADAM_B2 ** ADAM_STEP)
    return -ADAM_LR * (m_hat / (jnp.sqrt(v_hat) + ADAM_EPS) + ADAM_WD * w), mn, vn


def _adamw(w, g, m, v, name):
    rows, width = w.shape
    tr = _row_tile(rows, width * 7, target=12 * 1024 * 1024)

    def body(w_ref, g_ref, m_ref, v_ref, d_ref, nm_ref, nv_ref):
        d_ref[...], nm_ref[...], nv_ref[...] = _adamw_math(w_ref[...], g_ref[...], m_ref[...], v_ref[...])

    blk = pl.BlockSpec((tr, width), lambda i: (i, 0))
    return pl.pallas_call(body, grid=(rows // tr,), name=name, in_specs=[blk] * 4, out_specs=[blk] * 3,
                          out_shape=[SDS((rows, width), f32)] * 3, compiler_params=_cp(40))(w, g, m, v)


def _adamw_rows(w, m, v, gblob, tr, first_tile, name):
    layers, rows, width = w.shape

    def body(w_ref, g_ref, m_ref, v_ref, d_ref, nm_ref, nv_ref):
        d_ref[0], nm_ref[0], nv_ref[0] = _adamw_math(w_ref[0], g_ref[...], m_ref[0], v_ref[0])

    blk = pl.BlockSpec((1, tr, width), lambda l, i: (l, i, 0))
    gblk = pl.BlockSpec((tr, width), lambda l, i: (first_tile(l) + i, 0))
    return pl.pallas_call(body, grid=(layers, rows // tr), name=name, in_specs=[blk, gblk, blk, blk], out_specs=[blk] * 3,
                          out_shape=[SDS(w.shape, f32)] * 3, compiler_params=_cp(40))(w, gblob, m, v)


ANY = pl.BlockSpec(memory_space=pl.ANY)


def _place():
    x, y, c = lax.axis_index("x"), lax.axis_index("y"), lax.axis_index("c")
    chips = [(1 - x, y), (x, 1 - y), (1 - x, 1 - y)]
    return x, y, c, chips


NQ_ICI = 4
NQ_D2D = 8


def _chunks(rows, want):
    n = max(k for k in range(1, want + 1) if rows % k == 0 and (rows // k) % 16 == 0)
    step = rows // n
    return [(q * step, step) for q in range(n)]


def _scatter_copies(ins, outs, ssem, rsem, lsem):
    x, y, c, chips = _place()
    me = (x, y, c)
    locals_, sends, lands = [], [], []
    for b in range(len(ins)):
        for q, (off, n) in enumerate(_chunks(ins[b].shape[1], NQ_ICI)):
            rows = pl.ds(off, n)
            mine = outs[b].at[2 * x + y, rows, :]
            locals_.append(pltpu.make_async_copy(ins[b].at[2 * x + y, rows, :], mine, lsem.at[b, q]))
            for j, chip in enumerate(chips):
                sends.append(_remote(ins[b].at[2 * chip[0] + chip[1], rows, :], mine, ssem.at[b, j, q], rsem.at[b, j, q],
                                     (*chip, c)))
                slot = outs[b].at[2 * chip[0] + chip[1], rows, :]
                lands.append(_remote(slot, slot, ssem.at[b, j, q], rsem.at[b, j, q], me))
    return locals_, sends, lands


def _scatter_start(ins, outs, ssem, rsem, lsem):
    locals_, sends, _ = _scatter_copies(ins, outs, ssem, rsem, lsem)
    for cp in locals_ + sends:
        cp.start()


def _scatter_finish(ins, outs, ssem, rsem, lsem):
    locals_, sends, lands = _scatter_copies(ins, outs, ssem, rsem, lsem)
    for cp in lands:
        cp.wait_recv()
    for cp in sends:
        cp.wait_send()
    for cp in locals_:
        cp.wait()


def _scatter_sems(nb):
    return [pltpu.SemaphoreType.DMA((nb, 3, NQ_ICI)), pltpu.SemaphoreType.DMA((nb, 3, NQ_ICI)),
            pltpu.SemaphoreType.DMA((nb, NQ_ICI))]


def _remote(src, dst, ssem, rsem, dev):
    return pltpu.make_async_remote_copy(src_ref=src, dst_ref=dst, send_sem=ssem, recv_sem=rsem, device_id=dev,
                                        device_id_type=MESH)


def _all_gather_weights(shards):
    nb = len(shards)

    def body(*refs):
        ins, outs = refs[:nb], refs[nb:2 * nb]
        ssem, rsem, lsem = refs[2 * nb:]
        x, y, c, chips = _place()
        me, sib = (x, y, c), (x, y, 1 - c)
        started, locals_ = [], []
        for b in range(nb):
            half = ins[b].shape[0] // 2
            for q, (off, n) in enumerate(_chunks(half, NQ_ICI)):
                mine = pl.ds(c * half + off, n)
                own = outs[b].at[2 * x + y, mine, :]
                local = pltpu.make_async_copy(ins[b].at[mine, :], own, lsem.at[b, q])
                local.start()
                first = [_remote(ins[b].at[mine, :], own, ssem.at[b, 0, q], rsem.at[b, 0, q], sib)]
                first += [_remote(ins[b].at[mine, :], own, ssem.at[b, 1 + j, q], rsem.at[b, 1 + j, q], (*chip, c))
                          for j, chip in enumerate(chips)]
                for cp in first:
                    cp.start()
                started += first
                locals_.append(local)
        for b in range(nb):
            half = ins[b].shape[0] // 2
            for q, (off, n) in enumerate(_chunks(half, NQ_ICI)):
                mine = pl.ds(c * half + off, n)
                for j, chip in enumerate(chips):
                    landed = outs[b].at[2 * chip[0] + chip[1], mine, :]
                    _remote(landed, landed, ssem.at[b, 1 + j, q], rsem.at[b, 1 + j, q], me).wait_recv()
                    cp = _remote(landed, landed, ssem.at[b, 4 + j, q], rsem.at[b, 4 + j, q], sib)
                    cp.start()
                    started.append(cp)
        for b in range(nb):
            half = ins[b].shape[0] // 2
            for q, (off, n) in enumerate(_chunks(half, NQ_ICI)):
                other = pl.ds((1 - c) * half + off, n)
                theirs = outs[b].at[2 * x + y, other, :]
                _remote(theirs, theirs, ssem.at[b, 0, q], rsem.at[b, 0, q], me).wait_recv()
                for j, chip in enumerate(chips):
                    fwd = outs[b].at[2 * chip[0] + chip[1], other, :]
                    _remote(fwd, fwd, ssem.at[b, 4 + j, q], rsem.at[b, 4 + j, q], me).wait_recv()
        for cp in started:
            cp.wait_send()
        for cp in locals_:
            cp.wait()

    return pl.pallas_call(
        body, name="all_gather_weights", in_specs=[ANY] * nb, out_specs=[ANY] * nb,
        out_shape=[SDS((NCH,) + s.shape, s.dtype) for s in shards],
        scratch_shapes=[pltpu.SemaphoreType.DMA((nb, 7, NQ_ICI)), pltpu.SemaphoreType.DMA((nb, 7, NQ_ICI)),
                        pltpu.SemaphoreType.DMA((nb, NQ_ICI))],
    )(*shards)


def _send_sibling_half(parts):
    nb = len(parts)

    def body(*refs):
        ins, gots = refs[:nb], refs[nb:2 * nb]
        ssem, rsem = refs[2 * nb:]
        x, y, c, _ = _place()
        sib = (x, y, 1 - c)
        todo = []
        for b in range(nb):
            half = ins[b].shape[1] // 2
            for q, (off, n) in enumerate(_chunks(half, NQ_D2D)):
                cp = _remote(ins[b].at[:, pl.ds((1 - c) * half + off, n), :], gots[b].at[:, pl.ds(off, n), :],
                             ssem.at[b, q], rsem.at[b, q], sib)
                cp.start()
                todo.append(cp)
        for cp in todo:
            cp.wait()

    return pl.pallas_call(
        body, name="send_sibling_half", in_specs=[ANY] * nb, out_specs=[ANY] * nb,
        out_shape=[SDS((p.shape[0], p.shape[1] // 2, p.shape[2]), p.dtype) for p in parts],
        scratch_shapes=[pltpu.SemaphoreType.DMA((nb, NQ_D2D)), pltpu.SemaphoreType.DMA((nb, NQ_D2D))],
    )(*parts)


def _scatter_to_chips(parts):
    nb = len(parts)

    def body(*refs):
        ins, outs, sems = refs[:nb], refs[nb:2 * nb], refs[2 * nb:]
        _scatter_start(ins, outs, *sems)
        _scatter_finish(ins, outs, *sems)

    return pl.pallas_call(
        body, name="scatter_to_chips", in_specs=[ANY] * nb, out_specs=[ANY] * nb,
        out_shape=[SDS(p.shape, p.dtype) for p in parts], scratch_shapes=_scatter_sems(nb),
    )(*parts)


def _join_halves(fulls, ranges):
    nb = len(fulls)
    nr = max(len(r) for r in ranges)

    def body(*refs):
        ins, outs = refs[:nb], refs[nb:2 * nb]
        ssem, rsem = refs[2 * nb:]
        x, y, c, _ = _place()
        sib = (x, y, 1 - c)
        sends, lands = [], []
        for b in range(nb):
            for g, (row0, rows) in enumerate(ranges[b]):
                half = rows // 2
                for q, (off, n) in enumerate(_chunks(half, NQ_D2D)):
                    mine = pl.ds(row0 + c * half + off, n)
                    sends.append(_remote(ins[b].at[mine, :], outs[b].at[mine, :], ssem.at[b, g, q], rsem.at[b, g, q], sib))
                    other = outs[b].at[pl.ds(row0 + (1 - c) * half + off, n), :]
                    lands.append(_remote(other, other, ssem.at[b, g, q], rsem.at[b, g, q], sib))
        for cp in sends:
            cp.start()
        for cp in lands:
            cp.wait_recv()
        for cp in sends:
            cp.wait_send()

    return pl.pallas_call(
        body, name="join_halves", in_specs=[ANY] * nb, out_specs=[ANY] * nb,
        out_shape=[SDS(h.shape, h.dtype) for h in fulls], input_output_aliases={b: b for b in range(nb)},
        scratch_shapes=[pltpu.SemaphoreType.DMA((nb, nr, NQ_D2D)), pltpu.SemaphoreType.DMA((nb, nr, NQ_D2D))],
    )(*fulls)


def _gather_small(vec):
    def body(v_ref, o_ref, ssem, rsem, lsem):
        x, y, c, _ = _place()
        mine = o_ref.at[4 * x + 2 * y + c]
        local = pltpu.make_async_copy(v_ref, mine, lsem)
        local.start()
        sends = []
        for k in range(1, 8):
            peer = (x ^ (k >> 2), y ^ ((k >> 1) & 1), c ^ (k & 1))
            cp = _remote(v_ref, mine, ssem.at[k - 1], rsem.at[k - 1], peer)
            cp.start()
            sends.append(cp)
        for k in range(1, 8):
            px, py, pc = x ^ (k >> 2), y ^ ((k >> 1) & 1), c ^ (k & 1)
            slot = o_ref.at[4 * px + 2 * py + pc]
            _remote(slot, slot, ssem.at[k - 1], rsem.at[k - 1], (x, y, c)).wait_recv()
        for cp in sends:
            cp.wait_send()
        local.wait()

    return pl.pallas_call(
        body, name="gather_small", in_specs=[ANY], out_specs=ANY, out_shape=SDS((8,) + vec.shape, vec.dtype),
        scratch_shapes=[pltpu.SemaphoreType.DMA((7,)), pltpu.SemaphoreType.DMA((7,)), pltpu.SemaphoreType.DMA],
    )(vec)


def _block_diag(pw):
    return jnp.concatenate([jnp.pad(pw[g], ((0, 0), (64 * g, 192 - 64 * g))) for g in range(4)], axis=0)


def _own_columns(full, chip):
    n = full.shape[-1] // NCH
    parts = full.reshape(full.shape[:-1] + (NCH, n))
    sel = (lax.broadcasted_iota(jnp.int32, (NCH, 1), 0) == chip)
    return jnp.sum(jnp.where(sel, parts, 0.0), axis=-2)


def _at_own_columns(shard, chip):
    n = shard.shape[-1]
    sel = (lax.broadcasted_iota(jnp.int32, (NCH * n,), 0) // n == chip)
    return jnp.where(sel, jnp.tile(shard, NCH), 0.0)


def _pad_rows(a, rows):
    return jnp.pad(a, ((0, rows - a.shape[0]),) + ((0, 0),) * (a.ndim - 1))


def _ffn_block(l, which):
    return 7 * l + 3 * which


def _wout_block(l):
    return 7 * l + 6


def _layer_fwd(l, x0, pos, freq, wts, blob):
    sv = {"x0": x0}
    x1 = _ffn_fwd(x0, wts["ffn1_norm"][l:l + 1], blob, _ffn_block(l, 0))
    att, att4, att16, pu, dq, dz, dba = _inproj_fwd(x1, wts["mix_norm"][l:l + 1], wts["w_aug"][l], pos, freq)
    qkvs = [att, att4.reshape(S, 768), att16.reshape(S, 768)]
    (o1, l1), (o4, l4), (o16, l16) = [_attn_fwd(q, NB // d) for q, d in zip(qkvs, PATTERN_DIL)]
    ols = (o1, l1, o4.reshape(4, S // 4, ATT), l4.reshape(4, S // 4, ATT), o16.reshape(16, S // 16, ATT),
           l16.reshape(16, S // 16, ATT))
    ya = _merge_fwd(*ols)
    yb = _pool_fwd(pu, wts["pool_bd"][l], wts["pool_scale"][l:l + 1])
    c = _conv_fwd(dq, wts["conv_w"][l])
    u, w, qg, kd, intra, aux = _dn_local_fwd(c, dba, wts["dn_par"][l])
    o_dn, states = _dn_rec_fwd(u, w, qg, kd, intra, aux)
    yc = _dn_post_fwd(o_dn, dz, wts["dn_out_norm"][l:l + 1])
    x2 = _outproj_fwd(x1, ya, yb, yc, blob, _wout_block(l))
    x3 = _ffn_fwd(x2, wts["ffn2_norm"][l:l + 1], blob, _ffn_block(l, 1))
    sv.update(x1=x1, x2=x2, qkvs=qkvs, ols=ols, ya=ya, yb=yb, yc=yc, pu=pu, dq=dq, dz=dz, dba=dba, c=c,
              u=u, w=w, qg=qg, kd=kd, intra=intra, aux=aux, states=states, o_dn=o_dn)
    return x3, sv


def _wout_part(g):
    return jnp.pad(g.astype(bf16).reshape(NCH, 256, D), ((0, 0), (0, FC - 256), (0, 0)))


def _win_part(g):
    return g[:, :INW].astype(bf16).reshape(D, NCH, INC).transpose(1, 0, 2)


def _layer_bwd(l, dx3, sv, pos, freq, wts, blob, ride=None, prep=None):
    gr = {}
    g2, u2, d2, dh4, *pieces_before = _ffn_bwd(sv["x2"], wts["ffn2_norm"][l:l + 1], blob, _ffn_block(l, 1), dx3, ride)
    dx2, gr["ffn2_norm"] = _norm_bwd(sv["x2"], wts["ffn2_norm"][l:l + 1], dx3, dh4)
    gr.update(ffn2_w_gate=g2, ffn2_w_up=u2, ffn2_w_down=d2)
    dya, dyb, dyc, gr["w_out"] = _outproj_bwd(dx2, sv["ya"], sv["yb"], sv["yc"], blob, _wout_block(l))
    do_dn, ddz, gr["dn_out_norm"] = _dn_post_bwd(sv["o_dn"], sv["dz"], wts["dn_out_norm"][l:l + 1], dyc)
    du, dw, dqg, dkd, dintra, daux = _dn_rec_bwd(sv["u"], sv["w"], sv["qg"], sv["kd"], sv["intra"], sv["aux"], sv["states"], do_dn)
    dc, ddba, gr["dn_par"] = _dn_local_bwd(sv["c"], sv["dba"], wts["dn_par"][l], du, dw, dqg, dkd, dintra, daux)
    ddq, gr["conv_w"] = _conv_bwd(sv["dq"], wts["conv_w"][l], dc)
    dpu, gr["pool_bd"], gr["pool_scale"] = _pool_bwd(sv["pu"], wts["pool_bd"][l], wts["pool_scale"][l:l + 1], dyb)
    dols = _merge_bwd(*sv["ols"], dya)
    datts = [_attn_bwd(q, dols[2 * p].reshape(S, ATT), dols[2 * p + 1].reshape(S, ATT), NB // d)
             for p, (q, d) in enumerate(zip(sv["qkvs"], PATTERN_DIL))]
    dx1, gr["mix_norm"], gr["w_aug"] = _inproj_bwd(sv["x1"], wts["mix_norm"][l:l + 1], wts["w_aug"][l], pos, freq, dx2,
                                                    datts[0], datts[1].reshape(4, S // 4, 768),
                                                    datts[2].reshape(16, S // 16, 768), dpu, ddq, ddz, ddba)
    own = None
    if prep is not None:
        own = prep([jnp.concatenate([g2, u2, d2, _wout_part(gr["w_out"])], axis=1), _win_part(gr["w_aug"])])
    g1, u1, d1, dh4, *pieces_own = _ffn_bwd(sv["x0"], wts["ffn1_norm"][l:l + 1], blob, _ffn_block(l, 0), dx1, own)
    dx0, gr["ffn1_norm"] = _norm_bwd(sv["x0"], wts["ffn1_norm"][l:l + 1], dx1, dh4)
    gr.update(ffn1_w_gate=g1, ffn1_w_up=u1, ffn1_w_down=d1)
    return dx0, gr, pieces_before, pieces_own


def _device_step(x, pos, target, wts, blob, prep=None):
    freq = jnp.tile(ROPE_THETA ** (-jnp.arange(0, EH, 2, dtype=f32) / EH), 2 * ATT // EH).reshape(1, ATT)
    saved = []
    h = x
    for l in range(DEPTH):
        h, sv = _layer_fwd(l, h, pos, freq, wts, blob)
        saved.append(sv)
    dh, g_final, loss = _final(h, wts["final_norm"], target)
    grads = [None] * DEPTH
    dh, grads[1], _, _ = _layer_bwd(1, dh, saved[1], pos, freq, wts, blob)
    sums1 = None
    if prep is not None:
        g = grads[1]
        ffn = [g[f"ffn{f}_w_{n}"] for f in (1, 2) for n in ("gate", "up", "down")]
        sums1 = prep([jnp.concatenate(ffn + [_wout_part(g["w_out"])], axis=1), _win_part(g["w_aug"])])
    dh, grads[0], pieces1, pieces0 = _layer_bwd(0, dh, saved[0], pos, freq, wts, blob, sums1, prep)
    return loss, dh, g_final, grads, pieces1, pieces0


_SMALL = (("ffn1_norm", (DEPTH, D)), ("mix_norm", (DEPTH, D)), ("pool_w", (DEPTH, 4, 64, 64)), ("pool_scale", (DEPTH, 256)),
          ("dn_conv_w", (DEPTH, 4, CW)), ("dn_a_log", (DEPTH, 4)), ("dn_dt_bias", (DEPTH, 4)), ("dn_out_norm", (DEPTH, 128)),
          ("ffn2_norm", (DEPTH, D)), ("final_norm", (D,)), ("loss", (1,)))


def _pack_small(vals):
    rows = []
    for name, shape in _SMALL:
        flat = vals[name].astype(f32).reshape(-1)
        rows.append(jnp.pad(flat, (0, _small_rows(shape) * 128 - flat.shape[0])).reshape(-1, 128))
    out = jnp.concatenate(rows, axis=0)
    return _pad_rows(out, -(-out.shape[0] // 16) * 16)


def _small_rows(shape):
    return -(-int(np.prod(shape)) // 1024) * 8


def _unpack_small(packed):
    vals, r = {}, 0
    for name, shape in _SMALL:
        size, n = int(np.prod(shape)), _small_rows(shape)
        vals[name] = packed[r:r + n].reshape(-1)[:size].reshape(shape)
        r += n
    return vals


def kernel(x, positions, ffn1_norm, ffn1_w_gate, ffn1_w_up, ffn1_w_down, mix_norm, w_in, pool_w, pool_scale, dn_conv_w, dn_a_log, dn_dt_bias, dn_out_norm, w_out, ffn2_norm, ffn2_w_gate, ffn2_w_up, ffn2_w_down, final_norm, loss_target, m_ffn1_norm, m_ffn1_w_gate, m_ffn1_w_up, m_ffn1_w_down, m_mix_norm, m_w_in, m_pool_w, m_pool_scale, m_dn_conv_w, m_dn_a_log, m_dn_dt_bias, m_dn_out_norm, m_w_out, m_ffn2_norm, m_ffn2_w_gate, m_ffn2_w_up, m_ffn2_w_down, m_final_norm, v_ffn1_norm, v_ffn1_w_gate, v_ffn1_w_up, v_ffn1_w_down, v_mix_norm, v_w_in, v_pool_w, v_pool_scale, v_dn_conv_w, v_dn_a_log, v_dn_dt_bias, v_dn_out_norm, v_w_out, v_ffn2_norm, v_ffn2_w_gate, v_ffn2_w_up, v_ffn2_w_down, v_final_norm):
    names = ["ffn1_norm", "ffn1_w_gate", "ffn1_w_up", "ffn1_w_down", "mix_norm", "w_in", "pool_w", "pool_scale", "dn_conv_w",
             "dn_a_log", "dn_dt_bias", "dn_out_norm", "w_out", "ffn2_norm", "ffn2_w_gate", "ffn2_w_up", "ffn2_w_down", "final_norm"]
    W = dict(zip(names, [ffn1_norm, ffn1_w_gate, ffn1_w_up, ffn1_w_down, mix_norm, w_in, pool_w, pool_scale, dn_conv_w,
                         dn_a_log, dn_dt_bias, dn_out_norm, w_out, ffn2_norm, ffn2_w_gate, ffn2_w_up, ffn2_w_down, final_norm]))
    M = dict(zip(names, [m_ffn1_norm, m_ffn1_w_gate, m_ffn1_w_up, m_ffn1_w_down, m_mix_norm, m_w_in, m_pool_w, m_pool_scale,
                         m_dn_conv_w, m_dn_a_log, m_dn_dt_bias, m_dn_out_norm, m_w_out, m_ffn2_norm, m_ffn2_w_gate, m_ffn2_w_up,
                         m_ffn2_w_down, m_final_norm]))
    V = dict(zip(names, [v_ffn1_norm, v_ffn1_w_gate, v_ffn1_w_up, v_ffn1_w_down, v_mix_norm, v_w_in, v_pool_w, v_pool_scale,
                         v_dn_conv_w, v_dn_a_log, v_dn_dt_bias, v_dn_out_norm, v_w_out, v_ffn2_norm, v_ffn2_w_gate, v_ffn2_w_up,
                         v_ffn2_w_down, v_final_norm]))
    chip = 2 * lax.axis_index("x") + lax.axis_index("y")

    ffn_names = [(f"ffn{f}_w_gate", f"ffn{f}_w_up", f"ffn{f}_w_down") for f in (1, 2)]
    tr = lambda t: jnp.swapaxes(t, -1, -2)
    rows = []
    for l in range(DEPTH):
        for g, u, dn in ffn_names:
            rows += [tr(W[g][l]), tr(W[u][l]), W[dn][l]]
        rows.append(jnp.pad(W["w_out"][l], ((0, FC - 256), (0, 0))))
    sh_b = jnp.concatenate(rows, axis=0).astype(bf16)
    sh_c = jnp.concatenate([W["w_in"][l] for l in range(DEPTH)], axis=0).astype(bf16)
    blob, blob_c = _all_gather_weights([sh_b, sh_c])
    conv_all = _gather_small(_pad_rows(dn_conv_w.reshape(DEPTH * 4 * (CW // NCH) // 128, 128), 32))
    conv_full = jnp.concatenate([conv_all[2 * j, :DEPTH * 4 * (CW // NCH) // 128].reshape(DEPTH, 4, CW // NCH) for j in range(NCH)],
                                axis=-1)

    win = blob_c.reshape(NCH, DEPTH, D, INC).transpose(1, 2, 0, 3).reshape(DEPTH, D, INW)
    w_aug = jnp.concatenate([win, jnp.zeros((DEPTH, D, INP - INW), bf16)], axis=-1)
    par = jnp.pad(jnp.stack([dn_a_log, dn_dt_bias], axis=1), ((0, 0), (0, 6), (4, 120)))
    wts = dict(ffn1_norm=ffn1_norm, mix_norm=mix_norm, ffn2_norm=ffn2_norm, final_norm=final_norm.reshape(1, D),
               w_aug=w_aug, pool_bd=jnp.stack([_block_diag(pool_w[l]) for l in range(DEPTH)]).astype(bf16),
               pool_scale=pool_scale, conv_w=jnp.pad(conv_full, ((0, 0), (0, 4), (0, 0))),
               dn_par=par, dn_out_norm=dn_out_norm)

    c_arr = lax.axis_index("c").astype(jnp.int32).reshape(1)

    def prep(parts):
        return [_sum_core_pair(p, g, c_arr) for p, g in zip(parts, _send_sibling_half(parts))]

    loss, dx, g_final, grads, pieces1, pieces0 = _device_step(x[0], positions.reshape(S, 1), loss_target[0], wts, blob, prep)
    last = [jnp.concatenate([grads[0][n] for n in ffn_names[0]], axis=1)]
    pieces_last = _scatter_to_chips(prep(last))
    full_b = _sum_chips(pieces_last[0], c_arr, None, 0, RB)
    full_b = _sum_chips(pieces0[0], c_arr, full_b, 3 * FC, RB)
    full_b = _sum_chips(pieces1[0], c_arr, full_b, 7 * FC, RB)
    full_c = _sum_chips(pieces0[1], c_arr, None, 0, RC)
    full_c = _sum_chips(pieces1[1], c_arr, full_c, D, RC)
    full_b, full_c = _join_halves([full_b, full_c], [[(0, 3 * FC), (3 * FC, 4 * FC), (7 * FC, 7 * FC)], [(0, D), (D, D)]])

    small = {"loss": loss[0, 0:1], "final_norm": g_final.reshape(D)}
    for n in ("ffn1_norm", "mix_norm", "ffn2_norm", "pool_scale", "dn_out_norm"):
        small[n] = jnp.stack([grads[l][n].reshape(-1) for l in range(DEPTH)])
    small["pool_w"] = jnp.stack([jnp.stack([grads[l]["pool_bd"][64 * g:64 * (g + 1), 64 * g:64 * (g + 1)] for g in range(4)])
                                 for l in range(DEPTH)])
    small["dn_conv_w"] = jnp.stack([grads[l]["conv_w"][0:4] for l in range(DEPTH)])
    small["dn_a_log"] = jnp.stack([grads[l]["dn_par"][0, 4:8] for l in range(DEPTH)])
    small["dn_dt_bias"] = jnp.stack([grads[l]["dn_par"][1, 4:8] for l in range(DEPTH)])
    packed = _pack_small(small)
    g_small = _sum_pieces(_gather_small(packed), f32, "sum_small")
    gs = _unpack_small(g_small)

    transposed = ("ffn1_w_gate", "ffn1_w_up", "ffn2_w_gate", "ffn2_w_up")
    where = {"ffn1_w_gate": (full_b, FC // 2, lambda l: 14 * l), "ffn1_w_up": (full_b, FC // 2, lambda l: 14 * l + 2),
             "ffn1_w_down": (full_b, FC // 2, lambda l: 14 * l + 4), "ffn2_w_gate": (full_b, FC // 2, lambda l: 14 * l + 6),
             "ffn2_w_up": (full_b, FC // 2, lambda l: 14 * l + 8), "ffn2_w_down": (full_b, FC // 2, lambda l: 14 * l + 10),
             "w_out": (full_b, 64, lambda l: (FC // 64) * (7 * l + 6)), "w_in": (full_c, D // 2, lambda l: 2 * l)}
    big_res = {}
    for n, (gblob, tile, first) in where.items():
        t = tr if n in transposed else (lambda a: a)
        big_res[n] = [t(r) for r in _adamw_rows(t(W[n]), t(M[n]), t(V[n]), gblob, tile, first, "adamw_" + n)]

    def small_of(T):
        d = {n: T[n] for n, _ in _SMALL if n not in ("loss", "dn_conv_w")}
        d["loss"] = jnp.zeros((1,), f32)
        d["dn_conv_w"] = _at_own_columns(T["dn_conv_w"], chip)
        return _pack_small(d)

    res_s = _adamw(small_of(W), g_small, small_of(M), small_of(V), "adamw_small")
    small_out = [_unpack_small(r) for r in res_s]

    def split_blobs(b, c):
        out = {}
        b7 = b.reshape(DEPTH, 7, FC, D)
        for k, n in enumerate(n for names3 in ffn_names for n in names3):
            out[n] = tr(b7[:, k]) if n in transposed else b7[:, k]
        out["w_out"] = b7[:, 6, :256]
        out["w_in"] = c.reshape(DEPTH, D, INC)
        return out

    def assemble(big, sm):
        out = []
        for n in names:
            if n in big:
                out.append(big[n])
            elif n == "dn_conv_w":
                out.append(_own_columns(sm[n], chip))
            else:
                out.append(sm[n])
        return out

    grad_list = assemble(split_blobs(full_b, full_c), gs)
    outs = [gs["loss"].reshape(()), dx.reshape(1, S, D)] + grad_list
    for k in range(3):
        outs += assemble({n: r[k] for n, r in big_res.items()}, small_out[k])
    return tuple(outs)
```

```python
import functools
import math

import jax
import jax.numpy as jnp
import numpy as np
from jax import lax
from jax.experimental import pallas as pl
from jax.experimental.pallas import tpu as pltpu

f32 = jnp.float32
bf16 = jnp.bfloat16
SDS = jax.ShapeDtypeStruct
MESH = pl.DeviceIdType.MESH

S = 4096
D = 1024
DEPTH = 2
FF = 2816
NCH = 4
FC = FF // NCH
INW = 3080
INC = INW // NCH
INP = 3200
ATT = 256
EH = 64
NBLK = 128
DNW = 512
DH = 128
CH = 64
NCHUNK = S // CH
EPS = 1e-6
ROPE_THETA = 10000.0
PATTERN_DIL = (1, 4, 16)
ADAM_LR, ADAM_B1, ADAM_B2, ADAM_EPS, ADAM_WD, ADAM_STEP = 0.001, 0.9, 0.999, 1e-08, 0.01, 10
VMEM_BYTES_V7X = 64 * 1024 * 1024
NEG = -1e30

TM = 512
RB, RC = 14 * FC, 2 * D


def _cp(vmem_mb=48, sem=None):
    kw = dict(vmem_limit_bytes=vmem_mb * 1024 * 1024)
    if sem is not None:
        kw["dimension_semantics"] = sem
    return pltpu.CompilerParams(**kw)


def _dot(a, b):
    return jnp.dot(a, b, preferred_element_type=f32)


def _dot_nt(a, b):
    return lax.dot_general(a, b, (((1,), (1,)), ((), ())), preferred_element_type=f32)


def _dot_tn(a, b):
    return lax.dot_general(a, b, (((0,), (0,)), ((), ())), preferred_element_type=f32)


def _rms(x, w):
    r = lax.rsqrt(jnp.mean(x * x, axis=-1, keepdims=True) + EPS)
    return x * r * w, r


def _rms_bwd(x, w, r, dh):
    xhat = x * r
    dw = jnp.sum(dh * xhat, axis=0, keepdims=True)
    dxh = dh * w
    dx = r * (dxh - xhat * jnp.mean(dxh * xhat, axis=-1, keepdims=True))
    return dx, dw


def _ffn_fwd(x, nw, blob, k0, ride=None):
    kg, ku, kd = k0, k0 + 1, k0 + 2
    nr = 0 if ride is None else len(ride)
    ni = S // TM

    def body(*refs):
        x_ref, nw_ref, wg_ref, wu_ref, wd_ref = refs[:5]
        ride_in = refs[5:5 + nr]
        o_ref = refs[5 + nr]
        ride_out = refs[6 + nr:6 + 2 * nr]
        h_scr, acc_scr = refs[6 + 2 * nr:8 + 2 * nr]
        sems = refs[8 + 2 * nr:]
        i = pl.program_id(0)
        j = pl.program_id(1)

        if nr:
            @pl.when(jnp.logical_and(i == 0, j == 0))
            def _():
                _gather_start(ride_in, ride_out, *sems)

        @pl.when(j == 0)
        def _():
            h, _ = _rms(x_ref[...], nw_ref[...])
            h_scr[...] = h.astype(bf16)
            acc_scr[...] = jnp.zeros_like(acc_scr)

        h = h_scr[...]
        g = _dot_nt(h, wg_ref[0])
        u = _dot_nt(h, wu_ref[0])
        a = (g * jax.nn.sigmoid(g) * u).astype(bf16)
        acc_scr[...] += _dot(a, wd_ref[0])

        @pl.when(j == NCH - 1)
        def _():
            o_ref[...] = x_ref[...] + 0.5 * acc_scr[...]

        if nr:
            @pl.when(jnp.logical_and(i == ni - 1, j == NCH - 1))
            def _():
                _gather_finish(ride_in, ride_out, *sems)

    wspec = lambda k: pl.BlockSpec((1, FC, D), lambda i, j: (j, k, 0))
    rides = [] if ride is None else list(ride)
    res = pl.pallas_call(
        body, grid=(ni, NCH), name="ffn_fwd_ride" if nr else "ffn_fwd",
        in_specs=[pl.BlockSpec((TM, D), lambda i, j: (i, 0)),
                  pl.BlockSpec((1, D), lambda i, j: (0, 0)),
                  wspec(kg), wspec(ku), wspec(kd)] + [ANY] * nr,
        out_specs=[pl.BlockSpec((TM, D), lambda i, j: (i, 0))] + [ANY] * nr,
        out_shape=[SDS((S, D), f32)] + [SDS((NCH,) + r.shape, r.dtype) for r in rides],
        scratch_shapes=[pltpu.VMEM((TM, D), bf16), pltpu.VMEM((TM, D), f32)] + (_gather_sems(nr) if nr else []),
        compiler_params=_cp(40),
    )(x, nw, blob, blob, blob, *rides)
    return res if nr else res[0]


def _ffn_bwd(x, nw, blob, k0, dy, ride=None):
    nt = S // TM
    kg, ku, kd = k0, k0 + 1, k0 + 2
    nr = 0 if ride is None else len(ride)

    def body(*refs):
        x_ref, nw_ref, wg_ref, wu_ref, wd_ref, dy_ref = refs[:6]
        ride_in = refs[6:6 + nr]
        dwg_ref, dwu_ref, dwd_ref, dh_ref = refs[6 + nr:10 + nr]
        ride_out = refs[10 + nr:10 + 2 * nr]
        ag, au, ad = refs[10 + 2 * nr:13 + 2 * nr]
        sems = refs[13 + 2 * nr:]
        j = pl.program_id(0)
        i = pl.program_id(1)

        if nr:
            @pl.when(jnp.logical_and(j == 0, i == 0))
            def _():
                _scatter_start(ride_in, ride_out, *sems)

        @pl.when(i == 0)
        def _():
            ag[...] = jnp.zeros_like(ag)
            au[...] = jnp.zeros_like(au)
            ad[...] = jnp.zeros_like(ad)

        hf, _ = _rms(x_ref[...], nw_ref[...])
        h = hf.astype(bf16)
        g = _dot_nt(h, wg_ref[0])
        u = _dot_nt(h, wu_ref[0])
        sg = jax.nn.sigmoid(g)
        s = g * sg
        a = (s * u).astype(bf16)
        dyb = (0.5 * dy_ref[...]).astype(bf16)
        da = _dot_nt(dyb, wd_ref[0])
        ad[...] += _dot_tn(a, dyb)
        du = (da * s).astype(bf16)
        dg = (da * u * (sg * (1.0 + g * (1.0 - sg)))).astype(bf16)
        ag[...] += _dot_tn(dg, h)
        au[...] += _dot_tn(du, h)
        dh_ref[0] = (_dot(dg, wg_ref[0]) + _dot(du, wu_ref[0])).astype(bf16)

        @pl.when(i == nt - 1)
        def _():
            dwg_ref[0] = ag[...].astype(bf16)
            dwu_ref[0] = au[...].astype(bf16)
            dwd_ref[0] = ad[...].astype(bf16)

        if nr:
            @pl.when(jnp.logical_and(j == NCH - 1, i == nt - 1))
            def _():
                _scatter_finish(ride_in, ride_out, *sems)

    wspec = lambda k: pl.BlockSpec((1, FC, D), lambda j, i: (j, k, 0))
    gspec = pl.BlockSpec((1, FC, D), lambda j, i: (j, 0, 0))
    rides = [] if ride is None else list(ride)
    return pl.pallas_call(
        body, grid=(NCH, nt), name="ffn_bwd_ride" if nr else "ffn_bwd",
        in_specs=[pl.BlockSpec((TM, D), lambda j, i: (i, 0)),
                  pl.BlockSpec((1, D), lambda j, i: (0, 0)),
                  wspec(kg), wspec(ku), wspec(kd),
                  pl.BlockSpec((TM, D), lambda j, i: (i, 0))] + [ANY] * nr,
        out_specs=[gspec, gspec, gspec, pl.BlockSpec((1, TM, D), lambda j, i: (j, i, 0))] + [ANY] * nr,
        out_shape=[SDS((NCH, FC, D), bf16)] * 3 + [SDS((NCH, S, D), bf16)] + [SDS(r.shape, r.dtype) for r in rides],
        scratch_shapes=[pltpu.VMEM((FC, D), f32)] * 3 + (_scatter_sems(nr) if nr else []),
        compiler_params=_cp(56),
    )(x, nw, blob, blob, blob, dy, *rides)


def _norm_bwd(x, nw, dres, dh4):
    nt = S // TM
    nparts = dh4.shape[0]

    def body(x_ref, nw_ref, dres_ref, dh_ref, dx_ref, dnw_ref):
        i = pl.program_id(0)
        dh = dh_ref[0].astype(f32)
        for p in range(1, nparts):
            dh = dh + dh_ref[p].astype(f32)
        xv = x_ref[...]
        _, r = _rms(xv, nw_ref[...])
        dx, dw = _rms_bwd(xv, nw_ref[...], r, dh)
        dx_ref[...] = dres_ref[...] + dx

        @pl.when(i == 0)
        def _():
            dnw_ref[...] = jnp.zeros_like(dnw_ref)

        dnw_ref[...] += dw

    return pl.pallas_call(
        body, grid=(nt,), name="norm_bwd",
        in_specs=[pl.BlockSpec((TM, D), lambda i: (i, 0)),
                  pl.BlockSpec((1, D), lambda i: (0, 0)),
                  pl.BlockSpec((TM, D), lambda i: (i, 0)),
                  pl.BlockSpec((nparts, TM, D), lambda i: (0, i, 0))],
        out_specs=[pl.BlockSpec((TM, D), lambda i: (i, 0)), pl.BlockSpec((1, D), lambda i: (0, 0))],
        out_shape=[SDS((S, D), f32), SDS((1, D), f32)],
        compiler_params=_cp(40),
    )(x, nw, dres, dh4)


def _final(x, nw, target):
    nt = S // TM

    def body(x_ref, nw_ref, t_ref, dx_ref, dnw_ref, loss_ref):
        i = pl.program_id(0)
        xv = x_ref[...]
        y, r = _rms(xv, nw_ref[...])
        err = y - t_ref[...]
        part = 0.5 * jnp.sum(jnp.mean(err * err, axis=-1, keepdims=True), axis=0, keepdims=True)
        dx, dw = _rms_bwd(xv, nw_ref[...], r, err * (1.0 / D))
        dx_ref[...] = dx

        @pl.when(i == 0)
        def _():
            dnw_ref[...] = jnp.zeros_like(dnw_ref)
            loss_ref[...] = jnp.zeros_like(loss_ref)

        dnw_ref[...] += dw
        loss_ref[...] += jnp.broadcast_to(part, loss_ref.shape)

    return pl.pallas_call(
        body, grid=(nt,), name="final_loss",
        in_specs=[pl.BlockSpec((TM, D), lambda i: (i, 0)),
                  pl.BlockSpec((1, D), lambda i: (0, 0)),
                  pl.BlockSpec((TM, D), lambda i: (i, 0))],
        out_specs=[pl.BlockSpec((TM, D), lambda i: (i, 0)), pl.BlockSpec((1, D), lambda i: (0, 0)),
                   pl.BlockSpec((1, 128), lambda i: (0, 0))],
        out_shape=[SDS((S, D), f32), SDS((1, D), f32), SDS((1, 128), f32)],
        compiler_params=_cp(40),
    )(x, nw, target)


def _rot_half(t):
    lane = lax.broadcasted_iota(jnp.int32, t.shape, 1)
    first = (lane % EH) < (EH // 2)
    return jnp.where(first, -pltpu.roll(t, ATT - EH // 2, 1), pltpu.roll(t, EH // 2, 1))


def _rope_tables(pos_ref, freq_ref):
    ang = pos_ref[...].astype(f32) * freq_ref[...]
    return jnp.cos(ang), jnp.sin(ang)


def _split_residues(val, scr, outs):
    rows, cols = val.shape
    for j in range(cols // 128):
        scr[j] = val[:, 128 * j:128 * (j + 1)]
    for ref, d in outs:
        for j in range(cols // 128):
            for r in range(d):
                ref.at[r][:, 128 * j:128 * (j + 1)] = scr.at[j][pl.ds(r, rows // d, stride=d), :]


def _join_residues(ref, d, scr):
    rows, cols = scr.shape[1], ref.shape[2]
    for j in range(cols // 128):
        for r in range(d):
            scr.at[j][pl.ds(r, rows // d, stride=d), :] = ref.at[r][:, 128 * j:128 * (j + 1)]
    return jnp.concatenate([scr[j] for j in range(cols // 128)], axis=1)


def _res_spec(d, tile, cols):
    return pl.BlockSpec((d, tile // d, cols), lambda i: (0, i, 0))


def _inproj_fwd(x, nw, w_aug, pos, freq):
    TI = 256

    def body(x_ref, nw_ref, w_hbm, pos_ref, freq_ref, att_ref, att4_ref, att16_ref, pu_ref, dq_ref, dz_ref, dba_ref,
             w_scr, r_scr):
        @pl.when(pl.program_id(0) == 0)
        def _():
            pltpu.sync_copy(w_hbm, w_scr)

        h, _ = _rms(x_ref[...], nw_ref[...])
        proj = _dot(h.astype(bf16), w_scr[...])
        cos, sin = _rope_tables(pos_ref, freq_ref)
        q = proj[:, 0:ATT]
        k = proj[:, ATT:2 * ATT]
        att = jnp.concatenate([q * cos + _rot_half(q) * sin, k * cos + _rot_half(k) * sin, proj[:, 2 * ATT:3 * ATT]], axis=1)
        att_ref[...] = att
        _split_residues(att, r_scr, [(att4_ref, 4), (att16_ref, 16)])
        pu_ref[...] = proj[:, 768:1024]
        dq_ref[...] = proj[:, 1024:2560]
        dz_ref[...] = proj[:, 2560:3072]
        dba_ref[...] = proj[:, 3072:3200]

    return pl.pallas_call(
        body, grid=(S // TI,), name="inproj_fwd",
        in_specs=[pl.BlockSpec((TI, D), lambda i: (i, 0)),
                  pl.BlockSpec((1, D), lambda i: (0, 0)),
                  pl.BlockSpec(memory_space=pl.ANY),
                  pl.BlockSpec((TI, 1), lambda i: (i, 0)),
                  pl.BlockSpec((1, ATT), lambda i: (0, 0))],
        out_specs=[pl.BlockSpec((TI, 768), lambda i: (i, 0)), _res_spec(4, TI, 768), _res_spec(16, TI, 768),
                   pl.BlockSpec((TI, 256), lambda i: (i, 0)),
                   pl.BlockSpec((TI, 1536), lambda i: (i, 0)), pl.BlockSpec((TI, 512), lambda i: (i, 0)),
                   pl.BlockSpec((TI, 128), lambda i: (i, 0))],
        out_shape=[SDS((S, 768), f32), SDS((4, S // 4, 768), f32), SDS((16, S // 16, 768), f32), SDS((S, 256), f32),
                   SDS((S, 1536), f32), SDS((S, 512), f32), SDS((S, 128), f32)],
        scratch_shapes=[pltpu.VMEM((D, INP), bf16), pltpu.VMEM((6, TI, 128), f32)],
        compiler_params=_cp(48),
    )(x, nw, w_aug, pos, freq)


def _inproj_bwd(x, nw, w_aug, pos, freq, dres, datt, datt4, datt16, dpu, ddq, ddz, ddba):
    TI = 256
    nt = S // TI

    def body(x_ref, nw_ref, w_hbm, pos_ref, freq_ref, dres_ref, datt_ref, datt4_ref, datt16_ref, dpu_ref, ddq_ref, ddz_ref,
             ddba_ref, dx_ref, dnw_ref, dw_hbm, w_scr, acc, r_scr):
        i = pl.program_id(0)

        @pl.when(i == 0)
        def _():
            pltpu.sync_copy(w_hbm, w_scr)
            acc[...] = jnp.zeros_like(acc)
            dnw_ref[...] = jnp.zeros_like(dnw_ref)

        xv = x_ref[...]
        hf, r = _rms(xv, nw_ref[...])
        h = hf.astype(bf16)
        cos, sin = _rope_tables(pos_ref, freq_ref)
        datt = datt_ref[...] + _join_residues(datt4_ref, 4, r_scr)
        datt = datt + _join_residues(datt16_ref, 16, r_scr)
        dq = datt[:, 0:ATT]
        dk = datt[:, ATT:2 * ATT]
        dq = dq * cos - _rot_half(dq) * sin
        dk = dk * cos - _rot_half(dk) * sin
        dproj = jnp.concatenate([dq, dk, datt[:, 2 * ATT:3 * ATT], dpu_ref[...], ddq_ref[...], ddz_ref[...], ddba_ref[...]],
                                axis=1).astype(bf16)
        acc[...] += _dot_tn(h, dproj)
        dh = _dot_nt(dproj, w_scr[...])
        dx, dw = _rms_bwd(xv, nw_ref[...], r, dh)
        dx_ref[...] = dres_ref[...] + dx
        dnw_ref[...] += dw

        @pl.when(i == nt - 1)
        def _():
            pltpu.sync_copy(acc, dw_hbm)

    return pl.pallas_call(
        body, grid=(nt,), name="inproj_bwd",
        in_specs=[pl.BlockSpec((TI, D), lambda i: (i, 0)),
                  pl.BlockSpec((1, D), lambda i: (0, 0)),
                  pl.BlockSpec(memory_space=pl.ANY),
                  pl.BlockSpec((TI, 1), lambda i: (i, 0)),
                  pl.BlockSpec((1, ATT), lambda i: (0, 0)),
                  pl.BlockSpec((TI, D), lambda i: (i, 0)),
                  pl.BlockSpec((TI, 768), lambda i: (i, 0)), _res_spec(4, TI, 768), _res_spec(16, TI, 768),
                  pl.BlockSpec((TI, 256), lambda i: (i, 0)),
                  pl.BlockSpec((TI, 1536), lambda i: (i, 0)),
                  pl.BlockSpec((TI, 512), lambda i: (i, 0)),
                  pl.BlockSpec((TI, 128), lambda i: (i, 0))],
        out_specs=[pl.BlockSpec((TI, D), lambda i: (i, 0)), pl.BlockSpec((1, D), lambda i: (0, 0)),
                   pl.BlockSpec(memory_space=pl.ANY)],
        out_shape=[SDS((S, D), f32), SDS((1, D), f32), SDS((D, INP), f32)],
        scratch_shapes=[pltpu.VMEM((D, INP), bf16), pltpu.VMEM((D, INP), f32), pltpu.VMEM((6, TI, 128), f32)],
        compiler_params=_cp(56),
    )(x, nw, w_aug, pos, freq, dres, datt, datt4, datt16, dpu, ddq, ddz, ddba)


def _outproj_fwd(x, ya, yb, yc, blob_b, kw):
    def body(x_ref, ya_ref, yb_ref, yc_ref, w_ref, o_ref):
        ycat = jnp.concatenate([ya_ref[...], yb_ref[...], yc_ref[...]], axis=1).astype(bf16)
        o_ref[...] = x_ref[...] + _dot(ycat, w_ref[:, 0:256, :].reshape(D, D))

    return pl.pallas_call(
        body, grid=(S // TM,), name="outproj_fwd",
        in_specs=[pl.BlockSpec((TM, D), lambda i: (i, 0)),
                  pl.BlockSpec((TM, 256), lambda i: (i, 0)),
                  pl.BlockSpec((TM, 256), lambda i: (i, 0)),
                  pl.BlockSpec((TM, 512), lambda i: (i, 0)),
                  pl.BlockSpec((NCH, FC, D), lambda i: (0, kw, 0))],
        out_specs=pl.BlockSpec((TM, D), lambda i: (i, 0)),
        out_shape=SDS((S, D), f32),
        compiler_params=_cp(40),
    )(x, ya, yb, yc, blob_b)


def _outproj_bwd(dy, ya, yb, yc, blob_b, kw):
    nt = S // TM

    def body(dy_ref, ya_ref, yb_ref, yc_ref, w_ref, dya_ref, dyb_ref, dyc_ref, dw_ref):
        i = pl.program_id(0)

        @pl.when(i == 0)
        def _():
            dw_ref[...] = jnp.zeros_like(dw_ref)

        dyv = dy_ref[...].astype(bf16)
        ycat = jnp.concatenate([ya_ref[...], yb_ref[...], yc_ref[...]], axis=1).astype(bf16)
        dw_ref[...] += _dot_tn(ycat, dyv)
        dcat = _dot_nt(dyv, w_ref[:, 0:256, :].reshape(D, D))
        dya_ref[...] = dcat[:, 0:256]
        dyb_ref[...] = dcat[:, 256:512]
        dyc_ref[...] = dcat[:, 512:1024]

    return pl.pallas_call(
        body, grid=(nt,), name="outproj_bwd",
        in_specs=[pl.BlockSpec((TM, D), lambda i: (i, 0)),
                  pl.BlockSpec((TM, 256), lambda i: (i, 0)),
                  pl.BlockSpec((TM, 256), lambda i: (i, 0)),
                  pl.BlockSpec((TM, 512), lambda i: (i, 0)),
                  pl.BlockSpec((NCH, FC, D), lambda i: (0, kw, 0))],
        out_specs=[pl.BlockSpec((TM, 256), lambda i: (i, 0)), pl.BlockSpec((TM, 256), lambda i: (i, 0)),
                   pl.BlockSpec((TM, 512), lambda i: (i, 0)), pl.BlockSpec((D, D), lambda i: (0, 0))],
        out_shape=[SDS((S, 256), f32), SDS((S, 256), f32), SDS((S, 512), f32), SDS((D, D), f32)],
        compiler_params=_cp(40),
    )(dy, ya, yb, yc, blob_b)


NB = S // NBLK


def _attn_block(q, kp, kc, vp, vc, first):
    kk = jnp.concatenate([kp, kc], axis=0).astype(bf16)
    vv = jnp.concatenate([vp, vc], axis=0).astype(bf16)
    qi = lax.broadcasted_iota(jnp.int32, (4 * NBLK, 2 * NBLK), 0) % NBLK
    ki = lax.broadcasted_iota(jnp.int32, (4 * NBLK, 2 * NBLK), 1)
    dist = NBLK + qi - ki
    valid = (dist >= 0) & (dist <= NBLK) & (jnp.logical_not(first) | (ki >= NBLK))
    head = lax.broadcasted_iota(jnp.int32, (1, ATT), 1) // EH
    masks = [(head == h).astype(f32) for h in range(4)]
    qs = jnp.concatenate([q * mh for mh in masks], axis=0).astype(bf16)
    s = _dot_nt(qs, kk) * (1.0 / math.sqrt(EH))
    s = jnp.where(valid, s, NEG)
    m = lax.stop_gradient(jnp.max(s, axis=-1, keepdims=True))
    p = jnp.exp(s - m)
    den = jnp.sum(p, axis=-1, keepdims=True)
    po = _dot((p / den).astype(bf16), vv)
    lse = m + jnp.log(den)
    o = jnp.zeros((NBLK, ATT), f32)
    l = jnp.zeros((NBLK, ATT), f32)
    for h, mh in enumerate(masks):
        o = o + po[NBLK * h:NBLK * (h + 1)] * mh
        l = l + lse[NBLK * h:NBLK * (h + 1)] * mh
    return o, l


def _attn_specs():
    prev = lambda b: jnp.maximum(b - 1, 0)
    return [pl.BlockSpec((NBLK, ATT), lambda b: (b, 0)),
            pl.BlockSpec((NBLK, ATT), lambda b: (prev(b), 1)),
            pl.BlockSpec((NBLK, ATT), lambda b: (b, 1)),
            pl.BlockSpec((NBLK, ATT), lambda b: (prev(b), 2)),
            pl.BlockSpec((NBLK, ATT), lambda b: (b, 2))]


def _attn_fwd(qkv, per_seq):
    def body(q_ref, kp_ref, kc_ref, vp_ref, vc_ref, o_ref, l_ref):
        first = pl.program_id(0) % per_seq == 0
        o, l = _attn_block(q_ref[...], kp_ref[...], kc_ref[...], vp_ref[...], vc_ref[...], first)
        o_ref[...] = o
        l_ref[...] = l

    blk = pl.BlockSpec((NBLK, ATT), lambda b: (b, 0))
    return pl.pallas_call(
        body, grid=(NB,), name="attn_fwd", in_specs=_attn_specs(), out_specs=[blk, blk],
        out_shape=[SDS((S, ATT), f32), SDS((S, ATT), f32)], compiler_params=_cp(32),
    )(qkv, qkv, qkv, qkv, qkv)


def _attn_bwd(qkv, do, dl, per_seq):
    def body(q_ref, kp_ref, kc_ref, vp_ref, vc_ref, do_ref, dl_ref, o_ref, k_carry, v_carry):
        step = pl.program_id(0)

        @pl.when(step == 0)
        def _():
            k_carry[...] = jnp.zeros_like(k_carry)
            v_carry[...] = jnp.zeros_like(v_carry)

        b = NB - 1 - step
        first = b % per_seq == 0
        last = b % per_seq == per_seq - 1
        fn = lambda q, kp, kc, vp, vc: _attn_block(q, kp, kc, vp, vc, first)
        _, vjp = jax.vjp(fn, q_ref[...], kp_ref[...], kc_ref[...], vp_ref[...], vc_ref[...])
        dq, dkp, dkc, dvp, dvc = vjp((do_ref[...], dl_ref[...]))
        o_ref[:, 0:ATT] = dq
        o_ref[:, ATT:2 * ATT] = dkc + jnp.where(last, 0.0, k_carry[...])
        o_ref[:, 2 * ATT:3 * ATT] = dvc + jnp.where(last, 0.0, v_carry[...])
        k_carry[...] = dkp
        v_carry[...] = dvp

    rev = lambda s: NB - 1 - s
    prev = lambda s: jnp.maximum(rev(s) - 1, 0)
    specs = [pl.BlockSpec((NBLK, ATT), lambda s: (rev(s), 0)),
             pl.BlockSpec((NBLK, ATT), lambda s: (prev(s), 1)),
             pl.BlockSpec((NBLK, ATT), lambda s: (rev(s), 1)),
             pl.BlockSpec((NBLK, ATT), lambda s: (prev(s), 2)),
             pl.BlockSpec((NBLK, ATT), lambda s: (rev(s), 2)),
             pl.BlockSpec((NBLK, ATT), lambda s: (rev(s), 0)),
             pl.BlockSpec((NBLK, ATT), lambda s: (rev(s), 0))]
    return pl.pallas_call(
        body, grid=(NB,), name="attn_bwd", in_specs=specs, out_specs=pl.BlockSpec((NBLK, 768), lambda s: (rev(s), 0)),
        out_shape=SDS((S, 768), f32), scratch_shapes=[pltpu.VMEM((NBLK, ATT), f32)] * 2, compiler_params=_cp(32),
    )(qkv, qkv, qkv, qkv, qkv, do, dl)


def _merge_weights(l0, l1, l2):
    m = jnp.maximum(jnp.maximum(l0, l1), l2)
    e0, e1, e2 = jnp.exp(l0 - m), jnp.exp(l1 - m), jnp.exp(l2 - m)
    tot = e0 + e1 + e2
    return e0 / tot, e1 / tot, e2 / tot


def _merge_specs():
    nat = pl.BlockSpec((TM, ATT), lambda i: (i, 0))
    return nat, _res_spec(4, TM, ATT), _res_spec(16, TM, ATT)


def _merge_fwd(o1, l1, o4, l4, o16, l16):
    def body(o1_ref, l1_ref, o4_ref, l4_ref, o16_ref, l16_ref, y_ref, scr):
        o4v, l4v = _join_residues(o4_ref, 4, scr), _join_residues(l4_ref, 4, scr)
        o16v, l16v = _join_residues(o16_ref, 16, scr), _join_residues(l16_ref, 16, scr)
        w0, w1, w2 = _merge_weights(l1_ref[...], l4v, l16v)
        y_ref[...] = w0 * o1_ref[...] + w1 * o4v + w2 * o16v

    nat, r4, r16 = _merge_specs()
    return pl.pallas_call(body, grid=(S // TM,), name="merge_fwd", in_specs=[nat, nat, r4, r4, r16, r16],
                          out_specs=nat, out_shape=SDS((S, ATT), f32), scratch_shapes=[pltpu.VMEM((2, TM, 128), f32)],
                          compiler_params=_cp(32))(o1, l1, o4, l4, o16, l16)


def _merge_bwd(o1, l1, o4, l4, o16, l16, dy):
    def body(o1_ref, l1_ref, o4_ref, l4_ref, o16_ref, l16_ref, dy_ref, do1_ref, dl1_ref, do4_ref, dl4_ref, do16_ref, dl16_ref, scr):
        o4v, l4v = _join_residues(o4_ref, 4, scr), _join_residues(l4_ref, 4, scr)
        o16v, l16v = _join_residues(o16_ref, 16, scr), _join_residues(l16_ref, 16, scr)
        o1v = o1_ref[...]
        w0, w1, w2 = _merge_weights(l1_ref[...], l4v, l16v)
        y = w0 * o1v + w1 * o4v + w2 * o16v
        dyv = dy_ref[...]
        do1_ref[...] = w0 * dyv
        dl1_ref[...] = w0 * (o1v - y) * dyv
        _split_residues(w1 * dyv, scr, [(do4_ref, 4)])
        _split_residues(w1 * (o4v - y) * dyv, scr, [(dl4_ref, 4)])
        _split_residues(w2 * dyv, scr, [(do16_ref, 16)])
        _split_residues(w2 * (o16v - y) * dyv, scr, [(dl16_ref, 16)])

    nat, r4, r16 = _merge_specs()
    return pl.pallas_call(body, grid=(S // TM,), name="merge_bwd", in_specs=[nat, nat, r4, r4, r16, r16, nat],
                          out_specs=[nat, nat, r4, r4, r16, r16],
                          out_shape=[SDS((S, ATT), f32)] * 2 + [SDS((4, S // 4, ATT), f32)] * 2 + [SDS((16, S // 16, ATT), f32)] * 2,
                          scratch_shapes=[pltpu.VMEM((2, TM, 128), f32)], compiler_params=_cp(32))(o1, l1, o4, l4, o16, l16, dy)


HALO = 16


def _pool_consts(i, rows):
    grp = lax.broadcasted_iota(jnp.int32, (rows, 256), 1) // 64
    t = i * TM + lax.broadcasted_iota(jnp.int32, (rows, 256), 0)
    win = jnp.where(grp == 0, 2, jnp.where(grp == 1, 4, jnp.where(grp == 2, 8, 16)))
    cnt = jnp.minimum(t + 1, win).astype(f32)
    return grp, cnt


def _pool_select(grp, s2, s4, s8, s16):
    return jnp.where(grp == 0, s2, jnp.where(grp == 1, s4, jnp.where(grp == 2, s8, s16)))


def _pooled(i, cur, halo):
    xx = jnp.concatenate([halo, cur], axis=0)
    s2 = xx + pltpu.roll(xx, 1, 0)
    s4 = s2 + pltpu.roll(s2, 2, 0)
    s8 = s4 + pltpu.roll(s4, 4, 0)
    s16 = s8 + pltpu.roll(s8, 8, 0)
    grp, cnt = _pool_consts(i, TM)
    tot = _pool_select(grp, s2[HALO:], s4[HALO:], s8[HALO:], s16[HALO:])
    return tot / cnt - cur


def _pool_fwd(u, wp, scale):
    def body(u_ref, halo_ref, wp_ref, sc_ref, y_ref):
        i = pl.program_id(0)
        halo = halo_ref[...] * (i > 0).astype(f32)
        pooled = _pooled(i, u_ref[...], halo)
        y_ref[...] = _dot(pooled.astype(bf16), wp_ref[...]) * sc_ref[...]

    return pl.pallas_call(
        body, grid=(S // TM,), name="pool_fwd",
        in_specs=[pl.BlockSpec((TM, 256), lambda i: (i, 0)),
                  pl.BlockSpec((HALO, 256), lambda i: (jnp.maximum(i * (TM // HALO) - 1, 0), 0)),
                  pl.BlockSpec((256, 256), lambda i: (0, 0)),
                  pl.BlockSpec((1, 256), lambda i: (0, 0))],
        out_specs=pl.BlockSpec((TM, 256), lambda i: (i, 0)), out_shape=SDS((S, 256), f32), compiler_params=_cp(32),
    )(u, u, wp, scale)


def _pool_bwd(u, wp, scale, dy):
    nt = S // TM

    def body(u_ref, halo_ref, wp_ref, sc_ref, dy_ref, dyn_ref, du_ref, dwp_ref, dsc_ref):
        i = pl.program_id(0)

        @pl.when(i == 0)
        def _():
            dwp_ref[...] = jnp.zeros_like(dwp_ref)
            dsc_ref[...] = jnp.zeros_like(dsc_ref)

        halo = halo_ref[...] * (i > 0).astype(f32)
        pooled = _pooled(i, u_ref[...], halo).astype(bf16)
        dyv = dy_ref[...]
        dsc_ref[...] += jnp.sum(dyv * _dot(pooled, wp_ref[...]), axis=0, keepdims=True)
        dys = (dyv * sc_ref[...]).astype(bf16)
        dwp_ref[...] += _dot_tn(pooled, dys)
        dpool = _dot_nt(dys, wp_ref[...])
        grp, cnt = _pool_consts(i, TM)
        dyn = ((dyn_ref[...] * (i < nt - 1).astype(f32)) * sc_ref[...]).astype(bf16)
        _, cntn = _pool_consts(i + 1, HALO)
        zn = _dot_nt(dyn, wp_ref[...]) / cntn
        zz = jnp.concatenate([dpool / cnt, zn], axis=0)
        n = TM + HALO
        a2 = zz + pltpu.roll(zz, n - 1, 0)
        a4 = a2 + pltpu.roll(a2, n - 2, 0)
        a8 = a4 + pltpu.roll(a4, n - 4, 0)
        a16 = a8 + pltpu.roll(a8, n - 8, 0)
        du_ref[...] = _pool_select(grp, a2[:TM], a4[:TM], a8[:TM], a16[:TM]) - dpool

    return pl.pallas_call(
        body, grid=(nt,), name="pool_bwd",
        in_specs=[pl.BlockSpec((TM, 256), lambda i: (i, 0)),
                  pl.BlockSpec((HALO, 256), lambda i: (jnp.maximum(i * (TM // HALO) - 1, 0), 0)),
                  pl.BlockSpec((256, 256), lambda i: (0, 0)),
                  pl.BlockSpec((1, 256), lambda i: (0, 0)),
                  pl.BlockSpec((TM, 256), lambda i: (i, 0)),
                  pl.BlockSpec((HALO, 256), lambda i: (jnp.minimum((i + 1) * (TM // HALO), S // HALO - 1), 0))],
        out_specs=[pl.BlockSpec((TM, 256), lambda i: (i, 0)), pl.BlockSpec((256, 256), lambda i: (0, 0)),
                   pl.BlockSpec((1, 256), lambda i: (0, 0))],
        out_shape=[SDS((S, 256), f32), SDS((256, 256), f32), SDS((1, 256), f32)], compiler_params=_cp(32),
    )(u, u, wp, scale, dy, dy)


CW = 3 * DNW
CHALO = 8
TC = 256


def _conv_fwd(u, w):
    def body(u_ref, halo_ref, w_ref, c_ref):
        i = pl.program_id(0)
        xx = jnp.concatenate([halo_ref[...] * (i > 0).astype(f32), u_ref[...]], axis=0)
        c = (w_ref[3:4, :] * xx + w_ref[2:3, :] * pltpu.roll(xx, 1, 0) + w_ref[1:2, :] * pltpu.roll(xx, 2, 0)
             + w_ref[0:1, :] * pltpu.roll(xx, 3, 0))
        c_ref[...] = c[CHALO:]

    return pl.pallas_call(
        body, grid=(S // TC,), name="conv_fwd",
        in_specs=[pl.BlockSpec((TC, CW), lambda i: (i, 0)),
                  pl.BlockSpec((CHALO, CW), lambda i: (jnp.maximum(i * (TC // CHALO) - 1, 0), 0)),
                  pl.BlockSpec((8, CW), lambda i: (0, 0))],
        out_specs=pl.BlockSpec((TC, CW), lambda i: (i, 0)), out_shape=SDS((S, CW), f32), compiler_params=_cp(32),
    )(u, u, w)


def _conv_bwd(u, w, dc):
    nt = S // TC

    def body(u_ref, halo_ref, w_ref, dc_ref, dcn_ref, du_ref, dw_ref):
        i = pl.program_id(0)

        @pl.when(i == 0)
        def _():
            dw_ref[...] = jnp.zeros_like(dw_ref)

        dcv = dc_ref[...]
        zz = jnp.concatenate([dcv, dcn_ref[...] * (i < nt - 1).astype(f32)], axis=0)
        n = TC + CHALO
        du = (w_ref[3:4, :] * zz + w_ref[2:3, :] * pltpu.roll(zz, n - 1, 0) + w_ref[1:2, :] * pltpu.roll(zz, n - 2, 0)
              + w_ref[0:1, :] * pltpu.roll(zz, n - 3, 0))
        du_ref[...] = du[:TC]
        xx = jnp.concatenate([halo_ref[...] * (i > 0).astype(f32), u_ref[...]], axis=0)
        for j in range(4):
            shifted = xx if j == 3 else pltpu.roll(xx, 3 - j, 0)
            dw_ref[j:j + 1, :] += jnp.sum(dcv * shifted[CHALO:], axis=0, keepdims=True)

    return pl.pallas_call(
        body, grid=(nt,), name="conv_bwd",
        in_specs=[pl.BlockSpec((TC, CW), lambda i: (i, 0)),
                  pl.BlockSpec((CHALO, CW), lambda i: (jnp.maximum(i * (TC // CHALO) - 1, 0), 0)),
                  pl.BlockSpec((8, CW), lambda i: (0, 0)),
                  pl.BlockSpec((TC, CW), lambda i: (i, 0)),
                  pl.BlockSpec((CHALO, CW), lambda i: (jnp.minimum((i + 1) * (TC // CHALO), S // CHALO - 1), 0))],
        out_specs=[pl.BlockSpec((TC, CW), lambda i: (i, 0)), pl.BlockSpec((8, CW), lambda i: (0, 0))],
        out_shape=[SDS((S, CW), f32), SDS((8, CW), f32)], compiler_params=_cp(32),
    )(u, u, w, dc, dc)


TL = 256
NCL = TL // CH


def _bdot(a, b):
    return jnp.einsum('nik,nkj->nij', a.astype(bf16), b.astype(bf16), preferred_element_type=f32)


def _bdot_nt(a, b):
    return jnp.einsum('nik,njk->nij', a.astype(bf16), b.astype(bf16), preferred_element_type=f32)


def _bdot_tn(a, b):
    return jnp.einsum('nki,nkj->nij', a.astype(bf16), b.astype(bf16), preferred_element_type=f32)


@jax.custom_vjp
def _inv_unit_lower(a):
    ii = lax.broadcasted_iota(jnp.int32, (1, CH, CH), 1)
    jj = lax.broadcasted_iota(jnp.int32, (1, CH, CH), 2)
    t = (ii == jj).astype(f32) - a
    p = a
    for _ in range(5):
        p = _bdot(p, p)
        t = t + _bdot(t, p)
    return t


def _inv_unit_lower_fwd(a):
    t = _inv_unit_lower(a)
    return t, t


def _inv_unit_lower_bwd(t, dt):
    return (-_bdot_tn(t, _bdot_nt(dt, t)),)


_inv_unit_lower.defvjp(_inv_unit_lower_fwd, _inv_unit_lower_bwd)


def _dn_local(c, dba, a_row, b_row):
    act = c * jax.nn.sigmoid(c)
    lane = lax.broadcasted_iota(jnp.int32, (TL, 128), 1)
    beta_all = jax.nn.sigmoid(dba)
    xs = dba + b_row
    softplus = jnp.maximum(xs, 0.0) + jnp.log(1.0 + jnp.exp(-jnp.abs(xs)))
    g_all = -jnp.exp(a_row) * softplus
    ii = lax.broadcasted_iota(jnp.int32, (1, CH, CH), 1)
    jj = lax.broadcasted_iota(jnp.int32, (1, CH, CH), 2)
    lower = jj <= ii
    strict = jj < ii
    eye = (ii == jj).astype(f32)
    us, ws, qgs, kds, intras = [], [], [], [], []
    aux = jnp.zeros((TL, 128), f32)
    for h in range(4):
        q = act[:, DH * h:DH * (h + 1)]
        k = act[:, DNW + DH * h:DNW + DH * (h + 1)]
        v = act[:, 2 * DNW + DH * h:2 * DNW + DH * (h + 1)]
        q = q * lax.rsqrt(jnp.sum(q * q, axis=-1, keepdims=True) + EPS) * (DH ** -0.5)
        k = k * lax.rsqrt(jnp.sum(k * k, axis=-1, keepdims=True) + EPS)
        beta = jnp.sum(jnp.where(lane == h, beta_all, 0.0), axis=1, keepdims=True)
        g = jnp.sum(jnp.where(lane == 4 + h, g_all, 0.0), axis=1, keepdims=True)
        q3, k3, v3 = q.reshape(NCL, CH, DH), k.reshape(NCL, CH, DH), v.reshape(NCL, CH, DH)
        beta3, g3 = beta.reshape(NCL, CH, 1), g.reshape(NCL, CH, 1)
        g_row = jnp.sum(eye * g3, axis=1, keepdims=True)
        gc_col = jnp.sum(jnp.where(lower, g_row, 0.0), axis=2, keepdims=True)
        gc_row = jnp.sum(jnp.where(ii <= jj, g3, 0.0), axis=1, keepdims=True)
        diff = gc_col - gc_row
        decay = jnp.where(lower, jnp.exp(jnp.where(lower, diff, 0.0)), 0.0)
        kb = k3 * beta3
        vb = v3 * beta3
        a = jnp.where(strict, _bdot_nt(kb, k3) * decay, 0.0)
        t = _inv_unit_lower(a)
        u3 = _bdot(t, vb)
        w3 = _bdot(t, kb * jnp.exp(gc_col))
        intra = jnp.where(lower, _bdot_nt(q3, k3) * decay, 0.0)
        g_last = jnp.sum(g3, axis=1, keepdims=True)
        us.append(u3.reshape(TL, DH))
        ws.append(w3.reshape(TL, DH))
        qgs.append((q3 * jnp.exp(gc_col)).reshape(TL, DH))
        kds.append((k3 * jnp.exp(g_last - gc_col)).reshape(TL, DH))
        intras.append(intra.reshape(TL, CH))
        e_last = jnp.broadcast_to(jnp.exp(g_last), (NCL, CH, 1)).reshape(TL, 1)
        aux = aux + jnp.where(lane == h, e_last, 0.0)
    cat = lambda xs: jnp.concatenate(xs, axis=1)
    return cat(us), cat(ws), cat(qgs), cat(kds), jnp.stack(intras, axis=0), aux


def _dn_local_fwd(c, dba, par):
    def body(c_ref, dba_ref, par_ref, u_ref, w_ref, qg_ref, kd_ref, in_ref, aux_ref):
        u, w, qg, kd, intra, aux = _dn_local(c_ref[...], dba_ref[...], par_ref[0:1, :], par_ref[1:2, :])
        u_ref[...] = u
        w_ref[...] = w
        qg_ref[...] = qg
        kd_ref[...] = kd
        in_ref[...] = intra
        aux_ref[...] = aux

    wide = pl.BlockSpec((TL, DNW), lambda i: (i, 0))
    return pl.pallas_call(
        body, grid=(S // TL,), name="dn_local_fwd",
        in_specs=[pl.BlockSpec((TL, CW), lambda i: (i, 0)), pl.BlockSpec((TL, 128), lambda i: (i, 0)),
                  pl.BlockSpec((8, 128), lambda i: (0, 0))],
        out_specs=[wide, wide, wide, wide, pl.BlockSpec((4, TL, CH), lambda i: (0, i, 0)),
                   pl.BlockSpec((TL, 128), lambda i: (i, 0))],
        out_shape=[SDS((S, DNW), f32)] * 4 + [SDS((4, S, CH), f32), SDS((S, 128), f32)], compiler_params=_cp(48),
    )(c, dba, par)


def _dn_local_bwd(c, dba, par, du, dw, dqg, dkd, dintra, daux):
    def body(c_ref, dba_ref, par_ref, du_ref, dw_ref, dqg_ref, dkd_ref, din_ref, daux_ref, dc_ref, ddba_ref, dpar_ref):
        @pl.when(pl.program_id(0) == 0)
        def _():
            dpar_ref[...] = jnp.zeros_like(dpar_ref)

        _, vjp = jax.vjp(_dn_local, c_ref[...], dba_ref[...], par_ref[0:1, :], par_ref[1:2, :])
        dc, ddba, da_row, db_row = vjp((du_ref[...], dw_ref[...], dqg_ref[...], dkd_ref[...], din_ref[...], daux_ref[...]))
        dc_ref[...] = dc
        ddba_ref[...] = ddba
        dpar_ref[0:1, :] += da_row
        dpar_ref[1:2, :] += db_row

    wide = pl.BlockSpec((TL, DNW), lambda i: (i, 0))
    return pl.pallas_call(
        body, grid=(S // TL,), name="dn_local_bwd",
        in_specs=[pl.BlockSpec((TL, CW), lambda i: (i, 0)), pl.BlockSpec((TL, 128), lambda i: (i, 0)),
                  pl.BlockSpec((8, 128), lambda i: (0, 0)), wide, wide, wide, wide,
                  pl.BlockSpec((4, TL, CH), lambda i: (0, i, 0)), pl.BlockSpec((TL, 128), lambda i: (i, 0))],
        out_specs=[pl.BlockSpec((TL, CW), lambda i: (i, 0)), pl.BlockSpec((TL, 128), lambda i: (i, 0)),
                   pl.BlockSpec((8, 128), lambda i: (0, 0))],
        out_shape=[SDS((S, CW), f32), SDS((S, 128), f32), SDS((8, 128), f32)], compiler_params=_cp(56),
    )(c, dba, par, du, dw, dqg, dkd, dintra, daux)


def _dn_step(state, u, w, qg, kd, intra, aux):
    lane = lax.broadcasted_iota(jnp.int32, (CH, 128), 1)
    row = lax.broadcasted_iota(jnp.int32, (CH, 128), 0)
    outs, states = [], []
    for h in range(4):
        sl = slice(DH * h, DH * (h + 1))
        st = state[h]
        e = jnp.sum(jnp.sum(jnp.where((lane == h) & (row == 0), aux, 0.0), axis=1, keepdims=True), axis=0, keepdims=True)
        v_new = u[:, sl] - _dot(w[:, sl].astype(bf16), st.astype(bf16))
        vb = v_new.astype(bf16)
        outs.append(_dot(qg[:, sl].astype(bf16), st.astype(bf16)) + _dot(intra[h].astype(bf16), vb))
        states.append(st * e + _dot_tn(kd[:, sl].astype(bf16), vb))
    return jnp.concatenate(outs, axis=1), jnp.stack(states, axis=0)


def _dn_rec_fwd(u, w, qg, kd, intra, aux):
    def body(u_ref, w_ref, qg_ref, kd_ref, in_ref, aux_ref, o_ref, st_ref, st_scr):
        @pl.when(pl.program_id(0) == 0)
        def _():
            st_scr[...] = jnp.zeros_like(st_scr)

        st = st_scr[...]
        st_ref[0] = st
        o, new = _dn_step(st, u_ref[...], w_ref[...], qg_ref[...], kd_ref[...], in_ref[...], aux_ref[...])
        o_ref[...] = o
        st_scr[...] = new

    wide = pl.BlockSpec((CH, DNW), lambda n: (n, 0))
    return pl.pallas_call(
        body, grid=(NCHUNK,), name="dn_rec_fwd",
        in_specs=[wide, wide, wide, wide, pl.BlockSpec((4, CH, CH), lambda n: (0, n, 0)),
                  pl.BlockSpec((CH, 128), lambda n: (n, 0))],
        out_specs=[wide, pl.BlockSpec((1, 4, DH, DH), lambda n: (n, 0, 0, 0))],
        out_shape=[SDS((S, DNW), f32), SDS((NCHUNK, 4, DH, DH), f32)],
        scratch_shapes=[pltpu.VMEM((4, DH, DH), f32)], compiler_params=_cp(32),
    )(u, w, qg, kd, intra, aux)


def _dn_rec_bwd(u, w, qg, kd, intra, aux, states, do):
    def body(u_ref, w_ref, qg_ref, kd_ref, in_ref, aux_ref, st_ref, do_ref,
             du_ref, dw_ref, dqg_ref, dkd_ref, din_ref, daux_ref, ds_scr):
        @pl.when(pl.program_id(0) == 0)
        def _():
            ds_scr[...] = jnp.zeros_like(ds_scr)

        _, vjp = jax.vjp(_dn_step, st_ref[0], u_ref[...], w_ref[...], qg_ref[...], kd_ref[...], in_ref[...], aux_ref[...])
        dst, du, dw, dqg, dkd, din, daux = vjp((do_ref[...], ds_scr[...]))
        du_ref[...] = du
        dw_ref[...] = dw
        dqg_ref[...] = dqg
        dkd_ref[...] = dkd
        din_ref[...] = din
        daux_ref[...] = daux
        ds_scr[...] = dst

    rev = lambda n: NCHUNK - 1 - n
    wide = pl.BlockSpec((CH, DNW), lambda n: (rev(n), 0))
    inb = pl.BlockSpec((4, CH, CH), lambda n: (0, rev(n), 0))
    auxb = pl.BlockSpec((CH, 128), lambda n: (rev(n), 0))
    return pl.pallas_call(
        body, grid=(NCHUNK,), name="dn_rec_bwd",
        in_specs=[wide, wide, wide, wide, inb, auxb, pl.BlockSpec((1, 4, DH, DH), lambda n: (rev(n), 0, 0, 0)), wide],
        out_specs=[wide, wide, wide, wide, inb, auxb],
        out_shape=[SDS((S, DNW), f32)] * 4 + [SDS((4, S, CH), f32), SDS((S, 128), f32)],
        scratch_shapes=[pltpu.VMEM((4, DH, DH), f32)], compiler_params=_cp(32),
    )(u, w, qg, kd, intra, aux, states, do)


def _dn_post(o, z, nw):
    parts = []
    for h in range(4):
        sl = slice(DH * h, DH * (h + 1))
        oh = o[:, sl]
        y = oh * lax.rsqrt(jnp.mean(oh * oh, axis=-1, keepdims=True) + EPS) * nw
        zh = z[:, sl]
        parts.append(y * (zh * jax.nn.sigmoid(zh)))
    return jnp.concatenate(parts, axis=1)


def _dn_post_fwd(o, z, nw):
    def body(o_ref, z_ref, nw_ref, y_ref):
        y_ref[...] = _dn_post(o_ref[...], z_ref[...], nw_ref[...])

    wide = pl.BlockSpec((TM, DNW), lambda i: (i, 0))
    return pl.pallas_call(body, grid=(S // TM,), name="dn_post_fwd",
                          in_specs=[wide, wide, pl.BlockSpec((1, 128), lambda i: (0, 0))], out_specs=wide,
                          out_shape=SDS((S, DNW), f32), compiler_params=_cp(32))(o, z, nw)


def _dn_post_bwd(o, z, nw, dy):
    def body(o_ref, z_ref, nw_ref, dy_ref, do_ref, dz_ref, dnw_ref):
        @pl.when(pl.program_id(0) == 0)
        def _():
            dnw_ref[...] = jnp.zeros_like(dnw_ref)

        _, vjp = jax.vjp(_dn_post, o_ref[...], z_ref[...], nw_ref[...])
        do, dz, dnw = vjp(dy_ref[...])
        do_ref[...] = do
        dz_ref[...] = dz
        dnw_ref[...] += dnw

    wide = pl.BlockSpec((TM, DNW), lambda i: (i, 0))
    one = pl.BlockSpec((1, 128), lambda i: (0, 0))
    return pl.pallas_call(body, grid=(S // TM,), name="dn_post_bwd", in_specs=[wide, wide, one, wide],
                          out_specs=[wide, wide, one], out_shape=[SDS((S, DNW), f32), SDS((S, DNW), f32), SDS((1, 128), f32)],
                          compiler_params=_cp(32))(o, z, nw, dy)


def _row_tile(rows, width, itemsize=4, target=2 * 1024 * 1024):
    best = None
    for t in range(16, rows + 1, 16):
        if rows % t == 0 and t * width * itemsize <= target:
            best = t
    return best if best is not None else rows


def _sum_pieces(pieces, out_dtype, name):
    n, rows, width = pieces.shape
    tr = _row_tile(rows, width * n)

    def body(p_ref, o_ref):
        acc = p_ref[0].astype(f32)
        for s in range(1, n):
            acc = acc + p_ref[s].astype(f32)
        o_ref[...] = acc.astype(out_dtype)

    return pl.pallas_call(body, grid=(rows // tr,), name=name,
                          in_specs=[pl.BlockSpec((n, tr, width), lambda i: (0, i, 0))],
                          out_specs=pl.BlockSpec((tr, width), lambda i: (i, 0)),
                          out_shape=SDS((rows, width), out_dtype), compiler_params=_cp(32))(pieces)


def _sum_core_pair(part, got, c_arr):
    n, rows, width = part.shape
    half = rows // 2
    tr = _row_tile(half, width, itemsize=2)
    nt = half // tr

    def body(c_ref, p_ref, g_ref, o_ref):
        o_ref[...] = (p_ref[...].astype(f32) + g_ref[...].astype(f32)).astype(bf16)

    gs = pltpu.PrefetchScalarGridSpec(
        num_scalar_prefetch=1, grid=(n, nt),
        in_specs=[pl.BlockSpec((1, tr, width), lambda j, i, c: (j, c[0] * nt + i, 0)),
                  pl.BlockSpec((1, tr, width), lambda j, i, c: (j, i, 0))],
        out_specs=pl.BlockSpec((1, tr, width), lambda j, i, c: (j, i, 0)))
    return pl.pallas_call(body, grid_spec=gs, name="sum_core_pair", out_shape=SDS((n, half, width), bf16),
                          compiler_params=_cp(32))(c_arr, part, got)


def _sum_chips(pieces, c_arr, full, row0, total_rows):
    n, half, width = pieces.shape
    tr = max(t for t in range(16, 257, 16) if half % t == 0 and row0 % t == 0)
    nt = half // tr

    def body(c_ref, p_ref, *rest):
        o_ref = rest[-1]
        acc = p_ref[0].astype(f32)
        for s in range(1, n):
            acc = acc + p_ref[s].astype(f32)
        o_ref[...] = acc

    gs = pltpu.PrefetchScalarGridSpec(
        num_scalar_prefetch=1, grid=(nt,),
        in_specs=[pl.BlockSpec((n, tr, width), lambda i, c: (0, i, 0))] + ([] if full is None else [ANY]),
        out_specs=pl.BlockSpec((tr, width), lambda i, c: (row0 // tr + c[0] * nt + i, 0)))
    args = (c_arr, pieces) if full is None else (c_arr, pieces, full)
    return pl.pallas_call(body, grid_spec=gs, name="sum_chips", out_shape=SDS((total_rows, width), f32),
                          input_output_aliases={} if full is None else {2: 0}, compiler_params=_cp(32))(*args)


def _adamw_math(w, g, m, v):
    mn = ADAM_B1 * m + (1.0 - ADAM_B1) * g
    vn = ADAM_B2 * v + (1.0 - ADAM_B2) * (g * g)
    m_hat = mn / (1.0 - ADAM_B1 ** ADAM_STEP)
    v_hat = vn / (1.0 - ADAM_B2 ** ADAM_STEP)
    return -ADAM_LR * (m_hat / (jnp.sqrt(v_hat) + ADAM_EPS) + ADAM_WD * w), mn, vn


def _adamw(w, g, m, v, name):
    rows, width = w.shape
    tr = _row_tile(rows, width * 7, target=12 * 1024 * 1024)

    def body(w_ref, g_ref, m_ref, v_ref, d_ref, nm_ref, nv_ref):
        d_ref[...], nm_ref[...], nv_ref[...] = _adamw_math(w_ref[...], g_ref[...], m_ref[...], v_ref[...])

    blk = pl.BlockSpec((tr, width), lambda i: (i, 0))
    return pl.pallas_call(body, grid=(rows // tr,), name=name, in_specs=[blk] * 4, out_specs=[blk] * 3,
                          out_shape=[SDS((rows, width), f32)] * 3, compiler_params=_cp(40))(w, g, m, v)


def _adamw_rows(w, m, v, gblob, tr, first_tile, name):
    layers, rows, width = w.shape

    def body(w_ref, g_ref, m_ref, v_ref, d_ref, nm_ref, nv_ref):
        d_ref[0], nm_ref[0], nv_ref[0] = _adamw_math(w_ref[0], g_ref[...], m_ref[0], v_ref[0])

    blk = pl.BlockSpec((1, tr, width), lambda l, i: (l, i, 0))
    gblk = pl.BlockSpec((tr, width), lambda l, i: (first_tile(l) + i, 0))
    return pl.pallas_call(body, grid=(layers, rows // tr), name=name, in_specs=[blk, gblk, blk, blk], out_specs=[blk] * 3,
                          out_shape=[SDS(w.shape, f32)] * 3, compiler_params=_cp(40))(w, gblob, m, v)


ANY = pl.BlockSpec(memory_space=pl.ANY)


def _place():
    x, y, c = lax.axis_index("x"), lax.axis_index("y"), lax.axis_index("c")
    chips = [(1 - x, y), (x, 1 - y), (1 - x, 1 - y)]
    return x, y, c, chips


NQ_ICI = 4
NQ_D2D = 8


def _chunks(rows, want):
    n = max(k for k in range(1, want + 1) if rows % k == 0 and (rows // k) % 16 == 0)
    step = rows // n
    return [(q * step, step) for q in range(n)]


def _scatter_copies(ins, outs, ssem, rsem, lsem):
    x, y, c, chips = _place()
    me = (x, y, c)
    locals_, sends, lands = [], [], []
    for b in range(len(ins)):
        for q, (off, n) in enumerate(_chunks(ins[b].shape[1], NQ_ICI)):
            rows = pl.ds(off, n)
            mine = outs[b].at[2 * x + y, rows, :]
            locals_.append(pltpu.make_async_copy(ins[b].at[2 * x + y, rows, :], mine, lsem.at[b, q]))
            for j, chip in enumerate(chips):
                sends.append(_remote(ins[b].at[2 * chip[0] + chip[1], rows, :], mine, ssem.at[b, j, q], rsem.at[b, j, q],
                                     (*chip, c)))
                slot = outs[b].at[2 * chip[0] + chip[1], rows, :]
                lands.append(_remote(slot, slot, ssem.at[b, j, q], rsem.at[b, j, q], me))
    return locals_, sends, lands


def _scatter_start(ins, outs, ssem, rsem, lsem):
    locals_, sends, _ = _scatter_copies(ins, outs, ssem, rsem, lsem)
    for cp in locals_ + sends:
        cp.start()


def _scatter_finish(ins, outs, ssem, rsem, lsem):
    locals_, sends, lands = _scatter_copies(ins, outs, ssem, rsem, lsem)
    for cp in lands:
        cp.wait_recv()
    for cp in sends:
        cp.wait_send()
    for cp in locals_:
        cp.wait()


def _scatter_sems(nb):
    return [pltpu.SemaphoreType.DMA((nb, 3, NQ_ICI)), pltpu.SemaphoreType.DMA((nb, 3, NQ_ICI)),
            pltpu.SemaphoreType.DMA((nb, NQ_ICI))]


def _remote(src, dst, ssem, rsem, dev):
    return pltpu.make_async_remote_copy(src_ref=src, dst_ref=dst, send_sem=ssem, recv_sem=rsem, device_id=dev,
                                        device_id_type=MESH)


def _all_gather_weights(shards):
    nb = len(shards)

    def body(*refs):
        ins, outs, sems = refs[:nb], refs[nb:2 * nb], refs[2 * nb:]
        _gather_start(ins, outs, *sems)
        _gather_finish(ins, outs, *sems)

    return pl.pallas_call(
        body, name="all_gather_weights", in_specs=[ANY] * nb, out_specs=[ANY] * nb,
        out_shape=[SDS((NCH,) + s.shape, s.dtype) for s in shards], scratch_shapes=_gather_sems(nb),
    )(*shards)


def _gather_first(ins, outs, ssem, rsem, lsem):
    x, y, c, chips = _place()
    locals_, sends = [], []
    for b in range(len(ins)):
        half = ins[b].shape[0] // 2
        for q, (off, n) in enumerate(_chunks(half, NQ_ICI)):
            mine = pl.ds(c * half + off, n)
            own = outs[b].at[2 * x + y, mine, :]
            locals_.append(pltpu.make_async_copy(ins[b].at[mine, :], own, lsem.at[b, q]))
            sends.append(_remote(ins[b].at[mine, :], own, ssem.at[b, 0, q], rsem.at[b, 0, q], (x, y, 1 - c)))
            sends += [_remote(ins[b].at[mine, :], own, ssem.at[b, 1 + j, q], rsem.at[b, 1 + j, q], (*chip, c))
                      for j, chip in enumerate(chips)]
    return locals_, sends


def _gather_start(ins, outs, ssem, rsem, lsem):
    locals_, sends = _gather_first(ins, outs, ssem, rsem, lsem)
    for cp in locals_ + sends:
        cp.start()


def _gather_finish(ins, outs, ssem, rsem, lsem):
    x, y, c, chips = _place()
    me, sib = (x, y, c), (x, y, 1 - c)
    locals_, sends = _gather_first(ins, outs, ssem, rsem, lsem)
    for b in range(len(ins)):
        half = ins[b].shape[0] // 2
        for q, (off, n) in enumerate(_chunks(half, NQ_ICI)):
            mine = pl.ds(c * half + off, n)
            for j, chip in enumerate(chips):
                landed = outs[b].at[2 * chip[0] + chip[1], mine, :]
                _remote(landed, landed, ssem.at[b, 1 + j, q], rsem.at[b, 1 + j, q], me).wait_recv()
                cp = _remote(landed, landed, ssem.at[b, 4 + j, q], rsem.at[b, 4 + j, q], sib)
                cp.start()
                sends.append(cp)
    for b in range(len(ins)):
        half = ins[b].shape[0] // 2
        for q, (off, n) in enumerate(_chunks(half, NQ_ICI)):
            other = pl.ds((1 - c) * half + off, n)
            theirs = outs[b].at[2 * x + y, other, :]
            _remote(theirs, theirs, ssem.at[b, 0, q], rsem.at[b, 0, q], me).wait_recv()
            for j, chip in enumerate(chips):
                fwd = outs[b].at[2 * chip[0] + chip[1], other, :]
                _remote(fwd, fwd, ssem.at[b, 4 + j, q], rsem.at[b, 4 + j, q], me).wait_recv()
    for cp in sends:
        cp.wait_send()
    for cp in locals_:
        cp.wait()


def _gather_sems(nb):
    return [pltpu.SemaphoreType.DMA((nb, 7, NQ_ICI)), pltpu.SemaphoreType.DMA((nb, 7, NQ_ICI)),
            pltpu.SemaphoreType.DMA((nb, NQ_ICI))]


def _send_sibling_half(parts):
    nb = len(parts)

    def body(*refs):
        ins, gots = refs[:nb], refs[nb:2 * nb]
        ssem, rsem = refs[2 * nb:]
        x, y, c, _ = _place()
        sib = (x, y, 1 - c)
        todo = []
        for b in range(nb):
            half = ins[b].shape[1] // 2
            for q, (off, n) in enumerate(_chunks(half, NQ_D2D)):
                cp = _remote(ins[b].at[:, pl.ds((1 - c) * half + off, n), :], gots[b].at[:, pl.ds(off, n), :],
                             ssem.at[b, q], rsem.at[b, q], sib)
                cp.start()
                todo.append(cp)
        for cp in todo:
            cp.wait()

    return pl.pallas_call(
        body, name="send_sibling_half", in_specs=[ANY] * nb, out_specs=[ANY] * nb,
        out_shape=[SDS((p.shape[0], p.shape[1] // 2, p.shape[2]), p.dtype) for p in parts],
        scratch_shapes=[pltpu.SemaphoreType.DMA((nb, NQ_D2D)), pltpu.SemaphoreType.DMA((nb, NQ_D2D))],
    )(*parts)


def _scatter_to_chips(parts):
    nb = len(parts)

    def body(*refs):
        ins, outs, sems = refs[:nb], refs[nb:2 * nb], refs[2 * nb:]
        _scatter_start(ins, outs, *sems)
        _scatter_finish(ins, outs, *sems)

    return pl.pallas_call(
        body, name="scatter_to_chips", in_specs=[ANY] * nb, out_specs=[ANY] * nb,
        out_shape=[SDS(p.shape, p.dtype) for p in parts], scratch_shapes=_scatter_sems(nb),
    )(*parts)


def _join_halves(fulls, ranges):
    nb = len(fulls)
    nr = max(len(r) for r in ranges)

    def body(*refs):
        ins, outs = refs[:nb], refs[nb:2 * nb]
        ssem, rsem = refs[2 * nb:]
        x, y, c, _ = _place()
        sib = (x, y, 1 - c)
        sends, lands = [], []
        for b in range(nb):
            for g, (row0, rows) in enumerate(ranges[b]):
                half = rows // 2
                for q, (off, n) in enumerate(_chunks(half, NQ_D2D)):
                    mine = pl.ds(row0 + c * half + off, n)
                    sends.append(_remote(ins[b].at[mine, :], outs[b].at[mine, :], ssem.at[b, g, q], rsem.at[b, g, q], sib))
                    other = outs[b].at[pl.ds(row0 + (1 - c) * half + off, n), :]
                    lands.append(_remote(other, other, ssem.at[b, g, q], rsem.at[b, g, q], sib))
        for cp in sends:
            cp.start()
        for cp in lands:
            cp.wait_recv()
        for cp in sends:
            cp.wait_send()

    return pl.pallas_call(
        body, name="join_halves", in_specs=[ANY] * nb, out_specs=[ANY] * nb,
        out_shape=[SDS(h.shape, h.dtype) for h in fulls], input_output_aliases={b: b for b in range(nb)},
        scratch_shapes=[pltpu.SemaphoreType.DMA((nb, nr, NQ_D2D)), pltpu.SemaphoreType.DMA((nb, nr, NQ_D2D))],
    )(*fulls)


def _gather_small(vec):
    def body(v_ref, o_ref, ssem, rsem, lsem):
        x, y, c, _ = _place()
        mine = o_ref.at[4 * x + 2 * y + c]
        local = pltpu.make_async_copy(v_ref, mine, lsem)
        local.start()
        sends = []
        for k in range(1, 8):
            peer = (x ^ (k >> 2), y ^ ((k >> 1) & 1), c ^ (k & 1))
            cp = _remote(v_ref, mine, ssem.at[k - 1], rsem.at[k - 1], peer)
            cp.start()
            sends.append(cp)
        for k in range(1, 8):
            px, py, pc = x ^ (k >> 2), y ^ ((k >> 1) & 1), c ^ (k & 1)
            slot = o_ref.at[4 * px + 2 * py + pc]
            _remote(slot, slot, ssem.at[k - 1], rsem.at[k - 1], (x, y, c)).wait_recv()
        for cp in sends:
            cp.wait_send()
        local.wait()

    return pl.pallas_call(
        body, name="gather_small", in_specs=[ANY], out_specs=ANY, out_shape=SDS((8,) + vec.shape, vec.dtype),
        scratch_shapes=[pltpu.SemaphoreType.DMA((7,)), pltpu.SemaphoreType.DMA((7,)), pltpu.SemaphoreType.DMA],
    )(vec)


def _block_diag(pw):
    return jnp.concatenate([jnp.pad(pw[g], ((0, 0), (64 * g, 192 - 64 * g))) for g in range(4)], axis=0)


def _own_columns(full, chip):
    n = full.shape[-1] // NCH
    parts = full.reshape(full.shape[:-1] + (NCH, n))
    sel = (lax.broadcasted_iota(jnp.int32, (NCH, 1), 0) == chip)
    return jnp.sum(jnp.where(sel, parts, 0.0), axis=-2)


def _at_own_columns(shard, chip):
    n = shard.shape[-1]
    sel = (lax.broadcasted_iota(jnp.int32, (NCH * n,), 0) // n == chip)
    return jnp.where(sel, jnp.tile(shard, NCH), 0.0)


def _pad_rows(a, rows):
    return jnp.pad(a, ((0, rows - a.shape[0]),) + ((0, 0),) * (a.ndim - 1))


def _ffn_block(l, which):
    return 7 * l + 3 * which


def _wout_block(l):
    return 7 * l + 6


class _Weights:
    def __init__(self):
        self.ffn, self.wout, self.w_aug, self.rides = {}, {}, {}, {}

    @classmethod
    def from_blob(cls, blob, w_aug):
        self = cls()
        for l in range(DEPTH):
            self.ffn[(l, 0)], self.ffn[(l, 1)] = (blob, _ffn_block(l, 0)), (blob, _ffn_block(l, 1))
            self.wout[l], self.w_aug[l] = (blob, _wout_block(l)), w_aug[l]
        return self

    def set_w_in(self, l, gathered):
        self.w_aug[l] = jnp.pad(gathered.transpose(1, 0, 2).reshape(D, INW), ((0, 0), (0, INP - INW)))

    def ffn_fwd(self, l, which, x, nw):
        arr, k0 = self.ffn[(l, which)]
        if (l, which) not in self.rides:
            return _ffn_fwd(x, nw, arr, k0)
        shards, landed = self.rides[(l, which)]
        out, *gathered = _ffn_fwd(x, nw, arr, k0, shards)
        landed(gathered)
        return out


def _layer_fwd(l, x0, pos, freq, wts, ws):
    sv = {"x0": x0}
    x1 = ws.ffn_fwd(l, 0, x0, wts["ffn1_norm"][l:l + 1])
    att, att4, att16, pu, dq, dz, dba = _inproj_fwd(x1, wts["mix_norm"][l:l + 1], ws.w_aug[l], pos, freq)
    qkvs = [att, att4.reshape(S, 768), att16.reshape(S, 768)]
    (o1, l1), (o4, l4), (o16, l16) = [_attn_fwd(q, NB // d) for q, d in zip(qkvs, PATTERN_DIL)]
    ols = (o1, l1, o4.reshape(4, S // 4, ATT), l4.reshape(4, S // 4, ATT), o16.reshape(16, S // 16, ATT),
           l16.reshape(16, S // 16, ATT))
    ya = _merge_fwd(*ols)
    yb = _pool_fwd(pu, wts["pool_bd"][l], wts["pool_scale"][l:l + 1])
    c = _conv_fwd(dq, wts["conv_w"][l])
    u, w, qg, kd, intra, aux = _dn_local_fwd(c, dba, wts["dn_par"][l])
    o_dn, states = _dn_rec_fwd(u, w, qg, kd, intra, aux)
    yc = _dn_post_fwd(o_dn, dz, wts["dn_out_norm"][l:l + 1])
    x2 = _outproj_fwd(x1, ya, yb, yc, *ws.wout[l])
    x3 = ws.ffn_fwd(l, 1, x2, wts["ffn2_norm"][l:l + 1])
    sv.update(x1=x1, x2=x2, qkvs=qkvs, ols=ols, ya=ya, yb=yb, yc=yc, pu=pu, dq=dq, dz=dz, dba=dba, c=c,
              u=u, w=w, qg=qg, kd=kd, intra=intra, aux=aux, states=states, o_dn=o_dn)
    return x3, sv


def _wout_part(g):
    return jnp.pad(g.astype(bf16).reshape(NCH, 256, D), ((0, 0), (0, FC - 256), (0, 0)))


def _win_part(g):
    return g[:, :INW].astype(bf16).reshape(D, NCH, INC).transpose(1, 0, 2)


def _layer_bwd(l, dx3, sv, pos, freq, wts, ws, ride=None, prep=None):
    gr = {}
    g2, u2, d2, dh4, *pieces_before = _ffn_bwd(sv["x2"], wts["ffn2_norm"][l:l + 1], *ws.ffn[(l, 1)], dx3, ride)
    dx2, gr["ffn2_norm"] = _norm_bwd(sv["x2"], wts["ffn2_norm"][l:l + 1], dx3, dh4)
    gr.update(ffn2_w_gate=g2, ffn2_w_up=u2, ffn2_w_down=d2)
    dya, dyb, dyc, gr["w_out"] = _outproj_bwd(dx2, sv["ya"], sv["yb"], sv["yc"], *ws.wout[l])
    do_dn, ddz, gr["dn_out_norm"] = _dn_post_bwd(sv["o_dn"], sv["dz"], wts["dn_out_norm"][l:l + 1], dyc)
    du, dw, dqg, dkd, dintra, daux = _dn_rec_bwd(sv["u"], sv["w"], sv["qg"], sv["kd"], sv["intra"], sv["aux"], sv["states"], do_dn)
    dc, ddba, gr["dn_par"] = _dn_local_bwd(sv["c"], sv["dba"], wts["dn_par"][l], du, dw, dqg, dkd, dintra, daux)
    ddq, gr["conv_w"] = _conv_bwd(sv["dq"], wts["conv_w"][l], dc)
    dpu, gr["pool_bd"], gr["pool_scale"] = _pool_bwd(sv["pu"], wts["pool_bd"][l], wts["pool_scale"][l:l + 1], dyb)
    dols = _merge_bwd(*sv["ols"], dya)
    datts = [_attn_bwd(q, dols[2 * p].reshape(S, ATT), dols[2 * p + 1].reshape(S, ATT), NB // d)
             for p, (q, d) in enumerate(zip(sv["qkvs"], PATTERN_DIL))]
    dx1, gr["mix_norm"], gr["w_aug"] = _inproj_bwd(sv["x1"], wts["mix_norm"][l:l + 1], ws.w_aug[l], pos, freq, dx2,
                                                    datts[0], datts[1].reshape(4, S // 4, 768),
                                                    datts[2].reshape(16, S // 16, 768), dpu, ddq, ddz, ddba)
    own = None
    if prep is not None:
        own = prep([jnp.concatenate([g2, u2, d2, _wout_part(gr["w_out"])], axis=1), _win_part(gr["w_aug"])])
    g1, u1, d1, dh4, *pieces_own = _ffn_bwd(sv["x0"], wts["ffn1_norm"][l:l + 1], *ws.ffn[(l, 0)], dx1, own)
    dx0, gr["ffn1_norm"] = _norm_bwd(sv["x0"], wts["ffn1_norm"][l:l + 1], dx1, dh4)
    gr.update(ffn1_w_gate=g1, ffn1_w_up=u1, ffn1_w_down=d1)
    return dx0, gr, pieces_before, pieces_own


def _device_step(x, pos, target, wts, ws, prep=None):
    freq = jnp.tile(ROPE_THETA ** (-jnp.arange(0, EH, 2, dtype=f32) / EH), 2 * ATT // EH).reshape(1, ATT)
    saved = []
    h = x
    for l in range(DEPTH):
        h, sv = _layer_fwd(l, h, pos, freq, wts, ws)
        saved.append(sv)
    dh, g_final, loss = _final(h, wts["final_norm"], target)
    grads = [None] * DEPTH
    dh, grads[1], _, _ = _layer_bwd(1, dh, saved[1], pos, freq, wts, ws)
    sums1 = None
    if prep is not None:
        g = grads[1]
        ffn = [g[f"ffn{f}_w_{n}"] for f in (1, 2) for n in ("gate", "up", "down")]
        sums1 = prep([jnp.concatenate(ffn + [_wout_part(g["w_out"])], axis=1), _win_part(g["w_aug"])])
    dh, grads[0], pieces1, pieces0 = _layer_bwd(0, dh, saved[0], pos, freq, wts, ws, sums1, prep)
    return loss, dh, g_final, grads, pieces1, pieces0


_SMALL = (("ffn1_norm", (DEPTH, D)), ("mix_norm", (DEPTH, D)), ("pool_w", (DEPTH, 4, 64, 64)), ("pool_scale", (DEPTH, 256)),
          ("dn_conv_w", (DEPTH, 4, CW)), ("dn_a_log", (DEPTH, 4)), ("dn_dt_bias", (DEPTH, 4)), ("dn_out_norm", (DEPTH, 128)),
          ("ffn2_norm", (DEPTH, D)), ("final_norm", (D,)), ("loss", (1,)))


def _pack_small(vals):
    rows = []
    for name, shape in _SMALL:
        flat = vals[name].astype(f32).reshape(-1)
        rows.append(jnp.pad(flat, (0, _small_rows(shape) * 128 - flat.shape[0])).reshape(-1, 128))
    out = jnp.concatenate(rows, axis=0)
    return _pad_rows(out, -(-out.shape[0] // 16) * 16)


def _small_rows(shape):
    return -(-int(np.prod(shape)) // 1024) * 8


def _unpack_small(packed):
    vals, r = {}, 0
    for name, shape in _SMALL:
        size, n = int(np.prod(shape)), _small_rows(shape)
        vals[name] = packed[r:r + n].reshape(-1)[:size].reshape(shape)
        r += n
    return vals


def kernel(x, positions, ffn1_norm, ffn1_w_gate, ffn1_w_up, ffn1_w_down, mix_norm, w_in, pool_w, pool_scale, dn_conv_w, dn_a_log, dn_dt_bias, dn_out_norm, w_out, ffn2_norm, ffn2_w_gate, ffn2_w_up, ffn2_w_down, final_norm, loss_target, m_ffn1_norm, m_ffn1_w_gate, m_ffn1_w_up, m_ffn1_w_down, m_mix_norm, m_w_in, m_pool_w, m_pool_scale, m_dn_conv_w, m_dn_a_log, m_dn_dt_bias, m_dn_out_norm, m_w_out, m_ffn2_norm, m_ffn2_w_gate, m_ffn2_w_up, m_ffn2_w_down, m_final_norm, v_ffn1_norm, v_ffn1_w_gate, v_ffn1_w_up, v_ffn1_w_down, v_mix_norm, v_w_in, v_pool_w, v_pool_scale, v_dn_conv_w, v_dn_a_log, v_dn_dt_bias, v_dn_out_norm, v_w_out, v_ffn2_norm, v_ffn2_w_gate, v_ffn2_w_up, v_ffn2_w_down, v_final_norm):
    names = ["ffn1_norm", "ffn1_w_gate", "ffn1_w_up", "ffn1_w_down", "mix_norm", "w_in", "pool_w", "pool_scale", "dn_conv_w",
             "dn_a_log", "dn_dt_bias", "dn_out_norm", "w_out", "ffn2_norm", "ffn2_w_gate", "ffn2_w_up", "ffn2_w_down", "final_norm"]
    W = dict(zip(names, [ffn1_norm, ffn1_w_gate, ffn1_w_up, ffn1_w_down, mix_norm, w_in, pool_w, pool_scale, dn_conv_w,
                         dn_a_log, dn_dt_bias, dn_out_norm, w_out, ffn2_norm, ffn2_w_gate, ffn2_w_up, ffn2_w_down, final_norm]))
    M = dict(zip(names, [m_ffn1_norm, m_ffn1_w_gate, m_ffn1_w_up, m_ffn1_w_down, m_mix_norm, m_w_in, m_pool_w, m_pool_scale,
                         m_dn_conv_w, m_dn_a_log, m_dn_dt_bias, m_dn_out_norm, m_w_out, m_ffn2_norm, m_ffn2_w_gate, m_ffn2_w_up,
                         m_ffn2_w_down, m_final_norm]))
    V = dict(zip(names, [v_ffn1_norm, v_ffn1_w_gate, v_ffn1_w_up, v_ffn1_w_down, v_mix_norm, v_w_in, v_pool_w, v_pool_scale,
                         v_dn_conv_w, v_dn_a_log, v_dn_dt_bias, v_dn_out_norm, v_w_out, v_ffn2_norm, v_ffn2_w_gate, v_ffn2_w_up,
                         v_ffn2_w_down, v_final_norm]))
    chip = 2 * lax.axis_index("x") + lax.axis_index("y")

    ffn_names = [(f"ffn{f}_w_gate", f"ffn{f}_w_up", f"ffn{f}_w_down") for f in (1, 2)]
    tr = lambda t: jnp.swapaxes(t, -1, -2)
    def ffn_rows(l, which):
        g, u, dn = ffn_names[which]
        return [tr(W[g][l]), tr(W[u][l]), W[dn][l]]

    def second_half(l):
        return jnp.concatenate(ffn_rows(l, 1) + [jnp.pad(W["w_out"][l], ((0, FC - 256), (0, 0)))], axis=0).astype(bf16)

    ws = _Weights()
    first0, = _all_gather_weights([jnp.concatenate(ffn_rows(0, 0), axis=0).astype(bf16)])
    ws.ffn[(0, 0)] = (first0, 0)

    def landed_00(gathered):
        ws.ffn[(0, 1)], ws.wout[0] = (gathered[0], 0), (gathered[0], 3)
        ws.set_w_in(0, gathered[1])

    def landed_01(gathered):
        ws.ffn[(1, 0)] = (gathered[0], 0)
        ws.set_w_in(1, gathered[1])

    def landed_10(gathered):
        ws.ffn[(1, 1)], ws.wout[1] = (gathered[0], 0), (gathered[0], 3)

    ws.rides[(0, 0)] = ([second_half(0), W["w_in"][0].astype(bf16)], landed_00)
    ws.rides[(0, 1)] = ([jnp.concatenate(ffn_rows(1, 0), axis=0).astype(bf16), W["w_in"][1].astype(bf16)], landed_01)
    ws.rides[(1, 0)] = ([second_half(1)], landed_10)
    conv_all = _gather_small(_pad_rows(dn_conv_w.reshape(DEPTH * 4 * (CW // NCH) // 128, 128), 32))
    conv_full = jnp.concatenate([conv_all[2 * j, :DEPTH * 4 * (CW // NCH) // 128].reshape(DEPTH, 4, CW // NCH) for j in range(NCH)],
                                axis=-1)

    par = jnp.pad(jnp.stack([dn_a_log, dn_dt_bias], axis=1), ((0, 0), (0, 6), (4, 120)))
    wts = dict(ffn1_norm=ffn1_norm, mix_norm=mix_norm, ffn2_norm=ffn2_norm, final_norm=final_norm.reshape(1, D),
               pool_bd=jnp.stack([_block_diag(pool_w[l]) for l in range(DEPTH)]).astype(bf16),
               pool_scale=pool_scale, conv_w=jnp.pad(conv_full, ((0, 0), (0, 4), (0, 0))),
               dn_par=par, dn_out_norm=dn_out_norm)

    c_arr = lax.axis_index("c").astype(jnp.int32).reshape(1)

    def prep(parts):
        return [_sum_core_pair(p, g, c_arr) for p, g in zip(parts, _send_sibling_half(parts))]

    loss, dx, g_final, grads, pieces1, pieces0 = _device_step(x[0], positions.reshape(S, 1), loss_target[0], wts, ws, prep)
    last = [jnp.concatenate([grads[0][n] for n in ffn_names[0]], axis=1)]
    pieces_last = _scatter_to_chips(prep(last))
    full_b = _sum_chips(pieces_last[0], c_arr, None, 0, RB)
    full_b = _sum_chips(pieces0[0], c_arr, full_b, 3 * FC, RB)
    full_b = _sum_chips(pieces1[0], c_arr, full_b, 7 * FC, RB)
    full_c = _sum_chips(pieces0[1], c_arr, None, 0, RC)
    full_c = _sum_chips(pieces1[1], c_arr, full_c, D, RC)
    full_b, full_c = _join_halves([full_b, full_c], [[(0, 3 * FC), (3 * FC, 4 * FC), (7 * FC, 7 * FC)], [(0, D), (D, D)]])

    small = {"loss": loss[0, 0:1], "final_norm": g_final.reshape(D)}
    for n in ("ffn1_norm", "mix_norm", "ffn2_norm", "pool_scale", "dn_out_norm"):
        small[n] = jnp.stack([grads[l][n].reshape(-1) for l in range(DEPTH)])
    small["pool_w"] = jnp.stack([jnp.stack([grads[l]["pool_bd"][64 * g:64 * (g + 1), 64 * g:64 * (g + 1)] for g in range(4)])
                                 for l in range(DEPTH)])
    small["dn_conv_w"] = jnp.stack([grads[l]["conv_w"][0:4] for l in range(DEPTH)])
    small["dn_a_log"] = jnp.stack([grads[l]["dn_par"][0, 4:8] for l in range(DEPTH)])
    small["dn_dt_bias"] = jnp.stack([grads[l]["dn_par"][1, 4:8] for l in range(DEPTH)])
    packed = _pack_small(small)
    g_small = _sum_pieces(_gather_small(packed), f32, "sum_small")
    gs = _unpack_small(g_small)

    transposed = ("ffn1_w_gate", "ffn1_w_up", "ffn2_w_gate", "ffn2_w_up")
    where = {"ffn1_w_gate": (full_b, FC // 2, lambda l: 14 * l), "ffn1_w_up": (full_b, FC // 2, lambda l: 14 * l + 2),
             "ffn1_w_down": (full_b, FC // 2, lambda l: 14 * l + 4), "ffn2_w_gate": (full_b, FC // 2, lambda l: 14 * l + 6),
             "ffn2_w_up": (full_b, FC // 2, lambda l: 14 * l + 8), "ffn2_w_down": (full_b, FC // 2, lambda l: 14 * l + 10),
             "w_out": (full_b, 64, lambda l: (FC // 64) * (7 * l + 6)), "w_in": (full_c, D // 2, lambda l: 2 * l)}
    big_res = {}
    for n, (gblob, tile, first) in where.items():
        t = tr if n in transposed else (lambda a: a)
        big_res[n] = [t(r) for r in _adamw_rows(t(W[n]), t(M[n]), t(V[n]), gblob, tile, first, "adamw_" + n)]

    def small_of(T):
        d = {n: T[n] for n, _ in _SMALL if n not in ("loss", "dn_conv_w")}
        d["loss"] = jnp.zeros((1,), f32)
        d["dn_conv_w"] = _at_own_columns(T["dn_conv_w"], chip)
        return _pack_small(d)

    res_s = _adamw(small_of(W), g_small, small_of(M), small_of(V), "adamw_small")
    small_out = [_unpack_small(r) for r in res_s]

    def split_blobs(b, c):
        out = {}
        b7 = b.reshape(DEPTH, 7, FC, D)
        for k, n in enumerate(n for names3 in ffn_names for n in names3):
            out[n] = tr(b7[:, k]) if n in transposed else b7[:, k]
        out["w_out"] = b7[:, 6, :256]
        out["w_in"] = c.reshape(DEPTH, D, INC)
        return out

    def assemble(big, sm):
        out = []
        for n in names:
            if n in big:
                out.append(big[n])
            elif n == "dn_conv_w":
                out.append(_own_columns(sm[n], chip))
            else:
                out.append(sm[n])
        return out

    grad_list = assemble(split_blobs(full_b, full_c), gs)
    outs = [gs["loss"].reshape(()), dx.reshape(1, S, D)] + grad_list
    for k in range(3):
        outs += assemble({n: r[k] for n, r in big_res.items()}, small_out[k])
    return tuple(outs)
```

```python
import functools
import math

import jax
import jax.numpy as jnp
import numpy as np
from jax import lax
from jax.experimental import pallas as pl
from jax.experimental.pallas import tpu as pltpu

f32 = jnp.float32
bf16 = jnp.bfloat16
SDS = jax.ShapeDtypeStruct
MESH = pl.DeviceIdType.MESH

S = 4096
D = 1024
DEPTH = 2
FF = 2816
NCH = 4
FC = FF // NCH
INW = 3080
INC = INW // NCH
INP = 3200
ATT = 256
EH = 64
NBLK = 128
DNW = 512
DH = 128
CH = 64
NCHUNK = S // CH
EPS = 1e-6
ROPE_THETA = 10000.0
PATTERN_DIL = (1, 4, 16)
ADAM_LR, ADAM_B1, ADAM_B2, ADAM_EPS, ADAM_WD, ADAM_STEP = 0.001, 0.9, 0.999, 1e-08, 0.01, 10
VMEM_BYTES_V7X = 64 * 1024 * 1024
NEG = -1e30

TM = 512
RB, RC = 14 * FC, 2 * D


def _cp(vmem_mb=48, sem=None):
    kw = dict(vmem_limit_bytes=vmem_mb * 1024 * 1024)
    if sem is not None:
        kw["dimension_semantics"] = sem
    return pltpu.CompilerParams(**kw)


def _pc(*args, **kwargs):
    pin = lambda s: pltpu.HBM(s.shape, s.dtype) if isinstance(s, SDS) and jnp.issubdtype(s.dtype, jnp.floating) else s
    out = kwargs["out_shape"]
    kwargs["out_shape"] = [pin(s) for s in out] if isinstance(out, (list, tuple)) else pin(out)
    call = pl.pallas_call(*args, **kwargs)

    def run(*operands):
        pinned = [pltpu.with_memory_space_constraint(o, pltpu.HBM) if jnp.issubdtype(o.dtype, jnp.floating) else o
                  for o in operands]
        return call(*pinned)

    return run


def _dot(a, b):
    return jnp.dot(a, b, preferred_element_type=f32)


def _dot_nt(a, b):
    return lax.dot_general(a, b, (((1,), (1,)), ((), ())), preferred_element_type=f32)


def _dot_tn(a, b):
    return lax.dot_general(a, b, (((0,), (0,)), ((), ())), preferred_element_type=f32)


def _rms(x, w):
    r = lax.rsqrt(jnp.mean(x * x, axis=-1, keepdims=True) + EPS)
    return x * r * w, r


def _rms_bwd(x, w, r, dh):
    xhat = x * r
    dw = jnp.sum(dh * xhat, axis=0, keepdims=True)
    dxh = dh * w
    dx = r * (dxh - xhat * jnp.mean(dxh * xhat, axis=-1, keepdims=True))
    return dx, dw


def _ffn_fwd(x, nw, blob, k0, ride=None):
    kg, ku, kd = k0, k0 + 1, k0 + 2
    nr = 0 if ride is None else len(ride)
    ni = S // TM

    def body(*refs):
        x_ref, nw_ref, wg_ref, wu_ref, wd_ref = refs[:5]
        ride_in = refs[5:5 + nr]
        o_ref = refs[5 + nr]
        ride_out = refs[6 + nr:6 + 2 * nr]
        h_scr, acc_scr = refs[6 + 2 * nr:8 + 2 * nr]
        sems = refs[8 + 2 * nr:]
        i = pl.program_id(0)
        j = pl.program_id(1)

        if nr:
            @pl.when(jnp.logical_and(i == 0, j == 0))
            def _():
                _gather_start(ride_in, ride_out, *sems)

        @pl.when(j == 0)
        def _():
            h, _ = _rms(x_ref[...], nw_ref[...])
            h_scr[...] = h.astype(bf16)
            acc_scr[...] = jnp.zeros_like(acc_scr)

        h = h_scr[...]
        g = _dot_nt(h, wg_ref[0])
        u = _dot_nt(h, wu_ref[0])
        a = (g * jax.nn.sigmoid(g) * u).astype(bf16)
        acc_scr[...] += _dot(a, wd_ref[0])

        @pl.when(j == NCH - 1)
        def _():
            o_ref[...] = x_ref[...] + 0.5 * acc_scr[...]

        if nr:
            @pl.when(jnp.logical_and(i == ni - 1, j == NCH - 1))
            def _():
                _gather_finish(ride_in, ride_out, *sems)

    wspec = lambda k: pl.BlockSpec((1, FC, D), lambda i, j: (j, k, 0))
    rides = [] if ride is None else list(ride)
    res = _pc(
        body, grid=(ni, NCH), name="ffn_fwd_ride" if nr else "ffn_fwd",
        in_specs=[pl.BlockSpec((TM, D), lambda i, j: (i, 0)),
                  pl.BlockSpec((1, D), lambda i, j: (0, 0)),
                  wspec(kg), wspec(ku), wspec(kd)] + [ANY] * nr,
        out_specs=[pl.BlockSpec((TM, D), lambda i, j: (i, 0))] + [ANY] * nr,
        out_shape=[SDS((S, D), f32)] + [SDS((NCH,) + r.shape, r.dtype) for r in rides],
        scratch_shapes=[pltpu.VMEM((TM, D), bf16), pltpu.VMEM((TM, D), f32)] + (_gather_sems(nr) if nr else []),
        compiler_params=_cp(40),
    )(x, nw, blob, blob, blob, *rides)
    return res if nr else res[0]


def _ffn_bwd(x, nw, blob, k0, dy, ride=None):
    nt = S // TM
    kg, ku, kd = k0, k0 + 1, k0 + 2
    nr = 0 if ride is None else len(ride)

    def body(*refs):
        x_ref, nw_ref, wg_ref, wu_ref, wd_ref, dy_ref = refs[:6]
        ride_in = refs[6:6 + nr]
        dwg_ref, dwu_ref, dwd_ref, dh_ref = refs[6 + nr:10 + nr]
        ride_out = refs[10 + nr:10 + 2 * nr]
        ag, au, ad = refs[10 + 2 * nr:13 + 2 * nr]
        sems = refs[13 + 2 * nr:]
        j = pl.program_id(0)
        i = pl.program_id(1)

        if nr:
            @pl.when(jnp.logical_and(j == 0, i == 0))
            def _():
                _scatter_start(ride_in, ride_out, *sems)

        @pl.when(i == 0)
        def _():
            ag[...] = jnp.zeros_like(ag)
            au[...] = jnp.zeros_like(au)
            ad[...] = jnp.zeros_like(ad)

        hf, _ = _rms(x_ref[...], nw_ref[...])
        h = hf.astype(bf16)
        g = _dot_nt(h, wg_ref[0])
        u = _dot_nt(h, wu_ref[0])
        sg = jax.nn.sigmoid(g)
        s = g * sg
        a = (s * u).astype(bf16)
        dyb = (0.5 * dy_ref[...]).astype(bf16)
        da = _dot_nt(dyb, wd_ref[0])
        ad[...] += _dot_tn(a, dyb)
        du = (da * s).astype(bf16)
        dg = (da * u * (sg * (1.0 + g * (1.0 - sg)))).astype(bf16)
        ag[...] += _dot_tn(dg, h)
        au[...] += _dot_tn(du, h)
        dh_ref[0] = (_dot(dg, wg_ref[0]) + _dot(du, wu_ref[0])).astype(bf16)

        @pl.when(i == nt - 1)
        def _():
            dwg_ref[0] = ag[...].astype(bf16)
            dwu_ref[0] = au[...].astype(bf16)
            dwd_ref[0] = ad[...].astype(bf16)

        if nr:
            @pl.when(jnp.logical_and(j == NCH - 1, i == nt - 1))
            def _():
                _scatter_finish(ride_in, ride_out, *sems)

    wspec = lambda k: pl.BlockSpec((1, FC, D), lambda j, i: (j, k, 0))
    gspec = pl.BlockSpec((1, FC, D), lambda j, i: (j, 0, 0))
    rides = [] if ride is None else list(ride)
    return _pc(
        body, grid=(NCH, nt), name="ffn_bwd_ride" if nr else "ffn_bwd",
        in_specs=[pl.BlockSpec((TM, D), lambda j, i: (i, 0)),
                  pl.BlockSpec((1, D), lambda j, i: (0, 0)),
                  wspec(kg), wspec(ku), wspec(kd),
                  pl.BlockSpec((TM, D), lambda j, i: (i, 0))] + [ANY] * nr,
        out_specs=[gspec, gspec, gspec, pl.BlockSpec((1, TM, D), lambda j, i: (j, i, 0))] + [ANY] * nr,
        out_shape=[SDS((NCH, FC, D), bf16)] * 3 + [SDS((NCH, S, D), bf16)] + [SDS(r.shape, r.dtype) for r in rides],
        scratch_shapes=[pltpu.VMEM((FC, D), f32)] * 3 + (_scatter_sems(nr) if nr else []),
        compiler_params=_cp(56),
    )(x, nw, blob, blob, blob, dy, *rides)


def _norm_bwd(x, nw, dres, dh4):
    nt = S // TM
    nparts = dh4.shape[0]

    def body(x_ref, nw_ref, dres_ref, dh_ref, dx_ref, dnw_ref):
        i = pl.program_id(0)
        dh = dh_ref[0].astype(f32)
        for p in range(1, nparts):
            dh = dh + dh_ref[p].astype(f32)
        xv = x_ref[...]
        _, r = _rms(xv, nw_ref[...])
        dx, dw = _rms_bwd(xv, nw_ref[...], r, dh)
        dx_ref[...] = dres_ref[...] + dx

        @pl.when(i == 0)
        def _():
            dnw_ref[...] = jnp.zeros_like(dnw_ref)

        dnw_ref[...] += dw

    return _pc(
        body, grid=(nt,), name="norm_bwd",
        in_specs=[pl.BlockSpec((TM, D), lambda i: (i, 0)),
                  pl.BlockSpec((1, D), lambda i: (0, 0)),
                  pl.BlockSpec((TM, D), lambda i: (i, 0)),
                  pl.BlockSpec((nparts, TM, D), lambda i: (0, i, 0))],
        out_specs=[pl.BlockSpec((TM, D), lambda i: (i, 0)), pl.BlockSpec((1, D), lambda i: (0, 0))],
        out_shape=[SDS((S, D), f32), SDS((1, D), f32)],
        compiler_params=_cp(40),
    )(x, nw, dres, dh4)


def _final(x, nw, target):
    nt = S // TM

    def body(x_ref, nw_ref, t_ref, dx_ref, dnw_ref, loss_ref):
        i = pl.program_id(0)
        xv = x_ref[...]
        y, r = _rms(xv, nw_ref[...])
        err = y - t_ref[...]
        part = 0.5 * jnp.sum(jnp.mean(err * err, axis=-1, keepdims=True), axis=0, keepdims=True)
        dx, dw = _rms_bwd(xv, nw_ref[...], r, err * (1.0 / D))
        dx_ref[...] = dx

        @pl.when(i == 0)
        def _():
            dnw_ref[...] = jnp.zeros_like(dnw_ref)
            loss_ref[...] = jnp.zeros_like(loss_ref)

        dnw_ref[...] += dw
        loss_ref[...] += jnp.broadcast_to(part, loss_ref.shape)

    return _pc(
        body, grid=(nt,), name="final_loss",
        in_specs=[pl.BlockSpec((TM, D), lambda i: (i, 0)),
                  pl.BlockSpec((1, D), lambda i: (0, 0)),
                  pl.BlockSpec((TM, D), lambda i: (i, 0))],
        out_specs=[pl.BlockSpec((TM, D), lambda i: (i, 0)), pl.BlockSpec((1, D), lambda i: (0, 0)),
                   pl.BlockSpec((1, 128), lambda i: (0, 0))],
        out_shape=[SDS((S, D), f32), SDS((1, D), f32), SDS((1, 128), f32)],
        compiler_params=_cp(40),
    )(x, nw, target)


def _rot_half(t):
    lane = lax.broadcasted_iota(jnp.int32, t.shape, 1)
    first = (lane % EH) < (EH // 2)
    return jnp.where(first, -pltpu.roll(t, ATT - EH // 2, 1), pltpu.roll(t, EH // 2, 1))


def _rope_tables(pos_ref, freq_ref):
    ang = pos_ref[...].astype(f32) * freq_ref[...]
    return jnp.cos(ang), jnp.sin(ang)


def _split_residues(val, scr, outs):
    rows, cols = val.shape
    for j in range(cols // 128):
        scr[j] = val[:, 128 * j:128 * (j + 1)]
    for ref, d in outs:
        for j in range(cols // 128):
            for r in range(d):
                ref.at[r][:, 128 * j:128 * (j + 1)] = scr.at[j][pl.ds(r, rows // d, stride=d), :]


def _join_residues(ref, d, scr):
    rows, cols = scr.shape[1], ref.shape[2]
    for j in range(cols // 128):
        for r in range(d):
            scr.at[j][pl.ds(r, rows // d, stride=d), :] = ref.at[r][:, 128 * j:128 * (j + 1)]
    return jnp.concatenate([scr[j] for j in range(cols // 128)], axis=1)


def _res_spec(d, tile, cols):
    return pl.BlockSpec((d, tile // d, cols), lambda i: (0, i, 0))


def _inproj_fwd(x, nw, w_aug, pos, freq):
    TI = 256

    def body(x_ref, nw_ref, w_hbm, pos_ref, freq_ref, att_ref, att4_ref, att16_ref, pu_ref, dq_ref, dz_ref, dba_ref,
             w_scr, r_scr):
        @pl.when(pl.program_id(0) == 0)
        def _():
            pltpu.sync_copy(w_hbm, w_scr)

        h, _ = _rms(x_ref[...], nw_ref[...])
        proj = _dot(h.astype(bf16), w_scr[...])
        cos, sin = _rope_tables(pos_ref, freq_ref)
        q = proj[:, 0:ATT]
        k = proj[:, ATT:2 * ATT]
        att = jnp.concatenate([q * cos + _rot_half(q) * sin, k * cos + _rot_half(k) * sin, proj[:, 2 * ATT:3 * ATT]], axis=1)
        att_ref[...] = att
        _split_residues(att, r_scr, [(att4_ref, 4), (att16_ref, 16)])
        pu_ref[...] = proj[:, 768:1024]
        dq_ref[...] = proj[:, 1024:2560]
        dz_ref[...] = proj[:, 2560:3072]
        dba_ref[...] = proj[:, 3072:3200]

    return _pc(
        body, grid=(S // TI,), name="inproj_fwd",
        in_specs=[pl.BlockSpec((TI, D), lambda i: (i, 0)),
                  pl.BlockSpec((1, D), lambda i: (0, 0)),
                  pl.BlockSpec(memory_space=pl.ANY),
                  pl.BlockSpec((TI, 1), lambda i: (i, 0)),
                  pl.BlockSpec((1, ATT), lambda i: (0, 0))],
        out_specs=[pl.BlockSpec((TI, 768), lambda i: (i, 0)), _res_spec(4, TI, 768), _res_spec(16, TI, 768),
                   pl.BlockSpec((TI, 256), lambda i: (i, 0)),
                   pl.BlockSpec((TI, 1536), lambda i: (i, 0)), pl.BlockSpec((TI, 512), lambda i: (i, 0)),
                   pl.BlockSpec((TI, 128), lambda i: (i, 0))],
        out_shape=[SDS((S, 768), f32), SDS((4, S // 4, 768), f32), SDS((16, S // 16, 768), f32), SDS((S, 256), f32),
                   SDS((S, 1536), f32), SDS((S, 512), f32), SDS((S, 128), f32)],
        scratch_shapes=[pltpu.VMEM((D, INP), bf16), pltpu.VMEM((6, TI, 128), f32)],
        compiler_params=_cp(48),
    )(x, nw, w_aug, pos, freq)


def _inproj_bwd(x, nw, w_aug, pos, freq, dres, datt, datt4, datt16, dpu, ddq, ddz, ddba):
    TI = 256
    nt = S // TI

    def body(x_ref, nw_ref, w_hbm, pos_ref, freq_ref, dres_ref, datt_ref, datt4_ref, datt16_ref, dpu_ref, ddq_ref, ddz_ref,
             ddba_ref, dx_ref, dnw_ref, dw_hbm, w_scr, acc, r_scr):
        i = pl.program_id(0)

        @pl.when(i == 0)
        def _():
            pltpu.sync_copy(w_hbm, w_scr)
            acc[...] = jnp.zeros_like(acc)
            dnw_ref[...] = jnp.zeros_like(dnw_ref)

        xv = x_ref[...]
        hf, r = _rms(xv, nw_ref[...])
        h = hf.astype(bf16)
        cos, sin = _rope_tables(pos_ref, freq_ref)
        datt = datt_ref[...] + _join_residues(datt4_ref, 4, r_scr)
        datt = datt + _join_residues(datt16_ref, 16, r_scr)
        dq = datt[:, 0:ATT]
        dk = datt[:, ATT:2 * ATT]
        dq = dq * cos - _rot_half(dq) * sin
        dk = dk * cos - _rot_half(dk) * sin
        dproj = jnp.concatenate([dq, dk, datt[:, 2 * ATT:3 * ATT], dpu_ref[...], ddq_ref[...], ddz_ref[...], ddba_ref[...]],
                                axis=1).astype(bf16)
        acc[...] += _dot_tn(h, dproj)
        dh = _dot_nt(dproj, w_scr[...])
        dx, dw = _rms_bwd(xv, nw_ref[...], r, dh)
        dx_ref[...] = dres_ref[...] + dx
        dnw_ref[...] += dw

        @pl.when(i == nt - 1)
        def _():
            pltpu.sync_copy(acc, dw_hbm)

    return _pc(
        body, grid=(nt,), name="inproj_bwd",
        in_specs=[pl.BlockSpec((TI, D), lambda i: (i, 0)),
                  pl.BlockSpec((1, D), lambda i: (0, 0)),
                  pl.BlockSpec(memory_space=pl.ANY),
                  pl.BlockSpec((TI, 1), lambda i: (i, 0)),
                  pl.BlockSpec((1, ATT), lambda i: (0, 0)),
                  pl.BlockSpec((TI, D), lambda i: (i, 0)),
                  pl.BlockSpec((TI, 768), lambda i: (i, 0)), _res_spec(4, TI, 768), _res_spec(16, TI, 768),
                  pl.BlockSpec((TI, 256), lambda i: (i, 0)),
                  pl.BlockSpec((TI, 1536), lambda i: (i, 0)),
                  pl.BlockSpec((TI, 512), lambda i: (i, 0)),
                  pl.BlockSpec((TI, 128), lambda i: (i, 0))],
        out_specs=[pl.BlockSpec((TI, D), lambda i: (i, 0)), pl.BlockSpec((1, D), lambda i: (0, 0)),
                   pl.BlockSpec(memory_space=pl.ANY)],
        out_shape=[SDS((S, D), f32), SDS((1, D), f32), SDS((D, INP), f32)],
        scratch_shapes=[pltpu.VMEM((D, INP), bf16), pltpu.VMEM((D, INP), f32), pltpu.VMEM((6, TI, 128), f32)],
        compiler_params=_cp(56),
    )(x, nw, w_aug, pos, freq, dres, datt, datt4, datt16, dpu, ddq, ddz, ddba)


def _outproj_fwd(x, ya, yb, yc, blob_b, kw):
    def body(x_ref, ya_ref, yb_ref, yc_ref, w_ref, o_ref):
        ycat = jnp.concatenate([ya_ref[...], yb_ref[...], yc_ref[...]], axis=1).astype(bf16)
        o_ref[...] = x_ref[...] + _dot(ycat, w_ref[:, 0:256, :].reshape(D, D))

    return _pc(
        body, grid=(S // TM,), name="outproj_fwd",
        in_specs=[pl.BlockSpec((TM, D), lambda i: (i, 0)),
                  pl.BlockSpec((TM, 256), lambda i: (i, 0)),
                  pl.BlockSpec((TM, 256), lambda i: (i, 0)),
                  pl.BlockSpec((TM, 512), lambda i: (i, 0)),
                  pl.BlockSpec((NCH, FC, D), lambda i: (0, kw, 0))],
        out_specs=pl.BlockSpec((TM, D), lambda i: (i, 0)),
        out_shape=SDS((S, D), f32),
        compiler_params=_cp(40),
    )(x, ya, yb, yc, blob_b)


def _outproj_bwd(dy, ya, yb, yc, blob_b, kw):
    nt = S // TM

    def body(dy_ref, ya_ref, yb_ref, yc_ref, w_ref, dya_ref, dyb_ref, dyc_ref, dw_ref):
        i = pl.program_id(0)

        @pl.when(i == 0)
        def _():
            dw_ref[...] = jnp.zeros_like(dw_ref)

        dyv = dy_ref[...].astype(bf16)
        ycat = jnp.concatenate([ya_ref[...], yb_ref[...], yc_ref[...]], axis=1).astype(bf16)
        dw_ref[...] += _dot_tn(ycat, dyv)
        dcat = _dot_nt(dyv, w_ref[:, 0:256, :].reshape(D, D))
        dya_ref[...] = dcat[:, 0:256]
        dyb_ref[...] = dcat[:, 256:512]
        dyc_ref[...] = dcat[:, 512:1024]

    return _pc(
        body, grid=(nt,), name="outproj_bwd",
        in_specs=[pl.BlockSpec((TM, D), lambda i: (i, 0)),
                  pl.BlockSpec((TM, 256), lambda i: (i, 0)),
                  pl.BlockSpec((TM, 256), lambda i: (i, 0)),
                  pl.BlockSpec((TM, 512), lambda i: (i, 0)),
                  pl.BlockSpec((NCH, FC, D), lambda i: (0, kw, 0))],
        out_specs=[pl.BlockSpec((TM, 256), lambda i: (i, 0)), pl.BlockSpec((TM, 256), lambda i: (i, 0)),
                   pl.BlockSpec((TM, 512), lambda i: (i, 0)), pl.BlockSpec((D, D), lambda i: (0, 0))],
        out_shape=[SDS((S, 256), f32), SDS((S, 256), f32), SDS((S, 512), f32), SDS((D, D), f32)],
        compiler_params=_cp(40),
    )(dy, ya, yb, yc, blob_b)


NB = S // NBLK


def _attn_block(q, kp, kc, vp, vc, first):
    kk = jnp.concatenate([kp, kc], axis=0).astype(bf16)
    vv = jnp.concatenate([vp, vc], axis=0).astype(bf16)
    qi = lax.broadcasted_iota(jnp.int32, (4 * NBLK, 2 * NBLK), 0) % NBLK
    ki = lax.broadcasted_iota(jnp.int32, (4 * NBLK, 2 * NBLK), 1)
    dist = NBLK + qi - ki
    valid = (dist >= 0) & (dist <= NBLK) & (jnp.logical_not(first) | (ki >= NBLK))
    head = lax.broadcasted_iota(jnp.int32, (1, ATT), 1) // EH
    masks = [(head == h).astype(f32) for h in range(4)]
    qs = jnp.concatenate([q * mh for mh in masks], axis=0).astype(bf16)
    s = _dot_nt(qs, kk) * (1.0 / math.sqrt(EH))
    s = jnp.where(valid, s, NEG)
    m = lax.stop_gradient(jnp.max(s, axis=-1, keepdims=True))
    p = jnp.exp(s - m)
    den = jnp.sum(p, axis=-1, keepdims=True)
    po = _dot((p / den).astype(bf16), vv)
    lse = m + jnp.log(den)
    o = jnp.zeros((NBLK, ATT), f32)
    l = jnp.zeros((NBLK, ATT), f32)
    for h, mh in enumerate(masks):
        o = o + po[NBLK * h:NBLK * (h + 1)] * mh
        l = l + lse[NBLK * h:NBLK * (h + 1)] * mh
    return o, l


def _attn_specs():
    prev = lambda b: jnp.maximum(b - 1, 0)
    return [pl.BlockSpec((NBLK, ATT), lambda b: (b, 0)),
            pl.BlockSpec((NBLK, ATT), lambda b: (prev(b), 1)),
            pl.BlockSpec((NBLK, ATT), lambda b: (b, 1)),
            pl.BlockSpec((NBLK, ATT), lambda b: (prev(b), 2)),
            pl.BlockSpec((NBLK, ATT), lambda b: (b, 2))]


def _attn_fwd(qkv, per_seq):
    def body(q_ref, kp_ref, kc_ref, vp_ref, vc_ref, o_ref, l_ref):
        first = pl.program_id(0) % per_seq == 0
        o, l = _attn_block(q_ref[...], kp_ref[...], kc_ref[...], vp_ref[...], vc_ref[...], first)
        o_ref[...] = o
        l_ref[...] = l

    blk = pl.BlockSpec((NBLK, ATT), lambda b: (b, 0))
    return _pc(
        body, grid=(NB,), name="attn_fwd", in_specs=_attn_specs(), out_specs=[blk, blk],
        out_shape=[SDS((S, ATT), f32), SDS((S, ATT), f32)], compiler_params=_cp(32),
    )(qkv, qkv, qkv, qkv, qkv)


def _attn_bwd(qkv, do, dl, per_seq):
    def body(q_ref, kp_ref, kc_ref, vp_ref, vc_ref, do_ref, dl_ref, o_ref, k_carry, v_carry):
        step = pl.program_id(0)

        @pl.when(step == 0)
        def _():
            k_carry[...] = jnp.zeros_like(k_carry)
            v_carry[...] = jnp.zeros_like(v_carry)

        b = NB - 1 - step
        first = b % per_seq == 0
        last = b % per_seq == per_seq - 1
        fn = lambda q, kp, kc, vp, vc: _attn_block(q, kp, kc, vp, vc, first)
        _, vjp = jax.vjp(fn, q_ref[...], kp_ref[...], kc_ref[...], vp_ref[...], vc_ref[...])
        dq, dkp, dkc, dvp, dvc = vjp((do_ref[...], dl_ref[...]))
        o_ref[:, 0:ATT] = dq
        o_ref[:, ATT:2 * ATT] = dkc + jnp.where(last, 0.0, k_carry[...])
        o_ref[:, 2 * ATT:3 * ATT] = dvc + jnp.where(last, 0.0, v_carry[...])
        k_carry[...] = dkp
        v_carry[...] = dvp

    rev = lambda s: NB - 1 - s
    prev = lambda s: jnp.maximum(rev(s) - 1, 0)
    specs = [pl.BlockSpec((NBLK, ATT), lambda s: (rev(s), 0)),
             pl.BlockSpec((NBLK, ATT), lambda s: (prev(s), 1)),
             pl.BlockSpec((NBLK, ATT), lambda s: (rev(s), 1)),
             pl.BlockSpec((NBLK, ATT), lambda s: (prev(s), 2)),
             pl.BlockSpec((NBLK, ATT), lambda s: (rev(s), 2)),
             pl.BlockSpec((NBLK, ATT), lambda s: (rev(s), 0)),
             pl.BlockSpec((NBLK, ATT), lambda s: (rev(s), 0))]
    return _pc(
        body, grid=(NB,), name="attn_bwd", in_specs=specs, out_specs=pl.BlockSpec((NBLK, 768), lambda s: (rev(s), 0)),
        out_shape=SDS((S, 768), f32), scratch_shapes=[pltpu.VMEM((NBLK, ATT), f32)] * 2, compiler_params=_cp(32),
    )(qkv, qkv, qkv, qkv, qkv, do, dl)


def _merge_weights(l0, l1, l2):
    m = jnp.maximum(jnp.maximum(l0, l1), l2)
    e0, e1, e2 = jnp.exp(l0 - m), jnp.exp(l1 - m), jnp.exp(l2 - m)
    tot = e0 + e1 + e2
    return e0 / tot, e1 / tot, e2 / tot


def _merge_specs():
    nat = pl.BlockSpec((TM, ATT), lambda i: (i, 0))
    return nat, _res_spec(4, TM, ATT), _res_spec(16, TM, ATT)


def _merge_fwd(o1, l1, o4, l4, o16, l16):
    def body(o1_ref, l1_ref, o4_ref, l4_ref, o16_ref, l16_ref, y_ref, scr):
        o4v, l4v = _join_residues(o4_ref, 4, scr), _join_residues(l4_ref, 4, scr)
        o16v, l16v = _join_residues(o16_ref, 16, scr), _join_residues(l16_ref, 16, scr)
        w0, w1, w2 = _merge_weights(l1_ref[...], l4v, l16v)
        y_ref[...] = w0 * o1_ref[...] + w1 * o4v + w2 * o16v

    nat, r4, r16 = _merge_specs()
    return _pc(body, grid=(S // TM,), name="merge_fwd", in_specs=[nat, nat, r4, r4, r16, r16],
                          out_specs=nat, out_shape=SDS((S, ATT), f32), scratch_shapes=[pltpu.VMEM((2, TM, 128), f32)],
                          compiler_params=_cp(32))(o1, l1, o4, l4, o16, l16)


def _merge_bwd(o1, l1, o4, l4, o16, l16, dy):
    def body(o1_ref, l1_ref, o4_ref, l4_ref, o16_ref, l16_ref, dy_ref, do1_ref, dl1_ref, do4_ref, dl4_ref, do16_ref, dl16_ref, scr):
        o4v, l4v = _join_residues(o4_ref, 4, scr), _join_residues(l4_ref, 4, scr)
        o16v, l16v = _join_residues(o16_ref, 16, scr), _join_residues(l16_ref, 16, scr)
        o1v = o1_ref[...]
        w0, w1, w2 = _merge_weights(l1_ref[...], l4v, l16v)
        y = w0 * o1v + w1 * o4v + w2 * o16v
        dyv = dy_ref[...]
        do1_ref[...] = w0 * dyv
        dl1_ref[...] = w0 * (o1v - y) * dyv
        _split_residues(w1 * dyv, scr, [(do4_ref, 4)])
        _split_residues(w1 * (o4v - y) * dyv, scr, [(dl4_ref, 4)])
        _split_residues(w2 * dyv, scr, [(do16_ref, 16)])
        _split_residues(w2 * (o16v - y) * dyv, scr, [(dl16_ref, 16)])

    nat, r4, r16 = _merge_specs()
    return _pc(body, grid=(S // TM,), name="merge_bwd", in_specs=[nat, nat, r4, r4, r16, r16, nat],
                          out_specs=[nat, nat, r4, r4, r16, r16],
                          out_shape=[SDS((S, ATT), f32)] * 2 + [SDS((4, S // 4, ATT), f32)] * 2 + [SDS((16, S // 16, ATT), f32)] * 2,
                          scratch_shapes=[pltpu.VMEM((2, TM, 128), f32)], compiler_params=_cp(32))(o1, l1, o4, l4, o16, l16, dy)


HALO = 16


def _pool_consts(i, rows):
    grp = lax.broadcasted_iota(jnp.int32, (rows, 256), 1) // 64
    t = i * TM + lax.broadcasted_iota(jnp.int32, (rows, 256), 0)
    win = jnp.where(grp == 0, 2, jnp.where(grp == 1, 4, jnp.where(grp == 2, 8, 16)))
    cnt = jnp.minimum(t + 1, win).astype(f32)
    return grp, cnt


def _pool_select(grp, s2, s4, s8, s16):
    return jnp.where(grp == 0, s2, jnp.where(grp == 1, s4, jnp.where(grp == 2, s8, s16)))


def _pooled(i, cur, halo):
    xx = jnp.concatenate([halo, cur], axis=0)
    s2 = xx + pltpu.roll(xx, 1, 0)
    s4 = s2 + pltpu.roll(s2, 2, 0)
    s8 = s4 + pltpu.roll(s4, 4, 0)
    s16 = s8 + pltpu.roll(s8, 8, 0)
    grp, cnt = _pool_consts(i, TM)
    tot = _pool_select(grp, s2[HALO:], s4[HALO:], s8[HALO:], s16[HALO:])
    return tot / cnt - cur


def _pool_fwd(u, wp, scale):
    def body(u_ref, halo_ref, wp_ref, sc_ref, y_ref):
        i = pl.program_id(0)
        halo = halo_ref[...] * (i > 0).astype(f32)
        pooled = _pooled(i, u_ref[...], halo)
        y_ref[...] = _dot(pooled.astype(bf16), wp_ref[...]) * sc_ref[...]

    return _pc(
        body, grid=(S // TM,), name="pool_fwd",
        in_specs=[pl.BlockSpec((TM, 256), lambda i: (i, 0)),
                  pl.BlockSpec((HALO, 256), lambda i: (jnp.maximum(i * (TM // HALO) - 1, 0), 0)),
                  pl.BlockSpec((256, 256), lambda i: (0, 0)),
                  pl.BlockSpec((1, 256), lambda i: (0, 0))],
        out_specs=pl.BlockSpec((TM, 256), lambda i: (i, 0)), out_shape=SDS((S, 256), f32), compiler_params=_cp(32),
    )(u, u, wp, scale)


def _pool_bwd(u, wp, scale, dy):
    nt = S // TM

    def body(u_ref, halo_ref, wp_ref, sc_ref, dy_ref, dyn_ref, du_ref, dwp_ref, dsc_ref):
        i = pl.program_id(0)

        @pl.when(i == 0)
        def _():
            dwp_ref[...] = jnp.zeros_like(dwp_ref)
            dsc_ref[...] = jnp.zeros_like(dsc_ref)

        halo = halo_ref[...] * (i > 0).astype(f32)
        pooled = _pooled(i, u_ref[...], halo).astype(bf16)
        dyv = dy_ref[...]
        dsc_ref[...] += jnp.sum(dyv * _dot(pooled, wp_ref[...]), axis=0, keepdims=True)
        dys = (dyv * sc_ref[...]).astype(bf16)
        dwp_ref[...] += _dot_tn(pooled, dys)
        dpool = _dot_nt(dys, wp_ref[...])
        grp, cnt = _pool_consts(i, TM)
        dyn = ((dyn_ref[...] * (i < nt - 1).astype(f32)) * sc_ref[...]).astype(bf16)
        _, cntn = _pool_consts(i + 1, HALO)
        zn = _dot_nt(dyn, wp_ref[...]) / cntn
        zz = jnp.concatenate([dpool / cnt, zn], axis=0)
        n = TM + HALO
        a2 = zz + pltpu.roll(zz, n - 1, 0)
        a4 = a2 + pltpu.roll(a2, n - 2, 0)
        a8 = a4 + pltpu.roll(a4, n - 4, 0)
        a16 = a8 + pltpu.roll(a8, n - 8, 0)
        du_ref[...] = _pool_select(grp, a2[:TM], a4[:TM], a8[:TM], a16[:TM]) - dpool

    return _pc(
        body, grid=(nt,), name="pool_bwd",
        in_specs=[pl.BlockSpec((TM, 256), lambda i: (i, 0)),
                  pl.BlockSpec((HALO, 256), lambda i: (jnp.maximum(i * (TM // HALO) - 1, 0), 0)),
                  pl.BlockSpec((256, 256), lambda i: (0, 0)),
                  pl.BlockSpec((1, 256), lambda i: (0, 0)),
                  pl.BlockSpec((TM, 256), lambda i: (i, 0)),
                  pl.BlockSpec((HALO, 256), lambda i: (jnp.minimum((i + 1) * (TM // HALO), S // HALO - 1), 0))],
        out_specs=[pl.BlockSpec((TM, 256), lambda i: (i, 0)), pl.BlockSpec((256, 256), lambda i: (0, 0)),
                   pl.BlockSpec((1, 256), lambda i: (0, 0))],
        out_shape=[SDS((S, 256), f32), SDS((256, 256), f32), SDS((1, 256), f32)], compiler_params=_cp(32),
    )(u, u, wp, scale, dy, dy)


CW = 3 * DNW
CHALO = 8
TC = 256


def _conv_fwd(u, w):
    def body(u_ref, halo_ref, w_ref, c_ref):
        i = pl.program_id(0)
        xx = jnp.concatenate([halo_ref[...] * (i > 0).astype(f32), u_ref[...]], axis=0)
        c = (w_ref[3:4, :] * xx + w_ref[2:3, :] * pltpu.roll(xx, 1, 0) + w_ref[1:2, :] * pltpu.roll(xx, 2, 0)
             + w_ref[0:1, :] * pltpu.roll(xx, 3, 0))
        c_ref[...] = c[CHALO:]

    return _pc(
        body, grid=(S // TC,), name="conv_fwd",
        in_specs=[pl.BlockSpec((TC, CW), lambda i: (i, 0)),
                  pl.BlockSpec((CHALO, CW), lambda i: (jnp.maximum(i * (TC // CHALO) - 1, 0), 0)),
                  pl.BlockSpec((8, CW), lambda i: (0, 0))],
        out_specs=pl.BlockSpec((TC, CW), lambda i: (i, 0)), out_shape=SDS((S, CW), f32), compiler_params=_cp(32),
    )(u, u, w)


def _conv_bwd(u, w, dc):
    nt = S // TC

    def body(u_ref, halo_ref, w_ref, dc_ref, dcn_ref, du_ref, dw_ref):
        i = pl.program_id(0)

        @pl.when(i == 0)
        def _():
            dw_ref[...] = jnp.zeros_like(dw_ref)

        dcv = dc_ref[...]
        zz = jnp.concatenate([dcv, dcn_ref[...] * (i < nt - 1).astype(f32)], axis=0)
        n = TC + CHALO
        du = (w_ref[3:4, :] * zz + w_ref[2:3, :] * pltpu.roll(zz, n - 1, 0) + w_ref[1:2, :] * pltpu.roll(zz, n - 2, 0)
              + w_ref[0:1, :] * pltpu.roll(zz, n - 3, 0))
        du_ref[...] = du[:TC]
        xx = jnp.concatenate([halo_ref[...] * (i > 0).astype(f32), u_ref[...]], axis=0)
        for j in range(4):
            shifted = xx if j == 3 else pltpu.roll(xx, 3 - j, 0)
            dw_ref[j:j + 1, :] += jnp.sum(dcv * shifted[CHALO:], axis=0, keepdims=True)

    return _pc(
        body, grid=(nt,), name="conv_bwd",
        in_specs=[pl.BlockSpec((TC, CW), lambda i: (i, 0)),
                  pl.BlockSpec((CHALO, CW), lambda i: (jnp.maximum(i * (TC // CHALO) - 1, 0), 0)),
                  pl.BlockSpec((8, CW), lambda i: (0, 0)),
                  pl.BlockSpec((TC, CW), lambda i: (i, 0)),
                  pl.BlockSpec((CHALO, CW), lambda i: (jnp.minimum((i + 1) * (TC // CHALO), S // CHALO - 1), 0))],
        out_specs=[pl.BlockSpec((TC, CW), lambda i: (i, 0)), pl.BlockSpec((8, CW), lambda i: (0, 0))],
        out_shape=[SDS((S, CW), f32), SDS((8, CW), f32)], compiler_params=_cp(32),
    )(u, u, w, dc, dc)


TL = 512
NCL = TL // CH


def _bdot(a, b):
    return jnp.einsum('nik,nkj->nij', a.astype(bf16), b.astype(bf16), preferred_element_type=f32)


def _bdot_nt(a, b):
    return jnp.einsum('nik,njk->nij', a.astype(bf16), b.astype(bf16), preferred_element_type=f32)


def _bdot_tn(a, b):
    return jnp.einsum('nki,nkj->nij', a.astype(bf16), b.astype(bf16), preferred_element_type=f32)


@jax.custom_vjp
def _inv_unit_lower(a):
    ii = lax.broadcasted_iota(jnp.int32, (1, CH, CH), 1)
    jj = lax.broadcasted_iota(jnp.int32, (1, CH, CH), 2)
    t = (ii == jj).astype(f32) - a
    p = a
    for _ in range(5):
        p = _bdot(p, p)
        t = t + _bdot(t, p)
    return t


def _inv_unit_lower_fwd(a):
    t = _inv_unit_lower(a)
    return t, t


def _inv_unit_lower_bwd(t, dt):
    return (-_bdot_tn(t, _bdot_nt(dt, t)),)


_inv_unit_lower.defvjp(_inv_unit_lower_fwd, _inv_unit_lower_bwd)


def _dn_local(c, dba, a_row, b_row):
    act = c * jax.nn.sigmoid(c)
    lane = lax.broadcasted_iota(jnp.int32, (TL, 128), 1)
    beta_all = jax.nn.sigmoid(dba)
    xs = dba + b_row
    softplus = jnp.maximum(xs, 0.0) + jnp.log(1.0 + jnp.exp(-jnp.abs(xs)))
    g_all = -jnp.exp(a_row) * softplus
    ii = lax.broadcasted_iota(jnp.int32, (1, CH, CH), 1)
    jj = lax.broadcasted_iota(jnp.int32, (1, CH, CH), 2)
    lower = jj <= ii
    strict = jj < ii
    eye = (ii == jj).astype(f32)
    us, ws, qgs, kds, intras = [], [], [], [], []
    aux = jnp.zeros((TL, 128), f32)
    for h in range(4):
        q = act[:, DH * h:DH * (h + 1)]
        k = act[:, DNW + DH * h:DNW + DH * (h + 1)]
        v = act[:, 2 * DNW + DH * h:2 * DNW + DH * (h + 1)]
        q = q * lax.rsqrt(jnp.sum(q * q, axis=-1, keepdims=True) + EPS) * (DH ** -0.5)
        k = k * lax.rsqrt(jnp.sum(k * k, axis=-1, keepdims=True) + EPS)
        beta = jnp.sum(jnp.where(lane == h, beta_all, 0.0), axis=1, keepdims=True)
        g = jnp.sum(jnp.where(lane == 4 + h, g_all, 0.0), axis=1, keepdims=True)
        q3, k3, v3 = q.reshape(NCL, CH, DH), k.reshape(NCL, CH, DH), v.reshape(NCL, CH, DH)
        beta3, g3 = beta.reshape(NCL, CH, 1), g.reshape(NCL, CH, 1)
        g_row = jnp.sum(eye * g3, axis=1, keepdims=True)
        gc_col = jnp.sum(jnp.where(lower, g_row, 0.0), axis=2, keepdims=True)
        gc_row = jnp.sum(jnp.where(ii <= jj, g3, 0.0), axis=1, keepdims=True)
        diff = gc_col - gc_row
        decay = jnp.where(lower, jnp.exp(jnp.where(lower, diff, 0.0)), 0.0)
        kb = k3 * beta3
        vb = v3 * beta3
        a = jnp.where(strict, _bdot_nt(kb, k3) * decay, 0.0)
        t = _inv_unit_lower(a)
        u3 = _bdot(t, vb)
        w3 = _bdot(t, kb * jnp.exp(gc_col))
        intra = jnp.where(lower, _bdot_nt(q3, k3) * decay, 0.0)
        g_last = jnp.sum(g3, axis=1, keepdims=True)
        us.append(u3.reshape(TL, DH))
        ws.append(w3.reshape(TL, DH))
        qgs.append((q3 * jnp.exp(gc_col)).reshape(TL, DH))
        kds.append((k3 * jnp.exp(g_last - gc_col)).reshape(TL, DH))
        intras.append(intra.reshape(TL, CH))
        e_last = jnp.broadcast_to(jnp.exp(g_last), (NCL, CH, 1)).reshape(TL, 1)
        aux = aux + jnp.where(lane == h, e_last, 0.0)
    cat = lambda xs: jnp.concatenate(xs, axis=1)
    return cat(us), cat(ws), cat(qgs), cat(kds), jnp.stack(intras, axis=0), aux


def _dn_local_fwd(c, dba, par):
    def body(c_ref, dba_ref, par_ref, u_ref, w_ref, qg_ref, kd_ref, in_ref, aux_ref):
        u, w, qg, kd, intra, aux = _dn_local(c_ref[...], dba_ref[...], par_ref[0:1, :], par_ref[1:2, :])
        u_ref[...] = u
        w_ref[...] = w
        qg_ref[...] = qg
        kd_ref[...] = kd
        in_ref[...] = intra
        aux_ref[...] = aux

    wide = pl.BlockSpec((TL, DNW), lambda i: (i, 0))
    return _pc(
        body, grid=(S // TL,), name="dn_local_fwd",
        in_specs=[pl.BlockSpec((TL, CW), lambda i: (i, 0)), pl.BlockSpec((TL, 128), lambda i: (i, 0)),
                  pl.BlockSpec((8, 128), lambda i: (0, 0))],
        out_specs=[wide, wide, wide, wide, pl.BlockSpec((4, TL, CH), lambda i: (0, i, 0)),
                   pl.BlockSpec((TL, 128), lambda i: (i, 0))],
        out_shape=[SDS((S, DNW), f32)] * 4 + [SDS((4, S, CH), f32), SDS((S, 128), f32)], compiler_params=_cp(48),
    )(c, dba, par)


def _dn_local_bwd(c, dba, par, du, dw, dqg, dkd, dintra, daux):
    def body(c_ref, dba_ref, par_ref, du_ref, dw_ref, dqg_ref, dkd_ref, din_ref, daux_ref, dc_ref, ddba_ref, dpar_ref):
        @pl.when(pl.program_id(0) == 0)
        def _():
            dpar_ref[...] = jnp.zeros_like(dpar_ref)

        _, vjp = jax.vjp(_dn_local, c_ref[...], dba_ref[...], par_ref[0:1, :], par_ref[1:2, :])
        dc, ddba, da_row, db_row = vjp((du_ref[...], dw_ref[...], dqg_ref[...], dkd_ref[...], din_ref[...], daux_ref[...]))
        dc_ref[...] = dc
        ddba_ref[...] = ddba
        dpar_ref[0:1, :] += da_row
        dpar_ref[1:2, :] += db_row

    wide = pl.BlockSpec((TL, DNW), lambda i: (i, 0))
    return _pc(
        body, grid=(S // TL,), name="dn_local_bwd",
        in_specs=[pl.BlockSpec((TL, CW), lambda i: (i, 0)), pl.BlockSpec((TL, 128), lambda i: (i, 0)),
                  pl.BlockSpec((8, 128), lambda i: (0, 0)), wide, wide, wide, wide,
                  pl.BlockSpec((4, TL, CH), lambda i: (0, i, 0)), pl.BlockSpec((TL, 128), lambda i: (i, 0))],
        out_specs=[pl.BlockSpec((TL, CW), lambda i: (i, 0)), pl.BlockSpec((TL, 128), lambda i: (i, 0)),
                   pl.BlockSpec((8, 128), lambda i: (0, 0))],
        out_shape=[SDS((S, CW), f32), SDS((S, 128), f32), SDS((8, 128), f32)], compiler_params=_cp(56),
    )(c, dba, par, du, dw, dqg, dkd, dintra, daux)


def _dn_step(state, u, w, qg, kd, intra, aux):
    lane = lax.broadcasted_iota(jnp.int32, (CH, 128), 1)
    row = lax.broadcasted_iota(jnp.int32, (CH, 128), 0)
    outs, states = [], []
    for h in range(4):
        sl = slice(DH * h, DH * (h + 1))
        st = state[h]
        e = jnp.sum(jnp.sum(jnp.where((lane == h) & (row == 0), aux, 0.0), axis=1, keepdims=True), axis=0, keepdims=True)
        v_new = u[:, sl] - _dot(w[:, sl].astype(bf16), st.astype(bf16))
        vb = v_new.astype(bf16)
        outs.append(_dot(qg[:, sl].astype(bf16), st.astype(bf16)) + _dot(intra[h].astype(bf16), vb))
        states.append(st * e + _dot_tn(kd[:, sl].astype(bf16), vb))
    return jnp.concatenate(outs, axis=1), jnp.stack(states, axis=0)


def _dn_rec_fwd(u, w, qg, kd, intra, aux):
    def body(u_ref, w_ref, qg_ref, kd_ref, in_ref, aux_ref, o_ref, st_ref, st_scr):
        @pl.when(pl.program_id(0) == 0)
        def _():
            st_scr[...] = jnp.zeros_like(st_scr)

        st = st_scr[...]
        st_ref[0] = st
        o, new = _dn_step(st, u_ref[...], w_ref[...], qg_ref[...], kd_ref[...], in_ref[...], aux_ref[...])
        o_ref[...] = o
        st_scr[...] = new

    wide = pl.BlockSpec((CH, DNW), lambda n: (n, 0))
    return _pc(
        body, grid=(NCHUNK,), name="dn_rec_fwd",
        in_specs=[wide, wide, wide, wide, pl.BlockSpec((4, CH, CH), lambda n: (0, n, 0)),
                  pl.BlockSpec((CH, 128), lambda n: (n, 0))],
        out_specs=[wide, pl.BlockSpec((1, 4, DH, DH), lambda n: (n, 0, 0, 0))],
        out_shape=[SDS((S, DNW), f32), SDS((NCHUNK, 4, DH, DH), f32)],
        scratch_shapes=[pltpu.VMEM((4, DH, DH), f32)], compiler_params=_cp(32),
    )(u, w, qg, kd, intra, aux)


def _dn_rec_bwd(u, w, qg, kd, intra, aux, states, do):
    def body(u_ref, w_ref, qg_ref, kd_ref, in_ref, aux_ref, st_ref, do_ref,
             du_ref, dw_ref, dqg_ref, dkd_ref, din_ref, daux_ref, ds_scr):
        @pl.when(pl.program_id(0) == 0)
        def _():
            ds_scr[...] = jnp.zeros_like(ds_scr)

        _, vjp = jax.vjp(_dn_step, st_ref[0], u_ref[...], w_ref[...], qg_ref[...], kd_ref[...], in_ref[...], aux_ref[...])
        dst, du, dw, dqg, dkd, din, daux = vjp((do_ref[...], ds_scr[...]))
        du_ref[...] = du
        dw_ref[...] = dw
        dqg_ref[...] = dqg
        dkd_ref[...] = dkd
        din_ref[...] = din
        daux_ref[...] = daux
        ds_scr[...] = dst

    rev = lambda n: NCHUNK - 1 - n
    wide = pl.BlockSpec((CH, DNW), lambda n: (rev(n), 0))
    inb = pl.BlockSpec((4, CH, CH), lambda n: (0, rev(n), 0))
    auxb = pl.BlockSpec((CH, 128), lambda n: (rev(n), 0))
    return _pc(
        body, grid=(NCHUNK,), name="dn_rec_bwd",
        in_specs=[wide, wide, wide, wide, inb, auxb, pl.BlockSpec((1, 4, DH, DH), lambda n: (rev(n), 0, 0, 0)), wide],
        out_specs=[wide, wide, wide, wide, inb, auxb],
        out_shape=[SDS((S, DNW), f32)] * 4 + [SDS((4, S, CH), f32), SDS((S, 128), f32)],
        scratch_shapes=[pltpu.VMEM((4, DH, DH), f32)], compiler_params=_cp(32),
    )(u, w, qg, kd, intra, aux, states, do)


def _dn_post(o, z, nw):
    parts = []
    for h in range(4):
        sl = slice(DH * h, DH * (h + 1))
        oh = o[:, sl]
        y = oh * lax.rsqrt(jnp.mean(oh * oh, axis=-1, keepdims=True) + EPS) * nw
        zh = z[:, sl]
        parts.append(y * (zh * jax.nn.sigmoid(zh)))
    return jnp.concatenate(parts, axis=1)


def _dn_post_fwd(o, z, nw):
    def body(o_ref, z_ref, nw_ref, y_ref):
        y_ref[...] = _dn_post(o_ref[...], z_ref[...], nw_ref[...])

    wide = pl.BlockSpec((TM, DNW), lambda i: (i, 0))
    return _pc(body, grid=(S // TM,), name="dn_post_fwd",
                          in_specs=[wide, wide, pl.BlockSpec((1, 128), lambda i: (0, 0))], out_specs=wide,
                          out_shape=SDS((S, DNW), f32), compiler_params=_cp(32))(o, z, nw)


def _dn_post_bwd(o, z, nw, dy):
    def body(o_ref, z_ref, nw_ref, dy_ref, do_ref, dz_ref, dnw_ref):
        @pl.when(pl.program_id(0) == 0)
        def _():
            dnw_ref[...] = jnp.zeros_like(dnw_ref)

        _, vjp = jax.vjp(_dn_post, o_ref[...], z_ref[...], nw_ref[...])
        do, dz, dnw = vjp(dy_ref[...])
        do_ref[...] = do
        dz_ref[...] = dz
        dnw_ref[...] += dnw

    wide = pl.BlockSpec((TM, DNW), lambda i: (i, 0))
    one = pl.BlockSpec((1, 128), lambda i: (0, 0))
    return _pc(body, grid=(S // TM,), name="dn_post_bwd", in_specs=[wide, wide, one, wide],
                          out_specs=[wide, wide, one], out_shape=[SDS((S, DNW), f32), SDS((S, DNW), f32), SDS((1, 128), f32)],
                          compiler_params=_cp(32))(o, z, nw, dy)


def _row_tile(rows, width, itemsize=4, target=2 * 1024 * 1024):
    best = None
    for t in range(16, rows + 1, 16):
        if rows % t == 0 and t * width * itemsize <= target:
            best = t
    return best if best is not None else rows


def _sum_pieces(pieces, out_dtype, name):
    n, rows, width = pieces.shape
    tr = _row_tile(rows, width * n)

    def body(p_ref, o_ref):
        acc = p_ref[0].astype(f32)
        for s in range(1, n):
            acc = acc + p_ref[s].astype(f32)
        o_ref[...] = acc.astype(out_dtype)

    return _pc(body, grid=(rows // tr,), name=name,
                          in_specs=[pl.BlockSpec((n, tr, width), lambda i: (0, i, 0))],
                          out_specs=pl.BlockSpec((tr, width), lambda i: (i, 0)),
                          out_shape=SDS((rows, width), out_dtype), compiler_params=_cp(32))(pieces)


def _sum_core_pair(part, got, c_arr):
    n, rows, width = part.shape
    half = rows // 2
    tr = _row_tile(half, width, itemsize=2)
    nt = half // tr

    def body(c_ref, p_ref, g_ref, o_ref):
        o_ref[...] = (p_ref[...].astype(f32) + g_ref[...].astype(f32)).astype(bf16)

    gs = pltpu.PrefetchScalarGridSpec(
        num_scalar_prefetch=1, grid=(n, nt),
        in_specs=[pl.BlockSpec((1, tr, width), lambda j, i, c: (j, c[0] * nt + i, 0)),
                  pl.BlockSpec((1, tr, width), lambda j, i, c: (j, i, 0))],
        out_specs=pl.BlockSpec((1, tr, width), lambda j, i, c: (j, i, 0)))
    return _pc(body, grid_spec=gs, name="sum_core_pair", out_shape=SDS((n, half, width), bf16),
                          compiler_params=_cp(32))(c_arr, part, got)


def _sum_chips(pieces, c_arr, full, row0, total_rows):
    n, half, width = pieces.shape
    tr = max(t for t in range(16, 257, 16) if half % t == 0 and row0 % t == 0)
    nt = half // tr

    def body(c_ref, p_ref, *rest):
        o_ref = rest[-1]
        acc = p_ref[0].astype(f32)
        for s in range(1, n):
            acc = acc + p_ref[s].astype(f32)
        o_ref[...] = acc

    gs = pltpu.PrefetchScalarGridSpec(
        num_scalar_prefetch=1, grid=(nt,),
        in_specs=[pl.BlockSpec((n, tr, width), lambda i, c: (0, i, 0))] + ([] if full is None else [ANY]),
        out_specs=pl.BlockSpec((tr, width), lambda i, c: (row0 // tr + c[0] * nt + i, 0)))
    args = (c_arr, pieces) if full is None else (c_arr, pieces, full)
    return _pc(body, grid_spec=gs, name="sum_chips", out_shape=SDS((total_rows, width), f32),
                          input_output_aliases={} if full is None else {2: 0}, compiler_params=_cp(32))(*args)


def _adamw_math(w, g, m, v):
    mn = ADAM_B1 * m + (1.0 - ADAM_B1) * g
    vn = ADAM_B2 * v + (1.0 - ADAM_B2) * (g * g)
    m_hat = mn / (1.0 - ADAM_B1 ** ADAM_STEP)
    v_hat = vn / (1.0 - ADAM_B2 ** ADAM_STEP)
    return -ADAM_LR * (m_hat / (jnp.sqrt(v_hat) + ADAM_EPS) + ADAM_WD * w), mn, vn


def _adamw(w, g, m, v, name):
    rows, width = w.shape
    tr = _row_tile(rows, width * 7, target=12 * 1024 * 1024)

    def body(w_ref, g_ref, m_ref, v_ref, d_ref, nm_ref, nv_ref):
        d_ref[...], nm_ref[...], nv_ref[...] = _adamw_math(w_ref[...], g_ref[...], m_ref[...], v_ref[...])

    blk = pl.BlockSpec((tr, width), lambda i: (i, 0))
    return _pc(body, grid=(rows // tr,), name=name, in_specs=[blk] * 4, out_specs=[blk] * 3,
                          out_shape=[SDS((rows, width), f32)] * 3, compiler_params=_cp(40))(w, g, m, v)


def _adamw_rows(w, m, v, gblob, tr, first_tile, name):
    layers, rows, width = w.shape

    def body(w_ref, g_ref, m_ref, v_ref, d_ref, nm_ref, nv_ref):
        d_ref[0], nm_ref[0], nv_ref[0] = _adamw_math(w_ref[0], g_ref[...], m_ref[0], v_ref[0])

    blk = pl.BlockSpec((1, tr, width), lambda l, i: (l, i, 0))
    gblk = pl.BlockSpec((tr, width), lambda l, i: (first_tile(l) + i, 0))
    return _pc(body, grid=(layers, rows // tr), name=name, in_specs=[blk, gblk, blk, blk], out_specs=[blk] * 3,
                          out_shape=[SDS(w.shape, f32)] * 3, compiler_params=_cp(40))(w, gblob, m, v)


ANY = pl.BlockSpec(memory_space=pl.ANY)


def _place():
    x, y, c = lax.axis_index("x"), lax.axis_index("y"), lax.axis_index("c")
    chips = [(1 - x, y), (x, 1 - y), (1 - x, 1 - y)]
    return x, y, c, chips


NQ_ICI = 4
NQ_D2D = 8


def _chunks(rows, want):
    n = max(k for k in range(1, want + 1) if rows % k == 0 and (rows // k) % 16 == 0)
    step = rows // n
    return [(q * step, step) for q in range(n)]


def _scatter_copies(ins, outs, ssem, rsem, lsem):
    x, y, c, chips = _place()
    me = (x, y, c)
    locals_, sends, lands = [], [], []
    for b in range(len(ins)):
        for q, (off, n) in enumerate(_chunks(ins[b].shape[1], NQ_ICI)):
            rows = pl.ds(off, n)
            mine = outs[b].at[2 * x + y, rows, :]
            locals_.append(pltpu.make_async_copy(ins[b].at[2 * x + y, rows, :], mine, lsem.at[b, q]))
            for j, chip in enumerate(chips):
                sends.append(_remote(ins[b].at[2 * chip[0] + chip[1], rows, :], mine, ssem.at[b, j, q], rsem.at[b, j, q],
                                     (*chip, c)))
                slot = outs[b].at[2 * chip[0] + chip[1], rows, :]
                lands.append(_remote(slot, slot, ssem.at[b, j, q], rsem.at[b, j, q], me))
    return locals_, sends, lands


def _scatter_start(ins, outs, ssem, rsem, lsem):
    locals_, sends, _ = _scatter_copies(ins, outs, ssem, rsem, lsem)
    for cp in locals_ + sends:
        cp.start()


def _scatter_finish(ins, outs, ssem, rsem, lsem):
    locals_, sends, lands = _scatter_copies(ins, outs, ssem, rsem, lsem)
    for cp in lands:
        cp.wait_recv()
    for cp in sends:
        cp.wait_send()
    for cp in locals_:
        cp.wait()


def _scatter_sems(nb):
    return [pltpu.SemaphoreType.DMA((nb, 3, NQ_ICI)), pltpu.SemaphoreType.DMA((nb, 3, NQ_ICI)),
            pltpu.SemaphoreType.DMA((nb, NQ_ICI))]


def _remote(src, dst, ssem, rsem, dev):
    return pltpu.make_async_remote_copy(src_ref=src, dst_ref=dst, send_sem=ssem, recv_sem=rsem, device_id=dev,
                                        device_id_type=MESH)


def _all_gather_weights(shards):
    nb = len(shards)

    def body(*refs):
        ins, outs, sems = refs[:nb], refs[nb:2 * nb], refs[2 * nb:]
        _gather_start(ins, outs, *sems)
        _gather_finish(ins, outs, *sems)

    return _pc(
        body, name="all_gather_weights", in_specs=[ANY] * nb, out_specs=[ANY] * nb,
        out_shape=[SDS((NCH,) + s.shape, s.dtype) for s in shards], scratch_shapes=_gather_sems(nb),
    )(*shards)


def _gather_first(ins, outs, ssem, rsem, lsem):
    x, y, c, chips = _place()
    locals_, sends = [], []
    for b in range(len(ins)):
        half = ins[b].shape[0] // 2
        for q, (off, n) in enumerate(_chunks(half, NQ_ICI)):
            mine = pl.ds(c * half + off, n)
            own = outs[b].at[2 * x + y, mine, :]
            locals_.append(pltpu.make_async_copy(ins[b].at[mine, :], own, lsem.at[b, q]))
            sends.append(_remote(ins[b].at[mine, :], own, ssem.at[b, 0, q], rsem.at[b, 0, q], (x, y, 1 - c)))
            sends += [_remote(ins[b].at[mine, :], own, ssem.at[b, 1 + j, q], rsem.at[b, 1 + j, q], (*chip, c))
                      for j, chip in enumerate(chips)]
    return locals_, sends


def _gather_start(ins, outs, ssem, rsem, lsem):
    locals_, sends = _gather_first(ins, outs, ssem, rsem, lsem)
    for cp in locals_ + sends:
        cp.start()


def _gather_finish(ins, outs, ssem, rsem, lsem):
    x, y, c, chips = _place()
    me, sib = (x, y, c), (x, y, 1 - c)
    locals_, sends = _gather_first(ins, outs, ssem, rsem, lsem)
    for b in range(len(ins)):
        half = ins[b].shape[0] // 2
        for q, (off, n) in enumerate(_chunks(half, NQ_ICI)):
            mine = pl.ds(c * half + off, n)
            for j, chip in enumerate(chips):
                landed = outs[b].at[2 * chip[0] + chip[1], mine, :]
                _remote(landed, landed, ssem.at[b, 1 + j, q], rsem.at[b, 1 + j, q], me).wait_recv()
                cp = _remote(landed, landed, ssem.at[b, 4 + j, q], rsem.at[b, 4 + j, q], sib)
                cp.start()
                sends.append(cp)
    for b in range(len(ins)):
        half = ins[b].shape[0] // 2
        for q, (off, n) in enumerate(_chunks(half, NQ_ICI)):
            other = pl.ds((1 - c) * half + off, n)
            theirs = outs[b].at[2 * x + y, other, :]
            _remote(theirs, theirs, ssem.at[b, 0, q], rsem.at[b, 0, q], me).wait_recv()
            for j, chip in enumerate(chips):
                fwd = outs[b].at[2 * chip[0] + chip[1], other, :]
                _remote(fwd, fwd, ssem.at[b, 4 + j, q], rsem.at[b, 4 + j, q], me).wait_recv()
    for cp in sends:
        cp.wait_send()
    for cp in locals_:
        cp.wait()


def _gather_sems(nb):
    return [pltpu.SemaphoreType.DMA((nb, 7, NQ_ICI)), pltpu.SemaphoreType.DMA((nb, 7, NQ_ICI)),
            pltpu.SemaphoreType.DMA((nb, NQ_ICI))]


def _send_sibling_half(parts):
    nb = len(parts)

    def body(*refs):
        ins, gots = refs[:nb], refs[nb:2 * nb]
        ssem, rsem = refs[2 * nb:]
        x, y, c, _ = _place()
        sib = (x, y, 1 - c)
        todo = []
        for b in range(nb):
            half = ins[b].shape[1] // 2
            for q, (off, n) in enumerate(_chunks(half, NQ_D2D)):
                cp = _remote(ins[b].at[:, pl.ds((1 - c) * half + off, n), :], gots[b].at[:, pl.ds(off, n), :],
                             ssem.at[b, q], rsem.at[b, q], sib)
                cp.start()
                todo.append(cp)
        for cp in todo:
            cp.wait()

    return _pc(
        body, name="send_sibling_half", in_specs=[ANY] * nb, out_specs=[ANY] * nb,
        out_shape=[SDS((p.shape[0], p.shape[1] // 2, p.shape[2]), p.dtype) for p in parts],
        scratch_shapes=[pltpu.SemaphoreType.DMA((nb, NQ_D2D)), pltpu.SemaphoreType.DMA((nb, NQ_D2D))],
    )(*parts)


def _scatter_to_chips(parts):
    nb = len(parts)

    def body(*refs):
        ins, outs, sems = refs[:nb], refs[nb:2 * nb], refs[2 * nb:]
        _scatter_start(ins, outs, *sems)
        _scatter_finish(ins, outs, *sems)

    return _pc(
        body, name="scatter_to_chips", in_specs=[ANY] * nb, out_specs=[ANY] * nb,
        out_shape=[SDS(p.shape, p.dtype) for p in parts], scratch_shapes=_scatter_sems(nb),
    )(*parts)


def _join_halves(fulls, ranges):
    nb = len(fulls)
    nr = max(len(r) for r in ranges)

    def body(*refs):
        ins, outs = refs[:nb], refs[nb:2 * nb]
        ssem, rsem = refs[2 * nb:]
        x, y, c, _ = _place()
        sib = (x, y, 1 - c)
        sends, lands = [], []
        for b in range(nb):
            for g, (row0, rows) in enumerate(ranges[b]):
                half = rows // 2
                for q, (off, n) in enumerate(_chunks(half, NQ_D2D)):
                    mine = pl.ds(row0 + c * half + off, n)
                    sends.append(_remote(ins[b].at[mine, :], outs[b].at[mine, :], ssem.at[b, g, q], rsem.at[b, g, q], sib))
                    other = outs[b].at[pl.ds(row0 + (1 - c) * half + off, n), :]
                    lands.append(_remote(other, other, ssem.at[b, g, q], rsem.at[b, g, q], sib))
        for cp in sends:
            cp.start()
        for cp in lands:
            cp.wait_recv()
        for cp in sends:
            cp.wait_send()

    return _pc(
        body, name="join_halves", in_specs=[ANY] * nb, out_specs=[ANY] * nb,
        out_shape=[SDS(h.shape, h.dtype) for h in fulls], input_output_aliases={b: b for b in range(nb)},
        scratch_shapes=[pltpu.SemaphoreType.DMA((nb, nr, NQ_D2D)), pltpu.SemaphoreType.DMA((nb, nr, NQ_D2D))],
    )(*fulls)


def _gather_small(vec):
    def body(v_ref, o_ref, ssem, rsem, lsem):
        x, y, c, _ = _place()
        mine = o_ref.at[4 * x + 2 * y + c]
        local = pltpu.make_async_copy(v_ref, mine, lsem)
        local.start()
        sends = []
        for k in range(1, 8):
            peer = (x ^ (k >> 2), y ^ ((k >> 1) & 1), c ^ (k & 1))
            cp = _remote(v_ref, mine, ssem.at[k - 1], rsem.at[k - 1], peer)
            cp.start()
            sends.append(cp)
        for k in range(1, 8):
            px, py, pc = x ^ (k >> 2), y ^ ((k >> 1) & 1), c ^ (k & 1)
            slot = o_ref.at[4 * px + 2 * py + pc]
            _remote(slot, slot, ssem.at[k - 1], rsem.at[k - 1], (x, y, c)).wait_recv()
        for cp in sends:
            cp.wait_send()
        local.wait()

    return _pc(
        body, name="gather_small", in_specs=[ANY], out_specs=ANY, out_shape=SDS((8,) + vec.shape, vec.dtype),
        scratch_shapes=[pltpu.SemaphoreType.DMA((7,)), pltpu.SemaphoreType.DMA((7,)), pltpu.SemaphoreType.DMA],
    )(vec)


def _block_diag(pw):
    return jnp.concatenate([jnp.pad(pw[g], ((0, 0), (64 * g, 192 - 64 * g))) for g in range(4)], axis=0)


def _own_columns(full, chip):
    n = full.shape[-1] // NCH
    parts = full.reshape(full.shape[:-1] + (NCH, n))
    sel = (lax.broadcasted_iota(jnp.int32, (NCH, 1), 0) == chip)
    return jnp.sum(jnp.where(sel, parts, 0.0), axis=-2)


def _at_own_columns(shard, chip):
    n = shard.shape[-1]
    sel = (lax.broadcasted_iota(jnp.int32, (NCH * n,), 0) // n == chip)
    return jnp.where(sel, jnp.tile(shard, NCH), 0.0)


def _pad_rows(a, rows):
    return jnp.pad(a, ((0, rows - a.shape[0]),) + ((0, 0),) * (a.ndim - 1))


def _ffn_block(l, which):
    return 7 * l + 3 * which


def _wout_block(l):
    return 7 * l + 6


class _Weights:
    def __init__(self):
        self.ffn, self.wout, self.w_aug, self.rides = {}, {}, {}, {}

    @classmethod
    def from_blob(cls, blob, w_aug):
        self = cls()
        for l in range(DEPTH):
            self.ffn[(l, 0)], self.ffn[(l, 1)] = (blob, _ffn_block(l, 0)), (blob, _ffn_block(l, 1))
            self.wout[l], self.w_aug[l] = (blob, _wout_block(l)), w_aug[l]
        return self

    def set_w_in(self, l, gathered):
        self.w_aug[l] = jnp.pad(gathered.transpose(1, 0, 2).reshape(D, INW), ((0, 0), (0, INP - INW)))

    def ffn_fwd(self, l, which, x, nw):
        arr, k0 = self.ffn[(l, which)]
        if (l, which) not in self.rides:
            return _ffn_fwd(x, nw, arr, k0)
        shards, landed = self.rides[(l, which)]
        out, *gathered = _ffn_fwd(x, nw, arr, k0, shards)
        landed(gathered)
        return out


def _layer_fwd(l, x0, pos, freq, wts, ws):
    sv = {"x0": x0}
    x1 = ws.ffn_fwd(l, 0, x0, wts["ffn1_norm"][l:l + 1])
    att, att4, att16, pu, dq, dz, dba = _inproj_fwd(x1, wts["mix_norm"][l:l + 1], ws.w_aug[l], pos, freq)
    qkvs = [att, att4.reshape(S, 768), att16.reshape(S, 768)]
    (o1, l1), (o4, l4), (o16, l16) = [_attn_fwd(q, NB // d) for q, d in zip(qkvs, PATTERN_DIL)]
    ols = (o1, l1, o4.reshape(4, S // 4, ATT), l4.reshape(4, S // 4, ATT), o16.reshape(16, S // 16, ATT),
           l16.reshape(16, S // 16, ATT))
    ya = _merge_fwd(*ols)
    yb = _pool_fwd(pu, wts["pool_bd"][l], wts["pool_scale"][l:l + 1])
    c = _conv_fwd(dq, wts["conv_w"][l])
    u, w, qg, kd, intra, aux = _dn_local_fwd(c, dba, wts["dn_par"][l])
    o_dn, states = _dn_rec_fwd(u, w, qg, kd, intra, aux)
    yc = _dn_post_fwd(o_dn, dz, wts["dn_out_norm"][l:l + 1])
    x2 = _outproj_fwd(x1, ya, yb, yc, *ws.wout[l])
    x3 = ws.ffn_fwd(l, 1, x2, wts["ffn2_norm"][l:l + 1])
    sv.update(x1=x1, x2=x2, qkvs=qkvs, ols=ols, ya=ya, yb=yb, yc=yc, pu=pu, dq=dq, dz=dz, dba=dba, c=c,
              u=u, w=w, qg=qg, kd=kd, intra=intra, aux=aux, states=states, o_dn=o_dn)
    return x3, sv


def _wout_part(g):
    return jnp.pad(g.astype(bf16).reshape(NCH, 256, D), ((0, 0), (0, FC - 256), (0, 0)))


def _win_part(g):
    return g[:, :INW].astype(bf16).reshape(D, NCH, INC).transpose(1, 0, 2)


def _layer_bwd(l, dx3, sv, pos, freq, wts, ws, ride=None, prep=None):
    gr = {}
    g2, u2, d2, dh4, *pieces_before = _ffn_bwd(sv["x2"], wts["ffn2_norm"][l:l + 1], *ws.ffn[(l, 1)], dx3, ride)
    dx2, gr["ffn2_norm"] = _norm_bwd(sv["x2"], wts["ffn2_norm"][l:l + 1], dx3, dh4)
    gr.update(ffn2_w_gate=g2, ffn2_w_up=u2, ffn2_w_down=d2)
    dya, dyb, dyc, gr["w_out"] = _outproj_bwd(dx2, sv["ya"], sv["yb"], sv["yc"], *ws.wout[l])
    do_dn, ddz, gr["dn_out_norm"] = _dn_post_bwd(sv["o_dn"], sv["dz"], wts["dn_out_norm"][l:l + 1], dyc)
    du, dw, dqg, dkd, dintra, daux = _dn_rec_bwd(sv["u"], sv["w"], sv["qg"], sv["kd"], sv["intra"], sv["aux"], sv["states"], do_dn)
    dc, ddba, gr["dn_par"] = _dn_local_bwd(sv["c"], sv["dba"], wts["dn_par"][l], du, dw, dqg, dkd, dintra, daux)
    ddq, gr["conv_w"] = _conv_bwd(sv["dq"], wts["conv_w"][l], dc)
    dpu, gr["pool_bd"], gr["pool_scale"] = _pool_bwd(sv["pu"], wts["pool_bd"][l], wts["pool_scale"][l:l + 1], dyb)
    dols = _merge_bwd(*sv["ols"], dya)
    datts = [_attn_bwd(q, dols[2 * p].reshape(S, ATT), dols[2 * p + 1].reshape(S, ATT), NB // d)
             for p, (q, d) in enumerate(zip(sv["qkvs"], PATTERN_DIL))]
    dx1, gr["mix_norm"], gr["w_aug"] = _inproj_bwd(sv["x1"], wts["mix_norm"][l:l + 1], ws.w_aug[l], pos, freq, dx2,
                                                    datts[0], datts[1].reshape(4, S // 4, 768),
                                                    datts[2].reshape(16, S // 16, 768), dpu, ddq, ddz, ddba)
    own = None
    if prep is not None:
        own = prep([jnp.concatenate([g2, u2, d2, _wout_part(gr["w_out"])], axis=1), _win_part(gr["w_aug"])])
    g1, u1, d1, dh4, *pieces_own = _ffn_bwd(sv["x0"], wts["ffn1_norm"][l:l + 1], *ws.ffn[(l, 0)], dx1, own)
    dx0, gr["ffn1_norm"] = _norm_bwd(sv["x0"], wts["ffn1_norm"][l:l + 1], dx1, dh4)
    gr.update(ffn1_w_gate=g1, ffn1_w_up=u1, ffn1_w_down=d1)
    return dx0, gr, pieces_before, pieces_own


def _device_step(x, pos, target, wts, ws, prep=None):
    freq = jnp.tile(ROPE_THETA ** (-jnp.arange(0, EH, 2, dtype=f32) / EH), 2 * ATT // EH).reshape(1, ATT)
    saved = []
    h = x
    for l in range(DEPTH):
        h, sv = _layer_fwd(l, h, pos, freq, wts, ws)
        saved.append(sv)
    dh, g_final, loss = _final(h, wts["final_norm"], target)
    grads = [None] * DEPTH
    dh, grads[1], _, _ = _layer_bwd(1, dh, saved[1], pos, freq, wts, ws)
    sums1 = None
    if prep is not None:
        g = grads[1]
        ffn = [g[f"ffn{f}_w_{n}"] for f in (1, 2) for n in ("gate", "up", "down")]
        sums1 = prep([jnp.concatenate(ffn + [_wout_part(g["w_out"])], axis=1), _win_part(g["w_aug"])])
    dh, grads[0], pieces1, pieces0 = _layer_bwd(0, dh, saved[0], pos, freq, wts, ws, sums1, prep)
    return loss, dh, g_final, grads, pieces1, pieces0


_SMALL = (("ffn1_norm", (DEPTH, D)), ("mix_norm", (DEPTH, D)), ("pool_w", (DEPTH, 4, 64, 64)), ("pool_scale", (DEPTH, 256)),
          ("dn_conv_w", (DEPTH, 4, CW)), ("dn_a_log", (DEPTH, 4)), ("dn_dt_bias", (DEPTH, 4)), ("dn_out_norm", (DEPTH, 128)),
          ("ffn2_norm", (DEPTH, D)), ("final_norm", (D,)), ("loss", (1,)))


def _pack_small(vals):
    rows = []
    for name, shape in _SMALL:
        flat = vals[name].astype(f32).reshape(-1)
        rows.append(jnp.pad(flat, (0, _small_rows(shape) * 128 - flat.shape[0])).reshape(-1, 128))
    out = jnp.concatenate(rows, axis=0)
    return _pad_rows(out, -(-out.shape[0] // 16) * 16)


def _small_rows(shape):
    return -(-int(np.prod(shape)) // 1024) * 8


def _unpack_small(packed):
    vals, r = {}, 0
    for name, shape in _SMALL:
        size, n = int(np.prod(shape)), _small_rows(shape)
        vals[name] = packed[r:r + n].reshape(-1)[:size].reshape(shape)
        r += n
    return vals


def kernel(x, positions, ffn1_norm, ffn1_w_gate, ffn1_w_up, ffn1_w_down, mix_norm, w_in, pool_w, pool_scale, dn_conv_w, dn_a_log, dn_dt_bias, dn_out_norm, w_out, ffn2_norm, ffn2_w_gate, ffn2_w_up, ffn2_w_down, final_norm, loss_target, m_ffn1_norm, m_ffn1_w_gate, m_ffn1_w_up, m_ffn1_w_down, m_mix_norm, m_w_in, m_pool_w, m_pool_scale, m_dn_conv_w, m_dn_a_log, m_dn_dt_bias, m_dn_out_norm, m_w_out, m_ffn2_norm, m_ffn2_w_gate, m_ffn2_w_up, m_ffn2_w_down, m_final_norm, v_ffn1_norm, v_ffn1_w_gate, v_ffn1_w_up, v_ffn1_w_down, v_mix_norm, v_w_in, v_pool_w, v_pool_scale, v_dn_conv_w, v_dn_a_log, v_dn_dt_bias, v_dn_out_norm, v_w_out, v_ffn2_norm, v_ffn2_w_gate, v_ffn2_w_up, v_ffn2_w_down, v_final_norm):
    names = ["ffn1_norm", "ffn1_w_gate", "ffn1_w_up", "ffn1_w_down", "mix_norm", "w_in", "pool_w", "pool_scale", "dn_conv_w",
             "dn_a_log", "dn_dt_bias", "dn_out_norm", "w_out", "ffn2_norm", "ffn2_w_gate", "ffn2_w_up", "ffn2_w_down", "final_norm"]
    W = dict(zip(names, [ffn1_norm, ffn1_w_gate, ffn1_w_up, ffn1_w_down, mix_norm, w_in, pool_w, pool_scale, dn_conv_w,
                         dn_a_log, dn_dt_bias, dn_out_norm, w_out, ffn2_norm, ffn2_w_gate, ffn2_w_up, ffn2_w_down, final_norm]))
    M = dict(zip(names, [m_ffn1_norm, m_ffn1_w_gate, m_ffn1_w_up, m_ffn1_w_down, m_mix_norm, m_w_in, m_pool_w, m_pool_scale,
                         m_dn_conv_w, m_dn_a_log, m_dn_dt_bias, m_dn_out_norm, m_w_out, m_ffn2_norm, m_ffn2_w_gate, m_ffn2_w_up,
                         m_ffn2_w_down, m_final_norm]))
    V = dict(zip(names, [v_ffn1_norm, v_ffn1_w_gate, v_ffn1_w_up, v_ffn1_w_down, v_mix_norm, v_w_in, v_pool_w, v_pool_scale,
                         v_dn_conv_w, v_dn_a_log, v_dn_dt_bias, v_dn_out_norm, v_w_out, v_ffn2_norm, v_ffn2_w_gate, v_ffn2_w_up,
                         v_ffn2_w_down, v_final_norm]))
    chip = 2 * lax.axis_index("x") + lax.axis_index("y")

    ffn_names = [(f"ffn{f}_w_gate", f"ffn{f}_w_up", f"ffn{f}_w_down") for f in (1, 2)]
    tr = lambda t: jnp.swapaxes(t, -1, -2)
    def ffn_rows(l, which):
        g, u, dn = ffn_names[which]
        return [tr(W[g][l]), tr(W[u][l]), W[dn][l]]

    def second_half(l):
        return jnp.concatenate(ffn_rows(l, 1) + [jnp.pad(W["w_out"][l], ((0, FC - 256), (0, 0)))], axis=0).astype(bf16)

    ws = _Weights()
    first0, = _all_gather_weights([jnp.concatenate(ffn_rows(0, 0), axis=0).astype(bf16)])
    ws.ffn[(0, 0)] = (first0, 0)

    def landed_00(gathered):
        ws.ffn[(0, 1)], ws.wout[0] = (gathered[0], 0), (gathered[0], 3)
        ws.set_w_in(0, gathered[1])

    def landed_01(gathered):
        ws.ffn[(1, 0)] = (gathered[0], 0)
        ws.set_w_in(1, gathered[1])

    def landed_10(gathered):
        ws.ffn[(1, 1)], ws.wout[1] = (gathered[0], 0), (gathered[0], 3)

    ws.rides[(0, 0)] = ([second_half(0), W["w_in"][0].astype(bf16)], landed_00)
    ws.rides[(0, 1)] = ([jnp.concatenate(ffn_rows(1, 0), axis=0).astype(bf16), W["w_in"][1].astype(bf16)], landed_01)
    ws.rides[(1, 0)] = ([second_half(1)], landed_10)
    conv_all = _gather_small(_pad_rows(dn_conv_w.reshape(DEPTH * 4 * (CW // NCH) // 128, 128), 32))
    conv_full = jnp.concatenate([conv_all[2 * j, :DEPTH * 4 * (CW // NCH) // 128].reshape(DEPTH, 4, CW // NCH) for j in range(NCH)],
                                axis=-1)

    par = jnp.pad(jnp.stack([dn_a_log, dn_dt_bias], axis=1), ((0, 0), (0, 6), (4, 120)))
    wts = dict(ffn1_norm=ffn1_norm, mix_norm=mix_norm, ffn2_norm=ffn2_norm, final_norm=final_norm.reshape(1, D),
               pool_bd=jnp.stack([_block_diag(pool_w[l]) for l in range(DEPTH)]).astype(bf16),
               pool_scale=pool_scale, conv_w=jnp.pad(conv_full, ((0, 0), (0, 4), (0, 0))),
               dn_par=par, dn_out_norm=dn_out_norm)

    c_arr = lax.axis_index("c").astype(jnp.int32).reshape(1)

    def prep(parts):
        return [_sum_core_pair(p, g, c_arr) for p, g in zip(parts, _send_sibling_half(parts))]

    loss, dx, g_final, grads, pieces1, pieces0 = _device_step(x[0], positions.reshape(S, 1), loss_target[0], wts, ws, prep)
    last = [jnp.concatenate([grads[0][n] for n in ffn_names[0]], axis=1)]
    pieces_last = _scatter_to_chips(prep(last))
    full_b = _sum_chips(pieces_last[0], c_arr, None, 0, RB)
    full_b = _sum_chips(pieces0[0], c_arr, full_b, 3 * FC, RB)
    full_b = _sum_chips(pieces1[0], c_arr, full_b, 7 * FC, RB)
    full_c = _sum_chips(pieces0[1], c_arr, None, 0, RC)
    full_c = _sum_chips(pieces1[1], c_arr, full_c, D, RC)
    full_b, full_c = _join_halves([full_b, full_c], [[(0, 3 * FC), (3 * FC, 4 * FC), (7 * FC, 7 * FC)], [(0, D), (D, D)]])

    small = {"loss": loss[0, 0:1], "final_norm": g_final.reshape(D)}
    for n in ("ffn1_norm", "mix_norm", "ffn2_norm", "pool_scale", "dn_out_norm"):
        small[n] = jnp.stack([grads[l][n].reshape(-1) for l in range(DEPTH)])
    small["pool_w"] = jnp.stack([jnp.stack([grads[l]["pool_bd"][64 * g:64 * (g + 1), 64 * g:64 * (g + 1)] for g in range(4)])
                                 for l in range(DEPTH)])
    small["dn_conv_w"] = jnp.stack([grads[l]["conv_w"][0:4] for l in range(DEPTH)])
    small["dn_a_log"] = jnp.stack([grads[l]["dn_par"][0, 4:8] for l in range(DEPTH)])
    small["dn_dt_bias"] = jnp.stack([grads[l]["dn_par"][1, 4:8] for l in range(DEPTH)])
    packed = _pack_small(small)
    g_small = _sum_pieces(_gather_small(packed), f32, "sum_small")
    gs = _unpack_small(g_small)

    transposed = ("ffn1_w_gate", "ffn1_w_up", "ffn2_w_gate", "ffn2_w_up")
    where = {"ffn1_w_gate": (full_b, FC // 2, lambda l: 14 * l), "ffn1_w_up": (full_b, FC // 2, lambda l: 14 * l + 2),
             "ffn1_w_down": (full_b, FC // 2, lambda l: 14 * l + 4), "ffn2_w_gate": (full_b, FC // 2, lambda l: 14 * l + 6),
             "ffn2_w_up": (full_b, FC // 2, lambda l: 14 * l + 8), "ffn2_w_down": (full_b, FC // 2, lambda l: 14 * l + 10),
             "w_out": (full_b, 64, lambda l: (FC // 64) * (7 * l + 6)), "w_in": (full_c, D // 2, lambda l: 2 * l)}
    big_res = {}
    for n, (gblob, tile, first) in where.items():
        t = tr if n in transposed else (lambda a: a)
        big_res[n] = [t(r) for r in _adamw_rows(t(W[n]), t(M[n]), t(V[n]), gblob, tile, first, "adamw_" + n)]

    def small_of(T):
        d = {n: T[n] for n, _ in _SMALL if n not in ("loss", "dn_conv_w")}
        d["loss"] = jnp.zeros((1,), f32)
        d["dn_conv_w"] = _at_own_columns(T["dn_conv_w"], chip)
        return _pack_small(d)

    res_s = _adamw(small_of(W), g_small, small_of(M), small_of(V), "adamw_small")
    small_out = [_unpack_small(r) for r in res_s]

    def split_blobs(b, c):
        out = {}
        b7 = b.reshape(DEPTH, 7, FC, D)
        for k, n in enumerate(n for names3 in ffn_names for n in names3):
            out[n] = tr(b7[:, k]) if n in transposed else b7[:, k]
        out["w_out"] = b7[:, 6, :256]
        out["w_in"] = c.reshape(DEPTH, D, INC)
        return out

    def assemble(big, sm):
        out = []
        for n in names:
            if n in big:
                out.append(big[n])
            elif n == "dn_conv_w":
                out.append(_own_columns(sm[n], chip))
            else:
                out.append(sm[n])
        return out

    grad_list = assemble(split_blobs(full_b, full_c), gs)
    outs = [gs["loss"].reshape(()), dx.reshape(1, S, D)] + grad_list
    for k in range(3):
        outs += assemble({n: r[k] for n, r in big_res.items()}, small_out[k])
    return tuple(outs)
```

```python
import functools
import math

import jax
import jax.numpy as jnp
import numpy as np
from jax import lax
from jax.experimental import pallas as pl
from jax.experimental.pallas import tpu as pltpu

f32 = jnp.float32
bf16 = jnp.bfloat16
SDS = jax.ShapeDtypeStruct
MESH = pl.DeviceIdType.MESH

S = 4096
D = 1024
DEPTH = 2
FF = 2816
NCH = 4
FC = FF // NCH
INW = 3080
INC = INW // NCH
INP = 3200
ATT = 256
EH = 64
NBLK = 128
DNW = 512
DH = 128
CH = 64
NCHUNK = S // CH
EPS = 1e-6
ROPE_THETA = 10000.0
PATTERN_DIL = (1, 4, 16)
ADAM_LR, ADAM_B1, ADAM_B2, ADAM_EPS, ADAM_WD, ADAM_STEP = 0.001, 0.9, 0.999, 1e-08, 0.01, 10
VMEM_BYTES_V7X = 64 * 1024 * 1024
NEG = -1e30

TM = 512
RB, RC = 14 * FC, 2 * D


def _cp(vmem_mb=48, sem=None):
    kw = dict(vmem_limit_bytes=vmem_mb * 1024 * 1024)
    if sem is not None:
        kw["dimension_semantics"] = sem
    return pltpu.CompilerParams(**kw)


def _pc(*args, **kwargs):
    pin = lambda s: pltpu.HBM(s.shape, s.dtype) if isinstance(s, SDS) and jnp.issubdtype(s.dtype, jnp.floating) else s
    out = kwargs["out_shape"]
    kwargs["out_shape"] = [pin(s) for s in out] if isinstance(out, (list, tuple)) else pin(out)
    call = pl.pallas_call(*args, **kwargs)

    def run(*operands):
        pinned = [pltpu.with_memory_space_constraint(o, pltpu.HBM) if jnp.issubdtype(o.dtype, jnp.floating) else o
                  for o in operands]
        return call(*pinned)

    return run


def _dot(a, b):
    return jnp.dot(a, b, preferred_element_type=f32)


def _dot_nt(a, b):
    return lax.dot_general(a, b, (((1,), (1,)), ((), ())), preferred_element_type=f32)


def _dot_tn(a, b):
    return lax.dot_general(a, b, (((0,), (0,)), ((), ())), preferred_element_type=f32)


def _rms(x, w):
    r = lax.rsqrt(jnp.mean(x * x, axis=-1, keepdims=True) + EPS)
    return x * r * w, r


def _rms_bwd(x, w, r, dh):
    xhat = x * r
    dw = jnp.sum(dh * xhat, axis=0, keepdims=True)
    dxh = dh * w
    dx = r * (dxh - xhat * jnp.mean(dxh * xhat, axis=-1, keepdims=True))
    return dx, dw


def _ffn_fwd(x, nw, blob, k0, ride=None):
    kg, ku, kd = k0, k0 + 1, k0 + 2
    nr = 0 if ride is None else len(ride)
    ni = S // TM

    def body(*refs):
        x_ref, nw_ref, wg_ref, wu_ref, wd_ref = refs[:5]
        ride_in = refs[5:5 + nr]
        o_ref = refs[5 + nr]
        ride_out = refs[6 + nr:6 + 2 * nr]
        h_scr, acc_scr = refs[6 + 2 * nr:8 + 2 * nr]
        sems = refs[8 + 2 * nr:]
        i = pl.program_id(0)
        j = pl.program_id(1)

        if nr:
            @pl.when(jnp.logical_and(i == 0, j == 0))
            def _():
                _gather_start(ride_in, ride_out, *sems)

        @pl.when(j == 0)
        def _():
            h, _ = _rms(x_ref[...], nw_ref[...])
            h_scr[...] = h.astype(bf16)
            acc_scr[...] = jnp.zeros_like(acc_scr)

        h = h_scr[...]
        g = _dot_nt(h, wg_ref[0])
        u = _dot_nt(h, wu_ref[0])
        a = (g * jax.nn.sigmoid(g) * u).astype(bf16)
        acc_scr[...] += _dot(a, wd_ref[0])

        @pl.when(j == NCH - 1)
        def _():
            o_ref[...] = x_ref[...] + 0.5 * acc_scr[...]

        if nr:
            @pl.when(jnp.logical_and(i == ni - 1, j == NCH - 1))
            def _():
                _gather_finish(ride_in, ride_out, *sems)

    wspec = lambda k: pl.BlockSpec((1, FC, D), lambda i, j: (j, k, 0))
    rides = [] if ride is None else list(ride)
    res = _pc(
        body, grid=(ni, NCH), name="ffn_fwd_ride" if nr else "ffn_fwd",
        in_specs=[pl.BlockSpec((TM, D), lambda i, j: (i, 0)),
                  pl.BlockSpec((1, D), lambda i, j: (0, 0)),
                  wspec(kg), wspec(ku), wspec(kd)] + [ANY] * nr,
        out_specs=[pl.BlockSpec((TM, D), lambda i, j: (i, 0))] + [ANY] * nr,
        out_shape=[SDS((S, D), f32)] + [SDS((NCH,) + r.shape, r.dtype) for r in rides],
        scratch_shapes=[pltpu.VMEM((TM, D), bf16), pltpu.VMEM((TM, D), f32)] + (_gather_sems(nr) if nr else []),
        compiler_params=_cp(40),
    )(x, nw, blob, blob, blob, *rides)
    return res if nr else res[0]


def _ffn_bwd(x, nw, blob, k0, dy, ride=None):
    nt = S // TM
    kg, ku, kd = k0, k0 + 1, k0 + 2
    nr = 0 if ride is None else len(ride)

    def body(*refs):
        x_ref, nw_ref, wg_ref, wu_ref, wd_ref, dy_ref = refs[:6]
        ride_in = refs[6:6 + nr]
        dwg_ref, dwu_ref, dwd_ref, dh_ref = refs[6 + nr:10 + nr]
        ride_out = refs[10 + nr:10 + 2 * nr]
        ag, au, ad = refs[10 + 2 * nr:13 + 2 * nr]
        sems = refs[13 + 2 * nr:]
        j = pl.program_id(0)
        i = pl.program_id(1)

        if nr:
            @pl.when(jnp.logical_and(j == 0, i == 0))
            def _():
                _scatter_start(ride_in, ride_out, *sems)

        @pl.when(i == 0)
        def _():
            ag[...] = jnp.zeros_like(ag)
            au[...] = jnp.zeros_like(au)
            ad[...] = jnp.zeros_like(ad)

        hf, _ = _rms(x_ref[...], nw_ref[...])
        h = hf.astype(bf16)
        g = _dot_nt(h, wg_ref[0])
        u = _dot_nt(h, wu_ref[0])
        sg = jax.nn.sigmoid(g)
        s = g * sg
        a = (s * u).astype(bf16)
        dyb = (0.5 * dy_ref[...]).astype(bf16)
        da = _dot_nt(dyb, wd_ref[0])
        ad[...] += _dot_tn(a, dyb)
        du = (da * s).astype(bf16)
        dg = (da * u * (sg * (1.0 + g * (1.0 - sg)))).astype(bf16)
        ag[...] += _dot_tn(dg, h)
        au[...] += _dot_tn(du, h)
        dh_ref[0] = (_dot(dg, wg_ref[0]) + _dot(du, wu_ref[0])).astype(bf16)

        @pl.when(i == nt - 1)
        def _():
            dwg_ref[0] = ag[...].astype(bf16)
            dwu_ref[0] = au[...].astype(bf16)
            dwd_ref[0] = ad[...].astype(bf16)

        if nr:
            @pl.when(jnp.logical_and(j == NCH - 1, i == nt - 1))
            def _():
                _scatter_finish(ride_in, ride_out, *sems)

    wspec = lambda k: pl.BlockSpec((1, FC, D), lambda j, i: (j, k, 0))
    gspec = pl.BlockSpec((1, FC, D), lambda j, i: (j, 0, 0))
    rides = [] if ride is None else list(ride)
    return _pc(
        body, grid=(NCH, nt), name="ffn_bwd_ride" if nr else "ffn_bwd",
        in_specs=[pl.BlockSpec((TM, D), lambda j, i: (i, 0)),
                  pl.BlockSpec((1, D), lambda j, i: (0, 0)),
                  wspec(kg), wspec(ku), wspec(kd),
                  pl.BlockSpec((TM, D), lambda j, i: (i, 0))] + [ANY] * nr,
        out_specs=[gspec, gspec, gspec, pl.BlockSpec((1, TM, D), lambda j, i: (j, i, 0))] + [ANY] * nr,
        out_shape=[SDS((NCH, FC, D), bf16)] * 3 + [SDS((NCH, S, D), bf16)] + [SDS(r.shape, r.dtype) for r in rides],
        scratch_shapes=[pltpu.VMEM((FC, D), f32)] * 3 + (_scatter_sems(nr) if nr else []),
        compiler_params=_cp(56),
    )(x, nw, blob, blob, blob, dy, *rides)


def _norm_bwd(x, nw, dres, dh4):
    nt = S // TM
    nparts = dh4.shape[0]

    def body(x_ref, nw_ref, dres_ref, dh_ref, dx_ref, dnw_ref):
        i = pl.program_id(0)
        dh = dh_ref[0].astype(f32)
        for p in range(1, nparts):
            dh = dh + dh_ref[p].astype(f32)
        xv = x_ref[...]
        _, r = _rms(xv, nw_ref[...])
        dx, dw = _rms_bwd(xv, nw_ref[...], r, dh)
        dx_ref[...] = dres_ref[...] + dx

        @pl.when(i == 0)
        def _():
            dnw_ref[...] = jnp.zeros_like(dnw_ref)

        dnw_ref[...] += dw

    return _pc(
        body, grid=(nt,), name="norm_bwd",
        in_specs=[pl.BlockSpec((TM, D), lambda i: (i, 0)),
                  pl.BlockSpec((1, D), lambda i: (0, 0)),
                  pl.BlockSpec((TM, D), lambda i: (i, 0)),
                  pl.BlockSpec((nparts, TM, D), lambda i: (0, i, 0))],
        out_specs=[pl.BlockSpec((TM, D), lambda i: (i, 0)), pl.BlockSpec((1, D), lambda i: (0, 0))],
        out_shape=[SDS((S, D), f32), SDS((1, D), f32)],
        compiler_params=_cp(40),
    )(x, nw, dres, dh4)


def _final(x, nw, target):
    nt = S // TM

    def body(x_ref, nw_ref, t_ref, dx_ref, dnw_ref, loss_ref):
        i = pl.program_id(0)
        xv = x_ref[...]
        y, r = _rms(xv, nw_ref[...])
        err = y - t_ref[...]
        part = 0.5 * jnp.sum(jnp.mean(err * err, axis=-1, keepdims=True), axis=0, keepdims=True)
        dx, dw = _rms_bwd(xv, nw_ref[...], r, err * (1.0 / D))
        dx_ref[...] = dx

        @pl.when(i == 0)
        def _():
            dnw_ref[...] = jnp.zeros_like(dnw_ref)
            loss_ref[...] = jnp.zeros_like(loss_ref)

        dnw_ref[...] += dw
        loss_ref[...] += jnp.broadcast_to(part, loss_ref.shape)

    return _pc(
        body, grid=(nt,), name="final_loss",
        in_specs=[pl.BlockSpec((TM, D), lambda i: (i, 0)),
                  pl.BlockSpec((1, D), lambda i: (0, 0)),
                  pl.BlockSpec((TM, D), lambda i: (i, 0))],
        out_specs=[pl.BlockSpec((TM, D), lambda i: (i, 0)), pl.BlockSpec((1, D), lambda i: (0, 0)),
                   pl.BlockSpec((1, 128), lambda i: (0, 0))],
        out_shape=[SDS((S, D), f32), SDS((1, D), f32), SDS((1, 128), f32)],
        compiler_params=_cp(40),
    )(x, nw, target)


def _rot_half(t):
    lane = lax.broadcasted_iota(jnp.int32, t.shape, 1)
    first = (lane % EH) < (EH // 2)
    return jnp.where(first, -pltpu.roll(t, ATT - EH // 2, 1), pltpu.roll(t, EH // 2, 1))


def _rope_tables(pos_ref, freq_ref):
    ang = pos_ref[...].astype(f32) * freq_ref[...]
    return jnp.cos(ang), jnp.sin(ang)


def _split_residues(val, scr, outs):
    rows, cols = val.shape
    for j in range(cols // 128):
        scr[j] = val[:, 128 * j:128 * (j + 1)]
    for ref, d in outs:
        for j in range(cols // 128):
            for r in range(d):
                ref.at[r][:, 128 * j:128 * (j + 1)] = scr.at[j][pl.ds(r, rows // d, stride=d), :]


def _join_residues(ref, d, scr):
    rows, cols = scr.shape[1], ref.shape[2]
    for j in range(cols // 128):
        for r in range(d):
            scr.at[j][pl.ds(r, rows // d, stride=d), :] = ref.at[r][:, 128 * j:128 * (j + 1)]
    return jnp.concatenate([scr[j] for j in range(cols // 128)], axis=1)


def _res_spec(d, tile, cols):
    return pl.BlockSpec((d, tile // d, cols), lambda i: (0, i, 0))


def _inproj_fwd(x, nw, w_aug, pos, freq):
    TI = 256

    def body(x_ref, nw_ref, w_hbm, pos_ref, freq_ref, att_ref, att4_ref, att16_ref, pu_ref, dq_ref, dz_ref, dba_ref,
             w_scr, r_scr):
        @pl.when(pl.program_id(0) == 0)
        def _():
            pltpu.sync_copy(w_hbm, w_scr)

        h, _ = _rms(x_ref[...], nw_ref[...])
        proj = _dot(h.astype(bf16), w_scr[...])
        cos, sin = _rope_tables(pos_ref, freq_ref)
        q = proj[:, 0:ATT]
        k = proj[:, ATT:2 * ATT]
        att = jnp.concatenate([q * cos + _rot_half(q) * sin, k * cos + _rot_half(k) * sin, proj[:, 2 * ATT:3 * ATT]], axis=1)
        att_ref[...] = att
        _split_residues(att, r_scr, [(att4_ref, 4), (att16_ref, 16)])
        pu_ref[...] = proj[:, 768:1024]
        dq_ref[...] = proj[:, 1024:2560]
        dz_ref[...] = proj[:, 2560:3072]
        dba_ref[...] = proj[:, 3072:3200]

    return _pc(
        body, grid=(S // TI,), name="inproj_fwd",
        in_specs=[pl.BlockSpec((TI, D), lambda i: (i, 0)),
                  pl.BlockSpec((1, D), lambda i: (0, 0)),
                  pl.BlockSpec(memory_space=pl.ANY),
                  pl.BlockSpec((TI, 1), lambda i: (i, 0)),
                  pl.BlockSpec((1, ATT), lambda i: (0, 0))],
        out_specs=[pl.BlockSpec((TI, 768), lambda i: (i, 0)), _res_spec(4, TI, 768), _res_spec(16, TI, 768),
                   pl.BlockSpec((TI, 256), lambda i: (i, 0)),
                   pl.BlockSpec((TI, 1536), lambda i: (i, 0)), pl.BlockSpec((TI, 512), lambda i: (i, 0)),
                   pl.BlockSpec((TI, 128), lambda i: (i, 0))],
        out_shape=[SDS((S, 768), f32), SDS((4, S // 4, 768), f32), SDS((16, S // 16, 768), f32), SDS((S, 256), f32),
                   SDS((S, 1536), f32), SDS((S, 512), f32), SDS((S, 128), f32)],
        scratch_shapes=[pltpu.VMEM((D, INP), bf16), pltpu.VMEM((6, TI, 128), f32)],
        compiler_params=_cp(48),
    )(x, nw, w_aug, pos, freq)


def _inproj_bwd(x, nw, w_aug, pos, freq, dres, datt, datt4, datt16, dpu, ddq, ddz, ddba):
    TI = 256
    nt = S // TI

    def body(x_ref, nw_ref, w_hbm, pos_ref, freq_ref, dres_ref, datt_ref, datt4_ref, datt16_ref, dpu_ref, ddq_ref, ddz_ref,
             ddba_ref, dx_ref, dnw_ref, dw_hbm, w_scr, acc, r_scr):
        i = pl.program_id(0)

        @pl.when(i == 0)
        def _():
            pltpu.sync_copy(w_hbm, w_scr)
            acc[...] = jnp.zeros_like(acc)
            dnw_ref[...] = jnp.zeros_like(dnw_ref)

        xv = x_ref[...]
        hf, r = _rms(xv, nw_ref[...])
        h = hf.astype(bf16)
        cos, sin = _rope_tables(pos_ref, freq_ref)
        datt = datt_ref[...] + _join_residues(datt4_ref, 4, r_scr)
        datt = datt + _join_residues(datt16_ref, 16, r_scr)
        dq = datt[:, 0:ATT]
        dk = datt[:, ATT:2 * ATT]
        dq = dq * cos - _rot_half(dq) * sin
        dk = dk * cos - _rot_half(dk) * sin
        dproj = jnp.concatenate([dq, dk, datt[:, 2 * ATT:3 * ATT], dpu_ref[...], ddq_ref[...], ddz_ref[...], ddba_ref[...]],
                                axis=1).astype(bf16)
        acc[...] += _dot_tn(h, dproj)
        dh = _dot_nt(dproj, w_scr[...])
        dx, dw = _rms_bwd(xv, nw_ref[...], r, dh)
        dx_ref[...] = dres_ref[...] + dx
        dnw_ref[...] += dw

        @pl.when(i == nt - 1)
        def _():
            pltpu.sync_copy(acc, dw_hbm)

    return _pc(
        body, grid=(nt,), name="inproj_bwd",
        in_specs=[pl.BlockSpec((TI, D), lambda i: (i, 0)),
                  pl.BlockSpec((1, D), lambda i: (0, 0)),
                  pl.BlockSpec(memory_space=pl.ANY),
                  pl.BlockSpec((TI, 1), lambda i: (i, 0)),
                  pl.BlockSpec((1, ATT), lambda i: (0, 0)),
                  pl.BlockSpec((TI, D), lambda i: (i, 0)),
                  pl.BlockSpec((TI, 768), lambda i: (i, 0)), _res_spec(4, TI, 768), _res_spec(16, TI, 768),
                  pl.BlockSpec((TI, 256), lambda i: (i, 0)),
                  pl.BlockSpec((TI, 1536), lambda i: (i, 0)),
                  pl.BlockSpec((TI, 512), lambda i: (i, 0)),
                  pl.BlockSpec((TI, 128), lambda i: (i, 0))],
        out_specs=[pl.BlockSpec((TI, D), lambda i: (i, 0)), pl.BlockSpec((1, D), lambda i: (0, 0)),
                   pl.BlockSpec(memory_space=pl.ANY)],
        out_shape=[SDS((S, D), f32), SDS((1, D), f32), SDS((D, INP), f32)],
        scratch_shapes=[pltpu.VMEM((D, INP), bf16), pltpu.VMEM((D, INP), f32), pltpu.VMEM((6, TI, 128), f32)],
        compiler_params=_cp(56),
    )(x, nw, w_aug, pos, freq, dres, datt, datt4, datt16, dpu, ddq, ddz, ddba)


def _outproj_fwd(x, ya, yb, yc, blob_b, kw):
    def body(x_ref, ya_ref, yb_ref, yc_ref, w_ref, o_ref):
        ycat = jnp.concatenate([ya_ref[...], yb_ref[...], yc_ref[...]], axis=1).astype(bf16)
        o_ref[...] = x_ref[...] + _dot(ycat, w_ref[:, 0:256, :].reshape(D, D))

    return _pc(
        body, grid=(S // TM,), name="outproj_fwd",
        in_specs=[pl.BlockSpec((TM, D), lambda i: (i, 0)),
                  pl.BlockSpec((TM, 256), lambda i: (i, 0)),
                  pl.BlockSpec((TM, 256), lambda i: (i, 0)),
                  pl.BlockSpec((TM, 512), lambda i: (i, 0)),
                  pl.BlockSpec((NCH, FC, D), lambda i: (0, kw, 0))],
        out_specs=pl.BlockSpec((TM, D), lambda i: (i, 0)),
        out_shape=SDS((S, D), f32),
        compiler_params=_cp(40),
    )(x, ya, yb, yc, blob_b)


def _outproj_bwd(dy, ya, yb, yc, blob_b, kw):
    nt = S // TM

    def body(dy_ref, ya_ref, yb_ref, yc_ref, w_ref, dya_ref, dyb_ref, dyc_ref, dw_ref):
        i = pl.program_id(0)

        @pl.when(i == 0)
        def _():
            dw_ref[...] = jnp.zeros_like(dw_ref)

        dyv = dy_ref[...].astype(bf16)
        ycat = jnp.concatenate([ya_ref[...], yb_ref[...], yc_ref[...]], axis=1).astype(bf16)
        dw_ref[...] += _dot_tn(ycat, dyv)
        dcat = _dot_nt(dyv, w_ref[:, 0:256, :].reshape(D, D))
        dya_ref[...] = dcat[:, 0:256]
        dyb_ref[...] = dcat[:, 256:512]
        dyc_ref[...] = dcat[:, 512:1024]

    return _pc(
        body, grid=(nt,), name="outproj_bwd",
        in_specs=[pl.BlockSpec((TM, D), lambda i: (i, 0)),
                  pl.BlockSpec((TM, 256), lambda i: (i, 0)),
                  pl.BlockSpec((TM, 256), lambda i: (i, 0)),
                  pl.BlockSpec((TM, 512), lambda i: (i, 0)),
                  pl.BlockSpec((NCH, FC, D), lambda i: (0, kw, 0))],
        out_specs=[pl.BlockSpec((TM, 256), lambda i: (i, 0)), pl.BlockSpec((TM, 256), lambda i: (i, 0)),
                   pl.BlockSpec((TM, 512), lambda i: (i, 0)), pl.BlockSpec((D, D), lambda i: (0, 0))],
        out_shape=[SDS((S, 256), f32), SDS((S, 256), f32), SDS((S, 512), f32), SDS((D, D), f32)],
        compiler_params=_cp(40),
    )(dy, ya, yb, yc, blob_b)


QT = NBLK
NB = S // QT


def _attn_block(q, kp, kc, vp, vc, first):
    kk = jnp.concatenate([kp, kc], axis=0).astype(bf16)
    vv = jnp.concatenate([vp, vc], axis=0).astype(bf16)
    qi = lax.broadcasted_iota(jnp.int32, (4 * QT, NBLK + QT), 0) % QT
    ki = lax.broadcasted_iota(jnp.int32, (4 * QT, NBLK + QT), 1)
    dist = NBLK + qi - ki
    valid = (dist >= 0) & (dist <= NBLK) & (jnp.logical_not(first) | (ki >= NBLK))
    head = lax.broadcasted_iota(jnp.int32, (1, ATT), 1) // EH
    masks = [(head == h).astype(f32) for h in range(4)]
    qs = jnp.concatenate([q * (mh * (1.0 / math.sqrt(EH))) for mh in masks], axis=0).astype(bf16)
    s = _dot_nt(qs, kk)
    s = jnp.where(valid, s, NEG)
    m = lax.stop_gradient(jnp.max(s, axis=-1, keepdims=True))
    p = jnp.exp(s - m)
    den = jnp.sum(p, axis=-1, keepdims=True)
    po = _dot((p * (1.0 / den)).astype(bf16), vv)
    lse = m + jnp.log(den)
    o = jnp.zeros((QT, ATT), f32)
    l = jnp.zeros((QT, ATT), f32)
    for h, mh in enumerate(masks):
        o = o + po[QT * h:QT * (h + 1)] * mh
        l = l + lse[QT * h:QT * (h + 1)] * mh
    return o, l


def _attn_specs(tile):
    own = lambda col: pl.BlockSpec((QT, ATT), lambda s: (tile(s), col))
    prev = lambda col: pl.BlockSpec((NBLK, ATT), lambda s: (jnp.maximum((QT // NBLK) * tile(s) - 1, 0), col))
    return [own(0), prev(1), own(1), prev(2), own(2)]


def _attn_fwd(qkv, per_seq):
    def body(q_ref, kp_ref, kc_ref, vp_ref, vc_ref, o_ref, l_ref):
        first = pl.program_id(0) % per_seq == 0
        o, l = _attn_block(q_ref[...], kp_ref[...], kc_ref[...], vp_ref[...], vc_ref[...], first)
        o_ref[...] = o
        l_ref[...] = l

    blk = pl.BlockSpec((QT, ATT), lambda t: (t, 0))
    return _pc(
        body, grid=(NB,), name="attn_fwd", in_specs=_attn_specs(lambda t: t), out_specs=[blk, blk],
        out_shape=[SDS((S, ATT), f32), SDS((S, ATT), f32)], compiler_params=_cp(32),
    )(qkv, qkv, qkv, qkv, qkv)


def _attn_bwd(qkv, do, dl, per_seq):
    def body(q_ref, kp_ref, kc_ref, vp_ref, vc_ref, do_ref, dl_ref, o_ref, k_carry, v_carry):
        step = pl.program_id(0)

        @pl.when(step == 0)
        def _():
            k_carry[...] = jnp.zeros_like(k_carry)
            v_carry[...] = jnp.zeros_like(v_carry)

        t = NB - 1 - step
        first = t % per_seq == 0
        last = t % per_seq == per_seq - 1
        fn = lambda q, kp, kc, vp, vc: _attn_block(q, kp, kc, vp, vc, first)
        _, vjp = jax.vjp(fn, q_ref[...], kp_ref[...], kc_ref[...], vp_ref[...], vc_ref[...])
        dq, dkp, dkc, dvp, dvc = vjp((do_ref[...], dl_ref[...]))
        o_ref[:, 0:ATT] = dq
        o_ref[:, ATT:2 * ATT] = dkc
        o_ref[:, 2 * ATT:3 * ATT] = dvc
        o_ref[QT - NBLK:QT, ATT:2 * ATT] += jnp.where(last, 0.0, k_carry[...])
        o_ref[QT - NBLK:QT, 2 * ATT:3 * ATT] += jnp.where(last, 0.0, v_carry[...])
        k_carry[...] = dkp
        v_carry[...] = dvp

    rev = lambda s: NB - 1 - s
    blk = pl.BlockSpec((QT, ATT), lambda s: (rev(s), 0))
    return _pc(
        body, grid=(NB,), name="attn_bwd", in_specs=_attn_specs(rev) + [blk, blk],
        out_specs=pl.BlockSpec((QT, 768), lambda s: (rev(s), 0)),
        out_shape=SDS((S, 768), f32), scratch_shapes=[pltpu.VMEM((NBLK, ATT), f32)] * 2, compiler_params=_cp(40),
    )(qkv, qkv, qkv, qkv, qkv, do, dl)


def _merge_weights(l0, l1, l2):
    m = jnp.maximum(jnp.maximum(l0, l1), l2)
    e0, e1, e2 = jnp.exp(l0 - m), jnp.exp(l1 - m), jnp.exp(l2 - m)
    tot = e0 + e1 + e2
    return e0 / tot, e1 / tot, e2 / tot


def _merge_specs():
    nat = pl.BlockSpec((TM, ATT), lambda i: (i, 0))
    return nat, _res_spec(4, TM, ATT), _res_spec(16, TM, ATT)


def _merge_fwd(o1, l1, o4, l4, o16, l16):
    def body(o1_ref, l1_ref, o4_ref, l4_ref, o16_ref, l16_ref, y_ref, scr):
        o4v, l4v = _join_residues(o4_ref, 4, scr), _join_residues(l4_ref, 4, scr)
        o16v, l16v = _join_residues(o16_ref, 16, scr), _join_residues(l16_ref, 16, scr)
        w0, w1, w2 = _merge_weights(l1_ref[...], l4v, l16v)
        y_ref[...] = w0 * o1_ref[...] + w1 * o4v + w2 * o16v

    nat, r4, r16 = _merge_specs()
    return _pc(body, grid=(S // TM,), name="merge_fwd", in_specs=[nat, nat, r4, r4, r16, r16],
                          out_specs=nat, out_shape=SDS((S, ATT), f32), scratch_shapes=[pltpu.VMEM((2, TM, 128), f32)],
                          compiler_params=_cp(32))(o1, l1, o4, l4, o16, l16)


def _merge_bwd(o1, l1, o4, l4, o16, l16, dy):
    def body(o1_ref, l1_ref, o4_ref, l4_ref, o16_ref, l16_ref, dy_ref, do1_ref, dl1_ref, do4_ref, dl4_ref, do16_ref, dl16_ref, scr):
        o4v, l4v = _join_residues(o4_ref, 4, scr), _join_residues(l4_ref, 4, scr)
        o16v, l16v = _join_residues(o16_ref, 16, scr), _join_residues(l16_ref, 16, scr)
        o1v = o1_ref[...]
        w0, w1, w2 = _merge_weights(l1_ref[...], l4v, l16v)
        y = w0 * o1v + w1 * o4v + w2 * o16v
        dyv = dy_ref[...]
        do1_ref[...] = w0 * dyv
        dl1_ref[...] = w0 * (o1v - y) * dyv
        _split_residues(w1 * dyv, scr, [(do4_ref, 4)])
        _split_residues(w1 * (o4v - y) * dyv, scr, [(dl4_ref, 4)])
        _split_residues(w2 * dyv, scr, [(do16_ref, 16)])
        _split_residues(w2 * (o16v - y) * dyv, scr, [(dl16_ref, 16)])

    nat, r4, r16 = _merge_specs()
    return _pc(body, grid=(S // TM,), name="merge_bwd", in_specs=[nat, nat, r4, r4, r16, r16, nat],
                          out_specs=[nat, nat, r4, r4, r16, r16],
                          out_shape=[SDS((S, ATT), f32)] * 2 + [SDS((4, S // 4, ATT), f32)] * 2 + [SDS((16, S // 16, ATT), f32)] * 2,
                          scratch_shapes=[pltpu.VMEM((2, TM, 128), f32)], compiler_params=_cp(32))(o1, l1, o4, l4, o16, l16, dy)


HALO = 16


def _pool_consts(i, rows):
    grp = lax.broadcasted_iota(jnp.int32, (rows, 256), 1) // 64
    t = i * TM + lax.broadcasted_iota(jnp.int32, (rows, 256), 0)
    win = jnp.where(grp == 0, 2, jnp.where(grp == 1, 4, jnp.where(grp == 2, 8, 16)))
    cnt = jnp.minimum(t + 1, win).astype(f32)
    return grp, cnt


def _pool_select(grp, s2, s4, s8, s16):
    return jnp.where(grp == 0, s2, jnp.where(grp == 1, s4, jnp.where(grp == 2, s8, s16)))


def _pooled(i, cur, halo):
    xx = jnp.concatenate([halo, cur], axis=0)
    s2 = xx + pltpu.roll(xx, 1, 0)
    s4 = s2 + pltpu.roll(s2, 2, 0)
    s8 = s4 + pltpu.roll(s4, 4, 0)
    s16 = s8 + pltpu.roll(s8, 8, 0)
    grp, cnt = _pool_consts(i, TM)
    tot = _pool_select(grp, s2[HALO:], s4[HALO:], s8[HALO:], s16[HALO:])
    return tot / cnt - cur


def _pool_fwd(u, wp, scale):
    def body(u_ref, halo_ref, wp_ref, sc_ref, y_ref):
        i = pl.program_id(0)
        halo = halo_ref[...] * (i > 0).astype(f32)
        pooled = _pooled(i, u_ref[...], halo)
        y_ref[...] = _dot(pooled.astype(bf16), wp_ref[...]) * sc_ref[...]

    return _pc(
        body, grid=(S // TM,), name="pool_fwd",
        in_specs=[pl.BlockSpec((TM, 256), lambda i: (i, 0)),
                  pl.BlockSpec((HALO, 256), lambda i: (jnp.maximum(i * (TM // HALO) - 1, 0), 0)),
                  pl.BlockSpec((256, 256), lambda i: (0, 0)),
                  pl.BlockSpec((1, 256), lambda i: (0, 0))],
        out_specs=pl.BlockSpec((TM, 256), lambda i: (i, 0)), out_shape=SDS((S, 256), f32), compiler_params=_cp(32),
    )(u, u, wp, scale)


def _pool_bwd(u, wp, scale, dy):
    nt = S // TM

    def body(u_ref, halo_ref, wp_ref, sc_ref, dy_ref, dyn_ref, du_ref, dwp_ref, dsc_ref):
        i = pl.program_id(0)

        @pl.when(i == 0)
        def _():
            dwp_ref[...] = jnp.zeros_like(dwp_ref)
            dsc_ref[...] = jnp.zeros_like(dsc_ref)

        halo = halo_ref[...] * (i > 0).astype(f32)
        pooled = _pooled(i, u_ref[...], halo).astype(bf16)
        dyv = dy_ref[...]
        dsc_ref[...] += jnp.sum(dyv * _dot(pooled, wp_ref[...]), axis=0, keepdims=True)
        dys = (dyv * sc_ref[...]).astype(bf16)
        dwp_ref[...] += _dot_tn(pooled, dys)
        dpool = _dot_nt(dys, wp_ref[...])
        grp, cnt = _pool_consts(i, TM)
        dyn = ((dyn_ref[...] * (i < nt - 1).astype(f32)) * sc_ref[...]).astype(bf16)
        _, cntn = _pool_consts(i + 1, HALO)
        zn = _dot_nt(dyn, wp_ref[...]) / cntn
        zz = jnp.concatenate([dpool / cnt, zn], axis=0)
        n = TM + HALO
        a2 = zz + pltpu.roll(zz, n - 1, 0)
        a4 = a2 + pltpu.roll(a2, n - 2, 0)
        a8 = a4 + pltpu.roll(a4, n - 4, 0)
        a16 = a8 + pltpu.roll(a8, n - 8, 0)
        du_ref[...] = _pool_select(grp, a2[:TM], a4[:TM], a8[:TM], a16[:TM]) - dpool

    return _pc(
        body, grid=(nt,), name="pool_bwd",
        in_specs=[pl.BlockSpec((TM, 256), lambda i: (i, 0)),
                  pl.BlockSpec((HALO, 256), lambda i: (jnp.maximum(i * (TM // HALO) - 1, 0), 0)),
                  pl.BlockSpec((256, 256), lambda i: (0, 0)),
                  pl.BlockSpec((1, 256), lambda i: (0, 0)),
                  pl.BlockSpec((TM, 256), lambda i: (i, 0)),
                  pl.BlockSpec((HALO, 256), lambda i: (jnp.minimum((i + 1) * (TM // HALO), S // HALO - 1), 0))],
        out_specs=[pl.BlockSpec((TM, 256), lambda i: (i, 0)), pl.BlockSpec((256, 256), lambda i: (0, 0)),
                   pl.BlockSpec((1, 256), lambda i: (0, 0))],
        out_shape=[SDS((S, 256), f32), SDS((256, 256), f32), SDS((1, 256), f32)], compiler_params=_cp(32),
    )(u, u, wp, scale, dy, dy)


CW = 3 * DNW
CHALO = 8
TC = 256


def _conv_fwd(u, w):
    def body(u_ref, halo_ref, w_ref, c_ref):
        i = pl.program_id(0)
        xx = jnp.concatenate([halo_ref[...] * (i > 0).astype(f32), u_ref[...]], axis=0)
        c = (w_ref[3:4, :] * xx + w_ref[2:3, :] * pltpu.roll(xx, 1, 0) + w_ref[1:2, :] * pltpu.roll(xx, 2, 0)
             + w_ref[0:1, :] * pltpu.roll(xx, 3, 0))
        c_ref[...] = c[CHALO:]

    return _pc(
        body, grid=(S // TC,), name="conv_fwd",
        in_specs=[pl.BlockSpec((TC, CW), lambda i: (i, 0)),
                  pl.BlockSpec((CHALO, CW), lambda i: (jnp.maximum(i * (TC // CHALO) - 1, 0), 0)),
                  pl.BlockSpec((8, CW), lambda i: (0, 0))],
        out_specs=pl.BlockSpec((TC, CW), lambda i: (i, 0)), out_shape=SDS((S, CW), f32), compiler_params=_cp(32),
    )(u, u, w)


def _conv_bwd(u, w, dc):
    nt = S // TC

    def body(u_ref, halo_ref, w_ref, dc_ref, dcn_ref, du_ref, dw_ref):
        i = pl.program_id(0)

        @pl.when(i == 0)
        def _():
            dw_ref[...] = jnp.zeros_like(dw_ref)

        dcv = dc_ref[...]
        zz = jnp.concatenate([dcv, dcn_ref[...] * (i < nt - 1).astype(f32)], axis=0)
        n = TC + CHALO
        du = (w_ref[3:4, :] * zz + w_ref[2:3, :] * pltpu.roll(zz, n - 1, 0) + w_ref[1:2, :] * pltpu.roll(zz, n - 2, 0)
              + w_ref[0:1, :] * pltpu.roll(zz, n - 3, 0))
        du_ref[...] = du[:TC]
        xx = jnp.concatenate([halo_ref[...] * (i > 0).astype(f32), u_ref[...]], axis=0)
        for j in range(4):
            shifted = xx if j == 3 else pltpu.roll(xx, 3 - j, 0)
            dw_ref[j:j + 1, :] += jnp.sum(dcv * shifted[CHALO:], axis=0, keepdims=True)

    return _pc(
        body, grid=(nt,), name="conv_bwd",
        in_specs=[pl.BlockSpec((TC, CW), lambda i: (i, 0)),
                  pl.BlockSpec((CHALO, CW), lambda i: (jnp.maximum(i * (TC // CHALO) - 1, 0), 0)),
                  pl.BlockSpec((8, CW), lambda i: (0, 0)),
                  pl.BlockSpec((TC, CW), lambda i: (i, 0)),
                  pl.BlockSpec((CHALO, CW), lambda i: (jnp.minimum((i + 1) * (TC // CHALO), S // CHALO - 1), 0))],
        out_specs=[pl.BlockSpec((TC, CW), lambda i: (i, 0)), pl.BlockSpec((8, CW), lambda i: (0, 0))],
        out_shape=[SDS((S, CW), f32), SDS((8, CW), f32)], compiler_params=_cp(32),
    )(u, u, w, dc, dc)


TL = 512
NCL = TL // CH


def _bdot(a, b):
    return jnp.einsum('nik,nkj->nij', a.astype(bf16), b.astype(bf16), preferred_element_type=f32)


def _bdot_nt(a, b):
    return jnp.einsum('nik,njk->nij', a.astype(bf16), b.astype(bf16), preferred_element_type=f32)


def _bdot_tn(a, b):
    return jnp.einsum('nki,nkj->nij', a.astype(bf16), b.astype(bf16), preferred_element_type=f32)


@jax.custom_vjp
def _inv_unit_lower(a):
    ii = lax.broadcasted_iota(jnp.int32, (1, CH, CH), 1)
    jj = lax.broadcasted_iota(jnp.int32, (1, CH, CH), 2)
    t = (ii == jj).astype(f32) - a
    p = a
    for _ in range(5):
        p = _bdot(p, p)
        t = t + _bdot(t, p)
    return t


def _inv_unit_lower_fwd(a):
    t = _inv_unit_lower(a)
    return t, t


def _inv_unit_lower_bwd(t, dt):
    return (-_bdot_tn(t, _bdot_nt(dt, t)),)


_inv_unit_lower.defvjp(_inv_unit_lower_fwd, _inv_unit_lower_bwd)


def _dn_local(c, dba, a_row, b_row):
    act = c * jax.nn.sigmoid(c)
    lane = lax.broadcasted_iota(jnp.int32, (TL, 128), 1)
    beta_all = jax.nn.sigmoid(dba)
    xs = dba + b_row
    softplus = jnp.maximum(xs, 0.0) + jnp.log(1.0 + jnp.exp(-jnp.abs(xs)))
    g_all = -jnp.exp(a_row) * softplus
    ii = lax.broadcasted_iota(jnp.int32, (1, CH, CH), 1)
    jj = lax.broadcasted_iota(jnp.int32, (1, CH, CH), 2)
    lower = jj <= ii
    strict = jj < ii
    eye = (ii == jj).astype(f32)
    us, ws, qgs, kds, intras = [], [], [], [], []
    aux = jnp.zeros((TL, 128), f32)
    for h in range(4):
        q = act[:, DH * h:DH * (h + 1)]
        k = act[:, DNW + DH * h:DNW + DH * (h + 1)]
        v = act[:, 2 * DNW + DH * h:2 * DNW + DH * (h + 1)]
        q = q * lax.rsqrt(jnp.sum(q * q, axis=-1, keepdims=True) + EPS) * (DH ** -0.5)
        k = k * lax.rsqrt(jnp.sum(k * k, axis=-1, keepdims=True) + EPS)
        beta = jnp.sum(jnp.where(lane == h, beta_all, 0.0), axis=1, keepdims=True)
        g = jnp.sum(jnp.where(lane == 4 + h, g_all, 0.0), axis=1, keepdims=True)
        q3, k3, v3 = q.reshape(NCL, CH, DH), k.reshape(NCL, CH, DH), v.reshape(NCL, CH, DH)
        beta3, g3 = beta.reshape(NCL, CH, 1), g.reshape(NCL, CH, 1)
        g_row = jnp.sum(eye * g3, axis=1, keepdims=True)
        gc_col = jnp.sum(jnp.where(lower, g_row, 0.0), axis=2, keepdims=True)
        gc_row = jnp.sum(jnp.where(ii <= jj, g3, 0.0), axis=1, keepdims=True)
        diff = gc_col - gc_row
        decay = jnp.where(lower, jnp.exp(jnp.where(lower, diff, 0.0)), 0.0)
        kb = k3 * beta3
        vb = v3 * beta3
        a = jnp.where(strict, _bdot_nt(kb, k3) * decay, 0.0)
        t = _inv_unit_lower(a)
        u3 = _bdot(t, vb)
        w3 = _bdot(t, kb * jnp.exp(gc_col))
        intra = jnp.where(lower, _bdot_nt(q3, k3) * decay, 0.0)
        g_last = jnp.sum(g3, axis=1, keepdims=True)
        us.append(u3.reshape(TL, DH))
        ws.append(w3.reshape(TL, DH))
        qgs.append((q3 * jnp.exp(gc_col)).reshape(TL, DH))
        kds.append((k3 * jnp.exp(g_last - gc_col)).reshape(TL, DH))
        intras.append(intra.reshape(TL, CH))
        e_last = jnp.broadcast_to(jnp.exp(g_last), (NCL, CH, 1)).reshape(TL, 1)
        aux = aux + jnp.where(lane == h, e_last, 0.0)
    cat = lambda xs: jnp.concatenate(xs, axis=1)
    return cat(us), cat(ws), cat(qgs), cat(kds), jnp.stack(intras, axis=0), aux


def _dn_local_fwd(c, dba, par):
    def body(c_ref, dba_ref, par_ref, u_ref, w_ref, qg_ref, kd_ref, in_ref, aux_ref):
        u, w, qg, kd, intra, aux = _dn_local(c_ref[...], dba_ref[...], par_ref[0:1, :], par_ref[1:2, :])
        u_ref[...] = u
        w_ref[...] = w
        qg_ref[...] = qg
        kd_ref[...] = kd
        in_ref[...] = intra
        aux_ref[...] = aux

    wide = pl.BlockSpec((TL, DNW), lambda i: (i, 0))
    return _pc(
        body, grid=(S // TL,), name="dn_local_fwd",
        in_specs=[pl.BlockSpec((TL, CW), lambda i: (i, 0)), pl.BlockSpec((TL, 128), lambda i: (i, 0)),
                  pl.BlockSpec((8, 128), lambda i: (0, 0))],
        out_specs=[wide, wide, wide, wide, pl.BlockSpec((4, TL, CH), lambda i: (0, i, 0)),
                   pl.BlockSpec((TL, 128), lambda i: (i, 0))],
        out_shape=[SDS((S, DNW), f32)] * 4 + [SDS((4, S, CH), f32), SDS((S, 128), f32)], compiler_params=_cp(48),
    )(c, dba, par)


def _dn_local_bwd(c, dba, par, du, dw, dqg, dkd, dintra, daux):
    def body(c_ref, dba_ref, par_ref, du_ref, dw_ref, dqg_ref, dkd_ref, din_ref, daux_ref, dc_ref, ddba_ref, dpar_ref):
        @pl.when(pl.program_id(0) == 0)
        def _():
            dpar_ref[...] = jnp.zeros_like(dpar_ref)

        _, vjp = jax.vjp(_dn_local, c_ref[...], dba_ref[...], par_ref[0:1, :], par_ref[1:2, :])
        dc, ddba, da_row, db_row = vjp((du_ref[...], dw_ref[...], dqg_ref[...], dkd_ref[...], din_ref[...], daux_ref[...]))
        dc_ref[...] = dc
        ddba_ref[...] = ddba
        dpar_ref[0:1, :] += da_row
        dpar_ref[1:2, :] += db_row

    wide = pl.BlockSpec((TL, DNW), lambda i: (i, 0))
    return _pc(
        body, grid=(S // TL,), name="dn_local_bwd",
        in_specs=[pl.BlockSpec((TL, CW), lambda i: (i, 0)), pl.BlockSpec((TL, 128), lambda i: (i, 0)),
                  pl.BlockSpec((8, 128), lambda i: (0, 0)), wide, wide, wide, wide,
                  pl.BlockSpec((4, TL, CH), lambda i: (0, i, 0)), pl.BlockSpec((TL, 128), lambda i: (i, 0))],
        out_specs=[pl.BlockSpec((TL, CW), lambda i: (i, 0)), pl.BlockSpec((TL, 128), lambda i: (i, 0)),
                   pl.BlockSpec((8, 128), lambda i: (0, 0))],
        out_shape=[SDS((S, CW), f32), SDS((S, 128), f32), SDS((8, 128), f32)], compiler_params=_cp(56),
    )(c, dba, par, du, dw, dqg, dkd, dintra, daux)


def _dn_step(state, u, w, qg, kd, intra, aux):
    lane = lax.broadcasted_iota(jnp.int32, (CH, 128), 1)
    row = lax.broadcasted_iota(jnp.int32, (CH, 128), 0)
    outs, states = [], []
    for h in range(4):
        sl = slice(DH * h, DH * (h + 1))
        st = state[h]
        e = jnp.sum(jnp.sum(jnp.where((lane == h) & (row == 0), aux, 0.0), axis=1, keepdims=True), axis=0, keepdims=True)
        v_new = u[:, sl] - _dot(w[:, sl].astype(bf16), st.astype(bf16))
        vb = v_new.astype(bf16)
        outs.append(_dot(qg[:, sl].astype(bf16), st.astype(bf16)) + _dot(intra[h].astype(bf16), vb))
        states.append(st * e + _dot_tn(kd[:, sl].astype(bf16), vb))
    return jnp.concatenate(outs, axis=1), jnp.stack(states, axis=0)


CPS = 4
NSTEP = NCHUNK // CPS


def _dn_rec_specs(index):
    wide = pl.BlockSpec((CPS * CH, DNW), lambda n: (index(n), 0))
    inb = pl.BlockSpec((4, CPS * CH, CH), lambda n: (0, index(n), 0))
    auxb = pl.BlockSpec((CPS * CH, 128), lambda n: (index(n), 0))
    stb = pl.BlockSpec((CPS, 4, DH, DH), lambda n: (index(n), 0, 0, 0))
    return wide, inb, auxb, stb


def _dn_rec_fwd(u, w, qg, kd, intra, aux):
    def body(u_ref, w_ref, qg_ref, kd_ref, in_ref, aux_ref, o_ref, st_ref, st_scr):
        @pl.when(pl.program_id(0) == 0)
        def _():
            st_scr[...] = jnp.zeros_like(st_scr)

        st = st_scr[...]
        for k in range(CPS):
            rows = slice(CH * k, CH * (k + 1))
            st_ref[k] = st
            o, st = _dn_step(st, u_ref[rows, :], w_ref[rows, :], qg_ref[rows, :], kd_ref[rows, :], in_ref[:, rows, :],
                             aux_ref[rows, :])
            o_ref[rows, :] = o
        st_scr[...] = st

    wide, inb, auxb, stb = _dn_rec_specs(lambda n: n)
    return _pc(
        body, grid=(NSTEP,), name="dn_rec_fwd", in_specs=[wide, wide, wide, wide, inb, auxb], out_specs=[wide, stb],
        out_shape=[SDS((S, DNW), f32), SDS((NCHUNK, 4, DH, DH), f32)],
        scratch_shapes=[pltpu.VMEM((4, DH, DH), f32)], compiler_params=_cp(32),
    )(u, w, qg, kd, intra, aux)


def _dn_rec_bwd(u, w, qg, kd, intra, aux, states, do):
    def body(u_ref, w_ref, qg_ref, kd_ref, in_ref, aux_ref, st_ref, do_ref,
             du_ref, dw_ref, dqg_ref, dkd_ref, din_ref, daux_ref, ds_scr):
        @pl.when(pl.program_id(0) == 0)
        def _():
            ds_scr[...] = jnp.zeros_like(ds_scr)

        ds = ds_scr[...]
        for k in reversed(range(CPS)):
            rows = slice(CH * k, CH * (k + 1))
            _, vjp = jax.vjp(_dn_step, st_ref[k], u_ref[rows, :], w_ref[rows, :], qg_ref[rows, :], kd_ref[rows, :],
                             in_ref[:, rows, :], aux_ref[rows, :])
            ds, du, dw, dqg, dkd, din, daux = vjp((do_ref[rows, :], ds))
            du_ref[rows, :] = du
            dw_ref[rows, :] = dw
            dqg_ref[rows, :] = dqg
            dkd_ref[rows, :] = dkd
            din_ref[:, rows, :] = din
            daux_ref[rows, :] = daux
        ds_scr[...] = ds

    wide, inb, auxb, stb = _dn_rec_specs(lambda n: NSTEP - 1 - n)
    return _pc(
        body, grid=(NSTEP,), name="dn_rec_bwd", in_specs=[wide, wide, wide, wide, inb, auxb, stb, wide],
        out_specs=[wide, wide, wide, wide, inb, auxb],
        out_shape=[SDS((S, DNW), f32)] * 4 + [SDS((4, S, CH), f32), SDS((S, 128), f32)],
        scratch_shapes=[pltpu.VMEM((4, DH, DH), f32)], compiler_params=_cp(40),
    )(u, w, qg, kd, intra, aux, states, do)


def _dn_post(o, z, nw):
    parts = []
    for h in range(4):
        sl = slice(DH * h, DH * (h + 1))
        oh = o[:, sl]
        y = oh * lax.rsqrt(jnp.mean(oh * oh, axis=-1, keepdims=True) + EPS) * nw
        zh = z[:, sl]
        parts.append(y * (zh * jax.nn.sigmoid(zh)))
    return jnp.concatenate(parts, axis=1)


def _dn_post_fwd(o, z, nw):
    def body(o_ref, z_ref, nw_ref, y_ref):
        y_ref[...] = _dn_post(o_ref[...], z_ref[...], nw_ref[...])

    wide = pl.BlockSpec((TM, DNW), lambda i: (i, 0))
    return _pc(body, grid=(S // TM,), name="dn_post_fwd",
                          in_specs=[wide, wide, pl.BlockSpec((1, 128), lambda i: (0, 0))], out_specs=wide,
                          out_shape=SDS((S, DNW), f32), compiler_params=_cp(32))(o, z, nw)


def _dn_post_bwd(o, z, nw, dy):
    def body(o_ref, z_ref, nw_ref, dy_ref, do_ref, dz_ref, dnw_ref):
        @pl.when(pl.program_id(0) == 0)
        def _():
            dnw_ref[...] = jnp.zeros_like(dnw_ref)

        _, vjp = jax.vjp(_dn_post, o_ref[...], z_ref[...], nw_ref[...])
        do, dz, dnw = vjp(dy_ref[...])
        do_ref[...] = do
        dz_ref[...] = dz
        dnw_ref[...] += dnw

    wide = pl.BlockSpec((TM, DNW), lambda i: (i, 0))
    one = pl.BlockSpec((1, 128), lambda i: (0, 0))
    return _pc(body, grid=(S // TM,), name="dn_post_bwd", in_specs=[wide, wide, one, wide],
                          out_specs=[wide, wide, one], out_shape=[SDS((S, DNW), f32), SDS((S, DNW), f32), SDS((1, 128), f32)],
                          compiler_params=_cp(32))(o, z, nw, dy)


def _row_tile(rows, width, itemsize=4, target=2 * 1024 * 1024):
    best = None
    for t in range(16, rows + 1, 16):
        if rows % t == 0 and t * width * itemsize <= target:
            best = t
    return best if best is not None else rows


def _sum_pieces(pieces, out_dtype, name):
    n, rows, width = pieces.shape
    tr = _row_tile(rows, width * n)

    def body(p_ref, o_ref):
        acc = p_ref[0].astype(f32)
        for s in range(1, n):
            acc = acc + p_ref[s].astype(f32)
        o_ref[...] = acc.astype(out_dtype)

    return _pc(body, grid=(rows // tr,), name=name,
                          in_specs=[pl.BlockSpec((n, tr, width), lambda i: (0, i, 0))],
                          out_specs=pl.BlockSpec((tr, width), lambda i: (i, 0)),
                          out_shape=SDS((rows, width), out_dtype), compiler_params=_cp(32))(pieces)


def _sum_core_pair(part, got, c_arr):
    n, rows, width = part.shape
    half = rows // 2
    tr = _row_tile(half, width, itemsize=2)
    nt = half // tr

    def body(c_ref, p_ref, g_ref, o_ref):
        o_ref[...] = (p_ref[...].astype(f32) + g_ref[...].astype(f32)).astype(bf16)

    gs = pltpu.PrefetchScalarGridSpec(
        num_scalar_prefetch=1, grid=(n, nt),
        in_specs=[pl.BlockSpec((1, tr, width), lambda j, i, c: (j, c[0] * nt + i, 0)),
                  pl.BlockSpec((1, tr, width), lambda j, i, c: (j, i, 0))],
        out_specs=pl.BlockSpec((1, tr, width), lambda j, i, c: (j, i, 0)))
    return _pc(body, grid_spec=gs, name="sum_core_pair", out_shape=SDS((n, half, width), bf16),
                          compiler_params=_cp(32))(c_arr, part, got)


def _sum_chips(pieces, c_arr, full, row0, total_rows):
    n, half, width = pieces.shape
    tr = max(t for t in range(16, 257, 16) if half % t == 0 and row0 % t == 0)
    nt = half // tr

    def body(c_ref, p_ref, *rest):
        o_ref = rest[-1]
        acc = p_ref[0].astype(f32)
        for s in range(1, n):
            acc = acc + p_ref[s].astype(f32)
        o_ref[...] = acc

    gs = pltpu.PrefetchScalarGridSpec(
        num_scalar_prefetch=1, grid=(nt,),
        in_specs=[pl.BlockSpec((n, tr, width), lambda i, c: (0, i, 0))] + ([] if full is None else [ANY]),
        out_specs=pl.BlockSpec((tr, width), lambda i, c: (row0 // tr + c[0] * nt + i, 0)))
    args = (c_arr, pieces) if full is None else (c_arr, pieces, full)
    return _pc(body, grid_spec=gs, name="sum_chips", out_shape=SDS((total_rows, width), f32),
                          input_output_aliases={} if full is None else {2: 0}, compiler_params=_cp(32))(*args)


def _adamw_math(w, g, m, v):
    mn = ADAM_B1 * m + (1.0 - ADAM_B1) * g
    vn = ADAM_B2 * v + (1.0 - ADAM_B2) * (g * g)
    m_hat = mn / (1.0 - ADAM_B1 ** ADAM_STEP)
    v_hat = vn / (1.0 - ADAM_B2 ** ADAM_STEP)
    return -ADAM_LR * (m_hat / (jnp.sqrt(v_hat) + ADAM_EPS) + ADAM_WD * w), mn, vn


def _adamw(w, g, m, v, name):
    rows, width = w.shape
    tr = _row_tile(rows, width * 7, target=12 * 1024 * 1024)

    def body(w_ref, g_ref, m_ref, v_ref, d_ref, nm_ref, nv_ref):
        d_ref[...], nm_ref[...], nv_ref[...] = _adamw_math(w_ref[...], g_ref[...], m_ref[...], v_ref[...])

    blk = pl.BlockSpec((tr, width), lambda i: (i, 0))
    return _pc(body, grid=(rows // tr,), name=name, in_specs=[blk] * 4, out_specs=[blk] * 3,
                          out_shape=[SDS((rows, width), f32)] * 3, compiler_params=_cp(40))(w, g, m, v)


def _adamw_rows(w, m, v, gblob, tr, first_tile, name):
    layers, rows, width = w.shape

    def body(w_ref, g_ref, m_ref, v_ref, d_ref, nm_ref, nv_ref):
        d_ref[0], nm_ref[0], nv_ref[0] = _adamw_math(w_ref[0], g_ref[...], m_ref[0], v_ref[0])

    blk = pl.BlockSpec((1, tr, width), lambda l, i: (l, i, 0))
    gblk = pl.BlockSpec((tr, width), lambda l, i: (first_tile(l) + i, 0))
    return _pc(body, grid=(layers, rows // tr), name=name, in_specs=[blk, gblk, blk, blk], out_specs=[blk] * 3,
                          out_shape=[SDS(w.shape, f32)] * 3, compiler_params=_cp(40))(w, gblob, m, v)


ANY = pl.BlockSpec(memory_space=pl.ANY)


def _place():
    x, y, c = lax.axis_index("x"), lax.axis_index("y"), lax.axis_index("c")
    chips = [(1 - x, y), (x, 1 - y), (1 - x, 1 - y)]
    return x, y, c, chips


NQ_ICI = 4
NQ_D2D = 8


def _chunks(rows, want):
    n = max(k for k in range(1, want + 1) if rows % k == 0 and (rows // k) % 16 == 0)
    step = rows // n
    return [(q * step, step) for q in range(n)]


def _scatter_copies(ins, outs, ssem, rsem, lsem):
    x, y, c, chips = _place()
    me = (x, y, c)
    locals_, sends, lands = [], [], []
    for b in range(len(ins)):
        for q, (off, n) in enumerate(_chunks(ins[b].shape[1], NQ_ICI)):
            rows = pl.ds(off, n)
            mine = outs[b].at[2 * x + y, rows, :]
            locals_.append(pltpu.make_async_copy(ins[b].at[2 * x + y, rows, :], mine, lsem.at[b, q]))
            for j, chip in enumerate(chips):
                sends.append(_remote(ins[b].at[2 * chip[0] + chip[1], rows, :], mine, ssem.at[b, j, q], rsem.at[b, j, q],
                                     (*chip, c)))
                slot = outs[b].at[2 * chip[0] + chip[1], rows, :]
                lands.append(_remote(slot, slot, ssem.at[b, j, q], rsem.at[b, j, q], me))
    return locals_, sends, lands


def _scatter_start(ins, outs, ssem, rsem, lsem):
    locals_, sends, _ = _scatter_copies(ins, outs, ssem, rsem, lsem)
    for cp in locals_ + sends:
        cp.start()


def _scatter_finish(ins, outs, ssem, rsem, lsem):
    locals_, sends, lands = _scatter_copies(ins, outs, ssem, rsem, lsem)
    for cp in lands:
        cp.wait_recv()
    for cp in sends:
        cp.wait_send()
    for cp in locals_:
        cp.wait()


def _scatter_sems(nb):
    return [pltpu.SemaphoreType.DMA((nb, 3, NQ_ICI)), pltpu.SemaphoreType.DMA((nb, 3, NQ_ICI)),
            pltpu.SemaphoreType.DMA((nb, NQ_ICI))]


def _remote(src, dst, ssem, rsem, dev):
    return pltpu.make_async_remote_copy(src_ref=src, dst_ref=dst, send_sem=ssem, recv_sem=rsem, device_id=dev,
                                        device_id_type=MESH)


def _all_gather_weights(shards):
    nb = len(shards)

    def body(*refs):
        ins, outs, sems = refs[:nb], refs[nb:2 * nb], refs[2 * nb:]
        _gather_start(ins, outs, *sems)
        _gather_finish(ins, outs, *sems)

    return _pc(
        body, name="all_gather_weights", in_specs=[ANY] * nb, out_specs=[ANY] * nb,
        out_shape=[SDS((NCH,) + s.shape, s.dtype) for s in shards], scratch_shapes=_gather_sems(nb),
    )(*shards)


def _gather_first(ins, outs, ssem, rsem, lsem):
    x, y, c, chips = _place()
    locals_, sends = [], []
    for b in range(len(ins)):
        half = ins[b].shape[0] // 2
        for q, (off, n) in enumerate(_chunks(half, NQ_ICI)):
            mine = pl.ds(c * half + off, n)
            own = outs[b].at[2 * x + y, mine, :]
            locals_.append(pltpu.make_async_copy(ins[b].at[mine, :], own, lsem.at[b, q]))
            sends.append(_remote(ins[b].at[mine, :], own, ssem.at[b, 0, q], rsem.at[b, 0, q], (x, y, 1 - c)))
            sends += [_remote(ins[b].at[mine, :], own, ssem.at[b, 1 + j, q], rsem.at[b, 1 + j, q], (*chip, c))
                      for j, chip in enumerate(chips)]
    return locals_, sends


def _gather_start(ins, outs, ssem, rsem, lsem):
    locals_, sends = _gather_first(ins, outs, ssem, rsem, lsem)
    for cp in locals_ + sends:
        cp.start()


def _gather_finish(ins, outs, ssem, rsem, lsem):
    x, y, c, chips = _place()
    me, sib = (x, y, c), (x, y, 1 - c)
    locals_, sends = _gather_first(ins, outs, ssem, rsem, lsem)
    for b in range(len(ins)):
        half = ins[b].shape[0] // 2
        for q, (off, n) in enumerate(_chunks(half, NQ_ICI)):
            mine = pl.ds(c * half + off, n)
            for j, chip in enumerate(chips):
                landed = outs[b].at[2 * chip[0] + chip[1], mine, :]
                _remote(landed, landed, ssem.at[b, 1 + j, q], rsem.at[b, 1 + j, q], me).wait_recv()
                cp = _remote(landed, landed, ssem.at[b, 4 + j, q], rsem.at[b, 4 + j, q], sib)
                cp.start()
                sends.append(cp)
    for b in range(len(ins)):
        half = ins[b].shape[0] // 2
        for q, (off, n) in enumerate(_chunks(half, NQ_ICI)):
            other = pl.ds((1 - c) * half + off, n)
            theirs = outs[b].at[2 * x + y, other, :]
            _remote(theirs, theirs, ssem.at[b, 0, q], rsem.at[b, 0, q], me).wait_recv()
            for j, chip in enumerate(chips):
                fwd = outs[b].at[2 * chip[0] + chip[1], other, :]
                _remote(fwd, fwd, ssem.at[b, 4 + j, q], rsem.at[b, 4 + j, q], me).wait_recv()
    for cp in sends:
        cp.wait_send()
    for cp in locals_:
        cp.wait()


def _gather_sems(nb):
    return [pltpu.SemaphoreType.DMA((nb, 7, NQ_ICI)), pltpu.SemaphoreType.DMA((nb, 7, NQ_ICI)),
            pltpu.SemaphoreType.DMA((nb, NQ_ICI))]


def _send_sibling_half(parts):
    nb = len(parts)

    def body(*refs):
        ins, gots = refs[:nb], refs[nb:2 * nb]
        ssem, rsem = refs[2 * nb:]
        x, y, c, _ = _place()
        sib = (x, y, 1 - c)
        todo = []
        for b in range(nb):
            half = ins[b].shape[1] // 2
            for q, (off, n) in enumerate(_chunks(half, NQ_D2D)):
                cp = _remote(ins[b].at[:, pl.ds((1 - c) * half + off, n), :], gots[b].at[:, pl.ds(off, n), :],
                             ssem.at[b, q], rsem.at[b, q], sib)
                cp.start()
                todo.append(cp)
        for cp in todo:
            cp.wait()

    return _pc(
        body, name="send_sibling_half", in_specs=[ANY] * nb, out_specs=[ANY] * nb,
        out_shape=[SDS((p.shape[0], p.shape[1] // 2, p.shape[2]), p.dtype) for p in parts],
        scratch_shapes=[pltpu.SemaphoreType.DMA((nb, NQ_D2D)), pltpu.SemaphoreType.DMA((nb, NQ_D2D))],
    )(*parts)


def _scatter_to_chips(parts):
    nb = len(parts)

    def body(*refs):
        ins, outs, sems = refs[:nb], refs[nb:2 * nb], refs[2 * nb:]
        _scatter_start(ins, outs, *sems)
        _scatter_finish(ins, outs, *sems)

    return _pc(
        body, name="scatter_to_chips", in_specs=[ANY] * nb, out_specs=[ANY] * nb,
        out_shape=[SDS(p.shape, p.dtype) for p in parts], scratch_shapes=_scatter_sems(nb),
    )(*parts)


def _join_halves(fulls, ranges):
    nb = len(fulls)
    nr = max(len(r) for r in ranges)

    def body(*refs):
        ins, outs = refs[:nb], refs[nb:2 * nb]
        ssem, rsem = refs[2 * nb:]
        x, y, c, _ = _place()
        sib = (x, y, 1 - c)
        sends, lands = [], []
        for b in range(nb):
            for g, (row0, rows) in enumerate(ranges[b]):
                half = rows // 2
                for q, (off, n) in enumerate(_chunks(half, NQ_D2D)):
                    mine = pl.ds(row0 + c * half + off, n)
                    sends.append(_remote(ins[b].at[mine, :], outs[b].at[mine, :], ssem.at[b, g, q], rsem.at[b, g, q], sib))
                    other = outs[b].at[pl.ds(row0 + (1 - c) * half + off, n), :]
                    lands.append(_remote(other, other, ssem.at[b, g, q], rsem.at[b, g, q], sib))
        for cp in sends:
            cp.start()
        for cp in lands:
            cp.wait_recv()
        for cp in sends:
            cp.wait_send()

    return _pc(
        body, name="join_halves", in_specs=[ANY] * nb, out_specs=[ANY] * nb,
        out_shape=[SDS(h.shape, h.dtype) for h in fulls], input_output_aliases={b: b for b in range(nb)},
        scratch_shapes=[pltpu.SemaphoreType.DMA((nb, nr, NQ_D2D)), pltpu.SemaphoreType.DMA((nb, nr, NQ_D2D))],
    )(*fulls)


def _gather_small(vec):
    def body(v_ref, o_ref, ssem, rsem, lsem):
        x, y, c, _ = _place()
        mine = o_ref.at[4 * x + 2 * y + c]
        local = pltpu.make_async_copy(v_ref, mine, lsem)
        local.start()
        sends = []
        for k in range(1, 8):
            peer = (x ^ (k >> 2), y ^ ((k >> 1) & 1), c ^ (k & 1))
            cp = _remote(v_ref, mine, ssem.at[k - 1], rsem.at[k - 1], peer)
            cp.start()
            sends.append(cp)
        for k in range(1, 8):
            px, py, pc = x ^ (k >> 2), y ^ ((k >> 1) & 1), c ^ (k & 1)
            slot = o_ref.at[4 * px + 2 * py + pc]
            _remote(slot, slot, ssem.at[k - 1], rsem.at[k - 1], (x, y, c)).wait_recv()
        for cp in sends:
            cp.wait_send()
        local.wait()

    return _pc(
        body, name="gather_small", in_specs=[ANY], out_specs=ANY, out_shape=SDS((8,) + vec.shape, vec.dtype),
        scratch_shapes=[pltpu.SemaphoreType.DMA((7,)), pltpu.SemaphoreType.DMA((7,)), pltpu.SemaphoreType.DMA],
    )(vec)


def _block_diag(pw):
    return jnp.concatenate([jnp.pad(pw[g], ((0, 0), (64 * g, 192 - 64 * g))) for g in range(4)], axis=0)


def _own_columns(full, chip):
    n = full.shape[-1] // NCH
    parts = full.reshape(full.shape[:-1] + (NCH, n))
    sel = (lax.broadcasted_iota(jnp.int32, (NCH, 1), 0) == chip)
    return jnp.sum(jnp.where(sel, parts, 0.0), axis=-2)


def _at_own_columns(shard, chip):
    n = shard.shape[-1]
    sel = (lax.broadcasted_iota(jnp.int32, (NCH * n,), 0) // n == chip)
    return jnp.where(sel, jnp.tile(shard, NCH), 0.0)


def _pad_rows(a, rows):
    return jnp.pad(a, ((0, rows - a.shape[0]),) + ((0, 0),) * (a.ndim - 1))


def _ffn_block(l, which):
    return 7 * l + 3 * which


def _wout_block(l):
    return 7 * l + 6


class _Weights:
    def __init__(self):
        self.ffn, self.wout, self.w_aug, self.rides = {}, {}, {}, {}

    @classmethod
    def from_blob(cls, blob, w_aug):
        self = cls()
        for l in range(DEPTH):
            self.ffn[(l, 0)], self.ffn[(l, 1)] = (blob, _ffn_block(l, 0)), (blob, _ffn_block(l, 1))
            self.wout[l], self.w_aug[l] = (blob, _wout_block(l)), w_aug[l]
        return self

    def set_w_in(self, l, gathered):
        self.w_aug[l] = jnp.pad(gathered.transpose(1, 0, 2).reshape(D, INW), ((0, 0), (0, INP - INW)))

    def ffn_fwd(self, l, which, x, nw):
        arr, k0 = self.ffn[(l, which)]
        if (l, which) not in self.rides:
            return _ffn_fwd(x, nw, arr, k0)
        shards, landed = self.rides[(l, which)]
        out, *gathered = _ffn_fwd(x, nw, arr, k0, shards)
        landed(gathered)
        return out


def _layer_fwd(l, x0, pos, freq, wts, ws):
    sv = {"x0": x0}
    x1 = ws.ffn_fwd(l, 0, x0, wts["ffn1_norm"][l:l + 1])
    att, att4, att16, pu, dq, dz, dba = _inproj_fwd(x1, wts["mix_norm"][l:l + 1], ws.w_aug[l], pos, freq)
    qkvs = [att, att4.reshape(S, 768), att16.reshape(S, 768)]
    (o1, l1), (o4, l4), (o16, l16) = [_attn_fwd(q, NB // d) for q, d in zip(qkvs, PATTERN_DIL)]
    ols = (o1, l1, o4.reshape(4, S // 4, ATT), l4.reshape(4, S // 4, ATT), o16.reshape(16, S // 16, ATT),
           l16.reshape(16, S // 16, ATT))
    ya = _merge_fwd(*ols)
    yb = _pool_fwd(pu, wts["pool_bd"][l], wts["pool_scale"][l:l + 1])
    c = _conv_fwd(dq, wts["conv_w"][l])
    u, w, qg, kd, intra, aux = _dn_local_fwd(c, dba, wts["dn_par"][l])
    o_dn, states = _dn_rec_fwd(u, w, qg, kd, intra, aux)
    yc = _dn_post_fwd(o_dn, dz, wts["dn_out_norm"][l:l + 1])
    x2 = _outproj_fwd(x1, ya, yb, yc, *ws.wout[l])
    x3 = ws.ffn_fwd(l, 1, x2, wts["ffn2_norm"][l:l + 1])
    sv.update(x1=x1, x2=x2, qkvs=qkvs, ols=ols, ya=ya, yb=yb, yc=yc, pu=pu, dq=dq, dz=dz, dba=dba, c=c,
              u=u, w=w, qg=qg, kd=kd, intra=intra, aux=aux, states=states, o_dn=o_dn)
    return x3, sv


def _wout_part(g):
    return jnp.pad(g.astype(bf16).reshape(NCH, 256, D), ((0, 0), (0, FC - 256), (0, 0)))


def _win_part(g):
    return g[:, :INW].astype(bf16).reshape(D, NCH, INC).transpose(1, 0, 2)


def _layer_bwd(l, dx3, sv, pos, freq, wts, ws, ride=None, prep=None):
    gr = {}
    g2, u2, d2, dh4, *pieces_before = _ffn_bwd(sv["x2"], wts["ffn2_norm"][l:l + 1], *ws.ffn[(l, 1)], dx3, ride)
    dx2, gr["ffn2_norm"] = _norm_bwd(sv["x2"], wts["ffn2_norm"][l:l + 1], dx3, dh4)
    gr.update(ffn2_w_gate=g2, ffn2_w_up=u2, ffn2_w_down=d2)
    dya, dyb, dyc, gr["w_out"] = _outproj_bwd(dx2, sv["ya"], sv["yb"], sv["yc"], *ws.wout[l])
    do_dn, ddz, gr["dn_out_norm"] = _dn_post_bwd(sv["o_dn"], sv["dz"], wts["dn_out_norm"][l:l + 1], dyc)
    du, dw, dqg, dkd, dintra, daux = _dn_rec_bwd(sv["u"], sv["w"], sv["qg"], sv["kd"], sv["intra"], sv["aux"], sv["states"], do_dn)
    dc, ddba, gr["dn_par"] = _dn_local_bwd(sv["c"], sv["dba"], wts["dn_par"][l], du, dw, dqg, dkd, dintra, daux)
    ddq, gr["conv_w"] = _conv_bwd(sv["dq"], wts["conv_w"][l], dc)
    dpu, gr["pool_bd"], gr["pool_scale"] = _pool_bwd(sv["pu"], wts["pool_bd"][l], wts["pool_scale"][l:l + 1], dyb)
    dols = _merge_bwd(*sv["ols"], dya)
    datts = [_attn_bwd(q, dols[2 * p].reshape(S, ATT), dols[2 * p + 1].reshape(S, ATT), NB // d)
             for p, (q, d) in enumerate(zip(sv["qkvs"], PATTERN_DIL))]
    dx1, gr["mix_norm"], gr["w_aug"] = _inproj_bwd(sv["x1"], wts["mix_norm"][l:l + 1], ws.w_aug[l], pos, freq, dx2,
                                                    datts[0], datts[1].reshape(4, S // 4, 768),
                                                    datts[2].reshape(16, S // 16, 768), dpu, ddq, ddz, ddba)
    own = None
    if prep is not None:
        own = prep([jnp.concatenate([g2, u2, d2, _wout_part(gr["w_out"])], axis=1), _win_part(gr["w_aug"])])
    g1, u1, d1, dh4, *pieces_own = _ffn_bwd(sv["x0"], wts["ffn1_norm"][l:l + 1], *ws.ffn[(l, 0)], dx1, own)
    dx0, gr["ffn1_norm"] = _norm_bwd(sv["x0"], wts["ffn1_norm"][l:l + 1], dx1, dh4)
    gr.update(ffn1_w_gate=g1, ffn1_w_up=u1, ffn1_w_down=d1)
    return dx0, gr, pieces_before, pieces_own


def _device_step(x, pos, target, wts, ws, prep=None):
    freq = jnp.tile(ROPE_THETA ** (-jnp.arange(0, EH, 2, dtype=f32) / EH), 2 * ATT // EH).reshape(1, ATT)
    saved = []
    h = x
    for l in range(DEPTH):
        h, sv = _layer_fwd(l, h, pos, freq, wts, ws)
        saved.append(sv)
    dh, g_final, loss = _final(h, wts["final_norm"], target)
    grads = [None] * DEPTH
    dh, grads[1], _, _ = _layer_bwd(1, dh, saved[1], pos, freq, wts, ws)
    sums1 = None
    if prep is not None:
        g = grads[1]
        ffn = [g[f"ffn{f}_w_{n}"] for f in (1, 2) for n in ("gate", "up", "down")]
        sums1 = prep([jnp.concatenate(ffn + [_wout_part(g["w_out"])], axis=1), _win_part(g["w_aug"])])
    dh, grads[0], pieces1, pieces0 = _layer_bwd(0, dh, saved[0], pos, freq, wts, ws, sums1, prep)
    return loss, dh, g_final, grads, pieces1, pieces0


_SMALL = (("ffn1_norm", (DEPTH, D)), ("mix_norm", (DEPTH, D)), ("pool_w", (DEPTH, 4, 64, 64)), ("pool_scale", (DEPTH, 256)),
          ("dn_conv_w", (DEPTH, 4, CW)), ("dn_a_log", (DEPTH, 4)), ("dn_dt_bias", (DEPTH, 4)), ("dn_out_norm", (DEPTH, 128)),
          ("ffn2_norm", (DEPTH, D)), ("final_norm", (D,)), ("loss", (1,)))


def _pack_small(vals):
    rows = []
    for name, shape in _SMALL:
        flat = vals[name].astype(f32).reshape(-1)
        rows.append(jnp.pad(flat, (0, _small_rows(shape) * 128 - flat.shape[0])).reshape(-1, 128))
    out = jnp.concatenate(rows, axis=0)
    return _pad_rows(out, -(-out.shape[0] // 16) * 16)


def _small_rows(shape):
    return -(-int(np.prod(shape)) // 1024) * 8


def _unpack_small(packed):
    vals, r = {}, 0
    for name, shape in _SMALL:
        size, n = int(np.prod(shape)), _small_rows(shape)
        vals[name] = packed[r:r + n].reshape(-1)[:size].reshape(shape)
        r += n
    return vals


def kernel(x, positions, ffn1_norm, ffn1_w_gate, ffn1_w_up, ffn1_w_down, mix_norm, w_in, pool_w, pool_scale, dn_conv_w, dn_a_log, dn_dt_bias, dn_out_norm, w_out, ffn2_norm, ffn2_w_gate, ffn2_w_up, ffn2_w_down, final_norm, loss_target, m_ffn1_norm, m_ffn1_w_gate, m_ffn1_w_up, m_ffn1_w_down, m_mix_norm, m_w_in, m_pool_w, m_pool_scale, m_dn_conv_w, m_dn_a_log, m_dn_dt_bias, m_dn_out_norm, m_w_out, m_ffn2_norm, m_ffn2_w_gate, m_ffn2_w_up, m_ffn2_w_down, m_final_norm, v_ffn1_norm, v_ffn1_w_gate, v_ffn1_w_up, v_ffn1_w_down, v_mix_norm, v_w_in, v_pool_w, v_pool_scale, v_dn_conv_w, v_dn_a_log, v_dn_dt_bias, v_dn_out_norm, v_w_out, v_ffn2_norm, v_ffn2_w_gate, v_ffn2_w_up, v_ffn2_w_down, v_final_norm):
    names = ["ffn1_norm", "ffn1_w_gate", "ffn1_w_up", "ffn1_w_down", "mix_norm", "w_in", "pool_w", "pool_scale", "dn_conv_w",
             "dn_a_log", "dn_dt_bias", "dn_out_norm", "w_out", "ffn2_norm", "ffn2_w_gate", "ffn2_w_up", "ffn2_w_down", "final_norm"]
    W = dict(zip(names, [ffn1_norm, ffn1_w_gate, ffn1_w_up, ffn1_w_down, mix_norm, w_in, pool_w, pool_scale, dn_conv_w,
                         dn_a_log, dn_dt_bias, dn_out_norm, w_out, ffn2_norm, ffn2_w_gate, ffn2_w_up, ffn2_w_down, final_norm]))
    M = dict(zip(names, [m_ffn1_norm, m_ffn1_w_gate, m_ffn1_w_up, m_ffn1_w_down, m_mix_norm, m_w_in, m_pool_w, m_pool_scale,
                         m_dn_conv_w, m_dn_a_log, m_dn_dt_bias, m_dn_out_norm, m_w_out, m_ffn2_norm, m_ffn2_w_gate, m_ffn2_w_up,
                         m_ffn2_w_down, m_final_norm]))
    V = dict(zip(names, [v_ffn1_norm, v_ffn1_w_gate, v_ffn1_w_up, v_ffn1_w_down, v_mix_norm, v_w_in, v_pool_w, v_pool_scale,
                         v_dn_conv_w, v_dn_a_log, v_dn_dt_bias, v_dn_out_norm, v_w_out, v_ffn2_norm, v_ffn2_w_gate, v_ffn2_w_up,
                         v_ffn2_w_down, v_final_norm]))
    chip = 2 * lax.axis_index("x") + lax.axis_index("y")

    ffn_names = [(f"ffn{f}_w_gate", f"ffn{f}_w_up", f"ffn{f}_w_down") for f in (1, 2)]
    tr = lambda t: jnp.swapaxes(t, -1, -2)
    def ffn_rows(l, which):
        g, u, dn = ffn_names[which]
        return [tr(W[g][l]), tr(W[u][l]), W[dn][l]]

    def second_half(l):
        return jnp.concatenate(ffn_rows(l, 1) + [jnp.pad(W["w_out"][l], ((0, FC - 256), (0, 0)))], axis=0).astype(bf16)

    ws = _Weights()
    first0, = _all_gather_weights([jnp.concatenate(ffn_rows(0, 0), axis=0).astype(bf16)])
    ws.ffn[(0, 0)] = (first0, 0)

    def landed_00(gathered):
        ws.ffn[(0, 1)], ws.wout[0] = (gathered[0], 0), (gathered[0], 3)
        ws.set_w_in(0, gathered[1])

    def landed_01(gathered):
        ws.ffn[(1, 0)] = (gathered[0], 0)
        ws.set_w_in(1, gathered[1])

    def landed_10(gathered):
        ws.ffn[(1, 1)], ws.wout[1] = (gathered[0], 0), (gathered[0], 3)

    ws.rides[(0, 0)] = ([second_half(0), W["w_in"][0].astype(bf16)], landed_00)
    ws.rides[(0, 1)] = ([jnp.concatenate(ffn_rows(1, 0), axis=0).astype(bf16), W["w_in"][1].astype(bf16)], landed_01)
    ws.rides[(1, 0)] = ([second_half(1)], landed_10)
    conv_all = _gather_small(_pad_rows(dn_conv_w.reshape(DEPTH * 4 * (CW // NCH) // 128, 128), 32))
    conv_full = jnp.concatenate([conv_all[2 * j, :DEPTH * 4 * (CW // NCH) // 128].reshape(DEPTH, 4, CW // NCH) for j in range(NCH)],
                                axis=-1)

    par = jnp.pad(jnp.stack([dn_a_log, dn_dt_bias], axis=1), ((0, 0), (0, 6), (4, 120)))
    wts = dict(ffn1_norm=ffn1_norm, mix_norm=mix_norm, ffn2_norm=ffn2_norm, final_norm=final_norm.reshape(1, D),
               pool_bd=jnp.stack([_block_diag(pool_w[l]) for l in range(DEPTH)]).astype(bf16),
               pool_scale=pool_scale, conv_w=jnp.pad(conv_full, ((0, 0), (0, 4), (0, 0))),
               dn_par=par, dn_out_norm=dn_out_norm)

    c_arr = lax.axis_index("c").astype(jnp.int32).reshape(1)

    def prep(parts):
        return [_sum_core_pair(p, g, c_arr) for p, g in zip(parts, _send_sibling_half(parts))]

    loss, dx, g_final, grads, pieces1, pieces0 = _device_step(x[0], positions.reshape(S, 1), loss_target[0], wts, ws, prep)
    last = [jnp.concatenate([grads[0][n] for n in ffn_names[0]], axis=1)]
    pieces_last = _scatter_to_chips(prep(last))
    full_b = _sum_chips(pieces_last[0], c_arr, None, 0, RB)
    full_b = _sum_chips(pieces0[0], c_arr, full_b, 3 * FC, RB)
    full_b = _sum_chips(pieces1[0], c_arr, full_b, 7 * FC, RB)
    full_c = _sum_chips(pieces0[1], c_arr, None, 0, RC)
    full_c = _sum_chips(pieces1[1], c_arr, full_c, D, RC)
    full_b, full_c = _join_halves([full_b, full_c], [[(0, 3 * FC), (3 * FC, 4 * FC), (7 * FC, 7 * FC)], [(0, D), (D, D)]])

    small = {"loss": loss[0, 0:1], "final_norm": g_final.reshape(D)}
    for n in ("ffn1_norm", "mix_norm", "ffn2_norm", "pool_scale", "dn_out_norm"):
        small[n] = jnp.stack([grads[l][n].reshape(-1) for l in range(DEPTH)])
    small["pool_w"] = jnp.stack([jnp.stack([grads[l]["pool_bd"][64 * g:64 * (g + 1), 64 * g:64 * (g + 1)] for g in range(4)])
                                 for l in range(DEPTH)])
    small["dn_conv_w"] = jnp.stack([grads[l]["conv_w"][0:4] for l in range(DEPTH)])
    small["dn_a_log"] = jnp.stack([grads[l]["dn_par"][0, 4:8] for l in range(DEPTH)])
    small["dn_dt_bias"] = jnp.stack([grads[l]["dn_par"][1, 4:8] for l in range(DEPTH)])
    packed = _pack_small(small)
    g_small = _sum_pieces(_gather_small(packed), f32, "sum_small")
    gs = _unpack_small(g_small)

    transposed = ("ffn1_w_gate", "ffn1_w_up", "ffn2_w_gate", "ffn2_w_up")
    where = {"ffn1_w_gate": (full_b, FC // 2, lambda l: 14 * l), "ffn1_w_up": (full_b, FC // 2, lambda l: 14 * l + 2),
             "ffn1_w_down": (full_b, FC // 2, lambda l: 14 * l + 4), "ffn2_w_gate": (full_b, FC // 2, lambda l: 14 * l + 6),
             "ffn2_w_up": (full_b, FC // 2, lambda l: 14 * l + 8), "ffn2_w_down": (full_b, FC // 2, lambda l: 14 * l + 10),
             "w_out": (full_b, 64, lambda l: (FC // 64) * (7 * l + 6)), "w_in": (full_c, D // 2, lambda l: 2 * l)}
    big_res = {}
    for n, (gblob, tile, first) in where.items():
        t = tr if n in transposed else (lambda a: a)
        big_res[n] = [t(r) for r in _adamw_rows(t(W[n]), t(M[n]), t(V[n]), gblob, tile, first, "adamw_" + n)]

    def small_of(T):
        d = {n: T[n] for n, _ in _SMALL if n not in ("loss", "dn_conv_w")}
        d["loss"] = jnp.zeros((1,), f32)
        d["dn_conv_w"] = _at_own_columns(T["dn_conv_w"], chip)
        return _pack_small(d)

    res_s = _adamw(small_of(W), g_small, small_of(M), small_of(V), "adamw_small")
    small_out = [_unpack_small(r) for r in res_s]

    def split_blobs(b, c):
        out = {}
        b7 = b.reshape(DEPTH, 7, FC, D)
        for k, n in enumerate(n for names3 in ffn_names for n in names3):
            out[n] = tr(b7[:, k]) if n in transposed else b7[:, k]
        out["w_out"] = b7[:, 6, :256]
        out["w_in"] = c.reshape(DEPTH, D, INC)
        return out

    def assemble(big, sm):
        out = []
        for n in names:
            if n in big:
                out.append(big[n])
            elif n == "dn_conv_w":
                out.append(_own_columns(sm[n], chip))
            else:
                out.append(sm[n])
        return out

    grad_list = assemble(split_blobs(full_b, full_c), gs)
    outs = [gs["loss"].reshape(()), dx.reshape(1, S, D)] + grad_list
    for k in range(3):
        outs += assemble({n: r[k] for n, r in big_res.items()}, small_out[k])
    return tuple(outs)
```

```python
import functools
import math

import jax
import jax.numpy as jnp
import numpy as np
from jax import lax
from jax.experimental import pallas as pl
from jax.experimental.pallas import tpu as pltpu

f32 = jnp.float32
bf16 = jnp.bfloat16
SDS = jax.ShapeDtypeStruct
MESH = pl.DeviceIdType.MESH

S = 4096
D = 1024
DEPTH = 2
FF = 2816
NCH = 4
FC = FF // NCH
INW = 3080
INC = INW // NCH
INP = 3200
ATT = 256
EH = 64
NBLK = 128
DNW = 512
DH = 128
CH = 64
NCHUNK = S // CH
EPS = 1e-6
ROPE_THETA = 10000.0
PATTERN_DIL = (1, 4, 16)
ADAM_LR, ADAM_B1, ADAM_B2, ADAM_EPS, ADAM_WD, ADAM_STEP = 0.001, 0.9, 0.999, 1e-08, 0.01, 10
VMEM_BYTES_V7X = 64 * 1024 * 1024
NEG = -1e30

TM = 512
RB, RC = 14 * FC, 2 * D


def _cp(vmem_mb=48, sem=None):
    kw = dict(vmem_limit_bytes=vmem_mb * 1024 * 1024)
    if sem is not None:
        kw["dimension_semantics"] = sem
    return pltpu.CompilerParams(**kw)


def _pc(*args, **kwargs):
    pin = lambda s: pltpu.HBM(s.shape, s.dtype) if isinstance(s, SDS) and jnp.issubdtype(s.dtype, jnp.floating) else s
    out = kwargs["out_shape"]
    kwargs["out_shape"] = [pin(s) for s in out] if isinstance(out, (list, tuple)) else pin(out)
    call = pl.pallas_call(*args, **kwargs)

    def run(*operands):
        pinned = [pltpu.with_memory_space_constraint(o, pltpu.HBM) if jnp.issubdtype(o.dtype, jnp.floating) else o
                  for o in operands]
        return call(*pinned)

    return run


def _dot(a, b):
    return jnp.dot(a, b, preferred_element_type=f32)


def _dot_nt(a, b):
    return lax.dot_general(a, b, (((1,), (1,)), ((), ())), preferred_element_type=f32)


def _dot_tn(a, b):
    return lax.dot_general(a, b, (((0,), (0,)), ((), ())), preferred_element_type=f32)


def _rms(x, w):
    r = lax.rsqrt(jnp.mean(x * x, axis=-1, keepdims=True) + EPS)
    return x * r * w, r


def _rms_bwd(x, w, r, dh):
    xhat = x * r
    dw = jnp.sum(dh * xhat, axis=0, keepdims=True)
    dxh = dh * w
    dx = r * (dxh - xhat * jnp.mean(dxh * xhat, axis=-1, keepdims=True))
    return dx, dw


def _ffn_fwd(x, nw, blob, k0, ride=None):
    kg, ku, kd = k0, k0 + 1, k0 + 2
    nr = 0 if ride is None else len(ride)
    ni = S // TM

    def body(*refs):
        x_ref, nw_ref, wg_ref, wu_ref, wd_ref = refs[:5]
        ride_in = refs[5:5 + nr]
        o_ref = refs[5 + nr]
        ride_out = refs[6 + nr:6 + 2 * nr]
        h_scr, acc_scr = refs[6 + 2 * nr:8 + 2 * nr]
        sems = refs[8 + 2 * nr:]
        i = pl.program_id(0)
        j = pl.program_id(1)

        if nr:
            @pl.when(jnp.logical_and(i == 0, j == 0))
            def _():
                _gather_start(ride_in, ride_out, *sems)

        @pl.when(j == 0)
        def _():
            h, _ = _rms(x_ref[...], nw_ref[...])
            h_scr[...] = h.astype(bf16)
            acc_scr[...] = jnp.zeros_like(acc_scr)

        h = h_scr[...]
        g = _dot_nt(h, wg_ref[0])
        u = _dot_nt(h, wu_ref[0])
        a = (g * jax.nn.sigmoid(g) * u).astype(bf16)
        acc_scr[...] += _dot(a, wd_ref[0])

        @pl.when(j == NCH - 1)
        def _():
            o_ref[...] = x_ref[...] + 0.5 * acc_scr[...]

        if nr:
            @pl.when(jnp.logical_and(i == ni - 1, j == NCH - 1))
            def _():
                _gather_finish(ride_in, ride_out, *sems)

    wspec = lambda k: pl.BlockSpec((1, FC, D), lambda i, j: (j, k, 0))
    rides = [] if ride is None else list(ride)
    res = _pc(
        body, grid=(ni, NCH), name="ffn_fwd_ride" if nr else "ffn_fwd",
        in_specs=[pl.BlockSpec((TM, D), lambda i, j: (i, 0)),
                  pl.BlockSpec((1, D), lambda i, j: (0, 0)),
                  wspec(kg), wspec(ku), wspec(kd)] + [ANY] * nr,
        out_specs=[pl.BlockSpec((TM, D), lambda i, j: (i, 0))] + [ANY] * nr,
        out_shape=[SDS((S, D), f32)] + [SDS((NCH,) + r.shape, r.dtype) for r in rides],
        scratch_shapes=[pltpu.VMEM((TM, D), bf16), pltpu.VMEM((TM, D), f32)] + (_gather_sems(nr) if nr else []),
        compiler_params=_cp(40),
    )(x, nw, blob, blob, blob, *rides)
    return res if nr else res[0]


def _ffn_bwd(x, nw, blob, k0, dy, ride=None):
    nt = S // TM
    kg, ku, kd = k0, k0 + 1, k0 + 2
    nr = 0 if ride is None else len(ride)

    def body(*refs):
        x_ref, nw_ref, wg_ref, wu_ref, wd_ref, dy_ref = refs[:6]
        ride_in = refs[6:6 + nr]
        dwg_ref, dwu_ref, dwd_ref, dh_ref = refs[6 + nr:10 + nr]
        ride_out = refs[10 + nr:10 + 2 * nr]
        ag, au, ad = refs[10 + 2 * nr:13 + 2 * nr]
        sems = refs[13 + 2 * nr:]
        j = pl.program_id(0)
        i = pl.program_id(1)

        if nr:
            @pl.when(jnp.logical_and(j == 0, i == 0))
            def _():
                _scatter_start(ride_in, ride_out, *sems)

        @pl.when(i == 0)
        def _():
            ag[...] = jnp.zeros_like(ag)
            au[...] = jnp.zeros_like(au)
            ad[...] = jnp.zeros_like(ad)

        hf, _ = _rms(x_ref[...], nw_ref[...])
        h = hf.astype(bf16)
        g = _dot_nt(h, wg_ref[0])
        u = _dot_nt(h, wu_ref[0])
        sg = jax.nn.sigmoid(g)
        s = g * sg
        a = (s * u).astype(bf16)
        dyb = (0.5 * dy_ref[...]).astype(bf16)
        da = _dot_nt(dyb, wd_ref[0])
        ad[...] += _dot_tn(a, dyb)
        du = (da * s).astype(bf16)
        dg = (da * u * (sg * (1.0 + g * (1.0 - sg)))).astype(bf16)
        ag[...] += _dot_tn(dg, h)
        au[...] += _dot_tn(du, h)
        dh_ref[0] = (_dot(dg, wg_ref[0]) + _dot(du, wu_ref[0])).astype(bf16)

        @pl.when(i == nt - 1)
        def _():
            dwg_ref[0] = ag[...].astype(bf16)
            dwu_ref[0] = au[...].astype(bf16)
            dwd_ref[0] = ad[...].astype(bf16)

        if nr:
            @pl.when(jnp.logical_and(j == NCH - 1, i == nt - 1))
            def _():
                _scatter_finish(ride_in, ride_out, *sems)

    wspec = lambda k: pl.BlockSpec((1, FC, D), lambda j, i: (j, k, 0))
    gspec = pl.BlockSpec((1, FC, D), lambda j, i: (j, 0, 0))
    rides = [] if ride is None else list(ride)
    return _pc(
        body, grid=(NCH, nt), name="ffn_bwd_ride" if nr else "ffn_bwd",
        in_specs=[pl.BlockSpec((TM, D), lambda j, i: (i, 0)),
                  pl.BlockSpec((1, D), lambda j, i: (0, 0)),
                  wspec(kg), wspec(ku), wspec(kd),
                  pl.BlockSpec((TM, D), lambda j, i: (i, 0))] + [ANY] * nr,
        out_specs=[gspec, gspec, gspec, pl.BlockSpec((1, TM, D), lambda j, i: (j, i, 0))] + [ANY] * nr,
        out_shape=[SDS((NCH, FC, D), bf16)] * 3 + [SDS((NCH, S, D), bf16)] + [SDS(r.shape, r.dtype) for r in rides],
        scratch_shapes=[pltpu.VMEM((FC, D), f32)] * 3 + (_scatter_sems(nr) if nr else []),
        compiler_params=_cp(56),
    )(x, nw, blob, blob, blob, dy, *rides)


def _norm_bwd(x, nw, dres, dh4):
    nt = S // TM
    nparts = dh4.shape[0]

    def body(x_ref, nw_ref, dres_ref, dh_ref, dx_ref, dnw_ref):
        i = pl.program_id(0)
        dh = dh_ref[0].astype(f32)
        for p in range(1, nparts):
            dh = dh + dh_ref[p].astype(f32)
        xv = x_ref[...]
        _, r = _rms(xv, nw_ref[...])
        dx, dw = _rms_bwd(xv, nw_ref[...], r, dh)
        dx_ref[...] = dres_ref[...] + dx

        @pl.when(i == 0)
        def _():
            dnw_ref[...] = jnp.zeros_like(dnw_ref)

        dnw_ref[...] += dw

    return _pc(
        body, grid=(nt,), name="norm_bwd",
        in_specs=[pl.BlockSpec((TM, D), lambda i: (i, 0)),
                  pl.BlockSpec((1, D), lambda i: (0, 0)),
                  pl.BlockSpec((TM, D), lambda i: (i, 0)),
                  pl.BlockSpec((nparts, TM, D), lambda i: (0, i, 0))],
        out_specs=[pl.BlockSpec((TM, D), lambda i: (i, 0)), pl.BlockSpec((1, D), lambda i: (0, 0))],
        out_shape=[SDS((S, D), f32), SDS((1, D), f32)],
        compiler_params=_cp(40),
    )(x, nw, dres, dh4)


def _final(x, nw, target):
    nt = S // TM

    def body(x_ref, nw_ref, t_ref, dx_ref, dnw_ref, loss_ref):
        i = pl.program_id(0)
        xv = x_ref[...]
        y, r = _rms(xv, nw_ref[...])
        err = y - t_ref[...]
        part = 0.5 * jnp.sum(jnp.mean(err * err, axis=-1, keepdims=True), axis=0, keepdims=True)
        dx, dw = _rms_bwd(xv, nw_ref[...], r, err * (1.0 / D))
        dx_ref[...] = dx

        @pl.when(i == 0)
        def _():
            dnw_ref[...] = jnp.zeros_like(dnw_ref)
            loss_ref[...] = jnp.zeros_like(loss_ref)

        dnw_ref[...] += dw
        loss_ref[...] += jnp.broadcast_to(part, loss_ref.shape)

    return _pc(
        body, grid=(nt,), name="final_loss",
        in_specs=[pl.BlockSpec((TM, D), lambda i: (i, 0)),
                  pl.BlockSpec((1, D), lambda i: (0, 0)),
                  pl.BlockSpec((TM, D), lambda i: (i, 0))],
        out_specs=[pl.BlockSpec((TM, D), lambda i: (i, 0)), pl.BlockSpec((1, D), lambda i: (0, 0)),
                   pl.BlockSpec((1, 128), lambda i: (0, 0))],
        out_shape=[SDS((S, D), f32), SDS((1, D), f32), SDS((1, 128), f32)],
        compiler_params=_cp(40),
    )(x, nw, target)


def _rot_half(t):
    lane = lax.broadcasted_iota(jnp.int32, t.shape, 1)
    first = (lane % EH) < (EH // 2)
    return jnp.where(first, -pltpu.roll(t, ATT - EH // 2, 1), pltpu.roll(t, EH // 2, 1))


def _rope_tables(pos_ref, freq_ref):
    ang = pos_ref[...].astype(f32) * freq_ref[...]
    return jnp.cos(ang), jnp.sin(ang)


def _split_residues(val, scr, outs):
    rows, cols = val.shape
    for j in range(cols // 128):
        scr[j] = val[:, 128 * j:128 * (j + 1)]
    for ref, d in outs:
        for j in range(cols // 128):
            for r in range(d):
                ref.at[r][:, 128 * j:128 * (j + 1)] = scr.at[j][pl.ds(r, rows // d, stride=d), :]


def _join_residues(ref, d, scr):
    rows, cols = scr.shape[1], ref.shape[2]
    for j in range(cols // 128):
        for r in range(d):
            scr.at[j][pl.ds(r, rows // d, stride=d), :] = ref.at[r][:, 128 * j:128 * (j + 1)]
    return jnp.concatenate([scr[j] for j in range(cols // 128)], axis=1)


def _res_spec(d, tile, cols):
    return pl.BlockSpec((d, tile // d, cols), lambda i: (0, i, 0))


def _inproj_fwd(x, nw, w_aug, pos, freq):
    TI = 256

    def body(x_ref, nw_ref, w_hbm, pos_ref, freq_ref, att_ref, att4_ref, att16_ref, pu_ref, dq_ref, dz_ref, dba_ref,
             w_scr, r_scr):
        @pl.when(pl.program_id(0) == 0)
        def _():
            pltpu.sync_copy(w_hbm, w_scr)

        h, _ = _rms(x_ref[...], nw_ref[...])
        proj = _dot(h.astype(bf16), w_scr[...])
        cos, sin = _rope_tables(pos_ref, freq_ref)
        q = proj[:, 0:ATT]
        k = proj[:, ATT:2 * ATT]
        att = jnp.concatenate([q * cos + _rot_half(q) * sin, k * cos + _rot_half(k) * sin, proj[:, 2 * ATT:3 * ATT]], axis=1)
        att_ref[...] = att
        _split_residues(att, r_scr, [(att4_ref, 4), (att16_ref, 16)])
        pu_ref[...] = proj[:, 768:1024]
        dq_ref[...] = proj[:, 1024:2560]
        dz_ref[...] = proj[:, 2560:3072]
        dba_ref[...] = proj[:, 3072:3200]

    return _pc(
        body, grid=(S // TI,), name="inproj_fwd",
        in_specs=[pl.BlockSpec((TI, D), lambda i: (i, 0)),
                  pl.BlockSpec((1, D), lambda i: (0, 0)),
                  pl.BlockSpec(memory_space=pl.ANY),
                  pl.BlockSpec((TI, 1), lambda i: (i, 0)),
                  pl.BlockSpec((1, ATT), lambda i: (0, 0))],
        out_specs=[pl.BlockSpec((TI, 768), lambda i: (i, 0)), _res_spec(4, TI, 768), _res_spec(16, TI, 768),
                   pl.BlockSpec((TI, 256), lambda i: (i, 0)),
                   pl.BlockSpec((TI, 1536), lambda i: (i, 0)), pl.BlockSpec((TI, 512), lambda i: (i, 0)),
                   pl.BlockSpec((TI, 128), lambda i: (i, 0))],
        out_shape=[SDS((S, 768), f32), SDS((4, S // 4, 768), f32), SDS((16, S // 16, 768), f32), SDS((S, 256), f32),
                   SDS((S, 1536), f32), SDS((S, 512), f32), SDS((S, 128), f32)],
        scratch_shapes=[pltpu.VMEM((D, INP), bf16), pltpu.VMEM((6, TI, 128), f32)],
        compiler_params=_cp(48),
    )(x, nw, w_aug, pos, freq)


def _inproj_bwd(x, nw, w_aug, pos, freq, dres, datt, datt4, datt16, dpu, ddq, ddz, ddba):
    TI = 256
    nt = S // TI

    keep = 4

    def body(x_ref, nw_ref, w_hbm, pos_ref, freq_ref, dres_ref, datt_ref, datt4_ref, datt16_ref, dpu_ref, ddq_ref, ddz_ref,
             ddba_ref, dx_ref, dnw_ref, dw_hbm, w_scr, acc, r_scr, h_keep, p_keep):
        i = pl.program_id(0)

        @pl.when(i == 0)
        def _():
            pltpu.sync_copy(w_hbm, w_scr)
            acc[...] = jnp.zeros_like(acc)
            dnw_ref[...] = jnp.zeros_like(dnw_ref)

        xv = x_ref[...]
        hf, r = _rms(xv, nw_ref[...])
        h = hf.astype(bf16)
        cos, sin = _rope_tables(pos_ref, freq_ref)
        datt = datt_ref[...] + _join_residues(datt4_ref, 4, r_scr)
        datt = datt + _join_residues(datt16_ref, 16, r_scr)
        dq = datt[:, 0:ATT]
        dk = datt[:, ATT:2 * ATT]
        dq = dq * cos - _rot_half(dq) * sin
        dk = dk * cos - _rot_half(dk) * sin
        dproj = jnp.concatenate([dq, dk, datt[:, 2 * ATT:3 * ATT], dpu_ref[...], ddq_ref[...], ddz_ref[...], ddba_ref[...]],
                                axis=1).astype(bf16)
        slot = pl.ds(pl.multiple_of((i % keep) * TI, TI), TI)
        h_keep[slot, :] = h
        p_keep[slot, :] = dproj

        @pl.when(i % keep == keep - 1)
        def _():
            acc[...] += _dot_tn(h_keep[...], p_keep[...])

        dh = _dot_nt(dproj, w_scr[...])
        dx, dw = _rms_bwd(xv, nw_ref[...], r, dh)
        dx_ref[...] = dres_ref[...] + dx
        dnw_ref[...] += dw

        @pl.when(i == nt - 1)
        def _():
            pltpu.sync_copy(acc, dw_hbm)

    return _pc(
        body, grid=(nt,), name="inproj_bwd",
        in_specs=[pl.BlockSpec((TI, D), lambda i: (i, 0)),
                  pl.BlockSpec((1, D), lambda i: (0, 0)),
                  pl.BlockSpec(memory_space=pl.ANY),
                  pl.BlockSpec((TI, 1), lambda i: (i, 0)),
                  pl.BlockSpec((1, ATT), lambda i: (0, 0)),
                  pl.BlockSpec((TI, D), lambda i: (i, 0)),
                  pl.BlockSpec((TI, 768), lambda i: (i, 0)), _res_spec(4, TI, 768), _res_spec(16, TI, 768),
                  pl.BlockSpec((TI, 256), lambda i: (i, 0)),
                  pl.BlockSpec((TI, 1536), lambda i: (i, 0)),
                  pl.BlockSpec((TI, 512), lambda i: (i, 0)),
                  pl.BlockSpec((TI, 128), lambda i: (i, 0))],
        out_specs=[pl.BlockSpec((TI, D), lambda i: (i, 0)), pl.BlockSpec((1, D), lambda i: (0, 0)),
                   pl.BlockSpec(memory_space=pl.ANY)],
        out_shape=[SDS((S, D), f32), SDS((1, D), f32), SDS((D, INP), f32)],
        scratch_shapes=[pltpu.VMEM((D, INP), bf16), pltpu.VMEM((D, INP), f32), pltpu.VMEM((6, TI, 128), f32),
                        pltpu.VMEM((keep * TI, D), bf16), pltpu.VMEM((keep * TI, INP), bf16)],
        compiler_params=_cp(56),
    )(x, nw, w_aug, pos, freq, dres, datt, datt4, datt16, dpu, ddq, ddz, ddba)


def _outproj_fwd(x, ya, yb, yc, blob_b, kw):
    def body(x_ref, ya_ref, yb_ref, yc_ref, w_ref, o_ref):
        ycat = jnp.concatenate([ya_ref[...], yb_ref[...], yc_ref[...]], axis=1).astype(bf16)
        o_ref[...] = x_ref[...] + _dot(ycat, w_ref[:, 0:256, :].reshape(D, D))

    return _pc(
        body, grid=(S // TM,), name="outproj_fwd",
        in_specs=[pl.BlockSpec((TM, D), lambda i: (i, 0)),
                  pl.BlockSpec((TM, 256), lambda i: (i, 0)),
                  pl.BlockSpec((TM, 256), lambda i: (i, 0)),
                  pl.BlockSpec((TM, 512), lambda i: (i, 0)),
                  pl.BlockSpec((NCH, FC, D), lambda i: (0, kw, 0))],
        out_specs=pl.BlockSpec((TM, D), lambda i: (i, 0)),
        out_shape=SDS((S, D), f32),
        compiler_params=_cp(40),
    )(x, ya, yb, yc, blob_b)


def _outproj_bwd(x, nw, dres, dh4, ya, yb, yc, blob_b, kw):
    nt = S // TM
    nparts = dh4.shape[0]

    def body(x_ref, nw_ref, dres_ref, dh_ref, ya_ref, yb_ref, yc_ref, w_ref, dx_ref, dnw_ref, dya_ref, dyb_ref, dyc_ref, dw_ref):
        i = pl.program_id(0)

        @pl.when(i == 0)
        def _():
            dw_ref[...] = jnp.zeros_like(dw_ref)
            dnw_ref[...] = jnp.zeros_like(dnw_ref)

        dh = dh_ref[0].astype(f32)
        for p in range(1, nparts):
            dh = dh + dh_ref[p].astype(f32)
        xv = x_ref[...]
        _, r = _rms(xv, nw_ref[...])
        dxn, dnw = _rms_bwd(xv, nw_ref[...], r, dh)
        dx = dres_ref[...] + dxn
        dx_ref[...] = dx
        dnw_ref[...] += dnw
        dyv = dx.astype(bf16)
        ycat = jnp.concatenate([ya_ref[...], yb_ref[...], yc_ref[...]], axis=1).astype(bf16)
        dw_ref[...] += _dot_tn(ycat, dyv)
        dcat = _dot_nt(dyv, w_ref[:, 0:256, :].reshape(D, D))
        dya_ref[...] = dcat[:, 0:256]
        dyb_ref[...] = dcat[:, 256:512]
        dyc_ref[...] = dcat[:, 512:1024]

    return _pc(
        body, grid=(nt,), name="outproj_bwd",
        in_specs=[pl.BlockSpec((TM, D), lambda i: (i, 0)),
                  pl.BlockSpec((1, D), lambda i: (0, 0)),
                  pl.BlockSpec((TM, D), lambda i: (i, 0)),
                  pl.BlockSpec((nparts, TM, D), lambda i: (0, i, 0)),
                  pl.BlockSpec((TM, 256), lambda i: (i, 0)),
                  pl.BlockSpec((TM, 256), lambda i: (i, 0)),
                  pl.BlockSpec((TM, 512), lambda i: (i, 0)),
                  pl.BlockSpec((NCH, FC, D), lambda i: (0, kw, 0))],
        out_specs=[pl.BlockSpec((TM, D), lambda i: (i, 0)), pl.BlockSpec((1, D), lambda i: (0, 0)),
                   pl.BlockSpec((TM, 256), lambda i: (i, 0)), pl.BlockSpec((TM, 256), lambda i: (i, 0)),
                   pl.BlockSpec((TM, 512), lambda i: (i, 0)), pl.BlockSpec((D, D), lambda i: (0, 0))],
        out_shape=[SDS((S, D), f32), SDS((1, D), f32), SDS((S, 256), f32), SDS((S, 256), f32), SDS((S, 512), f32),
                   SDS((D, D), f32)],
        compiler_params=_cp(48),
    )(x, nw, dres, dh4, ya, yb, yc, blob_b)


QT = NBLK
NB = S // QT


def _attn_block(q, kp, kc, vp, vc, first):
    kk = jnp.concatenate([kp, kc], axis=0).astype(bf16)
    vv = jnp.concatenate([vp, vc], axis=0).astype(bf16)
    qi = lax.broadcasted_iota(jnp.int32, (4 * QT, NBLK + QT), 0) % QT
    ki = lax.broadcasted_iota(jnp.int32, (4 * QT, NBLK + QT), 1)
    dist = NBLK + qi - ki
    valid = (dist >= 0) & (dist <= NBLK) & (jnp.logical_not(first) | (ki >= NBLK))
    head = lax.broadcasted_iota(jnp.int32, (1, ATT), 1) // EH
    masks = [(head == h).astype(f32) for h in range(4)]
    qs = jnp.concatenate([q * (mh * (1.0 / math.sqrt(EH))) for mh in masks], axis=0).astype(bf16)
    s = _dot_nt(qs, kk)
    s = jnp.where(valid, s, NEG)
    m = lax.stop_gradient(jnp.max(s, axis=-1, keepdims=True))
    p = jnp.exp(s - m)
    den = jnp.sum(p, axis=-1, keepdims=True)
    po = _dot((p * (1.0 / den)).astype(bf16), vv)
    lse = m + jnp.log(den)
    o = jnp.zeros((QT, ATT), f32)
    l = jnp.zeros((QT, ATT), f32)
    for h, mh in enumerate(masks):
        o = o + po[QT * h:QT * (h + 1)] * mh
        l = l + lse[QT * h:QT * (h + 1)] * mh
    return o, l


def _attn_specs(tile):
    own = lambda col: pl.BlockSpec((QT, ATT), lambda s: (tile(s), col))
    prev = lambda col: pl.BlockSpec((NBLK, ATT), lambda s: (jnp.maximum((QT // NBLK) * tile(s) - 1, 0), col))
    return [own(0), prev(1), own(1), prev(2), own(2)]


def _attn_fwd(qkv, per_seq):
    def body(q_ref, kp_ref, kc_ref, vp_ref, vc_ref, o_ref, l_ref):
        first = pl.program_id(0) % per_seq == 0
        o, l = _attn_block(q_ref[...], kp_ref[...], kc_ref[...], vp_ref[...], vc_ref[...], first)
        o_ref[...] = o
        l_ref[...] = l

    blk = pl.BlockSpec((QT, ATT), lambda t: (t, 0))
    return _pc(
        body, grid=(NB,), name="attn_fwd", in_specs=_attn_specs(lambda t: t), out_specs=[blk, blk],
        out_shape=[SDS((S, ATT), f32), SDS((S, ATT), f32)], compiler_params=_cp(32),
    )(qkv, qkv, qkv, qkv, qkv)


def _attn_bwd(qkv, do, dl, per_seq):
    def body(q_ref, kp_ref, kc_ref, vp_ref, vc_ref, do_ref, dl_ref, o_ref, k_carry, v_carry):
        step = pl.program_id(0)

        @pl.when(step == 0)
        def _():
            k_carry[...] = jnp.zeros_like(k_carry)
            v_carry[...] = jnp.zeros_like(v_carry)

        t = NB - 1 - step
        first = t % per_seq == 0
        last = t % per_seq == per_seq - 1
        fn = lambda q, kp, kc, vp, vc: _attn_block(q, kp, kc, vp, vc, first)
        _, vjp = jax.vjp(fn, q_ref[...], kp_ref[...], kc_ref[...], vp_ref[...], vc_ref[...])
        dq, dkp, dkc, dvp, dvc = vjp((do_ref[...], dl_ref[...]))
        o_ref[:, 0:ATT] = dq
        o_ref[:, ATT:2 * ATT] = dkc
        o_ref[:, 2 * ATT:3 * ATT] = dvc
        o_ref[QT - NBLK:QT, ATT:2 * ATT] += jnp.where(last, 0.0, k_carry[...])
        o_ref[QT - NBLK:QT, 2 * ATT:3 * ATT] += jnp.where(last, 0.0, v_carry[...])
        k_carry[...] = dkp
        v_carry[...] = dvp

    rev = lambda s: NB - 1 - s
    blk = pl.BlockSpec((QT, ATT), lambda s: (rev(s), 0))
    return _pc(
        body, grid=(NB,), name="attn_bwd", in_specs=_attn_specs(rev) + [blk, blk],
        out_specs=pl.BlockSpec((QT, 768), lambda s: (rev(s), 0)),
        out_shape=SDS((S, 768), f32), scratch_shapes=[pltpu.VMEM((NBLK, ATT), f32)] * 2, compiler_params=_cp(40),
    )(qkv, qkv, qkv, qkv, qkv, do, dl)


def _merge_weights(l0, l1, l2):
    m = jnp.maximum(jnp.maximum(l0, l1), l2)
    e0, e1, e2 = jnp.exp(l0 - m), jnp.exp(l1 - m), jnp.exp(l2 - m)
    tot = e0 + e1 + e2
    return e0 / tot, e1 / tot, e2 / tot


def _merge_specs():
    nat = pl.BlockSpec((TM, ATT), lambda i: (i, 0))
    return nat, _res_spec(4, TM, ATT), _res_spec(16, TM, ATT)


def _merge_fwd(o1, l1, o4, l4, o16, l16):
    def body(o1_ref, l1_ref, o4_ref, l4_ref, o16_ref, l16_ref, y_ref, scr):
        o4v, l4v = _join_residues(o4_ref, 4, scr), _join_residues(l4_ref, 4, scr)
        o16v, l16v = _join_residues(o16_ref, 16, scr), _join_residues(l16_ref, 16, scr)
        w0, w1, w2 = _merge_weights(l1_ref[...], l4v, l16v)
        y_ref[...] = w0 * o1_ref[...] + w1 * o4v + w2 * o16v

    nat, r4, r16 = _merge_specs()
    return _pc(body, grid=(S // TM,), name="merge_fwd", in_specs=[nat, nat, r4, r4, r16, r16],
                          out_specs=nat, out_shape=SDS((S, ATT), f32), scratch_shapes=[pltpu.VMEM((2, TM, 128), f32)],
                          compiler_params=_cp(32))(o1, l1, o4, l4, o16, l16)


def _merge_bwd(o1, l1, o4, l4, o16, l16, dy):
    def body(o1_ref, l1_ref, o4_ref, l4_ref, o16_ref, l16_ref, dy_ref, do1_ref, dl1_ref, do4_ref, dl4_ref, do16_ref, dl16_ref, scr):
        o4v, l4v = _join_residues(o4_ref, 4, scr), _join_residues(l4_ref, 4, scr)
        o16v, l16v = _join_residues(o16_ref, 16, scr), _join_residues(l16_ref, 16, scr)
        o1v = o1_ref[...]
        w0, w1, w2 = _merge_weights(l1_ref[...], l4v, l16v)
        y = w0 * o1v + w1 * o4v + w2 * o16v
        dyv = dy_ref[...]
        do1_ref[...] = w0 * dyv
        dl1_ref[...] = w0 * (o1v - y) * dyv
        _split_residues(w1 * dyv, scr, [(do4_ref, 4)])
        _split_residues(w1 * (o4v - y) * dyv, scr, [(dl4_ref, 4)])
        _split_residues(w2 * dyv, scr, [(do16_ref, 16)])
        _split_residues(w2 * (o16v - y) * dyv, scr, [(dl16_ref, 16)])

    nat, r4, r16 = _merge_specs()
    return _pc(body, grid=(S // TM,), name="merge_bwd", in_specs=[nat, nat, r4, r4, r16, r16, nat],
                          out_specs=[nat, nat, r4, r4, r16, r16],
                          out_shape=[SDS((S, ATT), f32)] * 2 + [SDS((4, S // 4, ATT), f32)] * 2 + [SDS((16, S // 16, ATT), f32)] * 2,
                          scratch_shapes=[pltpu.VMEM((2, TM, 128), f32)], compiler_params=_cp(32))(o1, l1, o4, l4, o16, l16, dy)


HALO = 16


def _pool_consts(i, rows):
    grp = lax.broadcasted_iota(jnp.int32, (rows, 256), 1) // 64
    t = i * TM + lax.broadcasted_iota(jnp.int32, (rows, 256), 0)
    win = jnp.where(grp == 0, 2, jnp.where(grp == 1, 4, jnp.where(grp == 2, 8, 16)))
    cnt = jnp.minimum(t + 1, win).astype(f32)
    return grp, cnt


def _pool_select(grp, s2, s4, s8, s16):
    return jnp.where(grp == 0, s2, jnp.where(grp == 1, s4, jnp.where(grp == 2, s8, s16)))


def _pooled(i, cur, halo):
    xx = jnp.concatenate([halo, cur], axis=0)
    s2 = xx + pltpu.roll(xx, 1, 0)
    s4 = s2 + pltpu.roll(s2, 2, 0)
    s8 = s4 + pltpu.roll(s4, 4, 0)
    s16 = s8 + pltpu.roll(s8, 8, 0)
    grp, cnt = _pool_consts(i, TM)
    tot = _pool_select(grp, s2[HALO:], s4[HALO:], s8[HALO:], s16[HALO:])
    return tot / cnt - cur


def _pool_fwd(u, wp, scale):
    def body(u_ref, halo_ref, wp_ref, sc_ref, y_ref):
        i = pl.program_id(0)
        halo = halo_ref[...] * (i > 0).astype(f32)
        pooled = _pooled(i, u_ref[...], halo)
        y_ref[...] = _dot(pooled.astype(bf16), wp_ref[...]) * sc_ref[...]

    return _pc(
        body, grid=(S // TM,), name="pool_fwd",
        in_specs=[pl.BlockSpec((TM, 256), lambda i: (i, 0)),
                  pl.BlockSpec((HALO, 256), lambda i: (jnp.maximum(i * (TM // HALO) - 1, 0), 0)),
                  pl.BlockSpec((256, 256), lambda i: (0, 0)),
                  pl.BlockSpec((1, 256), lambda i: (0, 0))],
        out_specs=pl.BlockSpec((TM, 256), lambda i: (i, 0)), out_shape=SDS((S, 256), f32), compiler_params=_cp(32),
    )(u, u, wp, scale)


def _pool_bwd(u, wp, scale, dy):
    nt = S // TM

    def body(u_ref, halo_ref, wp_ref, sc_ref, dy_ref, dyn_ref, du_ref, dwp_ref, dsc_ref):
        i = pl.program_id(0)

        @pl.when(i == 0)
        def _():
            dwp_ref[...] = jnp.zeros_like(dwp_ref)
            dsc_ref[...] = jnp.zeros_like(dsc_ref)

        halo = halo_ref[...] * (i > 0).astype(f32)
        pooled = _pooled(i, u_ref[...], halo).astype(bf16)
        dyv = dy_ref[...]
        dsc_ref[...] += jnp.sum(dyv * _dot(pooled, wp_ref[...]), axis=0, keepdims=True)
        dys = (dyv * sc_ref[...]).astype(bf16)
        dwp_ref[...] += _dot_tn(pooled, dys)
        dpool = _dot_nt(dys, wp_ref[...])
        grp, cnt = _pool_consts(i, TM)
        dyn = ((dyn_ref[...] * (i < nt - 1).astype(f32)) * sc_ref[...]).astype(bf16)
        _, cntn = _pool_consts(i + 1, HALO)
        zn = _dot_nt(dyn, wp_ref[...]) / cntn
        zz = jnp.concatenate([dpool / cnt, zn], axis=0)
        n = TM + HALO
        a2 = zz + pltpu.roll(zz, n - 1, 0)
        a4 = a2 + pltpu.roll(a2, n - 2, 0)
        a8 = a4 + pltpu.roll(a4, n - 4, 0)
        a16 = a8 + pltpu.roll(a8, n - 8, 0)
        du_ref[...] = _pool_select(grp, a2[:TM], a4[:TM], a8[:TM], a16[:TM]) - dpool

    return _pc(
        body, grid=(nt,), name="pool_bwd",
        in_specs=[pl.BlockSpec((TM, 256), lambda i: (i, 0)),
                  pl.BlockSpec((HALO, 256), lambda i: (jnp.maximum(i * (TM // HALO) - 1, 0), 0)),
                  pl.BlockSpec((256, 256), lambda i: (0, 0)),
                  pl.BlockSpec((1, 256), lambda i: (0, 0)),
                  pl.BlockSpec((TM, 256), lambda i: (i, 0)),
                  pl.BlockSpec((HALO, 256), lambda i: (jnp.minimum((i + 1) * (TM // HALO), S // HALO - 1), 0))],
        out_specs=[pl.BlockSpec((TM, 256), lambda i: (i, 0)), pl.BlockSpec((256, 256), lambda i: (0, 0)),
                   pl.BlockSpec((1, 256), lambda i: (0, 0))],
        out_shape=[SDS((S, 256), f32), SDS((256, 256), f32), SDS((1, 256), f32)], compiler_params=_cp(32),
    )(u, u, wp, scale, dy, dy)


CW = 3 * DNW
CHALO = 8
TC = 256


def _conv_fwd(u, w):
    def body(u_ref, halo_ref, w_ref, c_ref):
        i = pl.program_id(0)
        xx = jnp.concatenate([halo_ref[...] * (i > 0).astype(f32), u_ref[...]], axis=0)
        c = (w_ref[3:4, :] * xx + w_ref[2:3, :] * pltpu.roll(xx, 1, 0) + w_ref[1:2, :] * pltpu.roll(xx, 2, 0)
             + w_ref[0:1, :] * pltpu.roll(xx, 3, 0))
        c_ref[...] = c[CHALO:]

    return _pc(
        body, grid=(S // TC,), name="conv_fwd",
        in_specs=[pl.BlockSpec((TC, CW), lambda i: (i, 0)),
                  pl.BlockSpec((CHALO, CW), lambda i: (jnp.maximum(i * (TC // CHALO) - 1, 0), 0)),
                  pl.BlockSpec((8, CW), lambda i: (0, 0))],
        out_specs=pl.BlockSpec((TC, CW), lambda i: (i, 0)), out_shape=SDS((S, CW), f32), compiler_params=_cp(32),
    )(u, u, w)


def _conv_bwd(u, w, dc):
    nt = S // TC

    def body(u_ref, halo_ref, w_ref, dc_ref, dcn_ref, du_ref, dw_ref):
        i = pl.program_id(0)

        @pl.when(i == 0)
        def _():
            dw_ref[...] = jnp.zeros_like(dw_ref)

        dcv = dc_ref[...]
        zz = jnp.concatenate([dcv, dcn_ref[...] * (i < nt - 1).astype(f32)], axis=0)
        n = TC + CHALO
        du = (w_ref[3:4, :] * zz + w_ref[2:3, :] * pltpu.roll(zz, n - 1, 0) + w_ref[1:2, :] * pltpu.roll(zz, n - 2, 0)
              + w_ref[0:1, :] * pltpu.roll(zz, n - 3, 0))
        du_ref[...] = du[:TC]
        xx = jnp.concatenate([halo_ref[...] * (i > 0).astype(f32), u_ref[...]], axis=0)
        for j in range(4):
            shifted = xx if j == 3 else pltpu.roll(xx, 3 - j, 0)
            dw_ref[j:j + 1, :] += jnp.sum(dcv * shifted[CHALO:], axis=0, keepdims=True)

    return _pc(
        body, grid=(nt,), name="conv_bwd",
        in_specs=[pl.BlockSpec((TC, CW), lambda i: (i, 0)),
                  pl.BlockSpec((CHALO, CW), lambda i: (jnp.maximum(i * (TC // CHALO) - 1, 0), 0)),
                  pl.BlockSpec((8, CW), lambda i: (0, 0)),
                  pl.BlockSpec((TC, CW), lambda i: (i, 0)),
                  pl.BlockSpec((CHALO, CW), lambda i: (jnp.minimum((i + 1) * (TC // CHALO), S // CHALO - 1), 0))],
        out_specs=[pl.BlockSpec((TC, CW), lambda i: (i, 0)), pl.BlockSpec((8, CW), lambda i: (0, 0))],
        out_shape=[SDS((S, CW), f32), SDS((8, CW), f32)], compiler_params=_cp(32),
    )(u, u, w, dc, dc)


TL = 512
NCL = TL // CH


def _bdot(a, b):
    return jnp.einsum('nik,nkj->nij', a.astype(bf16), b.astype(bf16), preferred_element_type=f32)


def _bdot_nt(a, b):
    return jnp.einsum('nik,njk->nij', a.astype(bf16), b.astype(bf16), preferred_element_type=f32)


def _bdot_tn(a, b):
    return jnp.einsum('nki,nkj->nij', a.astype(bf16), b.astype(bf16), preferred_element_type=f32)


@jax.custom_vjp
def _inv_unit_lower(a):
    ii = lax.broadcasted_iota(jnp.int32, (1, CH, CH), 1)
    jj = lax.broadcasted_iota(jnp.int32, (1, CH, CH), 2)
    t = (ii == jj).astype(f32) - a
    p = a
    for _ in range(5):
        p = _bdot(p, p)
        t = t + _bdot(t, p)
    return t


def _inv_unit_lower_fwd(a):
    t = _inv_unit_lower(a)
    return t, t


def _inv_unit_lower_bwd(t, dt):
    return (-_bdot_tn(t, _bdot_nt(dt, t)),)


_inv_unit_lower.defvjp(_inv_unit_lower_fwd, _inv_unit_lower_bwd)


def _dn_local(c, dba, a_row, b_row):
    act = c * jax.nn.sigmoid(c)
    lane = lax.broadcasted_iota(jnp.int32, (TL, 128), 1)
    beta_all = jax.nn.sigmoid(dba)
    xs = dba + b_row
    softplus = jnp.maximum(xs, 0.0) + jnp.log(1.0 + jnp.exp(-jnp.abs(xs)))
    g_all = -jnp.exp(a_row) * softplus
    ii = lax.broadcasted_iota(jnp.int32, (1, CH, CH), 1)
    jj = lax.broadcasted_iota(jnp.int32, (1, CH, CH), 2)
    lower = jj <= ii
    strict = jj < ii
    eye = (ii == jj).astype(f32)
    us, ws, qgs, kds, intras = [], [], [], [], []
    aux = jnp.zeros((TL, 128), f32)
    for h in range(4):
        q = act[:, DH * h:DH * (h + 1)]
        k = act[:, DNW + DH * h:DNW + DH * (h + 1)]
        v = act[:, 2 * DNW + DH * h:2 * DNW + DH * (h + 1)]
        q = q * lax.rsqrt(jnp.sum(q * q, axis=-1, keepdims=True) + EPS) * (DH ** -0.5)
        k = k * lax.rsqrt(jnp.sum(k * k, axis=-1, keepdims=True) + EPS)
        beta = jnp.sum(jnp.where(lane == h, beta_all, 0.0), axis=1, keepdims=True)
        g = jnp.sum(jnp.where(lane == 4 + h, g_all, 0.0), axis=1, keepdims=True)
        q3, k3, v3 = q.reshape(NCL, CH, DH), k.reshape(NCL, CH, DH), v.reshape(NCL, CH, DH)
        beta3, g3 = beta.reshape(NCL, CH, 1), g.reshape(NCL, CH, 1)
        g_row = jnp.sum(eye * g3, axis=1, keepdims=True)
        gc_col = jnp.sum(jnp.where(lower, g_row, 0.0), axis=2, keepdims=True)
        gc_row = jnp.sum(jnp.where(ii <= jj, g3, 0.0), axis=1, keepdims=True)
        diff = gc_col - gc_row
        decay = jnp.where(lower, jnp.exp(jnp.where(lower, diff, 0.0)), 0.0)
        kb = k3 * beta3
        vb = v3 * beta3
        a = jnp.where(strict, _bdot_nt(kb, k3) * decay, 0.0)
        t = _inv_unit_lower(a)
        u3 = _bdot(t, vb)
        w3 = _bdot(t, kb * jnp.exp(gc_col))
        intra = jnp.where(lower, _bdot_nt(q3, k3) * decay, 0.0)
        g_last = jnp.sum(g3, axis=1, keepdims=True)
        us.append(u3.reshape(TL, DH))
        ws.append(w3.reshape(TL, DH))
        qgs.append((q3 * jnp.exp(gc_col)).reshape(TL, DH))
        kds.append((k3 * jnp.exp(g_last - gc_col)).reshape(TL, DH))
        intras.append(intra.reshape(TL, CH))
        e_last = jnp.broadcast_to(jnp.exp(g_last), (NCL, CH, 1)).reshape(TL, 1)
        aux = aux + jnp.where(lane == h, e_last, 0.0)
    cat = lambda xs: jnp.concatenate(xs, axis=1)
    return cat(us), cat(ws), cat(qgs), cat(kds), jnp.stack(intras, axis=0), aux


def _dn_local_fwd(c, dba, par):
    def body(c_ref, dba_ref, par_ref, u_ref, w_ref, qg_ref, kd_ref, in_ref, aux_ref):
        u, w, qg, kd, intra, aux = _dn_local(c_ref[...], dba_ref[...], par_ref[0:1, :], par_ref[1:2, :])
        u_ref[...] = u
        w_ref[...] = w
        qg_ref[...] = qg
        kd_ref[...] = kd
        in_ref[...] = intra
        aux_ref[...] = aux

    wide = pl.BlockSpec((TL, DNW), lambda i: (i, 0))
    return _pc(
        body, grid=(S // TL,), name="dn_local_fwd",
        in_specs=[pl.BlockSpec((TL, CW), lambda i: (i, 0)), pl.BlockSpec((TL, 128), lambda i: (i, 0)),
                  pl.BlockSpec((8, 128), lambda i: (0, 0))],
        out_specs=[wide, wide, wide, wide, pl.BlockSpec((4, TL, CH), lambda i: (0, i, 0)),
                   pl.BlockSpec((TL, 128), lambda i: (i, 0))],
        out_shape=[SDS((S, DNW), f32)] * 4 + [SDS((4, S, CH), f32), SDS((S, 128), f32)], compiler_params=_cp(48),
    )(c, dba, par)


def _dn_local_bwd(c, dba, par, du, dw, dqg, dkd, dintra, daux):
    def body(c_ref, dba_ref, par_ref, du_ref, dw_ref, dqg_ref, dkd_ref, din_ref, daux_ref, dc_ref, ddba_ref, dpar_ref):
        @pl.when(pl.program_id(0) == 0)
        def _():
            dpar_ref[...] = jnp.zeros_like(dpar_ref)

        _, vjp = jax.vjp(_dn_local, c_ref[...], dba_ref[...], par_ref[0:1, :], par_ref[1:2, :])
        dc, ddba, da_row, db_row = vjp((du_ref[...], dw_ref[...], dqg_ref[...], dkd_ref[...], din_ref[...], daux_ref[...]))
        dc_ref[...] = dc
        ddba_ref[...] = ddba
        dpar_ref[0:1, :] += da_row
        dpar_ref[1:2, :] += db_row

    wide = pl.BlockSpec((TL, DNW), lambda i: (i, 0))
    return _pc(
        body, grid=(S // TL,), name="dn_local_bwd",
        in_specs=[pl.BlockSpec((TL, CW), lambda i: (i, 0)), pl.BlockSpec((TL, 128), lambda i: (i, 0)),
                  pl.BlockSpec((8, 128), lambda i: (0, 0)), wide, wide, wide, wide,
                  pl.BlockSpec((4, TL, CH), lambda i: (0, i, 0)), pl.BlockSpec((TL, 128), lambda i: (i, 0))],
        out_specs=[pl.BlockSpec((TL, CW), lambda i: (i, 0)), pl.BlockSpec((TL, 128), lambda i: (i, 0)),
                   pl.BlockSpec((8, 128), lambda i: (0, 0))],
        out_shape=[SDS((S, CW), f32), SDS((S, 128), f32), SDS((8, 128), f32)], compiler_params=_cp(56),
    )(c, dba, par, du, dw, dqg, dkd, dintra, daux)


def _dn_step(state, u, w, qg, kd, intra, aux):
    lane = lax.broadcasted_iota(jnp.int32, (CH, 128), 1)
    row = lax.broadcasted_iota(jnp.int32, (CH, 128), 0)
    outs, states = [], []
    for h in range(4):
        sl = slice(DH * h, DH * (h + 1))
        st = state[h]
        e = jnp.sum(jnp.sum(jnp.where((lane == h) & (row == 0), aux, 0.0), axis=1, keepdims=True), axis=0, keepdims=True)
        v_new = u[:, sl] - _dot(w[:, sl].astype(bf16), st.astype(bf16))
        vb = v_new.astype(bf16)
        outs.append(_dot(qg[:, sl].astype(bf16), st.astype(bf16)) + _dot(intra[h].astype(bf16), vb))
        states.append(st * e + _dot_tn(kd[:, sl].astype(bf16), vb))
    return jnp.concatenate(outs, axis=1), jnp.stack(states, axis=0)


CPS = 4
NSTEP = NCHUNK // CPS


def _dn_rec_specs(index):
    wide = pl.BlockSpec((CPS * CH, DNW), lambda n: (index(n), 0))
    inb = pl.BlockSpec((4, CPS * CH, CH), lambda n: (0, index(n), 0))
    auxb = pl.BlockSpec((CPS * CH, 128), lambda n: (index(n), 0))
    stb = pl.BlockSpec((CPS, 4, DH, DH), lambda n: (index(n), 0, 0, 0))
    return wide, inb, auxb, stb


def _dn_rec_fwd(u, w, qg, kd, intra, aux):
    def body(u_ref, w_ref, qg_ref, kd_ref, in_ref, aux_ref, o_ref, st_ref, st_scr):
        @pl.when(pl.program_id(0) == 0)
        def _():
            st_scr[...] = jnp.zeros_like(st_scr)

        st = st_scr[...]
        for k in range(CPS):
            rows = slice(CH * k, CH * (k + 1))
            st_ref[k] = st
            o, st = _dn_step(st, u_ref[rows, :], w_ref[rows, :], qg_ref[rows, :], kd_ref[rows, :], in_ref[:, rows, :],
                             aux_ref[rows, :])
            o_ref[rows, :] = o
        st_scr[...] = st

    wide, inb, auxb, stb = _dn_rec_specs(lambda n: n)
    return _pc(
        body, grid=(NSTEP,), name="dn_rec_fwd", in_specs=[wide, wide, wide, wide, inb, auxb], out_specs=[wide, stb],
        out_shape=[SDS((S, DNW), f32), SDS((NCHUNK, 4, DH, DH), f32)],
        scratch_shapes=[pltpu.VMEM((4, DH, DH), f32)], compiler_params=_cp(32),
    )(u, w, qg, kd, intra, aux)


def _dn_rec_bwd(u, w, qg, kd, intra, aux, states, do):
    def body(u_ref, w_ref, qg_ref, kd_ref, in_ref, aux_ref, st_ref, do_ref,
             du_ref, dw_ref, dqg_ref, dkd_ref, din_ref, daux_ref, ds_scr):
        @pl.when(pl.program_id(0) == 0)
        def _():
            ds_scr[...] = jnp.zeros_like(ds_scr)

        ds = ds_scr[...]
        for k in reversed(range(CPS)):
            rows = slice(CH * k, CH * (k + 1))
            _, vjp = jax.vjp(_dn_step, st_ref[k], u_ref[rows, :], w_ref[rows, :], qg_ref[rows, :], kd_ref[rows, :],
                             in_ref[:, rows, :], aux_ref[rows, :])
            ds, du, dw, dqg, dkd, din, daux = vjp((do_ref[rows, :], ds))
            du_ref[rows, :] = du
            dw_ref[rows, :] = dw
            dqg_ref[rows, :] = dqg
            dkd_ref[rows, :] = dkd
            din_ref[:, rows, :] = din
            daux_ref[rows, :] = daux
        ds_scr[...] = ds

    wide, inb, auxb, stb = _dn_rec_specs(lambda n: NSTEP - 1 - n)
    return _pc(
        body, grid=(NSTEP,), name="dn_rec_bwd", in_specs=[wide, wide, wide, wide, inb, auxb, stb, wide],
        out_specs=[wide, wide, wide, wide, inb, auxb],
        out_shape=[SDS((S, DNW), f32)] * 4 + [SDS((4, S, CH), f32), SDS((S, 128), f32)],
        scratch_shapes=[pltpu.VMEM((4, DH, DH), f32)], compiler_params=_cp(40),
    )(u, w, qg, kd, intra, aux, states, do)


def _dn_post(o, z, nw):
    parts = []
    for h in range(4):
        sl = slice(DH * h, DH * (h + 1))
        oh = o[:, sl]
        y = oh * lax.rsqrt(jnp.mean(oh * oh, axis=-1, keepdims=True) + EPS) * nw
        zh = z[:, sl]
        parts.append(y * (zh * jax.nn.sigmoid(zh)))
    return jnp.concatenate(parts, axis=1)


def _dn_post_fwd(o, z, nw):
    def body(o_ref, z_ref, nw_ref, y_ref):
        y_ref[...] = _dn_post(o_ref[...], z_ref[...], nw_ref[...])

    wide = pl.BlockSpec((TM, DNW), lambda i: (i, 0))
    return _pc(body, grid=(S // TM,), name="dn_post_fwd",
                          in_specs=[wide, wide, pl.BlockSpec((1, 128), lambda i: (0, 0))], out_specs=wide,
                          out_shape=SDS((S, DNW), f32), compiler_params=_cp(32))(o, z, nw)


def _dn_post_bwd(o, z, nw, dy):
    def body(o_ref, z_ref, nw_ref, dy_ref, do_ref, dz_ref, dnw_ref):
        @pl.when(pl.program_id(0) == 0)
        def _():
            dnw_ref[...] = jnp.zeros_like(dnw_ref)

        _, vjp = jax.vjp(_dn_post, o_ref[...], z_ref[...], nw_ref[...])
        do, dz, dnw = vjp(dy_ref[...])
        do_ref[...] = do
        dz_ref[...] = dz
        dnw_ref[...] += dnw

    wide = pl.BlockSpec((TM, DNW), lambda i: (i, 0))
    one = pl.BlockSpec((1, 128), lambda i: (0, 0))
    return _pc(body, grid=(S // TM,), name="dn_post_bwd", in_specs=[wide, wide, one, wide],
                          out_specs=[wide, wide, one], out_shape=[SDS((S, DNW), f32), SDS((S, DNW), f32), SDS((1, 128), f32)],
                          compiler_params=_cp(32))(o, z, nw, dy)


def _row_tile(rows, width, itemsize=4, target=2 * 1024 * 1024):
    best = None
    for t in range(16, rows + 1, 16):
        if rows % t == 0 and t * width * itemsize <= target:
            best = t
    return best if best is not None else rows


def _sum_pieces(pieces, out_dtype, name):
    n, rows, width = pieces.shape
    tr = _row_tile(rows, width * n)

    def body(p_ref, o_ref):
        acc = p_ref[0].astype(f32)
        for s in range(1, n):
            acc = acc + p_ref[s].astype(f32)
        o_ref[...] = acc.astype(out_dtype)

    return _pc(body, grid=(rows // tr,), name=name,
                          in_specs=[pl.BlockSpec((n, tr, width), lambda i: (0, i, 0))],
                          out_specs=pl.BlockSpec((tr, width), lambda i: (i, 0)),
                          out_shape=SDS((rows, width), out_dtype), compiler_params=_cp(32))(pieces)


def _sum_core_pair(part, got, c_arr):
    n, rows, width = part.shape
    half = rows // 2
    tr = _row_tile(half, width, itemsize=2)
    nt = half // tr

    def body(c_ref, p_ref, g_ref, o_ref):
        o_ref[...] = (p_ref[...].astype(f32) + g_ref[...].astype(f32)).astype(bf16)

    gs = pltpu.PrefetchScalarGridSpec(
        num_scalar_prefetch=1, grid=(n, nt),
        in_specs=[pl.BlockSpec((1, tr, width), lambda j, i, c: (j, c[0] * nt + i, 0)),
                  pl.BlockSpec((1, tr, width), lambda j, i, c: (j, i, 0))],
        out_specs=pl.BlockSpec((1, tr, width), lambda j, i, c: (j, i, 0)))
    return _pc(body, grid_spec=gs, name="sum_core_pair", out_shape=SDS((n, half, width), bf16),
                          compiler_params=_cp(32))(c_arr, part, got)


def _sum_chips(pieces, c_arr, full, row0, total_rows):
    n, half, width = pieces.shape
    tr = max(t for t in range(16, 257, 16) if half % t == 0 and row0 % t == 0)
    nt = half // tr

    def body(c_ref, p_ref, *rest):
        o_ref = rest[-1]
        acc = p_ref[0].astype(f32)
        for s in range(1, n):
            acc = acc + p_ref[s].astype(f32)
        o_ref[...] = acc

    gs = pltpu.PrefetchScalarGridSpec(
        num_scalar_prefetch=1, grid=(nt,),
        in_specs=[pl.BlockSpec((n, tr, width), lambda i, c: (0, i, 0))] + ([] if full is None else [ANY]),
        out_specs=pl.BlockSpec((tr, width), lambda i, c: (row0 // tr + c[0] * nt + i, 0)))
    args = (c_arr, pieces) if full is None else (c_arr, pieces, full)
    return _pc(body, grid_spec=gs, name="sum_chips", out_shape=SDS((total_rows, width), f32),
                          input_output_aliases={} if full is None else {2: 0}, compiler_params=_cp(32))(*args)


def _adamw_math(w, g, m, v):
    mn = ADAM_B1 * m + (1.0 - ADAM_B1) * g
    vn = ADAM_B2 * v + (1.0 - ADAM_B2) * (g * g)
    m_hat = mn / (1.0 - ADAM_B1 ** ADAM_STEP)
    v_hat = vn / (1.0 - ADAM_B2 ** ADAM_STEP)
    return -ADAM_LR * (m_hat / (jnp.sqrt(v_hat) + ADAM_EPS) + ADAM_WD * w), mn, vn


def _adamw(w, g, m, v, name):
    rows, width = w.shape
    tr = _row_tile(rows, width * 7, target=12 * 1024 * 1024)

    def body(w_ref, g_ref, m_ref, v_ref, d_ref, nm_ref, nv_ref):
        d_ref[...], nm_ref[...], nv_ref[...] = _adamw_math(w_ref[...], g_ref[...], m_ref[...], v_ref[...])

    blk = pl.BlockSpec((tr, width), lambda i: (i, 0))
    return _pc(body, grid=(rows // tr,), name=name, in_specs=[blk] * 4, out_specs=[blk] * 3,
                          out_shape=[SDS((rows, width), f32)] * 3, compiler_params=_cp(40))(w, g, m, v)


def _adamw_rows(w, m, v, gblob, tr, first_tile, name):
    layers, rows, width = w.shape

    def body(w_ref, g_ref, m_ref, v_ref, d_ref, nm_ref, nv_ref):
        d_ref[0], nm_ref[0], nv_ref[0] = _adamw_math(w_ref[0], g_ref[...], m_ref[0], v_ref[0])

    blk = pl.BlockSpec((1, tr, width), lambda l, i: (l, i, 0))
    gblk = pl.BlockSpec((tr, width), lambda l, i: (first_tile(l) + i, 0))
    return _pc(body, grid=(layers, rows // tr), name=name, in_specs=[blk, gblk, blk, blk], out_specs=[blk] * 3,
                          out_shape=[SDS(w.shape, f32)] * 3, compiler_params=_cp(40))(w, gblob, m, v)


ANY = pl.BlockSpec(memory_space=pl.ANY)


def _place():
    x, y, c = lax.axis_index("x"), lax.axis_index("y"), lax.axis_index("c")
    chips = [(1 - x, y), (x, 1 - y), (1 - x, 1 - y)]
    return x, y, c, chips


NQ_ICI = 4
NQ_D2D = 8


def _chunks(rows, want):
    n = max(k for k in range(1, want + 1) if rows % k == 0 and (rows // k) % 16 == 0)
    step = rows // n
    return [(q * step, step) for q in range(n)]


def _scatter_copies(ins, outs, ssem, rsem, lsem):
    x, y, c, chips = _place()
    me = (x, y, c)
    locals_, sends, lands = [], [], []
    for b in range(len(ins)):
        for q, (off, n) in enumerate(_chunks(ins[b].shape[1], NQ_ICI)):
            rows = pl.ds(off, n)
            mine = outs[b].at[2 * x + y, rows, :]
            locals_.append(pltpu.make_async_copy(ins[b].at[2 * x + y, rows, :], mine, lsem.at[b, q]))
            for j, chip in enumerate(chips):
                sends.append(_remote(ins[b].at[2 * chip[0] + chip[1], rows, :], mine, ssem.at[b, j, q], rsem.at[b, j, q],
                                     (*chip, c)))
                slot = outs[b].at[2 * chip[0] + chip[1], rows, :]
                lands.append(_remote(slot, slot, ssem.at[b, j, q], rsem.at[b, j, q], me))
    return locals_, sends, lands


def _scatter_start(ins, outs, ssem, rsem, lsem):
    locals_, sends, _ = _scatter_copies(ins, outs, ssem, rsem, lsem)
    for cp in locals_ + sends:
        cp.start()


def _scatter_finish(ins, outs, ssem, rsem, lsem):
    locals_, sends, lands = _scatter_copies(ins, outs, ssem, rsem, lsem)
    for cp in lands:
        cp.wait_recv()
    for cp in sends:
        cp.wait_send()
    for cp in locals_:
        cp.wait()


def _scatter_sems(nb):
    return [pltpu.SemaphoreType.DMA((nb, 3, NQ_ICI)), pltpu.SemaphoreType.DMA((nb, 3, NQ_ICI)),
            pltpu.SemaphoreType.DMA((nb, NQ_ICI))]


def _remote(src, dst, ssem, rsem, dev):
    return pltpu.make_async_remote_copy(src_ref=src, dst_ref=dst, send_sem=ssem, recv_sem=rsem, device_id=dev,
                                        device_id_type=MESH)


def _all_gather_weights(shards):
    nb = len(shards)

    def body(*refs):
        ins, outs, sems = refs[:nb], refs[nb:2 * nb], refs[2 * nb:]
        _gather_start(ins, outs, *sems)
        _gather_finish(ins, outs, *sems)

    return _pc(
        body, name="all_gather_weights", in_specs=[ANY] * nb, out_specs=[ANY] * nb,
        out_shape=[SDS((NCH,) + s.shape, s.dtype) for s in shards], scratch_shapes=_gather_sems(nb),
    )(*shards)


def _gather_first(ins, outs, ssem, rsem, lsem):
    x, y, c, chips = _place()
    locals_, sends = [], []
    for b in range(len(ins)):
        half = ins[b].shape[0] // 2
        for q, (off, n) in enumerate(_chunks(half, NQ_ICI)):
            mine = pl.ds(c * half + off, n)
            own = outs[b].at[2 * x + y, mine, :]
            locals_.append(pltpu.make_async_copy(ins[b].at[mine, :], own, lsem.at[b, q]))
            sends.append(_remote(ins[b].at[mine, :], own, ssem.at[b, 0, q], rsem.at[b, 0, q], (x, y, 1 - c)))
            sends += [_remote(ins[b].at[mine, :], own, ssem.at[b, 1 + j, q], rsem.at[b, 1 + j, q], (*chip, c))
                      for j, chip in enumerate(chips)]
    return locals_, sends


def _gather_start(ins, outs, ssem, rsem, lsem):
    locals_, sends = _gather_first(ins, outs, ssem, rsem, lsem)
    for cp in locals_ + sends:
        cp.start()


def _gather_finish(ins, outs, ssem, rsem, lsem):
    x, y, c, chips = _place()
    me, sib = (x, y, c), (x, y, 1 - c)
    locals_, sends = _gather_first(ins, outs, ssem, rsem, lsem)
    for b in range(len(ins)):
        half = ins[b].shape[0] // 2
        for q, (off, n) in enumerate(_chunks(half, NQ_ICI)):
            mine = pl.ds(c * half + off, n)
            for j, chip in enumerate(chips):
                landed = outs[b].at[2 * chip[0] + chip[1], mine, :]
                _remote(landed, landed, ssem.at[b, 1 + j, q], rsem.at[b, 1 + j, q], me).wait_recv()
                cp = _remote(landed, landed, ssem.at[b, 4 + j, q], rsem.at[b, 4 + j, q], sib)
                cp.start()
                sends.append(cp)
    for b in range(len(ins)):
        half = ins[b].shape[0] // 2
        for q, (off, n) in enumerate(_chunks(half, NQ_ICI)):
            other = pl.ds((1 - c) * half + off, n)
            theirs = outs[b].at[2 * x + y, other, :]
            _remote(theirs, theirs, ssem.at[b, 0, q], rsem.at[b, 0, q], me).wait_recv()
            for j, chip in enumerate(chips):
                fwd = outs[b].at[2 * chip[0] + chip[1], other, :]
                _remote(fwd, fwd, ssem.at[b, 4 + j, q], rsem.at[b, 4 + j, q], me).wait_recv()
    for cp in sends:
        cp.wait_send()
    for cp in locals_:
        cp.wait()


def _gather_sems(nb):
    return [pltpu.SemaphoreType.DMA((nb, 7, NQ_ICI)), pltpu.SemaphoreType.DMA((nb, 7, NQ_ICI)),
            pltpu.SemaphoreType.DMA((nb, NQ_ICI))]


def _send_sibling_half(parts):
    nb = len(parts)

    def body(*refs):
        ins, gots = refs[:nb], refs[nb:2 * nb]
        ssem, rsem = refs[2 * nb:]
        x, y, c, _ = _place()
        sib = (x, y, 1 - c)
        todo = []
        for b in range(nb):
            half = ins[b].shape[1] // 2
            for q, (off, n) in enumerate(_chunks(half, NQ_D2D)):
                cp = _remote(ins[b].at[:, pl.ds((1 - c) * half + off, n), :], gots[b].at[:, pl.ds(off, n), :],
                             ssem.at[b, q], rsem.at[b, q], sib)
                cp.start()
                todo.append(cp)
        for cp in todo:
            cp.wait()

    return _pc(
        body, name="send_sibling_half", in_specs=[ANY] * nb, out_specs=[ANY] * nb,
        out_shape=[SDS((p.shape[0], p.shape[1] // 2, p.shape[2]), p.dtype) for p in parts],
        scratch_shapes=[pltpu.SemaphoreType.DMA((nb, NQ_D2D)), pltpu.SemaphoreType.DMA((nb, NQ_D2D))],
    )(*parts)


def _scatter_to_chips(parts):
    nb = len(parts)

    def body(*refs):
        ins, outs, sems = refs[:nb], refs[nb:2 * nb], refs[2 * nb:]
        _scatter_start(ins, outs, *sems)
        _scatter_finish(ins, outs, *sems)

    return _pc(
        body, name="scatter_to_chips", in_specs=[ANY] * nb, out_specs=[ANY] * nb,
        out_shape=[SDS(p.shape, p.dtype) for p in parts], scratch_shapes=_scatter_sems(nb),
    )(*parts)


def _join_halves(fulls, ranges):
    nb = len(fulls)
    nr = max(len(r) for r in ranges)

    def body(*refs):
        ins, outs = refs[:nb], refs[nb:2 * nb]
        ssem, rsem = refs[2 * nb:]
        x, y, c, _ = _place()
        sib = (x, y, 1 - c)
        sends, lands = [], []
        for b in range(nb):
            for g, (row0, rows) in enumerate(ranges[b]):
                half = rows // 2
                for q, (off, n) in enumerate(_chunks(half, NQ_D2D)):
                    mine = pl.ds(row0 + c * half + off, n)
                    sends.append(_remote(ins[b].at[mine, :], outs[b].at[mine, :], ssem.at[b, g, q], rsem.at[b, g, q], sib))
                    other = outs[b].at[pl.ds(row0 + (1 - c) * half + off, n), :]
                    lands.append(_remote(other, other, ssem.at[b, g, q], rsem.at[b, g, q], sib))
        for cp in sends:
            cp.start()
        for cp in lands:
            cp.wait_recv()
        for cp in sends:
            cp.wait_send()

    return _pc(
        body, name="join_halves", in_specs=[ANY] * nb, out_specs=[ANY] * nb,
        out_shape=[SDS(h.shape, h.dtype) for h in fulls], input_output_aliases={b: b for b in range(nb)},
        scratch_shapes=[pltpu.SemaphoreType.DMA((nb, nr, NQ_D2D)), pltpu.SemaphoreType.DMA((nb, nr, NQ_D2D))],
    )(*fulls)


def _gather_small(vec):
    def body(v_ref, o_ref, ssem, rsem, lsem):
        x, y, c, _ = _place()
        mine = o_ref.at[4 * x + 2 * y + c]
        local = pltpu.make_async_copy(v_ref, mine, lsem)
        local.start()
        sends = []
        for k in range(1, 8):
            peer = (x ^ (k >> 2), y ^ ((k >> 1) & 1), c ^ (k & 1))
            cp = _remote(v_ref, mine, ssem.at[k - 1], rsem.at[k - 1], peer)
            cp.start()
            sends.append(cp)
        for k in range(1, 8):
            px, py, pc = x ^ (k >> 2), y ^ ((k >> 1) & 1), c ^ (k & 1)
            slot = o_ref.at[4 * px + 2 * py + pc]
            _remote(slot, slot, ssem.at[k - 1], rsem.at[k - 1], (x, y, c)).wait_recv()
        for cp in sends:
            cp.wait_send()
        local.wait()

    return _pc(
        body, name="gather_small", in_specs=[ANY], out_specs=ANY, out_shape=SDS((8,) + vec.shape, vec.dtype),
        scratch_shapes=[pltpu.SemaphoreType.DMA((7,)), pltpu.SemaphoreType.DMA((7,)), pltpu.SemaphoreType.DMA],
    )(vec)


def _block_diag(pw):
    return jnp.concatenate([jnp.pad(pw[g], ((0, 0), (64 * g, 192 - 64 * g))) for g in range(4)], axis=0)


def _own_columns(full, chip):
    n = full.shape[-1] // NCH
    parts = full.reshape(full.shape[:-1] + (NCH, n))
    sel = (lax.broadcasted_iota(jnp.int32, (NCH, 1), 0) == chip)
    return jnp.sum(jnp.where(sel, parts, 0.0), axis=-2)


def _at_own_columns(shard, chip):
    n = shard.shape[-1]
    sel = (lax.broadcasted_iota(jnp.int32, (NCH * n,), 0) // n == chip)
    return jnp.where(sel, jnp.tile(shard, NCH), 0.0)


def _pad_rows(a, rows):
    return jnp.pad(a, ((0, rows - a.shape[0]),) + ((0, 0),) * (a.ndim - 1))


def _ffn_block(l, which):
    return 7 * l + 3 * which


def _wout_block(l):
    return 7 * l + 6


class _Weights:
    def __init__(self):
        self.ffn, self.wout, self.w_aug, self.rides = {}, {}, {}, {}

    @classmethod
    def from_blob(cls, blob, w_aug):
        self = cls()
        for l in range(DEPTH):
            self.ffn[(l, 0)], self.ffn[(l, 1)] = (blob, _ffn_block(l, 0)), (blob, _ffn_block(l, 1))
            self.wout[l], self.w_aug[l] = (blob, _wout_block(l)), w_aug[l]
        return self

    def set_w_in(self, l, gathered):
        self.w_aug[l] = jnp.pad(gathered.transpose(1, 0, 2).reshape(D, INW), ((0, 0), (0, INP - INW)))

    def ffn_fwd(self, l, which, x, nw):
        arr, k0 = self.ffn[(l, which)]
        if (l, which) not in self.rides:
            return _ffn_fwd(x, nw, arr, k0)
        shards, landed = self.rides[(l, which)]
        out, *gathered = _ffn_fwd(x, nw, arr, k0, shards)
        landed(gathered)
        return out


def _layer_fwd(l, x0, pos, freq, wts, ws):
    sv = {"x0": x0}
    x1 = ws.ffn_fwd(l, 0, x0, wts["ffn1_norm"][l:l + 1])
    att, att4, att16, pu, dq, dz, dba = _inproj_fwd(x1, wts["mix_norm"][l:l + 1], ws.w_aug[l], pos, freq)
    qkvs = [att, att4.reshape(S, 768), att16.reshape(S, 768)]
    (o1, l1), (o4, l4), (o16, l16) = [_attn_fwd(q, NB // d) for q, d in zip(qkvs, PATTERN_DIL)]
    ols = (o1, l1, o4.reshape(4, S // 4, ATT), l4.reshape(4, S // 4, ATT), o16.reshape(16, S // 16, ATT),
           l16.reshape(16, S // 16, ATT))
    ya = _merge_fwd(*ols)
    yb = _pool_fwd(pu, wts["pool_bd"][l], wts["pool_scale"][l:l + 1])
    c = _conv_fwd(dq, wts["conv_w"][l])
    u, w, qg, kd, intra, aux = _dn_local_fwd(c, dba, wts["dn_par"][l])
    o_dn, states = _dn_rec_fwd(u, w, qg, kd, intra, aux)
    yc = _dn_post_fwd(o_dn, dz, wts["dn_out_norm"][l:l + 1])
    x2 = _outproj_fwd(x1, ya, yb, yc, *ws.wout[l])
    x3 = ws.ffn_fwd(l, 1, x2, wts["ffn2_norm"][l:l + 1])
    sv.update(x1=x1, x2=x2, qkvs=qkvs, ols=ols, ya=ya, yb=yb, yc=yc, pu=pu, dq=dq, dz=dz, dba=dba, c=c,
              u=u, w=w, qg=qg, kd=kd, intra=intra, aux=aux, states=states, o_dn=o_dn)
    return x3, sv


def _wout_part(g):
    return jnp.pad(g.astype(bf16).reshape(NCH, 256, D), ((0, 0), (0, FC - 256), (0, 0)))


def _win_part(g):
    return g[:, :INW].astype(bf16).reshape(D, NCH, INC).transpose(1, 0, 2)


def _layer_bwd(l, dx3, sv, pos, freq, wts, ws, ride=None, prep=None):
    gr = {}
    g2, u2, d2, dh4, *pieces_before = _ffn_bwd(sv["x2"], wts["ffn2_norm"][l:l + 1], *ws.ffn[(l, 1)], dx3, ride)
    gr.update(ffn2_w_gate=g2, ffn2_w_up=u2, ffn2_w_down=d2)
    dx2, gr["ffn2_norm"], dya, dyb, dyc, gr["w_out"] = _outproj_bwd(sv["x2"], wts["ffn2_norm"][l:l + 1], dx3, dh4, sv["ya"],
                                                                     sv["yb"], sv["yc"], *ws.wout[l])
    do_dn, ddz, gr["dn_out_norm"] = _dn_post_bwd(sv["o_dn"], sv["dz"], wts["dn_out_norm"][l:l + 1], dyc)
    du, dw, dqg, dkd, dintra, daux = _dn_rec_bwd(sv["u"], sv["w"], sv["qg"], sv["kd"], sv["intra"], sv["aux"], sv["states"], do_dn)
    dc, ddba, gr["dn_par"] = _dn_local_bwd(sv["c"], sv["dba"], wts["dn_par"][l], du, dw, dqg, dkd, dintra, daux)
    ddq, gr["conv_w"] = _conv_bwd(sv["dq"], wts["conv_w"][l], dc)
    dpu, gr["pool_bd"], gr["pool_scale"] = _pool_bwd(sv["pu"], wts["pool_bd"][l], wts["pool_scale"][l:l + 1], dyb)
    dols = _merge_bwd(*sv["ols"], dya)
    datts = [_attn_bwd(q, dols[2 * p].reshape(S, ATT), dols[2 * p + 1].reshape(S, ATT), NB // d)
             for p, (q, d) in enumerate(zip(sv["qkvs"], PATTERN_DIL))]
    dx1, gr["mix_norm"], gr["w_aug"] = _inproj_bwd(sv["x1"], wts["mix_norm"][l:l + 1], ws.w_aug[l], pos, freq, dx2,
                                                    datts[0], datts[1].reshape(4, S // 4, 768),
                                                    datts[2].reshape(16, S // 16, 768), dpu, ddq, ddz, ddba)
    own = None
    if prep is not None:
        own = prep([jnp.concatenate([g2, u2, d2, _wout_part(gr["w_out"])], axis=1), _win_part(gr["w_aug"])])
    g1, u1, d1, dh4, *pieces_own = _ffn_bwd(sv["x0"], wts["ffn1_norm"][l:l + 1], *ws.ffn[(l, 0)], dx1, own)
    dx0, gr["ffn1_norm"] = _norm_bwd(sv["x0"], wts["ffn1_norm"][l:l + 1], dx1, dh4)
    gr.update(ffn1_w_gate=g1, ffn1_w_up=u1, ffn1_w_down=d1)
    return dx0, gr, pieces_before, pieces_own


def _device_step(x, pos, target, wts, ws, prep=None):
    freq = jnp.tile(ROPE_THETA ** (-jnp.arange(0, EH, 2, dtype=f32) / EH), 2 * ATT // EH).reshape(1, ATT)
    saved = []
    h = x
    for l in range(DEPTH):
        h, sv = _layer_fwd(l, h, pos, freq, wts, ws)
        saved.append(sv)
    dh, g_final, loss = _final(h, wts["final_norm"], target)
    grads = [None] * DEPTH
    dh, grads[1], _, _ = _layer_bwd(1, dh, saved[1], pos, freq, wts, ws)
    sums1 = None
    if prep is not None:
        g = grads[1]
        ffn = [g[f"ffn{f}_w_{n}"] for f in (1, 2) for n in ("gate", "up", "down")]
        sums1 = prep([jnp.concatenate(ffn + [_wout_part(g["w_out"])], axis=1), _win_part(g["w_aug"])])
    dh, grads[0], pieces1, pieces0 = _layer_bwd(0, dh, saved[0], pos, freq, wts, ws, sums1, prep)
    return loss, dh, g_final, grads, pieces1, pieces0


_SMALL = (("ffn1_norm", (DEPTH, D)), ("mix_norm", (DEPTH, D)), ("pool_w", (DEPTH, 4, 64, 64)), ("pool_scale", (DEPTH, 256)),
          ("dn_conv_w", (DEPTH, 4, CW)), ("dn_a_log", (DEPTH, 4)), ("dn_dt_bias", (DEPTH, 4)), ("dn_out_norm", (DEPTH, 128)),
          ("ffn2_norm", (DEPTH, D)), ("final_norm", (D,)), ("loss", (1,)))


def _pack_small(vals):
    rows = []
    for name, shape in _SMALL:
        flat = vals[name].astype(f32).reshape(-1)
        rows.append(jnp.pad(flat, (0, _small_rows(shape) * 128 - flat.shape[0])).reshape(-1, 128))
    out = jnp.concatenate(rows, axis=0)
    return _pad_rows(out, -(-out.shape[0] // 16) * 16)


def _small_rows(shape):
    return -(-int(np.prod(shape)) // 1024) * 8


def _unpack_small(packed):
    vals, r = {}, 0
    for name, shape in _SMALL:
        size, n = int(np.prod(shape)), _small_rows(shape)
        vals[name] = packed[r:r + n].reshape(-1)[:size].reshape(shape)
        r += n
    return vals


def kernel(x, positions, ffn1_norm, ffn1_w_gate, ffn1_w_up, ffn1_w_down, mix_norm, w_in, pool_w, pool_scale, dn_conv_w, dn_a_log, dn_dt_bias, dn_out_norm, w_out, ffn2_norm, ffn2_w_gate, ffn2_w_up, ffn2_w_down, final_norm, loss_target, m_ffn1_norm, m_ffn1_w_gate, m_ffn1_w_up, m_ffn1_w_down, m_mix_norm, m_w_in, m_pool_w, m_pool_scale, m_dn_conv_w, m_dn_a_log, m_dn_dt_bias, m_dn_out_norm, m_w_out, m_ffn2_norm, m_ffn2_w_gate, m_ffn2_w_up, m_ffn2_w_down, m_final_norm, v_ffn1_norm, v_ffn1_w_gate, v_ffn1_w_up, v_ffn1_w_down, v_mix_norm, v_w_in, v_pool_w, v_pool_scale, v_dn_conv_w, v_dn_a_log, v_dn_dt_bias, v_dn_out_norm, v_w_out, v_ffn2_norm, v_ffn2_w_gate, v_ffn2_w_up, v_ffn2_w_down, v_final_norm):
    names = ["ffn1_norm", "ffn1_w_gate", "ffn1_w_up", "ffn1_w_down", "mix_norm", "w_in", "pool_w", "pool_scale", "dn_conv_w",
             "dn_a_log", "dn_dt_bias", "dn_out_norm", "w_out", "ffn2_norm", "ffn2_w_gate", "ffn2_w_up", "ffn2_w_down", "final_norm"]
    W = dict(zip(names, [ffn1_norm, ffn1_w_gate, ffn1_w_up, ffn1_w_down, mix_norm, w_in, pool_w, pool_scale, dn_conv_w,
                         dn_a_log, dn_dt_bias, dn_out_norm, w_out, ffn2_norm, ffn2_w_gate, ffn2_w_up, ffn2_w_down, final_norm]))
    M = dict(zip(names, [m_ffn1_norm, m_ffn1_w_gate, m_ffn1_w_up, m_ffn1_w_down, m_mix_norm, m_w_in, m_pool_w, m_pool_scale,
                         m_dn_conv_w, m_dn_a_log, m_dn_dt_bias, m_dn_out_norm, m_w_out, m_ffn2_norm, m_ffn2_w_gate, m_ffn2_w_up,
                         m_ffn2_w_down, m_final_norm]))
    V = dict(zip(names, [v_ffn1_norm, v_ffn1_w_gate, v_ffn1_w_up, v_ffn1_w_down, v_mix_norm, v_w_in, v_pool_w, v_pool_scale,
                         v_dn_conv_w, v_dn_a_log, v_dn_dt_bias, v_dn_out_norm, v_w_out, v_ffn2_norm, v_ffn2_w_gate, v_ffn2_w_up,
                         v_ffn2_w_down, v_final_norm]))
    chip = 2 * lax.axis_index("x") + lax.axis_index("y")

    ffn_names = [(f"ffn{f}_w_gate", f"ffn{f}_w_up", f"ffn{f}_w_down") for f in (1, 2)]
    tr = lambda t: jnp.swapaxes(t, -1, -2)
    def ffn_rows(l, which):
        g, u, dn = ffn_names[which]
        return [tr(W[g][l]), tr(W[u][l]), W[dn][l]]

    def second_half(l):
        return jnp.concatenate(ffn_rows(l, 1) + [jnp.pad(W["w_out"][l], ((0, FC - 256), (0, 0)))], axis=0).astype(bf16)

    ws = _Weights()
    first0, = _all_gather_weights([jnp.concatenate(ffn_rows(0, 0), axis=0).astype(bf16)])
    ws.ffn[(0, 0)] = (first0, 0)

    def landed_00(gathered):
        ws.ffn[(0, 1)], ws.wout[0] = (gathered[0], 0), (gathered[0], 3)
        ws.set_w_in(0, gathered[1])

    def landed_01(gathered):
        ws.ffn[(1, 0)] = (gathered[0], 0)
        ws.set_w_in(1, gathered[1])

    def landed_10(gathered):
        ws.ffn[(1, 1)], ws.wout[1] = (gathered[0], 0), (gathered[0], 3)

    ws.rides[(0, 0)] = ([second_half(0), W["w_in"][0].astype(bf16)], landed_00)
    ws.rides[(0, 1)] = ([jnp.concatenate(ffn_rows(1, 0), axis=0).astype(bf16), W["w_in"][1].astype(bf16)], landed_01)
    ws.rides[(1, 0)] = ([second_half(1)], landed_10)
    conv_all = _gather_small(_pad_rows(dn_conv_w.reshape(DEPTH * 4 * (CW // NCH) // 128, 128), 32))
    conv_full = jnp.concatenate([conv_all[2 * j, :DEPTH * 4 * (CW // NCH) // 128].reshape(DEPTH, 4, CW // NCH) for j in range(NCH)],
                                axis=-1)

    par = jnp.pad(jnp.stack([dn_a_log, dn_dt_bias], axis=1), ((0, 0), (0, 6), (4, 120)))
    wts = dict(ffn1_norm=ffn1_norm, mix_norm=mix_norm, ffn2_norm=ffn2_norm, final_norm=final_norm.reshape(1, D),
               pool_bd=jnp.stack([_block_diag(pool_w[l]) for l in range(DEPTH)]).astype(bf16),
               pool_scale=pool_scale, conv_w=jnp.pad(conv_full, ((0, 0), (0, 4), (0, 0))),
               dn_par=par, dn_out_norm=dn_out_norm)

    c_arr = lax.axis_index("c").astype(jnp.int32).reshape(1)

    def prep(parts):
        return [_sum_core_pair(p, g, c_arr) for p, g in zip(parts, _send_sibling_half(parts))]

    loss, dx, g_final, grads, pieces1, pieces0 = _device_step(x[0], positions.reshape(S, 1), loss_target[0], wts, ws, prep)
    last = [jnp.concatenate([grads[0][n] for n in ffn_names[0]], axis=1)]
    pieces_last = _scatter_to_chips(prep(last))
    full_b = _sum_chips(pieces_last[0], c_arr, None, 0, RB)
    full_b = _sum_chips(pieces0[0], c_arr, full_b, 3 * FC, RB)
    full_b = _sum_chips(pieces1[0], c_arr, full_b, 7 * FC, RB)
    full_c = _sum_chips(pieces0[1], c_arr, None, 0, RC)
    full_c = _sum_chips(pieces1[1], c_arr, full_c, D, RC)
    full_b, full_c = _join_halves([full_b, full_c], [[(0, 3 * FC), (3 * FC, 4 * FC), (7 * FC, 7 * FC)], [(0, D), (D, D)]])

    small = {"loss": loss[0, 0:1], "final_norm": g_final.reshape(D)}
    for n in ("ffn1_norm", "mix_norm", "ffn2_norm", "pool_scale", "dn_out_norm"):
        small[n] = jnp.stack([grads[l][n].reshape(-1) for l in range(DEPTH)])
    small["pool_w"] = jnp.stack([jnp.stack([grads[l]["pool_bd"][64 * g:64 * (g + 1), 64 * g:64 * (g + 1)] for g in range(4)])
                                 for l in range(DEPTH)])
    small["dn_conv_w"] = jnp.stack([grads[l]["conv_w"][0:4] for l in range(DEPTH)])
    small["dn_a_log"] = jnp.stack([grads[l]["dn_par"][0, 4:8] for l in range(DEPTH)])
    small["dn_dt_bias"] = jnp.stack([grads[l]["dn_par"][1, 4:8] for l in range(DEPTH)])
    packed = _pack_small(small)
    g_small = _sum_pieces(_gather_small(packed), f32, "sum_small")
    gs = _unpack_small(g_small)

    transposed = ("ffn1_w_gate", "ffn1_w_up", "ffn2_w_gate", "ffn2_w_up")
    where = {"ffn1_w_gate": (full_b, FC // 2, lambda l: 14 * l), "ffn1_w_up": (full_b, FC // 2, lambda l: 14 * l + 2),
             "ffn1_w_down": (full_b, FC // 2, lambda l: 14 * l + 4), "ffn2_w_gate": (full_b, FC // 2, lambda l: 14 * l + 6),
             "ffn2_w_up": (full_b, FC // 2, lambda l: 14 * l + 8), "ffn2_w_down": (full_b, FC // 2, lambda l: 14 * l + 10),
             "w_out": (full_b, 64, lambda l: (FC // 64) * (7 * l + 6)), "w_in": (full_c, D // 2, lambda l: 2 * l)}
    big_res = {}
    for n, (gblob, tile, first) in where.items():
        t = tr if n in transposed else (lambda a: a)
        big_res[n] = [t(r) for r in _adamw_rows(t(W[n]), t(M[n]), t(V[n]), gblob, tile, first, "adamw_" + n)]

    def small_of(T):
        d = {n: T[n] for n, _ in _SMALL if n not in ("loss", "dn_conv_w")}
        d["loss"] = jnp.zeros((1,), f32)
        d["dn_conv_w"] = _at_own_columns(T["dn_conv_w"], chip)
        return _pack_small(d)

    res_s = _adamw(small_of(W), g_small, small_of(M), small_of(V), "adamw_small")
    small_out = [_unpack_small(r) for r in res_s]

    def split_blobs(b, c):
        out = {}
        b7 = b.reshape(DEPTH, 7, FC, D)
        for k, n in enumerate(n for names3 in ffn_names for n in names3):
            out[n] = tr(b7[:, k]) if n in transposed else b7[:, k]
        out["w_out"] = b7[:, 6, :256]
        out["w_in"] = c.reshape(DEPTH, D, INC)
        return out

    def assemble(big, sm):
        out = []
        for n in names:
            if n in big:
                out.append(big[n])
            elif n == "dn_conv_w":
                out.append(_own_columns(sm[n], chip))
            else:
                out.append(sm[n])
        return out

    grad_list = assemble(split_blobs(full_b, full_c), gs)
    outs = [gs["loss"].reshape(()), dx.reshape(1, S, D)] + grad_list
    for k in range(3):
        outs += assemble({n: r[k] for n, r in big_res.items()}, small_out[k])
    return tuple(outs)
```

```python
import functools
import math

import jax
import jax.numpy as jnp
import numpy as np
from jax import lax
from jax.experimental import pallas as pl
from jax.experimental.pallas import tpu as pltpu

f32 = jnp.float32
bf16 = jnp.bfloat16
SDS = jax.ShapeDtypeStruct
MESH = pl.DeviceIdType.MESH

S = 4096
D = 1024
DEPTH = 2
FF = 2816
NCH = 4
FC = FF // NCH
INW = 3080
INC = INW // NCH
INP = 3200
ATT = 256
EH = 64
NBLK = 128
DNW = 512
DH = 128
CH = 64
NCHUNK = S // CH
EPS = 1e-6
ROPE_THETA = 10000.0
PATTERN_DIL = (1, 4, 16)
ADAM_LR, ADAM_B1, ADAM_B2, ADAM_EPS, ADAM_WD, ADAM_STEP = 0.001, 0.9, 0.999, 1e-08, 0.01, 10
VMEM_BYTES_V7X = 64 * 1024 * 1024
NEG = -1e30

TM = 512
RB, RC = 14 * FC, 2 * D


def _cp(vmem_mb=48, sem=None):
    kw = dict(vmem_limit_bytes=vmem_mb * 1024 * 1024)
    if sem is not None:
        kw["dimension_semantics"] = sem
    return pltpu.CompilerParams(**kw)


def _pc(*args, **kwargs):
    pin = lambda s: pltpu.HBM(s.shape, s.dtype) if isinstance(s, SDS) and jnp.issubdtype(s.dtype, jnp.floating) else s
    out = kwargs["out_shape"]
    kwargs["out_shape"] = [pin(s) for s in out] if isinstance(out, (list, tuple)) else pin(out)
    call = pl.pallas_call(*args, **kwargs)

    def run(*operands):
        pinned = [pltpu.with_memory_space_constraint(o, pltpu.HBM) if jnp.issubdtype(o.dtype, jnp.floating) else o
                  for o in operands]
        return call(*pinned)

    return run


def _dot(a, b):
    return jnp.dot(a, b, preferred_element_type=f32)


def _dot_nt(a, b):
    return lax.dot_general(a, b, (((1,), (1,)), ((), ())), preferred_element_type=f32)


def _dot_tn(a, b):
    return lax.dot_general(a, b, (((0,), (0,)), ((), ())), preferred_element_type=f32)


def _rms(x, w):
    r = lax.rsqrt(jnp.mean(x * x, axis=-1, keepdims=True) + EPS)
    return x * r * w, r


def _rms_bwd(x, w, r, dh):
    xhat = x * r
    dw = jnp.sum(dh * xhat, axis=0, keepdims=True)
    dxh = dh * w
    dx = r * (dxh - xhat * jnp.mean(dxh * xhat, axis=-1, keepdims=True))
    return dx, dw


def _ffn_fwd(x, nw, blob, k0, ride=None):
    kg, ku, kd = k0, k0 + 1, k0 + 2
    nr = 0 if ride is None else len(ride)
    ni = S // TM

    def body(*refs):
        x_ref, nw_ref, wg_ref, wu_ref, wd_ref = refs[:5]
        ride_in = refs[5:5 + nr]
        o_ref = refs[5 + nr]
        ride_out = refs[6 + nr:6 + 2 * nr]
        h_scr, acc_scr = refs[6 + 2 * nr:8 + 2 * nr]
        sems = refs[8 + 2 * nr:]
        i = pl.program_id(0)
        j = pl.program_id(1)

        if nr:
            @pl.when(jnp.logical_and(i == 0, j == 0))
            def _():
                _gather_start(ride_in, ride_out, *sems)

        @pl.when(j == 0)
        def _():
            h, _ = _rms(x_ref[...], nw_ref[...])
            h_scr[...] = h.astype(bf16)
            acc_scr[...] = jnp.zeros_like(acc_scr)

        h = h_scr[...]
        g = _dot_nt(h, wg_ref[0])
        u = _dot_nt(h, wu_ref[0])
        a = (g * jax.nn.sigmoid(g) * u).astype(bf16)
        acc_scr[...] += _dot(a, wd_ref[0])

        @pl.when(j == NCH - 1)
        def _():
            o_ref[...] = x_ref[...] + 0.5 * acc_scr[...]

        if nr:
            @pl.when(jnp.logical_and(i == ni - 1, j == NCH - 1))
            def _():
                _gather_finish(ride_in, ride_out, *sems)

    wspec = lambda k: pl.BlockSpec((1, FC, D), lambda i, j: (j, k, 0))
    rides = [] if ride is None else list(ride)
    res = _pc(
        body, grid=(ni, NCH), name="ffn_fwd_ride" if nr else "ffn_fwd",
        in_specs=[pl.BlockSpec((TM, D), lambda i, j: (i, 0)),
                  pl.BlockSpec((1, D), lambda i, j: (0, 0)),
                  wspec(kg), wspec(ku), wspec(kd)] + [ANY] * nr,
        out_specs=[pl.BlockSpec((TM, D), lambda i, j: (i, 0))] + [ANY] * nr,
        out_shape=[SDS((S, D), f32)] + [SDS((NCH,) + r.shape, r.dtype) for r in rides],
        scratch_shapes=[pltpu.VMEM((TM, D), bf16), pltpu.VMEM((TM, D), f32)] + (_gather_sems(nr) if nr else []),
        compiler_params=_cp(40),
    )(x, nw, blob, blob, blob, *rides)
    return res if nr else res[0]


def _ffn_bwd(x, nw, blob, k0, dy, ride=None):
    nt = S // TM
    kg, ku, kd = k0, k0 + 1, k0 + 2
    nr = 0 if ride is None else len(ride)

    def body(*refs):
        x_ref, nw_ref, wg_ref, wu_ref, wd_ref, dy_ref = refs[:6]
        ride_in = refs[6:6 + nr]
        dwg_ref, dwu_ref, dwd_ref, dh_ref = refs[6 + nr:10 + nr]
        ride_out = refs[10 + nr:10 + 2 * nr]
        ag, au, ad = refs[10 + 2 * nr:13 + 2 * nr]
        sems = refs[13 + 2 * nr:]
        j = pl.program_id(0)
        i = pl.program_id(1)

        if nr:
            @pl.when(jnp.logical_and(j == 0, i == 0))
            def _():
                _scatter_start(ride_in, ride_out, *sems)

        @pl.when(i == 0)
        def _():
            ag[...] = jnp.zeros_like(ag)
            au[...] = jnp.zeros_like(au)
            ad[...] = jnp.zeros_like(ad)

        hf, _ = _rms(x_ref[...], nw_ref[...])
        h = hf.astype(bf16)
        g = _dot_nt(h, wg_ref[0])
        u = _dot_nt(h, wu_ref[0])
        sg = jax.nn.sigmoid(g)
        s = g * sg
        a = (s * u).astype(bf16)
        dyb = (0.5 * dy_ref[...]).astype(bf16)
        da = _dot_nt(dyb, wd_ref[0])
        ad[...] += _dot_tn(a, dyb)
        du = (da * s).astype(bf16)
        dg = (da * u * (sg * (1.0 + g * (1.0 - sg)))).astype(bf16)
        ag[...] += _dot_tn(dg, h)
        au[...] += _dot_tn(du, h)
        dh_ref[0] = (_dot(dg, wg_ref[0]) + _dot(du, wu_ref[0])).astype(bf16)

        @pl.when(i == nt - 1)
        def _():
            dwg_ref[0] = ag[...].astype(bf16)
            dwu_ref[0] = au[...].astype(bf16)
            dwd_ref[0] = ad[...].astype(bf16)

        if nr:
            @pl.when(jnp.logical_and(j == NCH - 1, i == nt - 1))
            def _():
                _scatter_finish(ride_in, ride_out, *sems)

    wspec = lambda k: pl.BlockSpec((1, FC, D), lambda j, i: (j, k, 0))
    gspec = pl.BlockSpec((1, FC, D), lambda j, i: (j, 0, 0))
    rides = [] if ride is None else list(ride)
    return _pc(
        body, grid=(NCH, nt), name="ffn_bwd_ride" if nr else "ffn_bwd",
        in_specs=[pl.BlockSpec((TM, D), lambda j, i: (i, 0)),
                  pl.BlockSpec((1, D), lambda j, i: (0, 0)),
                  wspec(kg), wspec(ku), wspec(kd),
                  pl.BlockSpec((TM, D), lambda j, i: (i, 0))] + [ANY] * nr,
        out_specs=[gspec, gspec, gspec, pl.BlockSpec((1, TM, D), lambda j, i: (j, i, 0))] + [ANY] * nr,
        out_shape=[SDS((NCH, FC, D), bf16)] * 3 + [SDS((NCH, S, D), bf16)] + [SDS(r.shape, r.dtype) for r in rides],
        scratch_shapes=[pltpu.VMEM((FC, D), f32)] * 3 + (_scatter_sems(nr) if nr else []),
        compiler_params=_cp(56),
    )(x, nw, blob, blob, blob, dy, *rides)


def _norm_bwd(x, nw, dres, dh4):
    nt = S // TM
    nparts = dh4.shape[0]

    def body(x_ref, nw_ref, dres_ref, dh_ref, dx_ref, dnw_ref):
        i = pl.program_id(0)
        dh = dh_ref[0].astype(f32)
        for p in range(1, nparts):
            dh = dh + dh_ref[p].astype(f32)
        xv = x_ref[...]
        _, r = _rms(xv, nw_ref[...])
        dx, dw = _rms_bwd(xv, nw_ref[...], r, dh)
        dx_ref[...] = dres_ref[...] + dx

        @pl.when(i == 0)
        def _():
            dnw_ref[...] = jnp.zeros_like(dnw_ref)

        dnw_ref[...] += dw

    return _pc(
        body, grid=(nt,), name="norm_bwd",
        in_specs=[pl.BlockSpec((TM, D), lambda i: (i, 0)),
                  pl.BlockSpec((1, D), lambda i: (0, 0)),
                  pl.BlockSpec((TM, D), lambda i: (i, 0)),
                  pl.BlockSpec((nparts, TM, D), lambda i: (0, i, 0))],
        out_specs=[pl.BlockSpec((TM, D), lambda i: (i, 0)), pl.BlockSpec((1, D), lambda i: (0, 0))],
        out_shape=[SDS((S, D), f32), SDS((1, D), f32)],
        compiler_params=_cp(40),
    )(x, nw, dres, dh4)


def _final(x, nw, target):
    nt = S // TM

    def body(x_ref, nw_ref, t_ref, dx_ref, dnw_ref, loss_ref):
        i = pl.program_id(0)
        xv = x_ref[...]
        y, r = _rms(xv, nw_ref[...])
        err = y - t_ref[...]
        part = 0.5 * jnp.sum(jnp.mean(err * err, axis=-1, keepdims=True), axis=0, keepdims=True)
        dx, dw = _rms_bwd(xv, nw_ref[...], r, err * (1.0 / D))
        dx_ref[...] = dx

        @pl.when(i == 0)
        def _():
            dnw_ref[...] = jnp.zeros_like(dnw_ref)
            loss_ref[...] = jnp.zeros_like(loss_ref)

        dnw_ref[...] += dw
        loss_ref[...] += jnp.broadcast_to(part, loss_ref.shape)

    return _pc(
        body, grid=(nt,), name="final_loss",
        in_specs=[pl.BlockSpec((TM, D), lambda i: (i, 0)),
                  pl.BlockSpec((1, D), lambda i: (0, 0)),
                  pl.BlockSpec((TM, D), lambda i: (i, 0))],
        out_specs=[pl.BlockSpec((TM, D), lambda i: (i, 0)), pl.BlockSpec((1, D), lambda i: (0, 0)),
                   pl.BlockSpec((1, 128), lambda i: (0, 0))],
        out_shape=[SDS((S, D), f32), SDS((1, D), f32), SDS((1, 128), f32)],
        compiler_params=_cp(40),
    )(x, nw, target)


def _rot_half(t):
    lane = lax.broadcasted_iota(jnp.int32, t.shape, 1)
    first = (lane % EH) < (EH // 2)
    return jnp.where(first, -pltpu.roll(t, ATT - EH // 2, 1), pltpu.roll(t, EH // 2, 1))


def _rope_tables(pos_ref, freq_ref):
    ang = pos_ref[...].astype(f32) * freq_ref[...]
    return jnp.cos(ang), jnp.sin(ang)


def _split_residues(val, scr, outs):
    rows, cols = val.shape
    for j in range(cols // 128):
        scr[j] = val[:, 128 * j:128 * (j + 1)]
    for ref, d in outs:
        for j in range(cols // 128):
            for r in range(d):
                ref.at[r][:, 128 * j:128 * (j + 1)] = scr.at[j][pl.ds(r, rows // d, stride=d), :]


def _join_residues(ref, d, scr):
    rows, cols = scr.shape[1], ref.shape[2]
    for j in range(cols // 128):
        for r in range(d):
            scr.at[j][pl.ds(r, rows // d, stride=d), :] = ref.at[r][:, 128 * j:128 * (j + 1)]
    return jnp.concatenate([scr[j] for j in range(cols // 128)], axis=1)


def _res_spec(d, tile, cols):
    return pl.BlockSpec((d, tile // d, cols), lambda i: (0, i, 0))


def _inproj_fwd(x, nw, w_aug, pos, freq):
    TI = 256

    def body(x_ref, nw_ref, w_hbm, pos_ref, freq_ref, att_ref, att4_ref, att16_ref, pu_ref, dq_ref, dz_ref, dba_ref,
             w_scr, r_scr):
        @pl.when(pl.program_id(0) == 0)
        def _():
            pltpu.sync_copy(w_hbm, w_scr)

        h, _ = _rms(x_ref[...], nw_ref[...])
        proj = _dot(h.astype(bf16), w_scr[...])
        cos, sin = _rope_tables(pos_ref, freq_ref)
        q = proj[:, 0:ATT]
        k = proj[:, ATT:2 * ATT]
        att = jnp.concatenate([q * cos + _rot_half(q) * sin, k * cos + _rot_half(k) * sin, proj[:, 2 * ATT:3 * ATT]], axis=1)
        att_ref[...] = att
        _split_residues(att, r_scr, [(att4_ref, 4), (att16_ref, 16)])
        pu_ref[...] = proj[:, 768:1024]
        dq_ref[...] = proj[:, 1024:2560]
        dz_ref[...] = proj[:, 2560:3072]
        dba_ref[...] = proj[:, 3072:3200]

    return _pc(
        body, grid=(S // TI,), name="inproj_fwd",
        in_specs=[pl.BlockSpec((TI, D), lambda i: (i, 0)),
                  pl.BlockSpec((1, D), lambda i: (0, 0)),
                  pl.BlockSpec(memory_space=pl.ANY),
                  pl.BlockSpec((TI, 1), lambda i: (i, 0)),
                  pl.BlockSpec((1, ATT), lambda i: (0, 0))],
        out_specs=[pl.BlockSpec((TI, 768), lambda i: (i, 0)), _res_spec(4, TI, 768), _res_spec(16, TI, 768),
                   pl.BlockSpec((TI, 256), lambda i: (i, 0)),
                   pl.BlockSpec((TI, 1536), lambda i: (i, 0)), pl.BlockSpec((TI, 512), lambda i: (i, 0)),
                   pl.BlockSpec((TI, 128), lambda i: (i, 0))],
        out_shape=[SDS((S, 768), f32), SDS((4, S // 4, 768), f32), SDS((16, S // 16, 768), f32), SDS((S, 256), f32),
                   SDS((S, 1536), f32), SDS((S, 512), f32), SDS((S, 128), f32)],
        scratch_shapes=[pltpu.VMEM((D, INP), bf16), pltpu.VMEM((6, TI, 128), f32)],
        compiler_params=_cp(48),
    )(x, nw, w_aug, pos, freq)


def _inproj_bwd(x, nw, w_aug, pos, freq, dres, datt, datt4, datt16, dpu, ddq, ddz, ddba):
    TI = 256
    nt = S // TI

    def body(x_ref, nw_ref, w_hbm, pos_ref, freq_ref, dres_ref, datt_ref, datt4_ref, datt16_ref, dpu_ref, ddq_ref, ddz_ref,
             ddba_ref, dx_ref, dnw_ref, dw_hbm, w_scr, acc, r_scr):
        i = pl.program_id(0)

        @pl.when(i == 0)
        def _():
            pltpu.sync_copy(w_hbm, w_scr)
            acc[...] = jnp.zeros_like(acc)
            dnw_ref[...] = jnp.zeros_like(dnw_ref)

        xv = x_ref[...]
        hf, r = _rms(xv, nw_ref[...])
        h = hf.astype(bf16)
        cos, sin = _rope_tables(pos_ref, freq_ref)
        datt = datt_ref[...] + _join_residues(datt4_ref, 4, r_scr)
        datt = datt + _join_residues(datt16_ref, 16, r_scr)
        dq = datt[:, 0:ATT]
        dk = datt[:, ATT:2 * ATT]
        dq = dq * cos - _rot_half(dq) * sin
        dk = dk * cos - _rot_half(dk) * sin
        dproj = jnp.concatenate([dq, dk, datt[:, 2 * ATT:3 * ATT], dpu_ref[...], ddq_ref[...], ddz_ref[...], ddba_ref[...]],
                                axis=1).astype(bf16)
        acc[...] += _dot_tn(h, dproj)
        dh = _dot_nt(dproj, w_scr[...])
        dx, dw = _rms_bwd(xv, nw_ref[...], r, dh)
        dx_ref[...] = dres_ref[...] + dx
        dnw_ref[...] += dw

        @pl.when(i == nt - 1)
        def _():
            pltpu.sync_copy(acc, dw_hbm)

    return _pc(
        body, grid=(nt,), name="inproj_bwd",
        in_specs=[pl.BlockSpec((TI, D), lambda i: (i, 0)),
                  pl.BlockSpec((1, D), lambda i: (0, 0)),
                  pl.BlockSpec(memory_space=pl.ANY),
                  pl.BlockSpec((TI, 1), lambda i: (i, 0)),
                  pl.BlockSpec((1, ATT), lambda i: (0, 0)),
                  pl.BlockSpec((TI, D), lambda i: (i, 0)),
                  pl.BlockSpec((TI, 768), lambda i: (i, 0)), _res_spec(4, TI, 768), _res_spec(16, TI, 768),
                  pl.BlockSpec((TI, 256), lambda i: (i, 0)),
                  pl.BlockSpec((TI, 1536), lambda i: (i, 0)),
                  pl.BlockSpec((TI, 512), lambda i: (i, 0)),
                  pl.BlockSpec((TI, 128), lambda i: (i, 0))],
        out_specs=[pl.BlockSpec((TI, D), lambda i: (i, 0)), pl.BlockSpec((1, D), lambda i: (0, 0)),
                   pl.BlockSpec(memory_space=pl.ANY)],
        out_shape=[SDS((S, D), f32), SDS((1, D), f32), SDS((D, INP), f32)],
        scratch_shapes=[pltpu.VMEM((D, INP), bf16), pltpu.VMEM((D, INP), f32), pltpu.VMEM((6, TI, 128), f32)],
        compiler_params=_cp(56),
    )(x, nw, w_aug, pos, freq, dres, datt, datt4, datt16, dpu, ddq, ddz, ddba)


def _outproj_fwd(x, ya, yb, yc, blob_b, kw):
    def body(x_ref, ya_ref, yb_ref, yc_ref, w_ref, o_ref):
        ycat = jnp.concatenate([ya_ref[...], yb_ref[...], yc_ref[...]], axis=1).astype(bf16)
        o_ref[...] = x_ref[...] + _dot(ycat, w_ref[:, 0:256, :].reshape(D, D))

    return _pc(
        body, grid=(S // TM,), name="outproj_fwd",
        in_specs=[pl.BlockSpec((TM, D), lambda i: (i, 0)),
                  pl.BlockSpec((TM, 256), lambda i: (i, 0)),
                  pl.BlockSpec((TM, 256), lambda i: (i, 0)),
                  pl.BlockSpec((TM, 512), lambda i: (i, 0)),
                  pl.BlockSpec((NCH, FC, D), lambda i: (0, kw, 0))],
        out_specs=pl.BlockSpec((TM, D), lambda i: (i, 0)),
        out_shape=SDS((S, D), f32),
        compiler_params=_cp(40),
    )(x, ya, yb, yc, blob_b)


def _outproj_bwd(x, nw, dres, dh4, ya, yb, yc, blob_b, kw):
    nt = S // TM
    nparts = dh4.shape[0]

    def body(x_ref, nw_ref, dres_ref, dh_ref, ya_ref, yb_ref, yc_ref, w_ref, dx_ref, dnw_ref, dya_ref, dyb_ref, dyc_ref, dw_ref):
        i = pl.program_id(0)

        @pl.when(i == 0)
        def _():
            dw_ref[...] = jnp.zeros_like(dw_ref)
            dnw_ref[...] = jnp.zeros_like(dnw_ref)

        dh = dh_ref[0].astype(f32)
        for p in range(1, nparts):
            dh = dh + dh_ref[p].astype(f32)
        xv = x_ref[...]
        _, r = _rms(xv, nw_ref[...])
        dxn, dnw = _rms_bwd(xv, nw_ref[...], r, dh)
        dx = dres_ref[...] + dxn
        dx_ref[...] = dx
        dnw_ref[...] += dnw
        dyv = dx.astype(bf16)
        ycat = jnp.concatenate([ya_ref[...], yb_ref[...], yc_ref[...]], axis=1).astype(bf16)
        dw_ref[...] += _dot_tn(ycat, dyv)
        dcat = _dot_nt(dyv, w_ref[:, 0:256, :].reshape(D, D))
        dya_ref[...] = dcat[:, 0:256]
        dyb_ref[...] = dcat[:, 256:512]
        dyc_ref[...] = dcat[:, 512:1024]

    return _pc(
        body, grid=(nt,), name="outproj_bwd",
        in_specs=[pl.BlockSpec((TM, D), lambda i: (i, 0)),
                  pl.BlockSpec((1, D), lambda i: (0, 0)),
                  pl.BlockSpec((TM, D), lambda i: (i, 0)),
                  pl.BlockSpec((nparts, TM, D), lambda i: (0, i, 0)),
                  pl.BlockSpec((TM, 256), lambda i: (i, 0)),
                  pl.BlockSpec((TM, 256), lambda i: (i, 0)),
                  pl.BlockSpec((TM, 512), lambda i: (i, 0)),
                  pl.BlockSpec((NCH, FC, D), lambda i: (0, kw, 0))],
        out_specs=[pl.BlockSpec((TM, D), lambda i: (i, 0)), pl.BlockSpec((1, D), lambda i: (0, 0)),
                   pl.BlockSpec((TM, 256), lambda i: (i, 0)), pl.BlockSpec((TM, 256), lambda i: (i, 0)),
                   pl.BlockSpec((TM, 512), lambda i: (i, 0)), pl.BlockSpec((D, D), lambda i: (0, 0))],
        out_shape=[SDS((S, D), f32), SDS((1, D), f32), SDS((S, 256), f32), SDS((S, 256), f32), SDS((S, 512), f32),
                   SDS((D, D), f32)],
        compiler_params=_cp(48),
    )(x, nw, dres, dh4, ya, yb, yc, blob_b)


QT = NBLK
NB = S // QT


def _attn_block(q, kp, kc, vp, vc, first):
    kk = jnp.concatenate([kp, kc], axis=0).astype(bf16)
    vv = jnp.concatenate([vp, vc], axis=0).astype(bf16)
    qi = lax.broadcasted_iota(jnp.int32, (4 * QT, NBLK + QT), 0) % QT
    ki = lax.broadcasted_iota(jnp.int32, (4 * QT, NBLK + QT), 1)
    dist = NBLK + qi - ki
    valid = (dist >= 0) & (dist <= NBLK) & (jnp.logical_not(first) | (ki >= NBLK))
    head = lax.broadcasted_iota(jnp.int32, (1, ATT), 1) // EH
    masks = [(head == h).astype(f32) for h in range(4)]
    qs = jnp.concatenate([q * (mh * (1.0 / math.sqrt(EH))) for mh in masks], axis=0).astype(bf16)
    s = _dot_nt(qs, kk)
    s = jnp.where(valid, s, NEG)
    m = lax.stop_gradient(jnp.max(s, axis=-1, keepdims=True))
    p = jnp.exp(s - m)
    den = jnp.sum(p, axis=-1, keepdims=True)
    po = _dot((p * (1.0 / den)).astype(bf16), vv)
    lse = m + jnp.log(den)
    o = jnp.zeros((QT, ATT), f32)
    l = jnp.zeros((QT, ATT), f32)
    for h, mh in enumerate(masks):
        o = o + po[QT * h:QT * (h + 1)] * mh
        l = l + lse[QT * h:QT * (h + 1)] * mh
    return o, l


def _attn_specs(tile):
    own = lambda col: pl.BlockSpec((QT, ATT), lambda s: (tile(s), col))
    prev = lambda col: pl.BlockSpec((NBLK, ATT), lambda s: (jnp.maximum((QT // NBLK) * tile(s) - 1, 0), col))
    return [own(0), prev(1), own(1), prev(2), own(2)]


def _attn_fwd(qkv, per_seq):
    def body(q_ref, kp_ref, kc_ref, vp_ref, vc_ref, o_ref, l_ref):
        first = pl.program_id(0) % per_seq == 0
        o, l = _attn_block(q_ref[...], kp_ref[...], kc_ref[...], vp_ref[...], vc_ref[...], first)
        o_ref[...] = o
        l_ref[...] = l

    blk = pl.BlockSpec((QT, ATT), lambda t: (t, 0))
    return _pc(
        body, grid=(NB,), name="attn_fwd", in_specs=_attn_specs(lambda t: t), out_specs=[blk, blk],
        out_shape=[SDS((S, ATT), f32), SDS((S, ATT), f32)], compiler_params=_cp(32),
    )(qkv, qkv, qkv, qkv, qkv)


def _attn_bwd(qkv, do, dl, per_seq):
    def body(q_ref, kp_ref, kc_ref, vp_ref, vc_ref, do_ref, dl_ref, o_ref, k_carry, v_carry):
        step = pl.program_id(0)

        @pl.when(step == 0)
        def _():
            k_carry[...] = jnp.zeros_like(k_carry)
            v_carry[...] = jnp.zeros_like(v_carry)

        t = NB - 1 - step
        first = t % per_seq == 0
        last = t % per_seq == per_seq - 1
        fn = lambda q, kp, kc, vp, vc: _attn_block(q, kp, kc, vp, vc, first)
        _, vjp = jax.vjp(fn, q_ref[...], kp_ref[...], kc_ref[...], vp_ref[...], vc_ref[...])
        dq, dkp, dkc, dvp, dvc = vjp((do_ref[...], dl_ref[...]))
        o_ref[:, 0:ATT] = dq
        o_ref[:, ATT:2 * ATT] = dkc
        o_ref[:, 2 * ATT:3 * ATT] = dvc
        o_ref[QT - NBLK:QT, ATT:2 * ATT] += jnp.where(last, 0.0, k_carry[...])
        o_ref[QT - NBLK:QT, 2 * ATT:3 * ATT] += jnp.where(last, 0.0, v_carry[...])
        k_carry[...] = dkp
        v_carry[...] = dvp

    rev = lambda s: NB - 1 - s
    blk = pl.BlockSpec((QT, ATT), lambda s: (rev(s), 0))
    return _pc(
        body, grid=(NB,), name="attn_bwd", in_specs=_attn_specs(rev) + [blk, blk],
        out_specs=pl.BlockSpec((QT, 768), lambda s: (rev(s), 0)),
        out_shape=SDS((S, 768), f32), scratch_shapes=[pltpu.VMEM((NBLK, ATT), f32)] * 2, compiler_params=_cp(40),
    )(qkv, qkv, qkv, qkv, qkv, do, dl)


def _merge_weights(l0, l1, l2):
    m = jnp.maximum(jnp.maximum(l0, l1), l2)
    e0, e1, e2 = jnp.exp(l0 - m), jnp.exp(l1 - m), jnp.exp(l2 - m)
    tot = e0 + e1 + e2
    return e0 / tot, e1 / tot, e2 / tot


def _merge_specs():
    nat = pl.BlockSpec((TM, ATT), lambda i: (i, 0))
    return nat, _res_spec(4, TM, ATT), _res_spec(16, TM, ATT)


def _merge_fwd(o1, l1, o4, l4, o16, l16):
    def body(o1_ref, l1_ref, o4_ref, l4_ref, o16_ref, l16_ref, y_ref, scr):
        o4v, l4v = _join_residues(o4_ref, 4, scr), _join_residues(l4_ref, 4, scr)
        o16v, l16v = _join_residues(o16_ref, 16, scr), _join_residues(l16_ref, 16, scr)
        w0, w1, w2 = _merge_weights(l1_ref[...], l4v, l16v)
        y_ref[...] = w0 * o1_ref[...] + w1 * o4v + w2 * o16v

    nat, r4, r16 = _merge_specs()
    return _pc(body, grid=(S // TM,), name="merge_fwd", in_specs=[nat, nat, r4, r4, r16, r16],
                          out_specs=nat, out_shape=SDS((S, ATT), f32), scratch_shapes=[pltpu.VMEM((2, TM, 128), f32)],
                          compiler_params=_cp(32))(o1, l1, o4, l4, o16, l16)


def _merge_bwd(o1, l1, o4, l4, o16, l16, dy):
    def body(o1_ref, l1_ref, o4_ref, l4_ref, o16_ref, l16_ref, dy_ref, do1_ref, dl1_ref, do4_ref, dl4_ref, do16_ref, dl16_ref, scr):
        o4v, l4v = _join_residues(o4_ref, 4, scr), _join_residues(l4_ref, 4, scr)
        o16v, l16v = _join_residues(o16_ref, 16, scr), _join_residues(l16_ref, 16, scr)
        o1v = o1_ref[...]
        w0, w1, w2 = _merge_weights(l1_ref[...], l4v, l16v)
        y = w0 * o1v + w1 * o4v + w2 * o16v
        dyv = dy_ref[...]
        do1_ref[...] = w0 * dyv
        dl1_ref[...] = w0 * (o1v - y) * dyv
        _split_residues(w1 * dyv, scr, [(do4_ref, 4)])
        _split_residues(w1 * (o4v - y) * dyv, scr, [(dl4_ref, 4)])
        _split_residues(w2 * dyv, scr, [(do16_ref, 16)])
        _split_residues(w2 * (o16v - y) * dyv, scr, [(dl16_ref, 16)])

    nat, r4, r16 = _merge_specs()
    return _pc(body, grid=(S // TM,), name="merge_bwd", in_specs=[nat, nat, r4, r4, r16, r16, nat],
                          out_specs=[nat, nat, r4, r4, r16, r16],
                          out_shape=[SDS((S, ATT), f32)] * 2 + [SDS((4, S // 4, ATT), f32)] * 2 + [SDS((16, S // 16, ATT), f32)] * 2,
                          scratch_shapes=[pltpu.VMEM((2, TM, 128), f32)], compiler_params=_cp(32))(o1, l1, o4, l4, o16, l16, dy)


HALO = 16


def _pool_consts(i, rows):
    grp = lax.broadcasted_iota(jnp.int32, (rows, 256), 1) // 64
    t = i * TM + lax.broadcasted_iota(jnp.int32, (rows, 256), 0)
    win = jnp.where(grp == 0, 2, jnp.where(grp == 1, 4, jnp.where(grp == 2, 8, 16)))
    cnt = jnp.minimum(t + 1, win).astype(f32)
    return grp, cnt


def _pool_select(grp, s2, s4, s8, s16):
    return jnp.where(grp == 0, s2, jnp.where(grp == 1, s4, jnp.where(grp == 2, s8, s16)))


def _pooled(i, cur, halo):
    xx = jnp.concatenate([halo, cur], axis=0)
    s2 = xx + pltpu.roll(xx, 1, 0)
    s4 = s2 + pltpu.roll(s2, 2, 0)
    s8 = s4 + pltpu.roll(s4, 4, 0)
    s16 = s8 + pltpu.roll(s8, 8, 0)
    grp, cnt = _pool_consts(i, TM)
    tot = _pool_select(grp, s2[HALO:], s4[HALO:], s8[HALO:], s16[HALO:])
    return tot / cnt - cur


def _pool_fwd(u, wp, scale):
    def body(u_ref, halo_ref, wp_ref, sc_ref, y_ref):
        i = pl.program_id(0)
        halo = halo_ref[...] * (i > 0).astype(f32)
        pooled = _pooled(i, u_ref[...], halo)
        y_ref[...] = _dot(pooled.astype(bf16), wp_ref[...]) * sc_ref[...]

    return _pc(
        body, grid=(S // TM,), name="pool_fwd",
        in_specs=[pl.BlockSpec((TM, 256), lambda i: (i, 0)),
                  pl.BlockSpec((HALO, 256), lambda i: (jnp.maximum(i * (TM // HALO) - 1, 0), 0)),
                  pl.BlockSpec((256, 256), lambda i: (0, 0)),
                  pl.BlockSpec((1, 256), lambda i: (0, 0))],
        out_specs=pl.BlockSpec((TM, 256), lambda i: (i, 0)), out_shape=SDS((S, 256), f32), compiler_params=_cp(32),
    )(u, u, wp, scale)


def _pool_bwd(u, wp, scale, dy):
    nt = S // TM

    def body(u_ref, halo_ref, wp_ref, sc_ref, dy_ref, dyn_ref, du_ref, dwp_ref, dsc_ref):
        i = pl.program_id(0)

        @pl.when(i == 0)
        def _():
            dwp_ref[...] = jnp.zeros_like(dwp_ref)
            dsc_ref[...] = jnp.zeros_like(dsc_ref)

        halo = halo_ref[...] * (i > 0).astype(f32)
        pooled = _pooled(i, u_ref[...], halo).astype(bf16)
        dyv = dy_ref[...]
        dsc_ref[...] += jnp.sum(dyv * _dot(pooled, wp_ref[...]), axis=0, keepdims=True)
        dys = (dyv * sc_ref[...]).astype(bf16)
        dwp_ref[...] += _dot_tn(pooled, dys)
        dpool = _dot_nt(dys, wp_ref[...])
        grp, cnt = _pool_consts(i, TM)
        dyn = ((dyn_ref[...] * (i < nt - 1).astype(f32)) * sc_ref[...]).astype(bf16)
        _, cntn = _pool_consts(i + 1, HALO)
        zn = _dot_nt(dyn, wp_ref[...]) / cntn
        zz = jnp.concatenate([dpool / cnt, zn], axis=0)
        n = TM + HALO
        a2 = zz + pltpu.roll(zz, n - 1, 0)
        a4 = a2 + pltpu.roll(a2, n - 2, 0)
        a8 = a4 + pltpu.roll(a4, n - 4, 0)
        a16 = a8 + pltpu.roll(a8, n - 8, 0)
        du_ref[...] = _pool_select(grp, a2[:TM], a4[:TM], a8[:TM], a16[:TM]) - dpool

    return _pc(
        body, grid=(nt,), name="pool_bwd",
        in_specs=[pl.BlockSpec((TM, 256), lambda i: (i, 0)),
                  pl.BlockSpec((HALO, 256), lambda i: (jnp.maximum(i * (TM // HALO) - 1, 0), 0)),
                  pl.BlockSpec((256, 256), lambda i: (0, 0)),
                  pl.BlockSpec((1, 256), lambda i: (0, 0)),
                  pl.BlockSpec((TM, 256), lambda i: (i, 0)),
                  pl.BlockSpec((HALO, 256), lambda i: (jnp.minimum((i + 1) * (TM // HALO), S // HALO - 1), 0))],
        out_specs=[pl.BlockSpec((TM, 256), lambda i: (i, 0)), pl.BlockSpec((256, 256), lambda i: (0, 0)),
                   pl.BlockSpec((1, 256), lambda i: (0, 0))],
        out_shape=[SDS((S, 256), f32), SDS((256, 256), f32), SDS((1, 256), f32)], compiler_params=_cp(32),
    )(u, u, wp, scale, dy, dy)


CW = 3 * DNW
CHALO = 8
TC = 256


def _conv_fwd(u, w):
    def body(u_ref, halo_ref, w_ref, c_ref):
        i = pl.program_id(0)
        xx = jnp.concatenate([halo_ref[...] * (i > 0).astype(f32), u_ref[...]], axis=0)
        c = (w_ref[3:4, :] * xx + w_ref[2:3, :] * pltpu.roll(xx, 1, 0) + w_ref[1:2, :] * pltpu.roll(xx, 2, 0)
             + w_ref[0:1, :] * pltpu.roll(xx, 3, 0))
        c_ref[...] = c[CHALO:]

    return _pc(
        body, grid=(S // TC,), name="conv_fwd",
        in_specs=[pl.BlockSpec((TC, CW), lambda i: (i, 0)),
                  pl.BlockSpec((CHALO, CW), lambda i: (jnp.maximum(i * (TC // CHALO) - 1, 0), 0)),
                  pl.BlockSpec((8, CW), lambda i: (0, 0))],
        out_specs=pl.BlockSpec((TC, CW), lambda i: (i, 0)), out_shape=SDS((S, CW), f32), compiler_params=_cp(32),
    )(u, u, w)


def _conv_bwd(u, w, dc):
    nt = S // TC

    def body(u_ref, halo_ref, w_ref, dc_ref, dcn_ref, du_ref, dw_ref):
        i = pl.program_id(0)

        @pl.when(i == 0)
        def _():
            dw_ref[...] = jnp.zeros_like(dw_ref)

        dcv = dc_ref[...]
        zz = jnp.concatenate([dcv, dcn_ref[...] * (i < nt - 1).astype(f32)], axis=0)
        n = TC + CHALO
        du = (w_ref[3:4, :] * zz + w_ref[2:3, :] * pltpu.roll(zz, n - 1, 0) + w_ref[1:2, :] * pltpu.roll(zz, n - 2, 0)
              + w_ref[0:1, :] * pltpu.roll(zz, n - 3, 0))
        du_ref[...] = du[:TC]
        xx = jnp.concatenate([halo_ref[...] * (i > 0).astype(f32), u_ref[...]], axis=0)
        for j in range(4):
            shifted = xx if j == 3 else pltpu.roll(xx, 3 - j, 0)
            dw_ref[j:j + 1, :] += jnp.sum(dcv * shifted[CHALO:], axis=0, keepdims=True)

    return _pc(
        body, grid=(nt,), name="conv_bwd",
        in_specs=[pl.BlockSpec((TC, CW), lambda i: (i, 0)),
                  pl.BlockSpec((CHALO, CW), lambda i: (jnp.maximum(i * (TC // CHALO) - 1, 0), 0)),
                  pl.BlockSpec((8, CW), lambda i: (0, 0)),
                  pl.BlockSpec((TC, CW), lambda i: (i, 0)),
                  pl.BlockSpec((CHALO, CW), lambda i: (jnp.minimum((i + 1) * (TC // CHALO), S // CHALO - 1), 0))],
        out_specs=[pl.BlockSpec((TC, CW), lambda i: (i, 0)), pl.BlockSpec((8, CW), lambda i: (0, 0))],
        out_shape=[SDS((S, CW), f32), SDS((8, CW), f32)], compiler_params=_cp(32),
    )(u, u, w, dc, dc)


TL = 512
NCL = TL // CH


def _bdot(a, b):
    return jnp.einsum('nik,nkj->nij', a.astype(bf16), b.astype(bf16), preferred_element_type=f32)


def _bdot_nt(a, b):
    return jnp.einsum('nik,njk->nij', a.astype(bf16), b.astype(bf16), preferred_element_type=f32)


def _bdot_tn(a, b):
    return jnp.einsum('nki,nkj->nij', a.astype(bf16), b.astype(bf16), preferred_element_type=f32)


@jax.custom_vjp
def _inv_unit_lower(a):
    ii = lax.broadcasted_iota(jnp.int32, (1, CH, CH), 1)
    jj = lax.broadcasted_iota(jnp.int32, (1, CH, CH), 2)
    t = (ii == jj).astype(f32) - a
    p = a
    for _ in range(5):
        p = _bdot(p, p)
        t = t + _bdot(t, p)
    return t


def _inv_unit_lower_fwd(a):
    t = _inv_unit_lower(a)
    return t, t


def _inv_unit_lower_bwd(t, dt):
    return (-_bdot_tn(t, _bdot_nt(dt, t)),)


_inv_unit_lower.defvjp(_inv_unit_lower_fwd, _inv_unit_lower_bwd)


def _dn_local(c, dba, a_row, b_row):
    act = c * jax.nn.sigmoid(c)
    lane = lax.broadcasted_iota(jnp.int32, (TL, 128), 1)
    beta_all = jax.nn.sigmoid(dba)
    xs = dba + b_row
    softplus = jnp.maximum(xs, 0.0) + jnp.log(1.0 + jnp.exp(-jnp.abs(xs)))
    g_all = -jnp.exp(a_row) * softplus
    ii = lax.broadcasted_iota(jnp.int32, (1, CH, CH), 1)
    jj = lax.broadcasted_iota(jnp.int32, (1, CH, CH), 2)
    lower = jj <= ii
    strict = jj < ii
    eye = (ii == jj).astype(f32)
    us, ws, qgs, kds, intras = [], [], [], [], []
    aux = jnp.zeros((TL, 128), f32)
    for h in range(4):
        q = act[:, DH * h:DH * (h + 1)]
        k = act[:, DNW + DH * h:DNW + DH * (h + 1)]
        v = act[:, 2 * DNW + DH * h:2 * DNW + DH * (h + 1)]
        q = q * lax.rsqrt(jnp.sum(q * q, axis=-1, keepdims=True) + EPS) * (DH ** -0.5)
        k = k * lax.rsqrt(jnp.sum(k * k, axis=-1, keepdims=True) + EPS)
        beta = jnp.sum(jnp.where(lane == h, beta_all, 0.0), axis=1, keepdims=True)
        g = jnp.sum(jnp.where(lane == 4 + h, g_all, 0.0), axis=1, keepdims=True)
        q3, k3, v3 = q.reshape(NCL, CH, DH), k.reshape(NCL, CH, DH), v.reshape(NCL, CH, DH)
        beta3, g3 = beta.reshape(NCL, CH, 1), g.reshape(NCL, CH, 1)
        g_row = jnp.sum(eye * g3, axis=1, keepdims=True)
        gc_col = jnp.sum(jnp.where(lower, g_row, 0.0), axis=2, keepdims=True)
        gc_row = jnp.sum(jnp.where(ii <= jj, g3, 0.0), axis=1, keepdims=True)
        diff = gc_col - gc_row
        decay = jnp.where(lower, jnp.exp(jnp.where(lower, diff, 0.0)), 0.0)
        kb = k3 * beta3
        vb = v3 * beta3
        a = jnp.where(strict, _bdot_nt(kb, k3) * decay, 0.0)
        t = _inv_unit_lower(a)
        u3 = _bdot(t, vb)
        w3 = _bdot(t, kb * jnp.exp(gc_col))
        intra = jnp.where(lower, _bdot_nt(q3, k3) * decay, 0.0)
        g_last = jnp.sum(g3, axis=1, keepdims=True)
        us.append(u3.reshape(TL, DH))
        ws.append(w3.reshape(TL, DH))
        qgs.append((q3 * jnp.exp(gc_col)).reshape(TL, DH))
        kds.append((k3 * jnp.exp(g_last - gc_col)).reshape(TL, DH))
        intras.append(intra.reshape(TL, CH))
        e_last = jnp.broadcast_to(jnp.exp(g_last), (NCL, CH, 1)).reshape(TL, 1)
        aux = aux + jnp.where(lane == h, e_last, 0.0)
    cat = lambda xs: jnp.concatenate(xs, axis=1)
    return cat(us), cat(ws), cat(qgs), cat(kds), jnp.stack(intras, axis=0), aux


def _dn_local_fwd(c, dba, par):
    def body(c_ref, dba_ref, par_ref, u_ref, w_ref, qg_ref, kd_ref, in_ref, aux_ref):
        u, w, qg, kd, intra, aux = _dn_local(c_ref[...], dba_ref[...], par_ref[0:1, :], par_ref[1:2, :])
        u_ref[...] = u
        w_ref[...] = w
        qg_ref[...] = qg
        kd_ref[...] = kd
        in_ref[...] = intra
        aux_ref[...] = aux

    wide = pl.BlockSpec((TL, DNW), lambda i: (i, 0))
    return _pc(
        body, grid=(S // TL,), name="dn_local_fwd",
        in_specs=[pl.BlockSpec((TL, CW), lambda i: (i, 0)), pl.BlockSpec((TL, 128), lambda i: (i, 0)),
                  pl.BlockSpec((8, 128), lambda i: (0, 0))],
        out_specs=[wide, wide, wide, wide, pl.BlockSpec((4, TL, CH), lambda i: (0, i, 0)),
                   pl.BlockSpec((TL, 128), lambda i: (i, 0))],
        out_shape=[SDS((S, DNW), f32)] * 4 + [SDS((4, S, CH), f32), SDS((S, 128), f32)], compiler_params=_cp(48),
    )(c, dba, par)


def _dn_local_bwd(c, dba, par, du, dw, dqg, dkd, dintra, daux):
    def body(c_ref, dba_ref, par_ref, du_ref, dw_ref, dqg_ref, dkd_ref, din_ref, daux_ref, dc_ref, ddba_ref, dpar_ref):
        @pl.when(pl.program_id(0) == 0)
        def _():
            dpar_ref[...] = jnp.zeros_like(dpar_ref)

        _, vjp = jax.vjp(_dn_local, c_ref[...], dba_ref[...], par_ref[0:1, :], par_ref[1:2, :])
        dc, ddba, da_row, db_row = vjp((du_ref[...], dw_ref[...], dqg_ref[...], dkd_ref[...], din_ref[...], daux_ref[...]))
        dc_ref[...] = dc
        ddba_ref[...] = ddba
        dpar_ref[0:1, :] += da_row
        dpar_ref[1:2, :] += db_row

    wide = pl.BlockSpec((TL, DNW), lambda i: (i, 0))
    return _pc(
        body, grid=(S // TL,), name="dn_local_bwd",
        in_specs=[pl.BlockSpec((TL, CW), lambda i: (i, 0)), pl.BlockSpec((TL, 128), lambda i: (i, 0)),
                  pl.BlockSpec((8, 128), lambda i: (0, 0)), wide, wide, wide, wide,
                  pl.BlockSpec((4, TL, CH), lambda i: (0, i, 0)), pl.BlockSpec((TL, 128), lambda i: (i, 0))],
        out_specs=[pl.BlockSpec((TL, CW), lambda i: (i, 0)), pl.BlockSpec((TL, 128), lambda i: (i, 0)),
                   pl.BlockSpec((8, 128), lambda i: (0, 0))],
        out_shape=[SDS((S, CW), f32), SDS((S, 128), f32), SDS((8, 128), f32)], compiler_params=_cp(56),
    )(c, dba, par, du, dw, dqg, dkd, dintra, daux)


def _dn_step(state, u, w, qg, kd, intra, aux):
    lane = lax.broadcasted_iota(jnp.int32, (CH, 128), 1)
    row = lax.broadcasted_iota(jnp.int32, (CH, 128), 0)
    outs, states = [], []
    for h in range(4):
        sl = slice(DH * h, DH * (h + 1))
        st = state[h]
        e = jnp.sum(jnp.sum(jnp.where((lane == h) & (row == 0), aux, 0.0), axis=1, keepdims=True), axis=0, keepdims=True)
        v_new = u[:, sl] - _dot(w[:, sl].astype(bf16), st.astype(bf16))
        vb = v_new.astype(bf16)
        outs.append(_dot(qg[:, sl].astype(bf16), st.astype(bf16)) + _dot(intra[h].astype(bf16), vb))
        states.append(st * e + _dot_tn(kd[:, sl].astype(bf16), vb))
    return jnp.concatenate(outs, axis=1), jnp.stack(states, axis=0)


CPS = 8
NSTEP = NCHUNK // CPS


def _dn_rec_specs(index):
    wide = pl.BlockSpec((CPS * CH, DNW), lambda n: (index(n), 0))
    inb = pl.BlockSpec((4, CPS * CH, CH), lambda n: (0, index(n), 0))
    auxb = pl.BlockSpec((CPS * CH, 128), lambda n: (index(n), 0))
    stb = pl.BlockSpec((CPS, 4, DH, DH), lambda n: (index(n), 0, 0, 0))
    return wide, inb, auxb, stb


def _dn_rec_fwd(u, w, qg, kd, intra, aux):
    def body(u_ref, w_ref, qg_ref, kd_ref, in_ref, aux_ref, o_ref, st_ref, st_scr):
        @pl.when(pl.program_id(0) == 0)
        def _():
            st_scr[...] = jnp.zeros_like(st_scr)

        st = st_scr[...]
        for k in range(CPS):
            rows = slice(CH * k, CH * (k + 1))
            st_ref[k] = st
            o, st = _dn_step(st, u_ref[rows, :], w_ref[rows, :], qg_ref[rows, :], kd_ref[rows, :], in_ref[:, rows, :],
                             aux_ref[rows, :])
            o_ref[rows, :] = o
        st_scr[...] = st

    wide, inb, auxb, stb = _dn_rec_specs(lambda n: n)
    return _pc(
        body, grid=(NSTEP,), name="dn_rec_fwd", in_specs=[wide, wide, wide, wide, inb, auxb], out_specs=[wide, stb],
        out_shape=[SDS((S, DNW), f32), SDS((NCHUNK, 4, DH, DH), f32)],
        scratch_shapes=[pltpu.VMEM((4, DH, DH), f32)], compiler_params=_cp(32),
    )(u, w, qg, kd, intra, aux)


def _dn_rec_bwd(u, w, qg, kd, intra, aux, states, do):
    def body(u_ref, w_ref, qg_ref, kd_ref, in_ref, aux_ref, st_ref, do_ref,
             du_ref, dw_ref, dqg_ref, dkd_ref, din_ref, daux_ref, ds_scr):
        @pl.when(pl.program_id(0) == 0)
        def _():
            ds_scr[...] = jnp.zeros_like(ds_scr)

        ds = ds_scr[...]
        for k in reversed(range(CPS)):
            rows = slice(CH * k, CH * (k + 1))
            _, vjp = jax.vjp(_dn_step, st_ref[k], u_ref[rows, :], w_ref[rows, :], qg_ref[rows, :], kd_ref[rows, :],
                             in_ref[:, rows, :], aux_ref[rows, :])
            ds, du, dw, dqg, dkd, din, daux = vjp((do_ref[rows, :], ds))
            du_ref[rows, :] = du
            dw_ref[rows, :] = dw
            dqg_ref[rows, :] = dqg
            dkd_ref[rows, :] = dkd
            din_ref[:, rows, :] = din
            daux_ref[rows, :] = daux
        ds_scr[...] = ds

    wide, inb, auxb, stb = _dn_rec_specs(lambda n: NSTEP - 1 - n)
    return _pc(
        body, grid=(NSTEP,), name="dn_rec_bwd", in_specs=[wide, wide, wide, wide, inb, auxb, stb, wide],
        out_specs=[wide, wide, wide, wide, inb, auxb],
        out_shape=[SDS((S, DNW), f32)] * 4 + [SDS((4, S, CH), f32), SDS((S, 128), f32)],
        scratch_shapes=[pltpu.VMEM((4, DH, DH), f32)], compiler_params=_cp(40),
    )(u, w, qg, kd, intra, aux, states, do)


def _dn_post(o, z, nw):
    parts = []
    for h in range(4):
        sl = slice(DH * h, DH * (h + 1))
        oh = o[:, sl]
        y = oh * lax.rsqrt(jnp.mean(oh * oh, axis=-1, keepdims=True) + EPS) * nw
        zh = z[:, sl]
        parts.append(y * (zh * jax.nn.sigmoid(zh)))
    return jnp.concatenate(parts, axis=1)


def _dn_post_fwd(o, z, nw):
    def body(o_ref, z_ref, nw_ref, y_ref):
        y_ref[...] = _dn_post(o_ref[...], z_ref[...], nw_ref[...])

    wide = pl.BlockSpec((TM, DNW), lambda i: (i, 0))
    return _pc(body, grid=(S // TM,), name="dn_post_fwd",
                          in_specs=[wide, wide, pl.BlockSpec((1, 128), lambda i: (0, 0))], out_specs=wide,
                          out_shape=SDS((S, DNW), f32), compiler_params=_cp(32))(o, z, nw)


def _dn_post_bwd(o, z, nw, dy):
    def body(o_ref, z_ref, nw_ref, dy_ref, do_ref, dz_ref, dnw_ref):
        @pl.when(pl.program_id(0) == 0)
        def _():
            dnw_ref[...] = jnp.zeros_like(dnw_ref)

        _, vjp = jax.vjp(_dn_post, o_ref[...], z_ref[...], nw_ref[...])
        do, dz, dnw = vjp(dy_ref[...])
        do_ref[...] = do
        dz_ref[...] = dz
        dnw_ref[...] += dnw

    wide = pl.BlockSpec((TM, DNW), lambda i: (i, 0))
    one = pl.BlockSpec((1, 128), lambda i: (0, 0))
    return _pc(body, grid=(S // TM,), name="dn_post_bwd", in_specs=[wide, wide, one, wide],
                          out_specs=[wide, wide, one], out_shape=[SDS((S, DNW), f32), SDS((S, DNW), f32), SDS((1, 128), f32)],
                          compiler_params=_cp(32))(o, z, nw, dy)


def _row_tile(rows, width, itemsize=4, target=2 * 1024 * 1024):
    best = None
    for t in range(16, rows + 1, 16):
        if rows % t == 0 and t * width * itemsize <= target:
            best = t
    return best if best is not None else rows


def _sum_pieces(pieces, out_dtype, name):
    n, rows, width = pieces.shape
    tr = _row_tile(rows, width * n)

    def body(p_ref, o_ref):
        acc = p_ref[0].astype(f32)
        for s in range(1, n):
            acc = acc + p_ref[s].astype(f32)
        o_ref[...] = acc.astype(out_dtype)

    return _pc(body, grid=(rows // tr,), name=name,
                          in_specs=[pl.BlockSpec((n, tr, width), lambda i: (0, i, 0))],
                          out_specs=pl.BlockSpec((tr, width), lambda i: (i, 0)),
                          out_shape=SDS((rows, width), out_dtype), compiler_params=_cp(32))(pieces)


def _sum_core_pair(part, got, c_arr):
    n, rows, width = part.shape
    half = rows // 2
    tr = _row_tile(half, width, itemsize=2)
    nt = half // tr

    def body(c_ref, p_ref, g_ref, o_ref):
        o_ref[...] = (p_ref[...].astype(f32) + g_ref[...].astype(f32)).astype(bf16)

    gs = pltpu.PrefetchScalarGridSpec(
        num_scalar_prefetch=1, grid=(n, nt),
        in_specs=[pl.BlockSpec((1, tr, width), lambda j, i, c: (j, c[0] * nt + i, 0)),
                  pl.BlockSpec((1, tr, width), lambda j, i, c: (j, i, 0))],
        out_specs=pl.BlockSpec((1, tr, width), lambda j, i, c: (j, i, 0)))
    return _pc(body, grid_spec=gs, name="sum_core_pair", out_shape=SDS((n, half, width), bf16),
                          compiler_params=_cp(32))(c_arr, part, got)


def _sum_chips(pieces, c_arr, full, row0, total_rows):
    n, half, width = pieces.shape
    tr = max(t for t in range(16, 257, 16) if half % t == 0 and row0 % t == 0)
    nt = half // tr

    def body(c_ref, p_ref, *rest):
        o_ref = rest[-1]
        acc = p_ref[0].astype(f32)
        for s in range(1, n):
            acc = acc + p_ref[s].astype(f32)
        o_ref[...] = acc

    gs = pltpu.PrefetchScalarGridSpec(
        num_scalar_prefetch=1, grid=(nt,),
        in_specs=[pl.BlockSpec((n, tr, width), lambda i, c: (0, i, 0))] + ([] if full is None else [ANY]),
        out_specs=pl.BlockSpec((tr, width), lambda i, c: (row0 // tr + c[0] * nt + i, 0)))
    args = (c_arr, pieces) if full is None else (c_arr, pieces, full)
    return _pc(body, grid_spec=gs, name="sum_chips", out_shape=SDS((total_rows, width), f32),
                          input_output_aliases={} if full is None else {2: 0}, compiler_params=_cp(32))(*args)


def _adamw_math(w, g, m, v):
    mn = ADAM_B1 * m + (1.0 - ADAM_B1) * g
    vn = ADAM_B2 * v + (1.0 - ADAM_B2) * (g * g)
    m_hat = mn / (1.0 - ADAM_B1 ** ADAM_STEP)
    v_hat = vn / (1.0 - ADAM_B2 ** ADAM_STEP)
    return -ADAM_LR * (m_hat / (jnp.sqrt(v_hat) + ADAM_EPS) + ADAM_WD * w), mn, vn


def _adamw(w, g, m, v, name):
    rows, width = w.shape
    tr = _row_tile(rows, width * 7, target=12 * 1024 * 1024)

    def body(w_ref, g_ref, m_ref, v_ref, d_ref, nm_ref, nv_ref):
        d_ref[...], nm_ref[...], nv_ref[...] = _adamw_math(w_ref[...], g_ref[...], m_ref[...], v_ref[...])

    blk = pl.BlockSpec((tr, width), lambda i: (i, 0))
    return _pc(body, grid=(rows // tr,), name=name, in_specs=[blk] * 4, out_specs=[blk] * 3,
                          out_shape=[SDS((rows, width), f32)] * 3, compiler_params=_cp(40))(w, g, m, v)


def _adamw_rows(w, m, v, gblob, tr, first_tile, name):
    layers, rows, width = w.shape

    def body(w_ref, g_ref, m_ref, v_ref, d_ref, nm_ref, nv_ref):
        d_ref[0], nm_ref[0], nv_ref[0] = _adamw_math(w_ref[0], g_ref[...], m_ref[0], v_ref[0])

    blk = pl.BlockSpec((1, tr, width), lambda l, i: (l, i, 0))
    gblk = pl.BlockSpec((tr, width), lambda l, i: (first_tile(l) + i, 0))
    return _pc(body, grid=(layers, rows // tr), name=name, in_specs=[blk, gblk, blk, blk], out_specs=[blk] * 3,
                          out_shape=[SDS(w.shape, f32)] * 3, compiler_params=_cp(40))(w, gblob, m, v)


ANY = pl.BlockSpec(memory_space=pl.ANY)


def _place():
    x, y, c = lax.axis_index("x"), lax.axis_index("y"), lax.axis_index("c")
    chips = [(1 - x, y), (x, 1 - y), (1 - x, 1 - y)]
    return x, y, c, chips


NQ_ICI = 4
NQ_D2D = 8


def _chunks(rows, want):
    n = max(k for k in range(1, want + 1) if rows % k == 0 and (rows // k) % 16 == 0)
    step = rows // n
    return [(q * step, step) for q in range(n)]


def _scatter_copies(ins, outs, ssem, rsem, lsem):
    x, y, c, chips = _place()
    me = (x, y, c)
    locals_, sends, lands = [], [], []
    for b in range(len(ins)):
        for q, (off, n) in enumerate(_chunks(ins[b].shape[1], NQ_ICI)):
            rows = pl.ds(off, n)
            mine = outs[b].at[2 * x + y, rows, :]
            locals_.append(pltpu.make_async_copy(ins[b].at[2 * x + y, rows, :], mine, lsem.at[b, q]))
            for j, chip in enumerate(chips):
                sends.append(_remote(ins[b].at[2 * chip[0] + chip[1], rows, :], mine, ssem.at[b, j, q], rsem.at[b, j, q],
                                     (*chip, c)))
                slot = outs[b].at[2 * chip[0] + chip[1], rows, :]
                lands.append(_remote(slot, slot, ssem.at[b, j, q], rsem.at[b, j, q], me))
    return locals_, sends, lands


def _scatter_start(ins, outs, ssem, rsem, lsem):
    locals_, sends, _ = _scatter_copies(ins, outs, ssem, rsem, lsem)
    for cp in locals_ + sends:
        cp.start()


def _scatter_finish(ins, outs, ssem, rsem, lsem):
    locals_, sends, lands = _scatter_copies(ins, outs, ssem, rsem, lsem)
    for cp in lands:
        cp.wait_recv()
    for cp in sends:
        cp.wait_send()
    for cp in locals_:
        cp.wait()


def _scatter_sems(nb):
    return [pltpu.SemaphoreType.DMA((nb, 3, NQ_ICI)), pltpu.SemaphoreType.DMA((nb, 3, NQ_ICI)),
            pltpu.SemaphoreType.DMA((nb, NQ_ICI))]


def _remote(src, dst, ssem, rsem, dev):
    return pltpu.make_async_remote_copy(src_ref=src, dst_ref=dst, send_sem=ssem, recv_sem=rsem, device_id=dev,
                                        device_id_type=MESH)


def _all_gather_weights(shards):
    nb = len(shards)

    def body(*refs):
        ins, outs, sems = refs[:nb], refs[nb:2 * nb], refs[2 * nb:]
        _gather_start(ins, outs, *sems)
        _gather_finish(ins, outs, *sems)

    return _pc(
        body, name="all_gather_weights", in_specs=[ANY] * nb, out_specs=[ANY] * nb,
        out_shape=[SDS((NCH,) + s.shape, s.dtype) for s in shards], scratch_shapes=_gather_sems(nb),
    )(*shards)


def _gather_first(ins, outs, ssem, rsem, lsem):
    x, y, c, chips = _place()
    locals_, sends = [], []
    for b in range(len(ins)):
        half = ins[b].shape[0] // 2
        for q, (off, n) in enumerate(_chunks(half, NQ_ICI)):
            mine = pl.ds(c * half + off, n)
            own = outs[b].at[2 * x + y, mine, :]
            locals_.append(pltpu.make_async_copy(ins[b].at[mine, :], own, lsem.at[b, q]))
            sends.append(_remote(ins[b].at[mine, :], own, ssem.at[b, 0, q], rsem.at[b, 0, q], (x, y, 1 - c)))
            sends += [_remote(ins[b].at[mine, :], own, ssem.at[b, 1 + j, q], rsem.at[b, 1 + j, q], (*chip, c))
                      for j, chip in enumerate(chips)]
    return locals_, sends


def _gather_start(ins, outs, ssem, rsem, lsem):
    locals_, sends = _gather_first(ins, outs, ssem, rsem, lsem)
    for cp in locals_ + sends:
        cp.start()


def _gather_finish(ins, outs, ssem, rsem, lsem):
    x, y, c, chips = _place()
    me, sib = (x, y, c), (x, y, 1 - c)
    locals_, sends = _gather_first(ins, outs, ssem, rsem, lsem)
    for b in range(len(ins)):
        half = ins[b].shape[0] // 2
        for q, (off, n) in enumerate(_chunks(half, NQ_ICI)):
            mine = pl.ds(c * half + off, n)
            for j, chip in enumerate(chips):
                landed = outs[b].at[2 * chip[0] + chip[1], mine, :]
                _remote(landed, landed, ssem.at[b, 1 + j, q], rsem.at[b, 1 + j, q], me).wait_recv()
                cp = _remote(landed, landed, ssem.at[b, 4 + j, q], rsem.at[b, 4 + j, q], sib)
                cp.start()
                sends.append(cp)
    for b in range(len(ins)):
        half = ins[b].shape[0] // 2
        for q, (off, n) in enumerate(_chunks(half, NQ_ICI)):
            other = pl.ds((1 - c) * half + off, n)
            theirs = outs[b].at[2 * x + y, other, :]
            _remote(theirs, theirs, ssem.at[b, 0, q], rsem.at[b, 0, q], me).wait_recv()
            for j, chip in enumerate(chips):
                fwd = outs[b].at[2 * chip[0] + chip[1], other, :]
                _remote(fwd, fwd, ssem.at[b, 4 + j, q], rsem.at[b, 4 + j, q], me).wait_recv()
    for cp in sends:
        cp.wait_send()
    for cp in locals_:
        cp.wait()


def _gather_sems(nb):
    return [pltpu.SemaphoreType.DMA((nb, 7, NQ_ICI)), pltpu.SemaphoreType.DMA((nb, 7, NQ_ICI)),
            pltpu.SemaphoreType.DMA((nb, NQ_ICI))]


def _send_sibling_half(parts):
    nb = len(parts)

    def body(*refs):
        ins, gots = refs[:nb], refs[nb:2 * nb]
        ssem, rsem = refs[2 * nb:]
        x, y, c, _ = _place()
        sib = (x, y, 1 - c)
        todo = []
        for b in range(nb):
            half = ins[b].shape[1] // 2
            for q, (off, n) in enumerate(_chunks(half, NQ_D2D)):
                cp = _remote(ins[b].at[:, pl.ds((1 - c) * half + off, n), :], gots[b].at[:, pl.ds(off, n), :],
                             ssem.at[b, q], rsem.at[b, q], sib)
                cp.start()
                todo.append(cp)
        for cp in todo:
            cp.wait()

    return _pc(
        body, name="send_sibling_half", in_specs=[ANY] * nb, out_specs=[ANY] * nb,
        out_shape=[SDS((p.shape[0], p.shape[1] // 2, p.shape[2]), p.dtype) for p in parts],
        scratch_shapes=[pltpu.SemaphoreType.DMA((nb, NQ_D2D)), pltpu.SemaphoreType.DMA((nb, NQ_D2D))],
    )(*parts)


def _scatter_to_chips(parts):
    nb = len(parts)

    def body(*refs):
        ins, outs, sems = refs[:nb], refs[nb:2 * nb], refs[2 * nb:]
        _scatter_start(ins, outs, *sems)
        _scatter_finish(ins, outs, *sems)

    return _pc(
        body, name="scatter_to_chips", in_specs=[ANY] * nb, out_specs=[ANY] * nb,
        out_shape=[SDS(p.shape, p.dtype) for p in parts], scratch_shapes=_scatter_sems(nb),
    )(*parts)


def _join_halves(fulls, ranges):
    nb = len(fulls)
    nr = max(len(r) for r in ranges)

    def body(*refs):
        ins, outs = refs[:nb], refs[nb:2 * nb]
        ssem, rsem = refs[2 * nb:]
        x, y, c, _ = _place()
        sib = (x, y, 1 - c)
        sends, lands = [], []
        for b in range(nb):
            for g, (row0, rows) in enumerate(ranges[b]):
                half = rows // 2
                for q, (off, n) in enumerate(_chunks(half, NQ_D2D)):
                    mine = pl.ds(row0 + c * half + off, n)
                    sends.append(_remote(ins[b].at[mine, :], outs[b].at[mine, :], ssem.at[b, g, q], rsem.at[b, g, q], sib))
                    other = outs[b].at[pl.ds(row0 + (1 - c) * half + off, n), :]
                    lands.append(_remote(other, other, ssem.at[b, g, q], rsem.at[b, g, q], sib))
        for cp in sends:
            cp.start()
        for cp in lands:
            cp.wait_recv()
        for cp in sends:
            cp.wait_send()

    return _pc(
        body, name="join_halves", in_specs=[ANY] * nb, out_specs=[ANY] * nb,
        out_shape=[SDS(h.shape, h.dtype) for h in fulls], input_output_aliases={b: b for b in range(nb)},
        scratch_shapes=[pltpu.SemaphoreType.DMA((nb, nr, NQ_D2D)), pltpu.SemaphoreType.DMA((nb, nr, NQ_D2D))],
    )(*fulls)


def _gather_small(vec):
    def body(v_ref, o_ref, ssem, rsem, lsem):
        x, y, c, _ = _place()
        mine = o_ref.at[4 * x + 2 * y + c]
        local = pltpu.make_async_copy(v_ref, mine, lsem)
        local.start()
        sends = []
        for k in range(1, 8):
            peer = (x ^ (k >> 2), y ^ ((k >> 1) & 1), c ^ (k & 1))
            cp = _remote(v_ref, mine, ssem.at[k - 1], rsem.at[k - 1], peer)
            cp.start()
            sends.append(cp)
        for k in range(1, 8):
            px, py, pc = x ^ (k >> 2), y ^ ((k >> 1) & 1), c ^ (k & 1)
            slot = o_ref.at[4 * px + 2 * py + pc]
            _remote(slot, slot, ssem.at[k - 1], rsem.at[k - 1], (x, y, c)).wait_recv()
        for cp in sends:
            cp.wait_send()
        local.wait()

    return _pc(
        body, name="gather_small", in_specs=[ANY], out_specs=ANY, out_shape=SDS((8,) + vec.shape, vec.dtype),
        scratch_shapes=[pltpu.SemaphoreType.DMA((7,)), pltpu.SemaphoreType.DMA((7,)), pltpu.SemaphoreType.DMA],
    )(vec)


def _block_diag(pw):
    return jnp.concatenate([jnp.pad(pw[g], ((0, 0), (64 * g, 192 - 64 * g))) for g in range(4)], axis=0)


def _own_columns(full, chip):
    n = full.shape[-1] // NCH
    parts = full.reshape(full.shape[:-1] + (NCH, n))
    sel = (lax.broadcasted_iota(jnp.int32, (NCH, 1), 0) == chip)
    return jnp.sum(jnp.where(sel, parts, 0.0), axis=-2)


def _at_own_columns(shard, chip):
    n = shard.shape[-1]
    sel = (lax.broadcasted_iota(jnp.int32, (NCH * n,), 0) // n == chip)
    return jnp.where(sel, jnp.tile(shard, NCH), 0.0)


def _pad_rows(a, rows):
    return jnp.pad(a, ((0, rows - a.shape[0]),) + ((0, 0),) * (a.ndim - 1))


def _ffn_block(l, which):
    return 7 * l + 3 * which


def _wout_block(l):
    return 7 * l + 6


class _Weights:
    def __init__(self):
        self.ffn, self.wout, self.w_aug, self.rides = {}, {}, {}, {}

    @classmethod
    def from_blob(cls, blob, w_aug):
        self = cls()
        for l in range(DEPTH):
            self.ffn[(l, 0)], self.ffn[(l, 1)] = (blob, _ffn_block(l, 0)), (blob, _ffn_block(l, 1))
            self.wout[l], self.w_aug[l] = (blob, _wout_block(l)), w_aug[l]
        return self

    def set_w_in(self, l, gathered):
        self.w_aug[l] = jnp.pad(gathered.transpose(1, 0, 2).reshape(D, INW), ((0, 0), (0, INP - INW)))

    def ffn_fwd(self, l, which, x, nw):
        arr, k0 = self.ffn[(l, which)]
        if (l, which) not in self.rides:
            return _ffn_fwd(x, nw, arr, k0)
        shards, landed = self.rides[(l, which)]
        out, *gathered = _ffn_fwd(x, nw, arr, k0, shards)
        landed(gathered)
        return out


def _layer_fwd(l, x0, pos, freq, wts, ws):
    sv = {"x0": x0}
    x1 = ws.ffn_fwd(l, 0, x0, wts["ffn1_norm"][l:l + 1])
    att, att4, att16, pu, dq, dz, dba = _inproj_fwd(x1, wts["mix_norm"][l:l + 1], ws.w_aug[l], pos, freq)
    qkvs = [att, att4.reshape(S, 768), att16.reshape(S, 768)]
    (o1, l1), (o4, l4), (o16, l16) = [_attn_fwd(q, NB // d) for q, d in zip(qkvs, PATTERN_DIL)]
    ols = (o1, l1, o4.reshape(4, S // 4, ATT), l4.reshape(4, S // 4, ATT), o16.reshape(16, S // 16, ATT),
           l16.reshape(16, S // 16, ATT))
    ya = _merge_fwd(*ols)
    yb = _pool_fwd(pu, wts["pool_bd"][l], wts["pool_scale"][l:l + 1])
    c = _conv_fwd(dq, wts["conv_w"][l])
    u, w, qg, kd, intra, aux = _dn_local_fwd(c, dba, wts["dn_par"][l])
    o_dn, states = _dn_rec_fwd(u, w, qg, kd, intra, aux)
    yc = _dn_post_fwd(o_dn, dz, wts["dn_out_norm"][l:l + 1])
    x2 = _outproj_fwd(x1, ya, yb, yc, *ws.wout[l])
    x3 = ws.ffn_fwd(l, 1, x2, wts["ffn2_norm"][l:l + 1])
    sv.update(x1=x1, x2=x2, qkvs=qkvs, ols=ols, ya=ya, yb=yb, yc=yc, pu=pu, dq=dq, dz=dz, dba=dba, c=c,
              u=u, w=w, qg=qg, kd=kd, intra=intra, aux=aux, states=states, o_dn=o_dn)
    return x3, sv


def _wout_part(g):
    return jnp.pad(g.astype(bf16).reshape(NCH, 256, D), ((0, 0), (0, FC - 256), (0, 0)))


def _win_part(g):
    return g[:, :INW].astype(bf16).reshape(D, NCH, INC).transpose(1, 0, 2)


def _layer_bwd(l, dx3, sv, pos, freq, wts, ws, ride=None, prep=None):
    gr = {}
    g2, u2, d2, dh4, *pieces_before = _ffn_bwd(sv["x2"], wts["ffn2_norm"][l:l + 1], *ws.ffn[(l, 1)], dx3, ride)
    gr.update(ffn2_w_gate=g2, ffn2_w_up=u2, ffn2_w_down=d2)
    dx2, gr["ffn2_norm"], dya, dyb, dyc, gr["w_out"] = _outproj_bwd(sv["x2"], wts["ffn2_norm"][l:l + 1], dx3, dh4, sv["ya"],
                                                                     sv["yb"], sv["yc"], *ws.wout[l])
    do_dn, ddz, gr["dn_out_norm"] = _dn_post_bwd(sv["o_dn"], sv["dz"], wts["dn_out_norm"][l:l + 1], dyc)
    du, dw, dqg, dkd, dintra, daux = _dn_rec_bwd(sv["u"], sv["w"], sv["qg"], sv["kd"], sv["intra"], sv["aux"], sv["states"], do_dn)
    dc, ddba, gr["dn_par"] = _dn_local_bwd(sv["c"], sv["dba"], wts["dn_par"][l], du, dw, dqg, dkd, dintra, daux)
    ddq, gr["conv_w"] = _conv_bwd(sv["dq"], wts["conv_w"][l], dc)
    dpu, gr["pool_bd"], gr["pool_scale"] = _pool_bwd(sv["pu"], wts["pool_bd"][l], wts["pool_scale"][l:l + 1], dyb)
    dols = _merge_bwd(*sv["ols"], dya)
    datts = [_attn_bwd(q, dols[2 * p].reshape(S, ATT), dols[2 * p + 1].reshape(S, ATT), NB // d)
             for p, (q, d) in enumerate(zip(sv["qkvs"], PATTERN_DIL))]
    dx1, gr["mix_norm"], gr["w_aug"] = _inproj_bwd(sv["x1"], wts["mix_norm"][l:l + 1], ws.w_aug[l], pos, freq, dx2,
                                                    datts[0], datts[1].reshape(4, S // 4, 768),
                                                    datts[2].reshape(16, S // 16, 768), dpu, ddq, ddz, ddba)
    own = None
    if prep is not None:
        own = prep([jnp.concatenate([g2, u2, d2, _wout_part(gr["w_out"])], axis=1), _win_part(gr["w_aug"])])
    g1, u1, d1, dh4, *pieces_own = _ffn_bwd(sv["x0"], wts["ffn1_norm"][l:l + 1], *ws.ffn[(l, 0)], dx1, own)
    dx0, gr["ffn1_norm"] = _norm_bwd(sv["x0"], wts["ffn1_norm"][l:l + 1], dx1, dh4)
    gr.update(ffn1_w_gate=g1, ffn1_w_up=u1, ffn1_w_down=d1)
    return dx0, gr, pieces_before, pieces_own


def _device_step(x, pos, target, wts, ws, prep=None):
    freq = jnp.tile(ROPE_THETA ** (-jnp.arange(0, EH, 2, dtype=f32) / EH), 2 * ATT // EH).reshape(1, ATT)
    saved = []
    h = x
    for l in range(DEPTH):
        h, sv = _layer_fwd(l, h, pos, freq, wts, ws)
        saved.append(sv)
    dh, g_final, loss = _final(h, wts["final_norm"], target)
    grads = [None] * DEPTH
    dh, grads[1], _, _ = _layer_bwd(1, dh, saved[1], pos, freq, wts, ws)
    sums1 = None
    if prep is not None:
        g = grads[1]
        ffn = [g[f"ffn{f}_w_{n}"] for f in (1, 2) for n in ("gate", "up", "down")]
        sums1 = prep([jnp.concatenate(ffn + [_wout_part(g["w_out"])], axis=1), _win_part(g["w_aug"])])
    dh, grads[0], pieces1, pieces0 = _layer_bwd(0, dh, saved[0], pos, freq, wts, ws, sums1, prep)
    return loss, dh, g_final, grads, pieces1, pieces0


_SMALL = (("ffn1_norm", (DEPTH, D)), ("mix_norm", (DEPTH, D)), ("pool_w", (DEPTH, 4, 64, 64)), ("pool_scale", (DEPTH, 256)),
          ("dn_conv_w", (DEPTH, 4, CW)), ("dn_a_log", (DEPTH, 4)), ("dn_dt_bias", (DEPTH, 4)), ("dn_out_norm", (DEPTH, 128)),
          ("ffn2_norm", (DEPTH, D)), ("final_norm", (D,)), ("loss", (1,)))


def _pack_small(vals):
    rows = []
    for name, shape in _SMALL:
        flat = vals[name].astype(f32).reshape(-1)
        rows.append(jnp.pad(flat, (0, _small_rows(shape) * 128 - flat.shape[0])).reshape(-1, 128))
    out = jnp.concatenate(rows, axis=0)
    return _pad_rows(out, -(-out.shape[0] // 16) * 16)


def _small_rows(shape):
    return -(-int(np.prod(shape)) // 1024) * 8


def _unpack_small(packed):
    vals, r = {}, 0
    for name, shape in _SMALL:
        size, n = int(np.prod(shape)), _small_rows(shape)
        vals[name] = packed[r:r + n].reshape(-1)[:size].reshape(shape)
        r += n
    return vals


def kernel(x, positions, ffn1_norm, ffn1_w_gate, ffn1_w_up, ffn1_w_down, mix_norm, w_in, pool_w, pool_scale, dn_conv_w, dn_a_log, dn_dt_bias, dn_out_norm, w_out, ffn2_norm, ffn2_w_gate, ffn2_w_up, ffn2_w_down, final_norm, loss_target, m_ffn1_norm, m_ffn1_w_gate, m_ffn1_w_up, m_ffn1_w_down, m_mix_norm, m_w_in, m_pool_w, m_pool_scale, m_dn_conv_w, m_dn_a_log, m_dn_dt_bias, m_dn_out_norm, m_w_out, m_ffn2_norm, m_ffn2_w_gate, m_ffn2_w_up, m_ffn2_w_down, m_final_norm, v_ffn1_norm, v_ffn1_w_gate, v_ffn1_w_up, v_ffn1_w_down, v_mix_norm, v_w_in, v_pool_w, v_pool_scale, v_dn_conv_w, v_dn_a_log, v_dn_dt_bias, v_dn_out_norm, v_w_out, v_ffn2_norm, v_ffn2_w_gate, v_ffn2_w_up, v_ffn2_w_down, v_final_norm):
    names = ["ffn1_norm", "ffn1_w_gate", "ffn1_w_up", "ffn1_w_down", "mix_norm", "w_in", "pool_w", "pool_scale", "dn_conv_w",
             "dn_a_log", "dn_dt_bias", "dn_out_norm", "w_out", "ffn2_norm", "ffn2_w_gate", "ffn2_w_up", "ffn2_w_down", "final_norm"]
    W = dict(zip(names, [ffn1_norm, ffn1_w_gate, ffn1_w_up, ffn1_w_down, mix_norm, w_in, pool_w, pool_scale, dn_conv_w,
                         dn_a_log, dn_dt_bias, dn_out_norm, w_out, ffn2_norm, ffn2_w_gate, ffn2_w_up, ffn2_w_down, final_norm]))
    M = dict(zip(names, [m_ffn1_norm, m_ffn1_w_gate, m_ffn1_w_up, m_ffn1_w_down, m_mix_norm, m_w_in, m_pool_w, m_pool_scale,
                         m_dn_conv_w, m_dn_a_log, m_dn_dt_bias, m_dn_out_norm, m_w_out, m_ffn2_norm, m_ffn2_w_gate, m_ffn2_w_up,
                         m_ffn2_w_down, m_final_norm]))
    V = dict(zip(names, [v_ffn1_norm, v_ffn1_w_gate, v_ffn1_w_up, v_ffn1_w_down, v_mix_norm, v_w_in, v_pool_w, v_pool_scale,
                         v_dn_conv_w, v_dn_a_log, v_dn_dt_bias, v_dn_out_norm, v_w_out, v_ffn2_norm, v_ffn2_w_gate, v_ffn2_w_up,
                         v_ffn2_w_down, v_final_norm]))
    chip = 2 * lax.axis_index("x") + lax.axis_index("y")

    ffn_names = [(f"ffn{f}_w_gate", f"ffn{f}_w_up", f"ffn{f}_w_down") for f in (1, 2)]
    tr = lambda t: jnp.swapaxes(t, -1, -2)
    def ffn_rows(l, which):
        g, u, dn = ffn_names[which]
        return [tr(W[g][l]), tr(W[u][l]), W[dn][l]]

    def second_half(l):
        return jnp.concatenate(ffn_rows(l, 1) + [jnp.pad(W["w_out"][l], ((0, FC - 256), (0, 0)))], axis=0).astype(bf16)

    ws = _Weights()
    first0, = _all_gather_weights([jnp.concatenate(ffn_rows(0, 0), axis=0).astype(bf16)])
    ws.ffn[(0, 0)] = (first0, 0)

    def landed_00(gathered):
        ws.ffn[(0, 1)], ws.wout[0] = (gathered[0], 0), (gathered[0], 3)
        ws.set_w_in(0, gathered[1])

    def landed_01(gathered):
        ws.ffn[(1, 0)] = (gathered[0], 0)
        ws.set_w_in(1, gathered[1])

    def landed_10(gathered):
        ws.ffn[(1, 1)], ws.wout[1] = (gathered[0], 0), (gathered[0], 3)

    ws.rides[(0, 0)] = ([second_half(0), W["w_in"][0].astype(bf16)], landed_00)
    ws.rides[(0, 1)] = ([jnp.concatenate(ffn_rows(1, 0), axis=0).astype(bf16), W["w_in"][1].astype(bf16)], landed_01)
    ws.rides[(1, 0)] = ([second_half(1)], landed_10)
    conv_all = _gather_small(_pad_rows(dn_conv_w.reshape(DEPTH * 4 * (CW // NCH) // 128, 128), 32))
    conv_full = jnp.concatenate([conv_all[2 * j, :DEPTH * 4 * (CW // NCH) // 128].reshape(DEPTH, 4, CW // NCH) for j in range(NCH)],
                                axis=-1)

    par = jnp.pad(jnp.stack([dn_a_log, dn_dt_bias], axis=1), ((0, 0), (0, 6), (4, 120)))
    wts = dict(ffn1_norm=ffn1_norm, mix_norm=mix_norm, ffn2_norm=ffn2_norm, final_norm=final_norm.reshape(1, D),
               pool_bd=jnp.stack([_block_diag(pool_w[l]) for l in range(DEPTH)]).astype(bf16),
               pool_scale=pool_scale, conv_w=jnp.pad(conv_full, ((0, 0), (0, 4), (0, 0))),
               dn_par=par, dn_out_norm=dn_out_norm)

    c_arr = lax.axis_index("c").astype(jnp.int32).reshape(1)

    def prep(parts):
        return [_sum_core_pair(p, g, c_arr) for p, g in zip(parts, _send_sibling_half(parts))]

    loss, dx, g_final, grads, pieces1, pieces0 = _device_step(x[0], positions.reshape(S, 1), loss_target[0], wts, ws, prep)
    last = [jnp.concatenate([grads[0][n] for n in ffn_names[0]], axis=1)]
    pieces_last = _scatter_to_chips(prep(last))
    full_b = _sum_chips(pieces_last[0], c_arr, None, 0, RB)
    full_b = _sum_chips(pieces0[0], c_arr, full_b, 3 * FC, RB)
    full_b = _sum_chips(pieces1[0], c_arr, full_b, 7 * FC, RB)
    full_c = _sum_chips(pieces0[1], c_arr, None, 0, RC)
    full_c = _sum_chips(pieces1[1], c_arr, full_c, D, RC)
    full_b, full_c = _join_halves([full_b, full_c], [[(0, 3 * FC), (3 * FC, 4 * FC), (7 * FC, 7 * FC)], [(0, D), (D, D)]])

    small = {"loss": loss[0, 0:1], "final_norm": g_final.reshape(D)}
    for n in ("ffn1_norm", "mix_norm", "ffn2_norm", "pool_scale", "dn_out_norm"):
        small[n] = jnp.stack([grads[l][n].reshape(-1) for l in range(DEPTH)])
    small["pool_w"] = jnp.stack([jnp.stack([grads[l]["pool_bd"][64 * g:64 * (g + 1), 64 * g:64 * (g + 1)] for g in range(4)])
                                 for l in range(DEPTH)])
    small["dn_conv_w"] = jnp.stack([grads[l]["conv_w"][0:4] for l in range(DEPTH)])
    small["dn_a_log"] = jnp.stack([grads[l]["dn_par"][0, 4:8] for l in range(DEPTH)])
    small["dn_dt_bias"] = jnp.stack([grads[l]["dn_par"][1, 4:8] for l in range(DEPTH)])
    packed = _pack_small(small)
    g_small = _sum_pieces(_gather_small(packed), f32, "sum_small")
    gs = _unpack_small(g_small)

    transposed = ("ffn1_w_gate", "ffn1_w_up", "ffn2_w_gate", "ffn2_w_up")
    where = {"ffn1_w_gate": (full_b, FC // 2, lambda l: 14 * l), "ffn1_w_up": (full_b, FC // 2, lambda l: 14 * l + 2),
             "ffn1_w_down": (full_b, FC // 2, lambda l: 14 * l + 4), "ffn2_w_gate": (full_b, FC // 2, lambda l: 14 * l + 6),
             "ffn2_w_up": (full_b, FC // 2, lambda l: 14 * l + 8), "ffn2_w_down": (full_b, FC // 2, lambda l: 14 * l + 10),
             "w_out": (full_b, 64, lambda l: (FC // 64) * (7 * l + 6)), "w_in": (full_c, D // 2, lambda l: 2 * l)}
    big_res = {}
    for n, (gblob, tile, first) in where.items():
        t = tr if n in transposed else (lambda a: a)
        big_res[n] = [t(r) for r in _adamw_rows(t(W[n]), t(M[n]), t(V[n]), gblob, tile, first, "adamw_" + n)]

    def small_of(T):
        d = {n: T[n] for n, _ in _SMALL if n not in ("loss", "dn_conv_w")}
        d["loss"] = jnp.zeros((1,), f32)
        d["dn_conv_w"] = _at_own_columns(T["dn_conv_w"], chip)
        return _pack_small(d)

    res_s = _adamw(small_of(W), g_small, small_of(M), small_of(V), "adamw_small")
    small_out = [_unpack_small(r) for r in res_s]

    def split_blobs(b, c):
        out = {}
        b7 = b.reshape(DEPTH, 7, FC, D)
        for k, n in enumerate(n for names3 in ffn_names for n in names3):
            out[n] = tr(b7[:, k]) if n in transposed else b7[:, k]
        out["w_out"] = b7[:, 6, :256]
        out["w_in"] = c.reshape(DEPTH, D, INC)
        return out

    def assemble(big, sm):
        out = []
        for n in names:
            if n in big:
                out.append(big[n])
            elif n == "dn_conv_w":
                out.append(_own_columns(sm[n], chip))
            else:
                out.append(sm[n])
        return out

    grad_list = assemble(split_blobs(full_b, full_c), gs)
    outs = [gs["loss"].reshape(()), dx.reshape(1, S, D)] + grad_list
    for k in range(3):
        outs += assemble({n: r[k] for n, r in big_res.items()}, small_out[k])
    return tuple(outs)
```

```python
import functools
import math

import jax
import jax.numpy as jnp
import numpy as np
from jax import lax
from jax.experimental import pallas as pl
from jax.experimental.pallas import tpu as pltpu

f32 = jnp.float32
bf16 = jnp.bfloat16
SDS = jax.ShapeDtypeStruct
MESH = pl.DeviceIdType.MESH

S = 4096
D = 1024
DEPTH = 2
FF = 2816
NCH = 4
FC = FF // NCH
INW = 3080
INC = INW // NCH
INP = 3200
ATT = 256
EH = 64
NBLK = 128
DNW = 512
DH = 128
CH = 64
NCHUNK = S // CH
EPS = 1e-6
ROPE_THETA = 10000.0
PATTERN_DIL = (1, 4, 16)
ADAM_LR, ADAM_B1, ADAM_B2, ADAM_EPS, ADAM_WD, ADAM_STEP = 0.001, 0.9, 0.999, 1e-08, 0.01, 10
VMEM_BYTES_V7X = 64 * 1024 * 1024
NEG = -1e30

TM = 512
RB, RC = 14 * FC, 2 * D


def _cp(vmem_mb=48, sem=None):
    kw = dict(vmem_limit_bytes=vmem_mb * 1024 * 1024)
    if sem is not None:
        kw["dimension_semantics"] = sem
    return pltpu.CompilerParams(**kw)


def _pc(*args, **kwargs):
    pin = lambda s: pltpu.HBM(s.shape, s.dtype) if isinstance(s, SDS) and jnp.issubdtype(s.dtype, jnp.floating) else s
    out = kwargs["out_shape"]
    kwargs["out_shape"] = [pin(s) for s in out] if isinstance(out, (list, tuple)) else pin(out)
    call = pl.pallas_call(*args, **kwargs)

    def run(*operands):
        pinned = [pltpu.with_memory_space_constraint(o, pltpu.HBM) if jnp.issubdtype(o.dtype, jnp.floating) else o
                  for o in operands]
        return call(*pinned)

    return run


def _dot(a, b):
    return jnp.dot(a, b, preferred_element_type=f32)


def _dot_nt(a, b):
    return lax.dot_general(a, b, (((1,), (1,)), ((), ())), preferred_element_type=f32)


def _dot_tn(a, b):
    return lax.dot_general(a, b, (((0,), (0,)), ((), ())), preferred_element_type=f32)


def _rms(x, w):
    r = lax.rsqrt(jnp.mean(x * x, axis=-1, keepdims=True) + EPS)
    return x * r * w, r


def _rms_bwd(x, w, r, dh):
    xhat = x * r
    dw = jnp.sum(dh * xhat, axis=0, keepdims=True)
    dxh = dh * w
    dx = r * (dxh - xhat * jnp.mean(dxh * xhat, axis=-1, keepdims=True))
    return dx, dw


def _ffn_fwd(x, nw, blob, k0, ride=None):
    kg, ku, kd = k0, k0 + 1, k0 + 2
    nr = 0 if ride is None else len(ride)
    ni = S // TM

    def body(*refs):
        x_ref, nw_ref, wg_ref, wu_ref, wd_ref = refs[:5]
        ride_in = refs[5:5 + nr]
        o_ref = refs[5 + nr]
        ride_out = refs[6 + nr:6 + 2 * nr]
        h_scr, acc_scr = refs[6 + 2 * nr:8 + 2 * nr]
        sems = refs[8 + 2 * nr:]
        i = pl.program_id(0)
        j = pl.program_id(1)

        if nr:
            @pl.when(jnp.logical_and(i == 0, j == 0))
            def _():
                _gather_start(ride_in, ride_out, *sems)

        @pl.when(j == 0)
        def _():
            h, _ = _rms(x_ref[...], nw_ref[...])
            h_scr[...] = h.astype(bf16)
            acc_scr[...] = jnp.zeros_like(acc_scr)

        h = h_scr[...]
        g = _dot_nt(h, wg_ref[0])
        u = _dot_nt(h, wu_ref[0])
        a = (g * jax.nn.sigmoid(g) * u).astype(bf16)
        acc_scr[...] += _dot(a, wd_ref[0])

        @pl.when(j == NCH - 1)
        def _():
            o_ref[...] = x_ref[...] + 0.5 * acc_scr[...]

        if nr:
            @pl.when(jnp.logical_and(i == ni - 1, j == NCH - 1))
            def _():
                _gather_finish(ride_in, ride_out, *sems)

    wspec = lambda k: pl.BlockSpec((1, FC, D), lambda i, j: (j, k, 0))
    rides = [] if ride is None else list(ride)
    res = _pc(
        body, grid=(ni, NCH), name="ffn_fwd_ride" if nr else "ffn_fwd",
        in_specs=[pl.BlockSpec((TM, D), lambda i, j: (i, 0)),
                  pl.BlockSpec((1, D), lambda i, j: (0, 0)),
                  wspec(kg), wspec(ku), wspec(kd)] + [ANY] * nr,
        out_specs=[pl.BlockSpec((TM, D), lambda i, j: (i, 0))] + [ANY] * nr,
        out_shape=[SDS((S, D), f32)] + [SDS((NCH,) + r.shape, r.dtype) for r in rides],
        scratch_shapes=[pltpu.VMEM((TM, D), bf16), pltpu.VMEM((TM, D), f32)] + (_gather_sems(nr) if nr else []),
        compiler_params=_cp(40),
    )(x, nw, blob, blob, blob, *rides)
    return res if nr else res[0]


def _ffn_bwd(x, nw, blob, k0, dy, ride=None):
    nt = S // TM
    kg, ku, kd = k0, k0 + 1, k0 + 2
    nr = 0 if ride is None else len(ride)

    def body(*refs):
        x_ref, nw_ref, wg_ref, wu_ref, wd_ref, dy_ref = refs[:6]
        ride_in = refs[6:6 + nr]
        dwg_ref, dwu_ref, dwd_ref, dh_ref = refs[6 + nr:10 + nr]
        ride_out = refs[10 + nr:10 + 2 * nr]
        ag, au, ad = refs[10 + 2 * nr:13 + 2 * nr]
        sems = refs[13 + 2 * nr:]
        j = pl.program_id(0)
        i = pl.program_id(1)

        if nr:
            @pl.when(jnp.logical_and(j == 0, i == 0))
            def _():
                _scatter_start(ride_in, ride_out, *sems)

        @pl.when(i == 0)
        def _():
            ag[...] = jnp.zeros_like(ag)
            au[...] = jnp.zeros_like(au)
            ad[...] = jnp.zeros_like(ad)

        hf, _ = _rms(x_ref[...], nw_ref[...])
        h = hf.astype(bf16)
        g = _dot_nt(h, wg_ref[0])
        u = _dot_nt(h, wu_ref[0])
        sg = jax.nn.sigmoid(g)
        s = g * sg
        a = (s * u).astype(bf16)
        dyb = (0.5 * dy_ref[...]).astype(bf16)
        da = _dot_nt(dyb, wd_ref[0])
        ad[...] += _dot_tn(a, dyb)
        du = (da * s).astype(bf16)
        dg = (da * u * (sg * (1.0 + g * (1.0 - sg)))).astype(bf16)
        ag[...] += _dot_tn(dg, h)
        au[...] += _dot_tn(du, h)
        dh_ref[0] = (_dot(dg, wg_ref[0]) + _dot(du, wu_ref[0])).astype(bf16)

        @pl.when(i == nt - 1)
        def _():
            dwg_ref[0] = ag[...].astype(bf16)
            dwu_ref[0] = au[...].astype(bf16)
            dwd_ref[0] = ad[...].astype(bf16)

        if nr:
            @pl.when(jnp.logical_and(j == NCH - 1, i == nt - 1))
            def _():
                _scatter_finish(ride_in, ride_out, *sems)

    wspec = lambda k: pl.BlockSpec((1, FC, D), lambda j, i: (j, k, 0))
    gspec = pl.BlockSpec((1, FC, D), lambda j, i: (j, 0, 0))
    rides = [] if ride is None else list(ride)
    return _pc(
        body, grid=(NCH, nt), name="ffn_bwd_ride" if nr else "ffn_bwd",
        in_specs=[pl.BlockSpec((TM, D), lambda j, i: (i, 0)),
                  pl.BlockSpec((1, D), lambda j, i: (0, 0)),
                  wspec(kg), wspec(ku), wspec(kd),
                  pl.BlockSpec((TM, D), lambda j, i: (i, 0))] + [ANY] * nr,
        out_specs=[gspec, gspec, gspec, pl.BlockSpec((1, TM, D), lambda j, i: (j, i, 0))] + [ANY] * nr,
        out_shape=[SDS((NCH, FC, D), bf16)] * 3 + [SDS((NCH, S, D), bf16)] + [SDS(r.shape, r.dtype) for r in rides],
        scratch_shapes=[pltpu.VMEM((FC, D), f32)] * 3 + (_scatter_sems(nr) if nr else []),
        compiler_params=_cp(56),
    )(x, nw, blob, blob, blob, dy, *rides)


def _norm_bwd(x, nw, dres, dh4):
    nt = S // TM
    nparts = dh4.shape[0]

    def body(x_ref, nw_ref, dres_ref, dh_ref, dx_ref, dnw_ref):
        i = pl.program_id(0)
        dh = dh_ref[0].astype(f32)
        for p in range(1, nparts):
            dh = dh + dh_ref[p].astype(f32)
        xv = x_ref[...]
        _, r = _rms(xv, nw_ref[...])
        dx, dw = _rms_bwd(xv, nw_ref[...], r, dh)
        dx_ref[...] = dres_ref[...] + dx

        @pl.when(i == 0)
        def _():
            dnw_ref[...] = jnp.zeros_like(dnw_ref)

        dnw_ref[...] += dw

    return _pc(
        body, grid=(nt,), name="norm_bwd",
        in_specs=[pl.BlockSpec((TM, D), lambda i: (i, 0)),
                  pl.BlockSpec((1, D), lambda i: (0, 0)),
                  pl.BlockSpec((TM, D), lambda i: (i, 0)),
                  pl.BlockSpec((nparts, TM, D), lambda i: (0, i, 0))],
        out_specs=[pl.BlockSpec((TM, D), lambda i: (i, 0)), pl.BlockSpec((1, D), lambda i: (0, 0))],
        out_shape=[SDS((S, D), f32), SDS((1, D), f32)],
        compiler_params=_cp(40),
    )(x, nw, dres, dh4)


def _final(x, nw, target):
    nt = S // TM

    def body(x_ref, nw_ref, t_ref, dx_ref, dnw_ref, loss_ref):
        i = pl.program_id(0)
        xv = x_ref[...]
        y, r = _rms(xv, nw_ref[...])
        err = y - t_ref[...]
        part = 0.5 * jnp.sum(jnp.mean(err * err, axis=-1, keepdims=True), axis=0, keepdims=True)
        dx, dw = _rms_bwd(xv, nw_ref[...], r, err * (1.0 / D))
        dx_ref[...] = dx

        @pl.when(i == 0)
        def _():
            dnw_ref[...] = jnp.zeros_like(dnw_ref)
            loss_ref[...] = jnp.zeros_like(loss_ref)

        dnw_ref[...] += dw
        loss_ref[...] += jnp.broadcast_to(part, loss_ref.shape)

    return _pc(
        body, grid=(nt,), name="final_loss",
        in_specs=[pl.BlockSpec((TM, D), lambda i: (i, 0)),
                  pl.BlockSpec((1, D), lambda i: (0, 0)),
                  pl.BlockSpec((TM, D), lambda i: (i, 0))],
        out_specs=[pl.BlockSpec((TM, D), lambda i: (i, 0)), pl.BlockSpec((1, D), lambda i: (0, 0)),
                   pl.BlockSpec((1, 128), lambda i: (0, 0))],
        out_shape=[SDS((S, D), f32), SDS((1, D), f32), SDS((1, 128), f32)],
        compiler_params=_cp(40),
    )(x, nw, target)


def _rot_half(t):
    lane = lax.broadcasted_iota(jnp.int32, t.shape, 1)
    first = (lane % EH) < (EH // 2)
    return jnp.where(first, -pltpu.roll(t, ATT - EH // 2, 1), pltpu.roll(t, EH // 2, 1))


def _rope_tables(pos_ref, freq_ref):
    ang = pos_ref[...].astype(f32) * freq_ref[...]
    return jnp.cos(ang), jnp.sin(ang)


def _split_residues(val, scr, outs):
    rows, cols = val.shape
    for j in range(cols // 128):
        scr[j] = val[:, 128 * j:128 * (j + 1)]
    for ref, d in outs:
        for j in range(cols // 128):
            for r in range(d):
                ref.at[r][:, 128 * j:128 * (j + 1)] = scr.at[j][pl.ds(r, rows // d, stride=d), :]


def _join_residues(ref, d, scr):
    rows, cols = scr.shape[1], ref.shape[2]
    for j in range(cols // 128):
        for r in range(d):
            scr.at[j][pl.ds(r, rows // d, stride=d), :] = ref.at[r][:, 128 * j:128 * (j + 1)]
    return jnp.concatenate([scr[j] for j in range(cols // 128)], axis=1)


def _res_spec(d, tile, cols):
    return pl.BlockSpec((d, tile // d, cols), lambda i: (0, i, 0))


def _inproj_fwd(x, nw, w_aug, pos, freq):
    TI = 256

    def body(x_ref, nw_ref, w_hbm, pos_ref, freq_ref, att_ref, att4_ref, att16_ref, pu_ref, dq_ref, dz_ref, dba_ref,
             w_scr, r_scr):
        @pl.when(pl.program_id(0) == 0)
        def _():
            pltpu.sync_copy(w_hbm, w_scr)

        h, _ = _rms(x_ref[...], nw_ref[...])
        proj = _dot(h.astype(bf16), w_scr[...])
        cos, sin = _rope_tables(pos_ref, freq_ref)
        q = proj[:, 0:ATT]
        k = proj[:, ATT:2 * ATT]
        att = jnp.concatenate([q * cos + _rot_half(q) * sin, k * cos + _rot_half(k) * sin, proj[:, 2 * ATT:3 * ATT]], axis=1)
        att_ref[...] = att
        _split_residues(att, r_scr, [(att4_ref, 4), (att16_ref, 16)])
        pu_ref[...] = proj[:, 768:1024]
        dq_ref[...] = proj[:, 1024:2560]
        dz_ref[...] = proj[:, 2560:3072]
        dba_ref[...] = proj[:, 3072:3200]

    return _pc(
        body, grid=(S // TI,), name="inproj_fwd",
        in_specs=[pl.BlockSpec((TI, D), lambda i: (i, 0)),
                  pl.BlockSpec((1, D), lambda i: (0, 0)),
                  pl.BlockSpec(memory_space=pl.ANY),
                  pl.BlockSpec((TI, 1), lambda i: (i, 0)),
                  pl.BlockSpec((1, ATT), lambda i: (0, 0))],
        out_specs=[pl.BlockSpec((TI, 768), lambda i: (i, 0)), _res_spec(4, TI, 768), _res_spec(16, TI, 768),
                   pl.BlockSpec((TI, 256), lambda i: (i, 0)),
                   pl.BlockSpec((TI, 1536), lambda i: (i, 0)), pl.BlockSpec((TI, 512), lambda i: (i, 0)),
                   pl.BlockSpec((TI, 128), lambda i: (i, 0))],
        out_shape=[SDS((S, 768), f32), SDS((4, S // 4, 768), f32), SDS((16, S // 16, 768), f32), SDS((S, 256), f32),
                   SDS((S, 1536), f32), SDS((S, 512), f32), SDS((S, 128), f32)],
        scratch_shapes=[pltpu.VMEM((D, INP), bf16), pltpu.VMEM((6, TI, 128), f32)],
        compiler_params=_cp(48),
    )(x, nw, w_aug, pos, freq)


def _inproj_bwd(x, nw, w_aug, pos, freq, dres, datt, datt4, datt16, dpu, ddq, ddz, ddba):
    TI = 256
    nt = S // TI

    def body(x_ref, nw_ref, w_hbm, pos_ref, freq_ref, dres_ref, datt_ref, datt4_ref, datt16_ref, dpu_ref, ddq_ref, ddz_ref,
             ddba_ref, dx_ref, dnw_ref, dw_hbm, w_scr, acc, r_scr):
        i = pl.program_id(0)

        @pl.when(i == 0)
        def _():
            pltpu.sync_copy(w_hbm, w_scr)
            acc[...] = jnp.zeros_like(acc)
            dnw_ref[...] = jnp.zeros_like(dnw_ref)

        xv = x_ref[...]
        hf, r = _rms(xv, nw_ref[...])
        h = hf.astype(bf16)
        cos, sin = _rope_tables(pos_ref, freq_ref)
        datt = datt_ref[...] + _join_residues(datt4_ref, 4, r_scr)
        datt = datt + _join_residues(datt16_ref, 16, r_scr)
        dq = datt[:, 0:ATT]
        dk = datt[:, ATT:2 * ATT]
        dq = dq * cos - _rot_half(dq) * sin
        dk = dk * cos - _rot_half(dk) * sin
        dproj = jnp.concatenate([dq, dk, datt[:, 2 * ATT:3 * ATT], dpu_ref[...], ddq_ref[...], ddz_ref[...], ddba_ref[...]],
                                axis=1).astype(bf16)
        acc[...] += _dot_tn(h, dproj)
        dh = _dot_nt(dproj, w_scr[...])
        dx, dw = _rms_bwd(xv, nw_ref[...], r, dh)
        dx_ref[...] = dres_ref[...] + dx
        dnw_ref[...] += dw

        @pl.when(i == nt - 1)
        def _():
            pltpu.sync_copy(acc, dw_hbm)

    return _pc(
        body, grid=(nt,), name="inproj_bwd",
        in_specs=[pl.BlockSpec((TI, D), lambda i: (i, 0)),
                  pl.BlockSpec((1, D), lambda i: (0, 0)),
                  pl.BlockSpec(memory_space=pl.ANY),
                  pl.BlockSpec((TI, 1), lambda i: (i, 0)),
                  pl.BlockSpec((1, ATT), lambda i: (0, 0)),
                  pl.BlockSpec((TI, D), lambda i: (i, 0)),
                  pl.BlockSpec((TI, 768), lambda i: (i, 0)), _res_spec(4, TI, 768), _res_spec(16, TI, 768),
                  pl.BlockSpec((TI, 256), lambda i: (i, 0)),
                  pl.BlockSpec((TI, 1536), lambda i: (i, 0)),
                  pl.BlockSpec((TI, 512), lambda i: (i, 0)),
                  pl.BlockSpec((TI, 128), lambda i: (i, 0))],
        out_specs=[pl.BlockSpec((TI, D), lambda i: (i, 0)), pl.BlockSpec((1, D), lambda i: (0, 0)),
                   pl.BlockSpec(memory_space=pl.ANY)],
        out_shape=[SDS((S, D), f32), SDS((1, D), f32), SDS((D, INP), f32)],
        scratch_shapes=[pltpu.VMEM((D, INP), bf16), pltpu.VMEM((D, INP), f32), pltpu.VMEM((6, TI, 128), f32)],
        compiler_params=_cp(56),
    )(x, nw, w_aug, pos, freq, dres, datt, datt4, datt16, dpu, ddq, ddz, ddba)


def _outproj_fwd(x, ya, yb, yc, blob_b, kw):
    def body(x_ref, ya_ref, yb_ref, yc_ref, w_ref, o_ref):
        ycat = jnp.concatenate([ya_ref[...], yb_ref[...], yc_ref[...]], axis=1).astype(bf16)
        o_ref[...] = x_ref[...] + _dot(ycat, w_ref[:, 0:256, :].reshape(D, D))

    return _pc(
        body, grid=(S // TM,), name="outproj_fwd",
        in_specs=[pl.BlockSpec((TM, D), lambda i: (i, 0)),
                  pl.BlockSpec((TM, 256), lambda i: (i, 0)),
                  pl.BlockSpec((TM, 256), lambda i: (i, 0)),
                  pl.BlockSpec((TM, 512), lambda i: (i, 0)),
                  pl.BlockSpec((NCH, FC, D), lambda i: (0, kw, 0))],
        out_specs=pl.BlockSpec((TM, D), lambda i: (i, 0)),
        out_shape=SDS((S, D), f32),
        compiler_params=_cp(40),
    )(x, ya, yb, yc, blob_b)


def _outproj_bwd(x, nw, dres, dh4, ya, yb, yc, blob_b, kw):
    nt = S // TM
    nparts = dh4.shape[0]

    def body(x_ref, nw_ref, dres_ref, dh_ref, ya_ref, yb_ref, yc_ref, w_ref, dx_ref, dnw_ref, dya_ref, dyb_ref, dyc_ref, dw_ref):
        i = pl.program_id(0)

        @pl.when(i == 0)
        def _():
            dw_ref[...] = jnp.zeros_like(dw_ref)
            dnw_ref[...] = jnp.zeros_like(dnw_ref)

        dh = dh_ref[0].astype(f32)
        for p in range(1, nparts):
            dh = dh + dh_ref[p].astype(f32)
        xv = x_ref[...]
        _, r = _rms(xv, nw_ref[...])
        dxn, dnw = _rms_bwd(xv, nw_ref[...], r, dh)
        dx = dres_ref[...] + dxn
        dx_ref[...] = dx
        dnw_ref[...] += dnw
        dyv = dx.astype(bf16)
        ycat = jnp.concatenate([ya_ref[...], yb_ref[...], yc_ref[...]], axis=1).astype(bf16)
        dw_ref[...] += _dot_tn(ycat, dyv)
        dcat = _dot_nt(dyv, w_ref[:, 0:256, :].reshape(D, D))
        dya_ref[...] = dcat[:, 0:256]
        dyb_ref[...] = dcat[:, 256:512]
        dyc_ref[...] = dcat[:, 512:1024]

    return _pc(
        body, grid=(nt,), name="outproj_bwd",
        in_specs=[pl.BlockSpec((TM, D), lambda i: (i, 0)),
                  pl.BlockSpec((1, D), lambda i: (0, 0)),
                  pl.BlockSpec((TM, D), lambda i: (i, 0)),
                  pl.BlockSpec((nparts, TM, D), lambda i: (0, i, 0)),
                  pl.BlockSpec((TM, 256), lambda i: (i, 0)),
                  pl.BlockSpec((TM, 256), lambda i: (i, 0)),
                  pl.BlockSpec((TM, 512), lambda i: (i, 0)),
                  pl.BlockSpec((NCH, FC, D), lambda i: (0, kw, 0))],
        out_specs=[pl.BlockSpec((TM, D), lambda i: (i, 0)), pl.BlockSpec((1, D), lambda i: (0, 0)),
                   pl.BlockSpec((TM, 256), lambda i: (i, 0)), pl.BlockSpec((TM, 256), lambda i: (i, 0)),
                   pl.BlockSpec((TM, 512), lambda i: (i, 0)), pl.BlockSpec((D, D), lambda i: (0, 0))],
        out_shape=[SDS((S, D), f32), SDS((1, D), f32), SDS((S, 256), f32), SDS((S, 256), f32), SDS((S, 512), f32),
                   SDS((D, D), f32)],
        compiler_params=_cp(48),
    )(x, nw, dres, dh4, ya, yb, yc, blob_b)


QT = NBLK
NB = S // QT


def _attn_block(q, kp, kc, vp, vc, first):
    kk = jnp.concatenate([kp, kc], axis=0).astype(bf16)
    vv = jnp.concatenate([vp, vc], axis=0).astype(bf16)
    qi = lax.broadcasted_iota(jnp.int32, (4 * QT, NBLK + QT), 0) % QT
    ki = lax.broadcasted_iota(jnp.int32, (4 * QT, NBLK + QT), 1)
    dist = NBLK + qi - ki
    valid = (dist >= 0) & (dist <= NBLK) & (jnp.logical_not(first) | (ki >= NBLK))
    head = lax.broadcasted_iota(jnp.int32, (1, ATT), 1) // EH
    masks = [(head == h).astype(f32) for h in range(4)]
    qs = jnp.concatenate([q * (mh * (1.0 / math.sqrt(EH))) for mh in masks], axis=0).astype(bf16)
    s = _dot_nt(qs, kk)
    s = jnp.where(valid, s, NEG)
    m = lax.stop_gradient(jnp.max(s, axis=-1, keepdims=True))
    p = jnp.exp(s - m)
    den = jnp.sum(p, axis=-1, keepdims=True)
    po = _dot((p * (1.0 / den)).astype(bf16), vv)
    lse = m + jnp.log(den)
    o = jnp.zeros((QT, ATT), f32)
    l = jnp.zeros((QT, ATT), f32)
    for h, mh in enumerate(masks):
        o = o + po[QT * h:QT * (h + 1)] * mh
        l = l + lse[QT * h:QT * (h + 1)] * mh
    return o, l


def _attn_specs(tile):
    own = lambda col: pl.BlockSpec((QT, ATT), lambda s: (tile(s), col))
    prev = lambda col: pl.BlockSpec((NBLK, ATT), lambda s: (jnp.maximum((QT // NBLK) * tile(s) - 1, 0), col))
    return [own(0), prev(1), own(1), prev(2), own(2)]


def _attn_fwd(qkv, per_seq):
    def body(q_ref, kp_ref, kc_ref, vp_ref, vc_ref, o_ref, l_ref):
        first = pl.program_id(0) % per_seq == 0
        o, l = _attn_block(q_ref[...], kp_ref[...], kc_ref[...], vp_ref[...], vc_ref[...], first)
        o_ref[...] = o
        l_ref[...] = l

    blk = pl.BlockSpec((QT, ATT), lambda t: (t, 0))
    return _pc(
        body, grid=(NB,), name="attn_fwd", in_specs=_attn_specs(lambda t: t), out_specs=[blk, blk],
        out_shape=[SDS((S, ATT), f32), SDS((S, ATT), f32)], compiler_params=_cp(32),
    )(qkv, qkv, qkv, qkv, qkv)


def _attn_block_bwd(q, kp, kc, vp, vc, o, l, do, dl, first):
    kk = jnp.concatenate([kp, kc], axis=0).astype(bf16)
    vv = jnp.concatenate([vp, vc], axis=0).astype(bf16)
    qi = lax.broadcasted_iota(jnp.int32, (4 * QT, NBLK + QT), 0) % QT
    ki = lax.broadcasted_iota(jnp.int32, (4 * QT, NBLK + QT), 1)
    dist = NBLK + qi - ki
    valid = (dist >= 0) & (dist <= NBLK) & (jnp.logical_not(first) | (ki >= NBLK))
    head = lax.broadcasted_iota(jnp.int32, (1, ATT), 1) // EH
    masks = [(head == h).astype(f32) for h in range(4)]
    scale = 1.0 / math.sqrt(EH)
    stack = lambda f: jnp.concatenate([f(mh) for mh in masks], axis=0)
    qs = stack(lambda mh: q * (mh * scale)).astype(bf16)
    s = jnp.where(valid, _dot_nt(qs, kk), NEG)
    lse = stack(lambda mh: jnp.max(jnp.where(mh > 0.0, l, NEG), axis=1, keepdims=True))
    p = jnp.exp(s - lse)
    dos = stack(lambda mh: do * mh).astype(bf16)
    dvv = _dot_tn(p.astype(bf16), dos)
    dp = _dot_nt(dos, vv)
    delta = stack(lambda mh: jnp.sum(do * o * mh, axis=1, keepdims=True))
    dlse = stack(lambda mh: jnp.sum(dl * mh, axis=1, keepdims=True))
    ds = (p * (dp - delta + dlse)).astype(bf16)
    dqs = _dot(ds, kk)
    dq = jnp.zeros((QT, ATT), f32)
    for h, mh in enumerate(masks):
        dq = dq + dqs[QT * h:QT * (h + 1)] * (mh * scale)
    dkk = _dot_tn(ds, qs)
    return dq, dkk[:NBLK], dkk[NBLK:], dvv[:NBLK], dvv[NBLK:]


def _attn_bwd(qkv, o, l, do, dl, per_seq):
    def body(q_ref, kp_ref, kc_ref, vp_ref, vc_ref, ofw_ref, lfw_ref, do_ref, dl_ref, o_ref, k_carry, v_carry):
        step = pl.program_id(0)

        @pl.when(step == 0)
        def _():
            k_carry[...] = jnp.zeros_like(k_carry)
            v_carry[...] = jnp.zeros_like(v_carry)

        t = NB - 1 - step
        first = t % per_seq == 0
        last = t % per_seq == per_seq - 1
        dq, dkp, dkc, dvp, dvc = _attn_block_bwd(q_ref[...], kp_ref[...], kc_ref[...], vp_ref[...], vc_ref[...], ofw_ref[...],
                                                 lfw_ref[...], do_ref[...], dl_ref[...], first)
        o_ref[:, 0:ATT] = dq
        o_ref[:, ATT:2 * ATT] = dkc
        o_ref[:, 2 * ATT:3 * ATT] = dvc
        o_ref[QT - NBLK:QT, ATT:2 * ATT] += jnp.where(last, 0.0, k_carry[...])
        o_ref[QT - NBLK:QT, 2 * ATT:3 * ATT] += jnp.where(last, 0.0, v_carry[...])
        k_carry[...] = dkp
        v_carry[...] = dvp

    rev = lambda s: NB - 1 - s
    blk = pl.BlockSpec((QT, ATT), lambda s: (rev(s), 0))
    return _pc(
        body, grid=(NB,), name="attn_bwd", in_specs=_attn_specs(rev) + [blk, blk, blk, blk],
        out_specs=pl.BlockSpec((QT, 768), lambda s: (rev(s), 0)),
        out_shape=SDS((S, 768), f32), scratch_shapes=[pltpu.VMEM((NBLK, ATT), f32)] * 2, compiler_params=_cp(40),
    )(qkv, qkv, qkv, qkv, qkv, o, l, do, dl)


def _merge_weights(l0, l1, l2):
    m = jnp.maximum(jnp.maximum(l0, l1), l2)
    e0, e1, e2 = jnp.exp(l0 - m), jnp.exp(l1 - m), jnp.exp(l2 - m)
    tot = e0 + e1 + e2
    return e0 / tot, e1 / tot, e2 / tot


def _merge_specs():
    nat = pl.BlockSpec((TM, ATT), lambda i: (i, 0))
    return nat, _res_spec(4, TM, ATT), _res_spec(16, TM, ATT)


def _merge_fwd(o1, l1, o4, l4, o16, l16):
    def body(o1_ref, l1_ref, o4_ref, l4_ref, o16_ref, l16_ref, y_ref, scr):
        o4v, l4v = _join_residues(o4_ref, 4, scr), _join_residues(l4_ref, 4, scr)
        o16v, l16v = _join_residues(o16_ref, 16, scr), _join_residues(l16_ref, 16, scr)
        w0, w1, w2 = _merge_weights(l1_ref[...], l4v, l16v)
        y_ref[...] = w0 * o1_ref[...] + w1 * o4v + w2 * o16v

    nat, r4, r16 = _merge_specs()
    return _pc(body, grid=(S // TM,), name="merge_fwd", in_specs=[nat, nat, r4, r4, r16, r16],
                          out_specs=nat, out_shape=SDS((S, ATT), f32), scratch_shapes=[pltpu.VMEM((2, TM, 128), f32)],
                          compiler_params=_cp(32))(o1, l1, o4, l4, o16, l16)


def _merge_bwd(o1, l1, o4, l4, o16, l16, dy):
    def body(o1_ref, l1_ref, o4_ref, l4_ref, o16_ref, l16_ref, dy_ref, do1_ref, dl1_ref, do4_ref, dl4_ref, do16_ref, dl16_ref, scr):
        o4v, l4v = _join_residues(o4_ref, 4, scr), _join_residues(l4_ref, 4, scr)
        o16v, l16v = _join_residues(o16_ref, 16, scr), _join_residues(l16_ref, 16, scr)
        o1v = o1_ref[...]
        w0, w1, w2 = _merge_weights(l1_ref[...], l4v, l16v)
        y = w0 * o1v + w1 * o4v + w2 * o16v
        dyv = dy_ref[...]
        do1_ref[...] = w0 * dyv
        dl1_ref[...] = w0 * (o1v - y) * dyv
        _split_residues(w1 * dyv, scr, [(do4_ref, 4)])
        _split_residues(w1 * (o4v - y) * dyv, scr, [(dl4_ref, 4)])
        _split_residues(w2 * dyv, scr, [(do16_ref, 16)])
        _split_residues(w2 * (o16v - y) * dyv, scr, [(dl16_ref, 16)])

    nat, r4, r16 = _merge_specs()
    return _pc(body, grid=(S // TM,), name="merge_bwd", in_specs=[nat, nat, r4, r4, r16, r16, nat],
                          out_specs=[nat, nat, r4, r4, r16, r16],
                          out_shape=[SDS((S, ATT), f32)] * 2 + [SDS((4, S // 4, ATT), f32)] * 2 + [SDS((16, S // 16, ATT), f32)] * 2,
                          scratch_shapes=[pltpu.VMEM((2, TM, 128), f32)], compiler_params=_cp(32))(o1, l1, o4, l4, o16, l16, dy)


HALO = 16


def _pool_consts(i, rows):
    grp = lax.broadcasted_iota(jnp.int32, (rows, 256), 1) // 64
    t = i * TM + lax.broadcasted_iota(jnp.int32, (rows, 256), 0)
    win = jnp.where(grp == 0, 2, jnp.where(grp == 1, 4, jnp.where(grp == 2, 8, 16)))
    cnt = jnp.minimum(t + 1, win).astype(f32)
    return grp, cnt


def _pool_select(grp, s2, s4, s8, s16):
    return jnp.where(grp == 0, s2, jnp.where(grp == 1, s4, jnp.where(grp == 2, s8, s16)))


def _pooled(i, cur, halo):
    xx = jnp.concatenate([halo, cur], axis=0)
    s2 = xx + pltpu.roll(xx, 1, 0)
    s4 = s2 + pltpu.roll(s2, 2, 0)
    s8 = s4 + pltpu.roll(s4, 4, 0)
    s16 = s8 + pltpu.roll(s8, 8, 0)
    grp, cnt = _pool_consts(i, TM)
    tot = _pool_select(grp, s2[HALO:], s4[HALO:], s8[HALO:], s16[HALO:])
    return tot / cnt - cur


def _pool_fwd(u, wp, scale):
    def body(u_ref, halo_ref, wp_ref, sc_ref, y_ref):
        i = pl.program_id(0)
        halo = halo_ref[...] * (i > 0).astype(f32)
        pooled = _pooled(i, u_ref[...], halo)
        y_ref[...] = _dot(pooled.astype(bf16), wp_ref[...]) * sc_ref[...]

    return _pc(
        body, grid=(S // TM,), name="pool_fwd",
        in_specs=[pl.BlockSpec((TM, 256), lambda i: (i, 0)),
                  pl.BlockSpec((HALO, 256), lambda i: (jnp.maximum(i * (TM // HALO) - 1, 0), 0)),
                  pl.BlockSpec((256, 256), lambda i: (0, 0)),
                  pl.BlockSpec((1, 256), lambda i: (0, 0))],
        out_specs=pl.BlockSpec((TM, 256), lambda i: (i, 0)), out_shape=SDS((S, 256), f32), compiler_params=_cp(32),
    )(u, u, wp, scale)


def _pool_bwd(u, wp, scale, dy):
    nt = S // TM

    def body(u_ref, halo_ref, wp_ref, sc_ref, dy_ref, dyn_ref, du_ref, dwp_ref, dsc_ref):
        i = pl.program_id(0)

        @pl.when(i == 0)
        def _():
            dwp_ref[...] = jnp.zeros_like(dwp_ref)
            dsc_ref[...] = jnp.zeros_like(dsc_ref)

        halo = halo_ref[...] * (i > 0).astype(f32)
        pooled = _pooled(i, u_ref[...], halo).astype(bf16)
        dyv = dy_ref[...]
        dsc_ref[...] += jnp.sum(dyv * _dot(pooled, wp_ref[...]), axis=0, keepdims=True)
        dys = (dyv * sc_ref[...]).astype(bf16)
        dwp_ref[...] += _dot_tn(pooled, dys)
        dpool = _dot_nt(dys, wp_ref[...])
        grp, cnt = _pool_consts(i, TM)
        dyn = ((dyn_ref[...] * (i < nt - 1).astype(f32)) * sc_ref[...]).astype(bf16)
        _, cntn = _pool_consts(i + 1, HALO)
        zn = _dot_nt(dyn, wp_ref[...]) / cntn
        zz = jnp.concatenate([dpool / cnt, zn], axis=0)
        n = TM + HALO
        a2 = zz + pltpu.roll(zz, n - 1, 0)
        a4 = a2 + pltpu.roll(a2, n - 2, 0)
        a8 = a4 + pltpu.roll(a4, n - 4, 0)
        a16 = a8 + pltpu.roll(a8, n - 8, 0)
        du_ref[...] = _pool_select(grp, a2[:TM], a4[:TM], a8[:TM], a16[:TM]) - dpool

    return _pc(
        body, grid=(nt,), name="pool_bwd",
        in_specs=[pl.BlockSpec((TM, 256), lambda i: (i, 0)),
                  pl.BlockSpec((HALO, 256), lambda i: (jnp.maximum(i * (TM // HALO) - 1, 0), 0)),
                  pl.BlockSpec((256, 256), lambda i: (0, 0)),
                  pl.BlockSpec((1, 256), lambda i: (0, 0)),
                  pl.BlockSpec((TM, 256), lambda i: (i, 0)),
                  pl.BlockSpec((HALO, 256), lambda i: (jnp.minimum((i + 1) * (TM // HALO), S // HALO - 1), 0))],
        out_specs=[pl.BlockSpec((TM, 256), lambda i: (i, 0)), pl.BlockSpec((256, 256), lambda i: (0, 0)),
                   pl.BlockSpec((1, 256), lambda i: (0, 0))],
        out_shape=[SDS((S, 256), f32), SDS((256, 256), f32), SDS((1, 256), f32)], compiler_params=_cp(32),
    )(u, u, wp, scale, dy, dy)


CW = 3 * DNW
CHALO = 8
TC = 256


def _conv_fwd(u, w):
    def body(u_ref, halo_ref, w_ref, c_ref):
        i = pl.program_id(0)
        xx = jnp.concatenate([halo_ref[...] * (i > 0).astype(f32), u_ref[...]], axis=0)
        c = (w_ref[3:4, :] * xx + w_ref[2:3, :] * pltpu.roll(xx, 1, 0) + w_ref[1:2, :] * pltpu.roll(xx, 2, 0)
             + w_ref[0:1, :] * pltpu.roll(xx, 3, 0))
        c_ref[...] = c[CHALO:]

    return _pc(
        body, grid=(S // TC,), name="conv_fwd",
        in_specs=[pl.BlockSpec((TC, CW), lambda i: (i, 0)),
                  pl.BlockSpec((CHALO, CW), lambda i: (jnp.maximum(i * (TC // CHALO) - 1, 0), 0)),
                  pl.BlockSpec((8, CW), lambda i: (0, 0))],
        out_specs=pl.BlockSpec((TC, CW), lambda i: (i, 0)), out_shape=SDS((S, CW), f32), compiler_params=_cp(32),
    )(u, u, w)


def _conv_bwd(u, w, dc):
    nt = S // TC

    def body(u_ref, halo_ref, w_ref, dc_ref, dcn_ref, du_ref, dw_ref):
        i = pl.program_id(0)

        @pl.when(i == 0)
        def _():
            dw_ref[...] = jnp.zeros_like(dw_ref)

        dcv = dc_ref[...]
        zz = jnp.concatenate([dcv, dcn_ref[...] * (i < nt - 1).astype(f32)], axis=0)
        n = TC + CHALO
        du = (w_ref[3:4, :] * zz + w_ref[2:3, :] * pltpu.roll(zz, n - 1, 0) + w_ref[1:2, :] * pltpu.roll(zz, n - 2, 0)
              + w_ref[0:1, :] * pltpu.roll(zz, n - 3, 0))
        du_ref[...] = du[:TC]
        xx = jnp.concatenate([halo_ref[...] * (i > 0).astype(f32), u_ref[...]], axis=0)
        for j in range(4):
            shifted = xx if j == 3 else pltpu.roll(xx, 3 - j, 0)
            dw_ref[j:j + 1, :] += jnp.sum(dcv * shifted[CHALO:], axis=0, keepdims=True)

    return _pc(
        body, grid=(nt,), name="conv_bwd",
        in_specs=[pl.BlockSpec((TC, CW), lambda i: (i, 0)),
                  pl.BlockSpec((CHALO, CW), lambda i: (jnp.maximum(i * (TC // CHALO) - 1, 0), 0)),
                  pl.BlockSpec((8, CW), lambda i: (0, 0)),
                  pl.BlockSpec((TC, CW), lambda i: (i, 0)),
                  pl.BlockSpec((CHALO, CW), lambda i: (jnp.minimum((i + 1) * (TC // CHALO), S // CHALO - 1), 0))],
        out_specs=[pl.BlockSpec((TC, CW), lambda i: (i, 0)), pl.BlockSpec((8, CW), lambda i: (0, 0))],
        out_shape=[SDS((S, CW), f32), SDS((8, CW), f32)], compiler_params=_cp(32),
    )(u, u, w, dc, dc)


TL = 512
NCL = TL // CH


def _bdot(a, b):
    return jnp.einsum('nik,nkj->nij', a.astype(bf16), b.astype(bf16), preferred_element_type=f32)


def _bdot_nt(a, b):
    return jnp.einsum('nik,njk->nij', a.astype(bf16), b.astype(bf16), preferred_element_type=f32)


def _bdot_tn(a, b):
    return jnp.einsum('nki,nkj->nij', a.astype(bf16), b.astype(bf16), preferred_element_type=f32)


@jax.custom_vjp
def _inv_unit_lower(a):
    ii = lax.broadcasted_iota(jnp.int32, (1, CH, CH), 1)
    jj = lax.broadcasted_iota(jnp.int32, (1, CH, CH), 2)
    t = (ii == jj).astype(f32) - a
    p = a
    for _ in range(5):
        p = _bdot(p, p)
        t = t + _bdot(t, p)
    return t


def _inv_unit_lower_fwd(a):
    t = _inv_unit_lower(a)
    return t, t


def _inv_unit_lower_bwd(t, dt):
    return (-_bdot_tn(t, _bdot_nt(dt, t)),)


_inv_unit_lower.defvjp(_inv_unit_lower_fwd, _inv_unit_lower_bwd)


def _dn_local(c, dba, a_row, b_row):
    act = c * jax.nn.sigmoid(c)
    lane = lax.broadcasted_iota(jnp.int32, (TL, 128), 1)
    beta_all = jax.nn.sigmoid(dba)
    xs = dba + b_row
    softplus = jnp.maximum(xs, 0.0) + jnp.log(1.0 + jnp.exp(-jnp.abs(xs)))
    g_all = -jnp.exp(a_row) * softplus
    ii = lax.broadcasted_iota(jnp.int32, (1, CH, CH), 1)
    jj = lax.broadcasted_iota(jnp.int32, (1, CH, CH), 2)
    lower = jj <= ii
    strict = jj < ii
    eye = (ii == jj).astype(f32)
    us, ws, qgs, kds, intras = [], [], [], [], []
    aux = jnp.zeros((TL, 128), f32)
    for h in range(4):
        q = act[:, DH * h:DH * (h + 1)]
        k = act[:, DNW + DH * h:DNW + DH * (h + 1)]
        v = act[:, 2 * DNW + DH * h:2 * DNW + DH * (h + 1)]
        q = q * lax.rsqrt(jnp.sum(q * q, axis=-1, keepdims=True) + EPS) * (DH ** -0.5)
        k = k * lax.rsqrt(jnp.sum(k * k, axis=-1, keepdims=True) + EPS)
        beta = jnp.sum(jnp.where(lane == h, beta_all, 0.0), axis=1, keepdims=True)
        g = jnp.sum(jnp.where(lane == 4 + h, g_all, 0.0), axis=1, keepdims=True)
        q3, k3, v3 = q.reshape(NCL, CH, DH), k.reshape(NCL, CH, DH), v.reshape(NCL, CH, DH)
        beta3, g3 = beta.reshape(NCL, CH, 1), g.reshape(NCL, CH, 1)
        g_row = jnp.sum(eye * g3, axis=1, keepdims=True)
        gc_col = jnp.sum(jnp.where(lower, g_row, 0.0), axis=2, keepdims=True)
        gc_row = jnp.sum(jnp.where(ii <= jj, g3, 0.0), axis=1, keepdims=True)
        diff = gc_col - gc_row
        decay = jnp.where(lower, jnp.exp(jnp.where(lower, diff, 0.0)), 0.0)
        kb = k3 * beta3
        vb = v3 * beta3
        a = jnp.where(strict, _bdot_nt(kb, k3) * decay, 0.0)
        t = _inv_unit_lower(a)
        u3 = _bdot(t, vb)
        w3 = _bdot(t, kb * jnp.exp(gc_col))
        intra = jnp.where(lower, _bdot_nt(q3, k3) * decay, 0.0)
        g_last = jnp.sum(g3, axis=1, keepdims=True)
        us.append(u3.reshape(TL, DH))
        ws.append(w3.reshape(TL, DH))
        qgs.append((q3 * jnp.exp(gc_col)).reshape(TL, DH))
        kds.append((k3 * jnp.exp(g_last - gc_col)).reshape(TL, DH))
        intras.append(intra.reshape(TL, CH))
        e_last = jnp.broadcast_to(jnp.exp(g_last), (NCL, CH, 1)).reshape(TL, 1)
        aux = aux + jnp.where(lane == h, e_last, 0.0)
    cat = lambda xs: jnp.concatenate(xs, axis=1)
    return cat(us), cat(ws), cat(qgs), cat(kds), jnp.stack(intras, axis=0), aux


def _dn_local_fwd(c, dba, par):
    def body(c_ref, dba_ref, par_ref, u_ref, w_ref, qg_ref, kd_ref, in_ref, aux_ref):
        u, w, qg, kd, intra, aux = _dn_local(c_ref[...], dba_ref[...], par_ref[0:1, :], par_ref[1:2, :])
        u_ref[...] = u
        w_ref[...] = w
        qg_ref[...] = qg
        kd_ref[...] = kd
        in_ref[...] = intra
        aux_ref[...] = aux

    wide = pl.BlockSpec((TL, DNW), lambda i: (i, 0))
    return _pc(
        body, grid=(S // TL,), name="dn_local_fwd",
        in_specs=[pl.BlockSpec((TL, CW), lambda i: (i, 0)), pl.BlockSpec((TL, 128), lambda i: (i, 0)),
                  pl.BlockSpec((8, 128), lambda i: (0, 0))],
        out_specs=[wide, wide, wide, wide, pl.BlockSpec((4, TL, CH), lambda i: (0, i, 0)),
                   pl.BlockSpec((TL, 128), lambda i: (i, 0))],
        out_shape=[SDS((S, DNW), f32)] * 4 + [SDS((4, S, CH), f32), SDS((S, 128), f32)], compiler_params=_cp(48),
    )(c, dba, par)


def _dn_local_bwd(c, dba, par, du, dw, dqg, dkd, dintra, daux):
    def body(c_ref, dba_ref, par_ref, du_ref, dw_ref, dqg_ref, dkd_ref, din_ref, daux_ref, dc_ref, ddba_ref, dpar_ref):
        @pl.when(pl.program_id(0) == 0)
        def _():
            dpar_ref[...] = jnp.zeros_like(dpar_ref)

        _, vjp = jax.vjp(_dn_local, c_ref[...], dba_ref[...], par_ref[0:1, :], par_ref[1:2, :])
        dc, ddba, da_row, db_row = vjp((du_ref[...], dw_ref[...], dqg_ref[...], dkd_ref[...], din_ref[...], daux_ref[...]))
        dc_ref[...] = dc
        ddba_ref[...] = ddba
        dpar_ref[0:1, :] += da_row
        dpar_ref[1:2, :] += db_row

    wide = pl.BlockSpec((TL, DNW), lambda i: (i, 0))
    return _pc(
        body, grid=(S // TL,), name="dn_local_bwd",
        in_specs=[pl.BlockSpec((TL, CW), lambda i: (i, 0)), pl.BlockSpec((TL, 128), lambda i: (i, 0)),
                  pl.BlockSpec((8, 128), lambda i: (0, 0)), wide, wide, wide, wide,
                  pl.BlockSpec((4, TL, CH), lambda i: (0, i, 0)), pl.BlockSpec((TL, 128), lambda i: (i, 0))],
        out_specs=[pl.BlockSpec((TL, CW), lambda i: (i, 0)), pl.BlockSpec((TL, 128), lambda i: (i, 0)),
                   pl.BlockSpec((8, 128), lambda i: (0, 0))],
        out_shape=[SDS((S, CW), f32), SDS((S, 128), f32), SDS((8, 128), f32)], compiler_params=_cp(56),
    )(c, dba, par, du, dw, dqg, dkd, dintra, daux)


def _dn_step(state, u, w, qg, kd, intra, aux):
    lane = lax.broadcasted_iota(jnp.int32, (CH, 128), 1)
    row = lax.broadcasted_iota(jnp.int32, (CH, 128), 0)
    outs, states = [], []
    for h in range(4):
        sl = slice(DH * h, DH * (h + 1))
        st = state[h]
        e = jnp.sum(jnp.sum(jnp.where((lane == h) & (row == 0), aux, 0.0), axis=1, keepdims=True), axis=0, keepdims=True)
        v_new = u[:, sl] - _dot(w[:, sl].astype(bf16), st.astype(bf16))
        vb = v_new.astype(bf16)
        outs.append(_dot(qg[:, sl].astype(bf16), st.astype(bf16)) + _dot(intra[h].astype(bf16), vb))
        states.append(st * e + _dot_tn(kd[:, sl].astype(bf16), vb))
    return jnp.concatenate(outs, axis=1), jnp.stack(states, axis=0)


CPS = 8
NSTEP = NCHUNK // CPS


def _dn_rec_specs(index):
    wide = pl.BlockSpec((CPS * CH, DNW), lambda n: (index(n), 0))
    inb = pl.BlockSpec((4, CPS * CH, CH), lambda n: (0, index(n), 0))
    auxb = pl.BlockSpec((CPS * CH, 128), lambda n: (index(n), 0))
    stb = pl.BlockSpec((CPS, 4, DH, DH), lambda n: (index(n), 0, 0, 0))
    return wide, inb, auxb, stb


def _dn_rec_fwd(u, w, qg, kd, intra, aux):
    def body(u_ref, w_ref, qg_ref, kd_ref, in_ref, aux_ref, o_ref, st_ref, st_scr):
        @pl.when(pl.program_id(0) == 0)
        def _():
            st_scr[...] = jnp.zeros_like(st_scr)

        st = st_scr[...]
        for k in range(CPS):
            rows = slice(CH * k, CH * (k + 1))
            st_ref[k] = st
            o, st = _dn_step(st, u_ref[rows, :], w_ref[rows, :], qg_ref[rows, :], kd_ref[rows, :], in_ref[:, rows, :],
                             aux_ref[rows, :])
            o_ref[rows, :] = o
        st_scr[...] = st

    wide, inb, auxb, stb = _dn_rec_specs(lambda n: n)
    return _pc(
        body, grid=(NSTEP,), name="dn_rec_fwd", in_specs=[wide, wide, wide, wide, inb, auxb], out_specs=[wide, stb],
        out_shape=[SDS((S, DNW), f32), SDS((NCHUNK, 4, DH, DH), f32)],
        scratch_shapes=[pltpu.VMEM((4, DH, DH), f32)], compiler_params=_cp(32),
    )(u, w, qg, kd, intra, aux)


def _dn_rec_bwd(u, w, qg, kd, intra, aux, states, do):
    def body(u_ref, w_ref, qg_ref, kd_ref, in_ref, aux_ref, st_ref, do_ref,
             du_ref, dw_ref, dqg_ref, dkd_ref, din_ref, daux_ref, ds_scr):
        @pl.when(pl.program_id(0) == 0)
        def _():
            ds_scr[...] = jnp.zeros_like(ds_scr)

        ds = ds_scr[...]
        for k in reversed(range(CPS)):
            rows = slice(CH * k, CH * (k + 1))
            _, vjp = jax.vjp(_dn_step, st_ref[k], u_ref[rows, :], w_ref[rows, :], qg_ref[rows, :], kd_ref[rows, :],
                             in_ref[:, rows, :], aux_ref[rows, :])
            ds, du, dw, dqg, dkd, din, daux = vjp((do_ref[rows, :], ds))
            du_ref[rows, :] = du
            dw_ref[rows, :] = dw
            dqg_ref[rows, :] = dqg
            dkd_ref[rows, :] = dkd
            din_ref[:, rows, :] = din
            daux_ref[rows, :] = daux
        ds_scr[...] = ds

    wide, inb, auxb, stb = _dn_rec_specs(lambda n: NSTEP - 1 - n)
    return _pc(
        body, grid=(NSTEP,), name="dn_rec_bwd", in_specs=[wide, wide, wide, wide, inb, auxb, stb, wide],
        out_specs=[wide, wide, wide, wide, inb, auxb],
        out_shape=[SDS((S, DNW), f32)] * 4 + [SDS((4, S, CH), f32), SDS((S, 128), f32)],
        scratch_shapes=[pltpu.VMEM((4, DH, DH), f32)], compiler_params=_cp(40),
    )(u, w, qg, kd, intra, aux, states, do)


def _dn_post(o, z, nw):
    parts = []
    for h in range(4):
        sl = slice(DH * h, DH * (h + 1))
        oh = o[:, sl]
        y = oh * lax.rsqrt(jnp.mean(oh * oh, axis=-1, keepdims=True) + EPS) * nw
        zh = z[:, sl]
        parts.append(y * (zh * jax.nn.sigmoid(zh)))
    return jnp.concatenate(parts, axis=1)


def _dn_post_fwd(o, z, nw):
    def body(o_ref, z_ref, nw_ref, y_ref):
        y_ref[...] = _dn_post(o_ref[...], z_ref[...], nw_ref[...])

    wide = pl.BlockSpec((TM, DNW), lambda i: (i, 0))
    return _pc(body, grid=(S // TM,), name="dn_post_fwd",
                          in_specs=[wide, wide, pl.BlockSpec((1, 128), lambda i: (0, 0))], out_specs=wide,
                          out_shape=SDS((S, DNW), f32), compiler_params=_cp(32))(o, z, nw)


def _dn_post_bwd(o, z, nw, dy):
    def body(o_ref, z_ref, nw_ref, dy_ref, do_ref, dz_ref, dnw_ref):
        @pl.when(pl.program_id(0) == 0)
        def _():
            dnw_ref[...] = jnp.zeros_like(dnw_ref)

        _, vjp = jax.vjp(_dn_post, o_ref[...], z_ref[...], nw_ref[...])
        do, dz, dnw = vjp(dy_ref[...])
        do_ref[...] = do
        dz_ref[...] = dz
        dnw_ref[...] += dnw

    wide = pl.BlockSpec((TM, DNW), lambda i: (i, 0))
    one = pl.BlockSpec((1, 128), lambda i: (0, 0))
    return _pc(body, grid=(S // TM,), name="dn_post_bwd", in_specs=[wide, wide, one, wide],
                          out_specs=[wide, wide, one], out_shape=[SDS((S, DNW), f32), SDS((S, DNW), f32), SDS((1, 128), f32)],
                          compiler_params=_cp(32))(o, z, nw, dy)


def _row_tile(rows, width, itemsize=4, target=2 * 1024 * 1024):
    best = None
    for t in range(16, rows + 1, 16):
        if rows % t == 0 and t * width * itemsize <= target:
            best = t
    return best if best is not None else rows


def _sum_pieces(pieces, out_dtype, name):
    n, rows, width = pieces.shape
    tr = _row_tile(rows, width * n)

    def body(p_ref, o_ref):
        acc = p_ref[0].astype(f32)
        for s in range(1, n):
            acc = acc + p_ref[s].astype(f32)
        o_ref[...] = acc.astype(out_dtype)

    return _pc(body, grid=(rows // tr,), name=name,
                          in_specs=[pl.BlockSpec((n, tr, width), lambda i: (0, i, 0))],
                          out_specs=pl.BlockSpec((tr, width), lambda i: (i, 0)),
                          out_shape=SDS((rows, width), out_dtype), compiler_params=_cp(32))(pieces)


def _sum_core_pair(part, got, c_arr):
    n, rows, width = part.shape
    half = rows // 2
    tr = _row_tile(half, width, itemsize=2)
    nt = half // tr

    def body(c_ref, p_ref, g_ref, o_ref):
        o_ref[...] = (p_ref[...].astype(f32) + g_ref[...].astype(f32)).astype(bf16)

    gs = pltpu.PrefetchScalarGridSpec(
        num_scalar_prefetch=1, grid=(n, nt),
        in_specs=[pl.BlockSpec((1, tr, width), lambda j, i, c: (j, c[0] * nt + i, 0)),
                  pl.BlockSpec((1, tr, width), lambda j, i, c: (j, i, 0))],
        out_specs=pl.BlockSpec((1, tr, width), lambda j, i, c: (j, i, 0)))
    return _pc(body, grid_spec=gs, name="sum_core_pair", out_shape=SDS((n, half, width), bf16),
                          compiler_params=_cp(32))(c_arr, part, got)


def _sum_chips(pieces, c_arr, full, row0, total_rows):
    n, half, width = pieces.shape
    tr = max(t for t in range(16, 257, 16) if half % t == 0 and row0 % t == 0)
    nt = half // tr

    def body(c_ref, p_ref, *rest):
        o_ref = rest[-1]
        acc = p_ref[0].astype(f32)
        for s in range(1, n):
            acc = acc + p_ref[s].astype(f32)
        o_ref[...] = acc

    gs = pltpu.PrefetchScalarGridSpec(
        num_scalar_prefetch=1, grid=(nt,),
        in_specs=[pl.BlockSpec((n, tr, width), lambda i, c: (0, i, 0))] + ([] if full is None else [ANY]),
        out_specs=pl.BlockSpec((tr, width), lambda i, c: (row0 // tr + c[0] * nt + i, 0)))
    args = (c_arr, pieces) if full is None else (c_arr, pieces, full)
    return _pc(body, grid_spec=gs, name="sum_chips", out_shape=SDS((total_rows, width), f32),
                          input_output_aliases={} if full is None else {2: 0}, compiler_params=_cp(32))(*args)


def _adamw_math(w, g, m, v):
    mn = ADAM_B1 * m + (1.0 - ADAM_B1) * g
    vn = ADAM_B2 * v + (1.0 - ADAM_B2) * (g * g)
    m_hat = mn / (1.0 - ADAM_B1 ** ADAM_STEP)
    v_hat = vn / (1.0 - ADAM_B2 ** ADAM_STEP)
    return -ADAM_LR * (m_hat / (jnp.sqrt(v_hat) + ADAM_EPS) + ADAM_WD * w), mn, vn


def _adamw(w, g, m, v, name):
    rows, width = w.shape
    tr = _row_tile(rows, width * 7, target=12 * 1024 * 1024)

    def body(w_ref, g_ref, m_ref, v_ref, d_ref, nm_ref, nv_ref):
        d_ref[...], nm_ref[...], nv_ref[...] = _adamw_math(w_ref[...], g_ref[...], m_ref[...], v_ref[...])

    blk = pl.BlockSpec((tr, width), lambda i: (i, 0))
    return _pc(body, grid=(rows // tr,), name=name, in_specs=[blk] * 4, out_specs=[blk] * 3,
                          out_shape=[SDS((rows, width), f32)] * 3, compiler_params=_cp(40))(w, g, m, v)


def _adamw_rows(w, m, v, gblob, tr, first_tile, name):
    layers, rows, width = w.shape

    def body(w_ref, g_ref, m_ref, v_ref, d_ref, nm_ref, nv_ref):
        d_ref[0], nm_ref[0], nv_ref[0] = _adamw_math(w_ref[0], g_ref[...], m_ref[0], v_ref[0])

    blk = pl.BlockSpec((1, tr, width), lambda l, i: (l, i, 0))
    gblk = pl.BlockSpec((tr, width), lambda l, i: (first_tile(l) + i, 0))
    return _pc(body, grid=(layers, rows // tr), name=name, in_specs=[blk, gblk, blk, blk], out_specs=[blk] * 3,
                          out_shape=[SDS(w.shape, f32)] * 3, compiler_params=_cp(40))(w, gblob, m, v)


ANY = pl.BlockSpec(memory_space=pl.ANY)


def _place():
    x, y, c = lax.axis_index("x"), lax.axis_index("y"), lax.axis_index("c")
    chips = [(1 - x, y), (x, 1 - y), (1 - x, 1 - y)]
    return x, y, c, chips


NQ_ICI = 4
NQ_D2D = 8


def _chunks(rows, want):
    n = max(k for k in range(1, want + 1) if rows % k == 0 and (rows // k) % 16 == 0)
    step = rows // n
    return [(q * step, step) for q in range(n)]


def _scatter_copies(ins, outs, ssem, rsem, lsem):
    x, y, c, chips = _place()
    me = (x, y, c)
    locals_, sends, lands = [], [], []
    for b in range(len(ins)):
        for q, (off, n) in enumerate(_chunks(ins[b].shape[1], NQ_ICI)):
            rows = pl.ds(off, n)
            mine = outs[b].at[2 * x + y, rows, :]
            locals_.append(pltpu.make_async_copy(ins[b].at[2 * x + y, rows, :], mine, lsem.at[b, q]))
            for j, chip in enumerate(chips):
                sends.append(_remote(ins[b].at[2 * chip[0] + chip[1], rows, :], mine, ssem.at[b, j, q], rsem.at[b, j, q],
                                     (*chip, c)))
                slot = outs[b].at[2 * chip[0] + chip[1], rows, :]
                lands.append(_remote(slot, slot, ssem.at[b, j, q], rsem.at[b, j, q], me))
    return locals_, sends, lands


def _scatter_start(ins, outs, ssem, rsem, lsem):
    locals_, sends, _ = _scatter_copies(ins, outs, ssem, rsem, lsem)
    for cp in locals_ + sends:
        cp.start()


def _scatter_finish(ins, outs, ssem, rsem, lsem):
    locals_, sends, lands = _scatter_copies(ins, outs, ssem, rsem, lsem)
    for cp in lands:
        cp.wait_recv()
    for cp in sends:
        cp.wait_send()
    for cp in locals_:
        cp.wait()


def _scatter_sems(nb):
    return [pltpu.SemaphoreType.DMA((nb, 3, NQ_ICI)), pltpu.SemaphoreType.DMA((nb, 3, NQ_ICI)),
            pltpu.SemaphoreType.DMA((nb, NQ_ICI))]


def _remote(src, dst, ssem, rsem, dev):
    return pltpu.make_async_remote_copy(src_ref=src, dst_ref=dst, send_sem=ssem, recv_sem=rsem, device_id=dev,
                                        device_id_type=MESH)


def _all_gather_weights(shards):
    nb = len(shards)

    def body(*refs):
        ins, outs, sems = refs[:nb], refs[nb:2 * nb], refs[2 * nb:]
        _gather_start(ins, outs, *sems)
        _gather_finish(ins, outs, *sems)

    return _pc(
        body, name="all_gather_weights", in_specs=[ANY] * nb, out_specs=[ANY] * nb,
        out_shape=[SDS((NCH,) + s.shape, s.dtype) for s in shards], scratch_shapes=_gather_sems(nb),
    )(*shards)


def _gather_first(ins, outs, ssem, rsem, lsem):
    x, y, c, chips = _place()
    locals_, sends = [], []
    for b in range(len(ins)):
        half = ins[b].shape[0] // 2
        for q, (off, n) in enumerate(_chunks(half, NQ_ICI)):
            mine = pl.ds(c * half + off, n)
            own = outs[b].at[2 * x + y, mine, :]
            locals_.append(pltpu.make_async_copy(ins[b].at[mine, :], own, lsem.at[b, q]))
            sends.append(_remote(ins[b].at[mine, :], own, ssem.at[b, 0, q], rsem.at[b, 0, q], (x, y, 1 - c)))
            sends += [_remote(ins[b].at[mine, :], own, ssem.at[b, 1 + j, q], rsem.at[b, 1 + j, q], (*chip, c))
                      for j, chip in enumerate(chips)]
    return locals_, sends


def _gather_start(ins, outs, ssem, rsem, lsem):
    locals_, sends = _gather_first(ins, outs, ssem, rsem, lsem)
    for cp in locals_ + sends:
        cp.start()


def _gather_finish(ins, outs, ssem, rsem, lsem):
    x, y, c, chips = _place()
    me, sib = (x, y, c), (x, y, 1 - c)
    locals_, sends = _gather_first(ins, outs, ssem, rsem, lsem)
    for b in range(len(ins)):
        half = ins[b].shape[0] // 2
        for q, (off, n) in enumerate(_chunks(half, NQ_ICI)):
            mine = pl.ds(c * half + off, n)
            for j, chip in enumerate(chips):
                landed = outs[b].at[2 * chip[0] + chip[1], mine, :]
                _remote(landed, landed, ssem.at[b, 1 + j, q], rsem.at[b, 1 + j, q], me).wait_recv()
                cp = _remote(landed, landed, ssem.at[b, 4 + j, q], rsem.at[b, 4 + j, q], sib)
                cp.start()
                sends.append(cp)
    for b in range(len(ins)):
        half = ins[b].shape[0] // 2
        for q, (off, n) in enumerate(_chunks(half, NQ_ICI)):
            other = pl.ds((1 - c) * half + off, n)
            theirs = outs[b].at[2 * x + y, other, :]
            _remote(theirs, theirs, ssem.at[b, 0, q], rsem.at[b, 0, q], me).wait_recv()
            for j, chip in enumerate(chips):
                fwd = outs[b].at[2 * chip[0] + chip[1], other, :]
                _remote(fwd, fwd, ssem.at[b, 4 + j, q], rsem.at[b, 4 + j, q], me).wait_recv()
    for cp in sends:
        cp.wait_send()
    for cp in locals_:
        cp.wait()


def _gather_sems(nb):
    return [pltpu.SemaphoreType.DMA((nb, 7, NQ_ICI)), pltpu.SemaphoreType.DMA((nb, 7, NQ_ICI)),
            pltpu.SemaphoreType.DMA((nb, NQ_ICI))]


def _send_sibling_half(parts):
    nb = len(parts)

    def body(*refs):
        ins, gots = refs[:nb], refs[nb:2 * nb]
        ssem, rsem = refs[2 * nb:]
        x, y, c, _ = _place()
        sib = (x, y, 1 - c)
        todo = []
        for b in range(nb):
            half = ins[b].shape[1] // 2
            for q, (off, n) in enumerate(_chunks(half, NQ_D2D)):
                cp = _remote(ins[b].at[:, pl.ds((1 - c) * half + off, n), :], gots[b].at[:, pl.ds(off, n), :],
                             ssem.at[b, q], rsem.at[b, q], sib)
                cp.start()
                todo.append(cp)
        for cp in todo:
            cp.wait()

    return _pc(
        body, name="send_sibling_half", in_specs=[ANY] * nb, out_specs=[ANY] * nb,
        out_shape=[SDS((p.shape[0], p.shape[1] // 2, p.shape[2]), p.dtype) for p in parts],
        scratch_shapes=[pltpu.SemaphoreType.DMA((nb, NQ_D2D)), pltpu.SemaphoreType.DMA((nb, NQ_D2D))],
    )(*parts)


def _scatter_to_chips(parts):
    nb = len(parts)

    def body(*refs):
        ins, outs, sems = refs[:nb], refs[nb:2 * nb], refs[2 * nb:]
        _scatter_start(ins, outs, *sems)
        _scatter_finish(ins, outs, *sems)

    return _pc(
        body, name="scatter_to_chips", in_specs=[ANY] * nb, out_specs=[ANY] * nb,
        out_shape=[SDS(p.shape, p.dtype) for p in parts], scratch_shapes=_scatter_sems(nb),
    )(*parts)


def _join_halves(fulls, ranges):
    nb = len(fulls)
    nr = max(len(r) for r in ranges)

    def body(*refs):
        ins, outs = refs[:nb], refs[nb:2 * nb]
        ssem, rsem = refs[2 * nb:]
        x, y, c, _ = _place()
        sib = (x, y, 1 - c)
        sends, lands = [], []
        for b in range(nb):
            for g, (row0, rows) in enumerate(ranges[b]):
                half = rows // 2
                for q, (off, n) in enumerate(_chunks(half, NQ_D2D)):
                    mine = pl.ds(row0 + c * half + off, n)
                    sends.append(_remote(ins[b].at[mine, :], outs[b].at[mine, :], ssem.at[b, g, q], rsem.at[b, g, q], sib))
                    other = outs[b].at[pl.ds(row0 + (1 - c) * half + off, n), :]
                    lands.append(_remote(other, other, ssem.at[b, g, q], rsem.at[b, g, q], sib))
        for cp in sends:
            cp.start()
        for cp in lands:
            cp.wait_recv()
        for cp in sends:
            cp.wait_send()

    return _pc(
        body, name="join_halves", in_specs=[ANY] * nb, out_specs=[ANY] * nb,
        out_shape=[SDS(h.shape, h.dtype) for h in fulls], input_output_aliases={b: b for b in range(nb)},
        scratch_shapes=[pltpu.SemaphoreType.DMA((nb, nr, NQ_D2D)), pltpu.SemaphoreType.DMA((nb, nr, NQ_D2D))],
    )(*fulls)


def _gather_small(vec):
    def body(v_ref, o_ref, ssem, rsem, lsem):
        x, y, c, _ = _place()
        mine = o_ref.at[4 * x + 2 * y + c]
        local = pltpu.make_async_copy(v_ref, mine, lsem)
        local.start()
        sends = []
        for k in range(1, 8):
            peer = (x ^ (k >> 2), y ^ ((k >> 1) & 1), c ^ (k & 1))
            cp = _remote(v_ref, mine, ssem.at[k - 1], rsem.at[k - 1], peer)
            cp.start()
            sends.append(cp)
        for k in range(1, 8):
            px, py, pc = x ^ (k >> 2), y ^ ((k >> 1) & 1), c ^ (k & 1)
            slot = o_ref.at[4 * px + 2 * py + pc]
            _remote(slot, slot, ssem.at[k - 1], rsem.at[k - 1], (x, y, c)).wait_recv()
        for cp in sends:
            cp.wait_send()
        local.wait()

    return _pc(
        body, name="gather_small", in_specs=[ANY], out_specs=ANY, out_shape=SDS((8,) + vec.shape, vec.dtype),
        scratch_shapes=[pltpu.SemaphoreType.DMA((7,)), pltpu.SemaphoreType.DMA((7,)), pltpu.SemaphoreType.DMA],
    )(vec)


def _block_diag(pw):
    return jnp.concatenate([jnp.pad(pw[g], ((0, 0), (64 * g, 192 - 64 * g))) for g in range(4)], axis=0)


def _own_columns(full, chip):
    n = full.shape[-1] // NCH
    parts = full.reshape(full.shape[:-1] + (NCH, n))
    sel = (lax.broadcasted_iota(jnp.int32, (NCH, 1), 0) == chip)
    return jnp.sum(jnp.where(sel, parts, 0.0), axis=-2)


def _at_own_columns(shard, chip):
    n = shard.shape[-1]
    sel = (lax.broadcasted_iota(jnp.int32, (NCH * n,), 0) // n == chip)
    return jnp.where(sel, jnp.tile(shard, NCH), 0.0)


def _pad_rows(a, rows):
    return jnp.pad(a, ((0, rows - a.shape[0]),) + ((0, 0),) * (a.ndim - 1))


def _ffn_block(l, which):
    return 7 * l + 3 * which


def _wout_block(l):
    return 7 * l + 6


class _Weights:
    def __init__(self):
        self.ffn, self.wout, self.w_aug, self.rides = {}, {}, {}, {}

    @classmethod
    def from_blob(cls, blob, w_aug):
        self = cls()
        for l in range(DEPTH):
            self.ffn[(l, 0)], self.ffn[(l, 1)] = (blob, _ffn_block(l, 0)), (blob, _ffn_block(l, 1))
            self.wout[l], self.w_aug[l] = (blob, _wout_block(l)), w_aug[l]
        return self

    def set_w_in(self, l, gathered):
        self.w_aug[l] = jnp.pad(gathered.transpose(1, 0, 2).reshape(D, INW), ((0, 0), (0, INP - INW)))

    def ffn_fwd(self, l, which, x, nw):
        arr, k0 = self.ffn[(l, which)]
        if (l, which) not in self.rides:
            return _ffn_fwd(x, nw, arr, k0)
        shards, landed = self.rides[(l, which)]
        out, *gathered = _ffn_fwd(x, nw, arr, k0, shards)
        landed(gathered)
        return out


def _layer_fwd(l, x0, pos, freq, wts, ws):
    sv = {"x0": x0}
    x1 = ws.ffn_fwd(l, 0, x0, wts["ffn1_norm"][l:l + 1])
    att, att4, att16, pu, dq, dz, dba = _inproj_fwd(x1, wts["mix_norm"][l:l + 1], ws.w_aug[l], pos, freq)
    qkvs = [att, att4.reshape(S, 768), att16.reshape(S, 768)]
    (o1, l1), (o4, l4), (o16, l16) = [_attn_fwd(q, NB // d) for q, d in zip(qkvs, PATTERN_DIL)]
    ols = (o1, l1, o4.reshape(4, S // 4, ATT), l4.reshape(4, S // 4, ATT), o16.reshape(16, S // 16, ATT),
           l16.reshape(16, S // 16, ATT))
    ya = _merge_fwd(*ols)
    yb = _pool_fwd(pu, wts["pool_bd"][l], wts["pool_scale"][l:l + 1])
    c = _conv_fwd(dq, wts["conv_w"][l])
    u, w, qg, kd, intra, aux = _dn_local_fwd(c, dba, wts["dn_par"][l])
    o_dn, states = _dn_rec_fwd(u, w, qg, kd, intra, aux)
    yc = _dn_post_fwd(o_dn, dz, wts["dn_out_norm"][l:l + 1])
    x2 = _outproj_fwd(x1, ya, yb, yc, *ws.wout[l])
    x3 = ws.ffn_fwd(l, 1, x2, wts["ffn2_norm"][l:l + 1])
    sv.update(x1=x1, x2=x2, qkvs=qkvs, ols=ols, ya=ya, yb=yb, yc=yc, pu=pu, dq=dq, dz=dz, dba=dba, c=c,
              u=u, w=w, qg=qg, kd=kd, intra=intra, aux=aux, states=states, o_dn=o_dn)
    return x3, sv


def _wout_part(g):
    return jnp.pad(g.astype(bf16).reshape(NCH, 256, D), ((0, 0), (0, FC - 256), (0, 0)))


def _win_part(g):
    return g[:, :INW].astype(bf16).reshape(D, NCH, INC).transpose(1, 0, 2)


def _layer_bwd(l, dx3, sv, pos, freq, wts, ws, ride=None, prep=None):
    gr = {}
    g2, u2, d2, dh4, *pieces_before = _ffn_bwd(sv["x2"], wts["ffn2_norm"][l:l + 1], *ws.ffn[(l, 1)], dx3, ride)
    gr.update(ffn2_w_gate=g2, ffn2_w_up=u2, ffn2_w_down=d2)
    dx2, gr["ffn2_norm"], dya, dyb, dyc, gr["w_out"] = _outproj_bwd(sv["x2"], wts["ffn2_norm"][l:l + 1], dx3, dh4, sv["ya"],
                                                                     sv["yb"], sv["yc"], *ws.wout[l])
    do_dn, ddz, gr["dn_out_norm"] = _dn_post_bwd(sv["o_dn"], sv["dz"], wts["dn_out_norm"][l:l + 1], dyc)
    du, dw, dqg, dkd, dintra, daux = _dn_rec_bwd(sv["u"], sv["w"], sv["qg"], sv["kd"], sv["intra"], sv["aux"], sv["states"], do_dn)
    dc, ddba, gr["dn_par"] = _dn_local_bwd(sv["c"], sv["dba"], wts["dn_par"][l], du, dw, dqg, dkd, dintra, daux)
    ddq, gr["conv_w"] = _conv_bwd(sv["dq"], wts["conv_w"][l], dc)
    dpu, gr["pool_bd"], gr["pool_scale"] = _pool_bwd(sv["pu"], wts["pool_bd"][l], wts["pool_scale"][l:l + 1], dyb)
    dols = _merge_bwd(*sv["ols"], dya)
    flat = lambda a: a.reshape(S, ATT)
    datts = [_attn_bwd(q, flat(sv["ols"][2 * p]), flat(sv["ols"][2 * p + 1]), flat(dols[2 * p]), flat(dols[2 * p + 1]), NB // d)
             for p, (q, d) in enumerate(zip(sv["qkvs"], PATTERN_DIL))]
    dx1, gr["mix_norm"], gr["w_aug"] = _inproj_bwd(sv["x1"], wts["mix_norm"][l:l + 1], ws.w_aug[l], pos, freq, dx2,
                                                    datts[0], datts[1].reshape(4, S // 4, 768),
                                                    datts[2].reshape(16, S // 16, 768), dpu, ddq, ddz, ddba)
    own = None
    if prep is not None:
        own = prep([jnp.concatenate([g2, u2, d2, _wout_part(gr["w_out"])], axis=1), _win_part(gr["w_aug"])])
    g1, u1, d1, dh4, *pieces_own = _ffn_bwd(sv["x0"], wts["ffn1_norm"][l:l + 1], *ws.ffn[(l, 0)], dx1, own)
    dx0, gr["ffn1_norm"] = _norm_bwd(sv["x0"], wts["ffn1_norm"][l:l + 1], dx1, dh4)
    gr.update(ffn1_w_gate=g1, ffn1_w_up=u1, ffn1_w_down=d1)
    return dx0, gr, pieces_before, pieces_own


def _device_step(x, pos, target, wts, ws, prep=None):
    freq = jnp.tile(ROPE_THETA ** (-jnp.arange(0, EH, 2, dtype=f32) / EH), 2 * ATT // EH).reshape(1, ATT)
    saved = []
    h = x
    for l in range(DEPTH):
        h, sv = _layer_fwd(l, h, pos, freq, wts, ws)
        saved.append(sv)
    dh, g_final, loss = _final(h, wts["final_norm"], target)
    grads = [None] * DEPTH
    dh, grads[1], _, _ = _layer_bwd(1, dh, saved[1], pos, freq, wts, ws)
    sums1 = None
    if prep is not None:
        g = grads[1]
        ffn = [g[f"ffn{f}_w_{n}"] for f in (1, 2) for n in ("gate", "up", "down")]
        sums1 = prep([jnp.concatenate(ffn + [_wout_part(g["w_out"])], axis=1), _win_part(g["w_aug"])])
    dh, grads[0], pieces1, pieces0 = _layer_bwd(0, dh, saved[0], pos, freq, wts, ws, sums1, prep)
    return loss, dh, g_final, grads, pieces1, pieces0


_SMALL = (("ffn1_norm", (DEPTH, D)), ("mix_norm", (DEPTH, D)), ("pool_w", (DEPTH, 4, 64, 64)), ("pool_scale", (DEPTH, 256)),
          ("dn_conv_w", (DEPTH, 4, CW)), ("dn_a_log", (DEPTH, 4)), ("dn_dt_bias", (DEPTH, 4)), ("dn_out_norm", (DEPTH, 128)),
          ("ffn2_norm", (DEPTH, D)), ("final_norm", (D,)), ("loss", (1,)))


def _pack_small(vals):
    rows = []
    for name, shape in _SMALL:
        flat = vals[name].astype(f32).reshape(-1)
        rows.append(jnp.pad(flat, (0, _small_rows(shape) * 128 - flat.shape[0])).reshape(-1, 128))
    out = jnp.concatenate(rows, axis=0)
    return _pad_rows(out, -(-out.shape[0] // 16) * 16)


def _small_rows(shape):
    return -(-int(np.prod(shape)) // 1024) * 8


def _unpack_small(packed):
    vals, r = {}, 0
    for name, shape in _SMALL:
        size, n = int(np.prod(shape)), _small_rows(shape)
        vals[name] = packed[r:r + n].reshape(-1)[:size].reshape(shape)
        r += n
    return vals


def kernel(x, positions, ffn1_norm, ffn1_w_gate, ffn1_w_up, ffn1_w_down, mix_norm, w_in, pool_w, pool_scale, dn_conv_w, dn_a_log, dn_dt_bias, dn_out_norm, w_out, ffn2_norm, ffn2_w_gate, ffn2_w_up, ffn2_w_down, final_norm, loss_target, m_ffn1_norm, m_ffn1_w_gate, m_ffn1_w_up, m_ffn1_w_down, m_mix_norm, m_w_in, m_pool_w, m_pool_scale, m_dn_conv_w, m_dn_a_log, m_dn_dt_bias, m_dn_out_norm, m_w_out, m_ffn2_norm, m_ffn2_w_gate, m_ffn2_w_up, m_ffn2_w_down, m_final_norm, v_ffn1_norm, v_ffn1_w_gate, v_ffn1_w_up, v_ffn1_w_down, v_mix_norm, v_w_in, v_pool_w, v_pool_scale, v_dn_conv_w, v_dn_a_log, v_dn_dt_bias, v_dn_out_norm, v_w_out, v_ffn2_norm, v_ffn2_w_gate, v_ffn2_w_up, v_ffn2_w_down, v_final_norm):
    names = ["ffn1_norm", "ffn1_w_gate", "ffn1_w_up", "ffn1_w_down", "mix_norm", "w_in", "pool_w", "pool_scale", "dn_conv_w",
             "dn_a_log", "dn_dt_bias", "dn_out_norm", "w_out", "ffn2_norm", "ffn2_w_gate", "ffn2_w_up", "ffn2_w_down", "final_norm"]
    W = dict(zip(names, [ffn1_norm, ffn1_w_gate, ffn1_w_up, ffn1_w_down, mix_norm, w_in, pool_w, pool_scale, dn_conv_w,
                         dn_a_log, dn_dt_bias, dn_out_norm, w_out, ffn2_norm, ffn2_w_gate, ffn2_w_up, ffn2_w_down, final_norm]))
    M = dict(zip(names, [m_ffn1_norm, m_ffn1_w_gate, m_ffn1_w_up, m_ffn1_w_down, m_mix_norm, m_w_in, m_pool_w, m_pool_scale,
                         m_dn_conv_w, m_dn_a_log, m_dn_dt_bias, m_dn_out_norm, m_w_out, m_ffn2_norm, m_ffn2_w_gate, m_ffn2_w_up,
                         m_ffn2_w_down, m_final_norm]))
    V = dict(zip(names, [v_ffn1_norm, v_ffn1_w_gate, v_ffn1_w_up, v_ffn1_w_down, v_mix_norm, v_w_in, v_pool_w, v_pool_scale,
                         v_dn_conv_w, v_dn_a_log, v_dn_dt_bias, v_dn_out_norm, v_w_out, v_ffn2_norm, v_ffn2_w_gate, v_ffn2_w_up,
                         v_ffn2_w_down, v_final_norm]))
    chip = 2 * lax.axis_index("x") + lax.axis_index("y")

    ffn_names = [(f"ffn{f}_w_gate", f"ffn{f}_w_up", f"ffn{f}_w_down") for f in (1, 2)]
    tr = lambda t: jnp.swapaxes(t, -1, -2)
    def ffn_rows(l, which):
        g, u, dn = ffn_names[which]
        return [tr(W[g][l]), tr(W[u][l]), W[dn][l]]

    def second_half(l):
        return jnp.concatenate(ffn_rows(l, 1) + [jnp.pad(W["w_out"][l], ((0, FC - 256), (0, 0)))], axis=0).astype(bf16)

    ws = _Weights()
    first0, = _all_gather_weights([jnp.concatenate(ffn_rows(0, 0), axis=0).astype(bf16)])
    ws.ffn[(0, 0)] = (first0, 0)

    def landed_00(gathered):
        ws.ffn[(0, 1)], ws.wout[0] = (gathered[0], 0), (gathered[0], 3)
        ws.set_w_in(0, gathered[1])

    def landed_01(gathered):
        ws.ffn[(1, 0)] = (gathered[0], 0)
        ws.set_w_in(1, gathered[1])

    def landed_10(gathered):
        ws.ffn[(1, 1)], ws.wout[1] = (gathered[0], 0), (gathered[0], 3)

    ws.rides[(0, 0)] = ([second_half(0), W["w_in"][0].astype(bf16)], landed_00)
    ws.rides[(0, 1)] = ([jnp.concatenate(ffn_rows(1, 0), axis=0).astype(bf16), W["w_in"][1].astype(bf16)], landed_01)
    ws.rides[(1, 0)] = ([second_half(1)], landed_10)
    conv_all = _gather_small(_pad_rows(dn_conv_w.reshape(DEPTH * 4 * (CW // NCH) // 128, 128), 32))
    conv_full = jnp.concatenate([conv_all[2 * j, :DEPTH * 4 * (CW // NCH) // 128].reshape(DEPTH, 4, CW // NCH) for j in range(NCH)],
                                axis=-1)

    par = jnp.pad(jnp.stack([dn_a_log, dn_dt_bias], axis=1), ((0, 0), (0, 6), (4, 120)))
    wts = dict(ffn1_norm=ffn1_norm, mix_norm=mix_norm, ffn2_norm=ffn2_norm, final_norm=final_norm.reshape(1, D),
               pool_bd=jnp.stack([_block_diag(pool_w[l]) for l in range(DEPTH)]).astype(bf16),
               pool_scale=pool_scale, conv_w=jnp.pad(conv_full, ((0, 0), (0, 4), (0, 0))),
               dn_par=par, dn_out_norm=dn_out_norm)

    c_arr = lax.axis_index("c").astype(jnp.int32).reshape(1)

    def prep(parts):
        return [_sum_core_pair(p, g, c_arr) for p, g in zip(parts, _send_sibling_half(parts))]

    loss, dx, g_final, grads, pieces1, pieces0 = _device_step(x[0], positions.reshape(S, 1), loss_target[0], wts, ws, prep)
    last = [jnp.concatenate([grads[0][n] for n in ffn_names[0]], axis=1)]
    pieces_last = _scatter_to_chips(prep(last))
    full_b = _sum_chips(pieces_last[0], c_arr, None, 0, RB)
    full_b = _sum_chips(pieces0[0], c_arr, full_b, 3 * FC, RB)
    full_b = _sum_chips(pieces1[0], c_arr, full_b, 7 * FC, RB)
    full_c = _sum_chips(pieces0[1], c_arr, None, 0, RC)
    full_c = _sum_chips(pieces1[1], c_arr, full_c, D, RC)
    full_b, full_c = _join_halves([full_b, full_c], [[(0, 3 * FC), (3 * FC, 4 * FC), (7 * FC, 7 * FC)], [(0, D), (D, D)]])

    small = {"loss": loss[0, 0:1], "final_norm": g_final.reshape(D)}
    for n in ("ffn1_norm", "mix_norm", "ffn2_norm", "pool_scale", "dn_out_norm"):
        small[n] = jnp.stack([grads[l][n].reshape(-1) for l in range(DEPTH)])
    small["pool_w"] = jnp.stack([jnp.stack([grads[l]["pool_bd"][64 * g:64 * (g + 1), 64 * g:64 * (g + 1)] for g in range(4)])
                                 for l in range(DEPTH)])
    small["dn_conv_w"] = jnp.stack([grads[l]["conv_w"][0:4] for l in range(DEPTH)])
    small["dn_a_log"] = jnp.stack([grads[l]["dn_par"][0, 4:8] for l in range(DEPTH)])
    small["dn_dt_bias"] = jnp.stack([grads[l]["dn_par"][1, 4:8] for l in range(DEPTH)])
    packed = _pack_small(small)
    g_small = _sum_pieces(_gather_small(packed), f32, "sum_small")
    gs = _unpack_small(g_small)

    transposed = ("ffn1_w_gate", "ffn1_w_up", "ffn2_w_gate", "ffn2_w_up")
    where = {"ffn1_w_gate": (full_b, FC // 2, lambda l: 14 * l), "ffn1_w_up": (full_b, FC // 2, lambda l: 14 * l + 2),
             "ffn1_w_down": (full_b, FC // 2, lambda l: 14 * l + 4), "ffn2_w_gate": (full_b, FC // 2, lambda l: 14 * l + 6),
             "ffn2_w_up": (full_b, FC // 2, lambda l: 14 * l + 8), "ffn2_w_down": (full_b, FC // 2, lambda l: 14 * l + 10),
             "w_out": (full_b, 64, lambda l: (FC // 64) * (7 * l + 6)), "w_in": (full_c, D // 2, lambda l: 2 * l)}
    big_res = {}
    for n, (gblob, tile, first) in where.items():
        t = tr if n in transposed else (lambda a: a)
        big_res[n] = [t(r) for r in _adamw_rows(t(W[n]), t(M[n]), t(V[n]), gblob, tile, first, "adamw_" + n)]

    def small_of(T):
        d = {n: T[n] for n, _ in _SMALL if n not in ("loss", "dn_conv_w")}
        d["loss"] = jnp.zeros((1,), f32)
        d["dn_conv_w"] = _at_own_columns(T["dn_conv_w"], chip)
        return _pack_small(d)

    res_s = _adamw(small_of(W), g_small, small_of(M), small_of(V), "adamw_small")
    small_out = [_unpack_small(r) for r in res_s]

    def split_blobs(b, c):
        out = {}
        b7 = b.reshape(DEPTH, 7, FC, D)
        for k, n in enumerate(n for names3 in ffn_names for n in names3):
            out[n] = tr(b7[:, k]) if n in transposed else b7[:, k]
        out["w_out"] = b7[:, 6, :256]
        out["w_in"] = c.reshape(DEPTH, D, INC)
        return out

    def assemble(big, sm):
        out = []
        for n in names:
            if n in big:
                out.append(big[n])
            elif n == "dn_conv_w":
                out.append(_own_columns(sm[n], chip))
            else:
                out.append(sm[n])
        return out

    grad_list = assemble(split_blobs(full_b, full_c), gs)
    outs = [gs["loss"].reshape(()), dx.reshape(1, S, D)] + grad_list
    for k in range(3):
        outs += assemble({n: r[k] for n, r in big_res.items()}, small_out[k])
    return tuple(outs)
```

```python
import functools
import math

import jax
import jax.numpy as jnp
import numpy as np
from jax import lax
from jax.experimental import pallas as pl
from jax.experimental.pallas import tpu as pltpu

f32 = jnp.float32
bf16 = jnp.bfloat16
SDS = jax.ShapeDtypeStruct
MESH = pl.DeviceIdType.MESH

S = 4096
D = 1024
DEPTH = 2
FF = 2816
NCH = 4
FC = FF // NCH
INW = 3080
INC = INW // NCH
INP = 3200
ATT = 256
EH = 64
NBLK = 128
DNW = 512
DH = 128
CH = 64
NCHUNK = S // CH
EPS = 1e-6
ROPE_THETA = 10000.0
PATTERN_DIL = (1, 4, 16)
ADAM_LR, ADAM_B1, ADAM_B2, ADAM_EPS, ADAM_WD, ADAM_STEP = 0.001, 0.9, 0.999, 1e-08, 0.01, 10
VMEM_BYTES_V7X = 64 * 1024 * 1024
NEG = -1e30

TM = 512
RB, RC = 14 * FC, 2 * D


def _cp(vmem_mb=48, sem=None):
    kw = dict(vmem_limit_bytes=vmem_mb * 1024 * 1024)
    if sem is not None:
        kw["dimension_semantics"] = sem
    return pltpu.CompilerParams(**kw)


def _pc(*args, **kwargs):
    pin = lambda s: pltpu.HBM(s.shape, s.dtype) if isinstance(s, SDS) and jnp.issubdtype(s.dtype, jnp.floating) else s
    out = kwargs["out_shape"]
    kwargs["out_shape"] = [pin(s) for s in out] if isinstance(out, (list, tuple)) else pin(out)
    call = pl.pallas_call(*args, **kwargs)

    def run(*operands):
        pinned = [pltpu.with_memory_space_constraint(o, pltpu.HBM) if jnp.issubdtype(o.dtype, jnp.floating) else o
                  for o in operands]
        return call(*pinned)

    return run


def _dot(a, b):
    return jnp.dot(a, b, preferred_element_type=f32)


def _dot_nt(a, b):
    return lax.dot_general(a, b, (((1,), (1,)), ((), ())), preferred_element_type=f32)


def _dot_tn(a, b):
    return lax.dot_general(a, b, (((0,), (0,)), ((), ())), preferred_element_type=f32)


def _rms(x, w):
    r = lax.rsqrt(jnp.mean(x * x, axis=-1, keepdims=True) + EPS)
    return x * r * w, r


def _rms_bwd(x, w, r, dh):
    xhat = x * r
    dw = jnp.sum(dh * xhat, axis=0, keepdims=True)
    dxh = dh * w
    dx = r * (dxh - xhat * jnp.mean(dxh * xhat, axis=-1, keepdims=True))
    return dx, dw


def _ffn_fwd(x, nw, blob, k0, ride=None):
    kg, ku, kd = k0, k0 + 1, k0 + 2
    nr = 0 if ride is None else len(ride)
    ni = S // TM

    def body(*refs):
        x_ref, nw_ref, wg_ref, wu_ref, wd_ref = refs[:5]
        ride_in = refs[5:5 + nr]
        o_ref = refs[5 + nr]
        ride_out = refs[6 + nr:6 + 2 * nr]
        h_scr, acc_scr = refs[6 + 2 * nr:8 + 2 * nr]
        sems = refs[8 + 2 * nr:]
        i = pl.program_id(0)
        j = pl.program_id(1)

        if nr:
            @pl.when(jnp.logical_and(i == 0, j == 0))
            def _():
                _gather_start(ride_in, ride_out, *sems)

        @pl.when(j == 0)
        def _():
            h, _ = _rms(x_ref[...], nw_ref[...])
            h_scr[...] = h.astype(bf16)
            acc_scr[...] = jnp.zeros_like(acc_scr)

        h = h_scr[...]
        g = _dot_nt(h, wg_ref[0])
        u = _dot_nt(h, wu_ref[0])
        a = (g * jax.nn.sigmoid(g) * u).astype(bf16)
        acc_scr[...] += _dot(a, wd_ref[0])

        @pl.when(j == NCH - 1)
        def _():
            o_ref[...] = x_ref[...] + 0.5 * acc_scr[...]

        if nr:
            @pl.when(jnp.logical_and(i == ni - 1, j == NCH - 1))
            def _():
                _gather_finish(ride_in, ride_out, *sems)

    wspec = lambda k: pl.BlockSpec((1, FC, D), lambda i, j: (j, k, 0))
    rides = [] if ride is None else list(ride)
    res = _pc(
        body, grid=(ni, NCH), name="ffn_fwd_ride" if nr else "ffn_fwd",
        in_specs=[pl.BlockSpec((TM, D), lambda i, j: (i, 0)),
                  pl.BlockSpec((1, D), lambda i, j: (0, 0)),
                  wspec(kg), wspec(ku), wspec(kd)] + [ANY] * nr,
        out_specs=[pl.BlockSpec((TM, D), lambda i, j: (i, 0))] + [ANY] * nr,
        out_shape=[SDS((S, D), f32)] + [SDS((NCH,) + r.shape, r.dtype) for r in rides],
        scratch_shapes=[pltpu.VMEM((TM, D), bf16), pltpu.VMEM((TM, D), f32)] + (_gather_sems(nr) if nr else []),
        compiler_params=_cp(40),
    )(x, nw, blob, blob, blob, *rides)
    return res if nr else res[0]


def _ffn_bwd(x, nw, blob, k0, dy, ride=None):
    nt = S // TM
    kg, ku, kd = k0, k0 + 1, k0 + 2
    nr = 0 if ride is None else len(ride)

    def body(*refs):
        x_ref, nw_ref, wg_ref, wu_ref, wd_ref, dy_ref = refs[:6]
        ride_in = refs[6:6 + nr]
        dwg_ref, dwu_ref, dwd_ref, dh_ref = refs[6 + nr:10 + nr]
        ride_out = refs[10 + nr:10 + 2 * nr]
        ag, au, ad = refs[10 + 2 * nr:13 + 2 * nr]
        sems = refs[13 + 2 * nr:]
        j = pl.program_id(0)
        i = pl.program_id(1)

        if nr:
            @pl.when(jnp.logical_and(j == 0, i == 0))
            def _():
                _scatter_start(ride_in, ride_out, *sems)

        @pl.when(i == 0)
        def _():
            ag[...] = jnp.zeros_like(ag)
            au[...] = jnp.zeros_like(au)
            ad[...] = jnp.zeros_like(ad)

        hf, _ = _rms(x_ref[...], nw_ref[...])
        h = hf.astype(bf16)
        g = _dot_nt(h, wg_ref[0])
        u = _dot_nt(h, wu_ref[0])
        sg = jax.nn.sigmoid(g)
        s = g * sg
        a = (s * u).astype(bf16)
        dyb = (0.5 * dy_ref[...]).astype(bf16)
        da = _dot_nt(dyb, wd_ref[0])
        ad[...] += _dot_tn(a, dyb)
        du = (da * s).astype(bf16)
        dg = (da * u * (sg * (1.0 + g * (1.0 - sg)))).astype(bf16)
        ag[...] += _dot_tn(dg, h)
        au[...] += _dot_tn(du, h)
        dh_ref[0] = (_dot(dg, wg_ref[0]) + _dot(du, wu_ref[0])).astype(bf16)

        @pl.when(i == nt - 1)
        def _():
            dwg_ref[0] = ag[...].astype(bf16)
            dwu_ref[0] = au[...].astype(bf16)
            dwd_ref[0] = ad[...].astype(bf16)

        if nr:
            @pl.when(jnp.logical_and(j == NCH - 1, i == nt - 1))
            def _():
                _scatter_finish(ride_in, ride_out, *sems)

    wspec = lambda k: pl.BlockSpec((1, FC, D), lambda j, i: (j, k, 0))
    gspec = pl.BlockSpec((1, FC, D), lambda j, i: (j, 0, 0))
    rides = [] if ride is None else list(ride)
    return _pc(
        body, grid=(NCH, nt), name="ffn_bwd_ride" if nr else "ffn_bwd",
        in_specs=[pl.BlockSpec((TM, D), lambda j, i: (i, 0)),
                  pl.BlockSpec((1, D), lambda j, i: (0, 0)),
                  wspec(kg), wspec(ku), wspec(kd),
                  pl.BlockSpec((TM, D), lambda j, i: (i, 0))] + [ANY] * nr,
        out_specs=[gspec, gspec, gspec, pl.BlockSpec((1, TM, D), lambda j, i: (j, i, 0))] + [ANY] * nr,
        out_shape=[SDS((NCH, FC, D), bf16)] * 3 + [SDS((NCH, S, D), bf16)] + [SDS(r.shape, r.dtype) for r in rides],
        scratch_shapes=[pltpu.VMEM((FC, D), f32)] * 3 + (_scatter_sems(nr) if nr else []),
        compiler_params=_cp(56),
    )(x, nw, blob, blob, blob, dy, *rides)


def _norm_bwd(x, nw, dres, dh4):
    nt = S // TM
    nparts = dh4.shape[0]

    def body(x_ref, nw_ref, dres_ref, dh_ref, dx_ref, dnw_ref):
        i = pl.program_id(0)
        dh = dh_ref[0].astype(f32)
        for p in range(1, nparts):
            dh = dh + dh_ref[p].astype(f32)
        xv = x_ref[...]
        _, r = _rms(xv, nw_ref[...])
        dx, dw = _rms_bwd(xv, nw_ref[...], r, dh)
        dx_ref[...] = dres_ref[...] + dx

        @pl.when(i == 0)
        def _():
            dnw_ref[...] = jnp.zeros_like(dnw_ref)

        dnw_ref[...] += dw

    return _pc(
        body, grid=(nt,), name="norm_bwd",
        in_specs=[pl.BlockSpec((TM, D), lambda i: (i, 0)),
                  pl.BlockSpec((1, D), lambda i: (0, 0)),
                  pl.BlockSpec((TM, D), lambda i: (i, 0)),
                  pl.BlockSpec((nparts, TM, D), lambda i: (0, i, 0))],
        out_specs=[pl.BlockSpec((TM, D), lambda i: (i, 0)), pl.BlockSpec((1, D), lambda i: (0, 0))],
        out_shape=[SDS((S, D), f32), SDS((1, D), f32)],
        compiler_params=_cp(40),
    )(x, nw, dres, dh4)


def _final(x, nw, target):
    nt = S // TM

    def body(x_ref, nw_ref, t_ref, dx_ref, dnw_ref, loss_ref):
        i = pl.program_id(0)
        xv = x_ref[...]
        y, r = _rms(xv, nw_ref[...])
        err = y - t_ref[...]
        part = 0.5 * jnp.sum(jnp.mean(err * err, axis=-1, keepdims=True), axis=0, keepdims=True)
        dx, dw = _rms_bwd(xv, nw_ref[...], r, err * (1.0 / D))
        dx_ref[...] = dx

        @pl.when(i == 0)
        def _():
            dnw_ref[...] = jnp.zeros_like(dnw_ref)
            loss_ref[...] = jnp.zeros_like(loss_ref)

        dnw_ref[...] += dw
        loss_ref[...] += jnp.broadcast_to(part, loss_ref.shape)

    return _pc(
        body, grid=(nt,), name="final_loss",
        in_specs=[pl.BlockSpec((TM, D), lambda i: (i, 0)),
                  pl.BlockSpec((1, D), lambda i: (0, 0)),
                  pl.BlockSpec((TM, D), lambda i: (i, 0))],
        out_specs=[pl.BlockSpec((TM, D), lambda i: (i, 0)), pl.BlockSpec((1, D), lambda i: (0, 0)),
                   pl.BlockSpec((1, 128), lambda i: (0, 0))],
        out_shape=[SDS((S, D), f32), SDS((1, D), f32), SDS((1, 128), f32)],
        compiler_params=_cp(40),
    )(x, nw, target)


def _rot_half(t):
    lane = lax.broadcasted_iota(jnp.int32, t.shape, 1)
    first = (lane % EH) < (EH // 2)
    return jnp.where(first, -pltpu.roll(t, ATT - EH // 2, 1), pltpu.roll(t, EH // 2, 1))


def _rope_tables(pos_ref, freq_ref):
    ang = pos_ref[...].astype(f32) * freq_ref[...]
    return jnp.cos(ang), jnp.sin(ang)


def _split_residues(val, scr, outs):
    rows, cols = val.shape
    for j in range(cols // 128):
        scr[j] = val[:, 128 * j:128 * (j + 1)]
    for ref, d in outs:
        for j in range(cols // 128):
            for r in range(d):
                ref.at[r][:, 128 * j:128 * (j + 1)] = scr.at[j][pl.ds(r, rows // d, stride=d), :]


def _join_residues(ref, d, scr):
    rows, cols = scr.shape[1], ref.shape[2]
    for j in range(cols // 128):
        for r in range(d):
            scr.at[j][pl.ds(r, rows // d, stride=d), :] = ref.at[r][:, 128 * j:128 * (j + 1)]
    return jnp.concatenate([scr[j] for j in range(cols // 128)], axis=1)


def _res_spec(d, tile, cols):
    return pl.BlockSpec((d, tile // d, cols), lambda i: (0, i, 0))


def _inproj_fwd(x, nw, w_aug, pos, freq):
    TI = 256

    def body(x_ref, nw_ref, w_hbm, pos_ref, freq_ref, att_ref, att4_ref, att16_ref, pu_ref, dq_ref, dz_ref, dba_ref,
             w_scr, r_scr):
        @pl.when(pl.program_id(0) == 0)
        def _():
            pltpu.sync_copy(w_hbm, w_scr)

        h, _ = _rms(x_ref[...], nw_ref[...])
        proj = _dot(h.astype(bf16), w_scr[...])
        cos, sin = _rope_tables(pos_ref, freq_ref)
        q = proj[:, 0:ATT]
        k = proj[:, ATT:2 * ATT]
        att = jnp.concatenate([q * cos + _rot_half(q) * sin, k * cos + _rot_half(k) * sin, proj[:, 2 * ATT:3 * ATT]], axis=1)
        att_ref[...] = att
        _split_residues(att, r_scr, [(att4_ref, 4), (att16_ref, 16)])
        pu_ref[...] = proj[:, 768:1024]
        dq_ref[...] = proj[:, 1024:2560]
        dz_ref[...] = proj[:, 2560:3072]
        dba_ref[...] = proj[:, 3072:3200]

    return _pc(
        body, grid=(S // TI,), name="inproj_fwd",
        in_specs=[pl.BlockSpec((TI, D), lambda i: (i, 0)),
                  pl.BlockSpec((1, D), lambda i: (0, 0)),
                  pl.BlockSpec(memory_space=pl.ANY),
                  pl.BlockSpec((TI, 1), lambda i: (i, 0)),
                  pl.BlockSpec((1, ATT), lambda i: (0, 0))],
        out_specs=[pl.BlockSpec((TI, 768), lambda i: (i, 0)), _res_spec(4, TI, 768), _res_spec(16, TI, 768),
                   pl.BlockSpec((TI, 256), lambda i: (i, 0)),
                   pl.BlockSpec((TI, 1536), lambda i: (i, 0)), pl.BlockSpec((TI, 512), lambda i: (i, 0)),
                   pl.BlockSpec((TI, 128), lambda i: (i, 0))],
        out_shape=[SDS((S, 768), f32), SDS((4, S // 4, 768), f32), SDS((16, S // 16, 768), f32), SDS((S, 256), f32),
                   SDS((S, 1536), f32), SDS((S, 512), f32), SDS((S, 128), f32)],
        scratch_shapes=[pltpu.VMEM((D, INP), bf16), pltpu.VMEM((6, TI, 128), f32)],
        compiler_params=_cp(48),
    )(x, nw, w_aug, pos, freq)


def _inproj_bwd(x, nw, w_aug, pos, freq, dres, datt, datt4, datt16, dpu, ddq, ddz, ddba):
    TI = 256
    nt = S // TI

    def body(x_ref, nw_ref, w_hbm, pos_ref, freq_ref, dres_ref, datt_ref, datt4_ref, datt16_ref, dpu_ref, ddq_ref, ddz_ref,
             ddba_ref, dx_ref, dnw_ref, dw_hbm, w_scr, acc, r_scr):
        i = pl.program_id(0)

        @pl.when(i == 0)
        def _():
            pltpu.sync_copy(w_hbm, w_scr)
            acc[...] = jnp.zeros_like(acc)
            dnw_ref[...] = jnp.zeros_like(dnw_ref)

        xv = x_ref[...]
        hf, r = _rms(xv, nw_ref[...])
        h = hf.astype(bf16)
        cos, sin = _rope_tables(pos_ref, freq_ref)
        datt = datt_ref[...] + _join_residues(datt4_ref, 4, r_scr)
        datt = datt + _join_residues(datt16_ref, 16, r_scr)
        dq = datt[:, 0:ATT]
        dk = datt[:, ATT:2 * ATT]
        dq = dq * cos - _rot_half(dq) * sin
        dk = dk * cos - _rot_half(dk) * sin
        dproj = jnp.concatenate([dq, dk, datt[:, 2 * ATT:3 * ATT], dpu_ref[...], ddq_ref[...], ddz_ref[...], ddba_ref[...]],
                                axis=1).astype(bf16)
        acc[...] += _dot_tn(h, dproj)
        dh = _dot_nt(dproj, w_scr[...])
        dx, dw = _rms_bwd(xv, nw_ref[...], r, dh)
        dx_ref[...] = dres_ref[...] + dx
        dnw_ref[...] += dw

        @pl.when(i == nt - 1)
        def _():
            pltpu.sync_copy(acc, dw_hbm)

    return _pc(
        body, grid=(nt,), name="inproj_bwd",
        in_specs=[pl.BlockSpec((TI, D), lambda i: (i, 0)),
                  pl.BlockSpec((1, D), lambda i: (0, 0)),
                  pl.BlockSpec(memory_space=pl.ANY),
                  pl.BlockSpec((TI, 1), lambda i: (i, 0)),
                  pl.BlockSpec((1, ATT), lambda i: (0, 0)),
                  pl.BlockSpec((TI, D), lambda i: (i, 0)),
                  pl.BlockSpec((TI, 768), lambda i: (i, 0)), _res_spec(4, TI, 768), _res_spec(16, TI, 768),
                  pl.BlockSpec((TI, 256), lambda i: (i, 0)),
                  pl.BlockSpec((TI, 1536), lambda i: (i, 0)),
                  pl.BlockSpec((TI, 512), lambda i: (i, 0)),
                  pl.BlockSpec((TI, 128), lambda i: (i, 0))],
        out_specs=[pl.BlockSpec((TI, D), lambda i: (i, 0)), pl.BlockSpec((1, D), lambda i: (0, 0)),
                   pl.BlockSpec(memory_space=pl.ANY)],
        out_shape=[SDS((S, D), f32), SDS((1, D), f32), SDS((D, INP), f32)],
        scratch_shapes=[pltpu.VMEM((D, INP), bf16), pltpu.VMEM((D, INP), f32), pltpu.VMEM((6, TI, 128), f32)],
        compiler_params=_cp(56),
    )(x, nw, w_aug, pos, freq, dres, datt, datt4, datt16, dpu, ddq, ddz, ddba)


def _outproj_fwd(x, ya, yb, yc, blob_b, kw):
    def body(x_ref, ya_ref, yb_ref, yc_ref, w_ref, o_ref):
        ycat = jnp.concatenate([ya_ref[...], yb_ref[...], yc_ref[...]], axis=1).astype(bf16)
        o_ref[...] = x_ref[...] + _dot(ycat, w_ref[:, 0:256, :].reshape(D, D))

    return _pc(
        body, grid=(S // TM,), name="outproj_fwd",
        in_specs=[pl.BlockSpec((TM, D), lambda i: (i, 0)),
                  pl.BlockSpec((TM, 256), lambda i: (i, 0)),
                  pl.BlockSpec((TM, 256), lambda i: (i, 0)),
                  pl.BlockSpec((TM, 512), lambda i: (i, 0)),
                  pl.BlockSpec((NCH, FC, D), lambda i: (0, kw, 0))],
        out_specs=pl.BlockSpec((TM, D), lambda i: (i, 0)),
        out_shape=SDS((S, D), f32),
        compiler_params=_cp(40),
    )(x, ya, yb, yc, blob_b)


def _outproj_bwd(x, nw, dres, dh4, ya, yb, yc, blob_b, kw):
    nt = S // TM
    nparts = dh4.shape[0]

    def body(x_ref, nw_ref, dres_ref, dh_ref, ya_ref, yb_ref, yc_ref, w_ref, dx_ref, dnw_ref, dya_ref, dyb_ref, dyc_ref, dw_ref):
        i = pl.program_id(0)

        @pl.when(i == 0)
        def _():
            dw_ref[...] = jnp.zeros_like(dw_ref)
            dnw_ref[...] = jnp.zeros_like(dnw_ref)

        dh = dh_ref[0].astype(f32)
        for p in range(1, nparts):
            dh = dh + dh_ref[p].astype(f32)
        xv = x_ref[...]
        _, r = _rms(xv, nw_ref[...])
        dxn, dnw = _rms_bwd(xv, nw_ref[...], r, dh)
        dx = dres_ref[...] + dxn
        dx_ref[...] = dx
        dnw_ref[...] += dnw
        dyv = dx.astype(bf16)
        ycat = jnp.concatenate([ya_ref[...], yb_ref[...], yc_ref[...]], axis=1).astype(bf16)
        dw_ref[...] += _dot_tn(ycat, dyv)
        dcat = _dot_nt(dyv, w_ref[:, 0:256, :].reshape(D, D))
        dya_ref[...] = dcat[:, 0:256]
        dyb_ref[...] = dcat[:, 256:512]
        dyc_ref[...] = dcat[:, 512:1024]

    return _pc(
        body, grid=(nt,), name="outproj_bwd",
        in_specs=[pl.BlockSpec((TM, D), lambda i: (i, 0)),
                  pl.BlockSpec((1, D), lambda i: (0, 0)),
                  pl.BlockSpec((TM, D), lambda i: (i, 0)),
                  pl.BlockSpec((nparts, TM, D), lambda i: (0, i, 0)),
                  pl.BlockSpec((TM, 256), lambda i: (i, 0)),
                  pl.BlockSpec((TM, 256), lambda i: (i, 0)),
                  pl.BlockSpec((TM, 512), lambda i: (i, 0)),
                  pl.BlockSpec((NCH, FC, D), lambda i: (0, kw, 0))],
        out_specs=[pl.BlockSpec((TM, D), lambda i: (i, 0)), pl.BlockSpec((1, D), lambda i: (0, 0)),
                   pl.BlockSpec((TM, 256), lambda i: (i, 0)), pl.BlockSpec((TM, 256), lambda i: (i, 0)),
                   pl.BlockSpec((TM, 512), lambda i: (i, 0)), pl.BlockSpec((D, D), lambda i: (0, 0))],
        out_shape=[SDS((S, D), f32), SDS((1, D), f32), SDS((S, 256), f32), SDS((S, 256), f32), SDS((S, 512), f32),
                   SDS((D, D), f32)],
        compiler_params=_cp(48),
    )(x, nw, dres, dh4, ya, yb, yc, blob_b)


QT = NBLK
NB = S // QT


def _attn_block(q, kp, kc, vp, vc, first):
    kk = jnp.concatenate([kp, kc], axis=0).astype(bf16)
    vv = jnp.concatenate([vp, vc], axis=0).astype(bf16)
    qi = lax.broadcasted_iota(jnp.int32, (4 * QT, NBLK + QT), 0) % QT
    ki = lax.broadcasted_iota(jnp.int32, (4 * QT, NBLK + QT), 1)
    dist = NBLK + qi - ki
    valid = (dist >= 0) & (dist <= NBLK) & (jnp.logical_not(first) | (ki >= NBLK))
    head = lax.broadcasted_iota(jnp.int32, (1, ATT), 1) // EH
    masks = [(head == h).astype(f32) for h in range(4)]
    qs = jnp.concatenate([q * (mh * (1.0 / math.sqrt(EH))) for mh in masks], axis=0).astype(bf16)
    s = _dot_nt(qs, kk)
    s = jnp.where(valid, s, NEG)
    m = lax.stop_gradient(jnp.max(s, axis=-1, keepdims=True))
    p = jnp.exp(s - m)
    den = jnp.sum(p, axis=-1, keepdims=True)
    po = _dot((p * (1.0 / den)).astype(bf16), vv)
    lse = m + jnp.log(den)
    o = jnp.zeros((QT, ATT), f32)
    l = jnp.zeros((QT, ATT), f32)
    for h, mh in enumerate(masks):
        o = o + po[QT * h:QT * (h + 1)] * mh
        l = l + lse[QT * h:QT * (h + 1)] * mh
    return o, l


def _attn_specs(tile):
    own = lambda col: pl.BlockSpec((QT, ATT), lambda s: (tile(s), col))
    prev = lambda col: pl.BlockSpec((NBLK, ATT), lambda s: (jnp.maximum((QT // NBLK) * tile(s) - 1, 0), col))
    return [own(0), prev(1), own(1), prev(2), own(2)]


def _attn_fwd(qkv, per_seq):
    def body(q_ref, kp_ref, kc_ref, vp_ref, vc_ref, o_ref, l_ref):
        first = pl.program_id(0) % per_seq == 0
        o, l = _attn_block(q_ref[...], kp_ref[...], kc_ref[...], vp_ref[...], vc_ref[...], first)
        o_ref[...] = o
        l_ref[...] = l

    blk = pl.BlockSpec((QT, ATT), lambda t: (t, 0))
    return _pc(
        body, grid=(NB,), name="attn_fwd", in_specs=_attn_specs(lambda t: t), out_specs=[blk, blk],
        out_shape=[SDS((S, ATT), f32), SDS((S, ATT), f32)], compiler_params=_cp(32),
    )(qkv, qkv, qkv, qkv, qkv)


def _attn_block_bwd(q, kp, kc, vp, vc, o, l, do, dl, first):
    kk = jnp.concatenate([kp, kc], axis=0).astype(bf16)
    vv = jnp.concatenate([vp, vc], axis=0).astype(bf16)
    qi = lax.broadcasted_iota(jnp.int32, (4 * QT, NBLK + QT), 0) % QT
    ki = lax.broadcasted_iota(jnp.int32, (4 * QT, NBLK + QT), 1)
    dist = NBLK + qi - ki
    valid = (dist >= 0) & (dist <= NBLK) & (jnp.logical_not(first) | (ki >= NBLK))
    head = lax.broadcasted_iota(jnp.int32, (1, ATT), 1) // EH
    masks = [(head == h).astype(f32) for h in range(4)]
    scale = 1.0 / math.sqrt(EH)
    stack = lambda f: jnp.concatenate([f(mh) for mh in masks], axis=0)
    qs = stack(lambda mh: q * (mh * scale)).astype(bf16)
    s = jnp.where(valid, _dot_nt(qs, kk), NEG)
    lse = stack(lambda mh: jnp.max(jnp.where(mh > 0.0, l, NEG), axis=1, keepdims=True))
    p = jnp.exp(s - lse)
    dos = stack(lambda mh: do * mh).astype(bf16)
    dvv = _dot_tn(p.astype(bf16), dos)
    dp = _dot_nt(dos, vv)
    delta = stack(lambda mh: jnp.sum(do * o * mh, axis=1, keepdims=True))
    dlse = stack(lambda mh: jnp.sum(dl * mh, axis=1, keepdims=True))
    ds = (p * (dp - delta + dlse)).astype(bf16)
    dqs = _dot(ds, kk)
    dq = jnp.zeros((QT, ATT), f32)
    for h, mh in enumerate(masks):
        dq = dq + dqs[QT * h:QT * (h + 1)] * (mh * scale)
    dkk = _dot_tn(ds, qs)
    return dq, dkk[:NBLK], dkk[NBLK:], dvv[:NBLK], dvv[NBLK:]


def _attn_bwd(qkv, o, l, do, dl, per_seq):
    def body(q_ref, kp_ref, kc_ref, vp_ref, vc_ref, ofw_ref, lfw_ref, do_ref, dl_ref, o_ref, k_carry, v_carry):
        step = pl.program_id(0)

        @pl.when(step == 0)
        def _():
            k_carry[...] = jnp.zeros_like(k_carry)
            v_carry[...] = jnp.zeros_like(v_carry)

        t = NB - 1 - step
        first = t % per_seq == 0
        last = t % per_seq == per_seq - 1
        dq, dkp, dkc, dvp, dvc = _attn_block_bwd(q_ref[...], kp_ref[...], kc_ref[...], vp_ref[...], vc_ref[...], ofw_ref[...],
                                                 lfw_ref[...], do_ref[...], dl_ref[...], first)
        o_ref[:, 0:ATT] = dq
        o_ref[:, ATT:2 * ATT] = dkc
        o_ref[:, 2 * ATT:3 * ATT] = dvc
        o_ref[QT - NBLK:QT, ATT:2 * ATT] += jnp.where(last, 0.0, k_carry[...])
        o_ref[QT - NBLK:QT, 2 * ATT:3 * ATT] += jnp.where(last, 0.0, v_carry[...])
        k_carry[...] = dkp
        v_carry[...] = dvp

    rev = lambda s: NB - 1 - s
    blk = pl.BlockSpec((QT, ATT), lambda s: (rev(s), 0))
    return _pc(
        body, grid=(NB,), name="attn_bwd", in_specs=_attn_specs(rev) + [blk, blk, blk, blk],
        out_specs=pl.BlockSpec((QT, 768), lambda s: (rev(s), 0)),
        out_shape=SDS((S, 768), f32), scratch_shapes=[pltpu.VMEM((NBLK, ATT), f32)] * 2, compiler_params=_cp(40),
    )(qkv, qkv, qkv, qkv, qkv, o, l, do, dl)


def _merge_weights(l0, l1, l2):
    m = jnp.maximum(jnp.maximum(l0, l1), l2)
    e0, e1, e2 = jnp.exp(l0 - m), jnp.exp(l1 - m), jnp.exp(l2 - m)
    tot = e0 + e1 + e2
    return e0 / tot, e1 / tot, e2 / tot


def _merge_specs():
    nat = pl.BlockSpec((TM, ATT), lambda i: (i, 0))
    return nat, _res_spec(4, TM, ATT), _res_spec(16, TM, ATT)


def _merge_fwd(o1, l1, o4, l4, o16, l16):
    def body(o1_ref, l1_ref, o4_ref, l4_ref, o16_ref, l16_ref, y_ref, scr):
        o4v, l4v = _join_residues(o4_ref, 4, scr), _join_residues(l4_ref, 4, scr)
        o16v, l16v = _join_residues(o16_ref, 16, scr), _join_residues(l16_ref, 16, scr)
        w0, w1, w2 = _merge_weights(l1_ref[...], l4v, l16v)
        y_ref[...] = w0 * o1_ref[...] + w1 * o4v + w2 * o16v

    nat, r4, r16 = _merge_specs()
    return _pc(body, grid=(S // TM,), name="merge_fwd", in_specs=[nat, nat, r4, r4, r16, r16],
                          out_specs=nat, out_shape=SDS((S, ATT), f32), scratch_shapes=[pltpu.VMEM((2, TM, 128), f32)],
                          compiler_params=_cp(32))(o1, l1, o4, l4, o16, l16)


def _merge_bwd(o1, l1, o4, l4, o16, l16, dy):
    def body(o1_ref, l1_ref, o4_ref, l4_ref, o16_ref, l16_ref, dy_ref, do1_ref, dl1_ref, do4_ref, dl4_ref, do16_ref, dl16_ref, scr):
        o4v, l4v = _join_residues(o4_ref, 4, scr), _join_residues(l4_ref, 4, scr)
        o16v, l16v = _join_residues(o16_ref, 16, scr), _join_residues(l16_ref, 16, scr)
        o1v = o1_ref[...]
        w0, w1, w2 = _merge_weights(l1_ref[...], l4v, l16v)
        y = w0 * o1v + w1 * o4v + w2 * o16v
        dyv = dy_ref[...]
        do1_ref[...] = w0 * dyv
        dl1_ref[...] = w0 * (o1v - y) * dyv
        _split_residues(w1 * dyv, scr, [(do4_ref, 4)])
        _split_residues(w1 * (o4v - y) * dyv, scr, [(dl4_ref, 4)])
        _split_residues(w2 * dyv, scr, [(do16_ref, 16)])
        _split_residues(w2 * (o16v - y) * dyv, scr, [(dl16_ref, 16)])

    nat, r4, r16 = _merge_specs()
    return _pc(body, grid=(S // TM,), name="merge_bwd", in_specs=[nat, nat, r4, r4, r16, r16, nat],
                          out_specs=[nat, nat, r4, r4, r16, r16],
                          out_shape=[SDS((S, ATT), f32)] * 2 + [SDS((4, S // 4, ATT), f32)] * 2 + [SDS((16, S // 16, ATT), f32)] * 2,
                          scratch_shapes=[pltpu.VMEM((2, TM, 128), f32)], compiler_params=_cp(32))(o1, l1, o4, l4, o16, l16, dy)


HALO = 16


def _pool_consts(i, rows):
    grp = lax.broadcasted_iota(jnp.int32, (rows, 256), 1) // 64
    t = i * TM + lax.broadcasted_iota(jnp.int32, (rows, 256), 0)
    win = jnp.where(grp == 0, 2, jnp.where(grp == 1, 4, jnp.where(grp == 2, 8, 16)))
    cnt = jnp.minimum(t + 1, win).astype(f32)
    return grp, cnt


def _pool_select(grp, s2, s4, s8, s16):
    return jnp.where(grp == 0, s2, jnp.where(grp == 1, s4, jnp.where(grp == 2, s8, s16)))


def _pooled(i, cur, halo):
    xx = jnp.concatenate([halo, cur], axis=0)
    s2 = xx + pltpu.roll(xx, 1, 0)
    s4 = s2 + pltpu.roll(s2, 2, 0)
    s8 = s4 + pltpu.roll(s4, 4, 0)
    s16 = s8 + pltpu.roll(s8, 8, 0)
    grp, cnt = _pool_consts(i, TM)
    tot = _pool_select(grp, s2[HALO:], s4[HALO:], s8[HALO:], s16[HALO:])
    return tot / cnt - cur


def _pool_fwd(u, wp, scale):
    def body(u_ref, halo_ref, wp_ref, sc_ref, y_ref):
        i = pl.program_id(0)
        halo = halo_ref[...] * (i > 0).astype(f32)
        pooled = _pooled(i, u_ref[...], halo)
        y_ref[...] = _dot(pooled.astype(bf16), wp_ref[...]) * sc_ref[...]

    return _pc(
        body, grid=(S // TM,), name="pool_fwd",
        in_specs=[pl.BlockSpec((TM, 256), lambda i: (i, 0)),
                  pl.BlockSpec((HALO, 256), lambda i: (jnp.maximum(i * (TM // HALO) - 1, 0), 0)),
                  pl.BlockSpec((256, 256), lambda i: (0, 0)),
                  pl.BlockSpec((1, 256), lambda i: (0, 0))],
        out_specs=pl.BlockSpec((TM, 256), lambda i: (i, 0)), out_shape=SDS((S, 256), f32), compiler_params=_cp(32),
    )(u, u, wp, scale)


def _pool_bwd(u, wp, scale, dy):
    nt = S // TM

    def body(u_ref, halo_ref, wp_ref, sc_ref, dy_ref, dyn_ref, du_ref, dwp_ref, dsc_ref):
        i = pl.program_id(0)

        @pl.when(i == 0)
        def _():
            dwp_ref[...] = jnp.zeros_like(dwp_ref)
            dsc_ref[...] = jnp.zeros_like(dsc_ref)

        halo = halo_ref[...] * (i > 0).astype(f32)
        pooled = _pooled(i, u_ref[...], halo).astype(bf16)
        dyv = dy_ref[...]
        dsc_ref[...] += jnp.sum(dyv * _dot(pooled, wp_ref[...]), axis=0, keepdims=True)
        dys = (dyv * sc_ref[...]).astype(bf16)
        dwp_ref[...] += _dot_tn(pooled, dys)
        dpool = _dot_nt(dys, wp_ref[...])
        grp, cnt = _pool_consts(i, TM)
        dyn = ((dyn_ref[...] * (i < nt - 1).astype(f32)) * sc_ref[...]).astype(bf16)
        _, cntn = _pool_consts(i + 1, HALO)
        zn = _dot_nt(dyn, wp_ref[...]) / cntn
        zz = jnp.concatenate([dpool / cnt, zn], axis=0)
        n = TM + HALO
        a2 = zz + pltpu.roll(zz, n - 1, 0)
        a4 = a2 + pltpu.roll(a2, n - 2, 0)
        a8 = a4 + pltpu.roll(a4, n - 4, 0)
        a16 = a8 + pltpu.roll(a8, n - 8, 0)
        du_ref[...] = _pool_select(grp, a2[:TM], a4[:TM], a8[:TM], a16[:TM]) - dpool

    return _pc(
        body, grid=(nt,), name="pool_bwd",
        in_specs=[pl.BlockSpec((TM, 256), lambda i: (i, 0)),
                  pl.BlockSpec((HALO, 256), lambda i: (jnp.maximum(i * (TM // HALO) - 1, 0), 0)),
                  pl.BlockSpec((256, 256), lambda i: (0, 0)),
                  pl.BlockSpec((1, 256), lambda i: (0, 0)),
                  pl.BlockSpec((TM, 256), lambda i: (i, 0)),
                  pl.BlockSpec((HALO, 256), lambda i: (jnp.minimum((i + 1) * (TM // HALO), S // HALO - 1), 0))],
        out_specs=[pl.BlockSpec((TM, 256), lambda i: (i, 0)), pl.BlockSpec((256, 256), lambda i: (0, 0)),
                   pl.BlockSpec((1, 256), lambda i: (0, 0))],
        out_shape=[SDS((S, 256), f32), SDS((256, 256), f32), SDS((1, 256), f32)], compiler_params=_cp(32),
    )(u, u, wp, scale, dy, dy)


CW = 3 * DNW
CHALO = 8
TC = 256


def _conv_fwd(u, w):
    def body(u_ref, halo_ref, w_ref, c_ref):
        i = pl.program_id(0)
        xx = jnp.concatenate([halo_ref[...] * (i > 0).astype(f32), u_ref[...]], axis=0)
        c = (w_ref[3:4, :] * xx + w_ref[2:3, :] * pltpu.roll(xx, 1, 0) + w_ref[1:2, :] * pltpu.roll(xx, 2, 0)
             + w_ref[0:1, :] * pltpu.roll(xx, 3, 0))
        c_ref[...] = c[CHALO:]

    return _pc(
        body, grid=(S // TC,), name="conv_fwd",
        in_specs=[pl.BlockSpec((TC, CW), lambda i: (i, 0)),
                  pl.BlockSpec((CHALO, CW), lambda i: (jnp.maximum(i * (TC // CHALO) - 1, 0), 0)),
                  pl.BlockSpec((8, CW), lambda i: (0, 0))],
        out_specs=pl.BlockSpec((TC, CW), lambda i: (i, 0)), out_shape=SDS((S, CW), f32), compiler_params=_cp(32),
    )(u, u, w)


def _conv_bwd(u, w, dc):
    nt = S // TC

    def body(u_ref, halo_ref, w_ref, dc_ref, dcn_ref, du_ref, dw_ref):
        i = pl.program_id(0)

        @pl.when(i == 0)
        def _():
            dw_ref[...] = jnp.zeros_like(dw_ref)

        dcv = dc_ref[...]
        zz = jnp.concatenate([dcv, dcn_ref[...] * (i < nt - 1).astype(f32)], axis=0)
        n = TC + CHALO
        du = (w_ref[3:4, :] * zz + w_ref[2:3, :] * pltpu.roll(zz, n - 1, 0) + w_ref[1:2, :] * pltpu.roll(zz, n - 2, 0)
              + w_ref[0:1, :] * pltpu.roll(zz, n - 3, 0))
        du_ref[...] = du[:TC]
        xx = jnp.concatenate([halo_ref[...] * (i > 0).astype(f32), u_ref[...]], axis=0)
        for j in range(4):
            shifted = xx if j == 3 else pltpu.roll(xx, 3 - j, 0)
            dw_ref[j:j + 1, :] += jnp.sum(dcv * shifted[CHALO:], axis=0, keepdims=True)

    return _pc(
        body, grid=(nt,), name="conv_bwd",
        in_specs=[pl.BlockSpec((TC, CW), lambda i: (i, 0)),
                  pl.BlockSpec((CHALO, CW), lambda i: (jnp.maximum(i * (TC // CHALO) - 1, 0), 0)),
                  pl.BlockSpec((8, CW), lambda i: (0, 0)),
                  pl.BlockSpec((TC, CW), lambda i: (i, 0)),
                  pl.BlockSpec((CHALO, CW), lambda i: (jnp.minimum((i + 1) * (TC // CHALO), S // CHALO - 1), 0))],
        out_specs=[pl.BlockSpec((TC, CW), lambda i: (i, 0)), pl.BlockSpec((8, CW), lambda i: (0, 0))],
        out_shape=[SDS((S, CW), f32), SDS((8, CW), f32)], compiler_params=_cp(32),
    )(u, u, w, dc, dc)


TL = 512
NCL = TL // CH


def _ein(spec, a, b):
    return jnp.einsum(spec, a.astype(bf16), b.astype(bf16), preferred_element_type=f32)


def _ein_ct(spec, x, y, ct_first):
    ct = x if ct_first else y
    hi = ct.astype(bf16)
    lo = ct - hi.astype(f32)
    if ct_first:
        return _ein(spec, hi, y) + _ein(spec, lo, y)
    return _ein(spec, x, hi) + _ein(spec, x, lo)


def _bf16_dot(spec, grad_a, grad_b):
    @jax.custom_vjp
    def dot(a, b):
        return _ein(spec, a, b)

    def fwd(a, b):
        return _ein(spec, a, b), (a, b)

    def bwd(res, ct):
        a, b = res
        return grad_a(a, b, ct), grad_b(a, b, ct)

    dot.defvjp(fwd, bwd)
    return dot


def _bdot(a, b):
    return _ein('nik,nkj->nij', a, b)


def _bdot_nt(a, b):
    return _ein('nik,njk->nij', a, b)


def _bdot_tn(a, b):
    return _ein('nki,nkj->nij', a, b)


_mm = _bf16_dot('ik,kj->ij', lambda a, b, ct: _ein_ct('ij,kj->ik', ct, b, True), lambda a, b, ct: _ein_ct('ik,ij->kj', a, ct, False))
_mm_tn = _bf16_dot('ki,kj->ij', lambda a, b, ct: _ein_ct('kj,ij->ki', b, ct, False),
                   lambda a, b, ct: _ein_ct('ki,ij->kj', a, ct, False))


@jax.custom_vjp
def _inv_unit_lower(a):
    ii = lax.broadcasted_iota(jnp.int32, (1, CH, CH), 1)
    jj = lax.broadcasted_iota(jnp.int32, (1, CH, CH), 2)
    t = (ii == jj).astype(f32) - a
    p = a
    for _ in range(5):
        p = _bdot(p, p)
        t = t + _bdot(t, p)
    return t


def _inv_unit_lower_fwd(a):
    t = _inv_unit_lower(a)
    return t, t


def _inv_unit_lower_bwd(t, dt):
    return (-_bdot_tn(t, _bdot_nt(dt, t)),)


_inv_unit_lower.defvjp(_inv_unit_lower_fwd, _inv_unit_lower_bwd)


def _dn_local(c, dba, a_row, b_row):
    act = c * jax.nn.sigmoid(c)
    lane = lax.broadcasted_iota(jnp.int32, (TL, 128), 1)
    beta_all = jax.nn.sigmoid(dba)
    xs = dba + b_row
    softplus = jnp.maximum(xs, 0.0) + jnp.log(1.0 + jnp.exp(-jnp.abs(xs)))
    g_all = -jnp.exp(a_row) * softplus
    ii = lax.broadcasted_iota(jnp.int32, (1, CH, CH), 1)
    jj = lax.broadcasted_iota(jnp.int32, (1, CH, CH), 2)
    lower = jj <= ii
    strict = jj < ii
    eye = (ii == jj).astype(f32)
    us, ws, qgs, kds, intras = [], [], [], [], []
    aux = jnp.zeros((TL, 128), f32)
    for h in range(4):
        q = act[:, DH * h:DH * (h + 1)]
        k = act[:, DNW + DH * h:DNW + DH * (h + 1)]
        v = act[:, 2 * DNW + DH * h:2 * DNW + DH * (h + 1)]
        q = q * lax.rsqrt(jnp.sum(q * q, axis=-1, keepdims=True) + EPS) * (DH ** -0.5)
        k = k * lax.rsqrt(jnp.sum(k * k, axis=-1, keepdims=True) + EPS)
        beta = jnp.sum(jnp.where(lane == h, beta_all, 0.0), axis=1, keepdims=True)
        g = jnp.sum(jnp.where(lane == 4 + h, g_all, 0.0), axis=1, keepdims=True)
        q3, k3, v3 = q.reshape(NCL, CH, DH), k.reshape(NCL, CH, DH), v.reshape(NCL, CH, DH)
        beta3, g3 = beta.reshape(NCL, CH, 1), g.reshape(NCL, CH, 1)
        g_row = jnp.sum(eye * g3, axis=1, keepdims=True)
        gc_col = jnp.sum(jnp.where(lower, g_row, 0.0), axis=2, keepdims=True)
        gc_row = jnp.sum(jnp.where(ii <= jj, g3, 0.0), axis=1, keepdims=True)
        diff = gc_col - gc_row
        decay = jnp.where(lower, jnp.exp(jnp.where(lower, diff, 0.0)), 0.0)
        kb = k3 * beta3
        vb = v3 * beta3
        a = jnp.where(strict, _bdot_nt(kb, k3) * decay, 0.0)
        t = _inv_unit_lower(a)
        u3 = _bdot(t, vb)
        w3 = _bdot(t, kb * jnp.exp(gc_col))
        intra = jnp.where(lower, _bdot_nt(q3, k3) * decay, 0.0)
        g_last = jnp.sum(g3, axis=1, keepdims=True)
        us.append(u3.reshape(TL, DH))
        ws.append(w3.reshape(TL, DH))
        qgs.append((q3 * jnp.exp(gc_col)).reshape(TL, DH))
        kds.append((k3 * jnp.exp(g_last - gc_col)).reshape(TL, DH))
        intras.append(intra.reshape(TL, CH))
        e_last = jnp.broadcast_to(jnp.exp(g_last), (NCL, CH, 1)).reshape(TL, 1)
        aux = aux + jnp.where(lane == h, e_last, 0.0)
    cat = lambda xs: jnp.concatenate(xs, axis=1)
    return cat(us), cat(ws), cat(qgs), cat(kds), jnp.stack(intras, axis=0), aux


def _dn_local_fwd(c, dba, par):
    def body(c_ref, dba_ref, par_ref, u_ref, w_ref, qg_ref, kd_ref, in_ref, aux_ref):
        u, w, qg, kd, intra, aux = _dn_local(c_ref[...], dba_ref[...], par_ref[0:1, :], par_ref[1:2, :])
        u_ref[...] = u
        w_ref[...] = w
        qg_ref[...] = qg
        kd_ref[...] = kd
        in_ref[...] = intra
        aux_ref[...] = aux

    wide = pl.BlockSpec((TL, DNW), lambda i: (i, 0))
    return _pc(
        body, grid=(S // TL,), name="dn_local_fwd",
        in_specs=[pl.BlockSpec((TL, CW), lambda i: (i, 0)), pl.BlockSpec((TL, 128), lambda i: (i, 0)),
                  pl.BlockSpec((8, 128), lambda i: (0, 0))],
        out_specs=[wide, wide, wide, wide, pl.BlockSpec((4, TL, CH), lambda i: (0, i, 0)),
                   pl.BlockSpec((TL, 128), lambda i: (i, 0))],
        out_shape=[SDS((S, DNW), f32)] * 4 + [SDS((4, S, CH), f32), SDS((S, 128), f32)], compiler_params=_cp(48),
    )(c, dba, par)


def _dn_local_bwd(c, dba, par, du, dw, dqg, dkd, dintra, daux):
    def body(c_ref, dba_ref, par_ref, du_ref, dw_ref, dqg_ref, dkd_ref, din_ref, daux_ref, dc_ref, ddba_ref, dpar_ref):
        @pl.when(pl.program_id(0) == 0)
        def _():
            dpar_ref[...] = jnp.zeros_like(dpar_ref)

        _, vjp = jax.vjp(_dn_local, c_ref[...], dba_ref[...], par_ref[0:1, :], par_ref[1:2, :])
        dc, ddba, da_row, db_row = vjp((du_ref[...], dw_ref[...], dqg_ref[...], dkd_ref[...], din_ref[...], daux_ref[...]))
        dc_ref[...] = dc
        ddba_ref[...] = ddba
        dpar_ref[0:1, :] += da_row
        dpar_ref[1:2, :] += db_row

    wide = pl.BlockSpec((TL, DNW), lambda i: (i, 0))
    return _pc(
        body, grid=(S // TL,), name="dn_local_bwd",
        in_specs=[pl.BlockSpec((TL, CW), lambda i: (i, 0)), pl.BlockSpec((TL, 128), lambda i: (i, 0)),
                  pl.BlockSpec((8, 128), lambda i: (0, 0)), wide, wide, wide, wide,
                  pl.BlockSpec((4, TL, CH), lambda i: (0, i, 0)), pl.BlockSpec((TL, 128), lambda i: (i, 0))],
        out_specs=[pl.BlockSpec((TL, CW), lambda i: (i, 0)), pl.BlockSpec((TL, 128), lambda i: (i, 0)),
                   pl.BlockSpec((8, 128), lambda i: (0, 0))],
        out_shape=[SDS((S, CW), f32), SDS((S, 128), f32), SDS((8, 128), f32)], compiler_params=_cp(56),
    )(c, dba, par, du, dw, dqg, dkd, dintra, daux)


def _dn_step(state, u, w, qg, kd, intra, aux):
    lane = lax.broadcasted_iota(jnp.int32, (CH, 128), 1)
    row = lax.broadcasted_iota(jnp.int32, (CH, 128), 0)
    outs, states = [], []
    for h in range(4):
        sl = slice(DH * h, DH * (h + 1))
        st = state[h]
        e = jnp.sum(jnp.sum(jnp.where((lane == h) & (row == 0), aux, 0.0), axis=1, keepdims=True), axis=0, keepdims=True)
        v_new = u[:, sl] - _mm(w[:, sl], st)
        outs.append(_mm(qg[:, sl], st) + _mm(intra[h], v_new))
        states.append(st * e + _mm_tn(kd[:, sl], v_new))
    return jnp.concatenate(outs, axis=1), jnp.stack(states, axis=0)


CPS = 8
NSTEP = NCHUNK // CPS


def _dn_rec_specs(index):
    wide = pl.BlockSpec((CPS * CH, DNW), lambda n: (index(n), 0))
    inb = pl.BlockSpec((4, CPS * CH, CH), lambda n: (0, index(n), 0))
    auxb = pl.BlockSpec((CPS * CH, 128), lambda n: (index(n), 0))
    stb = pl.BlockSpec((CPS, 4, DH, DH), lambda n: (index(n), 0, 0, 0))
    return wide, inb, auxb, stb


def _dn_rec_fwd(u, w, qg, kd, intra, aux):
    def body(u_ref, w_ref, qg_ref, kd_ref, in_ref, aux_ref, o_ref, st_ref, st_scr):
        @pl.when(pl.program_id(0) == 0)
        def _():
            st_scr[...] = jnp.zeros_like(st_scr)

        st = st_scr[...]
        for k in range(CPS):
            rows = slice(CH * k, CH * (k + 1))
            st_ref[k] = st
            o, st = _dn_step(st, u_ref[rows, :], w_ref[rows, :], qg_ref[rows, :], kd_ref[rows, :], in_ref[:, rows, :],
                             aux_ref[rows, :])
            o_ref[rows, :] = o
        st_scr[...] = st

    wide, inb, auxb, stb = _dn_rec_specs(lambda n: n)
    return _pc(
        body, grid=(NSTEP,), name="dn_rec_fwd", in_specs=[wide, wide, wide, wide, inb, auxb], out_specs=[wide, stb],
        out_shape=[SDS((S, DNW), f32), SDS((NCHUNK, 4, DH, DH), f32)],
        scratch_shapes=[pltpu.VMEM((4, DH, DH), f32)], compiler_params=_cp(32),
    )(u, w, qg, kd, intra, aux)


def _dn_rec_bwd(u, w, qg, kd, intra, aux, states, do):
    def body(u_ref, w_ref, qg_ref, kd_ref, in_ref, aux_ref, st_ref, do_ref,
             du_ref, dw_ref, dqg_ref, dkd_ref, din_ref, daux_ref, ds_scr):
        @pl.when(pl.program_id(0) == 0)
        def _():
            ds_scr[...] = jnp.zeros_like(ds_scr)

        ds = ds_scr[...]
        for k in reversed(range(CPS)):
            rows = slice(CH * k, CH * (k + 1))
            _, vjp = jax.vjp(_dn_step, st_ref[k], u_ref[rows, :], w_ref[rows, :], qg_ref[rows, :], kd_ref[rows, :],
                             in_ref[:, rows, :], aux_ref[rows, :])
            ds, du, dw, dqg, dkd, din, daux = vjp((do_ref[rows, :], ds))
            du_ref[rows, :] = du
            dw_ref[rows, :] = dw
            dqg_ref[rows, :] = dqg
            dkd_ref[rows, :] = dkd
            din_ref[:, rows, :] = din
            daux_ref[rows, :] = daux
        ds_scr[...] = ds

    wide, inb, auxb, stb = _dn_rec_specs(lambda n: NSTEP - 1 - n)
    return _pc(
        body, grid=(NSTEP,), name="dn_rec_bwd", in_specs=[wide, wide, wide, wide, inb, auxb, stb, wide],
        out_specs=[wide, wide, wide, wide, inb, auxb],
        out_shape=[SDS((S, DNW), f32)] * 4 + [SDS((4, S, CH), f32), SDS((S, 128), f32)],
        scratch_shapes=[pltpu.VMEM((4, DH, DH), f32)], compiler_params=_cp(40),
    )(u, w, qg, kd, intra, aux, states, do)


def _dn_post(o, z, nw):
    parts = []
    for h in range(4):
        sl = slice(DH * h, DH * (h + 1))
        oh = o[:, sl]
        y = oh * lax.rsqrt(jnp.mean(oh * oh, axis=-1, keepdims=True) + EPS) * nw
        zh = z[:, sl]
        parts.append(y * (zh * jax.nn.sigmoid(zh)))
    return jnp.concatenate(parts, axis=1)


def _dn_post_fwd(o, z, nw):
    def body(o_ref, z_ref, nw_ref, y_ref):
        y_ref[...] = _dn_post(o_ref[...], z_ref[...], nw_ref[...])

    wide = pl.BlockSpec((TM, DNW), lambda i: (i, 0))
    return _pc(body, grid=(S // TM,), name="dn_post_fwd",
                          in_specs=[wide, wide, pl.BlockSpec((1, 128), lambda i: (0, 0))], out_specs=wide,
                          out_shape=SDS((S, DNW), f32), compiler_params=_cp(32))(o, z, nw)


def _dn_post_bwd(o, z, nw, dy):
    def body(o_ref, z_ref, nw_ref, dy_ref, do_ref, dz_ref, dnw_ref):
        @pl.when(pl.program_id(0) == 0)
        def _():
            dnw_ref[...] = jnp.zeros_like(dnw_ref)

        _, vjp = jax.vjp(_dn_post, o_ref[...], z_ref[...], nw_ref[...])
        do, dz, dnw = vjp(dy_ref[...])
        do_ref[...] = do
        dz_ref[...] = dz
        dnw_ref[...] += dnw

    wide = pl.BlockSpec((TM, DNW), lambda i: (i, 0))
    one = pl.BlockSpec((1, 128), lambda i: (0, 0))
    return _pc(body, grid=(S // TM,), name="dn_post_bwd", in_specs=[wide, wide, one, wide],
                          out_specs=[wide, wide, one], out_shape=[SDS((S, DNW), f32), SDS((S, DNW), f32), SDS((1, 128), f32)],
                          compiler_params=_cp(32))(o, z, nw, dy)


def _row_tile(rows, width, itemsize=4, target=2 * 1024 * 1024):
    best = None
    for t in range(16, rows + 1, 16):
        if rows % t == 0 and t * width * itemsize <= target:
            best = t
    return best if best is not None else rows


def _sum_pieces(pieces, out_dtype, name):
    n, rows, width = pieces.shape
    tr = _row_tile(rows, width * n)

    def body(p_ref, o_ref):
        acc = p_ref[0].astype(f32)
        for s in range(1, n):
            acc = acc + p_ref[s].astype(f32)
        o_ref[...] = acc.astype(out_dtype)

    return _pc(body, grid=(rows // tr,), name=name,
                          in_specs=[pl.BlockSpec((n, tr, width), lambda i: (0, i, 0))],
                          out_specs=pl.BlockSpec((tr, width), lambda i: (i, 0)),
                          out_shape=SDS((rows, width), out_dtype), compiler_params=_cp(32))(pieces)


def _sum_core_pair(part, got, c_arr):
    n, rows, width = part.shape
    half = rows // 2
    tr = _row_tile(half, width, itemsize=2)
    nt = half // tr

    def body(c_ref, p_ref, g_ref, o_ref):
        o_ref[...] = (p_ref[...].astype(f32) + g_ref[...].astype(f32)).astype(bf16)

    gs = pltpu.PrefetchScalarGridSpec(
        num_scalar_prefetch=1, grid=(n, nt),
        in_specs=[pl.BlockSpec((1, tr, width), lambda j, i, c: (j, c[0] * nt + i, 0)),
                  pl.BlockSpec((1, tr, width), lambda j, i, c: (j, i, 0))],
        out_specs=pl.BlockSpec((1, tr, width), lambda j, i, c: (j, i, 0)))
    return _pc(body, grid_spec=gs, name="sum_core_pair", out_shape=SDS((n, half, width), bf16),
                          compiler_params=_cp(32))(c_arr, part, got)


def _sum_chips(pieces, c_arr, full, row0, total_rows):
    n, half, width = pieces.shape
    tr = max(t for t in range(16, 257, 16) if half % t == 0 and row0 % t == 0)
    nt = half // tr

    def body(c_ref, p_ref, *rest):
        o_ref = rest[-1]
        acc = p_ref[0].astype(f32)
        for s in range(1, n):
            acc = acc + p_ref[s].astype(f32)
        o_ref[...] = acc

    gs = pltpu.PrefetchScalarGridSpec(
        num_scalar_prefetch=1, grid=(nt,),
        in_specs=[pl.BlockSpec((n, tr, width), lambda i, c: (0, i, 0))] + ([] if full is None else [ANY]),
        out_specs=pl.BlockSpec((tr, width), lambda i, c: (row0 // tr + c[0] * nt + i, 0)))
    args = (c_arr, pieces) if full is None else (c_arr, pieces, full)
    return _pc(body, grid_spec=gs, name="sum_chips", out_shape=SDS((total_rows, width), f32),
                          input_output_aliases={} if full is None else {2: 0}, compiler_params=_cp(32))(*args)


def _adamw_math(w, g, m, v):
    mn = ADAM_B1 * m + (1.0 - ADAM_B1) * g
    vn = ADAM_B2 * v + (1.0 - ADAM_B2) * (g * g)
    m_hat = mn / (1.0 - ADAM_B1 ** ADAM_STEP)
    v_hat = vn / (1.0 - ADAM_B2 ** ADAM_STEP)
    return -ADAM_LR * (m_hat / (jnp.sqrt(v_hat) + ADAM_EPS) + ADAM_WD * w), mn, vn


def _adamw(w, g, m, v, name):
    rows, width = w.shape
    tr = _row_tile(rows, width * 7, target=12 * 1024 * 1024)

    def body(w_ref, g_ref, m_ref, v_ref, d_ref, nm_ref, nv_ref):
        d_ref[...], nm_ref[...], nv_ref[...] = _adamw_math(w_ref[...], g_ref[...], m_ref[...], v_ref[...])

    blk = pl.BlockSpec((tr, width), lambda i: (i, 0))
    return _pc(body, grid=(rows // tr,), name=name, in_specs=[blk] * 4, out_specs=[blk] * 3,
                          out_shape=[SDS((rows, width), f32)] * 3, compiler_params=_cp(40))(w, g, m, v)


def _adamw_rows(w, m, v, gblob, tr, first_tile, name):
    layers, rows, width = w.shape

    def body(w_ref, g_ref, m_ref, v_ref, d_ref, nm_ref, nv_ref):
        d_ref[0], nm_ref[0], nv_ref[0] = _adamw_math(w_ref[0], g_ref[...], m_ref[0], v_ref[0])

    blk = pl.BlockSpec((1, tr, width), lambda l, i: (l, i, 0))
    gblk = pl.BlockSpec((tr, width), lambda l, i: (first_tile(l) + i, 0))
    return _pc(body, grid=(layers, rows // tr), name=name, in_specs=[blk, gblk, blk, blk], out_specs=[blk] * 3,
                          out_shape=[SDS(w.shape, f32)] * 3, compiler_params=_cp(40))(w, gblob, m, v)


ANY = pl.BlockSpec(memory_space=pl.ANY)


def _place():
    x, y, c = lax.axis_index("x"), lax.axis_index("y"), lax.axis_index("c")
    chips = [(1 - x, y), (x, 1 - y), (1 - x, 1 - y)]
    return x, y, c, chips


NQ_ICI = 4
NQ_D2D = 8


def _chunks(rows, want):
    n = max(k for k in range(1, want + 1) if rows % k == 0 and (rows // k) % 16 == 0)
    step = rows // n
    return [(q * step, step) for q in range(n)]


def _scatter_copies(ins, outs, ssem, rsem, lsem):
    x, y, c, chips = _place()
    me = (x, y, c)
    locals_, sends, lands = [], [], []
    for b in range(len(ins)):
        for q, (off, n) in enumerate(_chunks(ins[b].shape[1], NQ_ICI)):
            rows = pl.ds(off, n)
            mine = outs[b].at[2 * x + y, rows, :]
            locals_.append(pltpu.make_async_copy(ins[b].at[2 * x + y, rows, :], mine, lsem.at[b, q]))
            for j, chip in enumerate(chips):
                sends.append(_remote(ins[b].at[2 * chip[0] + chip[1], rows, :], mine, ssem.at[b, j, q], rsem.at[b, j, q],
                                     (*chip, c)))
                slot = outs[b].at[2 * chip[0] + chip[1], rows, :]
                lands.append(_remote(slot, slot, ssem.at[b, j, q], rsem.at[b, j, q], me))
    return locals_, sends, lands


def _scatter_start(ins, outs, ssem, rsem, lsem):
    locals_, sends, _ = _scatter_copies(ins, outs, ssem, rsem, lsem)
    for cp in locals_ + sends:
        cp.start()


def _scatter_finish(ins, outs, ssem, rsem, lsem):
    locals_, sends, lands = _scatter_copies(ins, outs, ssem, rsem, lsem)
    for cp in lands:
        cp.wait_recv()
    for cp in sends:
        cp.wait_send()
    for cp in locals_:
        cp.wait()


def _scatter_sems(nb):
    return [pltpu.SemaphoreType.DMA((nb, 3, NQ_ICI)), pltpu.SemaphoreType.DMA((nb, 3, NQ_ICI)),
            pltpu.SemaphoreType.DMA((nb, NQ_ICI))]


def _remote(src, dst, ssem, rsem, dev):
    return pltpu.make_async_remote_copy(src_ref=src, dst_ref=dst, send_sem=ssem, recv_sem=rsem, device_id=dev,
                                        device_id_type=MESH)


def _all_gather_weights(shards):
    nb = len(shards)

    def body(*refs):
        ins, outs, sems = refs[:nb], refs[nb:2 * nb], refs[2 * nb:]
        _gather_start(ins, outs, *sems)
        _gather_finish(ins, outs, *sems)

    return _pc(
        body, name="all_gather_weights", in_specs=[ANY] * nb, out_specs=[ANY] * nb,
        out_shape=[SDS((NCH,) + s.shape, s.dtype) for s in shards], scratch_shapes=_gather_sems(nb),
    )(*shards)


def _gather_first(ins, outs, ssem, rsem, lsem):
    x, y, c, chips = _place()
    locals_, sends = [], []
    for b in range(len(ins)):
        half = ins[b].shape[0] // 2
        for q, (off, n) in enumerate(_chunks(half, NQ_ICI)):
            mine = pl.ds(c * half + off, n)
            own = outs[b].at[2 * x + y, mine, :]
            locals_.append(pltpu.make_async_copy(ins[b].at[mine, :], own, lsem.at[b, q]))
            sends.append(_remote(ins[b].at[mine, :], own, ssem.at[b, 0, q], rsem.at[b, 0, q], (x, y, 1 - c)))
            sends += [_remote(ins[b].at[mine, :], own, ssem.at[b, 1 + j, q], rsem.at[b, 1 + j, q], (*chip, c))
                      for j, chip in enumerate(chips)]
    return locals_, sends


def _gather_start(ins, outs, ssem, rsem, lsem):
    locals_, sends = _gather_first(ins, outs, ssem, rsem, lsem)
    for cp in locals_ + sends:
        cp.start()


def _gather_finish(ins, outs, ssem, rsem, lsem):
    x, y, c, chips = _place()
    me, sib = (x, y, c), (x, y, 1 - c)
    locals_, sends = _gather_first(ins, outs, ssem, rsem, lsem)
    for b in range(len(ins)):
        half = ins[b].shape[0] // 2
        for q, (off, n) in enumerate(_chunks(half, NQ_ICI)):
            mine = pl.ds(c * half + off, n)
            for j, chip in enumerate(chips):
                landed = outs[b].at[2 * chip[0] + chip[1], mine, :]
                _remote(landed, landed, ssem.at[b, 1 + j, q], rsem.at[b, 1 + j, q], me).wait_recv()
                cp = _remote(landed, landed, ssem.at[b, 4 + j, q], rsem.at[b, 4 + j, q], sib)
                cp.start()
                sends.append(cp)
    for b in range(len(ins)):
        half = ins[b].shape[0] // 2
        for q, (off, n) in enumerate(_chunks(half, NQ_ICI)):
            other = pl.ds((1 - c) * half + off, n)
            theirs = outs[b].at[2 * x + y, other, :]
            _remote(theirs, theirs, ssem.at[b, 0, q], rsem.at[b, 0, q], me).wait_recv()
            for j, chip in enumerate(chips):
                fwd = outs[b].at[2 * chip[0] + chip[1], other, :]
                _remote(fwd, fwd, ssem.at[b, 4 + j, q], rsem.at[b, 4 + j, q], me).wait_recv()
    for cp in sends:
        cp.wait_send()
    for cp in locals_:
        cp.wait()


def _gather_sems(nb):
    return [pltpu.SemaphoreType.DMA((nb, 7, NQ_ICI)), pltpu.SemaphoreType.DMA((nb, 7, NQ_ICI)),
            pltpu.SemaphoreType.DMA((nb, NQ_ICI))]


def _send_sibling_half(parts):
    nb = len(parts)

    def body(*refs):
        ins, gots = refs[:nb], refs[nb:2 * nb]
        ssem, rsem = refs[2 * nb:]
        x, y, c, _ = _place()
        sib = (x, y, 1 - c)
        todo = []
        for b in range(nb):
            half = ins[b].shape[1] // 2
            for q, (off, n) in enumerate(_chunks(half, NQ_D2D)):
                cp = _remote(ins[b].at[:, pl.ds((1 - c) * half + off, n), :], gots[b].at[:, pl.ds(off, n), :],
                             ssem.at[b, q], rsem.at[b, q], sib)
                cp.start()
                todo.append(cp)
        for cp in todo:
            cp.wait()

    return _pc(
        body, name="send_sibling_half", in_specs=[ANY] * nb, out_specs=[ANY] * nb,
        out_shape=[SDS((p.shape[0], p.shape[1] // 2, p.shape[2]), p.dtype) for p in parts],
        scratch_shapes=[pltpu.SemaphoreType.DMA((nb, NQ_D2D)), pltpu.SemaphoreType.DMA((nb, NQ_D2D))],
    )(*parts)


def _scatter_to_chips(parts):
    nb = len(parts)

    def body(*refs):
        ins, outs, sems = refs[:nb], refs[nb:2 * nb], refs[2 * nb:]
        _scatter_start(ins, outs, *sems)
        _scatter_finish(ins, outs, *sems)

    return _pc(
        body, name="scatter_to_chips", in_specs=[ANY] * nb, out_specs=[ANY] * nb,
        out_shape=[SDS(p.shape, p.dtype) for p in parts], scratch_shapes=_scatter_sems(nb),
    )(*parts)


def _join_halves(fulls, ranges):
    nb = len(fulls)
    nr = max(len(r) for r in ranges)

    def body(*refs):
        ins, outs = refs[:nb], refs[nb:2 * nb]
        ssem, rsem = refs[2 * nb:]
        x, y, c, _ = _place()
        sib = (x, y, 1 - c)
        sends, lands = [], []
        for b in range(nb):
            for g, (row0, rows) in enumerate(ranges[b]):
                half = rows // 2
                for q, (off, n) in enumerate(_chunks(half, NQ_D2D)):
                    mine = pl.ds(row0 + c * half + off, n)
                    sends.append(_remote(ins[b].at[mine, :], outs[b].at[mine, :], ssem.at[b, g, q], rsem.at[b, g, q], sib))
                    other = outs[b].at[pl.ds(row0 + (1 - c) * half + off, n), :]
                    lands.append(_remote(other, other, ssem.at[b, g, q], rsem.at[b, g, q], sib))
        for cp in sends:
            cp.start()
        for cp in lands:
            cp.wait_recv()
        for cp in sends:
            cp.wait_send()

    return _pc(
        body, name="join_halves", in_specs=[ANY] * nb, out_specs=[ANY] * nb,
        out_shape=[SDS(h.shape, h.dtype) for h in fulls], input_output_aliases={b: b for b in range(nb)},
        scratch_shapes=[pltpu.SemaphoreType.DMA((nb, nr, NQ_D2D)), pltpu.SemaphoreType.DMA((nb, nr, NQ_D2D))],
    )(*fulls)


def _gather_small(vec):
    def body(v_ref, o_ref, ssem, rsem, lsem):
        x, y, c, _ = _place()
        mine = o_ref.at[4 * x + 2 * y + c]
        local = pltpu.make_async_copy(v_ref, mine, lsem)
        local.start()
        sends = []
        for k in range(1, 8):
            peer = (x ^ (k >> 2), y ^ ((k >> 1) & 1), c ^ (k & 1))
            cp = _remote(v_ref, mine, ssem.at[k - 1], rsem.at[k - 1], peer)
            cp.start()
            sends.append(cp)
        for k in range(1, 8):
            px, py, pc = x ^ (k >> 2), y ^ ((k >> 1) & 1), c ^ (k & 1)
            slot = o_ref.at[4 * px + 2 * py + pc]
            _remote(slot, slot, ssem.at[k - 1], rsem.at[k - 1], (x, y, c)).wait_recv()
        for cp in sends:
            cp.wait_send()
        local.wait()

    return _pc(
        body, name="gather_small", in_specs=[ANY], out_specs=ANY, out_shape=SDS((8,) + vec.shape, vec.dtype),
        scratch_shapes=[pltpu.SemaphoreType.DMA((7,)), pltpu.SemaphoreType.DMA((7,)), pltpu.SemaphoreType.DMA],
    )(vec)


def _block_diag(pw):
    return jnp.concatenate([jnp.pad(pw[g], ((0, 0), (64 * g, 192 - 64 * g))) for g in range(4)], axis=0)


def _own_columns(full, chip):
    n = full.shape[-1] // NCH
    parts = full.reshape(full.shape[:-1] + (NCH, n))
    sel = (lax.broadcasted_iota(jnp.int32, (NCH, 1), 0) == chip)
    return jnp.sum(jnp.where(sel, parts, 0.0), axis=-2)


def _at_own_columns(shard, chip):
    n = shard.shape[-1]
    sel = (lax.broadcasted_iota(jnp.int32, (NCH * n,), 0) // n == chip)
    return jnp.where(sel, jnp.tile(shard, NCH), 0.0)


def _pad_rows(a, rows):
    return jnp.pad(a, ((0, rows - a.shape[0]),) + ((0, 0),) * (a.ndim - 1))


def _ffn_block(l, which):
    return 7 * l + 3 * which


def _wout_block(l):
    return 7 * l + 6


class _Weights:
    def __init__(self):
        self.ffn, self.wout, self.w_aug, self.rides = {}, {}, {}, {}

    @classmethod
    def from_blob(cls, blob, w_aug):
        self = cls()
        for l in range(DEPTH):
            self.ffn[(l, 0)], self.ffn[(l, 1)] = (blob, _ffn_block(l, 0)), (blob, _ffn_block(l, 1))
            self.wout[l], self.w_aug[l] = (blob, _wout_block(l)), w_aug[l]
        return self

    def set_w_in(self, l, gathered):
        self.w_aug[l] = jnp.pad(gathered.transpose(1, 0, 2).reshape(D, INW), ((0, 0), (0, INP - INW)))

    def ffn_fwd(self, l, which, x, nw):
        arr, k0 = self.ffn[(l, which)]
        if (l, which) not in self.rides:
            return _ffn_fwd(x, nw, arr, k0)
        shards, landed = self.rides[(l, which)]
        out, *gathered = _ffn_fwd(x, nw, arr, k0, shards)
        landed(gathered)
        return out


def _layer_fwd(l, x0, pos, freq, wts, ws):
    sv = {"x0": x0}
    x1 = ws.ffn_fwd(l, 0, x0, wts["ffn1_norm"][l:l + 1])
    att, att4, att16, pu, dq, dz, dba = _inproj_fwd(x1, wts["mix_norm"][l:l + 1], ws.w_aug[l], pos, freq)
    qkvs = [att, att4.reshape(S, 768), att16.reshape(S, 768)]
    (o1, l1), (o4, l4), (o16, l16) = [_attn_fwd(q, NB // d) for q, d in zip(qkvs, PATTERN_DIL)]
    ols = (o1, l1, o4.reshape(4, S // 4, ATT), l4.reshape(4, S // 4, ATT), o16.reshape(16, S // 16, ATT),
           l16.reshape(16, S // 16, ATT))
    ya = _merge_fwd(*ols)
    yb = _pool_fwd(pu, wts["pool_bd"][l], wts["pool_scale"][l:l + 1])
    c = _conv_fwd(dq, wts["conv_w"][l])
    u, w, qg, kd, intra, aux = _dn_local_fwd(c, dba, wts["dn_par"][l])
    o_dn, states = _dn_rec_fwd(u, w, qg, kd, intra, aux)
    yc = _dn_post_fwd(o_dn, dz, wts["dn_out_norm"][l:l + 1])
    x2 = _outproj_fwd(x1, ya, yb, yc, *ws.wout[l])
    x3 = ws.ffn_fwd(l, 1, x2, wts["ffn2_norm"][l:l + 1])
    sv.update(x1=x1, x2=x2, qkvs=qkvs, ols=ols, ya=ya, yb=yb, yc=yc, pu=pu, dq=dq, dz=dz, dba=dba, c=c,
              u=u, w=w, qg=qg, kd=kd, intra=intra, aux=aux, states=states, o_dn=o_dn)
    return x3, sv


def _wout_part(g):
    return jnp.pad(g.astype(bf16).reshape(NCH, 256, D), ((0, 0), (0, FC - 256), (0, 0)))


def _win_part(g):
    return g[:, :INW].astype(bf16).reshape(D, NCH, INC).transpose(1, 0, 2)


def _layer_bwd(l, dx3, sv, pos, freq, wts, ws, ride=None, prep=None):
    gr = {}
    g2, u2, d2, dh4, *pieces_before = _ffn_bwd(sv["x2"], wts["ffn2_norm"][l:l + 1], *ws.ffn[(l, 1)], dx3, ride)
    gr.update(ffn2_w_gate=g2, ffn2_w_up=u2, ffn2_w_down=d2)
    dx2, gr["ffn2_norm"], dya, dyb, dyc, gr["w_out"] = _outproj_bwd(sv["x2"], wts["ffn2_norm"][l:l + 1], dx3, dh4, sv["ya"],
                                                                     sv["yb"], sv["yc"], *ws.wout[l])
    do_dn, ddz, gr["dn_out_norm"] = _dn_post_bwd(sv["o_dn"], sv["dz"], wts["dn_out_norm"][l:l + 1], dyc)
    du, dw, dqg, dkd, dintra, daux = _dn_rec_bwd(sv["u"], sv["w"], sv["qg"], sv["kd"], sv["intra"], sv["aux"], sv["states"], do_dn)
    dc, ddba, gr["dn_par"] = _dn_local_bwd(sv["c"], sv["dba"], wts["dn_par"][l], du, dw, dqg, dkd, dintra, daux)
    ddq, gr["conv_w"] = _conv_bwd(sv["dq"], wts["conv_w"][l], dc)
    dpu, gr["pool_bd"], gr["pool_scale"] = _pool_bwd(sv["pu"], wts["pool_bd"][l], wts["pool_scale"][l:l + 1], dyb)
    dols = _merge_bwd(*sv["ols"], dya)
    flat = lambda a: a.reshape(S, ATT)
    datts = [_attn_bwd(q, flat(sv["ols"][2 * p]), flat(sv["ols"][2 * p + 1]), flat(dols[2 * p]), flat(dols[2 * p + 1]), NB // d)
             for p, (q, d) in enumerate(zip(sv["qkvs"], PATTERN_DIL))]
    dx1, gr["mix_norm"], gr["w_aug"] = _inproj_bwd(sv["x1"], wts["mix_norm"][l:l + 1], ws.w_aug[l], pos, freq, dx2,
                                                    datts[0], datts[1].reshape(4, S // 4, 768),
                                                    datts[2].reshape(16, S // 16, 768), dpu, ddq, ddz, ddba)
    own = None
    if prep is not None:
        own = prep([jnp.concatenate([g2, u2, d2, _wout_part(gr["w_out"])], axis=1), _win_part(gr["w_aug"])])
    g1, u1, d1, dh4, *pieces_own = _ffn_bwd(sv["x0"], wts["ffn1_norm"][l:l + 1], *ws.ffn[(l, 0)], dx1, own)
    dx0, gr["ffn1_norm"] = _norm_bwd(sv["x0"], wts["ffn1_norm"][l:l + 1], dx1, dh4)
    gr.update(ffn1_w_gate=g1, ffn1_w_up=u1, ffn1_w_down=d1)
    return dx0, gr, pieces_before, pieces_own


def _device_step(x, pos, target, wts, ws, prep=None):
    freq = jnp.tile(ROPE_THETA ** (-jnp.arange(0, EH, 2, dtype=f32) / EH), 2 * ATT // EH).reshape(1, ATT)
    saved = []
    h = x
    for l in range(DEPTH):
        h, sv = _layer_fwd(l, h, pos, freq, wts, ws)
        saved.append(sv)
    dh, g_final, loss = _final(h, wts["final_norm"], target)
    grads = [None] * DEPTH
    dh, grads[1], _, _ = _layer_bwd(1, dh, saved[1], pos, freq, wts, ws)
    sums1 = None
    if prep is not None:
        g = grads[1]
        ffn = [g[f"ffn{f}_w_{n}"] for f in (1, 2) for n in ("gate", "up", "down")]
        sums1 = prep([jnp.concatenate(ffn + [_wout_part(g["w_out"])], axis=1), _win_part(g["w_aug"])])
    dh, grads[0], pieces1, pieces0 = _layer_bwd(0, dh, saved[0], pos, freq, wts, ws, sums1, prep)
    return loss, dh, g_final, grads, pieces1, pieces0


_SMALL = (("ffn1_norm", (DEPTH, D)), ("mix_norm", (DEPTH, D)), ("pool_w", (DEPTH, 4, 64, 64)), ("pool_scale", (DEPTH, 256)),
          ("dn_conv_w", (DEPTH, 4, CW)), ("dn_a_log", (DEPTH, 4)), ("dn_dt_bias", (DEPTH, 4)), ("dn_out_norm", (DEPTH, 128)),
          ("ffn2_norm", (DEPTH, D)), ("final_norm", (D,)), ("loss", (1,)))


def _pack_small(vals):
    rows = []
    for name, shape in _SMALL:
        flat = vals[name].astype(f32).reshape(-1)
        rows.append(jnp.pad(flat, (0, _small_rows(shape) * 128 - flat.shape[0])).reshape(-1, 128))
    out = jnp.concatenate(rows, axis=0)
    return _pad_rows(out, -(-out.shape[0] // 16) * 16)


def _small_rows(shape):
    return -(-int(np.prod(shape)) // 1024) * 8


def _unpack_small(packed):
    vals, r = {}, 0
    for name, shape in _SMALL:
        size, n = int(np.prod(shape)), _small_rows(shape)
        vals[name] = packed[r:r + n].reshape(-1)[:size].reshape(shape)
        r += n
    return vals


def kernel(x, positions, ffn1_norm, ffn1_w_gate, ffn1_w_up, ffn1_w_down, mix_norm, w_in, pool_w, pool_scale, dn_conv_w, dn_a_log, dn_dt_bias, dn_out_norm, w_out, ffn2_norm, ffn2_w_gate, ffn2_w_up, ffn2_w_down, final_norm, loss_target, m_ffn1_norm, m_ffn1_w_gate, m_ffn1_w_up, m_ffn1_w_down, m_mix_norm, m_w_in, m_pool_w, m_pool_scale, m_dn_conv_w, m_dn_a_log, m_dn_dt_bias, m_dn_out_norm, m_w_out, m_ffn2_norm, m_ffn2_w_gate, m_ffn2_w_up, m_ffn2_w_down, m_final_norm, v_ffn1_norm, v_ffn1_w_gate, v_ffn1_w_up, v_ffn1_w_down, v_mix_norm, v_w_in, v_pool_w, v_pool_scale, v_dn_conv_w, v_dn_a_log, v_dn_dt_bias, v_dn_out_norm, v_w_out, v_ffn2_norm, v_ffn2_w_gate, v_ffn2_w_up, v_ffn2_w_down, v_final_norm):
    names = ["ffn1_norm", "ffn1_w_gate", "ffn1_w_up", "ffn1_w_down", "mix_norm", "w_in", "pool_w", "pool_scale", "dn_conv_w",
             "dn_a_log", "dn_dt_bias", "dn_out_norm", "w_out", "ffn2_norm", "ffn2_w_gate", "ffn2_w_up", "ffn2_w_down", "final_norm"]
    W = dict(zip(names, [ffn1_norm, ffn1_w_gate, ffn1_w_up, ffn1_w_down, mix_norm, w_in, pool_w, pool_scale, dn_conv_w,
                         dn_a_log, dn_dt_bias, dn_out_norm, w_out, ffn2_norm, ffn2_w_gate, ffn2_w_up, ffn2_w_down, final_norm]))
    M = dict(zip(names, [m_ffn1_norm, m_ffn1_w_gate, m_ffn1_w_up, m_ffn1_w_down, m_mix_norm, m_w_in, m_pool_w, m_pool_scale,
                         m_dn_conv_w, m_dn_a_log, m_dn_dt_bias, m_dn_out_norm, m_w_out, m_ffn2_norm, m_ffn2_w_gate, m_ffn2_w_up,
                         m_ffn2_w_down, m_final_norm]))
    V = dict(zip(names, [v_ffn1_norm, v_ffn1_w_gate, v_ffn1_w_up, v_ffn1_w_down, v_mix_norm, v_w_in, v_pool_w, v_pool_scale,
                         v_dn_conv_w, v_dn_a_log, v_dn_dt_bias, v_dn_out_norm, v_w_out, v_ffn2_norm, v_ffn2_w_gate, v_ffn2_w_up,
                         v_ffn2_w_down, v_final_norm]))
    chip = 2 * lax.axis_index("x") + lax.axis_index("y")

    ffn_names = [(f"ffn{f}_w_gate", f"ffn{f}_w_up", f"ffn{f}_w_down") for f in (1, 2)]
    tr = lambda t: jnp.swapaxes(t, -1, -2)
    def ffn_rows(l, which):
        g, u, dn = ffn_names[which]
        return [tr(W[g][l]), tr(W[u][l]), W[dn][l]]

    def second_half(l):
        return jnp.concatenate(ffn_rows(l, 1) + [jnp.pad(W["w_out"][l], ((0, FC - 256), (0, 0)))], axis=0).astype(bf16)

    ws = _Weights()
    first0, = _all_gather_weights([jnp.concatenate(ffn_rows(0, 0), axis=0).astype(bf16)])
    ws.ffn[(0, 0)] = (first0, 0)

    def landed_00(gathered):
        ws.ffn[(0, 1)], ws.wout[0] = (gathered[0], 0), (gathered[0], 3)
        ws.set_w_in(0, gathered[1])

    def landed_01(gathered):
        ws.ffn[(1, 0)] = (gathered[0], 0)
        ws.set_w_in(1, gathered[1])

    def landed_10(gathered):
        ws.ffn[(1, 1)], ws.wout[1] = (gathered[0], 0), (gathered[0], 3)

    ws.rides[(0, 0)] = ([second_half(0), W["w_in"][0].astype(bf16)], landed_00)
    ws.rides[(0, 1)] = ([jnp.concatenate(ffn_rows(1, 0), axis=0).astype(bf16), W["w_in"][1].astype(bf16)], landed_01)
    ws.rides[(1, 0)] = ([second_half(1)], landed_10)
    conv_all = _gather_small(_pad_rows(dn_conv_w.reshape(DEPTH * 4 * (CW // NCH) // 128, 128), 32))
    conv_full = jnp.concatenate([conv_all[2 * j, :DEPTH * 4 * (CW // NCH) // 128].reshape(DEPTH, 4, CW // NCH) for j in range(NCH)],
                                axis=-1)

    par = jnp.pad(jnp.stack([dn_a_log, dn_dt_bias], axis=1), ((0, 0), (0, 6), (4, 120)))
    wts = dict(ffn1_norm=ffn1_norm, mix_norm=mix_norm, ffn2_norm=ffn2_norm, final_norm=final_norm.reshape(1, D),
               pool_bd=jnp.stack([_block_diag(pool_w[l]) for l in range(DEPTH)]).astype(bf16),
               pool_scale=pool_scale, conv_w=jnp.pad(conv_full, ((0, 0), (0, 4), (0, 0))),
               dn_par=par, dn_out_norm=dn_out_norm)

    c_arr = lax.axis_index("c").astype(jnp.int32).reshape(1)

    def prep(parts):
        return [_sum_core_pair(p, g, c_arr) for p, g in zip(parts, _send_sibling_half(parts))]

    loss, dx, g_final, grads, pieces1, pieces0 = _device_step(x[0], positions.reshape(S, 1), loss_target[0], wts, ws, prep)
    last = [jnp.concatenate([grads[0][n] for n in ffn_names[0]], axis=1)]
    pieces_last = _scatter_to_chips(prep(last))
    full_b = _sum_chips(pieces_last[0], c_arr, None, 0, RB)
    full_b = _sum_chips(pieces0[0], c_arr, full_b, 3 * FC, RB)
    full_b = _sum_chips(pieces1[0], c_arr, full_b, 7 * FC, RB)
    full_c = _sum_chips(pieces0[1], c_arr, None, 0, RC)
    full_c = _sum_chips(pieces1[1], c_arr, full_c, D, RC)
    full_b, full_c = _join_halves([full_b, full_c], [[(0, 3 * FC), (3 * FC, 4 * FC), (7 * FC, 7 * FC)], [(0, D), (D, D)]])

    small = {"loss": loss[0, 0:1], "final_norm": g_final.reshape(D)}
    for n in ("ffn1_norm", "mix_norm", "ffn2_norm", "pool_scale", "dn_out_norm"):
        small[n] = jnp.stack([grads[l][n].reshape(-1) for l in range(DEPTH)])
    small["pool_w"] = jnp.stack([jnp.stack([grads[l]["pool_bd"][64 * g:64 * (g + 1), 64 * g:64 * (g + 1)] for g in range(4)])
                                 for l in range(DEPTH)])
    small["dn_conv_w"] = jnp.stack([grads[l]["conv_w"][0:4] for l in range(DEPTH)])
    small["dn_a_log"] = jnp.stack([grads[l]["dn_par"][0, 4:8] for l in range(DEPTH)])
    small["dn_dt_bias"] = jnp.stack([grads[l]["dn_par"][1, 4:8] for l in range(DEPTH)])
    packed = _pack_small(small)
    g_small = _sum_pieces(_gather_small(packed), f32, "sum_small")
    gs = _unpack_small(g_small)

    transposed = ("ffn1_w_gate", "ffn1_w_up", "ffn2_w_gate", "ffn2_w_up")
    where = {"ffn1_w_gate": (full_b, FC // 2, lambda l: 14 * l), "ffn1_w_up": (full_b, FC // 2, lambda l: 14 * l + 2),
             "ffn1_w_down": (full_b, FC // 2, lambda l: 14 * l + 4), "ffn2_w_gate": (full_b, FC // 2, lambda l: 14 * l + 6),
             "ffn2_w_up": (full_b, FC // 2, lambda l: 14 * l + 8), "ffn2_w_down": (full_b, FC // 2, lambda l: 14 * l + 10),
             "w_out": (full_b, 64, lambda l: (FC // 64) * (7 * l + 6)), "w_in": (full_c, D // 2, lambda l: 2 * l)}
    big_res = {}
    for n, (gblob, tile, first) in where.items():
        t = tr if n in transposed else (lambda a: a)
        big_res[n] = [t(r) for r in _adamw_rows(t(W[n]), t(M[n]), t(V[n]), gblob, tile, first, "adamw_" + n)]

    def small_of(T):
        d = {n: T[n] for n, _ in _SMALL if n not in ("loss", "dn_conv_w")}
        d["loss"] = jnp.zeros((1,), f32)
        d["dn_conv_w"] = _at_own_columns(T["dn_conv_w"], chip)
        return _pack_small(d)

    res_s = _adamw(small_of(W), g_small, small_of(M), small_of(V), "adamw_small")
    small_out = [_unpack_small(r) for r in res_s]

    def split_blobs(b, c):
        out = {}
        b7 = b.reshape(DEPTH, 7, FC, D)
        for k, n in enumerate(n for names3 in ffn_names for n in names3):
            out[n] = tr(b7[:, k]) if n in transposed else b7[:, k]
        out["w_out"] = b7[:, 6, :256]
        out["w_in"] = c.reshape(DEPTH, D, INC)
        return out

    def assemble(big, sm):
        out = []
        for n in names:
            if n in big:
                out.append(big[n])
            elif n == "dn_conv_w":
                out.append(_own_columns(sm[n], chip))
            else:
                out.append(sm[n])
        return out

    grad_list = assemble(split_blobs(full_b, full_c), gs)
    outs = [gs["loss"].reshape(()), dx.reshape(1, S, D)] + grad_list
    for k in range(3):
        outs += assemble({n: r[k] for n, r in big_res.items()}, small_out[k])
    return tuple(outs)
```

```python
import functools
import math

import jax
import jax.numpy as jnp
import numpy as np
from jax import lax
from jax.experimental import pallas as pl
from jax.experimental.pallas import tpu as pltpu

f32 = jnp.float32
bf16 = jnp.bfloat16
SDS = jax.ShapeDtypeStruct
MESH = pl.DeviceIdType.MESH

S = 4096
D = 1024
DEPTH = 2
FF = 2816
NCH = 4
FC = FF // NCH
INW = 3080
INC = INW // NCH
INP = 3200
ATT = 256
EH = 64
NBLK = 128
DNW = 512
DH = 128
CH = 64
NCHUNK = S // CH
EPS = 1e-6
ROPE_THETA = 10000.0
PATTERN_DIL = (1, 4, 16)
ADAM_LR, ADAM_B1, ADAM_B2, ADAM_EPS, ADAM_WD, ADAM_STEP = 0.001, 0.9, 0.999, 1e-08, 0.01, 10
VMEM_BYTES_V7X = 64 * 1024 * 1024
NEG = -1e30

TM = 512
RB, RC = 14 * FC, 2 * D


def _cp(vmem_mb=48, sem=None):
    kw = dict(vmem_limit_bytes=vmem_mb * 1024 * 1024)
    if sem is not None:
        kw["dimension_semantics"] = sem
    return pltpu.CompilerParams(**kw)


def _pc(*args, **kwargs):
    pin = lambda s: pltpu.HBM(s.shape, s.dtype) if isinstance(s, SDS) and jnp.issubdtype(s.dtype, jnp.floating) else s
    out = kwargs["out_shape"]
    kwargs["out_shape"] = [pin(s) for s in out] if isinstance(out, (list, tuple)) else pin(out)
    call = pl.pallas_call(*args, **kwargs)

    def run(*operands):
        pinned = [pltpu.with_memory_space_constraint(o, pltpu.HBM) if jnp.issubdtype(o.dtype, jnp.floating) else o
                  for o in operands]
        return call(*pinned)

    return run


def _dot(a, b):
    return jnp.dot(a, b, preferred_element_type=f32)


def _dot_nt(a, b):
    return lax.dot_general(a, b, (((1,), (1,)), ((), ())), preferred_element_type=f32)


def _dot_tn(a, b):
    return lax.dot_general(a, b, (((0,), (0,)), ((), ())), preferred_element_type=f32)


def _rms(x, w):
    r = lax.rsqrt(jnp.mean(x * x, axis=-1, keepdims=True) + EPS)
    return x * r * w, r


def _rms_bwd(x, w, r, dh):
    xhat = x * r
    dw = jnp.sum(dh * xhat, axis=0, keepdims=True)
    dxh = dh * w
    dx = r * (dxh - xhat * jnp.mean(dxh * xhat, axis=-1, keepdims=True))
    return dx, dw


def _ffn_fwd(x, nw, blob, k0, ride=None):
    kg, ku, kd = k0, k0 + 1, k0 + 2
    nr = 0 if ride is None else len(ride)
    ni = S // TM

    def body(*refs):
        x_ref, nw_ref, wg_ref, wu_ref, wd_ref = refs[:5]
        ride_in = refs[5:5 + nr]
        o_ref = refs[5 + nr]
        ride_out = refs[6 + nr:6 + 2 * nr]
        h_scr, acc_scr = refs[6 + 2 * nr:8 + 2 * nr]
        sems = refs[8 + 2 * nr:]
        i = pl.program_id(0)
        j = pl.program_id(1)

        if nr:
            @pl.when(jnp.logical_and(i == 0, j == 0))
            def _():
                _gather_start(ride_in, ride_out, *sems)

        @pl.when(j == 0)
        def _():
            h, _ = _rms(x_ref[...], nw_ref[...])
            h_scr[...] = h.astype(bf16)
            acc_scr[...] = jnp.zeros_like(acc_scr)

        h = h_scr[...]
        g = _dot_nt(h, wg_ref[0])
        u = _dot_nt(h, wu_ref[0])
        a = (g * jax.nn.sigmoid(g) * u).astype(bf16)
        acc_scr[...] += _dot(a, wd_ref[0])

        @pl.when(j == NCH - 1)
        def _():
            o_ref[...] = x_ref[...] + 0.5 * acc_scr[...]

        if nr:
            @pl.when(jnp.logical_and(i == ni - 1, j == NCH - 1))
            def _():
                _gather_finish(ride_in, ride_out, *sems)

    wspec = lambda k: pl.BlockSpec((1, FC, D), lambda i, j: (j, k, 0))
    rides = [] if ride is None else list(ride)
    res = _pc(
        body, grid=(ni, NCH), name="ffn_fwd_ride" if nr else "ffn_fwd",
        in_specs=[pl.BlockSpec((TM, D), lambda i, j: (i, 0)),
                  pl.BlockSpec((1, D), lambda i, j: (0, 0)),
                  wspec(kg), wspec(ku), wspec(kd)] + [ANY] * nr,
        out_specs=[pl.BlockSpec((TM, D), lambda i, j: (i, 0))] + [ANY] * nr,
        out_shape=[SDS((S, D), f32)] + [SDS((NCH,) + r.shape, r.dtype) for r in rides],
        scratch_shapes=[pltpu.VMEM((TM, D), bf16), pltpu.VMEM((TM, D), f32)] + (_gather_sems(nr) if nr else []),
        compiler_params=_cp(40),
    )(x, nw, blob, blob, blob, *rides)
    return res if nr else res[0]


def _ffn_bwd(x, nw, blob, k0, dy, ride=None):
    nt = S // TM
    kg, ku, kd = k0, k0 + 1, k0 + 2
    nr = 0 if ride is None else len(ride)

    def body(*refs):
        x_ref, nw_ref, wg_ref, wu_ref, wd_ref, dy_ref = refs[:6]
        ride_in = refs[6:6 + nr]
        dwg_ref, dwu_ref, dwd_ref, dh_ref = refs[6 + nr:10 + nr]
        ride_out = refs[10 + nr:10 + 2 * nr]
        ag, au, ad = refs[10 + 2 * nr:13 + 2 * nr]
        sems = refs[13 + 2 * nr:]
        j = pl.program_id(0)
        i = pl.program_id(1)

        if nr:
            @pl.when(jnp.logical_and(j == 0, i == 0))
            def _():
                _scatter_start(ride_in, ride_out, *sems)

        @pl.when(i == 0)
        def _():
            ag[...] = jnp.zeros_like(ag)
            au[...] = jnp.zeros_like(au)
            ad[...] = jnp.zeros_like(ad)

        hf, _ = _rms(x_ref[...], nw_ref[...])
        h = hf.astype(bf16)
        g = _dot_nt(h, wg_ref[0])
        u = _dot_nt(h, wu_ref[0])
        sg = jax.nn.sigmoid(g)
        s = g * sg
        a = (s * u).astype(bf16)
        dyb = (0.5 * dy_ref[...]).astype(bf16)
        da = _dot_nt(dyb, wd_ref[0])
        ad[...] += _dot_tn(a, dyb)
        du = (da * s).astype(bf16)
        dg = (da * u * (sg * (1.0 + g * (1.0 - sg)))).astype(bf16)
        ag[...] += _dot_tn(dg, h)
        au[...] += _dot_tn(du, h)
        dh_ref[0] = (_dot(dg, wg_ref[0]) + _dot(du, wu_ref[0])).astype(bf16)

        @pl.when(i == nt - 1)
        def _():
            dwg_ref[0] = ag[...].astype(bf16)
            dwu_ref[0] = au[...].astype(bf16)
            dwd_ref[0] = ad[...].astype(bf16)

        if nr:
            @pl.when(jnp.logical_and(j == NCH - 1, i == nt - 1))
            def _():
                _scatter_finish(ride_in, ride_out, *sems)

    wspec = lambda k: pl.BlockSpec((1, FC, D), lambda j, i: (j, k, 0))
    gspec = pl.BlockSpec((1, FC, D), lambda j, i: (j, 0, 0))
    rides = [] if ride is None else list(ride)
    return _pc(
        body, grid=(NCH, nt), name="ffn_bwd_ride" if nr else "ffn_bwd",
        in_specs=[pl.BlockSpec((TM, D), lambda j, i: (i, 0)),
                  pl.BlockSpec((1, D), lambda j, i: (0, 0)),
                  wspec(kg), wspec(ku), wspec(kd),
                  pl.BlockSpec((TM, D), lambda j, i: (i, 0))] + [ANY] * nr,
        out_specs=[gspec, gspec, gspec, pl.BlockSpec((1, TM, D), lambda j, i: (j, i, 0))] + [ANY] * nr,
        out_shape=[SDS((NCH, FC, D), bf16)] * 3 + [SDS((NCH, S, D), bf16)] + [SDS(r.shape, r.dtype) for r in rides],
        scratch_shapes=[pltpu.VMEM((FC, D), f32)] * 3 + (_scatter_sems(nr) if nr else []),
        compiler_params=_cp(56),
    )(x, nw, blob, blob, blob, dy, *rides)


def _norm_bwd(x, nw, dres, dh4):
    nt = S // TM
    nparts = dh4.shape[0]

    def body(x_ref, nw_ref, dres_ref, dh_ref, dx_ref, dnw_ref):
        i = pl.program_id(0)
        dh = dh_ref[0].astype(f32)
        for p in range(1, nparts):
            dh = dh + dh_ref[p].astype(f32)
        xv = x_ref[...]
        _, r = _rms(xv, nw_ref[...])
        dx, dw = _rms_bwd(xv, nw_ref[...], r, dh)
        dx_ref[...] = dres_ref[...] + dx

        @pl.when(i == 0)
        def _():
            dnw_ref[...] = jnp.zeros_like(dnw_ref)

        dnw_ref[...] += dw

    return _pc(
        body, grid=(nt,), name="norm_bwd",
        in_specs=[pl.BlockSpec((TM, D), lambda i: (i, 0)),
                  pl.BlockSpec((1, D), lambda i: (0, 0)),
                  pl.BlockSpec((TM, D), lambda i: (i, 0)),
                  pl.BlockSpec((nparts, TM, D), lambda i: (0, i, 0))],
        out_specs=[pl.BlockSpec((TM, D), lambda i: (i, 0)), pl.BlockSpec((1, D), lambda i: (0, 0))],
        out_shape=[SDS((S, D), f32), SDS((1, D), f32)],
        compiler_params=_cp(40),
    )(x, nw, dres, dh4)


def _final(x, nw, target):
    nt = S // TM

    def body(x_ref, nw_ref, t_ref, dx_ref, dnw_ref, loss_ref):
        i = pl.program_id(0)
        xv = x_ref[...]
        y, r = _rms(xv, nw_ref[...])
        err = y - t_ref[...]
        part = 0.5 * jnp.sum(jnp.mean(err * err, axis=-1, keepdims=True), axis=0, keepdims=True)
        dx, dw = _rms_bwd(xv, nw_ref[...], r, err * (1.0 / D))
        dx_ref[...] = dx

        @pl.when(i == 0)
        def _():
            dnw_ref[...] = jnp.zeros_like(dnw_ref)
            loss_ref[...] = jnp.zeros_like(loss_ref)

        dnw_ref[...] += dw
        loss_ref[...] += jnp.broadcast_to(part, loss_ref.shape)

    return _pc(
        body, grid=(nt,), name="final_loss",
        in_specs=[pl.BlockSpec((TM, D), lambda i: (i, 0)),
                  pl.BlockSpec((1, D), lambda i: (0, 0)),
                  pl.BlockSpec((TM, D), lambda i: (i, 0))],
        out_specs=[pl.BlockSpec((TM, D), lambda i: (i, 0)), pl.BlockSpec((1, D), lambda i: (0, 0)),
                   pl.BlockSpec((1, 128), lambda i: (0, 0))],
        out_shape=[SDS((S, D), f32), SDS((1, D), f32), SDS((1, 128), f32)],
        compiler_params=_cp(40),
    )(x, nw, target)


def _rot_half(t):
    lane = lax.broadcasted_iota(jnp.int32, t.shape, 1)
    first = (lane % EH) < (EH // 2)
    return jnp.where(first, -pltpu.roll(t, ATT - EH // 2, 1), pltpu.roll(t, EH // 2, 1))


def _rope_tables(pos_ref, freq_ref):
    ang = pos_ref[...].astype(f32) * freq_ref[...]
    return jnp.cos(ang), jnp.sin(ang)


def _split_residues(val, scr, outs):
    rows, cols = val.shape
    for j in range(cols // 128):
        scr[j] = val[:, 128 * j:128 * (j + 1)]
    for ref, d in outs:
        for j in range(cols // 128):
            for r in range(d):
                ref.at[r][:, 128 * j:128 * (j + 1)] = scr.at[j][pl.ds(r, rows // d, stride=d), :]


def _join_residues(ref, d, scr):
    rows, cols = scr.shape[1], ref.shape[2]
    for j in range(cols // 128):
        for r in range(d):
            scr.at[j][pl.ds(r, rows // d, stride=d), :] = ref.at[r][:, 128 * j:128 * (j + 1)]
    return jnp.concatenate([scr[j] for j in range(cols // 128)], axis=1)


def _res_spec(d, tile, cols):
    return pl.BlockSpec((d, tile // d, cols), lambda i: (0, i, 0))


def _inproj_fwd(x, nw, w_aug, pos, freq):
    TI = 256

    def body(x_ref, nw_ref, w_hbm, pos_ref, freq_ref, att_ref, att4_ref, att16_ref, pu_ref, dq_ref, dz_ref, dba_ref,
             w_scr, r_scr):
        @pl.when(pl.program_id(0) == 0)
        def _():
            pltpu.sync_copy(w_hbm, w_scr)

        h, _ = _rms(x_ref[...], nw_ref[...])
        proj = _dot(h.astype(bf16), w_scr[...])
        cos, sin = _rope_tables(pos_ref, freq_ref)
        q = proj[:, 0:ATT]
        k = proj[:, ATT:2 * ATT]
        att = jnp.concatenate([q * cos + _rot_half(q) * sin, k * cos + _rot_half(k) * sin, proj[:, 2 * ATT:3 * ATT]], axis=1)
        att_ref[...] = att
        _split_residues(att, r_scr, [(att4_ref, 4), (att16_ref, 16)])
        pu_ref[...] = proj[:, 768:1024]
        dq_ref[...] = proj[:, 1024:2560]
        dz_ref[...] = proj[:, 2560:3072]
        dba_ref[...] = proj[:, 3072:3200]

    return _pc(
        body, grid=(S // TI,), name="inproj_fwd",
        in_specs=[pl.BlockSpec((TI, D), lambda i: (i, 0)),
                  pl.BlockSpec((1, D), lambda i: (0, 0)),
                  pl.BlockSpec(memory_space=pl.ANY),
                  pl.BlockSpec((TI, 1), lambda i: (i, 0)),
                  pl.BlockSpec((1, ATT), lambda i: (0, 0))],
        out_specs=[pl.BlockSpec((TI, 768), lambda i: (i, 0)), _res_spec(4, TI, 768), _res_spec(16, TI, 768),
                   pl.BlockSpec((TI, 256), lambda i: (i, 0)),
                   pl.BlockSpec((TI, 1536), lambda i: (i, 0)), pl.BlockSpec((TI, 512), lambda i: (i, 0)),
                   pl.BlockSpec((TI, 128), lambda i: (i, 0))],
        out_shape=[SDS((S, 768), f32), SDS((4, S // 4, 768), f32), SDS((16, S // 16, 768), f32), SDS((S, 256), f32),
                   SDS((S, 1536), f32), SDS((S, 512), f32), SDS((S, 128), f32)],
        scratch_shapes=[pltpu.VMEM((D, INP), bf16), pltpu.VMEM((6, TI, 128), f32)],
        compiler_params=_cp(48),
    )(x, nw, w_aug, pos, freq)


def _inproj_bwd(x, nw, w_aug, pos, freq, dres, datt, datt4, datt16, dpu, ddq, ddz, ddba):
    TI = 256
    nt = S // TI

    def body(x_ref, nw_ref, w_hbm, pos_ref, freq_ref, dres_ref, datt_ref, datt4_ref, datt16_ref, dpu_ref, ddq_ref, ddz_ref,
             ddba_ref, dx_ref, dnw_ref, dw_hbm, w_scr, acc, r_scr):
        i = pl.program_id(0)

        @pl.when(i == 0)
        def _():
            pltpu.sync_copy(w_hbm, w_scr)
            acc[...] = jnp.zeros_like(acc)
            dnw_ref[...] = jnp.zeros_like(dnw_ref)

        xv = x_ref[...]
        hf, r = _rms(xv, nw_ref[...])
        h = hf.astype(bf16)
        cos, sin = _rope_tables(pos_ref, freq_ref)
        datt = datt_ref[...] + _join_residues(datt4_ref, 4, r_scr)
        datt = datt + _join_residues(datt16_ref, 16, r_scr)
        dq = datt[:, 0:ATT]
        dk = datt[:, ATT:2 * ATT]
        dq = dq * cos - _rot_half(dq) * sin
        dk = dk * cos - _rot_half(dk) * sin
        dproj = jnp.concatenate([dq, dk, datt[:, 2 * ATT:3 * ATT], dpu_ref[...], ddq_ref[...], ddz_ref[...], ddba_ref[...]],
                                axis=1).astype(bf16)
        acc[...] += _dot_tn(h, dproj)
        dh = _dot_nt(dproj, w_scr[...])
        dx, dw = _rms_bwd(xv, nw_ref[...], r, dh)
        dx_ref[...] = dres_ref[...] + dx
        dnw_ref[...] += dw

        @pl.when(i == nt - 1)
        def _():
            pltpu.sync_copy(acc, dw_hbm)

    return _pc(
        body, grid=(nt,), name="inproj_bwd",
        in_specs=[pl.BlockSpec((TI, D), lambda i: (i, 0)),
                  pl.BlockSpec((1, D), lambda i: (0, 0)),
                  pl.BlockSpec(memory_space=pl.ANY),
                  pl.BlockSpec((TI, 1), lambda i: (i, 0)),
                  pl.BlockSpec((1, ATT), lambda i: (0, 0)),
                  pl.BlockSpec((TI, D), lambda i: (i, 0)),
                  pl.BlockSpec((TI, 768), lambda i: (i, 0)), _res_spec(4, TI, 768), _res_spec(16, TI, 768),
                  pl.BlockSpec((TI, 256), lambda i: (i, 0)),
                  pl.BlockSpec((TI, 1536), lambda i: (i, 0)),
                  pl.BlockSpec((TI, 512), lambda i: (i, 0)),
                  pl.BlockSpec((TI, 128), lambda i: (i, 0))],
        out_specs=[pl.BlockSpec((TI, D), lambda i: (i, 0)), pl.BlockSpec((1, D), lambda i: (0, 0)),
                   pl.BlockSpec(memory_space=pl.ANY)],
        out_shape=[SDS((S, D), f32), SDS((1, D), f32), SDS((D, INP), f32)],
        scratch_shapes=[pltpu.VMEM((D, INP), bf16), pltpu.VMEM((D, INP), f32), pltpu.VMEM((6, TI, 128), f32)],
        compiler_params=_cp(56),
    )(x, nw, w_aug, pos, freq, dres, datt, datt4, datt16, dpu, ddq, ddz, ddba)


def _outproj_fwd(x, ya, yb, yc, blob_b, kw):
    def body(x_ref, ya_ref, yb_ref, yc_ref, w_ref, o_ref):
        ycat = jnp.concatenate([ya_ref[...], yb_ref[...], yc_ref[...]], axis=1).astype(bf16)
        o_ref[...] = x_ref[...] + _dot(ycat, w_ref[:, 0:256, :].reshape(D, D))

    return _pc(
        body, grid=(S // TM,), name="outproj_fwd",
        in_specs=[pl.BlockSpec((TM, D), lambda i: (i, 0)),
                  pl.BlockSpec((TM, 256), lambda i: (i, 0)),
                  pl.BlockSpec((TM, 256), lambda i: (i, 0)),
                  pl.BlockSpec((TM, 512), lambda i: (i, 0)),
                  pl.BlockSpec((NCH, FC, D), lambda i: (0, kw, 0))],
        out_specs=pl.BlockSpec((TM, D), lambda i: (i, 0)),
        out_shape=SDS((S, D), f32),
        compiler_params=_cp(40),
    )(x, ya, yb, yc, blob_b)


def _outproj_bwd(x, nw, dres, dh4, ya, yb, yc, blob_b, kw):
    nt = S // TM
    nparts = dh4.shape[0]

    def body(x_ref, nw_ref, dres_ref, dh_ref, ya_ref, yb_ref, yc_ref, w_ref, dx_ref, dnw_ref, dya_ref, dyb_ref, dyc_ref, dw_ref):
        i = pl.program_id(0)

        @pl.when(i == 0)
        def _():
            dw_ref[...] = jnp.zeros_like(dw_ref)
            dnw_ref[...] = jnp.zeros_like(dnw_ref)

        dh = dh_ref[0].astype(f32)
        for p in range(1, nparts):
            dh = dh + dh_ref[p].astype(f32)
        xv = x_ref[...]
        _, r = _rms(xv, nw_ref[...])
        dxn, dnw = _rms_bwd(xv, nw_ref[...], r, dh)
        dx = dres_ref[...] + dxn
        dx_ref[...] = dx
        dnw_ref[...] += dnw
        dyv = dx.astype(bf16)
        ycat = jnp.concatenate([ya_ref[...], yb_ref[...], yc_ref[...]], axis=1).astype(bf16)
        dw_ref[...] += _dot_tn(ycat, dyv)
        dcat = _dot_nt(dyv, w_ref[:, 0:256, :].reshape(D, D))
        dya_ref[...] = dcat[:, 0:256]
        dyb_ref[...] = dcat[:, 256:512]
        dyc_ref[...] = dcat[:, 512:1024]

    return _pc(
        body, grid=(nt,), name="outproj_bwd",
        in_specs=[pl.BlockSpec((TM, D), lambda i: (i, 0)),
                  pl.BlockSpec((1, D), lambda i: (0, 0)),
                  pl.BlockSpec((TM, D), lambda i: (i, 0)),
                  pl.BlockSpec((nparts, TM, D), lambda i: (0, i, 0)),
                  pl.BlockSpec((TM, 256), lambda i: (i, 0)),
                  pl.BlockSpec((TM, 256), lambda i: (i, 0)),
                  pl.BlockSpec((TM, 512), lambda i: (i, 0)),
                  pl.BlockSpec((NCH, FC, D), lambda i: (0, kw, 0))],
        out_specs=[pl.BlockSpec((TM, D), lambda i: (i, 0)), pl.BlockSpec((1, D), lambda i: (0, 0)),
                   pl.BlockSpec((TM, 256), lambda i: (i, 0)), pl.BlockSpec((TM, 256), lambda i: (i, 0)),
                   pl.BlockSpec((TM, 512), lambda i: (i, 0)), pl.BlockSpec((D, D), lambda i: (0, 0))],
        out_shape=[SDS((S, D), f32), SDS((1, D), f32), SDS((S, 256), f32), SDS((S, 256), f32), SDS((S, 512), f32),
                   SDS((D, D), f32)],
        compiler_params=_cp(48),
    )(x, nw, dres, dh4, ya, yb, yc, blob_b)


QT = NBLK
NB = S // QT


def _attn_block(q, kp, kc, vp, vc, first):
    kk = jnp.concatenate([kp, kc], axis=0).astype(bf16)
    vv = jnp.concatenate([vp, vc], axis=0).astype(bf16)
    qi = lax.broadcasted_iota(jnp.int32, (4 * QT, NBLK + QT), 0) % QT
    ki = lax.broadcasted_iota(jnp.int32, (4 * QT, NBLK + QT), 1)
    dist = NBLK + qi - ki
    valid = (dist >= 0) & (dist <= NBLK) & (jnp.logical_not(first) | (ki >= NBLK))
    head = lax.broadcasted_iota(jnp.int32, (1, ATT), 1) // EH
    masks = [(head == h).astype(f32) for h in range(4)]
    qs = jnp.concatenate([q * (mh * (1.0 / math.sqrt(EH))) for mh in masks], axis=0).astype(bf16)
    s = _dot_nt(qs, kk)
    s = jnp.where(valid, s, NEG)
    m = lax.stop_gradient(jnp.max(s, axis=-1, keepdims=True))
    p = jnp.exp(s - m)
    den = jnp.sum(p, axis=-1, keepdims=True)
    po = _dot((p * (1.0 / den)).astype(bf16), vv)
    lse = m + jnp.log(den)
    o = jnp.zeros((QT, ATT), f32)
    l = jnp.zeros((QT, ATT), f32)
    for h, mh in enumerate(masks):
        o = o + po[QT * h:QT * (h + 1)] * mh
        l = l + lse[QT * h:QT * (h + 1)] * mh
    return o, l


def _attn_specs(tile):
    own = lambda col: pl.BlockSpec((QT, ATT), lambda s: (tile(s), col))
    prev = lambda col: pl.BlockSpec((NBLK, ATT), lambda s: (jnp.maximum((QT // NBLK) * tile(s) - 1, 0), col))
    return [own(0), prev(1), own(1), prev(2), own(2)]


def _attn_fwd(qkv, per_seq):
    def body(q_ref, kp_ref, kc_ref, vp_ref, vc_ref, o_ref, l_ref):
        first = pl.program_id(0) % per_seq == 0
        o, l = _attn_block(q_ref[...], kp_ref[...], kc_ref[...], vp_ref[...], vc_ref[...], first)
        o_ref[...] = o
        l_ref[...] = l

    blk = pl.BlockSpec((QT, ATT), lambda t: (t, 0))
    return _pc(
        body, grid=(NB,), name="attn_fwd", in_specs=_attn_specs(lambda t: t), out_specs=[blk, blk],
        out_shape=[SDS((S, ATT), f32), SDS((S, ATT), f32)], compiler_params=_cp(32),
    )(qkv, qkv, qkv, qkv, qkv)


def _attn_block_bwd(q, kp, kc, vp, vc, o, l, do, dl, first):
    kk = jnp.concatenate([kp, kc], axis=0).astype(bf16)
    vv = jnp.concatenate([vp, vc], axis=0).astype(bf16)
    qi = lax.broadcasted_iota(jnp.int32, (4 * QT, NBLK + QT), 0) % QT
    ki = lax.broadcasted_iota(jnp.int32, (4 * QT, NBLK + QT), 1)
    dist = NBLK + qi - ki
    valid = (dist >= 0) & (dist <= NBLK) & (jnp.logical_not(first) | (ki >= NBLK))
    head = lax.broadcasted_iota(jnp.int32, (1, ATT), 1) // EH
    masks = [(head == h).astype(f32) for h in range(4)]
    scale = 1.0 / math.sqrt(EH)
    stack = lambda f: jnp.concatenate([f(mh) for mh in masks], axis=0)
    qs = stack(lambda mh: q * (mh * scale)).astype(bf16)
    s = jnp.where(valid, _dot_nt(qs, kk), NEG)
    lse = stack(lambda mh: jnp.max(jnp.where(mh > 0.0, l, NEG), axis=1, keepdims=True))
    p = jnp.exp(s - lse)
    dos = stack(lambda mh: do * mh).astype(bf16)
    dvv = _dot_tn(p.astype(bf16), dos)
    dp = _dot_nt(dos, vv)
    delta = stack(lambda mh: jnp.sum(do * o * mh, axis=1, keepdims=True))
    dlse = stack(lambda mh: jnp.sum(dl * mh, axis=1, keepdims=True))
    ds = (p * (dp - delta + dlse)).astype(bf16)
    dqs = _dot(ds, kk)
    dq = jnp.zeros((QT, ATT), f32)
    for h, mh in enumerate(masks):
        dq = dq + dqs[QT * h:QT * (h + 1)] * (mh * scale)
    dkk = _dot_tn(ds, qs)
    return dq, dkk[:NBLK], dkk[NBLK:], dvv[:NBLK], dvv[NBLK:]


def _attn_bwd(qkv, o, l, do, dl, per_seq):
    def body(q_ref, kp_ref, kc_ref, vp_ref, vc_ref, ofw_ref, lfw_ref, do_ref, dl_ref, o_ref, k_carry, v_carry):
        step = pl.program_id(0)

        @pl.when(step == 0)
        def _():
            k_carry[...] = jnp.zeros_like(k_carry)
            v_carry[...] = jnp.zeros_like(v_carry)

        t = NB - 1 - step
        first = t % per_seq == 0
        last = t % per_seq == per_seq - 1
        dq, dkp, dkc, dvp, dvc = _attn_block_bwd(q_ref[...], kp_ref[...], kc_ref[...], vp_ref[...], vc_ref[...], ofw_ref[...],
                                                 lfw_ref[...], do_ref[...], dl_ref[...], first)
        o_ref[:, 0:ATT] = dq
        o_ref[:, ATT:2 * ATT] = dkc
        o_ref[:, 2 * ATT:3 * ATT] = dvc
        o_ref[QT - NBLK:QT, ATT:2 * ATT] += jnp.where(last, 0.0, k_carry[...])
        o_ref[QT - NBLK:QT, 2 * ATT:3 * ATT] += jnp.where(last, 0.0, v_carry[...])
        k_carry[...] = dkp
        v_carry[...] = dvp

    rev = lambda s: NB - 1 - s
    blk = pl.BlockSpec((QT, ATT), lambda s: (rev(s), 0))
    return _pc(
        body, grid=(NB,), name="attn_bwd", in_specs=_attn_specs(rev) + [blk, blk, blk, blk],
        out_specs=pl.BlockSpec((QT, 768), lambda s: (rev(s), 0)),
        out_shape=SDS((S, 768), f32), scratch_shapes=[pltpu.VMEM((NBLK, ATT), f32)] * 2, compiler_params=_cp(40),
    )(qkv, qkv, qkv, qkv, qkv, o, l, do, dl)


def _merge_weights(l0, l1, l2):
    m = jnp.maximum(jnp.maximum(l0, l1), l2)
    e0, e1, e2 = jnp.exp(l0 - m), jnp.exp(l1 - m), jnp.exp(l2 - m)
    tot = e0 + e1 + e2
    return e0 / tot, e1 / tot, e2 / tot


def _merge_specs():
    nat = pl.BlockSpec((TM, ATT), lambda i: (i, 0))
    return nat, _res_spec(4, TM, ATT), _res_spec(16, TM, ATT)


def _merge_fwd(o1, l1, o4, l4, o16, l16):
    def body(o1_ref, l1_ref, o4_ref, l4_ref, o16_ref, l16_ref, y_ref, scr):
        o4v, l4v = _join_residues(o4_ref, 4, scr), _join_residues(l4_ref, 4, scr)
        o16v, l16v = _join_residues(o16_ref, 16, scr), _join_residues(l16_ref, 16, scr)
        w0, w1, w2 = _merge_weights(l1_ref[...], l4v, l16v)
        y_ref[...] = w0 * o1_ref[...] + w1 * o4v + w2 * o16v

    nat, r4, r16 = _merge_specs()
    return _pc(body, grid=(S // TM,), name="merge_fwd", in_specs=[nat, nat, r4, r4, r16, r16],
                          out_specs=nat, out_shape=SDS((S, ATT), f32), scratch_shapes=[pltpu.VMEM((2, TM, 128), f32)],
                          compiler_params=_cp(32))(o1, l1, o4, l4, o16, l16)


def _merge_bwd(o1, l1, o4, l4, o16, l16, dy):
    def body(o1_ref, l1_ref, o4_ref, l4_ref, o16_ref, l16_ref, dy_ref, do1_ref, dl1_ref, do4_ref, dl4_ref, do16_ref, dl16_ref, scr):
        o4v, l4v = _join_residues(o4_ref, 4, scr), _join_residues(l4_ref, 4, scr)
        o16v, l16v = _join_residues(o16_ref, 16, scr), _join_residues(l16_ref, 16, scr)
        o1v = o1_ref[...]
        w0, w1, w2 = _merge_weights(l1_ref[...], l4v, l16v)
        y = w0 * o1v + w1 * o4v + w2 * o16v
        dyv = dy_ref[...]
        do1_ref[...] = w0 * dyv
        dl1_ref[...] = w0 * (o1v - y) * dyv
        _split_residues(w1 * dyv, scr, [(do4_ref, 4)])
        _split_residues(w1 * (o4v - y) * dyv, scr, [(dl4_ref, 4)])
        _split_residues(w2 * dyv, scr, [(do16_ref, 16)])
        _split_residues(w2 * (o16v - y) * dyv, scr, [(dl16_ref, 16)])

    nat, r4, r16 = _merge_specs()
    return _pc(body, grid=(S // TM,), name="merge_bwd", in_specs=[nat, nat, r4, r4, r16, r16, nat],
                          out_specs=[nat, nat, r4, r4, r16, r16],
                          out_shape=[SDS((S, ATT), f32)] * 2 + [SDS((4, S // 4, ATT), f32)] * 2 + [SDS((16, S // 16, ATT), f32)] * 2,
                          scratch_shapes=[pltpu.VMEM((2, TM, 128), f32)], compiler_params=_cp(32))(o1, l1, o4, l4, o16, l16, dy)


HALO = 16


def _pool_consts(i, rows):
    grp = lax.broadcasted_iota(jnp.int32, (rows, 256), 1) // 64
    t = i * TM + lax.broadcasted_iota(jnp.int32, (rows, 256), 0)
    win = jnp.where(grp == 0, 2, jnp.where(grp == 1, 4, jnp.where(grp == 2, 8, 16)))
    cnt = jnp.minimum(t + 1, win).astype(f32)
    return grp, cnt


def _pool_select(grp, s2, s4, s8, s16):
    return jnp.where(grp == 0, s2, jnp.where(grp == 1, s4, jnp.where(grp == 2, s8, s16)))


def _pooled(i, cur, halo):
    xx = jnp.concatenate([halo, cur], axis=0)
    s2 = xx + pltpu.roll(xx, 1, 0)
    s4 = s2 + pltpu.roll(s2, 2, 0)
    s8 = s4 + pltpu.roll(s4, 4, 0)
    s16 = s8 + pltpu.roll(s8, 8, 0)
    grp, cnt = _pool_consts(i, TM)
    tot = _pool_select(grp, s2[HALO:], s4[HALO:], s8[HALO:], s16[HALO:])
    return tot / cnt - cur


def _pool_fwd(u, wp, scale):
    def body(u_ref, halo_ref, wp_ref, sc_ref, y_ref):
        i = pl.program_id(0)
        halo = halo_ref[...] * (i > 0).astype(f32)
        pooled = _pooled(i, u_ref[...], halo)
        y_ref[...] = _dot(pooled.astype(bf16), wp_ref[...]) * sc_ref[...]

    return _pc(
        body, grid=(S // TM,), name="pool_fwd",
        in_specs=[pl.BlockSpec((TM, 256), lambda i: (i, 0)),
                  pl.BlockSpec((HALO, 256), lambda i: (jnp.maximum(i * (TM // HALO) - 1, 0), 0)),
                  pl.BlockSpec((256, 256), lambda i: (0, 0)),
                  pl.BlockSpec((1, 256), lambda i: (0, 0))],
        out_specs=pl.BlockSpec((TM, 256), lambda i: (i, 0)), out_shape=SDS((S, 256), f32), compiler_params=_cp(32),
    )(u, u, wp, scale)


def _pool_bwd(u, wp, scale, dy):
    nt = S // TM

    def body(u_ref, halo_ref, wp_ref, sc_ref, dy_ref, dyn_ref, du_ref, dwp_ref, dsc_ref):
        i = pl.program_id(0)

        @pl.when(i == 0)
        def _():
            dwp_ref[...] = jnp.zeros_like(dwp_ref)
            dsc_ref[...] = jnp.zeros_like(dsc_ref)

        halo = halo_ref[...] * (i > 0).astype(f32)
        pooled = _pooled(i, u_ref[...], halo).astype(bf16)
        dyv = dy_ref[...]
        dsc_ref[...] += jnp.sum(dyv * _dot(pooled, wp_ref[...]), axis=0, keepdims=True)
        dys = (dyv * sc_ref[...]).astype(bf16)
        dwp_ref[...] += _dot_tn(pooled, dys)
        dpool = _dot_nt(dys, wp_ref[...])
        grp, cnt = _pool_consts(i, TM)
        dyn = ((dyn_ref[...] * (i < nt - 1).astype(f32)) * sc_ref[...]).astype(bf16)
        _, cntn = _pool_consts(i + 1, HALO)
        zn = _dot_nt(dyn, wp_ref[...]) / cntn
        zz = jnp.concatenate([dpool / cnt, zn], axis=0)
        n = TM + HALO
        a2 = zz + pltpu.roll(zz, n - 1, 0)
        a4 = a2 + pltpu.roll(a2, n - 2, 0)
        a8 = a4 + pltpu.roll(a4, n - 4, 0)
        a16 = a8 + pltpu.roll(a8, n - 8, 0)
        du_ref[...] = _pool_select(grp, a2[:TM], a4[:TM], a8[:TM], a16[:TM]) - dpool

    return _pc(
        body, grid=(nt,), name="pool_bwd",
        in_specs=[pl.BlockSpec((TM, 256), lambda i: (i, 0)),
                  pl.BlockSpec((HALO, 256), lambda i: (jnp.maximum(i * (TM // HALO) - 1, 0), 0)),
                  pl.BlockSpec((256, 256), lambda i: (0, 0)),
                  pl.BlockSpec((1, 256), lambda i: (0, 0)),
                  pl.BlockSpec((TM, 256), lambda i: (i, 0)),
                  pl.BlockSpec((HALO, 256), lambda i: (jnp.minimum((i + 1) * (TM // HALO), S // HALO - 1), 0))],
        out_specs=[pl.BlockSpec((TM, 256), lambda i: (i, 0)), pl.BlockSpec((256, 256), lambda i: (0, 0)),
                   pl.BlockSpec((1, 256), lambda i: (0, 0))],
        out_shape=[SDS((S, 256), f32), SDS((256, 256), f32), SDS((1, 256), f32)], compiler_params=_cp(32),
    )(u, u, wp, scale, dy, dy)


CW = 3 * DNW
CHALO = 8
TC = 256


def _conv_fwd(u, w):
    def body(u_ref, halo_ref, w_ref, c_ref):
        i = pl.program_id(0)
        xx = jnp.concatenate([halo_ref[...] * (i > 0).astype(f32), u_ref[...]], axis=0)
        c = (w_ref[3:4, :] * xx + w_ref[2:3, :] * pltpu.roll(xx, 1, 0) + w_ref[1:2, :] * pltpu.roll(xx, 2, 0)
             + w_ref[0:1, :] * pltpu.roll(xx, 3, 0))
        c_ref[...] = c[CHALO:]

    return _pc(
        body, grid=(S // TC,), name="conv_fwd",
        in_specs=[pl.BlockSpec((TC, CW), lambda i: (i, 0)),
                  pl.BlockSpec((CHALO, CW), lambda i: (jnp.maximum(i * (TC // CHALO) - 1, 0), 0)),
                  pl.BlockSpec((8, CW), lambda i: (0, 0))],
        out_specs=pl.BlockSpec((TC, CW), lambda i: (i, 0)), out_shape=SDS((S, CW), f32), compiler_params=_cp(32),
    )(u, u, w)


def _conv_bwd(u, w, dc):
    nt = S // TC

    def body(u_ref, halo_ref, w_ref, dc_ref, dcn_ref, du_ref, dw_ref):
        i = pl.program_id(0)

        @pl.when(i == 0)
        def _():
            dw_ref[...] = jnp.zeros_like(dw_ref)

        dcv = dc_ref[...]
        zz = jnp.concatenate([dcv, dcn_ref[...] * (i < nt - 1).astype(f32)], axis=0)
        n = TC + CHALO
        du = (w_ref[3:4, :] * zz + w_ref[2:3, :] * pltpu.roll(zz, n - 1, 0) + w_ref[1:2, :] * pltpu.roll(zz, n - 2, 0)
              + w_ref[0:1, :] * pltpu.roll(zz, n - 3, 0))
        du_ref[...] = du[:TC]
        xx = jnp.concatenate([halo_ref[...] * (i > 0).astype(f32), u_ref[...]], axis=0)
        for j in range(4):
            shifted = xx if j == 3 else pltpu.roll(xx, 3 - j, 0)
            dw_ref[j:j + 1, :] += jnp.sum(dcv * shifted[CHALO:], axis=0, keepdims=True)

    return _pc(
        body, grid=(nt,), name="conv_bwd",
        in_specs=[pl.BlockSpec((TC, CW), lambda i: (i, 0)),
                  pl.BlockSpec((CHALO, CW), lambda i: (jnp.maximum(i * (TC // CHALO) - 1, 0), 0)),
                  pl.BlockSpec((8, CW), lambda i: (0, 0)),
                  pl.BlockSpec((TC, CW), lambda i: (i, 0)),
                  pl.BlockSpec((CHALO, CW), lambda i: (jnp.minimum((i + 1) * (TC // CHALO), S // CHALO - 1), 0))],
        out_specs=[pl.BlockSpec((TC, CW), lambda i: (i, 0)), pl.BlockSpec((8, CW), lambda i: (0, 0))],
        out_shape=[SDS((S, CW), f32), SDS((8, CW), f32)], compiler_params=_cp(32),
    )(u, u, w, dc, dc)


TL = 512
NCL = TL // CH


def _ein(spec, a, b):
    return jnp.einsum(spec, a.astype(bf16), b.astype(bf16), preferred_element_type=f32)


def _ein_ct(spec, x, y, ct_first):
    ct = x if ct_first else y
    hi = ct.astype(bf16)
    lo = ct - hi.astype(f32)
    if ct_first:
        return _ein(spec, hi, y) + _ein(spec, lo, y)
    return _ein(spec, x, hi) + _ein(spec, x, lo)


def _bf16_dot(spec, grad_a, grad_b):
    @jax.custom_vjp
    def dot(a, b):
        return _ein(spec, a, b)

    def fwd(a, b):
        return _ein(spec, a, b), (a, b)

    def bwd(res, ct):
        a, b = res
        return grad_a(a, b, ct), grad_b(a, b, ct)

    dot.defvjp(fwd, bwd)
    return dot


def _bdot(a, b):
    return _ein('nik,nkj->nij', a, b)


def _bdot_nt(a, b):
    return _ein('nik,njk->nij', a, b)


def _bdot_tn(a, b):
    return _ein('nki,nkj->nij', a, b)


_mm = _bf16_dot('ik,kj->ij', lambda a, b, ct: _ein_ct('ij,kj->ik', ct, b, True), lambda a, b, ct: _ein_ct('ik,ij->kj', a, ct, False))
_mm_tn = _bf16_dot('ki,kj->ij', lambda a, b, ct: _ein_ct('kj,ij->ki', b, ct, False),
                   lambda a, b, ct: _ein_ct('ki,ij->kj', a, ct, False))


@jax.custom_vjp
def _inv_unit_lower(a):
    ii = lax.broadcasted_iota(jnp.int32, (1, CH, CH), 1)
    jj = lax.broadcasted_iota(jnp.int32, (1, CH, CH), 2)
    t = (ii == jj).astype(f32) - a
    p = a
    for _ in range(5):
        p = _bdot(p, p)
        t = t + _bdot(t, p)
    return t


def _inv_unit_lower_fwd(a):
    t = _inv_unit_lower(a)
    return t, t


def _inv_unit_lower_bwd(t, dt):
    return (-_bdot_tn(t, _bdot_nt(dt, t)),)


_inv_unit_lower.defvjp(_inv_unit_lower_fwd, _inv_unit_lower_bwd)


def _dn_local(c, dba, a_row, b_row):
    act = c * jax.nn.sigmoid(c)
    lane = lax.broadcasted_iota(jnp.int32, (TL, 128), 1)
    beta_all = jax.nn.sigmoid(dba)
    xs = dba + b_row
    softplus = jnp.maximum(xs, 0.0) + jnp.log(1.0 + jnp.exp(-jnp.abs(xs)))
    g_all = -jnp.exp(a_row) * softplus
    ii = lax.broadcasted_iota(jnp.int32, (1, CH, CH), 1)
    jj = lax.broadcasted_iota(jnp.int32, (1, CH, CH), 2)
    lower = jj <= ii
    strict = jj < ii
    eye = (ii == jj).astype(f32)
    us, ws, qgs, kds, intras = [], [], [], [], []
    aux = jnp.zeros((TL, 128), f32)
    for h in range(4):
        q = act[:, DH * h:DH * (h + 1)]
        k = act[:, DNW + DH * h:DNW + DH * (h + 1)]
        v = act[:, 2 * DNW + DH * h:2 * DNW + DH * (h + 1)]
        q = q * lax.rsqrt(jnp.sum(q * q, axis=-1, keepdims=True) + EPS) * (DH ** -0.5)
        k = k * lax.rsqrt(jnp.sum(k * k, axis=-1, keepdims=True) + EPS)
        beta = jnp.sum(jnp.where(lane == h, beta_all, 0.0), axis=1, keepdims=True)
        g = jnp.sum(jnp.where(lane == 4 + h, g_all, 0.0), axis=1, keepdims=True)
        q3, k3, v3 = q.reshape(NCL, CH, DH), k.reshape(NCL, CH, DH), v.reshape(NCL, CH, DH)
        beta3, g3 = beta.reshape(NCL, CH, 1), g.reshape(NCL, CH, 1)
        g_row = jnp.sum(eye * g3, axis=1, keepdims=True)
        gc_col = jnp.sum(jnp.where(lower, g_row, 0.0), axis=2, keepdims=True)
        gc_row = jnp.sum(jnp.where(ii <= jj, g3, 0.0), axis=1, keepdims=True)
        diff = gc_col - gc_row
        decay = jnp.where(lower, jnp.exp(jnp.where(lower, diff, 0.0)), 0.0)
        kb = k3 * beta3
        vb = v3 * beta3
        a = jnp.where(strict, _bdot_nt(kb, k3) * decay, 0.0)
        t = _inv_unit_lower(a)
        u3 = _bdot(t, vb)
        w3 = _bdot(t, kb * jnp.exp(gc_col))
        intra = jnp.where(lower, _bdot_nt(q3, k3) * decay, 0.0)
        g_last = jnp.sum(g3, axis=1, keepdims=True)
        us.append(u3.reshape(TL, DH))
        ws.append(w3.reshape(TL, DH))
        qgs.append((q3 * jnp.exp(gc_col)).reshape(TL, DH))
        kds.append((k3 * jnp.exp(g_last - gc_col)).reshape(TL, DH))
        intras.append(intra.reshape(TL, CH))
        e_last = jnp.broadcast_to(jnp.exp(g_last), (NCL, CH, 1)).reshape(TL, 1)
        aux = aux + jnp.where(lane == h, e_last, 0.0)
    cat = lambda xs: jnp.concatenate(xs, axis=1)
    return cat(us), cat(ws), cat(qgs), cat(kds), jnp.stack(intras, axis=0), aux


def _dn_local_fwd(c, dba, par):
    def body(c_ref, dba_ref, par_ref, u_ref, w_ref, qg_ref, kd_ref, in_ref, aux_ref):
        u, w, qg, kd, intra, aux = _dn_local(c_ref[...], dba_ref[...], par_ref[0:1, :], par_ref[1:2, :])
        u_ref[...] = u
        w_ref[...] = w
        qg_ref[...] = qg
        kd_ref[...] = kd
        in_ref[...] = intra
        aux_ref[...] = aux

    wide = pl.BlockSpec((TL, DNW), lambda i: (i, 0))
    return _pc(
        body, grid=(S // TL,), name="dn_local_fwd",
        in_specs=[pl.BlockSpec((TL, CW), lambda i: (i, 0)), pl.BlockSpec((TL, 128), lambda i: (i, 0)),
                  pl.BlockSpec((8, 128), lambda i: (0, 0))],
        out_specs=[wide, wide, wide, wide, pl.BlockSpec((4, TL, CH), lambda i: (0, i, 0)),
                   pl.BlockSpec((TL, 128), lambda i: (i, 0))],
        out_shape=[SDS((S, DNW), f32)] * 4 + [SDS((4, S, CH), f32), SDS((S, 128), f32)], compiler_params=_cp(48),
    )(c, dba, par)


def _dn_local_bwd(c, dba, par, du, dw, dqg, dkd, dintra, daux):
    def body(c_ref, dba_ref, par_ref, du_ref, dw_ref, dqg_ref, dkd_ref, din_ref, daux_ref, dc_ref, ddba_ref, dpar_ref):
        @pl.when(pl.program_id(0) == 0)
        def _():
            dpar_ref[...] = jnp.zeros_like(dpar_ref)

        _, vjp = jax.vjp(_dn_local, c_ref[...], dba_ref[...], par_ref[0:1, :], par_ref[1:2, :])
        dc, ddba, da_row, db_row = vjp((du_ref[...], dw_ref[...], dqg_ref[...], dkd_ref[...], din_ref[...], daux_ref[...]))
        dc_ref[...] = dc
        ddba_ref[...] = ddba
        dpar_ref[0:1, :] += da_row
        dpar_ref[1:2, :] += db_row

    wide = pl.BlockSpec((TL, DNW), lambda i: (i, 0))
    return _pc(
        body, grid=(S // TL,), name="dn_local_bwd",
        in_specs=[pl.BlockSpec((TL, CW), lambda i: (i, 0)), pl.BlockSpec((TL, 128), lambda i: (i, 0)),
                  pl.BlockSpec((8, 128), lambda i: (0, 0)), wide, wide, wide, wide,
                  pl.BlockSpec((4, TL, CH), lambda i: (0, i, 0)), pl.BlockSpec((TL, 128), lambda i: (i, 0))],
        out_specs=[pl.BlockSpec((TL, CW), lambda i: (i, 0)), pl.BlockSpec((TL, 128), lambda i: (i, 0)),
                   pl.BlockSpec((8, 128), lambda i: (0, 0))],
        out_shape=[SDS((S, CW), f32), SDS((S, 128), f32), SDS((8, 128), f32)], compiler_params=_cp(56),
    )(c, dba, par, du, dw, dqg, dkd, dintra, daux)


def _dn_step(state, u, w, qg, kd, intra, aux):
    lane = lax.broadcasted_iota(jnp.int32, (CH, 128), 1)
    row = lax.broadcasted_iota(jnp.int32, (CH, 128), 0)
    outs, states = [], []
    for h in range(4):
        sl = slice(DH * h, DH * (h + 1))
        st = state[h]
        e = jnp.sum(jnp.sum(jnp.where((lane == h) & (row == 0), aux, 0.0), axis=1, keepdims=True), axis=0, keepdims=True)
        v_new = u[:, sl] - _mm(w[:, sl], st)
        outs.append(_mm(qg[:, sl], st) + _mm(intra[h], v_new))
        states.append(st * e + _mm_tn(kd[:, sl], v_new))
    return jnp.concatenate(outs, axis=1), jnp.stack(states, axis=0)


CPS = 8
NSTEP = NCHUNK // CPS


def _dn_rec_specs(index):
    wide = pl.BlockSpec((CPS * CH, DNW), lambda n: (index(n), 0))
    inb = pl.BlockSpec((4, CPS * CH, CH), lambda n: (0, index(n), 0))
    auxb = pl.BlockSpec((CPS * CH, 128), lambda n: (index(n), 0))
    stb = pl.BlockSpec((CPS, 4, DH, DH), lambda n: (index(n), 0, 0, 0))
    return wide, inb, auxb, stb


def _dn_rec_fwd(u, w, qg, kd, intra, aux):
    def body(u_ref, w_ref, qg_ref, kd_ref, in_ref, aux_ref, o_ref, st_ref, st_scr):
        @pl.when(pl.program_id(0) == 0)
        def _():
            st_scr[...] = jnp.zeros_like(st_scr)

        st = st_scr[...]
        for k in range(CPS):
            rows = slice(CH * k, CH * (k + 1))
            st_ref[k] = st
            o, st = _dn_step(st, u_ref[rows, :], w_ref[rows, :], qg_ref[rows, :], kd_ref[rows, :], in_ref[:, rows, :],
                             aux_ref[rows, :])
            o_ref[rows, :] = o
        st_scr[...] = st

    wide, inb, auxb, stb = _dn_rec_specs(lambda n: n)
    return _pc(
        body, grid=(NSTEP,), name="dn_rec_fwd", in_specs=[wide, wide, wide, wide, inb, auxb], out_specs=[wide, stb],
        out_shape=[SDS((S, DNW), f32), SDS((NCHUNK, 4, DH, DH), f32)],
        scratch_shapes=[pltpu.VMEM((4, DH, DH), f32)], compiler_params=_cp(32),
    )(u, w, qg, kd, intra, aux)


def _dn_rec_bwd(u, w, qg, kd, intra, aux, states, do):
    def body(u_ref, w_ref, qg_ref, kd_ref, in_ref, aux_ref, st_ref, do_ref,
             du_ref, dw_ref, dqg_ref, dkd_ref, din_ref, daux_ref, ds_scr):
        @pl.when(pl.program_id(0) == 0)
        def _():
            ds_scr[...] = jnp.zeros_like(ds_scr)

        ds = ds_scr[...]
        for k in reversed(range(CPS)):
            rows = slice(CH * k, CH * (k + 1))
            _, vjp = jax.vjp(_dn_step, st_ref[k], u_ref[rows, :], w_ref[rows, :], qg_ref[rows, :], kd_ref[rows, :],
                             in_ref[:, rows, :], aux_ref[rows, :])
            ds, du, dw, dqg, dkd, din, daux = vjp((do_ref[rows, :], ds))
            du_ref[rows, :] = du
            dw_ref[rows, :] = dw
            dqg_ref[rows, :] = dqg
            dkd_ref[rows, :] = dkd
            din_ref[:, rows, :] = din
            daux_ref[rows, :] = daux
        ds_scr[...] = ds

    wide, inb, auxb, stb = _dn_rec_specs(lambda n: NSTEP - 1 - n)
    return _pc(
        body, grid=(NSTEP,), name="dn_rec_bwd", in_specs=[wide, wide, wide, wide, inb, auxb, stb, wide],
        out_specs=[wide, wide, wide, wide, inb, auxb],
        out_shape=[SDS((S, DNW), f32)] * 4 + [SDS((4, S, CH), f32), SDS((S, 128), f32)],
        scratch_shapes=[pltpu.VMEM((4, DH, DH), f32)], compiler_params=_cp(40),
    )(u, w, qg, kd, intra, aux, states, do)


def _dn_post(o, z, nw):
    parts = []
    for h in range(4):
        sl = slice(DH * h, DH * (h + 1))
        oh = o[:, sl]
        y = oh * lax.rsqrt(jnp.mean(oh * oh, axis=-1, keepdims=True) + EPS) * nw
        zh = z[:, sl]
        parts.append(y * (zh * jax.nn.sigmoid(zh)))
    return jnp.concatenate(parts, axis=1)


def _dn_post_fwd(o, z, nw):
    def body(o_ref, z_ref, nw_ref, y_ref):
        y_ref[...] = _dn_post(o_ref[...], z_ref[...], nw_ref[...])

    wide = pl.BlockSpec((TM, DNW), lambda i: (i, 0))
    return _pc(body, grid=(S // TM,), name="dn_post_fwd",
                          in_specs=[wide, wide, pl.BlockSpec((1, 128), lambda i: (0, 0))], out_specs=wide,
                          out_shape=SDS((S, DNW), f32), compiler_params=_cp(32))(o, z, nw)


def _dn_post_bwd(o, z, nw, dy):
    def body(o_ref, z_ref, nw_ref, dy_ref, do_ref, dz_ref, dnw_ref):
        @pl.when(pl.program_id(0) == 0)
        def _():
            dnw_ref[...] = jnp.zeros_like(dnw_ref)

        _, vjp = jax.vjp(_dn_post, o_ref[...], z_ref[...], nw_ref[...])
        do, dz, dnw = vjp(dy_ref[...])
        do_ref[...] = do
        dz_ref[...] = dz
        dnw_ref[...] += dnw

    wide = pl.BlockSpec((TM, DNW), lambda i: (i, 0))
    one = pl.BlockSpec((1, 128), lambda i: (0, 0))
    return _pc(body, grid=(S // TM,), name="dn_post_bwd", in_specs=[wide, wide, one, wide],
                          out_specs=[wide, wide, one], out_shape=[SDS((S, DNW), f32), SDS((S, DNW), f32), SDS((1, 128), f32)],
                          compiler_params=_cp(32))(o, z, nw, dy)


def _row_tile(rows, width, itemsize=4, target=2 * 1024 * 1024):
    best = None
    for t in range(16, rows + 1, 16):
        if rows % t == 0 and t * width * itemsize <= target:
            best = t
    return best if best is not None else rows


def _sum_pieces(pieces, out_dtype, name):
    n, rows, width = pieces.shape
    tr = _row_tile(rows, width * n)

    def body(p_ref, o_ref):
        acc = p_ref[0].astype(f32)
        for s in range(1, n):
            acc = acc + p_ref[s].astype(f32)
        o_ref[...] = acc.astype(out_dtype)

    return _pc(body, grid=(rows // tr,), name=name,
                          in_specs=[pl.BlockSpec((n, tr, width), lambda i: (0, i, 0))],
                          out_specs=pl.BlockSpec((tr, width), lambda i: (i, 0)),
                          out_shape=SDS((rows, width), out_dtype), compiler_params=_cp(32))(pieces)


def _sum_core_pair(part, got, c_arr):
    n, rows, width = part.shape
    half = rows // 2
    tr = _row_tile(half, width, itemsize=2)
    nt = half // tr

    def body(c_ref, p_ref, g_ref, o_ref):
        o_ref[...] = (p_ref[...].astype(f32) + g_ref[...].astype(f32)).astype(bf16)

    gs = pltpu.PrefetchScalarGridSpec(
        num_scalar_prefetch=1, grid=(n, nt),
        in_specs=[pl.BlockSpec((1, tr, width), lambda j, i, c: (j, c[0] * nt + i, 0)),
                  pl.BlockSpec((1, tr, width), lambda j, i, c: (j, i, 0))],
        out_specs=pl.BlockSpec((1, tr, width), lambda j, i, c: (j, i, 0)))
    return _pc(body, grid_spec=gs, name="sum_core_pair", out_shape=SDS((n, half, width), bf16),
                          compiler_params=_cp(32))(c_arr, part, got)


def _sum_chips(pieces, c_arr, full, row0, total_rows):
    n, half, width = pieces.shape
    tr = max(t for t in range(16, 257, 16) if half % t == 0 and row0 % t == 0)
    nt = half // tr

    def body(c_ref, p_ref, *rest):
        o_ref = rest[-1]
        acc = p_ref[0].astype(f32)
        for s in range(1, n):
            acc = acc + p_ref[s].astype(f32)
        o_ref[...] = acc

    gs = pltpu.PrefetchScalarGridSpec(
        num_scalar_prefetch=1, grid=(nt,),
        in_specs=[pl.BlockSpec((n, tr, width), lambda i, c: (0, i, 0))] + ([] if full is None else [ANY]),
        out_specs=pl.BlockSpec((tr, width), lambda i, c: (row0 // tr + c[0] * nt + i, 0)))
    args = (c_arr, pieces) if full is None else (c_arr, pieces, full)
    return _pc(body, grid_spec=gs, name="sum_chips", out_shape=SDS((total_rows, width), f32),
                          input_output_aliases={} if full is None else {2: 0}, compiler_params=_cp(32))(*args)


def _adamw_math(w, g, m, v):
    mn = ADAM_B1 * m + (1.0 - ADAM_B1) * g
    vn = ADAM_B2 * v + (1.0 - ADAM_B2) * (g * g)
    m_hat = mn / (1.0 - ADAM_B1 ** ADAM_STEP)
    v_hat = vn / (1.0 - ADAM_B2 ** ADAM_STEP)
    return -ADAM_LR * (m_hat / (jnp.sqrt(v_hat) + ADAM_EPS) + ADAM_WD * w), mn, vn


def _adamw(w, g, m, v, name):
    rows, width = w.shape
    tr = _row_tile(rows, width * 7, target=12 * 1024 * 1024)

    def body(w_ref, g_ref, m_ref, v_ref, d_ref, nm_ref, nv_ref):
        d_ref[...], nm_ref[...], nv_ref[...] = _adamw_math(w_ref[...], g_ref[...], m_ref[...], v_ref[...])

    blk = pl.BlockSpec((tr, width), lambda i: (i, 0))
    return _pc(body, grid=(rows // tr,), name=name, in_specs=[blk] * 4, out_specs=[blk] * 3,
                          out_shape=[SDS((rows, width), f32)] * 3, compiler_params=_cp(40))(w, g, m, v)


def _adamw_rows(w, m, v, gblob, tr, first_tile, name):
    layers, rows, width = w.shape

    def body(w_ref, g_ref, m_ref, v_ref, d_ref, nm_ref, nv_ref):
        d_ref[0], nm_ref[0], nv_ref[0] = _adamw_math(w_ref[0], g_ref[...], m_ref[0], v_ref[0])

    blk = pl.BlockSpec((1, tr, width), lambda l, i: (l, i, 0))
    gblk = pl.BlockSpec((tr, width), lambda l, i: (first_tile(l) + i, 0))
    return _pc(body, grid=(layers, rows // tr), name=name, in_specs=[blk, gblk, blk, blk], out_specs=[blk] * 3,
                          out_shape=[SDS(w.shape, f32)] * 3, compiler_params=_cp(40))(w, gblob, m, v)


ANY = pl.BlockSpec(memory_space=pl.ANY)


def _place():
    x, y, c = lax.axis_index("x"), lax.axis_index("y"), lax.axis_index("c")
    chips = [(1 - x, y), (x, 1 - y), (1 - x, 1 - y)]
    return x, y, c, chips


NQ_ICI = 4
NQ_D2D = 8


def _chunks(rows, want):
    n = max(k for k in range(1, want + 1) if rows % k == 0 and (rows // k) % 16 == 0)
    step = rows // n
    return [(q * step, step) for q in range(n)]


def _scatter_copies(ins, outs, ssem, rsem, lsem):
    x, y, c, chips = _place()
    me = (x, y, c)
    locals_, sends, lands = [], [], []
    for b in range(len(ins)):
        for q, (off, n) in enumerate(_chunks(ins[b].shape[1], NQ_ICI)):
            rows = pl.ds(off, n)
            mine = outs[b].at[2 * x + y, rows, :]
            locals_.append(pltpu.make_async_copy(ins[b].at[2 * x + y, rows, :], mine, lsem.at[b, q]))
            for j, chip in enumerate(chips):
                sends.append(_remote(ins[b].at[2 * chip[0] + chip[1], rows, :], mine, ssem.at[b, j, q], rsem.at[b, j, q],
                                     (*chip, c)))
                slot = outs[b].at[2 * chip[0] + chip[1], rows, :]
                lands.append(_remote(slot, slot, ssem.at[b, j, q], rsem.at[b, j, q], me))
    return locals_, sends, lands


def _scatter_start(ins, outs, ssem, rsem, lsem):
    locals_, sends, _ = _scatter_copies(ins, outs, ssem, rsem, lsem)
    for cp in locals_ + sends:
        cp.start()


def _scatter_finish(ins, outs, ssem, rsem, lsem):
    locals_, sends, lands = _scatter_copies(ins, outs, ssem, rsem, lsem)
    for cp in lands:
        cp.wait_recv()
    for cp in sends:
        cp.wait_send()
    for cp in locals_:
        cp.wait()


def _scatter_sems(nb):
    return [pltpu.SemaphoreType.DMA((nb, 3, NQ_ICI)), pltpu.SemaphoreType.DMA((nb, 3, NQ_ICI)),
            pltpu.SemaphoreType.DMA((nb, NQ_ICI))]


def _remote(src, dst, ssem, rsem, dev):
    return pltpu.make_async_remote_copy(src_ref=src, dst_ref=dst, send_sem=ssem, recv_sem=rsem, device_id=dev,
                                        device_id_type=MESH)


def _all_gather_weights(shards):
    nb = len(shards)

    def body(*refs):
        ins, outs, sems = refs[:nb], refs[nb:2 * nb], refs[2 * nb:]
        _gather_start(ins, outs, *sems)
        _gather_finish(ins, outs, *sems)

    return _pc(
        body, name="all_gather_weights", in_specs=[ANY] * nb, out_specs=[ANY] * nb,
        out_shape=[SDS((NCH,) + s.shape, s.dtype) for s in shards], scratch_shapes=_gather_sems(nb),
    )(*shards)


def _rows_to_move(ref):
    return 3 * FC + 256 if ref.shape[0] == 4 * FC else ref.shape[0]


def _gather_first(ins, outs, ssem, rsem, lsem):
    x, y, c, chips = _place()
    locals_, sends = [], []
    for b in range(len(ins)):
        half = _rows_to_move(ins[b]) // 2
        for q, (off, n) in enumerate(_chunks(half, NQ_ICI)):
            mine = pl.ds(c * half + off, n)
            own = outs[b].at[2 * x + y, mine, :]
            locals_.append(pltpu.make_async_copy(ins[b].at[mine, :], own, lsem.at[b, q]))
            sends.append(_remote(ins[b].at[mine, :], own, ssem.at[b, 0, q], rsem.at[b, 0, q], (x, y, 1 - c)))
            sends += [_remote(ins[b].at[mine, :], own, ssem.at[b, 1 + j, q], rsem.at[b, 1 + j, q], (*chip, c))
                      for j, chip in enumerate(chips)]
    return locals_, sends


def _gather_start(ins, outs, ssem, rsem, lsem):
    locals_, sends = _gather_first(ins, outs, ssem, rsem, lsem)
    for cp in locals_ + sends:
        cp.start()


def _gather_finish(ins, outs, ssem, rsem, lsem):
    x, y, c, chips = _place()
    me, sib = (x, y, c), (x, y, 1 - c)
    locals_, sends = _gather_first(ins, outs, ssem, rsem, lsem)
    for b in range(len(ins)):
        half = _rows_to_move(ins[b]) // 2
        for q, (off, n) in enumerate(_chunks(half, NQ_ICI)):
            mine = pl.ds(c * half + off, n)
            for j, chip in enumerate(chips):
                landed = outs[b].at[2 * chip[0] + chip[1], mine, :]
                _remote(landed, landed, ssem.at[b, 1 + j, q], rsem.at[b, 1 + j, q], me).wait_recv()
                cp = _remote(landed, landed, ssem.at[b, 4 + j, q], rsem.at[b, 4 + j, q], sib)
                cp.start()
                sends.append(cp)
    for b in range(len(ins)):
        half = _rows_to_move(ins[b]) // 2
        for q, (off, n) in enumerate(_chunks(half, NQ_ICI)):
            other = pl.ds((1 - c) * half + off, n)
            theirs = outs[b].at[2 * x + y, other, :]
            _remote(theirs, theirs, ssem.at[b, 0, q], rsem.at[b, 0, q], me).wait_recv()
            for j, chip in enumerate(chips):
                fwd = outs[b].at[2 * chip[0] + chip[1], other, :]
                _remote(fwd, fwd, ssem.at[b, 4 + j, q], rsem.at[b, 4 + j, q], me).wait_recv()
    for cp in sends:
        cp.wait_send()
    for cp in locals_:
        cp.wait()


def _gather_sems(nb):
    return [pltpu.SemaphoreType.DMA((nb, 7, NQ_ICI)), pltpu.SemaphoreType.DMA((nb, 7, NQ_ICI)),
            pltpu.SemaphoreType.DMA((nb, NQ_ICI))]


def _send_sibling_half(parts):
    nb = len(parts)

    def body(*refs):
        ins, gots = refs[:nb], refs[nb:2 * nb]
        ssem, rsem = refs[2 * nb:]
        x, y, c, _ = _place()
        sib = (x, y, 1 - c)
        todo = []
        for b in range(nb):
            half = ins[b].shape[1] // 2
            for q, (off, n) in enumerate(_chunks(half, NQ_D2D)):
                cp = _remote(ins[b].at[:, pl.ds((1 - c) * half + off, n), :], gots[b].at[:, pl.ds(off, n), :],
                             ssem.at[b, q], rsem.at[b, q], sib)
                cp.start()
                todo.append(cp)
        for cp in todo:
            cp.wait()

    return _pc(
        body, name="send_sibling_half", in_specs=[ANY] * nb, out_specs=[ANY] * nb,
        out_shape=[SDS((p.shape[0], p.shape[1] // 2, p.shape[2]), p.dtype) for p in parts],
        scratch_shapes=[pltpu.SemaphoreType.DMA((nb, NQ_D2D)), pltpu.SemaphoreType.DMA((nb, NQ_D2D))],
    )(*parts)


def _scatter_to_chips(parts):
    nb = len(parts)

    def body(*refs):
        ins, outs, sems = refs[:nb], refs[nb:2 * nb], refs[2 * nb:]
        _scatter_start(ins, outs, *sems)
        _scatter_finish(ins, outs, *sems)

    return _pc(
        body, name="scatter_to_chips", in_specs=[ANY] * nb, out_specs=[ANY] * nb,
        out_shape=[SDS(p.shape, p.dtype) for p in parts], scratch_shapes=_scatter_sems(nb),
    )(*parts)


def _join_halves(fulls, ranges):
    nb = len(fulls)
    nr = max(len(r) for r in ranges)

    def body(*refs):
        ins, outs = refs[:nb], refs[nb:2 * nb]
        ssem, rsem = refs[2 * nb:]
        x, y, c, _ = _place()
        sib = (x, y, 1 - c)
        sends, lands = [], []
        for b in range(nb):
            for g, (row0, rows) in enumerate(ranges[b]):
                half = rows // 2
                for q, (off, n) in enumerate(_chunks(half, NQ_D2D)):
                    mine = pl.ds(row0 + c * half + off, n)
                    sends.append(_remote(ins[b].at[mine, :], outs[b].at[mine, :], ssem.at[b, g, q], rsem.at[b, g, q], sib))
                    other = outs[b].at[pl.ds(row0 + (1 - c) * half + off, n), :]
                    lands.append(_remote(other, other, ssem.at[b, g, q], rsem.at[b, g, q], sib))
        for cp in sends:
            cp.start()
        for cp in lands:
            cp.wait_recv()
        for cp in sends:
            cp.wait_send()

    return _pc(
        body, name="join_halves", in_specs=[ANY] * nb, out_specs=[ANY] * nb,
        out_shape=[SDS(h.shape, h.dtype) for h in fulls], input_output_aliases={b: b for b in range(nb)},
        scratch_shapes=[pltpu.SemaphoreType.DMA((nb, nr, NQ_D2D)), pltpu.SemaphoreType.DMA((nb, nr, NQ_D2D))],
    )(*fulls)


def _gather_small(vec):
    def body(v_ref, o_ref, ssem, rsem, lsem):
        x, y, c, _ = _place()
        mine = o_ref.at[4 * x + 2 * y + c]
        local = pltpu.make_async_copy(v_ref, mine, lsem)
        local.start()
        sends = []
        for k in range(1, 8):
            peer = (x ^ (k >> 2), y ^ ((k >> 1) & 1), c ^ (k & 1))
            cp = _remote(v_ref, mine, ssem.at[k - 1], rsem.at[k - 1], peer)
            cp.start()
            sends.append(cp)
        for k in range(1, 8):
            px, py, pc = x ^ (k >> 2), y ^ ((k >> 1) & 1), c ^ (k & 1)
            slot = o_ref.at[4 * px + 2 * py + pc]
            _remote(slot, slot, ssem.at[k - 1], rsem.at[k - 1], (x, y, c)).wait_recv()
        for cp in sends:
            cp.wait_send()
        local.wait()

    return _pc(
        body, name="gather_small", in_specs=[ANY], out_specs=ANY, out_shape=SDS((8,) + vec.shape, vec.dtype),
        scratch_shapes=[pltpu.SemaphoreType.DMA((7,)), pltpu.SemaphoreType.DMA((7,)), pltpu.SemaphoreType.DMA],
    )(vec)


def _block_diag(pw):
    return jnp.concatenate([jnp.pad(pw[g], ((0, 0), (64 * g, 192 - 64 * g))) for g in range(4)], axis=0)


def _own_columns(full, chip):
    n = full.shape[-1] // NCH
    parts = full.reshape(full.shape[:-1] + (NCH, n))
    sel = (lax.broadcasted_iota(jnp.int32, (NCH, 1), 0) == chip)
    return jnp.sum(jnp.where(sel, parts, 0.0), axis=-2)


def _at_own_columns(shard, chip):
    n = shard.shape[-1]
    sel = (lax.broadcasted_iota(jnp.int32, (NCH * n,), 0) // n == chip)
    return jnp.where(sel, jnp.tile(shard, NCH), 0.0)


def _pad_rows(a, rows):
    return jnp.pad(a, ((0, rows - a.shape[0]),) + ((0, 0),) * (a.ndim - 1))


def _ffn_block(l, which):
    return 7 * l + 3 * which


def _wout_block(l):
    return 7 * l + 6


class _Weights:
    def __init__(self):
        self.ffn, self.wout, self.w_aug, self.rides = {}, {}, {}, {}

    @classmethod
    def from_blob(cls, blob, w_aug):
        self = cls()
        for l in range(DEPTH):
            self.ffn[(l, 0)], self.ffn[(l, 1)] = (blob, _ffn_block(l, 0)), (blob, _ffn_block(l, 1))
            self.wout[l], self.w_aug[l] = (blob, _wout_block(l)), w_aug[l]
        return self

    def set_w_in(self, l, gathered):
        self.w_aug[l] = jnp.pad(gathered.transpose(1, 0, 2).reshape(D, INW), ((0, 0), (0, INP - INW)))

    def ffn_fwd(self, l, which, x, nw):
        arr, k0 = self.ffn[(l, which)]
        if (l, which) not in self.rides:
            return _ffn_fwd(x, nw, arr, k0)
        shards, landed = self.rides[(l, which)]
        out, *gathered = _ffn_fwd(x, nw, arr, k0, shards)
        landed(gathered)
        return out


def _layer_fwd(l, x0, pos, freq, wts, ws):
    sv = {"x0": x0}
    x1 = ws.ffn_fwd(l, 0, x0, wts["ffn1_norm"][l:l + 1])
    att, att4, att16, pu, dq, dz, dba = _inproj_fwd(x1, wts["mix_norm"][l:l + 1], ws.w_aug[l], pos, freq)
    qkvs = [att, att4.reshape(S, 768), att16.reshape(S, 768)]
    (o1, l1), (o4, l4), (o16, l16) = [_attn_fwd(q, NB // d) for q, d in zip(qkvs, PATTERN_DIL)]
    ols = (o1, l1, o4.reshape(4, S // 4, ATT), l4.reshape(4, S // 4, ATT), o16.reshape(16, S // 16, ATT),
           l16.reshape(16, S // 16, ATT))
    ya = _merge_fwd(*ols)
    yb = _pool_fwd(pu, wts["pool_bd"][l], wts["pool_scale"][l:l + 1])
    c = _conv_fwd(dq, wts["conv_w"][l])
    u, w, qg, kd, intra, aux = _dn_local_fwd(c, dba, wts["dn_par"][l])
    o_dn, states = _dn_rec_fwd(u, w, qg, kd, intra, aux)
    yc = _dn_post_fwd(o_dn, dz, wts["dn_out_norm"][l:l + 1])
    x2 = _outproj_fwd(x1, ya, yb, yc, *ws.wout[l])
    x3 = ws.ffn_fwd(l, 1, x2, wts["ffn2_norm"][l:l + 1])
    sv.update(x1=x1, x2=x2, qkvs=qkvs, ols=ols, ya=ya, yb=yb, yc=yc, pu=pu, dq=dq, dz=dz, dba=dba, c=c,
              u=u, w=w, qg=qg, kd=kd, intra=intra, aux=aux, states=states, o_dn=o_dn)
    return x3, sv


def _wout_part(g):
    return jnp.pad(g.astype(bf16).reshape(NCH, 256, D), ((0, 0), (0, FC - 256), (0, 0)))


def _win_part(g):
    return g[:, :INW].astype(bf16).reshape(D, NCH, INC).transpose(1, 0, 2)


def _layer_bwd(l, dx3, sv, pos, freq, wts, ws, ride=None, prep=None):
    gr = {}
    g2, u2, d2, dh4, *pieces_before = _ffn_bwd(sv["x2"], wts["ffn2_norm"][l:l + 1], *ws.ffn[(l, 1)], dx3, ride)
    gr.update(ffn2_w_gate=g2, ffn2_w_up=u2, ffn2_w_down=d2)
    dx2, gr["ffn2_norm"], dya, dyb, dyc, gr["w_out"] = _outproj_bwd(sv["x2"], wts["ffn2_norm"][l:l + 1], dx3, dh4, sv["ya"],
                                                                     sv["yb"], sv["yc"], *ws.wout[l])
    do_dn, ddz, gr["dn_out_norm"] = _dn_post_bwd(sv["o_dn"], sv["dz"], wts["dn_out_norm"][l:l + 1], dyc)
    du, dw, dqg, dkd, dintra, daux = _dn_rec_bwd(sv["u"], sv["w"], sv["qg"], sv["kd"], sv["intra"], sv["aux"], sv["states"], do_dn)
    dc, ddba, gr["dn_par"] = _dn_local_bwd(sv["c"], sv["dba"], wts["dn_par"][l], du, dw, dqg, dkd, dintra, daux)
    ddq, gr["conv_w"] = _conv_bwd(sv["dq"], wts["conv_w"][l], dc)
    dpu, gr["pool_bd"], gr["pool_scale"] = _pool_bwd(sv["pu"], wts["pool_bd"][l], wts["pool_scale"][l:l + 1], dyb)
    dols = _merge_bwd(*sv["ols"], dya)
    flat = lambda a: a.reshape(S, ATT)
    datts = [_attn_bwd(q, flat(sv["ols"][2 * p]), flat(sv["ols"][2 * p + 1]), flat(dols[2 * p]), flat(dols[2 * p + 1]), NB // d)
             for p, (q, d) in enumerate(zip(sv["qkvs"], PATTERN_DIL))]
    dx1, gr["mix_norm"], gr["w_aug"] = _inproj_bwd(sv["x1"], wts["mix_norm"][l:l + 1], ws.w_aug[l], pos, freq, dx2,
                                                    datts[0], datts[1].reshape(4, S // 4, 768),
                                                    datts[2].reshape(16, S // 16, 768), dpu, ddq, ddz, ddba)
    own = None
    if prep is not None:
        own = prep([jnp.concatenate([g2, u2, d2, _wout_part(gr["w_out"])], axis=1), _win_part(gr["w_aug"])])
    g1, u1, d1, dh4, *pieces_own = _ffn_bwd(sv["x0"], wts["ffn1_norm"][l:l + 1], *ws.ffn[(l, 0)], dx1, own)
    dx0, gr["ffn1_norm"] = _norm_bwd(sv["x0"], wts["ffn1_norm"][l:l + 1], dx1, dh4)
    gr.update(ffn1_w_gate=g1, ffn1_w_up=u1, ffn1_w_down=d1)
    return dx0, gr, pieces_before, pieces_own


def _device_step(x, pos, target, wts, ws, prep=None):
    freq = jnp.tile(ROPE_THETA ** (-jnp.arange(0, EH, 2, dtype=f32) / EH), 2 * ATT // EH).reshape(1, ATT)
    saved = []
    h = x
    for l in range(DEPTH):
        h, sv = _layer_fwd(l, h, pos, freq, wts, ws)
        saved.append(sv)
    dh, g_final, loss = _final(h, wts["final_norm"], target)
    grads = [None] * DEPTH
    dh, grads[1], _, _ = _layer_bwd(1, dh, saved[1], pos, freq, wts, ws)
    sums1 = None
    if prep is not None:
        g = grads[1]
        ffn = [g[f"ffn{f}_w_{n}"] for f in (1, 2) for n in ("gate", "up", "down")]
        sums1 = prep([jnp.concatenate(ffn + [_wout_part(g["w_out"])], axis=1), _win_part(g["w_aug"])])
    dh, grads[0], pieces1, pieces0 = _layer_bwd(0, dh, saved[0], pos, freq, wts, ws, sums1, prep)
    return loss, dh, g_final, grads, pieces1, pieces0


_SMALL = (("ffn1_norm", (DEPTH, D)), ("mix_norm", (DEPTH, D)), ("pool_w", (DEPTH, 4, 64, 64)), ("pool_scale", (DEPTH, 256)),
          ("dn_conv_w", (DEPTH, 4, CW)), ("dn_a_log", (DEPTH, 4)), ("dn_dt_bias", (DEPTH, 4)), ("dn_out_norm", (DEPTH, 128)),
          ("ffn2_norm", (DEPTH, D)), ("final_norm", (D,)), ("loss", (1,)))


def _pack_small(vals):
    rows = []
    for name, shape in _SMALL:
        flat = vals[name].astype(f32).reshape(-1)
        rows.append(jnp.pad(flat, (0, _small_rows(shape) * 128 - flat.shape[0])).reshape(-1, 128))
    out = jnp.concatenate(rows, axis=0)
    return _pad_rows(out, -(-out.shape[0] // 16) * 16)


def _small_rows(shape):
    return -(-int(np.prod(shape)) // 1024) * 8


def _unpack_small(packed):
    vals, r = {}, 0
    for name, shape in _SMALL:
        size, n = int(np.prod(shape)), _small_rows(shape)
        vals[name] = packed[r:r + n].reshape(-1)[:size].reshape(shape)
        r += n
    return vals


def kernel(x, positions, ffn1_norm, ffn1_w_gate, ffn1_w_up, ffn1_w_down, mix_norm, w_in, pool_w, pool_scale, dn_conv_w, dn_a_log, dn_dt_bias, dn_out_norm, w_out, ffn2_norm, ffn2_w_gate, ffn2_w_up, ffn2_w_down, final_norm, loss_target, m_ffn1_norm, m_ffn1_w_gate, m_ffn1_w_up, m_ffn1_w_down, m_mix_norm, m_w_in, m_pool_w, m_pool_scale, m_dn_conv_w, m_dn_a_log, m_dn_dt_bias, m_dn_out_norm, m_w_out, m_ffn2_norm, m_ffn2_w_gate, m_ffn2_w_up, m_ffn2_w_down, m_final_norm, v_ffn1_norm, v_ffn1_w_gate, v_ffn1_w_up, v_ffn1_w_down, v_mix_norm, v_w_in, v_pool_w, v_pool_scale, v_dn_conv_w, v_dn_a_log, v_dn_dt_bias, v_dn_out_norm, v_w_out, v_ffn2_norm, v_ffn2_w_gate, v_ffn2_w_up, v_ffn2_w_down, v_final_norm):
    names = ["ffn1_norm", "ffn1_w_gate", "ffn1_w_up", "ffn1_w_down", "mix_norm", "w_in", "pool_w", "pool_scale", "dn_conv_w",
             "dn_a_log", "dn_dt_bias", "dn_out_norm", "w_out", "ffn2_norm", "ffn2_w_gate", "ffn2_w_up", "ffn2_w_down", "final_norm"]
    W = dict(zip(names, [ffn1_norm, ffn1_w_gate, ffn1_w_up, ffn1_w_down, mix_norm, w_in, pool_w, pool_scale, dn_conv_w,
                         dn_a_log, dn_dt_bias, dn_out_norm, w_out, ffn2_norm, ffn2_w_gate, ffn2_w_up, ffn2_w_down, final_norm]))
    M = dict(zip(names, [m_ffn1_norm, m_ffn1_w_gate, m_ffn1_w_up, m_ffn1_w_down, m_mix_norm, m_w_in, m_pool_w, m_pool_scale,
                         m_dn_conv_w, m_dn_a_log, m_dn_dt_bias, m_dn_out_norm, m_w_out, m_ffn2_norm, m_ffn2_w_gate, m_ffn2_w_up,
                         m_ffn2_w_down, m_final_norm]))
    V = dict(zip(names, [v_ffn1_norm, v_ffn1_w_gate, v_ffn1_w_up, v_ffn1_w_down, v_mix_norm, v_w_in, v_pool_w, v_pool_scale,
                         v_dn_conv_w, v_dn_a_log, v_dn_dt_bias, v_dn_out_norm, v_w_out, v_ffn2_norm, v_ffn2_w_gate, v_ffn2_w_up,
                         v_ffn2_w_down, v_final_norm]))
    chip = 2 * lax.axis_index("x") + lax.axis_index("y")

    ffn_names = [(f"ffn{f}_w_gate", f"ffn{f}_w_up", f"ffn{f}_w_down") for f in (1, 2)]
    tr = lambda t: jnp.swapaxes(t, -1, -2)
    def ffn_rows(l, which):
        g, u, dn = ffn_names[which]
        return [tr(W[g][l]), tr(W[u][l]), W[dn][l]]

    def second_half(l):
        return jnp.concatenate(ffn_rows(l, 1) + [jnp.pad(W["w_out"][l], ((0, FC - 256), (0, 0)))], axis=0).astype(bf16)

    ws = _Weights()
    first0, = _all_gather_weights([jnp.concatenate(ffn_rows(0, 0), axis=0).astype(bf16)])
    ws.ffn[(0, 0)] = (first0, 0)

    def landed_00(gathered):
        ws.ffn[(0, 1)], ws.wout[0] = (gathered[0], 0), (gathered[0], 3)
        ws.set_w_in(0, gathered[1])

    def landed_01(gathered):
        ws.ffn[(1, 0)] = (gathered[0], 0)
        ws.set_w_in(1, gathered[1])

    def landed_10(gathered):
        ws.ffn[(1, 1)], ws.wout[1] = (gathered[0], 0), (gathered[0], 3)

    ws.rides[(0, 0)] = ([second_half(0), W["w_in"][0].astype(bf16)], landed_00)
    ws.rides[(0, 1)] = ([jnp.concatenate(ffn_rows(1, 0), axis=0).astype(bf16), W["w_in"][1].astype(bf16)], landed_01)
    ws.rides[(1, 0)] = ([second_half(1)], landed_10)
    conv_all = _gather_small(_pad_rows(dn_conv_w.reshape(DEPTH * 4 * (CW // NCH) // 128, 128), 32))
    conv_full = jnp.concatenate([conv_all[2 * j, :DEPTH * 4 * (CW // NCH) // 128].reshape(DEPTH, 4, CW // NCH) for j in range(NCH)],
                                axis=-1)

    par = jnp.pad(jnp.stack([dn_a_log, dn_dt_bias], axis=1), ((0, 0), (0, 6), (4, 120)))
    wts = dict(ffn1_norm=ffn1_norm, mix_norm=mix_norm, ffn2_norm=ffn2_norm, final_norm=final_norm.reshape(1, D),
               pool_bd=jnp.stack([_block_diag(pool_w[l]) for l in range(DEPTH)]).astype(bf16),
               pool_scale=pool_scale, conv_w=jnp.pad(conv_full, ((0, 0), (0, 4), (0, 0))),
               dn_par=par, dn_out_norm=dn_out_norm)

    c_arr = lax.axis_index("c").astype(jnp.int32).reshape(1)

    def prep(parts):
        return [_sum_core_pair(p, g, c_arr) for p, g in zip(parts, _send_sibling_half(parts))]

    loss, dx, g_final, grads, pieces1, pieces0 = _device_step(x[0], positions.reshape(S, 1), loss_target[0], wts, ws, prep)
    last = [jnp.concatenate([grads[0][n] for n in ffn_names[0]], axis=1)]
    pieces_last = _scatter_to_chips(prep(last))
    full_b = _sum_chips(pieces_last[0], c_arr, None, 0, RB)
    full_b = _sum_chips(pieces0[0], c_arr, full_b, 3 * FC, RB)
    full_b = _sum_chips(pieces1[0], c_arr, full_b, 7 * FC, RB)
    full_c = _sum_chips(pieces0[1], c_arr, None, 0, RC)
    full_c = _sum_chips(pieces1[1], c_arr, full_c, D, RC)
    full_b, full_c = _join_halves([full_b, full_c], [[(0, 3 * FC), (3 * FC, 4 * FC), (7 * FC, 7 * FC)], [(0, D), (D, D)]])

    small = {"loss": loss[0, 0:1], "final_norm": g_final.reshape(D)}
    for n in ("ffn1_norm", "mix_norm", "ffn2_norm", "pool_scale", "dn_out_norm"):
        small[n] = jnp.stack([grads[l][n].reshape(-1) for l in range(DEPTH)])
    small["pool_w"] = jnp.stack([jnp.stack([grads[l]["pool_bd"][64 * g:64 * (g + 1), 64 * g:64 * (g + 1)] for g in range(4)])
                                 for l in range(DEPTH)])
    small["dn_conv_w"] = jnp.stack([grads[l]["conv_w"][0:4] for l in range(DEPTH)])
    small["dn_a_log"] = jnp.stack([grads[l]["dn_par"][0, 4:8] for l in range(DEPTH)])
    small["dn_dt_bias"] = jnp.stack([grads[l]["dn_par"][1, 4:8] for l in range(DEPTH)])
    packed = _pack_small(small)
    g_small = _sum_pieces(_gather_small(packed), f32, "sum_small")
    gs = _unpack_small(g_small)

    transposed = ("ffn1_w_gate", "ffn1_w_up", "ffn2_w_gate", "ffn2_w_up")
    where = {"ffn1_w_gate": (full_b, FC // 2, lambda l: 14 * l), "ffn1_w_up": (full_b, FC // 2, lambda l: 14 * l + 2),
             "ffn1_w_down": (full_b, FC // 2, lambda l: 14 * l + 4), "ffn2_w_gate": (full_b, FC // 2, lambda l: 14 * l + 6),
             "ffn2_w_up": (full_b, FC // 2, lambda l: 14 * l + 8), "ffn2_w_down": (full_b, FC // 2, lambda l: 14 * l + 10),
             "w_out": (full_b, 64, lambda l: (FC // 64) * (7 * l + 6)), "w_in": (full_c, D // 2, lambda l: 2 * l)}
    big_res = {}
    for n, (gblob, tile, first) in where.items():
        t = tr if n in transposed else (lambda a: a)
        big_res[n] = [t(r) for r in _adamw_rows(t(W[n]), t(M[n]), t(V[n]), gblob, tile, first, "adamw_" + n)]

    def small_of(T):
        d = {n: T[n] for n, _ in _SMALL if n not in ("loss", "dn_conv_w")}
        d["loss"] = jnp.zeros((1,), f32)
        d["dn_conv_w"] = _at_own_columns(T["dn_conv_w"], chip)
        return _pack_small(d)

    res_s = _adamw(small_of(W), g_small, small_of(M), small_of(V), "adamw_small")
    small_out = [_unpack_small(r) for r in res_s]

    def split_blobs(b, c):
        out = {}
        b7 = b.reshape(DEPTH, 7, FC, D)
        for k, n in enumerate(n for names3 in ffn_names for n in names3):
            out[n] = tr(b7[:, k]) if n in transposed else b7[:, k]
        out["w_out"] = b7[:, 6, :256]
        out["w_in"] = c.reshape(DEPTH, D, INC)
        return out

    def assemble(big, sm):
        out = []
        for n in names:
            if n in big:
                out.append(big[n])
            elif n == "dn_conv_w":
                out.append(_own_columns(sm[n], chip))
            else:
                out.append(sm[n])
        return out

    grad_list = assemble(split_blobs(full_b, full_c), gs)
    outs = [gs["loss"].reshape(()), dx.reshape(1, S, D)] + grad_list
    for k in range(3):
        outs += assemble({n: r[k] for n, r in big_res.items()}, small_out[k])
    return tuple(outs)
```

```python
import functools
import math

import jax
import jax.numpy as jnp
import numpy as np
from jax import lax
from jax.experimental import pallas as pl
from jax.experimental.pallas import tpu as pltpu

f32 = jnp.float32
bf16 = jnp.bfloat16
SDS = jax.ShapeDtypeStruct
MESH = pl.DeviceIdType.MESH

S = 4096
D = 1024
DEPTH = 2
FF = 2816
NCH = 4
FC = FF // NCH
INW = 3080
INC = INW // NCH
INP = 3200
ATT = 256
EH = 64
NBLK = 128
DNW = 512
DH = 128
CH = 64
NCHUNK = S // CH
EPS = 1e-6
ROPE_THETA = 10000.0
PATTERN_DIL = (1, 4, 16)
ADAM_LR, ADAM_B1, ADAM_B2, ADAM_EPS, ADAM_WD, ADAM_STEP = 0.001, 0.9, 0.999, 1e-08, 0.01, 10
VMEM_BYTES_V7X = 64 * 1024 * 1024
NEG = -1e30

TM = 512
TMF = 1024
RB, RC = 14 * FC, 2 * D


def _cp(vmem_mb=48, sem=None):
    kw = dict(vmem_limit_bytes=vmem_mb * 1024 * 1024)
    if sem is not None:
        kw["dimension_semantics"] = sem
    return pltpu.CompilerParams(**kw)


def _pc(*args, **kwargs):
    pin = lambda s: pltpu.HBM(s.shape, s.dtype) if isinstance(s, SDS) and jnp.issubdtype(s.dtype, jnp.floating) else s
    out = kwargs["out_shape"]
    kwargs["out_shape"] = [pin(s) for s in out] if isinstance(out, (list, tuple)) else pin(out)
    call = pl.pallas_call(*args, **kwargs)

    def run(*operands):
        pinned = [pltpu.with_memory_space_constraint(o, pltpu.HBM) if jnp.issubdtype(o.dtype, jnp.floating) else o
                  for o in operands]
        return call(*pinned)

    return run


def _dot(a, b):
    return jnp.dot(a, b, preferred_element_type=f32)


def _dot_nt(a, b):
    return lax.dot_general(a, b, (((1,), (1,)), ((), ())), preferred_element_type=f32)


def _dot_tn(a, b):
    return lax.dot_general(a, b, (((0,), (0,)), ((), ())), preferred_element_type=f32)


def _rms(x, w):
    r = lax.rsqrt(jnp.mean(x * x, axis=-1, keepdims=True) + EPS)
    return x * r * w, r


def _rms_bwd(x, w, r, dh):
    xhat = x * r
    dw = jnp.sum(dh * xhat, axis=0, keepdims=True)
    dxh = dh * w
    dx = r * (dxh - xhat * jnp.mean(dxh * xhat, axis=-1, keepdims=True))
    return dx, dw


def _ffn_fwd(x, nw, blob, k0, ride=None):
    kg, ku, kd = k0, k0 + 1, k0 + 2
    nr = 0 if ride is None else len(ride)
    ni = S // TMF

    def body(*refs):
        x_ref, nw_ref, wg_ref, wu_ref, wd_ref = refs[:5]
        ride_in = refs[5:5 + nr]
        o_ref = refs[5 + nr]
        ride_out = refs[6 + nr:6 + 2 * nr]
        h_scr, acc_scr = refs[6 + 2 * nr:8 + 2 * nr]
        sems = refs[8 + 2 * nr:]
        i = pl.program_id(0)
        j = pl.program_id(1)

        if nr:
            @pl.when(jnp.logical_and(i == 0, j == 0))
            def _():
                _gather_start(ride_in, ride_out, *sems)

        @pl.when(j == 0)
        def _():
            h, _ = _rms(x_ref[...], nw_ref[...])
            h_scr[...] = h.astype(bf16)
            acc_scr[...] = jnp.zeros_like(acc_scr)

        h = h_scr[...]
        g = _dot_nt(h, wg_ref[0])
        u = _dot_nt(h, wu_ref[0])
        a = (g * jax.nn.sigmoid(g) * u).astype(bf16)
        acc_scr[...] += _dot(a, wd_ref[0])

        @pl.when(j == NCH - 1)
        def _():
            o_ref[...] = x_ref[...] + 0.5 * acc_scr[...]

        if nr:
            @pl.when(jnp.logical_and(i == ni - 1, j == NCH - 1))
            def _():
                _gather_finish(ride_in, ride_out, *sems)

    wspec = lambda k: pl.BlockSpec((1, FC, D), lambda i, j: (j, k, 0))
    rides = [] if ride is None else list(ride)
    res = _pc(
        body, grid=(ni, NCH), name="ffn_fwd_ride" if nr else "ffn_fwd",
        in_specs=[pl.BlockSpec((TMF, D), lambda i, j: (i, 0)),
                  pl.BlockSpec((1, D), lambda i, j: (0, 0)),
                  wspec(kg), wspec(ku), wspec(kd)] + [ANY] * nr,
        out_specs=[pl.BlockSpec((TMF, D), lambda i, j: (i, 0))] + [ANY] * nr,
        out_shape=[SDS((S, D), f32)] + [SDS((NCH,) + r.shape, r.dtype) for r in rides],
        scratch_shapes=[pltpu.VMEM((TMF, D), bf16), pltpu.VMEM((TMF, D), f32)] + (_gather_sems(nr) if nr else []),
        compiler_params=_cp(56),
    )(x, nw, blob, blob, blob, *rides)
    return res if nr else res[0]


def _ffn_bwd(x, nw, blob, k0, dy, ride=None):
    nt = S // TM
    kg, ku, kd = k0, k0 + 1, k0 + 2
    nr = 0 if ride is None else len(ride)

    def body(*refs):
        x_ref, nw_ref, wg_ref, wu_ref, wd_ref, dy_ref = refs[:6]
        ride_in = refs[6:6 + nr]
        dwg_ref, dwu_ref, dwd_ref, dh_ref = refs[6 + nr:10 + nr]
        ride_out = refs[10 + nr:10 + 2 * nr]
        ag, au, ad = refs[10 + 2 * nr:13 + 2 * nr]
        sems = refs[13 + 2 * nr:]
        j = pl.program_id(0)
        i = pl.program_id(1)

        if nr:
            @pl.when(jnp.logical_and(j == 0, i == 0))
            def _():
                _scatter_start(ride_in, ride_out, *sems)

        @pl.when(i == 0)
        def _():
            ag[...] = jnp.zeros_like(ag)
            au[...] = jnp.zeros_like(au)
            ad[...] = jnp.zeros_like(ad)

        hf, _ = _rms(x_ref[...], nw_ref[...])
        h = hf.astype(bf16)
        g = _dot_nt(h, wg_ref[0])
        u = _dot_nt(h, wu_ref[0])
        sg = jax.nn.sigmoid(g)
        s = g * sg
        a = (s * u).astype(bf16)
        dyb = (0.5 * dy_ref[...]).astype(bf16)
        da = _dot_nt(dyb, wd_ref[0])
        ad[...] += _dot_tn(a, dyb)
        du = (da * s).astype(bf16)
        dg = (da * u * (sg * (1.0 + g * (1.0 - sg)))).astype(bf16)
        ag[...] += _dot_tn(dg, h)
        au[...] += _dot_tn(du, h)
        dh_ref[0] = (_dot(dg, wg_ref[0]) + _dot(du, wu_ref[0])).astype(bf16)

        @pl.when(i == nt - 1)
        def _():
            dwg_ref[0] = ag[...].astype(bf16)
            dwu_ref[0] = au[...].astype(bf16)
            dwd_ref[0] = ad[...].astype(bf16)

        if nr:
            @pl.when(jnp.logical_and(j == NCH - 1, i == nt - 1))
            def _():
                _scatter_finish(ride_in, ride_out, *sems)

    wspec = lambda k: pl.BlockSpec((1, FC, D), lambda j, i: (j, k, 0))
    gspec = pl.BlockSpec((1, FC, D), lambda j, i: (j, 0, 0))
    rides = [] if ride is None else list(ride)
    return _pc(
        body, grid=(NCH, nt), name="ffn_bwd_ride" if nr else "ffn_bwd",
        in_specs=[pl.BlockSpec((TM, D), lambda j, i: (i, 0)),
                  pl.BlockSpec((1, D), lambda j, i: (0, 0)),
                  wspec(kg), wspec(ku), wspec(kd),
                  pl.BlockSpec((TM, D), lambda j, i: (i, 0))] + [ANY] * nr,
        out_specs=[gspec, gspec, gspec, pl.BlockSpec((1, TM, D), lambda j, i: (j, i, 0))] + [ANY] * nr,
        out_shape=[SDS((NCH, FC, D), bf16)] * 3 + [SDS((NCH, S, D), bf16)] + [SDS(r.shape, r.dtype) for r in rides],
        scratch_shapes=[pltpu.VMEM((FC, D), f32)] * 3 + (_scatter_sems(nr) if nr else []),
        compiler_params=_cp(56),
    )(x, nw, blob, blob, blob, dy, *rides)


def _norm_bwd(x, nw, dres, dh4):
    nt = S // TM
    nparts = dh4.shape[0]

    def body(x_ref, nw_ref, dres_ref, dh_ref, dx_ref, dnw_ref):
        i = pl.program_id(0)
        dh = dh_ref[0].astype(f32)
        for p in range(1, nparts):
            dh = dh + dh_ref[p].astype(f32)
        xv = x_ref[...]
        _, r = _rms(xv, nw_ref[...])
        dx, dw = _rms_bwd(xv, nw_ref[...], r, dh)
        dx_ref[...] = dres_ref[...] + dx

        @pl.when(i == 0)
        def _():
            dnw_ref[...] = jnp.zeros_like(dnw_ref)

        dnw_ref[...] += dw

    return _pc(
        body, grid=(nt,), name="norm_bwd",
        in_specs=[pl.BlockSpec((TM, D), lambda i: (i, 0)),
                  pl.BlockSpec((1, D), lambda i: (0, 0)),
                  pl.BlockSpec((TM, D), lambda i: (i, 0)),
                  pl.BlockSpec((nparts, TM, D), lambda i: (0, i, 0))],
        out_specs=[pl.BlockSpec((TM, D), lambda i: (i, 0)), pl.BlockSpec((1, D), lambda i: (0, 0))],
        out_shape=[SDS((S, D), f32), SDS((1, D), f32)],
        compiler_params=_cp(40),
    )(x, nw, dres, dh4)


def _final(x, nw, target):
    nt = S // TM

    def body(x_ref, nw_ref, t_ref, dx_ref, dnw_ref, loss_ref):
        i = pl.program_id(0)
        xv = x_ref[...]
        y, r = _rms(xv, nw_ref[...])
        err = y - t_ref[...]
        part = 0.5 * jnp.sum(jnp.mean(err * err, axis=-1, keepdims=True), axis=0, keepdims=True)
        dx, dw = _rms_bwd(xv, nw_ref[...], r, err * (1.0 / D))
        dx_ref[...] = dx

        @pl.when(i == 0)
        def _():
            dnw_ref[...] = jnp.zeros_like(dnw_ref)
            loss_ref[...] = jnp.zeros_like(loss_ref)

        dnw_ref[...] += dw
        loss_ref[...] += jnp.broadcast_to(part, loss_ref.shape)

    return _pc(
        body, grid=(nt,), name="final_loss",
        in_specs=[pl.BlockSpec((TM, D), lambda i: (i, 0)),
                  pl.BlockSpec((1, D), lambda i: (0, 0)),
                  pl.BlockSpec((TM, D), lambda i: (i, 0))],
        out_specs=[pl.BlockSpec((TM, D), lambda i: (i, 0)), pl.BlockSpec((1, D), lambda i: (0, 0)),
                   pl.BlockSpec((1, 128), lambda i: (0, 0))],
        out_shape=[SDS((S, D), f32), SDS((1, D), f32), SDS((1, 128), f32)],
        compiler_params=_cp(40),
    )(x, nw, target)


def _rot_half(t):
    lane = lax.broadcasted_iota(jnp.int32, t.shape, 1)
    first = (lane % EH) < (EH // 2)
    return jnp.where(first, -pltpu.roll(t, ATT - EH // 2, 1), pltpu.roll(t, EH // 2, 1))


def _rope_tables(pos_ref, freq_ref):
    ang = pos_ref[...].astype(f32) * freq_ref[...]
    return jnp.cos(ang), jnp.sin(ang)


def _split_residues(val, scr, outs):
    rows, cols = val.shape
    for j in range(cols // 128):
        scr[j] = val[:, 128 * j:128 * (j + 1)]
    for ref, d in outs:
        for j in range(cols // 128):
            for r in range(d):
                ref.at[r][:, 128 * j:128 * (j + 1)] = scr.at[j][pl.ds(r, rows // d, stride=d), :]


def _join_residues(ref, d, scr):
    rows, cols = scr.shape[1], ref.shape[2]
    for j in range(cols // 128):
        for r in range(d):
            scr.at[j][pl.ds(r, rows // d, stride=d), :] = ref.at[r][:, 128 * j:128 * (j + 1)]
    return jnp.concatenate([scr[j] for j in range(cols // 128)], axis=1)


def _res_spec(d, tile, cols):
    return pl.BlockSpec((d, tile // d, cols), lambda i: (0, i, 0))


def _inproj_fwd(x, nw, w_aug, pos, freq):
    TI = 256

    def body(x_ref, nw_ref, w_hbm, pos_ref, freq_ref, att_ref, att4_ref, att16_ref, pu_ref, dq_ref, dz_ref, dba_ref,
             w_scr, r_scr):
        @pl.when(pl.program_id(0) == 0)
        def _():
            pltpu.sync_copy(w_hbm, w_scr)

        h, _ = _rms(x_ref[...], nw_ref[...])
        proj = _dot(h.astype(bf16), w_scr[...])
        cos, sin = _rope_tables(pos_ref, freq_ref)
        q = proj[:, 0:ATT]
        k = proj[:, ATT:2 * ATT]
        att = jnp.concatenate([q * cos + _rot_half(q) * sin, k * cos + _rot_half(k) * sin, proj[:, 2 * ATT:3 * ATT]], axis=1)
        att_ref[...] = att
        _split_residues(att, r_scr, [(att4_ref, 4), (att16_ref, 16)])
        pu_ref[...] = proj[:, 768:1024]
        dq_ref[...] = proj[:, 1024:2560]
        dz_ref[...] = proj[:, 2560:3072]
        dba_ref[...] = proj[:, 3072:3200]

    return _pc(
        body, grid=(S // TI,), name="inproj_fwd",
        in_specs=[pl.BlockSpec((TI, D), lambda i: (i, 0)),
                  pl.BlockSpec((1, D), lambda i: (0, 0)),
                  pl.BlockSpec(memory_space=pl.ANY),
                  pl.BlockSpec((TI, 1), lambda i: (i, 0)),
                  pl.BlockSpec((1, ATT), lambda i: (0, 0))],
        out_specs=[pl.BlockSpec((TI, 768), lambda i: (i, 0)), _res_spec(4, TI, 768), _res_spec(16, TI, 768),
                   pl.BlockSpec((TI, 256), lambda i: (i, 0)),
                   pl.BlockSpec((TI, 1536), lambda i: (i, 0)), pl.BlockSpec((TI, 512), lambda i: (i, 0)),
                   pl.BlockSpec((TI, 128), lambda i: (i, 0))],
        out_shape=[SDS((S, 768), f32), SDS((4, S // 4, 768), f32), SDS((16, S // 16, 768), f32), SDS((S, 256), f32),
                   SDS((S, 1536), f32), SDS((S, 512), f32), SDS((S, 128), f32)],
        scratch_shapes=[pltpu.VMEM((D, INP), bf16), pltpu.VMEM((6, TI, 128), f32)],
        compiler_params=_cp(48),
    )(x, nw, w_aug, pos, freq)


def _inproj_bwd(x, nw, w_aug, pos, freq, dres, datt, datt4, datt16, dpu, ddq, ddz, ddba):
    TI = 256
    nt = S // TI

    def body(x_ref, nw_ref, w_hbm, pos_ref, freq_ref, dres_ref, datt_ref, datt4_ref, datt16_ref, dpu_ref, ddq_ref, ddz_ref,
             ddba_ref, dx_ref, dnw_ref, dw_hbm, w_scr, acc, r_scr):
        i = pl.program_id(0)

        @pl.when(i == 0)
        def _():
            pltpu.sync_copy(w_hbm, w_scr)
            acc[...] = jnp.zeros_like(acc)
            dnw_ref[...] = jnp.zeros_like(dnw_ref)

        xv = x_ref[...]
        hf, r = _rms(xv, nw_ref[...])
        h = hf.astype(bf16)
        cos, sin = _rope_tables(pos_ref, freq_ref)
        datt = datt_ref[...] + _join_residues(datt4_ref, 4, r_scr)
        datt = datt + _join_residues(datt16_ref, 16, r_scr)
        dq = datt[:, 0:ATT]
        dk = datt[:, ATT:2 * ATT]
        dq = dq * cos - _rot_half(dq) * sin
        dk = dk * cos - _rot_half(dk) * sin
        dproj = jnp.concatenate([dq, dk, datt[:, 2 * ATT:3 * ATT], dpu_ref[...], ddq_ref[...], ddz_ref[...], ddba_ref[...]],
                                axis=1).astype(bf16)
        acc[...] += _dot_tn(h, dproj)
        dh = _dot_nt(dproj, w_scr[...])
        dx, dw = _rms_bwd(xv, nw_ref[...], r, dh)
        dx_ref[...] = dres_ref[...] + dx
        dnw_ref[...] += dw

        @pl.when(i == nt - 1)
        def _():
            pltpu.sync_copy(acc, dw_hbm)

    return _pc(
        body, grid=(nt,), name="inproj_bwd",
        in_specs=[pl.BlockSpec((TI, D), lambda i: (i, 0)),
                  pl.BlockSpec((1, D), lambda i: (0, 0)),
                  pl.BlockSpec(memory_space=pl.ANY),
                  pl.BlockSpec((TI, 1), lambda i: (i, 0)),
                  pl.BlockSpec((1, ATT), lambda i: (0, 0)),
                  pl.BlockSpec((TI, D), lambda i: (i, 0)),
                  pl.BlockSpec((TI, 768), lambda i: (i, 0)), _res_spec(4, TI, 768), _res_spec(16, TI, 768),
                  pl.BlockSpec((TI, 256), lambda i: (i, 0)),
                  pl.BlockSpec((TI, 1536), lambda i: (i, 0)),
                  pl.BlockSpec((TI, 512), lambda i: (i, 0)),
                  pl.BlockSpec((TI, 128), lambda i: (i, 0))],
        out_specs=[pl.BlockSpec((TI, D), lambda i: (i, 0)), pl.BlockSpec((1, D), lambda i: (0, 0)),
                   pl.BlockSpec(memory_space=pl.ANY)],
        out_shape=[SDS((S, D), f32), SDS((1, D), f32), SDS((D, INP), f32)],
        scratch_shapes=[pltpu.VMEM((D, INP), bf16), pltpu.VMEM((D, INP), f32), pltpu.VMEM((6, TI, 128), f32)],
        compiler_params=_cp(56),
    )(x, nw, w_aug, pos, freq, dres, datt, datt4, datt16, dpu, ddq, ddz, ddba)


def _outproj_fwd(x, ya, yb, yc, blob_b, kw):
    def body(x_ref, ya_ref, yb_ref, yc_ref, w_ref, o_ref):
        ycat = jnp.concatenate([ya_ref[...], yb_ref[...], yc_ref[...]], axis=1).astype(bf16)
        o_ref[...] = x_ref[...] + _dot(ycat, w_ref[:, 0:256, :].reshape(D, D))

    return _pc(
        body, grid=(S // TM,), name="outproj_fwd",
        in_specs=[pl.BlockSpec((TM, D), lambda i: (i, 0)),
                  pl.BlockSpec((TM, 256), lambda i: (i, 0)),
                  pl.BlockSpec((TM, 256), lambda i: (i, 0)),
                  pl.BlockSpec((TM, 512), lambda i: (i, 0)),
                  pl.BlockSpec((NCH, FC, D), lambda i: (0, kw, 0))],
        out_specs=pl.BlockSpec((TM, D), lambda i: (i, 0)),
        out_shape=SDS((S, D), f32),
        compiler_params=_cp(40),
    )(x, ya, yb, yc, blob_b)


def _outproj_bwd(x, nw, dres, dh4, ya, yb, yc, blob_b, kw):
    nt = S // TM
    nparts = dh4.shape[0]

    def body(x_ref, nw_ref, dres_ref, dh_ref, ya_ref, yb_ref, yc_ref, w_ref, dx_ref, dnw_ref, dya_ref, dyb_ref, dyc_ref, dw_ref):
        i = pl.program_id(0)

        @pl.when(i == 0)
        def _():
            dw_ref[...] = jnp.zeros_like(dw_ref)
            dnw_ref[...] = jnp.zeros_like(dnw_ref)

        dh = dh_ref[0].astype(f32)
        for p in range(1, nparts):
            dh = dh + dh_ref[p].astype(f32)
        xv = x_ref[...]
        _, r = _rms(xv, nw_ref[...])
        dxn, dnw = _rms_bwd(xv, nw_ref[...], r, dh)
        dx = dres_ref[...] + dxn
        dx_ref[...] = dx
        dnw_ref[...] += dnw
        dyv = dx.astype(bf16)
        ycat = jnp.concatenate([ya_ref[...], yb_ref[...], yc_ref[...]], axis=1).astype(bf16)
        dw_ref[...] += _dot_tn(ycat, dyv)
        dcat = _dot_nt(dyv, w_ref[:, 0:256, :].reshape(D, D))
        dya_ref[...] = dcat[:, 0:256]
        dyb_ref[...] = dcat[:, 256:512]
        dyc_ref[...] = dcat[:, 512:1024]

    return _pc(
        body, grid=(nt,), name="outproj_bwd",
        in_specs=[pl.BlockSpec((TM, D), lambda i: (i, 0)),
                  pl.BlockSpec((1, D), lambda i: (0, 0)),
                  pl.BlockSpec((TM, D), lambda i: (i, 0)),
                  pl.BlockSpec((nparts, TM, D), lambda i: (0, i, 0)),
                  pl.BlockSpec((TM, 256), lambda i: (i, 0)),
                  pl.BlockSpec((TM, 256), lambda i: (i, 0)),
                  pl.BlockSpec((TM, 512), lambda i: (i, 0)),
                  pl.BlockSpec((NCH, FC, D), lambda i: (0, kw, 0))],
        out_specs=[pl.BlockSpec((TM, D), lambda i: (i, 0)), pl.BlockSpec((1, D), lambda i: (0, 0)),
                   pl.BlockSpec((TM, 256), lambda i: (i, 0)), pl.BlockSpec((TM, 256), lambda i: (i, 0)),
                   pl.BlockSpec((TM, 512), lambda i: (i, 0)), pl.BlockSpec((D, D), lambda i: (0, 0))],
        out_shape=[SDS((S, D), f32), SDS((1, D), f32), SDS((S, 256), f32), SDS((S, 256), f32), SDS((S, 512), f32),
                   SDS((D, D), f32)],
        compiler_params=_cp(48),
    )(x, nw, dres, dh4, ya, yb, yc, blob_b)


QT = NBLK
NB = S // QT


def _attn_block(q, kp, kc, vp, vc, first):
    kk = jnp.concatenate([kp, kc], axis=0).astype(bf16)
    vv = jnp.concatenate([vp, vc], axis=0).astype(bf16)
    qi = lax.broadcasted_iota(jnp.int32, (4 * QT, NBLK + QT), 0) % QT
    ki = lax.broadcasted_iota(jnp.int32, (4 * QT, NBLK + QT), 1)
    dist = NBLK + qi - ki
    valid = (dist >= 0) & (dist <= NBLK) & (jnp.logical_not(first) | (ki >= NBLK))
    head = lax.broadcasted_iota(jnp.int32, (1, ATT), 1) // EH
    masks = [(head == h).astype(f32) for h in range(4)]
    qs = jnp.concatenate([q * (mh * (1.0 / math.sqrt(EH))) for mh in masks], axis=0).astype(bf16)
    s = _dot_nt(qs, kk)
    s = jnp.where(valid, s, NEG)
    m = lax.stop_gradient(jnp.max(s, axis=-1, keepdims=True))
    p = jnp.exp(s - m)
    den = jnp.sum(p, axis=-1, keepdims=True)
    po = _dot((p * (1.0 / den)).astype(bf16), vv)
    lse = m + jnp.log(den)
    o = jnp.zeros((QT, ATT), f32)
    l = jnp.zeros((QT, ATT), f32)
    for h, mh in enumerate(masks):
        o = o + po[QT * h:QT * (h + 1)] * mh
        l = l + lse[QT * h:QT * (h + 1)] * mh
    return o, l


def _attn_specs(tile):
    own = lambda col: pl.BlockSpec((QT, ATT), lambda s: (tile(s), col))
    prev = lambda col: pl.BlockSpec((NBLK, ATT), lambda s: (jnp.maximum((QT // NBLK) * tile(s) - 1, 0), col))
    return [own(0), prev(1), own(1), prev(2), own(2)]


def _attn_fwd(qkv, per_seq):
    def body(q_ref, kp_ref, kc_ref, vp_ref, vc_ref, o_ref, l_ref):
        first = pl.program_id(0) % per_seq == 0
        o, l = _attn_block(q_ref[...], kp_ref[...], kc_ref[...], vp_ref[...], vc_ref[...], first)
        o_ref[...] = o
        l_ref[...] = l

    blk = pl.BlockSpec((QT, ATT), lambda t: (t, 0))
    return _pc(
        body, grid=(NB,), name="attn_fwd", in_specs=_attn_specs(lambda t: t), out_specs=[blk, blk],
        out_shape=[SDS((S, ATT), f32), SDS((S, ATT), f32)], compiler_params=_cp(32),
    )(qkv, qkv, qkv, qkv, qkv)


def _attn_block_bwd(q, kp, kc, vp, vc, o, l, do, dl, first):
    kk = jnp.concatenate([kp, kc], axis=0).astype(bf16)
    vv = jnp.concatenate([vp, vc], axis=0).astype(bf16)
    qi = lax.broadcasted_iota(jnp.int32, (4 * QT, NBLK + QT), 0) % QT
    ki = lax.broadcasted_iota(jnp.int32, (4 * QT, NBLK + QT), 1)
    dist = NBLK + qi - ki
    valid = (dist >= 0) & (dist <= NBLK) & (jnp.logical_not(first) | (ki >= NBLK))
    head = lax.broadcasted_iota(jnp.int32, (1, ATT), 1) // EH
    masks = [(head == h).astype(f32) for h in range(4)]
    scale = 1.0 / math.sqrt(EH)
    stack = lambda f: jnp.concatenate([f(mh) for mh in masks], axis=0)
    qs = stack(lambda mh: q * (mh * scale)).astype(bf16)
    s = jnp.where(valid, _dot_nt(qs, kk), NEG)
    lse = stack(lambda mh: jnp.max(jnp.where(mh > 0.0, l, NEG), axis=1, keepdims=True))
    p = jnp.exp(s - lse)
    dos = stack(lambda mh: do * mh).astype(bf16)
    dvv = _dot_tn(p.astype(bf16), dos)
    dp = _dot_nt(dos, vv)
    delta = stack(lambda mh: jnp.sum(do * o * mh, axis=1, keepdims=True))
    dlse = stack(lambda mh: jnp.sum(dl * mh, axis=1, keepdims=True))
    ds = (p * (dp - delta + dlse)).astype(bf16)
    dqs = _dot(ds, kk)
    dq = jnp.zeros((QT, ATT), f32)
    for h, mh in enumerate(masks):
        dq = dq + dqs[QT * h:QT * (h + 1)] * (mh * scale)
    dkk = _dot_tn(ds, qs)
    return dq, dkk[:NBLK], dkk[NBLK:], dvv[:NBLK], dvv[NBLK:]


def _attn_bwd(qkv, o, l, do, dl, per_seq):
    def body(q_ref, kp_ref, kc_ref, vp_ref, vc_ref, ofw_ref, lfw_ref, do_ref, dl_ref, o_ref, k_carry, v_carry):
        step = pl.program_id(0)

        @pl.when(step == 0)
        def _():
            k_carry[...] = jnp.zeros_like(k_carry)
            v_carry[...] = jnp.zeros_like(v_carry)

        t = NB - 1 - step
        first = t % per_seq == 0
        last = t % per_seq == per_seq - 1
        dq, dkp, dkc, dvp, dvc = _attn_block_bwd(q_ref[...], kp_ref[...], kc_ref[...], vp_ref[...], vc_ref[...], ofw_ref[...],
                                                 lfw_ref[...], do_ref[...], dl_ref[...], first)
        o_ref[:, 0:ATT] = dq
        o_ref[:, ATT:2 * ATT] = dkc
        o_ref[:, 2 * ATT:3 * ATT] = dvc
        o_ref[QT - NBLK:QT, ATT:2 * ATT] += jnp.where(last, 0.0, k_carry[...])
        o_ref[QT - NBLK:QT, 2 * ATT:3 * ATT] += jnp.where(last, 0.0, v_carry[...])
        k_carry[...] = dkp
        v_carry[...] = dvp

    rev = lambda s: NB - 1 - s
    blk = pl.BlockSpec((QT, ATT), lambda s: (rev(s), 0))
    return _pc(
        body, grid=(NB,), name="attn_bwd", in_specs=_attn_specs(rev) + [blk, blk, blk, blk],
        out_specs=pl.BlockSpec((QT, 768), lambda s: (rev(s), 0)),
        out_shape=SDS((S, 768), f32), scratch_shapes=[pltpu.VMEM((NBLK, ATT), f32)] * 2, compiler_params=_cp(40),
    )(qkv, qkv, qkv, qkv, qkv, o, l, do, dl)


def _merge_weights(l0, l1, l2):
    m = jnp.maximum(jnp.maximum(l0, l1), l2)
    e0, e1, e2 = jnp.exp(l0 - m), jnp.exp(l1 - m), jnp.exp(l2 - m)
    tot = e0 + e1 + e2
    return e0 / tot, e1 / tot, e2 / tot


def _merge_specs():
    nat = pl.BlockSpec((TM, ATT), lambda i: (i, 0))
    return nat, _res_spec(4, TM, ATT), _res_spec(16, TM, ATT)


def _merge_fwd(o1, l1, o4, l4, o16, l16):
    def body(o1_ref, l1_ref, o4_ref, l4_ref, o16_ref, l16_ref, y_ref, scr):
        o4v, l4v = _join_residues(o4_ref, 4, scr), _join_residues(l4_ref, 4, scr)
        o16v, l16v = _join_residues(o16_ref, 16, scr), _join_residues(l16_ref, 16, scr)
        w0, w1, w2 = _merge_weights(l1_ref[...], l4v, l16v)
        y_ref[...] = w0 * o1_ref[...] + w1 * o4v + w2 * o16v

    nat, r4, r16 = _merge_specs()
    return _pc(body, grid=(S // TM,), name="merge_fwd", in_specs=[nat, nat, r4, r4, r16, r16],
                          out_specs=nat, out_shape=SDS((S, ATT), f32), scratch_shapes=[pltpu.VMEM((2, TM, 128), f32)],
                          compiler_params=_cp(32))(o1, l1, o4, l4, o16, l16)


def _merge_bwd(o1, l1, o4, l4, o16, l16, dy):
    def body(o1_ref, l1_ref, o4_ref, l4_ref, o16_ref, l16_ref, dy_ref, do1_ref, dl1_ref, do4_ref, dl4_ref, do16_ref, dl16_ref, scr):
        o4v, l4v = _join_residues(o4_ref, 4, scr), _join_residues(l4_ref, 4, scr)
        o16v, l16v = _join_residues(o16_ref, 16, scr), _join_residues(l16_ref, 16, scr)
        o1v = o1_ref[...]
        w0, w1, w2 = _merge_weights(l1_ref[...], l4v, l16v)
        y = w0 * o1v + w1 * o4v + w2 * o16v
        dyv = dy_ref[...]
        do1_ref[...] = w0 * dyv
        dl1_ref[...] = w0 * (o1v - y) * dyv
        _split_residues(w1 * dyv, scr, [(do4_ref, 4)])
        _split_residues(w1 * (o4v - y) * dyv, scr, [(dl4_ref, 4)])
        _split_residues(w2 * dyv, scr, [(do16_ref, 16)])
        _split_residues(w2 * (o16v - y) * dyv, scr, [(dl16_ref, 16)])

    nat, r4, r16 = _merge_specs()
    return _pc(body, grid=(S // TM,), name="merge_bwd", in_specs=[nat, nat, r4, r4, r16, r16, nat],
                          out_specs=[nat, nat, r4, r4, r16, r16],
                          out_shape=[SDS((S, ATT), f32)] * 2 + [SDS((4, S // 4, ATT), f32)] * 2 + [SDS((16, S // 16, ATT), f32)] * 2,
                          scratch_shapes=[pltpu.VMEM((2, TM, 128), f32)], compiler_params=_cp(32))(o1, l1, o4, l4, o16, l16, dy)


HALO = 16


def _pool_consts(i, rows):
    grp = lax.broadcasted_iota(jnp.int32, (rows, 256), 1) // 64
    t = i * TM + lax.broadcasted_iota(jnp.int32, (rows, 256), 0)
    win = jnp.where(grp == 0, 2, jnp.where(grp == 1, 4, jnp.where(grp == 2, 8, 16)))
    cnt = jnp.minimum(t + 1, win).astype(f32)
    return grp, cnt


def _pool_select(grp, s2, s4, s8, s16):
    return jnp.where(grp == 0, s2, jnp.where(grp == 1, s4, jnp.where(grp == 2, s8, s16)))


def _pooled(i, cur, halo):
    xx = jnp.concatenate([halo, cur], axis=0)
    s2 = xx + pltpu.roll(xx, 1, 0)
    s4 = s2 + pltpu.roll(s2, 2, 0)
    s8 = s4 + pltpu.roll(s4, 4, 0)
    s16 = s8 + pltpu.roll(s8, 8, 0)
    grp, cnt = _pool_consts(i, TM)
    tot = _pool_select(grp, s2[HALO:], s4[HALO:], s8[HALO:], s16[HALO:])
    return tot / cnt - cur


def _pool_fwd(u, wp, scale):
    def body(u_ref, halo_ref, wp_ref, sc_ref, y_ref):
        i = pl.program_id(0)
        halo = halo_ref[...] * (i > 0).astype(f32)
        pooled = _pooled(i, u_ref[...], halo)
        y_ref[...] = _dot(pooled.astype(bf16), wp_ref[...]) * sc_ref[...]

    return _pc(
        body, grid=(S // TM,), name="pool_fwd",
        in_specs=[pl.BlockSpec((TM, 256), lambda i: (i, 0)),
                  pl.BlockSpec((HALO, 256), lambda i: (jnp.maximum(i * (TM // HALO) - 1, 0), 0)),
                  pl.BlockSpec((256, 256), lambda i: (0, 0)),
                  pl.BlockSpec((1, 256), lambda i: (0, 0))],
        out_specs=pl.BlockSpec((TM, 256), lambda i: (i, 0)), out_shape=SDS((S, 256), f32), compiler_params=_cp(32),
    )(u, u, wp, scale)


def _pool_bwd(u, wp, scale, dy):
    nt = S // TM

    def body(u_ref, halo_ref, wp_ref, sc_ref, dy_ref, dyn_ref, du_ref, dwp_ref, dsc_ref):
        i = pl.program_id(0)

        @pl.when(i == 0)
        def _():
            dwp_ref[...] = jnp.zeros_like(dwp_ref)
            dsc_ref[...] = jnp.zeros_like(dsc_ref)

        halo = halo_ref[...] * (i > 0).astype(f32)
        pooled = _pooled(i, u_ref[...], halo).astype(bf16)
        dyv = dy_ref[...]
        dsc_ref[...] += jnp.sum(dyv * _dot(pooled, wp_ref[...]), axis=0, keepdims=True)
        dys = (dyv * sc_ref[...]).astype(bf16)
        dwp_ref[...] += _dot_tn(pooled, dys)
        dpool = _dot_nt(dys, wp_ref[...])
        grp, cnt = _pool_consts(i, TM)
        dyn = ((dyn_ref[...] * (i < nt - 1).astype(f32)) * sc_ref[...]).astype(bf16)
        _, cntn = _pool_consts(i + 1, HALO)
        zn = _dot_nt(dyn, wp_ref[...]) / cntn
        zz = jnp.concatenate([dpool / cnt, zn], axis=0)
        n = TM + HALO
        a2 = zz + pltpu.roll(zz, n - 1, 0)
        a4 = a2 + pltpu.roll(a2, n - 2, 0)
        a8 = a4 + pltpu.roll(a4, n - 4, 0)
        a16 = a8 + pltpu.roll(a8, n - 8, 0)
        du_ref[...] = _pool_select(grp, a2[:TM], a4[:TM], a8[:TM], a16[:TM]) - dpool

    return _pc(
        body, grid=(nt,), name="pool_bwd",
        in_specs=[pl.BlockSpec((TM, 256), lambda i: (i, 0)),
                  pl.BlockSpec((HALO, 256), lambda i: (jnp.maximum(i * (TM // HALO) - 1, 0), 0)),
                  pl.BlockSpec((256, 256), lambda i: (0, 0)),
                  pl.BlockSpec((1, 256), lambda i: (0, 0)),
                  pl.BlockSpec((TM, 256), lambda i: (i, 0)),
                  pl.BlockSpec((HALO, 256), lambda i: (jnp.minimum((i + 1) * (TM // HALO), S // HALO - 1), 0))],
        out_specs=[pl.BlockSpec((TM, 256), lambda i: (i, 0)), pl.BlockSpec((256, 256), lambda i: (0, 0)),
                   pl.BlockSpec((1, 256), lambda i: (0, 0))],
        out_shape=[SDS((S, 256), f32), SDS((256, 256), f32), SDS((1, 256), f32)], compiler_params=_cp(32),
    )(u, u, wp, scale, dy, dy)


CW = 3 * DNW
CHALO = 8
TC = 256


def _conv_fwd(u, w):
    def body(u_ref, halo_ref, w_ref, c_ref):
        i = pl.program_id(0)
        xx = jnp.concatenate([halo_ref[...] * (i > 0).astype(f32), u_ref[...]], axis=0)
        c = (w_ref[3:4, :] * xx + w_ref[2:3, :] * pltpu.roll(xx, 1, 0) + w_ref[1:2, :] * pltpu.roll(xx, 2, 0)
             + w_ref[0:1, :] * pltpu.roll(xx, 3, 0))
        c_ref[...] = c[CHALO:]

    return _pc(
        body, grid=(S // TC,), name="conv_fwd",
        in_specs=[pl.BlockSpec((TC, CW), lambda i: (i, 0)),
                  pl.BlockSpec((CHALO, CW), lambda i: (jnp.maximum(i * (TC // CHALO) - 1, 0), 0)),
                  pl.BlockSpec((8, CW), lambda i: (0, 0))],
        out_specs=pl.BlockSpec((TC, CW), lambda i: (i, 0)), out_shape=SDS((S, CW), f32), compiler_params=_cp(32),
    )(u, u, w)


def _conv_bwd(u, w, dc):
    nt = S // TC

    def body(u_ref, halo_ref, w_ref, dc_ref, dcn_ref, du_ref, dw_ref):
        i = pl.program_id(0)

        @pl.when(i == 0)
        def _():
            dw_ref[...] = jnp.zeros_like(dw_ref)

        dcv = dc_ref[...]
        zz = jnp.concatenate([dcv, dcn_ref[...] * (i < nt - 1).astype(f32)], axis=0)
        n = TC + CHALO
        du = (w_ref[3:4, :] * zz + w_ref[2:3, :] * pltpu.roll(zz, n - 1, 0) + w_ref[1:2, :] * pltpu.roll(zz, n - 2, 0)
              + w_ref[0:1, :] * pltpu.roll(zz, n - 3, 0))
        du_ref[...] = du[:TC]
        xx = jnp.concatenate([halo_ref[...] * (i > 0).astype(f32), u_ref[...]], axis=0)
        for j in range(4):
            shifted = xx if j == 3 else pltpu.roll(xx, 3 - j, 0)
            dw_ref[j:j + 1, :] += jnp.sum(dcv * shifted[CHALO:], axis=0, keepdims=True)

    return _pc(
        body, grid=(nt,), name="conv_bwd",
        in_specs=[pl.BlockSpec((TC, CW), lambda i: (i, 0)),
                  pl.BlockSpec((CHALO, CW), lambda i: (jnp.maximum(i * (TC // CHALO) - 1, 0), 0)),
                  pl.BlockSpec((8, CW), lambda i: (0, 0)),
                  pl.BlockSpec((TC, CW), lambda i: (i, 0)),
                  pl.BlockSpec((CHALO, CW), lambda i: (jnp.minimum((i + 1) * (TC // CHALO), S // CHALO - 1), 0))],
        out_specs=[pl.BlockSpec((TC, CW), lambda i: (i, 0)), pl.BlockSpec((8, CW), lambda i: (0, 0))],
        out_shape=[SDS((S, CW), f32), SDS((8, CW), f32)], compiler_params=_cp(32),
    )(u, u, w, dc, dc)


TL = 512
NCL = TL // CH


def _ein(spec, a, b):
    return jnp.einsum(spec, a.astype(bf16), b.astype(bf16), preferred_element_type=f32)


def _ein_ct(spec, x, y, ct_first):
    ct = x if ct_first else y
    hi = ct.astype(bf16)
    lo = ct - hi.astype(f32)
    if ct_first:
        return _ein(spec, hi, y) + _ein(spec, lo, y)
    return _ein(spec, x, hi) + _ein(spec, x, lo)


def _bf16_dot(spec, grad_a, grad_b):
    @jax.custom_vjp
    def dot(a, b):
        return _ein(spec, a, b)

    def fwd(a, b):
        return _ein(spec, a, b), (a, b)

    def bwd(res, ct):
        a, b = res
        return grad_a(a, b, ct), grad_b(a, b, ct)

    dot.defvjp(fwd, bwd)
    return dot


def _bdot(a, b):
    return _ein('nik,nkj->nij', a, b)


def _bdot_nt(a, b):
    return _ein('nik,njk->nij', a, b)


def _bdot_tn(a, b):
    return _ein('nki,nkj->nij', a, b)


_mm = _bf16_dot('ik,kj->ij', lambda a, b, ct: _ein_ct('ij,kj->ik', ct, b, True), lambda a, b, ct: _ein_ct('ik,ij->kj', a, ct, False))
_mm_tn = _bf16_dot('ki,kj->ij', lambda a, b, ct: _ein_ct('kj,ij->ki', b, ct, False),
                   lambda a, b, ct: _ein_ct('ki,ij->kj', a, ct, False))


@jax.custom_vjp
def _inv_unit_lower(a):
    ii = lax.broadcasted_iota(jnp.int32, (1, CH, CH), 1)
    jj = lax.broadcasted_iota(jnp.int32, (1, CH, CH), 2)
    t = (ii == jj).astype(f32) - a
    p = a
    for _ in range(5):
        p = _bdot(p, p)
        t = t + _bdot(t, p)
    return t


def _inv_unit_lower_fwd(a):
    t = _inv_unit_lower(a)
    return t, t


def _inv_unit_lower_bwd(t, dt):
    return (-_bdot_tn(t, _bdot_nt(dt, t)),)


_inv_unit_lower.defvjp(_inv_unit_lower_fwd, _inv_unit_lower_bwd)


def _dn_local(c, dba, a_row, b_row):
    act = c * jax.nn.sigmoid(c)
    lane = lax.broadcasted_iota(jnp.int32, (TL, 128), 1)
    beta_all = jax.nn.sigmoid(dba)
    xs = dba + b_row
    softplus = jnp.maximum(xs, 0.0) + jnp.log(1.0 + jnp.exp(-jnp.abs(xs)))
    g_all = -jnp.exp(a_row) * softplus
    ii = lax.broadcasted_iota(jnp.int32, (1, CH, CH), 1)
    jj = lax.broadcasted_iota(jnp.int32, (1, CH, CH), 2)
    lower = jj <= ii
    strict = jj < ii
    eye = (ii == jj).astype(f32)
    us, ws, qgs, kds, intras = [], [], [], [], []
    aux = jnp.zeros((TL, 128), f32)
    for h in range(4):
        q = act[:, DH * h:DH * (h + 1)]
        k = act[:, DNW + DH * h:DNW + DH * (h + 1)]
        v = act[:, 2 * DNW + DH * h:2 * DNW + DH * (h + 1)]
        q = q * lax.rsqrt(jnp.sum(q * q, axis=-1, keepdims=True) + EPS) * (DH ** -0.5)
        k = k * lax.rsqrt(jnp.sum(k * k, axis=-1, keepdims=True) + EPS)
        beta = jnp.sum(jnp.where(lane == h, beta_all, 0.0), axis=1, keepdims=True)
        g = jnp.sum(jnp.where(lane == 4 + h, g_all, 0.0), axis=1, keepdims=True)
        q3, k3, v3 = q.reshape(NCL, CH, DH), k.reshape(NCL, CH, DH), v.reshape(NCL, CH, DH)
        beta3, g3 = beta.reshape(NCL, CH, 1), g.reshape(NCL, CH, 1)
        g_row = jnp.sum(eye * g3, axis=1, keepdims=True)
        gc_col = jnp.sum(jnp.where(lower, g_row, 0.0), axis=2, keepdims=True)
        gc_row = jnp.sum(jnp.where(ii <= jj, g3, 0.0), axis=1, keepdims=True)
        diff = gc_col - gc_row
        decay = jnp.where(lower, jnp.exp(jnp.where(lower, diff, 0.0)), 0.0)
        kb = k3 * beta3
        vb = v3 * beta3
        a = jnp.where(strict, _bdot_nt(kb, k3) * decay, 0.0)
        t = _inv_unit_lower(a)
        u3 = _bdot(t, vb)
        w3 = _bdot(t, kb * jnp.exp(gc_col))
        intra = jnp.where(lower, _bdot_nt(q3, k3) * decay, 0.0)
        g_last = jnp.sum(g3, axis=1, keepdims=True)
        us.append(u3.reshape(TL, DH))
        ws.append(w3.reshape(TL, DH))
        qgs.append((q3 * jnp.exp(gc_col)).reshape(TL, DH))
        kds.append((k3 * jnp.exp(g_last - gc_col)).reshape(TL, DH))
        intras.append(intra.reshape(TL, CH))
        e_last = jnp.broadcast_to(jnp.exp(g_last), (NCL, CH, 1)).reshape(TL, 1)
        aux = aux + jnp.where(lane == h, e_last, 0.0)
    cat = lambda xs: jnp.concatenate(xs, axis=1)
    return cat(us), cat(ws), cat(qgs), cat(kds), jnp.stack(intras, axis=0), aux


def _dn_local_fwd(c, dba, par):
    def body(c_ref, dba_ref, par_ref, u_ref, w_ref, qg_ref, kd_ref, in_ref, aux_ref):
        u, w, qg, kd, intra, aux = _dn_local(c_ref[...], dba_ref[...], par_ref[0:1, :], par_ref[1:2, :])
        u_ref[...] = u
        w_ref[...] = w
        qg_ref[...] = qg
        kd_ref[...] = kd
        in_ref[...] = intra
        aux_ref[...] = aux

    wide = pl.BlockSpec((TL, DNW), lambda i: (i, 0))
    return _pc(
        body, grid=(S // TL,), name="dn_local_fwd",
        in_specs=[pl.BlockSpec((TL, CW), lambda i: (i, 0)), pl.BlockSpec((TL, 128), lambda i: (i, 0)),
                  pl.BlockSpec((8, 128), lambda i: (0, 0))],
        out_specs=[wide, wide, wide, wide, pl.BlockSpec((4, TL, CH), lambda i: (0, i, 0)),
                   pl.BlockSpec((TL, 128), lambda i: (i, 0))],
        out_shape=[SDS((S, DNW), f32)] * 4 + [SDS((4, S, CH), f32), SDS((S, 128), f32)], compiler_params=_cp(48),
    )(c, dba, par)


def _dn_local_bwd(c, dba, par, du, dw, dqg, dkd, dintra, daux):
    def body(c_ref, dba_ref, par_ref, du_ref, dw_ref, dqg_ref, dkd_ref, din_ref, daux_ref, dc_ref, ddba_ref, dpar_ref):
        @pl.when(pl.program_id(0) == 0)
        def _():
            dpar_ref[...] = jnp.zeros_like(dpar_ref)

        _, vjp = jax.vjp(_dn_local, c_ref[...], dba_ref[...], par_ref[0:1, :], par_ref[1:2, :])
        dc, ddba, da_row, db_row = vjp((du_ref[...], dw_ref[...], dqg_ref[...], dkd_ref[...], din_ref[...], daux_ref[...]))
        dc_ref[...] = dc
        ddba_ref[...] = ddba
        dpar_ref[0:1, :] += da_row
        dpar_ref[1:2, :] += db_row

    wide = pl.BlockSpec((TL, DNW), lambda i: (i, 0))
    return _pc(
        body, grid=(S // TL,), name="dn_local_bwd",
        in_specs=[pl.BlockSpec((TL, CW), lambda i: (i, 0)), pl.BlockSpec((TL, 128), lambda i: (i, 0)),
                  pl.BlockSpec((8, 128), lambda i: (0, 0)), wide, wide, wide, wide,
                  pl.BlockSpec((4, TL, CH), lambda i: (0, i, 0)), pl.BlockSpec((TL, 128), lambda i: (i, 0))],
        out_specs=[pl.BlockSpec((TL, CW), lambda i: (i, 0)), pl.BlockSpec((TL, 128), lambda i: (i, 0)),
                   pl.BlockSpec((8, 128), lambda i: (0, 0))],
        out_shape=[SDS((S, CW), f32), SDS((S, 128), f32), SDS((8, 128), f32)], compiler_params=_cp(56),
    )(c, dba, par, du, dw, dqg, dkd, dintra, daux)


def _dn_step(state, u, w, qg, kd, intra, aux):
    lane = lax.broadcasted_iota(jnp.int32, (CH, 128), 1)
    row = lax.broadcasted_iota(jnp.int32, (CH, 128), 0)
    outs, states = [], []
    for h in range(4):
        sl = slice(DH * h, DH * (h + 1))
        st = state[h]
        e = jnp.sum(jnp.sum(jnp.where((lane == h) & (row == 0), aux, 0.0), axis=1, keepdims=True), axis=0, keepdims=True)
        v_new = u[:, sl] - _mm(w[:, sl], st)
        outs.append(_mm(qg[:, sl], st) + _mm(intra[h], v_new))
        states.append(st * e + _mm_tn(kd[:, sl], v_new))
    return jnp.concatenate(outs, axis=1), jnp.stack(states, axis=0)


CPS = 8
NSTEP = NCHUNK // CPS


def _dn_rec_specs(index):
    wide = pl.BlockSpec((CPS * CH, DNW), lambda n: (index(n), 0))
    inb = pl.BlockSpec((4, CPS * CH, CH), lambda n: (0, index(n), 0))
    auxb = pl.BlockSpec((CPS * CH, 128), lambda n: (index(n), 0))
    stb = pl.BlockSpec((CPS, 4, DH, DH), lambda n: (index(n), 0, 0, 0))
    return wide, inb, auxb, stb


def _dn_rec_fwd(u, w, qg, kd, intra, aux):
    def body(u_ref, w_ref, qg_ref, kd_ref, in_ref, aux_ref, o_ref, st_ref, st_scr):
        @pl.when(pl.program_id(0) == 0)
        def _():
            st_scr[...] = jnp.zeros_like(st_scr)

        st = st_scr[...]
        for k in range(CPS):
            rows = slice(CH * k, CH * (k + 1))
            st_ref[k] = st
            o, st = _dn_step(st, u_ref[rows, :], w_ref[rows, :], qg_ref[rows, :], kd_ref[rows, :], in_ref[:, rows, :],
                             aux_ref[rows, :])
            o_ref[rows, :] = o
        st_scr[...] = st

    wide, inb, auxb, stb = _dn_rec_specs(lambda n: n)
    return _pc(
        body, grid=(NSTEP,), name="dn_rec_fwd", in_specs=[wide, wide, wide, wide, inb, auxb], out_specs=[wide, stb],
        out_shape=[SDS((S, DNW), f32), SDS((NCHUNK, 4, DH, DH), f32)],
        scratch_shapes=[pltpu.VMEM((4, DH, DH), f32)], compiler_params=_cp(32),
    )(u, w, qg, kd, intra, aux)


def _dn_rec_bwd(u, w, qg, kd, intra, aux, states, do):
    def body(u_ref, w_ref, qg_ref, kd_ref, in_ref, aux_ref, st_ref, do_ref,
             du_ref, dw_ref, dqg_ref, dkd_ref, din_ref, daux_ref, ds_scr):
        @pl.when(pl.program_id(0) == 0)
        def _():
            ds_scr[...] = jnp.zeros_like(ds_scr)

        ds = ds_scr[...]
        for k in reversed(range(CPS)):
            rows = slice(CH * k, CH * (k + 1))
            _, vjp = jax.vjp(_dn_step, st_ref[k], u_ref[rows, :], w_ref[rows, :], qg_ref[rows, :], kd_ref[rows, :],
                             in_ref[:, rows, :], aux_ref[rows, :])
            ds, du, dw, dqg, dkd, din, daux = vjp((do_ref[rows, :], ds))
            du_ref[rows, :] = du
            dw_ref[rows, :] = dw
            dqg_ref[rows, :] = dqg
            dkd_ref[rows, :] = dkd
            din_ref[:, rows, :] = din
            daux_ref[rows, :] = daux
        ds_scr[...] = ds

    wide, inb, auxb, stb = _dn_rec_specs(lambda n: NSTEP - 1 - n)
    return _pc(
        body, grid=(NSTEP,), name="dn_rec_bwd", in_specs=[wide, wide, wide, wide, inb, auxb, stb, wide],
        out_specs=[wide, wide, wide, wide, inb, auxb],
        out_shape=[SDS((S, DNW), f32)] * 4 + [SDS((4, S, CH), f32), SDS((S, 128), f32)],
        scratch_shapes=[pltpu.VMEM((4, DH, DH), f32)], compiler_params=_cp(40),
    )(u, w, qg, kd, intra, aux, states, do)


def _dn_post(o, z, nw):
    parts = []
    for h in range(4):
        sl = slice(DH * h, DH * (h + 1))
        oh = o[:, sl]
        y = oh * lax.rsqrt(jnp.mean(oh * oh, axis=-1, keepdims=True) + EPS) * nw
        zh = z[:, sl]
        parts.append(y * (zh * jax.nn.sigmoid(zh)))
    return jnp.concatenate(parts, axis=1)


def _dn_post_fwd(o, z, nw):
    def body(o_ref, z_ref, nw_ref, y_ref):
        y_ref[...] = _dn_post(o_ref[...], z_ref[...], nw_ref[...])

    wide = pl.BlockSpec((TM, DNW), lambda i: (i, 0))
    return _pc(body, grid=(S // TM,), name="dn_post_fwd",
                          in_specs=[wide, wide, pl.BlockSpec((1, 128), lambda i: (0, 0))], out_specs=wide,
                          out_shape=SDS((S, DNW), f32), compiler_params=_cp(32))(o, z, nw)


def _dn_post_bwd(o, z, nw, dy):
    def body(o_ref, z_ref, nw_ref, dy_ref, do_ref, dz_ref, dnw_ref):
        @pl.when(pl.program_id(0) == 0)
        def _():
            dnw_ref[...] = jnp.zeros_like(dnw_ref)

        _, vjp = jax.vjp(_dn_post, o_ref[...], z_ref[...], nw_ref[...])
        do, dz, dnw = vjp(dy_ref[...])
        do_ref[...] = do
        dz_ref[...] = dz
        dnw_ref[...] += dnw

    wide = pl.BlockSpec((TM, DNW), lambda i: (i, 0))
    one = pl.BlockSpec((1, 128), lambda i: (0, 0))
    return _pc(body, grid=(S // TM,), name="dn_post_bwd", in_specs=[wide, wide, one, wide],
                          out_specs=[wide, wide, one], out_shape=[SDS((S, DNW), f32), SDS((S, DNW), f32), SDS((1, 128), f32)],
                          compiler_params=_cp(32))(o, z, nw, dy)


def _row_tile(rows, width, itemsize=4, target=2 * 1024 * 1024):
    best = None
    for t in range(16, rows + 1, 16):
        if rows % t == 0 and t * width * itemsize <= target:
            best = t
    return best if best is not None else rows


def _sum_pieces(pieces, out_dtype, name):
    n, rows, width = pieces.shape
    tr = _row_tile(rows, width * n)

    def body(p_ref, o_ref):
        acc = p_ref[0].astype(f32)
        for s in range(1, n):
            acc = acc + p_ref[s].astype(f32)
        o_ref[...] = acc.astype(out_dtype)

    return _pc(body, grid=(rows // tr,), name=name,
                          in_specs=[pl.BlockSpec((n, tr, width), lambda i: (0, i, 0))],
                          out_specs=pl.BlockSpec((tr, width), lambda i: (i, 0)),
                          out_shape=SDS((rows, width), out_dtype), compiler_params=_cp(32))(pieces)


def _sum_core_pair(part, got, c_arr):
    n, rows, width = part.shape
    half = rows // 2
    tr = _row_tile(half, width, itemsize=2)
    nt = half // tr

    def body(c_ref, p_ref, g_ref, o_ref):
        o_ref[...] = (p_ref[...].astype(f32) + g_ref[...].astype(f32)).astype(bf16)

    gs = pltpu.PrefetchScalarGridSpec(
        num_scalar_prefetch=1, grid=(n, nt),
        in_specs=[pl.BlockSpec((1, tr, width), lambda j, i, c: (j, c[0] * nt + i, 0)),
                  pl.BlockSpec((1, tr, width), lambda j, i, c: (j, i, 0))],
        out_specs=pl.BlockSpec((1, tr, width), lambda j, i, c: (j, i, 0)))
    return _pc(body, grid_spec=gs, name="sum_core_pair", out_shape=SDS((n, half, width), bf16),
                          compiler_params=_cp(32))(c_arr, part, got)


def _sum_chips(pieces, c_arr, full, row0, total_rows):
    n, half, width = pieces.shape
    tr = max(t for t in range(16, 257, 16) if half % t == 0 and row0 % t == 0)
    nt = half // tr

    def body(c_ref, p_ref, *rest):
        o_ref = rest[-1]
        acc = p_ref[0].astype(f32)
        for s in range(1, n):
            acc = acc + p_ref[s].astype(f32)
        o_ref[...] = acc

    gs = pltpu.PrefetchScalarGridSpec(
        num_scalar_prefetch=1, grid=(nt,),
        in_specs=[pl.BlockSpec((n, tr, width), lambda i, c: (0, i, 0))] + ([] if full is None else [ANY]),
        out_specs=pl.BlockSpec((tr, width), lambda i, c: (row0 // tr + c[0] * nt + i, 0)))
    args = (c_arr, pieces) if full is None else (c_arr, pieces, full)
    return _pc(body, grid_spec=gs, name="sum_chips", out_shape=SDS((total_rows, width), f32),
                          input_output_aliases={} if full is None else {2: 0}, compiler_params=_cp(32))(*args)


def _adamw_math(w, g, m, v):
    mn = ADAM_B1 * m + (1.0 - ADAM_B1) * g
    vn = ADAM_B2 * v + (1.0 - ADAM_B2) * (g * g)
    m_hat = mn / (1.0 - ADAM_B1 ** ADAM_STEP)
    v_hat = vn / (1.0 - ADAM_B2 ** ADAM_STEP)
    return -ADAM_LR * (m_hat / (jnp.sqrt(v_hat) + ADAM_EPS) + ADAM_WD * w), mn, vn


def _adamw(w, g, m, v, name):
    rows, width = w.shape
    tr = _row_tile(rows, width * 7, target=12 * 1024 * 1024)

    def body(w_ref, g_ref, m_ref, v_ref, d_ref, nm_ref, nv_ref):
        d_ref[...], nm_ref[...], nv_ref[...] = _adamw_math(w_ref[...], g_ref[...], m_ref[...], v_ref[...])

    blk = pl.BlockSpec((tr, width), lambda i: (i, 0))
    return _pc(body, grid=(rows // tr,), name=name, in_specs=[blk] * 4, out_specs=[blk] * 3,
                          out_shape=[SDS((rows, width), f32)] * 3, compiler_params=_cp(40))(w, g, m, v)


def _adamw_rows(w, m, v, gblob, tr, first_tile, name):
    layers, rows, width = w.shape

    def body(w_ref, g_ref, m_ref, v_ref, d_ref, nm_ref, nv_ref):
        d_ref[0], nm_ref[0], nv_ref[0] = _adamw_math(w_ref[0], g_ref[...], m_ref[0], v_ref[0])

    blk = pl.BlockSpec((1, tr, width), lambda l, i: (l, i, 0))
    gblk = pl.BlockSpec((tr, width), lambda l, i: (first_tile(l) + i, 0))
    return _pc(body, grid=(layers, rows // tr), name=name, in_specs=[blk, gblk, blk, blk], out_specs=[blk] * 3,
                          out_shape=[SDS(w.shape, f32)] * 3, compiler_params=_cp(40))(w, gblob, m, v)


ANY = pl.BlockSpec(memory_space=pl.ANY)


def _place():
    x, y, c = lax.axis_index("x"), lax.axis_index("y"), lax.axis_index("c")
    chips = [(1 - x, y), (x, 1 - y), (1 - x, 1 - y)]
    return x, y, c, chips


NQ_ICI = 4
NQ_D2D = 8


def _chunks(rows, want):
    n = max(k for k in range(1, want + 1) if rows % k == 0 and (rows // k) % 16 == 0)
    step = rows // n
    return [(q * step, step) for q in range(n)]


def _scatter_copies(ins, outs, ssem, rsem, lsem):
    x, y, c, chips = _place()
    me = (x, y, c)
    locals_, sends, lands = [], [], []
    for b in range(len(ins)):
        for q, (off, n) in enumerate(_chunks(ins[b].shape[1], NQ_ICI)):
            rows = pl.ds(off, n)
            mine = outs[b].at[2 * x + y, rows, :]
            locals_.append(pltpu.make_async_copy(ins[b].at[2 * x + y, rows, :], mine, lsem.at[b, q]))
            for j, chip in enumerate(chips):
                sends.append(_remote(ins[b].at[2 * chip[0] + chip[1], rows, :], mine, ssem.at[b, j, q], rsem.at[b, j, q],
                                     (*chip, c)))
                slot = outs[b].at[2 * chip[0] + chip[1], rows, :]
                lands.append(_remote(slot, slot, ssem.at[b, j, q], rsem.at[b, j, q], me))
    return locals_, sends, lands


def _scatter_start(ins, outs, ssem, rsem, lsem):
    locals_, sends, _ = _scatter_copies(ins, outs, ssem, rsem, lsem)
    for cp in locals_ + sends:
        cp.start()


def _scatter_finish(ins, outs, ssem, rsem, lsem):
    locals_, sends, lands = _scatter_copies(ins, outs, ssem, rsem, lsem)
    for cp in lands:
        cp.wait_recv()
    for cp in sends:
        cp.wait_send()
    for cp in locals_:
        cp.wait()


def _scatter_sems(nb):
    return [pltpu.SemaphoreType.DMA((nb, 3, NQ_ICI)), pltpu.SemaphoreType.DMA((nb, 3, NQ_ICI)),
            pltpu.SemaphoreType.DMA((nb, NQ_ICI))]


def _remote(src, dst, ssem, rsem, dev):
    return pltpu.make_async_remote_copy(src_ref=src, dst_ref=dst, send_sem=ssem, recv_sem=rsem, device_id=dev,
                                        device_id_type=MESH)


def _all_gather_weights(shards):
    nb = len(shards)

    def body(*refs):
        ins, outs, sems = refs[:nb], refs[nb:2 * nb], refs[2 * nb:]
        _gather_start(ins, outs, *sems)
        _gather_finish(ins, outs, *sems)

    return _pc(
        body, name="all_gather_weights", in_specs=[ANY] * nb, out_specs=[ANY] * nb,
        out_shape=[SDS((NCH,) + s.shape, s.dtype) for s in shards], scratch_shapes=_gather_sems(nb),
    )(*shards)


def _rows_to_move(ref):
    return 3 * FC + 256 if ref.shape[0] == 4 * FC else ref.shape[0]


def _gather_first(ins, outs, ssem, rsem, lsem):
    x, y, c, chips = _place()
    locals_, sends = [], []
    for b in range(len(ins)):
        half = _rows_to_move(ins[b]) // 2
        for q, (off, n) in enumerate(_chunks(half, NQ_ICI)):
            mine = pl.ds(c * half + off, n)
            own = outs[b].at[2 * x + y, mine, :]
            locals_.append(pltpu.make_async_copy(ins[b].at[mine, :], own, lsem.at[b, q]))
            sends.append(_remote(ins[b].at[mine, :], own, ssem.at[b, 0, q], rsem.at[b, 0, q], (x, y, 1 - c)))
            sends += [_remote(ins[b].at[mine, :], own, ssem.at[b, 1 + j, q], rsem.at[b, 1 + j, q], (*chip, c))
                      for j, chip in enumerate(chips)]
    return locals_, sends


def _gather_start(ins, outs, ssem, rsem, lsem):
    locals_, sends = _gather_first(ins, outs, ssem, rsem, lsem)
    for cp in locals_ + sends:
        cp.start()


def _gather_finish(ins, outs, ssem, rsem, lsem):
    x, y, c, chips = _place()
    me, sib = (x, y, c), (x, y, 1 - c)
    locals_, sends = _gather_first(ins, outs, ssem, rsem, lsem)
    for b in range(len(ins)):
        half = _rows_to_move(ins[b]) // 2
        for q, (off, n) in enumerate(_chunks(half, NQ_ICI)):
            mine = pl.ds(c * half + off, n)
            for j, chip in enumerate(chips):
                landed = outs[b].at[2 * chip[0] + chip[1], mine, :]
                _remote(landed, landed, ssem.at[b, 1 + j, q], rsem.at[b, 1 + j, q], me).wait_recv()
                cp = _remote(landed, landed, ssem.at[b, 4 + j, q], rsem.at[b, 4 + j, q], sib)
                cp.start()
                sends.append(cp)
    for b in range(len(ins)):
        half = _rows_to_move(ins[b]) // 2
        for q, (off, n) in enumerate(_chunks(half, NQ_ICI)):
            other = pl.ds((1 - c) * half + off, n)
            theirs = outs[b].at[2 * x + y, other, :]
            _remote(theirs, theirs, ssem.at[b, 0, q], rsem.at[b, 0, q], me).wait_recv()
            for j, chip in enumerate(chips):
                fwd = outs[b].at[2 * chip[0] + chip[1], other, :]
                _remote(fwd, fwd, ssem.at[b, 4 + j, q], rsem.at[b, 4 + j, q], me).wait_recv()
    for cp in sends:
        cp.wait_send()
    for cp in locals_:
        cp.wait()


def _gather_sems(nb):
    return [pltpu.SemaphoreType.DMA((nb, 7, NQ_ICI)), pltpu.SemaphoreType.DMA((nb, 7, NQ_ICI)),
            pltpu.SemaphoreType.DMA((nb, NQ_ICI))]


def _send_sibling_half(parts):
    nb = len(parts)

    def body(*refs):
        ins, gots = refs[:nb], refs[nb:2 * nb]
        ssem, rsem = refs[2 * nb:]
        x, y, c, _ = _place()
        sib = (x, y, 1 - c)
        todo = []
        for b in range(nb):
            half = ins[b].shape[1] // 2
            for q, (off, n) in enumerate(_chunks(half, NQ_D2D)):
                cp = _remote(ins[b].at[:, pl.ds((1 - c) * half + off, n), :], gots[b].at[:, pl.ds(off, n), :],
                             ssem.at[b, q], rsem.at[b, q], sib)
                cp.start()
                todo.append(cp)
        for cp in todo:
            cp.wait()

    return _pc(
        body, name="send_sibling_half", in_specs=[ANY] * nb, out_specs=[ANY] * nb,
        out_shape=[SDS((p.shape[0], p.shape[1] // 2, p.shape[2]), p.dtype) for p in parts],
        scratch_shapes=[pltpu.SemaphoreType.DMA((nb, NQ_D2D)), pltpu.SemaphoreType.DMA((nb, NQ_D2D))],
    )(*parts)


def _scatter_to_chips(parts):
    nb = len(parts)

    def body(*refs):
        ins, outs, sems = refs[:nb], refs[nb:2 * nb], refs[2 * nb:]
        _scatter_start(ins, outs, *sems)
        _scatter_finish(ins, outs, *sems)

    return _pc(
        body, name="scatter_to_chips", in_specs=[ANY] * nb, out_specs=[ANY] * nb,
        out_shape=[SDS(p.shape, p.dtype) for p in parts], scratch_shapes=_scatter_sems(nb),
    )(*parts)


def _join_halves(fulls, ranges):
    nb = len(fulls)
    nr = max(len(r) for r in ranges)

    def body(*refs):
        ins, outs = refs[:nb], refs[nb:2 * nb]
        ssem, rsem = refs[2 * nb:]
        x, y, c, _ = _place()
        sib = (x, y, 1 - c)
        sends, lands = [], []
        for b in range(nb):
            for g, (row0, rows) in enumerate(ranges[b]):
                half = rows // 2
                for q, (off, n) in enumerate(_chunks(half, NQ_D2D)):
                    mine = pl.ds(row0 + c * half + off, n)
                    sends.append(_remote(ins[b].at[mine, :], outs[b].at[mine, :], ssem.at[b, g, q], rsem.at[b, g, q], sib))
                    other = outs[b].at[pl.ds(row0 + (1 - c) * half + off, n), :]
                    lands.append(_remote(other, other, ssem.at[b, g, q], rsem.at[b, g, q], sib))
        for cp in sends:
            cp.start()
        for cp in lands:
            cp.wait_recv()
        for cp in sends:
            cp.wait_send()

    return _pc(
        body, name="join_halves", in_specs=[ANY] * nb, out_specs=[ANY] * nb,
        out_shape=[SDS(h.shape, h.dtype) for h in fulls], input_output_aliases={b: b for b in range(nb)},
        scratch_shapes=[pltpu.SemaphoreType.DMA((nb, nr, NQ_D2D)), pltpu.SemaphoreType.DMA((nb, nr, NQ_D2D))],
    )(*fulls)


def _gather_small(vec):
    def body(v_ref, o_ref, ssem, rsem, lsem):
        x, y, c, _ = _place()
        mine = o_ref.at[4 * x + 2 * y + c]
        local = pltpu.make_async_copy(v_ref, mine, lsem)
        local.start()
        sends = []
        for k in range(1, 8):
            peer = (x ^ (k >> 2), y ^ ((k >> 1) & 1), c ^ (k & 1))
            cp = _remote(v_ref, mine, ssem.at[k - 1], rsem.at[k - 1], peer)
            cp.start()
            sends.append(cp)
        for k in range(1, 8):
            px, py, pc = x ^ (k >> 2), y ^ ((k >> 1) & 1), c ^ (k & 1)
            slot = o_ref.at[4 * px + 2 * py + pc]
            _remote(slot, slot, ssem.at[k - 1], rsem.at[k - 1], (x, y, c)).wait_recv()
        for cp in sends:
            cp.wait_send()
        local.wait()

    return _pc(
        body, name="gather_small", in_specs=[ANY], out_specs=ANY, out_shape=SDS((8,) + vec.shape, vec.dtype),
        scratch_shapes=[pltpu.SemaphoreType.DMA((7,)), pltpu.SemaphoreType.DMA((7,)), pltpu.SemaphoreType.DMA],
    )(vec)


def _block_diag(pw):
    return jnp.concatenate([jnp.pad(pw[g], ((0, 0), (64 * g, 192 - 64 * g))) for g in range(4)], axis=0)


def _own_columns(full, chip):
    n = full.shape[-1] // NCH
    parts = full.reshape(full.shape[:-1] + (NCH, n))
    sel = (lax.broadcasted_iota(jnp.int32, (NCH, 1), 0) == chip)
    return jnp.sum(jnp.where(sel, parts, 0.0), axis=-2)


def _at_own_columns(shard, chip):
    n = shard.shape[-1]
    sel = (lax.broadcasted_iota(jnp.int32, (NCH * n,), 0) // n == chip)
    return jnp.where(sel, jnp.tile(shard, NCH), 0.0)


def _pad_rows(a, rows):
    return jnp.pad(a, ((0, rows - a.shape[0]),) + ((0, 0),) * (a.ndim - 1))


def _ffn_block(l, which):
    return 7 * l + 3 * which


def _wout_block(l):
    return 7 * l + 6


class _Weights:
    def __init__(self):
        self.ffn, self.wout, self.w_aug, self.rides = {}, {}, {}, {}

    @classmethod
    def from_blob(cls, blob, w_aug):
        self = cls()
        for l in range(DEPTH):
            self.ffn[(l, 0)], self.ffn[(l, 1)] = (blob, _ffn_block(l, 0)), (blob, _ffn_block(l, 1))
            self.wout[l], self.w_aug[l] = (blob, _wout_block(l)), w_aug[l]
        return self

    def set_w_in(self, l, gathered):
        self.w_aug[l] = jnp.pad(gathered.transpose(1, 0, 2).reshape(D, INW), ((0, 0), (0, INP - INW)))

    def ffn_fwd(self, l, which, x, nw):
        arr, k0 = self.ffn[(l, which)]
        if (l, which) not in self.rides:
            return _ffn_fwd(x, nw, arr, k0)
        shards, landed = self.rides[(l, which)]
        out, *gathered = _ffn_fwd(x, nw, arr, k0, shards)
        landed(gathered)
        return out


def _layer_fwd(l, x0, pos, freq, wts, ws):
    sv = {"x0": x0}
    x1 = ws.ffn_fwd(l, 0, x0, wts["ffn1_norm"][l:l + 1])
    att, att4, att16, pu, dq, dz, dba = _inproj_fwd(x1, wts["mix_norm"][l:l + 1], ws.w_aug[l], pos, freq)
    qkvs = [att, att4.reshape(S, 768), att16.reshape(S, 768)]
    (o1, l1), (o4, l4), (o16, l16) = [_attn_fwd(q, NB // d) for q, d in zip(qkvs, PATTERN_DIL)]
    ols = (o1, l1, o4.reshape(4, S // 4, ATT), l4.reshape(4, S // 4, ATT), o16.reshape(16, S // 16, ATT),
           l16.reshape(16, S // 16, ATT))
    ya = _merge_fwd(*ols)
    yb = _pool_fwd(pu, wts["pool_bd"][l], wts["pool_scale"][l:l + 1])
    c = _conv_fwd(dq, wts["conv_w"][l])
    u, w, qg, kd, intra, aux = _dn_local_fwd(c, dba, wts["dn_par"][l])
    o_dn, states = _dn_rec_fwd(u, w, qg, kd, intra, aux)
    yc = _dn_post_fwd(o_dn, dz, wts["dn_out_norm"][l:l + 1])
    x2 = _outproj_fwd(x1, ya, yb, yc, *ws.wout[l])
    x3 = ws.ffn_fwd(l, 1, x2, wts["ffn2_norm"][l:l + 1])
    sv.update(x1=x1, x2=x2, qkvs=qkvs, ols=ols, ya=ya, yb=yb, yc=yc, pu=pu, dq=dq, dz=dz, dba=dba, c=c,
              u=u, w=w, qg=qg, kd=kd, intra=intra, aux=aux, states=states, o_dn=o_dn)
    return x3, sv


def _wout_part(g):
    return jnp.pad(g.astype(bf16).reshape(NCH, 256, D), ((0, 0), (0, FC - 256), (0, 0)))


def _win_part(g):
    return g[:, :INW].astype(bf16).reshape(D, NCH, INC).transpose(1, 0, 2)


def _layer_bwd(l, dx3, sv, pos, freq, wts, ws, ride=None, prep=None):
    gr = {}
    g2, u2, d2, dh4, *pieces_before = _ffn_bwd(sv["x2"], wts["ffn2_norm"][l:l + 1], *ws.ffn[(l, 1)], dx3, ride)
    gr.update(ffn2_w_gate=g2, ffn2_w_up=u2, ffn2_w_down=d2)
    dx2, gr["ffn2_norm"], dya, dyb, dyc, gr["w_out"] = _outproj_bwd(sv["x2"], wts["ffn2_norm"][l:l + 1], dx3, dh4, sv["ya"],
                                                                     sv["yb"], sv["yc"], *ws.wout[l])
    do_dn, ddz, gr["dn_out_norm"] = _dn_post_bwd(sv["o_dn"], sv["dz"], wts["dn_out_norm"][l:l + 1], dyc)
    du, dw, dqg, dkd, dintra, daux = _dn_rec_bwd(sv["u"], sv["w"], sv["qg"], sv["kd"], sv["intra"], sv["aux"], sv["states"], do_dn)
    dc, ddba, gr["dn_par"] = _dn_local_bwd(sv["c"], sv["dba"], wts["dn_par"][l], du, dw, dqg, dkd, dintra, daux)
    ddq, gr["conv_w"] = _conv_bwd(sv["dq"], wts["conv_w"][l], dc)
    dpu, gr["pool_bd"], gr["pool_scale"] = _pool_bwd(sv["pu"], wts["pool_bd"][l], wts["pool_scale"][l:l + 1], dyb)
    dols = _merge_bwd(*sv["ols"], dya)
    flat = lambda a: a.reshape(S, ATT)
    datts = [_attn_bwd(q, flat(sv["ols"][2 * p]), flat(sv["ols"][2 * p + 1]), flat(dols[2 * p]), flat(dols[2 * p + 1]), NB // d)
             for p, (q, d) in enumerate(zip(sv["qkvs"], PATTERN_DIL))]
    dx1, gr["mix_norm"], gr["w_aug"] = _inproj_bwd(sv["x1"], wts["mix_norm"][l:l + 1], ws.w_aug[l], pos, freq, dx2,
                                                    datts[0], datts[1].reshape(4, S // 4, 768),
                                                    datts[2].reshape(16, S // 16, 768), dpu, ddq, ddz, ddba)
    own = None
    if prep is not None:
        own = prep([jnp.concatenate([g2, u2, d2, _wout_part(gr["w_out"])], axis=1), _win_part(gr["w_aug"])])
    g1, u1, d1, dh4, *pieces_own = _ffn_bwd(sv["x0"], wts["ffn1_norm"][l:l + 1], *ws.ffn[(l, 0)], dx1, own)
    dx0, gr["ffn1_norm"] = _norm_bwd(sv["x0"], wts["ffn1_norm"][l:l + 1], dx1, dh4)
    gr.update(ffn1_w_gate=g1, ffn1_w_up=u1, ffn1_w_down=d1)
    return dx0, gr, pieces_before, pieces_own


def _device_step(x, pos, target, wts, ws, prep=None):
    freq = jnp.tile(ROPE_THETA ** (-jnp.arange(0, EH, 2, dtype=f32) / EH), 2 * ATT // EH).reshape(1, ATT)
    saved = []
    h = x
    for l in range(DEPTH):
        h, sv = _layer_fwd(l, h, pos, freq, wts, ws)
        saved.append(sv)
    dh, g_final, loss = _final(h, wts["final_norm"], target)
    grads = [None] * DEPTH
    dh, grads[1], _, _ = _layer_bwd(1, dh, saved[1], pos, freq, wts, ws)
    sums1 = None
    if prep is not None:
        g = grads[1]
        ffn = [g[f"ffn{f}_w_{n}"] for f in (1, 2) for n in ("gate", "up", "down")]
        sums1 = prep([jnp.concatenate(ffn + [_wout_part(g["w_out"])], axis=1), _win_part(g["w_aug"])])
    dh, grads[0], pieces1, pieces0 = _layer_bwd(0, dh, saved[0], pos, freq, wts, ws, sums1, prep)
    return loss, dh, g_final, grads, pieces1, pieces0


_SMALL = (("ffn1_norm", (DEPTH, D)), ("mix_norm", (DEPTH, D)), ("pool_w", (DEPTH, 4, 64, 64)), ("pool_scale", (DEPTH, 256)),
          ("dn_conv_w", (DEPTH, 4, CW)), ("dn_a_log", (DEPTH, 4)), ("dn_dt_bias", (DEPTH, 4)), ("dn_out_norm", (DEPTH, 128)),
          ("ffn2_norm", (DEPTH, D)), ("final_norm", (D,)), ("loss", (1,)))


def _pack_small(vals):
    rows = []
    for name, shape in _SMALL:
        flat = vals[name].astype(f32).reshape(-1)
        rows.append(jnp.pad(flat, (0, _small_rows(shape) * 128 - flat.shape[0])).reshape(-1, 128))
    out = jnp.concatenate(rows, axis=0)
    return _pad_rows(out, -(-out.shape[0] // 16) * 16)


def _small_rows(shape):
    return -(-int(np.prod(shape)) // 1024) * 8


def _unpack_small(packed):
    vals, r = {}, 0
    for name, shape in _SMALL:
        size, n = int(np.prod(shape)), _small_rows(shape)
        vals[name] = packed[r:r + n].reshape(-1)[:size].reshape(shape)
        r += n
    return vals


def kernel(x, positions, ffn1_norm, ffn1_w_gate, ffn1_w_up, ffn1_w_down, mix_norm, w_in, pool_w, pool_scale, dn_conv_w, dn_a_log, dn_dt_bias, dn_out_norm, w_out, ffn2_norm, ffn2_w_gate, ffn2_w_up, ffn2_w_down, final_norm, loss_target, m_ffn1_norm, m_ffn1_w_gate, m_ffn1_w_up, m_ffn1_w_down, m_mix_norm, m_w_in, m_pool_w, m_pool_scale, m_dn_conv_w, m_dn_a_log, m_dn_dt_bias, m_dn_out_norm, m_w_out, m_ffn2_norm, m_ffn2_w_gate, m_ffn2_w_up, m_ffn2_w_down, m_final_norm, v_ffn1_norm, v_ffn1_w_gate, v_ffn1_w_up, v_ffn1_w_down, v_mix_norm, v_w_in, v_pool_w, v_pool_scale, v_dn_conv_w, v_dn_a_log, v_dn_dt_bias, v_dn_out_norm, v_w_out, v_ffn2_norm, v_ffn2_w_gate, v_ffn2_w_up, v_ffn2_w_down, v_final_norm):
    names = ["ffn1_norm", "ffn1_w_gate", "ffn1_w_up", "ffn1_w_down", "mix_norm", "w_in", "pool_w", "pool_scale", "dn_conv_w",
             "dn_a_log", "dn_dt_bias", "dn_out_norm", "w_out", "ffn2_norm", "ffn2_w_gate", "ffn2_w_up", "ffn2_w_down", "final_norm"]
    W = dict(zip(names, [ffn1_norm, ffn1_w_gate, ffn1_w_up, ffn1_w_down, mix_norm, w_in, pool_w, pool_scale, dn_conv_w,
                         dn_a_log, dn_dt_bias, dn_out_norm, w_out, ffn2_norm, ffn2_w_gate, ffn2_w_up, ffn2_w_down, final_norm]))
    M = dict(zip(names, [m_ffn1_norm, m_ffn1_w_gate, m_ffn1_w_up, m_ffn1_w_down, m_mix_norm, m_w_in, m_pool_w, m_pool_scale,
                         m_dn_conv_w, m_dn_a_log, m_dn_dt_bias, m_dn_out_norm, m_w_out, m_ffn2_norm, m_ffn2_w_gate, m_ffn2_w_up,
                         m_ffn2_w_down, m_final_norm]))
    V = dict(zip(names, [v_ffn1_norm, v_ffn1_w_gate, v_ffn1_w_up, v_ffn1_w_down, v_mix_norm, v_w_in, v_pool_w, v_pool_scale,
                         v_dn_conv_w, v_dn_a_log, v_dn_dt_bias, v_dn_out_norm, v_w_out, v_ffn2_norm, v_ffn2_w_gate, v_ffn2_w_up,
                         v_ffn2_w_down, v_final_norm]))
    chip = 2 * lax.axis_index("x") + lax.axis_index("y")

    ffn_names = [(f"ffn{f}_w_gate", f"ffn{f}_w_up", f"ffn{f}_w_down") for f in (1, 2)]
    tr = lambda t: jnp.swapaxes(t, -1, -2)
    def ffn_rows(l, which):
        g, u, dn = ffn_names[which]
        return [tr(W[g][l]), tr(W[u][l]), W[dn][l]]

    def second_half(l):
        return jnp.concatenate(ffn_rows(l, 1) + [jnp.pad(W["w_out"][l], ((0, FC - 256), (0, 0)))], axis=0).astype(bf16)

    ws = _Weights()
    first0, = _all_gather_weights([jnp.concatenate(ffn_rows(0, 0), axis=0).astype(bf16)])
    ws.ffn[(0, 0)] = (first0, 0)

    def landed_00(gathered):
        ws.ffn[(0, 1)], ws.wout[0] = (gathered[0], 0), (gathered[0], 3)
        ws.set_w_in(0, gathered[1])

    def landed_01(gathered):
        ws.ffn[(1, 0)] = (gathered[0], 0)
        ws.set_w_in(1, gathered[1])

    def landed_10(gathered):
        ws.ffn[(1, 1)], ws.wout[1] = (gathered[0], 0), (gathered[0], 3)

    ws.rides[(0, 0)] = ([second_half(0), W["w_in"][0].astype(bf16)], landed_00)
    ws.rides[(0, 1)] = ([jnp.concatenate(ffn_rows(1, 0), axis=0).astype(bf16), W["w_in"][1].astype(bf16)], landed_01)
    ws.rides[(1, 0)] = ([second_half(1)], landed_10)
    conv_all = _gather_small(_pad_rows(dn_conv_w.reshape(DEPTH * 4 * (CW // NCH) // 128, 128), 32))
    conv_full = jnp.concatenate([conv_all[2 * j, :DEPTH * 4 * (CW // NCH) // 128].reshape(DEPTH, 4, CW // NCH) for j in range(NCH)],
                                axis=-1)

    par = jnp.pad(jnp.stack([dn_a_log, dn_dt_bias], axis=1), ((0, 0), (0, 6), (4, 120)))
    wts = dict(ffn1_norm=ffn1_norm, mix_norm=mix_norm, ffn2_norm=ffn2_norm, final_norm=final_norm.reshape(1, D),
               pool_bd=jnp.stack([_block_diag(pool_w[l]) for l in range(DEPTH)]).astype(bf16),
               pool_scale=pool_scale, conv_w=jnp.pad(conv_full, ((0, 0), (0, 4), (0, 0))),
               dn_par=par, dn_out_norm=dn_out_norm)

    c_arr = lax.axis_index("c").astype(jnp.int32).reshape(1)

    def prep(parts):
        return [_sum_core_pair(p, g, c_arr) for p, g in zip(parts, _send_sibling_half(parts))]

    loss, dx, g_final, grads, pieces1, pieces0 = _device_step(x[0], positions.reshape(S, 1), loss_target[0], wts, ws, prep)
    last = [jnp.concatenate([grads[0][n] for n in ffn_names[0]], axis=1)]
    pieces_last = _scatter_to_chips(prep(last))
    full_b = _sum_chips(pieces_last[0], c_arr, None, 0, RB)
    full_b = _sum_chips(pieces0[0], c_arr, full_b, 3 * FC, RB)
    full_b = _sum_chips(pieces1[0], c_arr, full_b, 7 * FC, RB)
    full_c = _sum_chips(pieces0[1], c_arr, None, 0, RC)
    full_c = _sum_chips(pieces1[1], c_arr, full_c, D, RC)
    full_b, full_c = _join_halves([full_b, full_c], [[(0, 3 * FC), (3 * FC, 4 * FC), (7 * FC, 7 * FC)], [(0, D), (D, D)]])

    small = {"loss": loss[0, 0:1], "final_norm": g_final.reshape(D)}
    for n in ("ffn1_norm", "mix_norm", "ffn2_norm", "pool_scale", "dn_out_norm"):
        small[n] = jnp.stack([grads[l][n].reshape(-1) for l in range(DEPTH)])
    small["pool_w"] = jnp.stack([jnp.stack([grads[l]["pool_bd"][64 * g:64 * (g + 1), 64 * g:64 * (g + 1)] for g in range(4)])
                                 for l in range(DEPTH)])
    small["dn_conv_w"] = jnp.stack([grads[l]["conv_w"][0:4] for l in range(DEPTH)])
    small["dn_a_log"] = jnp.stack([grads[l]["dn_par"][0, 4:8] for l in range(DEPTH)])
    small["dn_dt_bias"] = jnp.stack([grads[l]["dn_par"][1, 4:8] for l in range(DEPTH)])
    packed = _pack_small(small)
    g_small = _sum_pieces(_gather_small(packed), f32, "sum_small")
    gs = _unpack_small(g_small)

    transposed = ("ffn1_w_gate", "ffn1_w_up", "ffn2_w_gate", "ffn2_w_up")
    where = {"ffn1_w_gate": (full_b, FC // 2, lambda l: 14 * l), "ffn1_w_up": (full_b, FC // 2, lambda l: 14 * l + 2),
             "ffn1_w_down": (full_b, FC // 2, lambda l: 14 * l + 4), "ffn2_w_gate": (full_b, FC // 2, lambda l: 14 * l + 6),
             "ffn2_w_up": (full_b, FC // 2, lambda l: 14 * l + 8), "ffn2_w_down": (full_b, FC // 2, lambda l: 14 * l + 10),
             "w_out": (full_b, 64, lambda l: (FC // 64) * (7 * l + 6)), "w_in": (full_c, D // 2, lambda l: 2 * l)}
    big_res = {}
    for n, (gblob, tile, first) in where.items():
        t = tr if n in transposed else (lambda a: a)
        big_res[n] = [t(r) for r in _adamw_rows(t(W[n]), t(M[n]), t(V[n]), gblob, tile, first, "adamw_" + n)]

    def small_of(T):
        d = {n: T[n] for n, _ in _SMALL if n not in ("loss", "dn_conv_w")}
        d["loss"] = jnp.zeros((1,), f32)
        d["dn_conv_w"] = _at_own_columns(T["dn_conv_w"], chip)
        return _pack_small(d)

    res_s = _adamw(small_of(W), g_small, small_of(M), small_of(V), "adamw_small")
    small_out = [_unpack_small(r) for r in res_s]

    def split_blobs(b, c):
        out = {}
        b7 = b.reshape(DEPTH, 7, FC, D)
        for k, n in enumerate(n for names3 in ffn_names for n in names3):
            out[n] = tr(b7[:, k]) if n in transposed else b7[:, k]
        out["w_out"] = b7[:, 6, :256]
        out["w_in"] = c.reshape(DEPTH, D, INC)
        return out

    def assemble(big, sm):
        out = []
        for n in names:
            if n in big:
                out.append(big[n])
            elif n == "dn_conv_w":
                out.append(_own_columns(sm[n], chip))
            else:
                out.append(sm[n])
        return out

    grad_list = assemble(split_blobs(full_b, full_c), gs)
    outs = [gs["loss"].reshape(()), dx.reshape(1, S, D)] + grad_list
    for k in range(3):
        outs += assemble({n: r[k] for n, r in big_res.items()}, small_out[k])
    return tuple(outs)
```

```python
import functools
import math

import jax
import jax.numpy as jnp
import numpy as np
from jax import lax
from jax.experimental import pallas as pl
from jax.experimental.pallas import tpu as pltpu

f32 = jnp.float32
bf16 = jnp.bfloat16
SDS = jax.ShapeDtypeStruct
MESH = pl.DeviceIdType.MESH

S = 4096
D = 1024
DEPTH = 2
FF = 2816
NCH = 4
FC = FF // NCH
INW = 3080
INC = INW // NCH
INP = 3200
ATT = 256
EH = 64
NBLK = 128
DNW = 512
DH = 128
CH = 64
NCHUNK = S // CH
EPS = 1e-6
ROPE_THETA = 10000.0
PATTERN_DIL = (1, 4, 16)
ADAM_LR, ADAM_B1, ADAM_B2, ADAM_EPS, ADAM_WD, ADAM_STEP = 0.001, 0.9, 0.999, 1e-08, 0.01, 10
VMEM_BYTES_V7X = 64 * 1024 * 1024
NEG = -1e30

TM = 512
RB, RC = 14 * FC, 2 * D


def _cp(vmem_mb=48, sem=None):
    kw = dict(vmem_limit_bytes=vmem_mb * 1024 * 1024)
    if sem is not None:
        kw["dimension_semantics"] = sem
    return pltpu.CompilerParams(**kw)


def _pc(*args, **kwargs):
    pin = lambda s: pltpu.HBM(s.shape, s.dtype) if isinstance(s, SDS) and jnp.issubdtype(s.dtype, jnp.floating) else s
    out = kwargs["out_shape"]
    kwargs["out_shape"] = [pin(s) for s in out] if isinstance(out, (list, tuple)) else pin(out)
    call = pl.pallas_call(*args, **kwargs)

    def run(*operands):
        pinned = [pltpu.with_memory_space_constraint(o, pltpu.HBM) if jnp.issubdtype(o.dtype, jnp.floating) else o
                  for o in operands]
        return call(*pinned)

    return run


def _dot(a, b):
    return jnp.dot(a, b, preferred_element_type=f32)


def _dot_nt(a, b):
    return lax.dot_general(a, b, (((1,), (1,)), ((), ())), preferred_element_type=f32)


def _dot_tn(a, b):
    return lax.dot_general(a, b, (((0,), (0,)), ((), ())), preferred_element_type=f32)


def _rms(x, w):
    r = lax.rsqrt(jnp.mean(x * x, axis=-1, keepdims=True) + EPS)
    return x * r * w, r


def _rms_bwd(x, w, r, dh):
    xhat = x * r
    dw = jnp.sum(dh * xhat, axis=0, keepdims=True)
    dxh = dh * w
    dx = r * (dxh - xhat * jnp.mean(dxh * xhat, axis=-1, keepdims=True))
    return dx, dw


def _ffn_fwd(x, nw, blob, k0, ride=None):
    kg, ku, kd = k0, k0 + 1, k0 + 2
    nr = 0 if ride is None else len(ride)
    ni = S // TM

    def body(*refs):
        x_ref, nw_ref, wg_ref, wu_ref, wd_ref = refs[:5]
        ride_in = refs[5:5 + nr]
        o_ref = refs[5 + nr]
        ride_out = refs[6 + nr:6 + 2 * nr]
        h_scr, acc_scr = refs[6 + 2 * nr:8 + 2 * nr]
        sems = refs[8 + 2 * nr:]
        i = pl.program_id(0)
        j = pl.program_id(1)

        if nr:
            @pl.when(jnp.logical_and(i == 0, j == 0))
            def _():
                _gather_start(ride_in, ride_out, *sems)

        @pl.when(j == 0)
        def _():
            h, _ = _rms(x_ref[...], nw_ref[...])
            h_scr[...] = h.astype(bf16)
            acc_scr[...] = jnp.zeros_like(acc_scr)

        h = h_scr[...]
        g = _dot_nt(h, wg_ref[0])
        u = _dot_nt(h, wu_ref[0])
        a = (g * jax.nn.sigmoid(g) * u).astype(bf16)
        acc_scr[...] += _dot(a, wd_ref[0])

        @pl.when(j == NCH - 1)
        def _():
            o_ref[...] = x_ref[...] + 0.5 * acc_scr[...]

        if nr:
            @pl.when(jnp.logical_and(i == ni - 1, j == NCH - 1))
            def _():
                _gather_finish(ride_in, ride_out, *sems)

    wspec = lambda k: pl.BlockSpec((1, FC, D), lambda i, j: (j, k, 0))
    rides = [] if ride is None else list(ride)
    res = _pc(
        body, grid=(ni, NCH), name="ffn_fwd_ride" if nr else "ffn_fwd",
        in_specs=[pl.BlockSpec((TM, D), lambda i, j: (i, 0)),
                  pl.BlockSpec((1, D), lambda i, j: (0, 0)),
                  wspec(kg), wspec(ku), wspec(kd)] + [ANY] * nr,
        out_specs=[pl.BlockSpec((TM, D), lambda i, j: (i, 0))] + [ANY] * nr,
        out_shape=[SDS((S, D), f32)] + [SDS((NCH,) + r.shape, r.dtype) for r in rides],
        scratch_shapes=[pltpu.VMEM((TM, D), bf16), pltpu.VMEM((TM, D), f32)] + (_gather_sems(nr) if nr else []),
        compiler_params=_cp(40),
    )(x, nw, blob, blob, blob, *rides)
    return res if nr else res[0]


def _ffn_bwd(x, nw, blob, k0, dy, ride=None):
    nt = S // TM
    kg, ku, kd = k0, k0 + 1, k0 + 2
    nr = 0 if ride is None else len(ride)

    def body(*refs):
        x_ref, nw_ref, wg_ref, wu_ref, wd_ref, dy_ref = refs[:6]
        ride_in = refs[6:6 + nr]
        dwg_ref, dwu_ref, dwd_ref, dh_ref = refs[6 + nr:10 + nr]
        ride_out = refs[10 + nr:10 + 2 * nr]
        ag, au, ad = refs[10 + 2 * nr:13 + 2 * nr]
        sems = refs[13 + 2 * nr:]
        j = pl.program_id(0)
        i = pl.program_id(1)

        if nr:
            @pl.when(jnp.logical_and(j == 0, i == 0))
            def _():
                _scatter_start(ride_in, ride_out, *sems)

        @pl.when(i == 0)
        def _():
            ag[...] = jnp.zeros_like(ag)
            au[...] = jnp.zeros_like(au)
            ad[...] = jnp.zeros_like(ad)

        hf, _ = _rms(x_ref[...], nw_ref[...])
        h = hf.astype(bf16)
        g = _dot_nt(h, wg_ref[0])
        u = _dot_nt(h, wu_ref[0])
        sg = jax.nn.sigmoid(g)
        s = g * sg
        a = (s * u).astype(bf16)
        dyb = (0.5 * dy_ref[...]).astype(bf16)
        da = _dot_nt(dyb, wd_ref[0])
        ad[...] += _dot_tn(a, dyb)
        du = (da * s).astype(bf16)
        dg = (da * u * (sg * (1.0 + g * (1.0 - sg)))).astype(bf16)
        ag[...] += _dot_tn(dg, h)
        au[...] += _dot_tn(du, h)
        dh_ref[0] = (_dot(dg, wg_ref[0]) + _dot(du, wu_ref[0])).astype(bf16)

        @pl.when(i == nt - 1)
        def _():
            dwg_ref[0] = ag[...].astype(bf16)
            dwu_ref[0] = au[...].astype(bf16)
            dwd_ref[0] = ad[...].astype(bf16)

        if nr:
            @pl.when(jnp.logical_and(j == NCH - 1, i == nt - 1))
            def _():
                _scatter_finish(ride_in, ride_out, *sems)

    wspec = lambda k: pl.BlockSpec((1, FC, D), lambda j, i: (j, k, 0))
    gspec = pl.BlockSpec((1, FC, D), lambda j, i: (j, 0, 0))
    rides = [] if ride is None else list(ride)
    return _pc(
        body, grid=(NCH, nt), name="ffn_bwd_ride" if nr else "ffn_bwd",
        in_specs=[pl.BlockSpec((TM, D), lambda j, i: (i, 0)),
                  pl.BlockSpec((1, D), lambda j, i: (0, 0)),
                  wspec(kg), wspec(ku), wspec(kd),
                  pl.BlockSpec((TM, D), lambda j, i: (i, 0))] + [ANY] * nr,
        out_specs=[gspec, gspec, gspec, pl.BlockSpec((1, TM, D), lambda j, i: (j, i, 0))] + [ANY] * nr,
        out_shape=[SDS((NCH, FC, D), bf16)] * 3 + [SDS((NCH, S, D), bf16)] + [SDS(r.shape, r.dtype) for r in rides],
        scratch_shapes=[pltpu.VMEM((FC, D), f32)] * 3 + (_scatter_sems(nr) if nr else []),
        compiler_params=_cp(56),
    )(x, nw, blob, blob, blob, dy, *rides)


def _norm_bwd(x, nw, dres, dh4):
    nt = S // TM
    nparts = dh4.shape[0]

    def body(x_ref, nw_ref, dres_ref, dh_ref, dx_ref, dnw_ref):
        i = pl.program_id(0)
        dh = dh_ref[0].astype(f32)
        for p in range(1, nparts):
            dh = dh + dh_ref[p].astype(f32)
        xv = x_ref[...]
        _, r = _rms(xv, nw_ref[...])
        dx, dw = _rms_bwd(xv, nw_ref[...], r, dh)
        dx_ref[...] = dres_ref[...] + dx

        @pl.when(i == 0)
        def _():
            dnw_ref[...] = jnp.zeros_like(dnw_ref)

        dnw_ref[...] += dw

    return _pc(
        body, grid=(nt,), name="norm_bwd",
        in_specs=[pl.BlockSpec((TM, D), lambda i: (i, 0)),
                  pl.BlockSpec((1, D), lambda i: (0, 0)),
                  pl.BlockSpec((TM, D), lambda i: (i, 0)),
                  pl.BlockSpec((nparts, TM, D), lambda i: (0, i, 0))],
        out_specs=[pl.BlockSpec((TM, D), lambda i: (i, 0)), pl.BlockSpec((1, D), lambda i: (0, 0))],
        out_shape=[SDS((S, D), f32), SDS((1, D), f32)],
        compiler_params=_cp(40),
    )(x, nw, dres, dh4)


def _final(x, nw, target):
    nt = S // TM

    def body(x_ref, nw_ref, t_ref, dx_ref, dnw_ref, loss_ref):
        i = pl.program_id(0)
        xv = x_ref[...]
        y, r = _rms(xv, nw_ref[...])
        err = y - t_ref[...]
        part = 0.5 * jnp.sum(jnp.mean(err * err, axis=-1, keepdims=True), axis=0, keepdims=True)
        dx, dw = _rms_bwd(xv, nw_ref[...], r, err * (1.0 / D))
        dx_ref[...] = dx

        @pl.when(i == 0)
        def _():
            dnw_ref[...] = jnp.zeros_like(dnw_ref)
            loss_ref[...] = jnp.zeros_like(loss_ref)

        dnw_ref[...] += dw
        loss_ref[...] += jnp.broadcast_to(part, loss_ref.shape)

    return _pc(
        body, grid=(nt,), name="final_loss",
        in_specs=[pl.BlockSpec((TM, D), lambda i: (i, 0)),
                  pl.BlockSpec((1, D), lambda i: (0, 0)),
                  pl.BlockSpec((TM, D), lambda i: (i, 0))],
        out_specs=[pl.BlockSpec((TM, D), lambda i: (i, 0)), pl.BlockSpec((1, D), lambda i: (0, 0)),
                   pl.BlockSpec((1, 128), lambda i: (0, 0))],
        out_shape=[SDS((S, D), f32), SDS((1, D), f32), SDS((1, 128), f32)],
        compiler_params=_cp(40),
    )(x, nw, target)


def _rot_half(t):
    lane = lax.broadcasted_iota(jnp.int32, t.shape, 1)
    first = (lane % EH) < (EH // 2)
    return jnp.where(first, -pltpu.roll(t, ATT - EH // 2, 1), pltpu.roll(t, EH // 2, 1))


def _rope_tables(pos_ref, freq_ref):
    ang = pos_ref[...].astype(f32) * freq_ref[...]
    return jnp.cos(ang), jnp.sin(ang)


def _split_residues(val, scr, outs):
    rows, cols = val.shape
    for j in range(cols // 128):
        scr[j] = val[:, 128 * j:128 * (j + 1)]
    for ref, d in outs:
        for j in range(cols // 128):
            for r in range(d):
                ref.at[r][:, 128 * j:128 * (j + 1)] = scr.at[j][pl.ds(r, rows // d, stride=d), :]


def _join_residues(ref, d, scr):
    rows, cols = scr.shape[1], ref.shape[2]
    for j in range(cols // 128):
        for r in range(d):
            scr.at[j][pl.ds(r, rows // d, stride=d), :] = ref.at[r][:, 128 * j:128 * (j + 1)]
    return jnp.concatenate([scr[j] for j in range(cols // 128)], axis=1)


def _res_spec(d, tile, cols):
    return pl.BlockSpec((d, tile // d, cols), lambda i: (0, i, 0))


def _inproj_fwd(x, nw, w_aug, pos, freq):
    TI = 512

    def body(x_ref, nw_ref, w_hbm, pos_ref, freq_ref, att_ref, att4_ref, att16_ref, pu_ref, dq_ref, dz_ref, dba_ref,
             w_scr, r_scr):
        @pl.when(pl.program_id(0) == 0)
        def _():
            pltpu.sync_copy(w_hbm, w_scr)

        h, _ = _rms(x_ref[...], nw_ref[...])
        proj = _dot(h.astype(bf16), w_scr[...])
        cos, sin = _rope_tables(pos_ref, freq_ref)
        q = proj[:, 0:ATT]
        k = proj[:, ATT:2 * ATT]
        att = jnp.concatenate([q * cos + _rot_half(q) * sin, k * cos + _rot_half(k) * sin, proj[:, 2 * ATT:3 * ATT]], axis=1)
        att_ref[...] = att
        _split_residues(att, r_scr, [(att4_ref, 4), (att16_ref, 16)])
        pu_ref[...] = proj[:, 768:1024]
        dq_ref[...] = proj[:, 1024:2560]
        dz_ref[...] = proj[:, 2560:3072]
        dba_ref[...] = proj[:, 3072:3200]

    return _pc(
        body, grid=(S // TI,), name="inproj_fwd",
        in_specs=[pl.BlockSpec((TI, D), lambda i: (i, 0)),
                  pl.BlockSpec((1, D), lambda i: (0, 0)),
                  pl.BlockSpec(memory_space=pl.ANY),
                  pl.BlockSpec((TI, 1), lambda i: (i, 0)),
                  pl.BlockSpec((1, ATT), lambda i: (0, 0))],
        out_specs=[pl.BlockSpec((TI, 768), lambda i: (i, 0)), _res_spec(4, TI, 768), _res_spec(16, TI, 768),
                   pl.BlockSpec((TI, 256), lambda i: (i, 0)),
                   pl.BlockSpec((TI, 1536), lambda i: (i, 0)), pl.BlockSpec((TI, 512), lambda i: (i, 0)),
                   pl.BlockSpec((TI, 128), lambda i: (i, 0))],
        out_shape=[SDS((S, 768), f32), SDS((4, S // 4, 768), f32), SDS((16, S // 16, 768), f32), SDS((S, 256), f32),
                   SDS((S, 1536), f32), SDS((S, 512), f32), SDS((S, 128), f32)],
        scratch_shapes=[pltpu.VMEM((D, INP), bf16), pltpu.VMEM((6, TI, 128), f32)],
        compiler_params=_cp(56),
    )(x, nw, w_aug, pos, freq)


def _inproj_bwd(x, nw, w_aug, pos, freq, dres, datt, datt4, datt16, dpu, ddq, ddz, ddba):
    TI = 256
    nt = S // TI

    def body(x_ref, nw_ref, w_hbm, pos_ref, freq_ref, dres_ref, datt_ref, datt4_ref, datt16_ref, dpu_ref, ddq_ref, ddz_ref,
             ddba_ref, dx_ref, dnw_ref, dw_hbm, w_scr, acc, r_scr):
        i = pl.program_id(0)

        @pl.when(i == 0)
        def _():
            pltpu.sync_copy(w_hbm, w_scr)
            acc[...] = jnp.zeros_like(acc)
            dnw_ref[...] = jnp.zeros_like(dnw_ref)

        xv = x_ref[...]
        hf, r = _rms(xv, nw_ref[...])
        h = hf.astype(bf16)
        cos, sin = _rope_tables(pos_ref, freq_ref)
        datt = datt_ref[...] + _join_residues(datt4_ref, 4, r_scr)
        datt = datt + _join_residues(datt16_ref, 16, r_scr)
        dq = datt[:, 0:ATT]
        dk = datt[:, ATT:2 * ATT]
        dq = dq * cos - _rot_half(dq) * sin
        dk = dk * cos - _rot_half(dk) * sin
        dproj = jnp.concatenate([dq, dk, datt[:, 2 * ATT:3 * ATT], dpu_ref[...], ddq_ref[...], ddz_ref[...], ddba_ref[...]],
                                axis=1).astype(bf16)
        acc[...] += _dot_tn(h, dproj)
        dh = _dot_nt(dproj, w_scr[...])
        dx, dw = _rms_bwd(xv, nw_ref[...], r, dh)
        dx_ref[...] = dres_ref[...] + dx
        dnw_ref[...] += dw

        @pl.when(i == nt - 1)
        def _():
            pltpu.sync_copy(acc, dw_hbm)

    return _pc(
        body, grid=(nt,), name="inproj_bwd",
        in_specs=[pl.BlockSpec((TI, D), lambda i: (i, 0)),
                  pl.BlockSpec((1, D), lambda i: (0, 0)),
                  pl.BlockSpec(memory_space=pl.ANY),
                  pl.BlockSpec((TI, 1), lambda i: (i, 0)),
                  pl.BlockSpec((1, ATT), lambda i: (0, 0)),
                  pl.BlockSpec((TI, D), lambda i: (i, 0)),
                  pl.BlockSpec((TI, 768), lambda i: (i, 0)), _res_spec(4, TI, 768), _res_spec(16, TI, 768),
                  pl.BlockSpec((TI, 256), lambda i: (i, 0)),
                  pl.BlockSpec((TI, 1536), lambda i: (i, 0)),
                  pl.BlockSpec((TI, 512), lambda i: (i, 0)),
                  pl.BlockSpec((TI, 128), lambda i: (i, 0))],
        out_specs=[pl.BlockSpec((TI, D), lambda i: (i, 0)), pl.BlockSpec((1, D), lambda i: (0, 0)),
                   pl.BlockSpec(memory_space=pl.ANY)],
        out_shape=[SDS((S, D), f32), SDS((1, D), f32), SDS((D, INP), f32)],
        scratch_shapes=[pltpu.VMEM((D, INP), bf16), pltpu.VMEM((D, INP), f32), pltpu.VMEM((6, TI, 128), f32)],
        compiler_params=_cp(56),
    )(x, nw, w_aug, pos, freq, dres, datt, datt4, datt16, dpu, ddq, ddz, ddba)


def _outproj_fwd(x, ya, yb, yc, blob_b, kw):
    def body(x_ref, ya_ref, yb_ref, yc_ref, w_ref, o_ref):
        ycat = jnp.concatenate([ya_ref[...], yb_ref[...], yc_ref[...]], axis=1).astype(bf16)
        o_ref[...] = x_ref[...] + _dot(ycat, w_ref[:, 0:256, :].reshape(D, D))

    return _pc(
        body, grid=(S // TM,), name="outproj_fwd",
        in_specs=[pl.BlockSpec((TM, D), lambda i: (i, 0)),
                  pl.BlockSpec((TM, 256), lambda i: (i, 0)),
                  pl.BlockSpec((TM, 256), lambda i: (i, 0)),
                  pl.BlockSpec((TM, 512), lambda i: (i, 0)),
                  pl.BlockSpec((NCH, FC, D), lambda i: (0, kw, 0))],
        out_specs=pl.BlockSpec((TM, D), lambda i: (i, 0)),
        out_shape=SDS((S, D), f32),
        compiler_params=_cp(40),
    )(x, ya, yb, yc, blob_b)


def _outproj_bwd(x, nw, dres, dh4, ya, yb, yc, blob_b, kw):
    nt = S // TM
    nparts = dh4.shape[0]

    def body(x_ref, nw_ref, dres_ref, dh_ref, ya_ref, yb_ref, yc_ref, w_ref, dx_ref, dnw_ref, dya_ref, dyb_ref, dyc_ref, dw_ref):
        i = pl.program_id(0)

        @pl.when(i == 0)
        def _():
            dw_ref[...] = jnp.zeros_like(dw_ref)
            dnw_ref[...] = jnp.zeros_like(dnw_ref)

        dh = dh_ref[0].astype(f32)
        for p in range(1, nparts):
            dh = dh + dh_ref[p].astype(f32)
        xv = x_ref[...]
        _, r = _rms(xv, nw_ref[...])
        dxn, dnw = _rms_bwd(xv, nw_ref[...], r, dh)
        dx = dres_ref[...] + dxn
        dx_ref[...] = dx
        dnw_ref[...] += dnw
        dyv = dx.astype(bf16)
        ycat = jnp.concatenate([ya_ref[...], yb_ref[...], yc_ref[...]], axis=1).astype(bf16)
        dw_ref[...] += _dot_tn(ycat, dyv)
        dcat = _dot_nt(dyv, w_ref[:, 0:256, :].reshape(D, D))
        dya_ref[...] = dcat[:, 0:256]
        dyb_ref[...] = dcat[:, 256:512]
        dyc_ref[...] = dcat[:, 512:1024]

    return _pc(
        body, grid=(nt,), name="outproj_bwd",
        in_specs=[pl.BlockSpec((TM, D), lambda i: (i, 0)),
                  pl.BlockSpec((1, D), lambda i: (0, 0)),
                  pl.BlockSpec((TM, D), lambda i: (i, 0)),
                  pl.BlockSpec((nparts, TM, D), lambda i: (0, i, 0)),
                  pl.BlockSpec((TM, 256), lambda i: (i, 0)),
                  pl.BlockSpec((TM, 256), lambda i: (i, 0)),
                  pl.BlockSpec((TM, 512), lambda i: (i, 0)),
                  pl.BlockSpec((NCH, FC, D), lambda i: (0, kw, 0))],
        out_specs=[pl.BlockSpec((TM, D), lambda i: (i, 0)), pl.BlockSpec((1, D), lambda i: (0, 0)),
                   pl.BlockSpec((TM, 256), lambda i: (i, 0)), pl.BlockSpec((TM, 256), lambda i: (i, 0)),
                   pl.BlockSpec((TM, 512), lambda i: (i, 0)), pl.BlockSpec((D, D), lambda i: (0, 0))],
        out_shape=[SDS((S, D), f32), SDS((1, D), f32), SDS((S, 256), f32), SDS((S, 256), f32), SDS((S, 512), f32),
                   SDS((D, D), f32)],
        compiler_params=_cp(48),
    )(x, nw, dres, dh4, ya, yb, yc, blob_b)


QT = NBLK
NB = S // QT


def _attn_block(q, kp, kc, vp, vc, first):
    kk = jnp.concatenate([kp, kc], axis=0).astype(bf16)
    vv = jnp.concatenate([vp, vc], axis=0).astype(bf16)
    qi = lax.broadcasted_iota(jnp.int32, (4 * QT, NBLK + QT), 0) % QT
    ki = lax.broadcasted_iota(jnp.int32, (4 * QT, NBLK + QT), 1)
    dist = NBLK + qi - ki
    valid = (dist >= 0) & (dist <= NBLK) & (jnp.logical_not(first) | (ki >= NBLK))
    head = lax.broadcasted_iota(jnp.int32, (1, ATT), 1) // EH
    masks = [(head == h).astype(f32) for h in range(4)]
    qs = jnp.concatenate([q * (mh * (1.0 / math.sqrt(EH))) for mh in masks], axis=0).astype(bf16)
    s = _dot_nt(qs, kk)
    s = jnp.where(valid, s, NEG)
    m = lax.stop_gradient(jnp.max(s, axis=-1, keepdims=True))
    p = jnp.exp(s - m)
    den = jnp.sum(p, axis=-1, keepdims=True)
    po = _dot((p * (1.0 / den)).astype(bf16), vv)
    lse = m + jnp.log(den)
    o = jnp.zeros((QT, ATT), f32)
    l = jnp.zeros((QT, ATT), f32)
    for h, mh in enumerate(masks):
        o = o + po[QT * h:QT * (h + 1)] * mh
        l = l + lse[QT * h:QT * (h + 1)] * mh
    return o, l


def _attn_specs(tile):
    own = lambda col: pl.BlockSpec((QT, ATT), lambda s: (tile(s), col))
    prev = lambda col: pl.BlockSpec((NBLK, ATT), lambda s: (jnp.maximum((QT // NBLK) * tile(s) - 1, 0), col))
    return [own(0), prev(1), own(1), prev(2), own(2)]


def _attn_fwd(qkv, per_seq):
    def body(q_ref, kp_ref, kc_ref, vp_ref, vc_ref, o_ref, l_ref):
        first = pl.program_id(0) % per_seq == 0
        o, l = _attn_block(q_ref[...], kp_ref[...], kc_ref[...], vp_ref[...], vc_ref[...], first)
        o_ref[...] = o
        l_ref[...] = l

    blk = pl.BlockSpec((QT, ATT), lambda t: (t, 0))
    return _pc(
        body, grid=(NB,), name="attn_fwd", in_specs=_attn_specs(lambda t: t), out_specs=[blk, blk],
        out_shape=[SDS((S, ATT), f32), SDS((S, ATT), f32)], compiler_params=_cp(32),
    )(qkv, qkv, qkv, qkv, qkv)


def _attn_block_bwd(q, kp, kc, vp, vc, o, l, do, dl, first):
    kk = jnp.concatenate([kp, kc], axis=0).astype(bf16)
    vv = jnp.concatenate([vp, vc], axis=0).astype(bf16)
    qi = lax.broadcasted_iota(jnp.int32, (4 * QT, NBLK + QT), 0) % QT
    ki = lax.broadcasted_iota(jnp.int32, (4 * QT, NBLK + QT), 1)
    dist = NBLK + qi - ki
    valid = (dist >= 0) & (dist <= NBLK) & (jnp.logical_not(first) | (ki >= NBLK))
    head = lax.broadcasted_iota(jnp.int32, (1, ATT), 1) // EH
    masks = [(head == h).astype(f32) for h in range(4)]
    scale = 1.0 / math.sqrt(EH)
    stack = lambda f: jnp.concatenate([f(mh) for mh in masks], axis=0)
    qs = stack(lambda mh: q * (mh * scale)).astype(bf16)
    s = jnp.where(valid, _dot_nt(qs, kk), NEG)
    lse = stack(lambda mh: jnp.max(jnp.where(mh > 0.0, l, NEG), axis=1, keepdims=True))
    p = jnp.exp(s - lse)
    dos = stack(lambda mh: do * mh).astype(bf16)
    dvv = _dot_tn(p.astype(bf16), dos)
    dp = _dot_nt(dos, vv)
    delta = stack(lambda mh: jnp.sum(do * o * mh, axis=1, keepdims=True))
    dlse = stack(lambda mh: jnp.sum(dl * mh, axis=1, keepdims=True))
    ds = (p * (dp - delta + dlse)).astype(bf16)
    dqs = _dot(ds, kk)
    dq = jnp.zeros((QT, ATT), f32)
    for h, mh in enumerate(masks):
        dq = dq + dqs[QT * h:QT * (h + 1)] * (mh * scale)
    dkk = _dot_tn(ds, qs)
    return dq, dkk[:NBLK], dkk[NBLK:], dvv[:NBLK], dvv[NBLK:]


def _attn_bwd(qkv, o, l, do, dl, per_seq):
    def body(q_ref, kp_ref, kc_ref, vp_ref, vc_ref, ofw_ref, lfw_ref, do_ref, dl_ref, o_ref, k_carry, v_carry):
        step = pl.program_id(0)

        @pl.when(step == 0)
        def _():
            k_carry[...] = jnp.zeros_like(k_carry)
            v_carry[...] = jnp.zeros_like(v_carry)

        t = NB - 1 - step
        first = t % per_seq == 0
        last = t % per_seq == per_seq - 1
        dq, dkp, dkc, dvp, dvc = _attn_block_bwd(q_ref[...], kp_ref[...], kc_ref[...], vp_ref[...], vc_ref[...], ofw_ref[...],
                                                 lfw_ref[...], do_ref[...], dl_ref[...], first)
        o_ref[:, 0:ATT] = dq
        o_ref[:, ATT:2 * ATT] = dkc
        o_ref[:, 2 * ATT:3 * ATT] = dvc
        o_ref[QT - NBLK:QT, ATT:2 * ATT] += jnp.where(last, 0.0, k_carry[...])
        o_ref[QT - NBLK:QT, 2 * ATT:3 * ATT] += jnp.where(last, 0.0, v_carry[...])
        k_carry[...] = dkp
        v_carry[...] = dvp

    rev = lambda s: NB - 1 - s
    blk = pl.BlockSpec((QT, ATT), lambda s: (rev(s), 0))
    return _pc(
        body, grid=(NB,), name="attn_bwd", in_specs=_attn_specs(rev) + [blk, blk, blk, blk],
        out_specs=pl.BlockSpec((QT, 768), lambda s: (rev(s), 0)),
        out_shape=SDS((S, 768), f32), scratch_shapes=[pltpu.VMEM((NBLK, ATT), f32)] * 2, compiler_params=_cp(40),
    )(qkv, qkv, qkv, qkv, qkv, o, l, do, dl)


def _merge_weights(l0, l1, l2):
    m = jnp.maximum(jnp.maximum(l0, l1), l2)
    e0, e1, e2 = jnp.exp(l0 - m), jnp.exp(l1 - m), jnp.exp(l2 - m)
    tot = e0 + e1 + e2
    return e0 / tot, e1 / tot, e2 / tot


def _merge_specs():
    nat = pl.BlockSpec((TM, ATT), lambda i: (i, 0))
    return nat, _res_spec(4, TM, ATT), _res_spec(16, TM, ATT)


def _merge_fwd(o1, l1, o4, l4, o16, l16):
    def body(o1_ref, l1_ref, o4_ref, l4_ref, o16_ref, l16_ref, y_ref, scr):
        o4v, l4v = _join_residues(o4_ref, 4, scr), _join_residues(l4_ref, 4, scr)
        o16v, l16v = _join_residues(o16_ref, 16, scr), _join_residues(l16_ref, 16, scr)
        w0, w1, w2 = _merge_weights(l1_ref[...], l4v, l16v)
        y_ref[...] = w0 * o1_ref[...] + w1 * o4v + w2 * o16v

    nat, r4, r16 = _merge_specs()
    return _pc(body, grid=(S // TM,), name="merge_fwd", in_specs=[nat, nat, r4, r4, r16, r16],
                          out_specs=nat, out_shape=SDS((S, ATT), f32), scratch_shapes=[pltpu.VMEM((2, TM, 128), f32)],
                          compiler_params=_cp(32))(o1, l1, o4, l4, o16, l16)


def _merge_bwd(o1, l1, o4, l4, o16, l16, dy):
    def body(o1_ref, l1_ref, o4_ref, l4_ref, o16_ref, l16_ref, dy_ref, do1_ref, dl1_ref, do4_ref, dl4_ref, do16_ref, dl16_ref, scr):
        o4v, l4v = _join_residues(o4_ref, 4, scr), _join_residues(l4_ref, 4, scr)
        o16v, l16v = _join_residues(o16_ref, 16, scr), _join_residues(l16_ref, 16, scr)
        o1v = o1_ref[...]
        w0, w1, w2 = _merge_weights(l1_ref[...], l4v, l16v)
        y = w0 * o1v + w1 * o4v + w2 * o16v
        dyv = dy_ref[...]
        do1_ref[...] = w0 * dyv
        dl1_ref[...] = w0 * (o1v - y) * dyv
        _split_residues(w1 * dyv, scr, [(do4_ref, 4)])
        _split_residues(w1 * (o4v - y) * dyv, scr, [(dl4_ref, 4)])
        _split_residues(w2 * dyv, scr, [(do16_ref, 16)])
        _split_residues(w2 * (o16v - y) * dyv, scr, [(dl16_ref, 16)])

    nat, r4, r16 = _merge_specs()
    return _pc(body, grid=(S // TM,), name="merge_bwd", in_specs=[nat, nat, r4, r4, r16, r16, nat],
                          out_specs=[nat, nat, r4, r4, r16, r16],
                          out_shape=[SDS((S, ATT), f32)] * 2 + [SDS((4, S // 4, ATT), f32)] * 2 + [SDS((16, S // 16, ATT), f32)] * 2,
                          scratch_shapes=[pltpu.VMEM((2, TM, 128), f32)], compiler_params=_cp(32))(o1, l1, o4, l4, o16, l16, dy)


HALO = 16


def _pool_consts(i, rows):
    grp = lax.broadcasted_iota(jnp.int32, (rows, 256), 1) // 64
    t = i * TM + lax.broadcasted_iota(jnp.int32, (rows, 256), 0)
    win = jnp.where(grp == 0, 2, jnp.where(grp == 1, 4, jnp.where(grp == 2, 8, 16)))
    cnt = jnp.minimum(t + 1, win).astype(f32)
    return grp, cnt


def _pool_select(grp, s2, s4, s8, s16):
    return jnp.where(grp == 0, s2, jnp.where(grp == 1, s4, jnp.where(grp == 2, s8, s16)))


def _pooled(i, cur, halo):
    xx = jnp.concatenate([halo, cur], axis=0)
    s2 = xx + pltpu.roll(xx, 1, 0)
    s4 = s2 + pltpu.roll(s2, 2, 0)
    s8 = s4 + pltpu.roll(s4, 4, 0)
    s16 = s8 + pltpu.roll(s8, 8, 0)
    grp, cnt = _pool_consts(i, TM)
    tot = _pool_select(grp, s2[HALO:], s4[HALO:], s8[HALO:], s16[HALO:])
    return tot / cnt - cur


def _pool_fwd(u, wp, scale):
    def body(u_ref, halo_ref, wp_ref, sc_ref, y_ref):
        i = pl.program_id(0)
        halo = halo_ref[...] * (i > 0).astype(f32)
        pooled = _pooled(i, u_ref[...], halo)
        y_ref[...] = _dot(pooled.astype(bf16), wp_ref[...]) * sc_ref[...]

    return _pc(
        body, grid=(S // TM,), name="pool_fwd",
        in_specs=[pl.BlockSpec((TM, 256), lambda i: (i, 0)),
                  pl.BlockSpec((HALO, 256), lambda i: (jnp.maximum(i * (TM // HALO) - 1, 0), 0)),
                  pl.BlockSpec((256, 256), lambda i: (0, 0)),
                  pl.BlockSpec((1, 256), lambda i: (0, 0))],
        out_specs=pl.BlockSpec((TM, 256), lambda i: (i, 0)), out_shape=SDS((S, 256), f32), compiler_params=_cp(32),
    )(u, u, wp, scale)


def _pool_bwd(u, wp, scale, dy):
    nt = S // TM

    def body(u_ref, halo_ref, wp_ref, sc_ref, dy_ref, dyn_ref, du_ref, dwp_ref, dsc_ref):
        i = pl.program_id(0)

        @pl.when(i == 0)
        def _():
            dwp_ref[...] = jnp.zeros_like(dwp_ref)
            dsc_ref[...] = jnp.zeros_like(dsc_ref)

        halo = halo_ref[...] * (i > 0).astype(f32)
        pooled = _pooled(i, u_ref[...], halo).astype(bf16)
        dyv = dy_ref[...]
        dsc_ref[...] += jnp.sum(dyv * _dot(pooled, wp_ref[...]), axis=0, keepdims=True)
        dys = (dyv * sc_ref[...]).astype(bf16)
        dwp_ref[...] += _dot_tn(pooled, dys)
        dpool = _dot_nt(dys, wp_ref[...])
        grp, cnt = _pool_consts(i, TM)
        dyn = ((dyn_ref[...] * (i < nt - 1).astype(f32)) * sc_ref[...]).astype(bf16)
        _, cntn = _pool_consts(i + 1, HALO)
        zn = _dot_nt(dyn, wp_ref[...]) / cntn
        zz = jnp.concatenate([dpool / cnt, zn], axis=0)
        n = TM + HALO
        a2 = zz + pltpu.roll(zz, n - 1, 0)
        a4 = a2 + pltpu.roll(a2, n - 2, 0)
        a8 = a4 + pltpu.roll(a4, n - 4, 0)
        a16 = a8 + pltpu.roll(a8, n - 8, 0)
        du_ref[...] = _pool_select(grp, a2[:TM], a4[:TM], a8[:TM], a16[:TM]) - dpool

    return _pc(
        body, grid=(nt,), name="pool_bwd",
        in_specs=[pl.BlockSpec((TM, 256), lambda i: (i, 0)),
                  pl.BlockSpec((HALO, 256), lambda i: (jnp.maximum(i * (TM // HALO) - 1, 0), 0)),
                  pl.BlockSpec((256, 256), lambda i: (0, 0)),
                  pl.BlockSpec((1, 256), lambda i: (0, 0)),
                  pl.BlockSpec((TM, 256), lambda i: (i, 0)),
                  pl.BlockSpec((HALO, 256), lambda i: (jnp.minimum((i + 1) * (TM // HALO), S // HALO - 1), 0))],
        out_specs=[pl.BlockSpec((TM, 256), lambda i: (i, 0)), pl.BlockSpec((256, 256), lambda i: (0, 0)),
                   pl.BlockSpec((1, 256), lambda i: (0, 0))],
        out_shape=[SDS((S, 256), f32), SDS((256, 256), f32), SDS((1, 256), f32)], compiler_params=_cp(32),
    )(u, u, wp, scale, dy, dy)


CW = 3 * DNW
CHALO = 8
TC = 256


def _conv_fwd(u, w):
    def body(u_ref, halo_ref, w_ref, c_ref):
        i = pl.program_id(0)
        xx = jnp.concatenate([halo_ref[...] * (i > 0).astype(f32), u_ref[...]], axis=0)
        c = (w_ref[3:4, :] * xx + w_ref[2:3, :] * pltpu.roll(xx, 1, 0) + w_ref[1:2, :] * pltpu.roll(xx, 2, 0)
             + w_ref[0:1, :] * pltpu.roll(xx, 3, 0))
        c_ref[...] = c[CHALO:]

    return _pc(
        body, grid=(S // TC,), name="conv_fwd",
        in_specs=[pl.BlockSpec((TC, CW), lambda i: (i, 0)),
                  pl.BlockSpec((CHALO, CW), lambda i: (jnp.maximum(i * (TC // CHALO) - 1, 0), 0)),
                  pl.BlockSpec((8, CW), lambda i: (0, 0))],
        out_specs=pl.BlockSpec((TC, CW), lambda i: (i, 0)), out_shape=SDS((S, CW), f32), compiler_params=_cp(32),
    )(u, u, w)


def _conv_bwd(u, w, dc):
    nt = S // TC

    def body(u_ref, halo_ref, w_ref, dc_ref, dcn_ref, du_ref, dw_ref):
        i = pl.program_id(0)

        @pl.when(i == 0)
        def _():
            dw_ref[...] = jnp.zeros_like(dw_ref)

        dcv = dc_ref[...]
        zz = jnp.concatenate([dcv, dcn_ref[...] * (i < nt - 1).astype(f32)], axis=0)
        n = TC + CHALO
        du = (w_ref[3:4, :] * zz + w_ref[2:3, :] * pltpu.roll(zz, n - 1, 0) + w_ref[1:2, :] * pltpu.roll(zz, n - 2, 0)
              + w_ref[0:1, :] * pltpu.roll(zz, n - 3, 0))
        du_ref[...] = du[:TC]
        xx = jnp.concatenate([halo_ref[...] * (i > 0).astype(f32), u_ref[...]], axis=0)
        for j in range(4):
            shifted = xx if j == 3 else pltpu.roll(xx, 3 - j, 0)
            dw_ref[j:j + 1, :] += jnp.sum(dcv * shifted[CHALO:], axis=0, keepdims=True)

    return _pc(
        body, grid=(nt,), name="conv_bwd",
        in_specs=[pl.BlockSpec((TC, CW), lambda i: (i, 0)),
                  pl.BlockSpec((CHALO, CW), lambda i: (jnp.maximum(i * (TC // CHALO) - 1, 0), 0)),
                  pl.BlockSpec((8, CW), lambda i: (0, 0)),
                  pl.BlockSpec((TC, CW), lambda i: (i, 0)),
                  pl.BlockSpec((CHALO, CW), lambda i: (jnp.minimum((i + 1) * (TC // CHALO), S // CHALO - 1), 0))],
        out_specs=[pl.BlockSpec((TC, CW), lambda i: (i, 0)), pl.BlockSpec((8, CW), lambda i: (0, 0))],
        out_shape=[SDS((S, CW), f32), SDS((8, CW), f32)], compiler_params=_cp(32),
    )(u, u, w, dc, dc)


TL = 512
NCL = TL // CH


def _ein(spec, a, b):
    return jnp.einsum(spec, a.astype(bf16), b.astype(bf16), preferred_element_type=f32)


def _ein_ct(spec, x, y, ct_first):
    ct = x if ct_first else y
    hi = ct.astype(bf16)
    lo = ct - hi.astype(f32)
    if ct_first:
        return _ein(spec, hi, y) + _ein(spec, lo, y)
    return _ein(spec, x, hi) + _ein(spec, x, lo)


def _bf16_dot(spec, grad_a, grad_b):
    @jax.custom_vjp
    def dot(a, b):
        return _ein(spec, a, b)

    def fwd(a, b):
        return _ein(spec, a, b), (a, b)

    def bwd(res, ct):
        a, b = res
        return grad_a(a, b, ct), grad_b(a, b, ct)

    dot.defvjp(fwd, bwd)
    return dot


def _bdot(a, b):
    return _ein('nik,nkj->nij', a, b)


def _bdot_nt(a, b):
    return _ein('nik,njk->nij', a, b)


def _bdot_tn(a, b):
    return _ein('nki,nkj->nij', a, b)


_mm = _bf16_dot('ik,kj->ij', lambda a, b, ct: _ein_ct('ij,kj->ik', ct, b, True), lambda a, b, ct: _ein_ct('ik,ij->kj', a, ct, False))
_mm_tn = _bf16_dot('ki,kj->ij', lambda a, b, ct: _ein_ct('kj,ij->ki', b, ct, False),
                   lambda a, b, ct: _ein_ct('ki,ij->kj', a, ct, False))


@jax.custom_vjp
def _inv_unit_lower(a):
    ii = lax.broadcasted_iota(jnp.int32, (1, CH, CH), 1)
    jj = lax.broadcasted_iota(jnp.int32, (1, CH, CH), 2)
    t = (ii == jj).astype(f32) - a
    p = a
    for _ in range(5):
        p = _bdot(p, p)
        t = t + _bdot(t, p)
    return t


def _inv_unit_lower_fwd(a):
    t = _inv_unit_lower(a)
    return t, t


def _inv_unit_lower_bwd(t, dt):
    return (-_bdot_tn(t, _bdot_nt(dt, t)),)


_inv_unit_lower.defvjp(_inv_unit_lower_fwd, _inv_unit_lower_bwd)


def _dn_local(c, dba, a_row, b_row):
    act = c * jax.nn.sigmoid(c)
    lane = lax.broadcasted_iota(jnp.int32, (TL, 128), 1)
    beta_all = jax.nn.sigmoid(dba)
    xs = dba + b_row
    softplus = jnp.maximum(xs, 0.0) + jnp.log(1.0 + jnp.exp(-jnp.abs(xs)))
    g_all = -jnp.exp(a_row) * softplus
    ii = lax.broadcasted_iota(jnp.int32, (1, CH, CH), 1)
    jj = lax.broadcasted_iota(jnp.int32, (1, CH, CH), 2)
    lower = jj <= ii
    strict = jj < ii
    eye = (ii == jj).astype(f32)
    us, ws, qgs, kds, intras = [], [], [], [], []
    aux = jnp.zeros((TL, 128), f32)
    for h in range(4):
        q = act[:, DH * h:DH * (h + 1)]
        k = act[:, DNW + DH * h:DNW + DH * (h + 1)]
        v = act[:, 2 * DNW + DH * h:2 * DNW + DH * (h + 1)]
        q = q * lax.rsqrt(jnp.sum(q * q, axis=-1, keepdims=True) + EPS) * (DH ** -0.5)
        k = k * lax.rsqrt(jnp.sum(k * k, axis=-1, keepdims=True) + EPS)
        beta = jnp.sum(jnp.where(lane == h, beta_all, 0.0), axis=1, keepdims=True)
        g = jnp.sum(jnp.where(lane == 4 + h, g_all, 0.0), axis=1, keepdims=True)
        q3, k3, v3 = q.reshape(NCL, CH, DH), k.reshape(NCL, CH, DH), v.reshape(NCL, CH, DH)
        beta3, g3 = beta.reshape(NCL, CH, 1), g.reshape(NCL, CH, 1)
        g_row = jnp.sum(eye * g3, axis=1, keepdims=True)
        gc_col = jnp.sum(jnp.where(lower, g_row, 0.0), axis=2, keepdims=True)
        gc_row = jnp.sum(jnp.where(ii <= jj, g3, 0.0), axis=1, keepdims=True)
        diff = gc_col - gc_row
        decay = jnp.where(lower, jnp.exp(jnp.where(lower, diff, 0.0)), 0.0)
        kb = k3 * beta3
        vb = v3 * beta3
        a = jnp.where(strict, _bdot_nt(kb, k3) * decay, 0.0)
        t = _inv_unit_lower(a)
        u3 = _bdot(t, vb)
        w3 = _bdot(t, kb * jnp.exp(gc_col))
        intra = jnp.where(lower, _bdot_nt(q3, k3) * decay, 0.0)
        g_last = jnp.sum(g3, axis=1, keepdims=True)
        us.append(u3.reshape(TL, DH))
        ws.append(w3.reshape(TL, DH))
        qgs.append((q3 * jnp.exp(gc_col)).reshape(TL, DH))
        kds.append((k3 * jnp.exp(g_last - gc_col)).reshape(TL, DH))
        intras.append(intra.reshape(TL, CH))
        e_last = jnp.broadcast_to(jnp.exp(g_last), (NCL, CH, 1)).reshape(TL, 1)
        aux = aux + jnp.where(lane == h, e_last, 0.0)
    cat = lambda xs: jnp.concatenate(xs, axis=1)
    return cat(us), cat(ws), cat(qgs), cat(kds), jnp.stack(intras, axis=0), aux


def _dn_local_fwd(c, dba, par):
    def body(c_ref, dba_ref, par_ref, u_ref, w_ref, qg_ref, kd_ref, in_ref, aux_ref):
        u, w, qg, kd, intra, aux = _dn_local(c_ref[...], dba_ref[...], par_ref[0:1, :], par_ref[1:2, :])
        u_ref[...] = u
        w_ref[...] = w
        qg_ref[...] = qg
        kd_ref[...] = kd
        in_ref[...] = intra
        aux_ref[...] = aux

    wide = pl.BlockSpec((TL, DNW), lambda i: (i, 0))
    return _pc(
        body, grid=(S // TL,), name="dn_local_fwd",
        in_specs=[pl.BlockSpec((TL, CW), lambda i: (i, 0)), pl.BlockSpec((TL, 128), lambda i: (i, 0)),
                  pl.BlockSpec((8, 128), lambda i: (0, 0))],
        out_specs=[wide, wide, wide, wide, pl.BlockSpec((4, TL, CH), lambda i: (0, i, 0)),
                   pl.BlockSpec((TL, 128), lambda i: (i, 0))],
        out_shape=[SDS((S, DNW), f32)] * 4 + [SDS((4, S, CH), f32), SDS((S, 128), f32)], compiler_params=_cp(48),
    )(c, dba, par)


def _dn_local_bwd(c, dba, par, du, dw, dqg, dkd, dintra, daux):
    def body(c_ref, dba_ref, par_ref, du_ref, dw_ref, dqg_ref, dkd_ref, din_ref, daux_ref, dc_ref, ddba_ref, dpar_ref):
        @pl.when(pl.program_id(0) == 0)
        def _():
            dpar_ref[...] = jnp.zeros_like(dpar_ref)

        _, vjp = jax.vjp(_dn_local, c_ref[...], dba_ref[...], par_ref[0:1, :], par_ref[1:2, :])
        dc, ddba, da_row, db_row = vjp((du_ref[...], dw_ref[...], dqg_ref[...], dkd_ref[...], din_ref[...], daux_ref[...]))
        dc_ref[...] = dc
        ddba_ref[...] = ddba
        dpar_ref[0:1, :] += da_row
        dpar_ref[1:2, :] += db_row

    wide = pl.BlockSpec((TL, DNW), lambda i: (i, 0))
    return _pc(
        body, grid=(S // TL,), name="dn_local_bwd",
        in_specs=[pl.BlockSpec((TL, CW), lambda i: (i, 0)), pl.BlockSpec((TL, 128), lambda i: (i, 0)),
                  pl.BlockSpec((8, 128), lambda i: (0, 0)), wide, wide, wide, wide,
                  pl.BlockSpec((4, TL, CH), lambda i: (0, i, 0)), pl.BlockSpec((TL, 128), lambda i: (i, 0))],
        out_specs=[pl.BlockSpec((TL, CW), lambda i: (i, 0)), pl.BlockSpec((TL, 128), lambda i: (i, 0)),
                   pl.BlockSpec((8, 128), lambda i: (0, 0))],
        out_shape=[SDS((S, CW), f32), SDS((S, 128), f32), SDS((8, 128), f32)], compiler_params=_cp(56),
    )(c, dba, par, du, dw, dqg, dkd, dintra, daux)


def _dn_step(state, u, w, qg, kd, intra, aux):
    lane = lax.broadcasted_iota(jnp.int32, (CH, 128), 1)
    row = lax.broadcasted_iota(jnp.int32, (CH, 128), 0)
    outs, states = [], []
    for h in range(4):
        sl = slice(DH * h, DH * (h + 1))
        st = state[h]
        e = jnp.sum(jnp.sum(jnp.where((lane == h) & (row == 0), aux, 0.0), axis=1, keepdims=True), axis=0, keepdims=True)
        v_new = u[:, sl] - _mm(w[:, sl], st)
        outs.append(_mm(qg[:, sl], st) + _mm(intra[h], v_new))
        states.append(st * e + _mm_tn(kd[:, sl], v_new))
    return jnp.concatenate(outs, axis=1), jnp.stack(states, axis=0)


CPS = 8
NSTEP = NCHUNK // CPS


def _dn_rec_specs(index):
    wide = pl.BlockSpec((CPS * CH, DNW), lambda n: (index(n), 0))
    inb = pl.BlockSpec((4, CPS * CH, CH), lambda n: (0, index(n), 0))
    auxb = pl.BlockSpec((CPS * CH, 128), lambda n: (index(n), 0))
    stb = pl.BlockSpec((CPS, 4, DH, DH), lambda n: (index(n), 0, 0, 0))
    return wide, inb, auxb, stb


def _dn_rec_fwd(u, w, qg, kd, intra, aux):
    def body(u_ref, w_ref, qg_ref, kd_ref, in_ref, aux_ref, o_ref, st_ref, st_scr):
        @pl.when(pl.program_id(0) == 0)
        def _():
            st_scr[...] = jnp.zeros_like(st_scr)

        st = st_scr[...]
        for k in range(CPS):
            rows = slice(CH * k, CH * (k + 1))
            st_ref[k] = st
            o, st = _dn_step(st, u_ref[rows, :], w_ref[rows, :], qg_ref[rows, :], kd_ref[rows, :], in_ref[:, rows, :],
                             aux_ref[rows, :])
            o_ref[rows, :] = o
        st_scr[...] = st

    wide, inb, auxb, stb = _dn_rec_specs(lambda n: n)
    return _pc(
        body, grid=(NSTEP,), name="dn_rec_fwd", in_specs=[wide, wide, wide, wide, inb, auxb], out_specs=[wide, stb],
        out_shape=[SDS((S, DNW), f32), SDS((NCHUNK, 4, DH, DH), f32)],
        scratch_shapes=[pltpu.VMEM((4, DH, DH), f32)], compiler_params=_cp(32),
    )(u, w, qg, kd, intra, aux)


def _dn_rec_bwd(u, w, qg, kd, intra, aux, states, do):
    def body(u_ref, w_ref, qg_ref, kd_ref, in_ref, aux_ref, st_ref, do_ref,
             du_ref, dw_ref, dqg_ref, dkd_ref, din_ref, daux_ref, ds_scr):
        @pl.when(pl.program_id(0) == 0)
        def _():
            ds_scr[...] = jnp.zeros_like(ds_scr)

        ds = ds_scr[...]
        for k in reversed(range(CPS)):
            rows = slice(CH * k, CH * (k + 1))
            _, vjp = jax.vjp(_dn_step, st_ref[k], u_ref[rows, :], w_ref[rows, :], qg_ref[rows, :], kd_ref[rows, :],
                             in_ref[:, rows, :], aux_ref[rows, :])
            ds, du, dw, dqg, dkd, din, daux = vjp((do_ref[rows, :], ds))
            du_ref[rows, :] = du
            dw_ref[rows, :] = dw
            dqg_ref[rows, :] = dqg
            dkd_ref[rows, :] = dkd
            din_ref[:, rows, :] = din
            daux_ref[rows, :] = daux
        ds_scr[...] = ds

    wide, inb, auxb, stb = _dn_rec_specs(lambda n: NSTEP - 1 - n)
    return _pc(
        body, grid=(NSTEP,), name="dn_rec_bwd", in_specs=[wide, wide, wide, wide, inb, auxb, stb, wide],
        out_specs=[wide, wide, wide, wide, inb, auxb],
        out_shape=[SDS((S, DNW), f32)] * 4 + [SDS((4, S, CH), f32), SDS((S, 128), f32)],
        scratch_shapes=[pltpu.VMEM((4, DH, DH), f32)], compiler_params=_cp(40),
    )(u, w, qg, kd, intra, aux, states, do)


def _dn_post(o, z, nw):
    parts = []
    for h in range(4):
        sl = slice(DH * h, DH * (h + 1))
        oh = o[:, sl]
        y = oh * lax.rsqrt(jnp.mean(oh * oh, axis=-1, keepdims=True) + EPS) * nw
        zh = z[:, sl]
        parts.append(y * (zh * jax.nn.sigmoid(zh)))
    return jnp.concatenate(parts, axis=1)


def _dn_post_fwd(o, z, nw):
    def body(o_ref, z_ref, nw_ref, y_ref):
        y_ref[...] = _dn_post(o_ref[...], z_ref[...], nw_ref[...])

    wide = pl.BlockSpec((TM, DNW), lambda i: (i, 0))
    return _pc(body, grid=(S // TM,), name="dn_post_fwd",
                          in_specs=[wide, wide, pl.BlockSpec((1, 128), lambda i: (0, 0))], out_specs=wide,
                          out_shape=SDS((S, DNW), f32), compiler_params=_cp(32))(o, z, nw)


def _dn_post_bwd(o, z, nw, dy):
    def body(o_ref, z_ref, nw_ref, dy_ref, do_ref, dz_ref, dnw_ref):
        @pl.when(pl.program_id(0) == 0)
        def _():
            dnw_ref[...] = jnp.zeros_like(dnw_ref)

        _, vjp = jax.vjp(_dn_post, o_ref[...], z_ref[...], nw_ref[...])
        do, dz, dnw = vjp(dy_ref[...])
        do_ref[...] = do
        dz_ref[...] = dz
        dnw_ref[...] += dnw

    wide = pl.BlockSpec((TM, DNW), lambda i: (i, 0))
    one = pl.BlockSpec((1, 128), lambda i: (0, 0))
    return _pc(body, grid=(S // TM,), name="dn_post_bwd", in_specs=[wide, wide, one, wide],
                          out_specs=[wide, wide, one], out_shape=[SDS((S, DNW), f32), SDS((S, DNW), f32), SDS((1, 128), f32)],
                          compiler_params=_cp(32))(o, z, nw, dy)


def _row_tile(rows, width, itemsize=4, target=2 * 1024 * 1024):
    best = None
    for t in range(16, rows + 1, 16):
        if rows % t == 0 and t * width * itemsize <= target:
            best = t
    return best if best is not None else rows


def _sum_pieces(pieces, out_dtype, name):
    n, rows, width = pieces.shape
    tr = _row_tile(rows, width * n)

    def body(p_ref, o_ref):
        acc = p_ref[0].astype(f32)
        for s in range(1, n):
            acc = acc + p_ref[s].astype(f32)
        o_ref[...] = acc.astype(out_dtype)

    return _pc(body, grid=(rows // tr,), name=name,
                          in_specs=[pl.BlockSpec((n, tr, width), lambda i: (0, i, 0))],
                          out_specs=pl.BlockSpec((tr, width), lambda i: (i, 0)),
                          out_shape=SDS((rows, width), out_dtype), compiler_params=_cp(32))(pieces)


def _sum_core_pair(part, got, c_arr):
    n, rows, width = part.shape
    half = rows // 2
    tr = _row_tile(half, width, itemsize=2)
    nt = half // tr

    def body(c_ref, p_ref, g_ref, o_ref):
        o_ref[...] = (p_ref[...].astype(f32) + g_ref[...].astype(f32)).astype(bf16)

    gs = pltpu.PrefetchScalarGridSpec(
        num_scalar_prefetch=1, grid=(n, nt),
        in_specs=[pl.BlockSpec((1, tr, width), lambda j, i, c: (j, c[0] * nt + i, 0)),
                  pl.BlockSpec((1, tr, width), lambda j, i, c: (j, i, 0))],
        out_specs=pl.BlockSpec((1, tr, width), lambda j, i, c: (j, i, 0)))
    return _pc(body, grid_spec=gs, name="sum_core_pair", out_shape=SDS((n, half, width), bf16),
                          compiler_params=_cp(32))(c_arr, part, got)


def _sum_chips(pieces, c_arr, full, row0, total_rows):
    n, half, width = pieces.shape
    tr = max(t for t in range(16, 257, 16) if half % t == 0 and row0 % t == 0)
    nt = half // tr

    def body(c_ref, p_ref, *rest):
        o_ref = rest[-1]
        acc = p_ref[0].astype(f32)
        for s in range(1, n):
            acc = acc + p_ref[s].astype(f32)
        o_ref[...] = acc

    gs = pltpu.PrefetchScalarGridSpec(
        num_scalar_prefetch=1, grid=(nt,),
        in_specs=[pl.BlockSpec((n, tr, width), lambda i, c: (0, i, 0))] + ([] if full is None else [ANY]),
        out_specs=pl.BlockSpec((tr, width), lambda i, c: (row0 // tr + c[0] * nt + i, 0)))
    args = (c_arr, pieces) if full is None else (c_arr, pieces, full)
    return _pc(body, grid_spec=gs, name="sum_chips", out_shape=SDS((total_rows, width), f32),
                          input_output_aliases={} if full is None else {2: 0}, compiler_params=_cp(32))(*args)


def _adamw_math(w, g, m, v):
    mn = ADAM_B1 * m + (1.0 - ADAM_B1) * g
    vn = ADAM_B2 * v + (1.0 - ADAM_B2) * (g * g)
    m_hat = mn / (1.0 - ADAM_B1 ** ADAM_STEP)
    v_hat = vn / (1.0 - ADAM_B2 ** ADAM_STEP)
    return -ADAM_LR * (m_hat / (jnp.sqrt(v_hat) + ADAM_EPS) + ADAM_WD * w), mn, vn


def _adamw(w, g, m, v, name):
    rows, width = w.shape
    tr = _row_tile(rows, width * 7, target=12 * 1024 * 1024)

    def body(w_ref, g_ref, m_ref, v_ref, d_ref, nm_ref, nv_ref):
        d_ref[...], nm_ref[...], nv_ref[...] = _adamw_math(w_ref[...], g_ref[...], m_ref[...], v_ref[...])

    blk = pl.BlockSpec((tr, width), lambda i: (i, 0))
    return _pc(body, grid=(rows // tr,), name=name, in_specs=[blk] * 4, out_specs=[blk] * 3,
                          out_shape=[SDS((rows, width), f32)] * 3, compiler_params=_cp(40))(w, g, m, v)


def _adamw_rows(w, m, v, gblob, tr, first_tile, name):
    layers, rows, width = w.shape

    def body(w_ref, g_ref, m_ref, v_ref, d_ref, nm_ref, nv_ref):
        d_ref[0], nm_ref[0], nv_ref[0] = _adamw_math(w_ref[0], g_ref[...], m_ref[0], v_ref[0])

    blk = pl.BlockSpec((1, tr, width), lambda l, i: (l, i, 0))
    gblk = pl.BlockSpec((tr, width), lambda l, i: (first_tile(l) + i, 0))
    return _pc(body, grid=(layers, rows // tr), name=name, in_specs=[blk, gblk, blk, blk], out_specs=[blk] * 3,
                          out_shape=[SDS(w.shape, f32)] * 3, compiler_params=_cp(40))(w, gblob, m, v)


ANY = pl.BlockSpec(memory_space=pl.ANY)


def _place():
    x, y, c = lax.axis_index("x"), lax.axis_index("y"), lax.axis_index("c")
    chips = [(1 - x, y), (x, 1 - y), (1 - x, 1 - y)]
    return x, y, c, chips


NQ_ICI = 4
NQ_D2D = 8


def _chunks(rows, want):
    n = max(k for k in range(1, want + 1) if rows % k == 0 and (rows // k) % 16 == 0)
    step = rows // n
    return [(q * step, step) for q in range(n)]


def _scatter_copies(ins, outs, ssem, rsem, lsem):
    x, y, c, chips = _place()
    me = (x, y, c)
    locals_, sends, lands = [], [], []
    for b in range(len(ins)):
        for q, (off, n) in enumerate(_chunks(ins[b].shape[1], NQ_ICI)):
            rows = pl.ds(off, n)
            mine = outs[b].at[2 * x + y, rows, :]
            locals_.append(pltpu.make_async_copy(ins[b].at[2 * x + y, rows, :], mine, lsem.at[b, q]))
            for j, chip in enumerate(chips):
                sends.append(_remote(ins[b].at[2 * chip[0] + chip[1], rows, :], mine, ssem.at[b, j, q], rsem.at[b, j, q],
                                     (*chip, c)))
                slot = outs[b].at[2 * chip[0] + chip[1], rows, :]
                lands.append(_remote(slot, slot, ssem.at[b, j, q], rsem.at[b, j, q], me))
    return locals_, sends, lands


def _scatter_start(ins, outs, ssem, rsem, lsem):
    locals_, sends, _ = _scatter_copies(ins, outs, ssem, rsem, lsem)
    for cp in locals_ + sends:
        cp.start()


def _scatter_finish(ins, outs, ssem, rsem, lsem):
    locals_, sends, lands = _scatter_copies(ins, outs, ssem, rsem, lsem)
    for cp in lands:
        cp.wait_recv()
    for cp in sends:
        cp.wait_send()
    for cp in locals_:
        cp.wait()


def _scatter_sems(nb):
    return [pltpu.SemaphoreType.DMA((nb, 3, NQ_ICI)), pltpu.SemaphoreType.DMA((nb, 3, NQ_ICI)),
            pltpu.SemaphoreType.DMA((nb, NQ_ICI))]


def _remote(src, dst, ssem, rsem, dev):
    return pltpu.make_async_remote_copy(src_ref=src, dst_ref=dst, send_sem=ssem, recv_sem=rsem, device_id=dev,
                                        device_id_type=MESH)


def _all_gather_weights(shards):
    nb = len(shards)

    def body(*refs):
        ins, outs, sems = refs[:nb], refs[nb:2 * nb], refs[2 * nb:]
        _gather_start(ins, outs, *sems)
        _gather_finish(ins, outs, *sems)

    return _pc(
        body, name="all_gather_weights", in_specs=[ANY] * nb, out_specs=[ANY] * nb,
        out_shape=[SDS((NCH,) + s.shape, s.dtype) for s in shards], scratch_shapes=_gather_sems(nb),
    )(*shards)


def _rows_to_move(ref):
    return 3 * FC + 256 if ref.shape[0] == 4 * FC else ref.shape[0]


def _gather_first(ins, outs, ssem, rsem, lsem):
    x, y, c, chips = _place()
    locals_, sends = [], []
    for b in range(len(ins)):
        half = _rows_to_move(ins[b]) // 2
        for q, (off, n) in enumerate(_chunks(half, NQ_ICI)):
            mine = pl.ds(c * half + off, n)
            own = outs[b].at[2 * x + y, mine, :]
            locals_.append(pltpu.make_async_copy(ins[b].at[mine, :], own, lsem.at[b, q]))
            sends.append(_remote(ins[b].at[mine, :], own, ssem.at[b, 0, q], rsem.at[b, 0, q], (x, y, 1 - c)))
            sends += [_remote(ins[b].at[mine, :], own, ssem.at[b, 1 + j, q], rsem.at[b, 1 + j, q], (*chip, c))
                      for j, chip in enumerate(chips)]
    return locals_, sends


def _gather_start(ins, outs, ssem, rsem, lsem):
    locals_, sends = _gather_first(ins, outs, ssem, rsem, lsem)
    for cp in locals_ + sends:
        cp.start()


def _gather_finish(ins, outs, ssem, rsem, lsem):
    x, y, c, chips = _place()
    me, sib = (x, y, c), (x, y, 1 - c)
    locals_, sends = _gather_first(ins, outs, ssem, rsem, lsem)
    for b in range(len(ins)):
        half = _rows_to_move(ins[b]) // 2
        for q, (off, n) in enumerate(_chunks(half, NQ_ICI)):
            mine = pl.ds(c * half + off, n)
            for j, chip in enumerate(chips):
                landed = outs[b].at[2 * chip[0] + chip[1], mine, :]
                _remote(landed, landed, ssem.at[b, 1 + j, q], rsem.at[b, 1 + j, q], me).wait_recv()
                cp = _remote(landed, landed, ssem.at[b, 4 + j, q], rsem.at[b, 4 + j, q], sib)
                cp.start()
                sends.append(cp)
    for b in range(len(ins)):
        half = _rows_to_move(ins[b]) // 2
        for q, (off, n) in enumerate(_chunks(half, NQ_ICI)):
            other = pl.ds((1 - c) * half + off, n)
            theirs = outs[b].at[2 * x + y, other, :]
            _remote(theirs, theirs, ssem.at[b, 0, q], rsem.at[b, 0, q], me).wait_recv()
            for j, chip in enumerate(chips):
                fwd = outs[b].at[2 * chip[0] + chip[1], other, :]
                _remote(fwd, fwd, ssem.at[b, 4 + j, q], rsem.at[b, 4 + j, q], me).wait_recv()
    for cp in sends:
        cp.wait_send()
    for cp in locals_:
        cp.wait()


def _gather_sems(nb):
    return [pltpu.SemaphoreType.DMA((nb, 7, NQ_ICI)), pltpu.SemaphoreType.DMA((nb, 7, NQ_ICI)),
            pltpu.SemaphoreType.DMA((nb, NQ_ICI))]


def _send_sibling_half(parts):
    nb = len(parts)

    def body(*refs):
        ins, gots = refs[:nb], refs[nb:2 * nb]
        ssem, rsem = refs[2 * nb:]
        x, y, c, _ = _place()
        sib = (x, y, 1 - c)
        todo = []
        for b in range(nb):
            half = ins[b].shape[1] // 2
            for q, (off, n) in enumerate(_chunks(half, NQ_D2D)):
                cp = _remote(ins[b].at[:, pl.ds((1 - c) * half + off, n), :], gots[b].at[:, pl.ds(off, n), :],
                             ssem.at[b, q], rsem.at[b, q], sib)
                cp.start()
                todo.append(cp)
        for cp in todo:
            cp.wait()

    return _pc(
        body, name="send_sibling_half", in_specs=[ANY] * nb, out_specs=[ANY] * nb,
        out_shape=[SDS((p.shape[0], p.shape[1] // 2, p.shape[2]), p.dtype) for p in parts],
        scratch_shapes=[pltpu.SemaphoreType.DMA((nb, NQ_D2D)), pltpu.SemaphoreType.DMA((nb, NQ_D2D))],
    )(*parts)


def _scatter_to_chips(parts):
    nb = len(parts)

    def body(*refs):
        ins, outs, sems = refs[:nb], refs[nb:2 * nb], refs[2 * nb:]
        _scatter_start(ins, outs, *sems)
        _scatter_finish(ins, outs, *sems)

    return _pc(
        body, name="scatter_to_chips", in_specs=[ANY] * nb, out_specs=[ANY] * nb,
        out_shape=[SDS(p.shape, p.dtype) for p in parts], scratch_shapes=_scatter_sems(nb),
    )(*parts)


def _join_halves(fulls, ranges):
    nb = len(fulls)
    nr = max(len(r) for r in ranges)

    def body(*refs):
        ins, outs = refs[:nb], refs[nb:2 * nb]
        ssem, rsem = refs[2 * nb:]
        x, y, c, _ = _place()
        sib = (x, y, 1 - c)
        sends, lands = [], []
        for b in range(nb):
            for g, (row0, rows) in enumerate(ranges[b]):
                half = rows // 2
                for q, (off, n) in enumerate(_chunks(half, NQ_D2D)):
                    mine = pl.ds(row0 + c * half + off, n)
                    sends.append(_remote(ins[b].at[mine, :], outs[b].at[mine, :], ssem.at[b, g, q], rsem.at[b, g, q], sib))
                    other = outs[b].at[pl.ds(row0 + (1 - c) * half + off, n), :]
                    lands.append(_remote(other, other, ssem.at[b, g, q], rsem.at[b, g, q], sib))
        for cp in sends:
            cp.start()
        for cp in lands:
            cp.wait_recv()
        for cp in sends:
            cp.wait_send()

    return _pc(
        body, name="join_halves", in_specs=[ANY] * nb, out_specs=[ANY] * nb,
        out_shape=[SDS(h.shape, h.dtype) for h in fulls], input_output_aliases={b: b for b in range(nb)},
        scratch_shapes=[pltpu.SemaphoreType.DMA((nb, nr, NQ_D2D)), pltpu.SemaphoreType.DMA((nb, nr, NQ_D2D))],
    )(*fulls)


def _gather_small(vec):
    def body(v_ref, o_ref, ssem, rsem, lsem):
        x, y, c, _ = _place()
        mine = o_ref.at[4 * x + 2 * y + c]
        local = pltpu.make_async_copy(v_ref, mine, lsem)
        local.start()
        sends = []
        for k in range(1, 8):
            peer = (x ^ (k >> 2), y ^ ((k >> 1) & 1), c ^ (k & 1))
            cp = _remote(v_ref, mine, ssem.at[k - 1], rsem.at[k - 1], peer)
            cp.start()
            sends.append(cp)
        for k in range(1, 8):
            px, py, pc = x ^ (k >> 2), y ^ ((k >> 1) & 1), c ^ (k & 1)
            slot = o_ref.at[4 * px + 2 * py + pc]
            _remote(slot, slot, ssem.at[k - 1], rsem.at[k - 1], (x, y, c)).wait_recv()
        for cp in sends:
            cp.wait_send()
        local.wait()

    return _pc(
        body, name="gather_small", in_specs=[ANY], out_specs=ANY, out_shape=SDS((8,) + vec.shape, vec.dtype),
        scratch_shapes=[pltpu.SemaphoreType.DMA((7,)), pltpu.SemaphoreType.DMA((7,)), pltpu.SemaphoreType.DMA],
    )(vec)


def _block_diag(pw):
    return jnp.concatenate([jnp.pad(pw[g], ((0, 0), (64 * g, 192 - 64 * g))) for g in range(4)], axis=0)


def _own_columns(full, chip):
    n = full.shape[-1] // NCH
    parts = full.reshape(full.shape[:-1] + (NCH, n))
    sel = (lax.broadcasted_iota(jnp.int32, (NCH, 1), 0) == chip)
    return jnp.sum(jnp.where(sel, parts, 0.0), axis=-2)


def _at_own_columns(shard, chip):
    n = shard.shape[-1]
    sel = (lax.broadcasted_iota(jnp.int32, (NCH * n,), 0) // n == chip)
    return jnp.where(sel, jnp.tile(shard, NCH), 0.0)


def _pad_rows(a, rows):
    return jnp.pad(a, ((0, rows - a.shape[0]),) + ((0, 0),) * (a.ndim - 1))


def _ffn_block(l, which):
    return 7 * l + 3 * which


def _wout_block(l):
    return 7 * l + 6


class _Weights:
    def __init__(self):
        self.ffn, self.wout, self.w_aug, self.rides = {}, {}, {}, {}

    @classmethod
    def from_blob(cls, blob, w_aug):
        self = cls()
        for l in range(DEPTH):
            self.ffn[(l, 0)], self.ffn[(l, 1)] = (blob, _ffn_block(l, 0)), (blob, _ffn_block(l, 1))
            self.wout[l], self.w_aug[l] = (blob, _wout_block(l)), w_aug[l]
        return self

    def set_w_in(self, l, gathered):
        self.w_aug[l] = jnp.pad(gathered.transpose(1, 0, 2).reshape(D, INW), ((0, 0), (0, INP - INW)))

    def ffn_fwd(self, l, which, x, nw):
        arr, k0 = self.ffn[(l, which)]
        if (l, which) not in self.rides:
            return _ffn_fwd(x, nw, arr, k0)
        shards, landed = self.rides[(l, which)]
        out, *gathered = _ffn_fwd(x, nw, arr, k0, shards)
        landed(gathered)
        return out


def _layer_fwd(l, x0, pos, freq, wts, ws):
    sv = {"x0": x0}
    x1 = ws.ffn_fwd(l, 0, x0, wts["ffn1_norm"][l:l + 1])
    att, att4, att16, pu, dq, dz, dba = _inproj_fwd(x1, wts["mix_norm"][l:l + 1], ws.w_aug[l], pos, freq)
    qkvs = [att, att4.reshape(S, 768), att16.reshape(S, 768)]
    (o1, l1), (o4, l4), (o16, l16) = [_attn_fwd(q, NB // d) for q, d in zip(qkvs, PATTERN_DIL)]
    ols = (o1, l1, o4.reshape(4, S // 4, ATT), l4.reshape(4, S // 4, ATT), o16.reshape(16, S // 16, ATT),
           l16.reshape(16, S // 16, ATT))
    ya = _merge_fwd(*ols)
    yb = _pool_fwd(pu, wts["pool_bd"][l], wts["pool_scale"][l:l + 1])
    c = _conv_fwd(dq, wts["conv_w"][l])
    u, w, qg, kd, intra, aux = _dn_local_fwd(c, dba, wts["dn_par"][l])
    o_dn, states = _dn_rec_fwd(u, w, qg, kd, intra, aux)
    yc = _dn_post_fwd(o_dn, dz, wts["dn_out_norm"][l:l + 1])
    x2 = _outproj_fwd(x1, ya, yb, yc, *ws.wout[l])
    x3 = ws.ffn_fwd(l, 1, x2, wts["ffn2_norm"][l:l + 1])
    sv.update(x1=x1, x2=x2, qkvs=qkvs, ols=ols, ya=ya, yb=yb, yc=yc, pu=pu, dq=dq, dz=dz, dba=dba, c=c,
              u=u, w=w, qg=qg, kd=kd, intra=intra, aux=aux, states=states, o_dn=o_dn)
    return x3, sv


def _wout_part(g):
    return jnp.pad(g.astype(bf16).reshape(NCH, 256, D), ((0, 0), (0, FC - 256), (0, 0)))


def _win_part(g):
    return g[:, :INW].astype(bf16).reshape(D, NCH, INC).transpose(1, 0, 2)


def _layer_bwd(l, dx3, sv, pos, freq, wts, ws, ride=None, prep=None):
    gr = {}
    g2, u2, d2, dh4, *pieces_before = _ffn_bwd(sv["x2"], wts["ffn2_norm"][l:l + 1], *ws.ffn[(l, 1)], dx3, ride)
    gr.update(ffn2_w_gate=g2, ffn2_w_up=u2, ffn2_w_down=d2)
    dx2, gr["ffn2_norm"], dya, dyb, dyc, gr["w_out"] = _outproj_bwd(sv["x2"], wts["ffn2_norm"][l:l + 1], dx3, dh4, sv["ya"],
                                                                     sv["yb"], sv["yc"], *ws.wout[l])
    do_dn, ddz, gr["dn_out_norm"] = _dn_post_bwd(sv["o_dn"], sv["dz"], wts["dn_out_norm"][l:l + 1], dyc)
    du, dw, dqg, dkd, dintra, daux = _dn_rec_bwd(sv["u"], sv["w"], sv["qg"], sv["kd"], sv["intra"], sv["aux"], sv["states"], do_dn)
    dc, ddba, gr["dn_par"] = _dn_local_bwd(sv["c"], sv["dba"], wts["dn_par"][l], du, dw, dqg, dkd, dintra, daux)
    ddq, gr["conv_w"] = _conv_bwd(sv["dq"], wts["conv_w"][l], dc)
    dpu, gr["pool_bd"], gr["pool_scale"] = _pool_bwd(sv["pu"], wts["pool_bd"][l], wts["pool_scale"][l:l + 1], dyb)
    dols = _merge_bwd(*sv["ols"], dya)
    flat = lambda a: a.reshape(S, ATT)
    datts = [_attn_bwd(q, flat(sv["ols"][2 * p]), flat(sv["ols"][2 * p + 1]), flat(dols[2 * p]), flat(dols[2 * p + 1]), NB // d)
             for p, (q, d) in enumerate(zip(sv["qkvs"], PATTERN_DIL))]
    dx1, gr["mix_norm"], gr["w_aug"] = _inproj_bwd(sv["x1"], wts["mix_norm"][l:l + 1], ws.w_aug[l], pos, freq, dx2,
                                                    datts[0], datts[1].reshape(4, S // 4, 768),
                                                    datts[2].reshape(16, S // 16, 768), dpu, ddq, ddz, ddba)
    own = None
    if prep is not None:
        own = prep([jnp.concatenate([g2, u2, d2, _wout_part(gr["w_out"])], axis=1), _win_part(gr["w_aug"])])
    g1, u1, d1, dh4, *pieces_own = _ffn_bwd(sv["x0"], wts["ffn1_norm"][l:l + 1], *ws.ffn[(l, 0)], dx1, own)
    dx0, gr["ffn1_norm"] = _norm_bwd(sv["x0"], wts["ffn1_norm"][l:l + 1], dx1, dh4)
    gr.update(ffn1_w_gate=g1, ffn1_w_up=u1, ffn1_w_down=d1)
    return dx0, gr, pieces_before, pieces_own


def _device_step(x, pos, target, wts, ws, prep=None):
    freq = jnp.tile(ROPE_THETA ** (-jnp.arange(0, EH, 2, dtype=f32) / EH), 2 * ATT // EH).reshape(1, ATT)
    saved = []
    h = x
    for l in range(DEPTH):
        h, sv = _layer_fwd(l, h, pos, freq, wts, ws)
        saved.append(sv)
    dh, g_final, loss = _final(h, wts["final_norm"], target)
    grads = [None] * DEPTH
    dh, grads[1], _, _ = _layer_bwd(1, dh, saved[1], pos, freq, wts, ws)
    sums1 = None
    if prep is not None:
        g = grads[1]
        ffn = [g[f"ffn{f}_w_{n}"] for f in (1, 2) for n in ("gate", "up", "down")]
        sums1 = prep([jnp.concatenate(ffn + [_wout_part(g["w_out"])], axis=1), _win_part(g["w_aug"])])
    dh, grads[0], pieces1, pieces0 = _layer_bwd(0, dh, saved[0], pos, freq, wts, ws, sums1, prep)
    return loss, dh, g_final, grads, pieces1, pieces0


_SMALL = (("ffn1_norm", (DEPTH, D)), ("mix_norm", (DEPTH, D)), ("pool_w", (DEPTH, 4, 64, 64)), ("pool_scale", (DEPTH, 256)),
          ("dn_conv_w", (DEPTH, 4, CW)), ("dn_a_log", (DEPTH, 4)), ("dn_dt_bias", (DEPTH, 4)), ("dn_out_norm", (DEPTH, 128)),
          ("ffn2_norm", (DEPTH, D)), ("final_norm", (D,)), ("loss", (1,)))


def _pack_small(vals):
    rows = []
    for name, shape in _SMALL:
        flat = vals[name].astype(f32).reshape(-1)
        rows.append(jnp.pad(flat, (0, _small_rows(shape) * 128 - flat.shape[0])).reshape(-1, 128))
    out = jnp.concatenate(rows, axis=0)
    return _pad_rows(out, -(-out.shape[0] // 16) * 16)


def _small_rows(shape):
    return -(-int(np.prod(shape)) // 1024) * 8


def _unpack_small(packed):
    vals, r = {}, 0
    for name, shape in _SMALL:
        size, n = int(np.prod(shape)), _small_rows(shape)
        vals[name] = packed[r:r + n].reshape(-1)[:size].reshape(shape)
        r += n
    return vals


def kernel(x, positions, ffn1_norm, ffn1_w_gate, ffn1_w_up, ffn1_w_down, mix_norm, w_in, pool_w, pool_scale, dn_conv_w, dn_a_log, dn_dt_bias, dn_out_norm, w_out, ffn2_norm, ffn2_w_gate, ffn2_w_up, ffn2_w_down, final_norm, loss_target, m_ffn1_norm, m_ffn1_w_gate, m_ffn1_w_up, m_ffn1_w_down, m_mix_norm, m_w_in, m_pool_w, m_pool_scale, m_dn_conv_w, m_dn_a_log, m_dn_dt_bias, m_dn_out_norm, m_w_out, m_ffn2_norm, m_ffn2_w_gate, m_ffn2_w_up, m_ffn2_w_down, m_final_norm, v_ffn1_norm, v_ffn1_w_gate, v_ffn1_w_up, v_ffn1_w_down, v_mix_norm, v_w_in, v_pool_w, v_pool_scale, v_dn_conv_w, v_dn_a_log, v_dn_dt_bias, v_dn_out_norm, v_w_out, v_ffn2_norm, v_ffn2_w_gate, v_ffn2_w_up, v_ffn2_w_down, v_final_norm):
    names = ["ffn1_norm", "ffn1_w_gate", "ffn1_w_up", "ffn1_w_down", "mix_norm", "w_in", "pool_w", "pool_scale", "dn_conv_w",
             "dn_a_log", "dn_dt_bias", "dn_out_norm", "w_out", "ffn2_norm", "ffn2_w_gate", "ffn2_w_up", "ffn2_w_down", "final_norm"]
    W = dict(zip(names, [ffn1_norm, ffn1_w_gate, ffn1_w_up, ffn1_w_down, mix_norm, w_in, pool_w, pool_scale, dn_conv_w,
                         dn_a_log, dn_dt_bias, dn_out_norm, w_out, ffn2_norm, ffn2_w_gate, ffn2_w_up, ffn2_w_down, final_norm]))
    M = dict(zip(names, [m_ffn1_norm, m_ffn1_w_gate, m_ffn1_w_up, m_ffn1_w_down, m_mix_norm, m_w_in, m_pool_w, m_pool_scale,
                         m_dn_conv_w, m_dn_a_log, m_dn_dt_bias, m_dn_out_norm, m_w_out, m_ffn2_norm, m_ffn2_w_gate, m_ffn2_w_up,
                         m_ffn2_w_down, m_final_norm]))
    V = dict(zip(names, [v_ffn1_norm, v_ffn1_w_gate, v_ffn1_w_up, v_ffn1_w_down, v_mix_norm, v_w_in, v_pool_w, v_pool_scale,
                         v_dn_conv_w, v_dn_a_log, v_dn_dt_bias, v_dn_out_norm, v_w_out, v_ffn2_norm, v_ffn2_w_gate, v_ffn2_w_up,
                         v_ffn2_w_down, v_final_norm]))
    chip = 2 * lax.axis_index("x") + lax.axis_index("y")

    ffn_names = [(f"ffn{f}_w_gate", f"ffn{f}_w_up", f"ffn{f}_w_down") for f in (1, 2)]
    tr = lambda t: jnp.swapaxes(t, -1, -2)
    def ffn_rows(l, which):
        g, u, dn = ffn_names[which]
        return [tr(W[g][l]), tr(W[u][l]), W[dn][l]]

    def second_half(l):
        return jnp.concatenate(ffn_rows(l, 1) + [jnp.pad(W["w_out"][l], ((0, FC - 256), (0, 0)))], axis=0).astype(bf16)

    ws = _Weights()
    first0, = _all_gather_weights([jnp.concatenate(ffn_rows(0, 0), axis=0).astype(bf16)])
    ws.ffn[(0, 0)] = (first0, 0)

    def landed_00(gathered):
        ws.ffn[(0, 1)], ws.wout[0] = (gathered[0], 0), (gathered[0], 3)
        ws.set_w_in(0, gathered[1])

    def landed_01(gathered):
        ws.ffn[(1, 0)] = (gathered[0], 0)

    def landed_10(gathered):
        ws.ffn[(1, 1)], ws.wout[1] = (gathered[0], 0), (gathered[0], 3)
        ws.set_w_in(1, gathered[1])

    ws.rides[(0, 0)] = ([second_half(0), W["w_in"][0].astype(bf16)], landed_00)
    ws.rides[(0, 1)] = ([jnp.concatenate(ffn_rows(1, 0), axis=0).astype(bf16)], landed_01)
    ws.rides[(1, 0)] = ([second_half(1), W["w_in"][1].astype(bf16)], landed_10)
    conv_all = _gather_small(_pad_rows(dn_conv_w.reshape(DEPTH * 4 * (CW // NCH) // 128, 128), 32))
    conv_full = jnp.concatenate([conv_all[2 * j, :DEPTH * 4 * (CW // NCH) // 128].reshape(DEPTH, 4, CW // NCH) for j in range(NCH)],
                                axis=-1)

    par = jnp.pad(jnp.stack([dn_a_log, dn_dt_bias], axis=1), ((0, 0), (0, 6), (4, 120)))
    wts = dict(ffn1_norm=ffn1_norm, mix_norm=mix_norm, ffn2_norm=ffn2_norm, final_norm=final_norm.reshape(1, D),
               pool_bd=jnp.stack([_block_diag(pool_w[l]) for l in range(DEPTH)]).astype(bf16),
               pool_scale=pool_scale, conv_w=jnp.pad(conv_full, ((0, 0), (0, 4), (0, 0))),
               dn_par=par, dn_out_norm=dn_out_norm)

    c_arr = lax.axis_index("c").astype(jnp.int32).reshape(1)

    def prep(parts):
        return [_sum_core_pair(p, g, c_arr) for p, g in zip(parts, _send_sibling_half(parts))]

    loss, dx, g_final, grads, pieces1, pieces0 = _device_step(x[0], positions.reshape(S, 1), loss_target[0], wts, ws, prep)
    last = [jnp.concatenate([grads[0][n] for n in ffn_names[0]], axis=1)]
    pieces_last = _scatter_to_chips(prep(last))
    full_b = _sum_chips(pieces_last[0], c_arr, None, 0, RB)
    full_b = _sum_chips(pieces0[0], c_arr, full_b, 3 * FC, RB)
    full_b = _sum_chips(pieces1[0], c_arr, full_b, 7 * FC, RB)
    full_c = _sum_chips(pieces0[1], c_arr, None, 0, RC)
    full_c = _sum_chips(pieces1[1], c_arr, full_c, D, RC)
    full_b, full_c = _join_halves([full_b, full_c], [[(0, 3 * FC), (3 * FC, 4 * FC), (7 * FC, 7 * FC)], [(0, D), (D, D)]])

    small = {"loss": loss[0, 0:1], "final_norm": g_final.reshape(D)}
    for n in ("ffn1_norm", "mix_norm", "ffn2_norm", "pool_scale", "dn_out_norm"):
        small[n] = jnp.stack([grads[l][n].reshape(-1) for l in range(DEPTH)])
    small["pool_w"] = jnp.stack([jnp.stack([grads[l]["pool_bd"][64 * g:64 * (g + 1), 64 * g:64 * (g + 1)] for g in range(4)])
                                 for l in range(DEPTH)])
    small["dn_conv_w"] = jnp.stack([grads[l]["conv_w"][0:4] for l in range(DEPTH)])
    small["dn_a_log"] = jnp.stack([grads[l]["dn_par"][0, 4:8] for l in range(DEPTH)])
    small["dn_dt_bias"] = jnp.stack([grads[l]["dn_par"][1, 4:8] for l in range(DEPTH)])
    packed = _pack_small(small)
    g_small = _sum_pieces(_gather_small(packed), f32, "sum_small")
    gs = _unpack_small(g_small)

    transposed = ("ffn1_w_gate", "ffn1_w_up", "ffn2_w_gate", "ffn2_w_up")
    where = {"ffn1_w_gate": (full_b, FC // 2, lambda l: 14 * l), "ffn1_w_up": (full_b, FC // 2, lambda l: 14 * l + 2),
             "ffn1_w_down": (full_b, FC // 2, lambda l: 14 * l + 4), "ffn2_w_gate": (full_b, FC // 2, lambda l: 14 * l + 6),
             "ffn2_w_up": (full_b, FC // 2, lambda l: 14 * l + 8), "ffn2_w_down": (full_b, FC // 2, lambda l: 14 * l + 10),
             "w_out": (full_b, 64, lambda l: (FC // 64) * (7 * l + 6)), "w_in": (full_c, D // 2, lambda l: 2 * l)}
    big_res = {}
    for n, (gblob, tile, first) in where.items():
        t = tr if n in transposed else (lambda a: a)
        big_res[n] = [t(r) for r in _adamw_rows(t(W[n]), t(M[n]), t(V[n]), gblob, tile, first, "adamw_" + n)]

    def small_of(T):
        d = {n: T[n] for n, _ in _SMALL if n not in ("loss", "dn_conv_w")}
        d["loss"] = jnp.zeros((1,), f32)
        d["dn_conv_w"] = _at_own_columns(T["dn_conv_w"], chip)
        return _pack_small(d)

    res_s = _adamw(small_of(W), g_small, small_of(M), small_of(V), "adamw_small")
    small_out = [_unpack_small(r) for r in res_s]

    def split_blobs(b, c):
        out = {}
        b7 = b.reshape(DEPTH, 7, FC, D)
        for k, n in enumerate(n for names3 in ffn_names for n in names3):
            out[n] = tr(b7[:, k]) if n in transposed else b7[:, k]
        out["w_out"] = b7[:, 6, :256]
        out["w_in"] = c.reshape(DEPTH, D, INC)
        return out

    def assemble(big, sm):
        out = []
        for n in names:
            if n in big:
                out.append(big[n])
            elif n == "dn_conv_w":
                out.append(_own_columns(sm[n], chip))
            else:
                out.append(sm[n])
        return out

    grad_list = assemble(split_blobs(full_b, full_c), gs)
    outs = [gs["loss"].reshape(()), dx.reshape(1, S, D)] + grad_list
    for k in range(3):
        outs += assemble({n: r[k] for n, r in big_res.items()}, small_out[k])
    return tuple(outs)
```

```python
import functools
import math

import jax
import jax.numpy as jnp
import numpy as np
from jax import lax
from jax.experimental import pallas as pl
from jax.experimental.pallas import tpu as pltpu

f32 = jnp.float32
bf16 = jnp.bfloat16
SDS = jax.ShapeDtypeStruct
MESH = pl.DeviceIdType.MESH

S = 4096
D = 1024
DEPTH = 2
FF = 2816
NCH = 4
FC = FF // NCH
INW = 3080
INC = INW // NCH
INP = 3200
ATT = 256
EH = 64
NBLK = 128
DNW = 512
DH = 128
CH = 64
NCHUNK = S // CH
EPS = 1e-6
ROPE_THETA = 10000.0
PATTERN_DIL = (1, 4, 16)
ADAM_LR, ADAM_B1, ADAM_B2, ADAM_EPS, ADAM_WD, ADAM_STEP = 0.001, 0.9, 0.999, 1e-08, 0.01, 10
VMEM_BYTES_V7X = 64 * 1024 * 1024
NEG = -1e30

TM = 512
RB, RC = 14 * FC, 2 * D


def _cp(vmem_mb=48, sem=None):
    kw = dict(vmem_limit_bytes=vmem_mb * 1024 * 1024)
    if sem is not None:
        kw["dimension_semantics"] = sem
    return pltpu.CompilerParams(**kw)


def _pc(*args, **kwargs):
    pin = lambda s: pltpu.HBM(s.shape, s.dtype) if isinstance(s, SDS) and jnp.issubdtype(s.dtype, jnp.floating) else s
    out = kwargs["out_shape"]
    kwargs["out_shape"] = [pin(s) for s in out] if isinstance(out, (list, tuple)) else pin(out)
    call = pl.pallas_call(*args, **kwargs)

    def run(*operands):
        pinned = [pltpu.with_memory_space_constraint(o, pltpu.HBM) if jnp.issubdtype(o.dtype, jnp.floating) else o
                  for o in operands]
        return call(*pinned)

    return run


def _dot(a, b):
    return jnp.dot(a, b, preferred_element_type=f32)


def _dot_nt(a, b):
    return lax.dot_general(a, b, (((1,), (1,)), ((), ())), preferred_element_type=f32)


def _dot_tn(a, b):
    return lax.dot_general(a, b, (((0,), (0,)), ((), ())), preferred_element_type=f32)


def _rms(x, w):
    r = lax.rsqrt(jnp.mean(x * x, axis=-1, keepdims=True) + EPS)
    return x * r * w, r


def _rms_bwd(x, w, r, dh):
    xhat = x * r
    dw = jnp.sum(dh * xhat, axis=0, keepdims=True)
    dxh = dh * w
    dx = r * (dxh - xhat * jnp.mean(dxh * xhat, axis=-1, keepdims=True))
    return dx, dw


def _ffn_fwd(x, nw, blob, k0, ride=None):
    kg, ku, kd = k0, k0 + 1, k0 + 2
    nr = 0 if ride is None else len(ride)
    ni = S // TM

    def body(*refs):
        x_ref, nw_ref, wg_ref, wu_ref, wd_ref = refs[:5]
        ride_in = refs[5:5 + nr]
        o_ref = refs[5 + nr]
        ride_out = refs[6 + nr:6 + 2 * nr]
        h_scr, acc_scr = refs[6 + 2 * nr:8 + 2 * nr]
        sems = refs[8 + 2 * nr:]
        i = pl.program_id(0)
        j = pl.program_id(1)

        if nr:
            @pl.when(jnp.logical_and(i == 0, j == 0))
            def _():
                _gather_start(ride_in, ride_out, *sems)

        @pl.when(j == 0)
        def _():
            h, _ = _rms(x_ref[...], nw_ref[...])
            h_scr[...] = h.astype(bf16)
            acc_scr[...] = jnp.zeros_like(acc_scr)

        h = h_scr[...]
        g = _dot_nt(h, wg_ref[0])
        u = _dot_nt(h, wu_ref[0])
        a = (g * jax.nn.sigmoid(g) * u).astype(bf16)
        acc_scr[...] += _dot(a, wd_ref[0])

        @pl.when(j == NCH - 1)
        def _():
            o_ref[...] = x_ref[...] + 0.5 * acc_scr[...]

        if nr:
            @pl.when(jnp.logical_and(i == ni - 1, j == NCH - 1))
            def _():
                _gather_finish(ride_in, ride_out, *sems)

    wspec = lambda k: pl.BlockSpec((1, FC, D), lambda i, j: (j, k, 0))
    rides = [] if ride is None else list(ride)
    res = _pc(
        body, grid=(ni, NCH), name="ffn_fwd_ride" if nr else "ffn_fwd",
        in_specs=[pl.BlockSpec((TM, D), lambda i, j: (i, 0)),
                  pl.BlockSpec((1, D), lambda i, j: (0, 0)),
                  wspec(kg), wspec(ku), wspec(kd)] + [ANY] * nr,
        out_specs=[pl.BlockSpec((TM, D), lambda i, j: (i, 0))] + [ANY] * nr,
        out_shape=[SDS((S, D), f32)] + [SDS((NCH,) + r.shape, r.dtype) for r in rides],
        scratch_shapes=[pltpu.VMEM((TM, D), bf16), pltpu.VMEM((TM, D), f32)] + (_gather_sems(nr) if nr else []),
        compiler_params=_cp(40),
    )(x, nw, blob, blob, blob, *rides)
    return res if nr else res[0]


def _ffn_bwd(x, nw, blob, k0, dy, ride=None):
    nt = S // TM
    kg, ku, kd = k0, k0 + 1, k0 + 2
    nr = 0 if ride is None else len(ride)

    def body(*refs):
        x_ref, nw_ref, wg_ref, wu_ref, wd_ref, dy_ref = refs[:6]
        ride_in = refs[6:6 + nr]
        dwg_ref, dwu_ref, dwd_ref, dh_ref = refs[6 + nr:10 + nr]
        ride_out = refs[10 + nr:10 + 2 * nr]
        ag, au, ad = refs[10 + 2 * nr:13 + 2 * nr]
        sems = refs[13 + 2 * nr:]
        j = pl.program_id(0)
        i = pl.program_id(1)

        if nr:
            @pl.when(jnp.logical_and(j == 0, i == 0))
            def _():
                _scatter_start(ride_in, ride_out, *sems)

        @pl.when(i == 0)
        def _():
            ag[...] = jnp.zeros_like(ag)
            au[...] = jnp.zeros_like(au)
            ad[...] = jnp.zeros_like(ad)

        hf, _ = _rms(x_ref[...], nw_ref[...])
        h = hf.astype(bf16)
        g = _dot_nt(h, wg_ref[0])
        u = _dot_nt(h, wu_ref[0])
        sg = jax.nn.sigmoid(g)
        s = g * sg
        a = (s * u).astype(bf16)
        dyb = (0.5 * dy_ref[...]).astype(bf16)
        da = _dot_nt(dyb, wd_ref[0])
        ad[...] += _dot_tn(a, dyb)
        du = (da * s).astype(bf16)
        dg = (da * u * (sg * (1.0 + g * (1.0 - sg)))).astype(bf16)
        ag[...] += _dot_tn(dg, h)
        au[...] += _dot_tn(du, h)
        dh_ref[0] = (_dot(dg, wg_ref[0]) + _dot(du, wu_ref[0])).astype(bf16)

        @pl.when(i == nt - 1)
        def _():
            dwg_ref[0] = ag[...].astype(bf16)
            dwu_ref[0] = au[...].astype(bf16)
            dwd_ref[0] = ad[...].astype(bf16)

        if nr:
            @pl.when(jnp.logical_and(j == NCH - 1, i == nt - 1))
            def _():
                _scatter_finish(ride_in, ride_out, *sems)

    wspec = lambda k: pl.BlockSpec((1, FC, D), lambda j, i: (j, k, 0))
    gspec = pl.BlockSpec((1, FC, D), lambda j, i: (j, 0, 0))
    rides = [] if ride is None else list(ride)
    return _pc(
        body, grid=(NCH, nt), name="ffn_bwd_ride" if nr else "ffn_bwd",
        in_specs=[pl.BlockSpec((TM, D), lambda j, i: (i, 0)),
                  pl.BlockSpec((1, D), lambda j, i: (0, 0)),
                  wspec(kg), wspec(ku), wspec(kd),
                  pl.BlockSpec((TM, D), lambda j, i: (i, 0))] + [ANY] * nr,
        out_specs=[gspec, gspec, gspec, pl.BlockSpec((1, TM, D), lambda j, i: (j, i, 0))] + [ANY] * nr,
        out_shape=[SDS((NCH, FC, D), bf16)] * 3 + [SDS((NCH, S, D), bf16)] + [SDS(r.shape, r.dtype) for r in rides],
        scratch_shapes=[pltpu.VMEM((FC, D), f32)] * 3 + (_scatter_sems(nr) if nr else []),
        compiler_params=_cp(56),
    )(x, nw, blob, blob, blob, dy, *rides)


def _norm_bwd(x, nw, dres, dh4):
    nt = S // TM
    nparts = dh4.shape[0]

    def body(x_ref, nw_ref, dres_ref, dh_ref, dx_ref, dnw_ref):
        i = pl.program_id(0)
        dh = dh_ref[0].astype(f32)
        for p in range(1, nparts):
            dh = dh + dh_ref[p].astype(f32)
        xv = x_ref[...]
        _, r = _rms(xv, nw_ref[...])
        dx, dw = _rms_bwd(xv, nw_ref[...], r, dh)
        dx_ref[...] = dres_ref[...] + dx

        @pl.when(i == 0)
        def _():
            dnw_ref[...] = jnp.zeros_like(dnw_ref)

        dnw_ref[...] += dw

    return _pc(
        body, grid=(nt,), name="norm_bwd",
        in_specs=[pl.BlockSpec((TM, D), lambda i: (i, 0)),
                  pl.BlockSpec((1, D), lambda i: (0, 0)),
                  pl.BlockSpec((TM, D), lambda i: (i, 0)),
                  pl.BlockSpec((nparts, TM, D), lambda i: (0, i, 0))],
        out_specs=[pl.BlockSpec((TM, D), lambda i: (i, 0)), pl.BlockSpec((1, D), lambda i: (0, 0))],
        out_shape=[SDS((S, D), f32), SDS((1, D), f32)],
        compiler_params=_cp(40),
    )(x, nw, dres, dh4)


def _final(x, nw, target):
    nt = S // TM

    def body(x_ref, nw_ref, t_ref, dx_ref, dnw_ref, loss_ref):
        i = pl.program_id(0)
        xv = x_ref[...]
        y, r = _rms(xv, nw_ref[...])
        err = y - t_ref[...]
        part = 0.5 * jnp.sum(jnp.mean(err * err, axis=-1, keepdims=True), axis=0, keepdims=True)
        dx, dw = _rms_bwd(xv, nw_ref[...], r, err * (1.0 / D))
        dx_ref[...] = dx

        @pl.when(i == 0)
        def _():
            dnw_ref[...] = jnp.zeros_like(dnw_ref)
            loss_ref[...] = jnp.zeros_like(loss_ref)

        dnw_ref[...] += dw
        loss_ref[...] += jnp.broadcast_to(part, loss_ref.shape)

    return _pc(
        body, grid=(nt,), name="final_loss",
        in_specs=[pl.BlockSpec((TM, D), lambda i: (i, 0)),
                  pl.BlockSpec((1, D), lambda i: (0, 0)),
                  pl.BlockSpec((TM, D), lambda i: (i, 0))],
        out_specs=[pl.BlockSpec((TM, D), lambda i: (i, 0)), pl.BlockSpec((1, D), lambda i: (0, 0)),
                   pl.BlockSpec((1, 128), lambda i: (0, 0))],
        out_shape=[SDS((S, D), f32), SDS((1, D), f32), SDS((1, 128), f32)],
        compiler_params=_cp(40),
    )(x, nw, target)


def _rot_half(t):
    lane = lax.broadcasted_iota(jnp.int32, t.shape, 1)
    first = (lane % EH) < (EH // 2)
    return jnp.where(first, -pltpu.roll(t, ATT - EH // 2, 1), pltpu.roll(t, EH // 2, 1))


def _rope_tables(pos_ref, freq_ref):
    ang = pos_ref[...].astype(f32) * freq_ref[...]
    return jnp.cos(ang), jnp.sin(ang)


def _split_residues(val, scr, outs):
    rows, cols = val.shape
    for j in range(cols // 128):
        scr[j] = val[:, 128 * j:128 * (j + 1)]
    for ref, d in outs:
        for j in range(cols // 128):
            for r in range(d):
                ref.at[r][:, 128 * j:128 * (j + 1)] = scr.at[j][pl.ds(r, rows // d, stride=d), :]


def _join_residues(ref, d, scr):
    rows, cols = scr.shape[1], ref.shape[2]
    for j in range(cols // 128):
        for r in range(d):
            scr.at[j][pl.ds(r, rows // d, stride=d), :] = ref.at[r][:, 128 * j:128 * (j + 1)]
    return jnp.concatenate([scr[j] for j in range(cols // 128)], axis=1)


def _res_spec(d, tile, cols):
    return pl.BlockSpec((d, tile // d, cols), lambda i: (0, i, 0))


def _inproj_fwd(x, nw, w_aug, pos, freq):
    TI = 256

    def body(x_ref, nw_ref, w_hbm, pos_ref, freq_ref, att_ref, att4_ref, att16_ref, pu_ref, dq_ref, dz_ref, dba_ref,
             w_scr, r_scr):
        @pl.when(pl.program_id(0) == 0)
        def _():
            pltpu.sync_copy(w_hbm, w_scr)

        h, _ = _rms(x_ref[...], nw_ref[...])
        proj = _dot(h.astype(bf16), w_scr[...])
        cos, sin = _rope_tables(pos_ref, freq_ref)
        q = proj[:, 0:ATT]
        k = proj[:, ATT:2 * ATT]
        att = jnp.concatenate([q * cos + _rot_half(q) * sin, k * cos + _rot_half(k) * sin, proj[:, 2 * ATT:3 * ATT]], axis=1)
        att_ref[...] = att
        _split_residues(att, r_scr, [(att4_ref, 4), (att16_ref, 16)])
        pu_ref[...] = proj[:, 768:1024]
        dq_ref[...] = proj[:, 1024:2560]
        dz_ref[...] = proj[:, 2560:3072]
        dba_ref[...] = proj[:, 3072:3200]

    return _pc(
        body, grid=(S // TI,), name="inproj_fwd",
        in_specs=[pl.BlockSpec((TI, D), lambda i: (i, 0)),
                  pl.BlockSpec((1, D), lambda i: (0, 0)),
                  pl.BlockSpec(memory_space=pl.ANY),
                  pl.BlockSpec((TI, 1), lambda i: (i, 0)),
                  pl.BlockSpec((1, ATT), lambda i: (0, 0))],
        out_specs=[pl.BlockSpec((TI, 768), lambda i: (i, 0)), _res_spec(4, TI, 768), _res_spec(16, TI, 768),
                   pl.BlockSpec((TI, 256), lambda i: (i, 0)),
                   pl.BlockSpec((TI, 1536), lambda i: (i, 0)), pl.BlockSpec((TI, 512), lambda i: (i, 0)),
                   pl.BlockSpec((TI, 128), lambda i: (i, 0))],
        out_shape=[SDS((S, 768), f32), SDS((4, S // 4, 768), f32), SDS((16, S // 16, 768), f32), SDS((S, 256), f32),
                   SDS((S, 1536), f32), SDS((S, 512), f32), SDS((S, 128), f32)],
        scratch_shapes=[pltpu.VMEM((D, INP), bf16), pltpu.VMEM((6, TI, 128), f32)],
        compiler_params=_cp(48),
    )(x, nw, w_aug, pos, freq)


def _inproj_bwd(x, nw, w_aug, pos, freq, dres, datt, datt4, datt16, dpu, ddq, ddz, ddba):
    TI = 256
    nt = S // TI

    def body(x_ref, nw_ref, w_hbm, pos_ref, freq_ref, dres_ref, datt_ref, datt4_ref, datt16_ref, dpu_ref, ddq_ref, ddz_ref,
             ddba_ref, dx_ref, dnw_ref, dw_hbm, w_scr, acc, r_scr):
        i = pl.program_id(0)

        @pl.when(i == 0)
        def _():
            pltpu.sync_copy(w_hbm, w_scr)
            acc[...] = jnp.zeros_like(acc)
            dnw_ref[...] = jnp.zeros_like(dnw_ref)

        xv = x_ref[...]
        hf, r = _rms(xv, nw_ref[...])
        h = hf.astype(bf16)
        cos, sin = _rope_tables(pos_ref, freq_ref)
        datt = datt_ref[...] + _join_residues(datt4_ref, 4, r_scr)
        datt = datt + _join_residues(datt16_ref, 16, r_scr)
        dq = datt[:, 0:ATT]
        dk = datt[:, ATT:2 * ATT]
        dq = dq * cos - _rot_half(dq) * sin
        dk = dk * cos - _rot_half(dk) * sin
        dproj = jnp.concatenate([dq, dk, datt[:, 2 * ATT:3 * ATT], dpu_ref[...], ddq_ref[...], ddz_ref[...], ddba_ref[...]],
                                axis=1).astype(bf16)
        acc[...] += _dot_tn(h, dproj)
        dh = _dot_nt(dproj, w_scr[...])
        dx, dw = _rms_bwd(xv, nw_ref[...], r, dh)
        dx_ref[...] = dres_ref[...] + dx
        dnw_ref[...] += dw

        @pl.when(i == nt - 1)
        def _():
            pltpu.sync_copy(acc, dw_hbm)

    return _pc(
        body, grid=(nt,), name="inproj_bwd",
        in_specs=[pl.BlockSpec((TI, D), lambda i: (i, 0)),
                  pl.BlockSpec((1, D), lambda i: (0, 0)),
                  pl.BlockSpec(memory_space=pl.ANY),
                  pl.BlockSpec((TI, 1), lambda i: (i, 0)),
                  pl.BlockSpec((1, ATT), lambda i: (0, 0)),
                  pl.BlockSpec((TI, D), lambda i: (i, 0)),
                  pl.BlockSpec((TI, 768), lambda i: (i, 0)), _res_spec(4, TI, 768), _res_spec(16, TI, 768),
                  pl.BlockSpec((TI, 256), lambda i: (i, 0)),
                  pl.BlockSpec((TI, 1536), lambda i: (i, 0)),
                  pl.BlockSpec((TI, 512), lambda i: (i, 0)),
                  pl.BlockSpec((TI, 128), lambda i: (i, 0))],
        out_specs=[pl.BlockSpec((TI, D), lambda i: (i, 0)), pl.BlockSpec((1, D), lambda i: (0, 0)),
                   pl.BlockSpec(memory_space=pl.ANY)],
        out_shape=[SDS((S, D), f32), SDS((1, D), f32), SDS((D, INP), f32)],
        scratch_shapes=[pltpu.VMEM((D, INP), bf16), pltpu.VMEM((D, INP), f32), pltpu.VMEM((6, TI, 128), f32)],
        compiler_params=_cp(56),
    )(x, nw, w_aug, pos, freq, dres, datt, datt4, datt16, dpu, ddq, ddz, ddba)


def _outproj_fwd(x, ya, yb, yc, blob_b, kw):
    def body(x_ref, ya_ref, yb_ref, yc_ref, w_ref, o_ref):
        ycat = jnp.concatenate([ya_ref[...], yb_ref[...], yc_ref[...]], axis=1).astype(bf16)
        o_ref[...] = x_ref[...] + _dot(ycat, w_ref[:, 0:256, :].reshape(D, D))

    return _pc(
        body, grid=(S // TM,), name="outproj_fwd",
        in_specs=[pl.BlockSpec((TM, D), lambda i: (i, 0)),
                  pl.BlockSpec((TM, 256), lambda i: (i, 0)),
                  pl.BlockSpec((TM, 256), lambda i: (i, 0)),
                  pl.BlockSpec((TM, 512), lambda i: (i, 0)),
                  pl.BlockSpec((NCH, FC, D), lambda i: (0, kw, 0))],
        out_specs=pl.BlockSpec((TM, D), lambda i: (i, 0)),
        out_shape=SDS((S, D), f32),
        compiler_params=_cp(40),
    )(x, ya, yb, yc, blob_b)


def _outproj_bwd(x, nw, dres, dh4, ya, yb, yc, blob_b, kw):
    nt = S // TM
    nparts = dh4.shape[0]

    def body(x_ref, nw_ref, dres_ref, dh_ref, ya_ref, yb_ref, yc_ref, w_ref, dx_ref, dnw_ref, dya_ref, dyb_ref, dyc_ref, dw_ref):
        i = pl.program_id(0)

        @pl.when(i == 0)
        def _():
            dw_ref[...] = jnp.zeros_like(dw_ref)
            dnw_ref[...] = jnp.zeros_like(dnw_ref)

        dh = dh_ref[0].astype(f32)
        for p in range(1, nparts):
            dh = dh + dh_ref[p].astype(f32)
        xv = x_ref[...]
        _, r = _rms(xv, nw_ref[...])
        dxn, dnw = _rms_bwd(xv, nw_ref[...], r, dh)
        dx = dres_ref[...] + dxn
        dx_ref[...] = dx
        dnw_ref[...] += dnw
        dyv = dx.astype(bf16)
        ycat = jnp.concatenate([ya_ref[...], yb_ref[...], yc_ref[...]], axis=1).astype(bf16)
        dw_ref[...] += _dot_tn(ycat, dyv)
        dcat = _dot_nt(dyv, w_ref[:, 0:256, :].reshape(D, D))
        dya_ref[...] = dcat[:, 0:256]
        dyb_ref[...] = dcat[:, 256:512]
        dyc_ref[...] = dcat[:, 512:1024]

    return _pc(
        body, grid=(nt,), name="outproj_bwd",
        in_specs=[pl.BlockSpec((TM, D), lambda i: (i, 0)),
                  pl.BlockSpec((1, D), lambda i: (0, 0)),
                  pl.BlockSpec((TM, D), lambda i: (i, 0)),
                  pl.BlockSpec((nparts, TM, D), lambda i: (0, i, 0)),
                  pl.BlockSpec((TM, 256), lambda i: (i, 0)),
                  pl.BlockSpec((TM, 256), lambda i: (i, 0)),
                  pl.BlockSpec((TM, 512), lambda i: (i, 0)),
                  pl.BlockSpec((NCH, FC, D), lambda i: (0, kw, 0))],
        out_specs=[pl.BlockSpec((TM, D), lambda i: (i, 0)), pl.BlockSpec((1, D), lambda i: (0, 0)),
                   pl.BlockSpec((TM, 256), lambda i: (i, 0)), pl.BlockSpec((TM, 256), lambda i: (i, 0)),
                   pl.BlockSpec((TM, 512), lambda i: (i, 0)), pl.BlockSpec((D, D), lambda i: (0, 0))],
        out_shape=[SDS((S, D), f32), SDS((1, D), f32), SDS((S, 256), f32), SDS((S, 256), f32), SDS((S, 512), f32),
                   SDS((D, D), f32)],
        compiler_params=_cp(48),
    )(x, nw, dres, dh4, ya, yb, yc, blob_b)


QT = NBLK
NB = S // QT


def _attn_block(q, kp, kc, vp, vc, first):
    kk = jnp.concatenate([kp, kc], axis=0).astype(bf16)
    vv = jnp.concatenate([vp, vc], axis=0).astype(bf16)
    qi = lax.broadcasted_iota(jnp.int32, (4 * QT, NBLK + QT), 0) % QT
    ki = lax.broadcasted_iota(jnp.int32, (4 * QT, NBLK + QT), 1)
    dist = NBLK + qi - ki
    valid = (dist >= 0) & (dist <= NBLK) & (jnp.logical_not(first) | (ki >= NBLK))
    head = lax.broadcasted_iota(jnp.int32, (1, ATT), 1) // EH
    masks = [(head == h).astype(f32) for h in range(4)]
    qs = jnp.concatenate([q * (mh * (1.0 / math.sqrt(EH))) for mh in masks], axis=0).astype(bf16)
    s = _dot_nt(qs, kk)
    s = jnp.where(valid, s, NEG)
    m = lax.stop_gradient(jnp.max(s, axis=-1, keepdims=True))
    p = jnp.exp(s - m)
    den = jnp.sum(p, axis=-1, keepdims=True)
    po = _dot((p * (1.0 / den)).astype(bf16), vv)
    lse = m + jnp.log(den)
    o = jnp.zeros((QT, ATT), f32)
    l = jnp.zeros((QT, ATT), f32)
    for h, mh in enumerate(masks):
        o = o + po[QT * h:QT * (h + 1)] * mh
        l = l + lse[QT * h:QT * (h + 1)] * mh
    return o, l


def _attn_specs(tile):
    own = lambda col: pl.BlockSpec((QT, ATT), lambda s: (tile(s), col))
    prev = lambda col: pl.BlockSpec((NBLK, ATT), lambda s: (jnp.maximum((QT // NBLK) * tile(s) - 1, 0), col))
    return [own(0), prev(1), own(1), prev(2), own(2)]


def _attn_fwd(qkv, per_seq):
    def body(q_ref, kp_ref, kc_ref, vp_ref, vc_ref, o_ref, l_ref):
        first = pl.program_id(0) % per_seq == 0
        o, l = _attn_block(q_ref[...], kp_ref[...], kc_ref[...], vp_ref[...], vc_ref[...], first)
        o_ref[...] = o
        l_ref[...] = l

    blk = pl.BlockSpec((QT, ATT), lambda t: (t, 0))
    return _pc(
        body, grid=(NB,), name="attn_fwd", in_specs=_attn_specs(lambda t: t), out_specs=[blk, blk],
        out_shape=[SDS((S, ATT), f32), SDS((S, ATT), f32)], compiler_params=_cp(32),
    )(qkv, qkv, qkv, qkv, qkv)


def _attn_block_bwd(q, kp, kc, vp, vc, o, l, do, dl, first):
    kk = jnp.concatenate([kp, kc], axis=0).astype(bf16)
    vv = jnp.concatenate([vp, vc], axis=0).astype(bf16)
    qi = lax.broadcasted_iota(jnp.int32, (4 * QT, NBLK + QT), 0) % QT
    ki = lax.broadcasted_iota(jnp.int32, (4 * QT, NBLK + QT), 1)
    dist = NBLK + qi - ki
    valid = (dist >= 0) & (dist <= NBLK) & (jnp.logical_not(first) | (ki >= NBLK))
    head = lax.broadcasted_iota(jnp.int32, (1, ATT), 1) // EH
    masks = [(head == h).astype(f32) for h in range(4)]
    scale = 1.0 / math.sqrt(EH)
    stack = lambda f: jnp.concatenate([f(mh) for mh in masks], axis=0)
    qs = stack(lambda mh: q * (mh * scale)).astype(bf16)
    s = jnp.where(valid, _dot_nt(qs, kk), NEG)
    lse = stack(lambda mh: jnp.max(jnp.where(mh > 0.0, l, NEG), axis=1, keepdims=True))
    p = jnp.exp(s - lse)
    dos = stack(lambda mh: do * mh).astype(bf16)
    dvv = _dot_tn(p.astype(bf16), dos)
    dp = _dot_nt(dos, vv)
    delta = stack(lambda mh: jnp.sum(do * o * mh, axis=1, keepdims=True))
    dlse = stack(lambda mh: jnp.sum(dl * mh, axis=1, keepdims=True))
    ds = (p * (dp - delta + dlse)).astype(bf16)
    dqs = _dot(ds, kk)
    dq = jnp.zeros((QT, ATT), f32)
    for h, mh in enumerate(masks):
        dq = dq + dqs[QT * h:QT * (h + 1)] * (mh * scale)
    dkk = _dot_tn(ds, qs)
    return dq, dkk[:NBLK], dkk[NBLK:], dvv[:NBLK], dvv[NBLK:]


def _attn_bwd(qkv, o, l, do, dl, per_seq):
    def body(q_ref, kp_ref, kc_ref, vp_ref, vc_ref, ofw_ref, lfw_ref, do_ref, dl_ref, o_ref, k_carry, v_carry):
        step = pl.program_id(0)

        @pl.when(step == 0)
        def _():
            k_carry[...] = jnp.zeros_like(k_carry)
            v_carry[...] = jnp.zeros_like(v_carry)

        t = NB - 1 - step
        first = t % per_seq == 0
        last = t % per_seq == per_seq - 1
        dq, dkp, dkc, dvp, dvc = _attn_block_bwd(q_ref[...], kp_ref[...], kc_ref[...], vp_ref[...], vc_ref[...], ofw_ref[...],
                                                 lfw_ref[...], do_ref[...], dl_ref[...], first)
        o_ref[:, 0:ATT] = dq
        o_ref[:, ATT:2 * ATT] = dkc
        o_ref[:, 2 * ATT:3 * ATT] = dvc
        o_ref[QT - NBLK:QT, ATT:2 * ATT] += jnp.where(last, 0.0, k_carry[...])
        o_ref[QT - NBLK:QT, 2 * ATT:3 * ATT] += jnp.where(last, 0.0, v_carry[...])
        k_carry[...] = dkp
        v_carry[...] = dvp

    rev = lambda s: NB - 1 - s
    blk = pl.BlockSpec((QT, ATT), lambda s: (rev(s), 0))
    return _pc(
        body, grid=(NB,), name="attn_bwd", in_specs=_attn_specs(rev) + [blk, blk, blk, blk],
        out_specs=pl.BlockSpec((QT, 768), lambda s: (rev(s), 0)),
        out_shape=SDS((S, 768), f32), scratch_shapes=[pltpu.VMEM((NBLK, ATT), f32)] * 2, compiler_params=_cp(40),
    )(qkv, qkv, qkv, qkv, qkv, o, l, do, dl)


def _merge_weights(l0, l1, l2):
    m = jnp.maximum(jnp.maximum(l0, l1), l2)
    e0, e1, e2 = jnp.exp(l0 - m), jnp.exp(l1 - m), jnp.exp(l2 - m)
    tot = e0 + e1 + e2
    return e0 / tot, e1 / tot, e2 / tot


def _merge_specs():
    nat = pl.BlockSpec((TM, ATT), lambda i: (i, 0))
    return nat, _res_spec(4, TM, ATT), _res_spec(16, TM, ATT)


def _merge_fwd(o1, l1, o4, l4, o16, l16):
    def body(o1_ref, l1_ref, o4_ref, l4_ref, o16_ref, l16_ref, y_ref, scr):
        o4v, l4v = _join_residues(o4_ref, 4, scr), _join_residues(l4_ref, 4, scr)
        o16v, l16v = _join_residues(o16_ref, 16, scr), _join_residues(l16_ref, 16, scr)
        w0, w1, w2 = _merge_weights(l1_ref[...], l4v, l16v)
        y_ref[...] = w0 * o1_ref[...] + w1 * o4v + w2 * o16v

    nat, r4, r16 = _merge_specs()
    return _pc(body, grid=(S // TM,), name="merge_fwd", in_specs=[nat, nat, r4, r4, r16, r16],
                          out_specs=nat, out_shape=SDS((S, ATT), f32), scratch_shapes=[pltpu.VMEM((2, TM, 128), f32)],
                          compiler_params=_cp(32))(o1, l1, o4, l4, o16, l16)


def _merge_bwd(o1, l1, o4, l4, o16, l16, dy):
    def body(o1_ref, l1_ref, o4_ref, l4_ref, o16_ref, l16_ref, dy_ref, do1_ref, dl1_ref, do4_ref, dl4_ref, do16_ref, dl16_ref, scr):
        o4v, l4v = _join_residues(o4_ref, 4, scr), _join_residues(l4_ref, 4, scr)
        o16v, l16v = _join_residues(o16_ref, 16, scr), _join_residues(l16_ref, 16, scr)
        o1v = o1_ref[...]
        w0, w1, w2 = _merge_weights(l1_ref[...], l4v, l16v)
        y = w0 * o1v + w1 * o4v + w2 * o16v
        dyv = dy_ref[...]
        do1_ref[...] = w0 * dyv
        dl1_ref[...] = w0 * (o1v - y) * dyv
        _split_residues(w1 * dyv, scr, [(do4_ref, 4)])
        _split_residues(w1 * (o4v - y) * dyv, scr, [(dl4_ref, 4)])
        _split_residues(w2 * dyv, scr, [(do16_ref, 16)])
        _split_residues(w2 * (o16v - y) * dyv, scr, [(dl16_ref, 16)])

    nat, r4, r16 = _merge_specs()
    return _pc(body, grid=(S // TM,), name="merge_bwd", in_specs=[nat, nat, r4, r4, r16, r16, nat],
                          out_specs=[nat, nat, r4, r4, r16, r16],
                          out_shape=[SDS((S, ATT), f32)] * 2 + [SDS((4, S // 4, ATT), f32)] * 2 + [SDS((16, S // 16, ATT), f32)] * 2,
                          scratch_shapes=[pltpu.VMEM((2, TM, 128), f32)], compiler_params=_cp(32))(o1, l1, o4, l4, o16, l16, dy)


HALO = 16


def _pool_consts(i, rows):
    grp = lax.broadcasted_iota(jnp.int32, (rows, 256), 1) // 64
    t = i * TM + lax.broadcasted_iota(jnp.int32, (rows, 256), 0)
    win = jnp.where(grp == 0, 2, jnp.where(grp == 1, 4, jnp.where(grp == 2, 8, 16)))
    cnt = jnp.minimum(t + 1, win).astype(f32)
    return grp, cnt


def _pool_select(grp, s2, s4, s8, s16):
    return jnp.where(grp == 0, s2, jnp.where(grp == 1, s4, jnp.where(grp == 2, s8, s16)))


def _pooled(i, cur, halo):
    xx = jnp.concatenate([halo, cur], axis=0)
    s2 = xx + pltpu.roll(xx, 1, 0)
    s4 = s2 + pltpu.roll(s2, 2, 0)
    s8 = s4 + pltpu.roll(s4, 4, 0)
    s16 = s8 + pltpu.roll(s8, 8, 0)
    grp, cnt = _pool_consts(i, TM)
    tot = _pool_select(grp, s2[HALO:], s4[HALO:], s8[HALO:], s16[HALO:])
    return tot / cnt - cur


def _pool_fwd(u, wp, scale):
    def body(u_ref, halo_ref, wp_ref, sc_ref, y_ref):
        i = pl.program_id(0)
        halo = halo_ref[...] * (i > 0).astype(f32)
        pooled = _pooled(i, u_ref[...], halo)
        y_ref[...] = _dot(pooled.astype(bf16), wp_ref[...]) * sc_ref[...]

    return _pc(
        body, grid=(S // TM,), name="pool_fwd",
        in_specs=[pl.BlockSpec((TM, 256), lambda i: (i, 0)),
                  pl.BlockSpec((HALO, 256), lambda i: (jnp.maximum(i * (TM // HALO) - 1, 0), 0)),
                  pl.BlockSpec((256, 256), lambda i: (0, 0)),
                  pl.BlockSpec((1, 256), lambda i: (0, 0))],
        out_specs=pl.BlockSpec((TM, 256), lambda i: (i, 0)), out_shape=SDS((S, 256), f32), compiler_params=_cp(32),
    )(u, u, wp, scale)


def _pool_bwd(u, wp, scale, dy):
    nt = S // TM

    def body(u_ref, halo_ref, wp_ref, sc_ref, dy_ref, dyn_ref, du_ref, dwp_ref, dsc_ref):
        i = pl.program_id(0)

        @pl.when(i == 0)
        def _():
            dwp_ref[...] = jnp.zeros_like(dwp_ref)
            dsc_ref[...] = jnp.zeros_like(dsc_ref)

        halo = halo_ref[...] * (i > 0).astype(f32)
        pooled = _pooled(i, u_ref[...], halo).astype(bf16)
        dyv = dy_ref[...]
        dsc_ref[...] += jnp.sum(dyv * _dot(pooled, wp_ref[...]), axis=0, keepdims=True)
        dys = (dyv * sc_ref[...]).astype(bf16)
        dwp_ref[...] += _dot_tn(pooled, dys)
        dpool = _dot_nt(dys, wp_ref[...])
        grp, cnt = _pool_consts(i, TM)
        dyn = ((dyn_ref[...] * (i < nt - 1).astype(f32)) * sc_ref[...]).astype(bf16)
        _, cntn = _pool_consts(i + 1, HALO)
        zn = _dot_nt(dyn, wp_ref[...]) / cntn
        zz = jnp.concatenate([dpool / cnt, zn], axis=0)
        n = TM + HALO
        a2 = zz + pltpu.roll(zz, n - 1, 0)
        a4 = a2 + pltpu.roll(a2, n - 2, 0)
        a8 = a4 + pltpu.roll(a4, n - 4, 0)
        a16 = a8 + pltpu.roll(a8, n - 8, 0)
        du_ref[...] = _pool_select(grp, a2[:TM], a4[:TM], a8[:TM], a16[:TM]) - dpool

    return _pc(
        body, grid=(nt,), name="pool_bwd",
        in_specs=[pl.BlockSpec((TM, 256), lambda i: (i, 0)),
                  pl.BlockSpec((HALO, 256), lambda i: (jnp.maximum(i * (TM // HALO) - 1, 0), 0)),
                  pl.BlockSpec((256, 256), lambda i: (0, 0)),
                  pl.BlockSpec((1, 256), lambda i: (0, 0)),
                  pl.BlockSpec((TM, 256), lambda i: (i, 0)),
                  pl.BlockSpec((HALO, 256), lambda i: (jnp.minimum((i + 1) * (TM // HALO), S // HALO - 1), 0))],
        out_specs=[pl.BlockSpec((TM, 256), lambda i: (i, 0)), pl.BlockSpec((256, 256), lambda i: (0, 0)),
                   pl.BlockSpec((1, 256), lambda i: (0, 0))],
        out_shape=[SDS((S, 256), f32), SDS((256, 256), f32), SDS((1, 256), f32)], compiler_params=_cp(32),
    )(u, u, wp, scale, dy, dy)


CW = 3 * DNW
CHALO = 8
TC = 256


def _conv_fwd(u, w):
    def body(u_ref, halo_ref, w_ref, c_ref):
        i = pl.program_id(0)
        xx = jnp.concatenate([halo_ref[...] * (i > 0).astype(f32), u_ref[...]], axis=0)
        c = (w_ref[3:4, :] * xx + w_ref[2:3, :] * pltpu.roll(xx, 1, 0) + w_ref[1:2, :] * pltpu.roll(xx, 2, 0)
             + w_ref[0:1, :] * pltpu.roll(xx, 3, 0))
        c_ref[...] = c[CHALO:]

    return _pc(
        body, grid=(S // TC,), name="conv_fwd",
        in_specs=[pl.BlockSpec((TC, CW), lambda i: (i, 0)),
                  pl.BlockSpec((CHALO, CW), lambda i: (jnp.maximum(i * (TC // CHALO) - 1, 0), 0)),
                  pl.BlockSpec((8, CW), lambda i: (0, 0))],
        out_specs=pl.BlockSpec((TC, CW), lambda i: (i, 0)), out_shape=SDS((S, CW), f32), compiler_params=_cp(32),
    )(u, u, w)


def _conv_bwd(u, w, dc):
    nt = S // TC

    def body(u_ref, halo_ref, w_ref, dc_ref, dcn_ref, du_ref, dw_ref):
        i = pl.program_id(0)

        @pl.when(i == 0)
        def _():
            dw_ref[...] = jnp.zeros_like(dw_ref)

        dcv = dc_ref[...]
        zz = jnp.concatenate([dcv, dcn_ref[...] * (i < nt - 1).astype(f32)], axis=0)
        n = TC + CHALO
        du = (w_ref[3:4, :] * zz + w_ref[2:3, :] * pltpu.roll(zz, n - 1, 0) + w_ref[1:2, :] * pltpu.roll(zz, n - 2, 0)
              + w_ref[0:1, :] * pltpu.roll(zz, n - 3, 0))
        du_ref[...] = du[:TC]
        xx = jnp.concatenate([halo_ref[...] * (i > 0).astype(f32), u_ref[...]], axis=0)
        for j in range(4):
            shifted = xx if j == 3 else pltpu.roll(xx, 3 - j, 0)
            dw_ref[j:j + 1, :] += jnp.sum(dcv * shifted[CHALO:], axis=0, keepdims=True)

    return _pc(
        body, grid=(nt,), name="conv_bwd",
        in_specs=[pl.BlockSpec((TC, CW), lambda i: (i, 0)),
                  pl.BlockSpec((CHALO, CW), lambda i: (jnp.maximum(i * (TC // CHALO) - 1, 0), 0)),
                  pl.BlockSpec((8, CW), lambda i: (0, 0)),
                  pl.BlockSpec((TC, CW), lambda i: (i, 0)),
                  pl.BlockSpec((CHALO, CW), lambda i: (jnp.minimum((i + 1) * (TC // CHALO), S // CHALO - 1), 0))],
        out_specs=[pl.BlockSpec((TC, CW), lambda i: (i, 0)), pl.BlockSpec((8, CW), lambda i: (0, 0))],
        out_shape=[SDS((S, CW), f32), SDS((8, CW), f32)], compiler_params=_cp(32),
    )(u, u, w, dc, dc)


TL = 512
NCL = TL // CH


def _ein(spec, a, b):
    return jnp.einsum(spec, a.astype(bf16), b.astype(bf16), preferred_element_type=f32)


def _ein_ct(spec, x, y, ct_first):
    ct = x if ct_first else y
    hi = ct.astype(bf16)
    lo = ct - hi.astype(f32)
    if ct_first:
        return _ein(spec, hi, y) + _ein(spec, lo, y)
    return _ein(spec, x, hi) + _ein(spec, x, lo)


def _bf16_dot(spec, grad_a, grad_b):
    @jax.custom_vjp
    def dot(a, b):
        return _ein(spec, a, b)

    def fwd(a, b):
        return _ein(spec, a, b), (a, b)

    def bwd(res, ct):
        a, b = res
        return grad_a(a, b, ct), grad_b(a, b, ct)

    dot.defvjp(fwd, bwd)
    return dot


def _bdot(a, b):
    return _ein('nik,nkj->nij', a, b)


def _bdot_nt(a, b):
    return _ein('nik,njk->nij', a, b)


def _bdot_tn(a, b):
    return _ein('nki,nkj->nij', a, b)


_mm = _bf16_dot('ik,kj->ij', lambda a, b, ct: _ein_ct('ij,kj->ik', ct, b, True), lambda a, b, ct: _ein_ct('ik,ij->kj', a, ct, False))
_mm_tn = _bf16_dot('ki,kj->ij', lambda a, b, ct: _ein_ct('kj,ij->ki', b, ct, False),
                   lambda a, b, ct: _ein_ct('ki,ij->kj', a, ct, False))


@jax.custom_vjp
def _inv_unit_lower(a):
    ii = lax.broadcasted_iota(jnp.int32, (1, CH, CH), 1)
    jj = lax.broadcasted_iota(jnp.int32, (1, CH, CH), 2)
    t = (ii == jj).astype(f32) - a
    p = a
    for _ in range(5):
        p = _bdot(p, p)
        t = t + _bdot(t, p)
    return t


def _inv_unit_lower_fwd(a):
    t = _inv_unit_lower(a)
    return t, t


def _inv_unit_lower_bwd(t, dt):
    return (-_bdot_tn(t, _bdot_nt(dt, t)),)


_inv_unit_lower.defvjp(_inv_unit_lower_fwd, _inv_unit_lower_bwd)


def _dn_local(c, dba, a_row, b_row):
    act = c * jax.nn.sigmoid(c)
    lane = lax.broadcasted_iota(jnp.int32, (TL, 128), 1)
    beta_all = jax.nn.sigmoid(dba)
    xs = dba + b_row
    softplus = jnp.maximum(xs, 0.0) + jnp.log(1.0 + jnp.exp(-jnp.abs(xs)))
    g_all = -jnp.exp(a_row) * softplus
    ii = lax.broadcasted_iota(jnp.int32, (1, CH, CH), 1)
    jj = lax.broadcasted_iota(jnp.int32, (1, CH, CH), 2)
    lower = jj <= ii
    strict = jj < ii
    eye = (ii == jj).astype(f32)
    us, ws, qgs, kds, intras = [], [], [], [], []
    aux = jnp.zeros((TL, 128), f32)
    for h in range(4):
        q = act[:, DH * h:DH * (h + 1)]
        k = act[:, DNW + DH * h:DNW + DH * (h + 1)]
        v = act[:, 2 * DNW + DH * h:2 * DNW + DH * (h + 1)]
        q = q * lax.rsqrt(jnp.sum(q * q, axis=-1, keepdims=True) + EPS) * (DH ** -0.5)
        k = k * lax.rsqrt(jnp.sum(k * k, axis=-1, keepdims=True) + EPS)
        beta = jnp.sum(jnp.where(lane == h, beta_all, 0.0), axis=1, keepdims=True)
        g = jnp.sum(jnp.where(lane == 4 + h, g_all, 0.0), axis=1, keepdims=True)
        q3, k3, v3 = q.reshape(NCL, CH, DH), k.reshape(NCL, CH, DH), v.reshape(NCL, CH, DH)
        beta3, g3 = beta.reshape(NCL, CH, 1), g.reshape(NCL, CH, 1)
        g_row = jnp.sum(eye * g3, axis=1, keepdims=True)
        gc_col = jnp.sum(jnp.where(lower, g_row, 0.0), axis=2, keepdims=True)
        gc_row = jnp.sum(jnp.where(ii <= jj, g3, 0.0), axis=1, keepdims=True)
        diff = gc_col - gc_row
        decay = jnp.where(lower, jnp.exp(jnp.where(lower, diff, 0.0)), 0.0)
        kb = k3 * beta3
        vb = v3 * beta3
        a = jnp.where(strict, _bdot_nt(kb, k3) * decay, 0.0)
        t = _inv_unit_lower(a)
        u3 = _bdot(t, vb)
        w3 = _bdot(t, kb * jnp.exp(gc_col))
        intra = jnp.where(lower, _bdot_nt(q3, k3) * decay, 0.0)
        g_last = jnp.sum(g3, axis=1, keepdims=True)
        us.append(u3.reshape(TL, DH))
        ws.append(w3.reshape(TL, DH))
        qgs.append((q3 * jnp.exp(gc_col)).reshape(TL, DH))
        kds.append((k3 * jnp.exp(g_last - gc_col)).reshape(TL, DH))
        intras.append(intra.reshape(TL, CH))
        e_last = jnp.broadcast_to(jnp.exp(g_last), (NCL, CH, 1)).reshape(TL, 1)
        aux = aux + jnp.where(lane == h, e_last, 0.0)
    cat = lambda xs: jnp.concatenate(xs, axis=1)
    return cat(us), cat(ws), cat(qgs), cat(kds), jnp.stack(intras, axis=0), aux


def _dn_local_fwd(c, dba, par):
    def body(c_ref, dba_ref, par_ref, u_ref, w_ref, qg_ref, kd_ref, in_ref, aux_ref):
        u, w, qg, kd, intra, aux = _dn_local(c_ref[...], dba_ref[...], par_ref[0:1, :], par_ref[1:2, :])
        u_ref[...] = u
        w_ref[...] = w
        qg_ref[...] = qg
        kd_ref[...] = kd
        in_ref[...] = intra
        aux_ref[...] = aux

    wide = pl.BlockSpec((TL, DNW), lambda i: (i, 0))
    return _pc(
        body, grid=(S // TL,), name="dn_local_fwd",
        in_specs=[pl.BlockSpec((TL, CW), lambda i: (i, 0)), pl.BlockSpec((TL, 128), lambda i: (i, 0)),
                  pl.BlockSpec((8, 128), lambda i: (0, 0))],
        out_specs=[wide, wide, wide, wide, pl.BlockSpec((4, TL, CH), lambda i: (0, i, 0)),
                   pl.BlockSpec((TL, 128), lambda i: (i, 0))],
        out_shape=[SDS((S, DNW), f32)] * 4 + [SDS((4, S, CH), f32), SDS((S, 128), f32)], compiler_params=_cp(48),
    )(c, dba, par)


def _dn_local_bwd(c, dba, par, du, dw, dqg, dkd, dintra, daux):
    def body(c_ref, dba_ref, par_ref, du_ref, dw_ref, dqg_ref, dkd_ref, din_ref, daux_ref, dc_ref, ddba_ref, dpar_ref):
        @pl.when(pl.program_id(0) == 0)
        def _():
            dpar_ref[...] = jnp.zeros_like(dpar_ref)

        _, vjp = jax.vjp(_dn_local, c_ref[...], dba_ref[...], par_ref[0:1, :], par_ref[1:2, :])
        dc, ddba, da_row, db_row = vjp((du_ref[...], dw_ref[...], dqg_ref[...], dkd_ref[...], din_ref[...], daux_ref[...]))
        dc_ref[...] = dc
        ddba_ref[...] = ddba
        dpar_ref[0:1, :] += da_row
        dpar_ref[1:2, :] += db_row

    wide = pl.BlockSpec((TL, DNW), lambda i: (i, 0))
    return _pc(
        body, grid=(S // TL,), name="dn_local_bwd",
        in_specs=[pl.BlockSpec((TL, CW), lambda i: (i, 0)), pl.BlockSpec((TL, 128), lambda i: (i, 0)),
                  pl.BlockSpec((8, 128), lambda i: (0, 0)), wide, wide, wide, wide,
                  pl.BlockSpec((4, TL, CH), lambda i: (0, i, 0)), pl.BlockSpec((TL, 128), lambda i: (i, 0))],
        out_specs=[pl.BlockSpec((TL, CW), lambda i: (i, 0)), pl.BlockSpec((TL, 128), lambda i: (i, 0)),
                   pl.BlockSpec((8, 128), lambda i: (0, 0))],
        out_shape=[SDS((S, CW), f32), SDS((S, 128), f32), SDS((8, 128), f32)], compiler_params=_cp(56),
    )(c, dba, par, du, dw, dqg, dkd, dintra, daux)


def _dn_step(state, u, w, qg, kd, intra, aux):
    lane = lax.broadcasted_iota(jnp.int32, (CH, 128), 1)
    row = lax.broadcasted_iota(jnp.int32, (CH, 128), 0)
    outs, states = [], []
    for h in range(4):
        sl = slice(DH * h, DH * (h + 1))
        st = state[h]
        e = jnp.sum(jnp.sum(jnp.where((lane == h) & (row == 0), aux, 0.0), axis=1, keepdims=True), axis=0, keepdims=True)
        v_new = u[:, sl] - _mm(w[:, sl], st)
        outs.append(_mm(qg[:, sl], st) + _mm(intra[h], v_new))
        states.append(st * e + _mm_tn(kd[:, sl], v_new))
    return jnp.concatenate(outs, axis=1), jnp.stack(states, axis=0)


CPS = 8
NSTEP = NCHUNK // CPS


def _dn_rec_specs(index):
    wide = pl.BlockSpec((CPS * CH, DNW), lambda n: (index(n), 0))
    inb = pl.BlockSpec((4, CPS * CH, CH), lambda n: (0, index(n), 0))
    auxb = pl.BlockSpec((CPS * CH, 128), lambda n: (index(n), 0))
    stb = pl.BlockSpec((CPS, 4, DH, DH), lambda n: (index(n), 0, 0, 0))
    return wide, inb, auxb, stb


def _dn_rec_fwd(u, w, qg, kd, intra, aux):
    def body(u_ref, w_ref, qg_ref, kd_ref, in_ref, aux_ref, o_ref, st_ref, st_scr):
        @pl.when(pl.program_id(0) == 0)
        def _():
            st_scr[...] = jnp.zeros_like(st_scr)

        st = st_scr[...]
        for k in range(CPS):
            rows = slice(CH * k, CH * (k + 1))
            st_ref[k] = st
            o, st = _dn_step(st, u_ref[rows, :], w_ref[rows, :], qg_ref[rows, :], kd_ref[rows, :], in_ref[:, rows, :],
                             aux_ref[rows, :])
            o_ref[rows, :] = o
        st_scr[...] = st

    wide, inb, auxb, stb = _dn_rec_specs(lambda n: n)
    return _pc(
        body, grid=(NSTEP,), name="dn_rec_fwd", in_specs=[wide, wide, wide, wide, inb, auxb], out_specs=[wide, stb],
        out_shape=[SDS((S, DNW), f32), SDS((NCHUNK, 4, DH, DH), f32)],
        scratch_shapes=[pltpu.VMEM((4, DH, DH), f32)], compiler_params=_cp(32),
    )(u, w, qg, kd, intra, aux)


def _dn_rec_bwd(u, w, qg, kd, intra, aux, states, do):
    def body(u_ref, w_ref, qg_ref, kd_ref, in_ref, aux_ref, st_ref, do_ref,
             du_ref, dw_ref, dqg_ref, dkd_ref, din_ref, daux_ref, ds_scr):
        @pl.when(pl.program_id(0) == 0)
        def _():
            ds_scr[...] = jnp.zeros_like(ds_scr)

        ds = ds_scr[...]
        for k in reversed(range(CPS)):
            rows = slice(CH * k, CH * (k + 1))
            _, vjp = jax.vjp(_dn_step, st_ref[k], u_ref[rows, :], w_ref[rows, :], qg_ref[rows, :], kd_ref[rows, :],
                             in_ref[:, rows, :], aux_ref[rows, :])
            ds, du, dw, dqg, dkd, din, daux = vjp((do_ref[rows, :], ds))
            du_ref[rows, :] = du
            dw_ref[rows, :] = dw
            dqg_ref[rows, :] = dqg
            dkd_ref[rows, :] = dkd
            din_ref[:, rows, :] = din
            daux_ref[rows, :] = daux
        ds_scr[...] = ds

    wide, inb, auxb, stb = _dn_rec_specs(lambda n: NSTEP - 1 - n)
    return _pc(
        body, grid=(NSTEP,), name="dn_rec_bwd", in_specs=[wide, wide, wide, wide, inb, auxb, stb, wide],
        out_specs=[wide, wide, wide, wide, inb, auxb],
        out_shape=[SDS((S, DNW), f32)] * 4 + [SDS((4, S, CH), f32), SDS((S, 128), f32)],
        scratch_shapes=[pltpu.VMEM((4, DH, DH), f32)], compiler_params=_cp(40),
    )(u, w, qg, kd, intra, aux, states, do)


def _dn_post(o, z, nw):
    parts = []
    for h in range(4):
        sl = slice(DH * h, DH * (h + 1))
        oh = o[:, sl]
        y = oh * lax.rsqrt(jnp.mean(oh * oh, axis=-1, keepdims=True) + EPS) * nw
        zh = z[:, sl]
        parts.append(y * (zh * jax.nn.sigmoid(zh)))
    return jnp.concatenate(parts, axis=1)


def _dn_post_fwd(o, z, nw):
    def body(o_ref, z_ref, nw_ref, y_ref):
        y_ref[...] = _dn_post(o_ref[...], z_ref[...], nw_ref[...])

    wide = pl.BlockSpec((TM, DNW), lambda i: (i, 0))
    return _pc(body, grid=(S // TM,), name="dn_post_fwd",
                          in_specs=[wide, wide, pl.BlockSpec((1, 128), lambda i: (0, 0))], out_specs=wide,
                          out_shape=SDS((S, DNW), f32), compiler_params=_cp(32))(o, z, nw)


def _dn_post_bwd(o, z, nw, dy):
    def body(o_ref, z_ref, nw_ref, dy_ref, do_ref, dz_ref, dnw_ref):
        @pl.when(pl.program_id(0) == 0)
        def _():
            dnw_ref[...] = jnp.zeros_like(dnw_ref)

        _, vjp = jax.vjp(_dn_post, o_ref[...], z_ref[...], nw_ref[...])
        do, dz, dnw = vjp(dy_ref[...])
        do_ref[...] = do
        dz_ref[...] = dz
        dnw_ref[...] += dnw

    wide = pl.BlockSpec((TM, DNW), lambda i: (i, 0))
    one = pl.BlockSpec((1, 128), lambda i: (0, 0))
    return _pc(body, grid=(S // TM,), name="dn_post_bwd", in_specs=[wide, wide, one, wide],
                          out_specs=[wide, wide, one], out_shape=[SDS((S, DNW), f32), SDS((S, DNW), f32), SDS((1, 128), f32)],
                          compiler_params=_cp(32))(o, z, nw, dy)


def _row_tile(rows, width, itemsize=4, target=2 * 1024 * 1024):
    best = None
    for t in range(16, rows + 1, 16):
        if rows % t == 0 and t * width * itemsize <= target:
            best = t
    return best if best is not None else rows


def _sum_pieces(pieces, out_dtype, name):
    n, rows, width = pieces.shape
    tr = _row_tile(rows, width * n)

    def body(p_ref, o_ref):
        acc = p_ref[0].astype(f32)
        for s in range(1, n):
            acc = acc + p_ref[s].astype(f32)
        o_ref[...] = acc.astype(out_dtype)

    return _pc(body, grid=(rows // tr,), name=name,
                          in_specs=[pl.BlockSpec((n, tr, width), lambda i: (0, i, 0))],
                          out_specs=pl.BlockSpec((tr, width), lambda i: (i, 0)),
                          out_shape=SDS((rows, width), out_dtype), compiler_params=_cp(32))(pieces)


def _sum_core_pair(part, got, c_arr):
    n, rows, width = part.shape
    half = rows // 2
    tr = _row_tile(half, width, itemsize=2)
    nt = half // tr

    def body(c_ref, p_ref, g_ref, o_ref):
        o_ref[...] = (p_ref[...].astype(f32) + g_ref[...].astype(f32)).astype(bf16)

    gs = pltpu.PrefetchScalarGridSpec(
        num_scalar_prefetch=1, grid=(n, nt),
        in_specs=[pl.BlockSpec((1, tr, width), lambda j, i, c: (j, c[0] * nt + i, 0)),
                  pl.BlockSpec((1, tr, width), lambda j, i, c: (j, i, 0))],
        out_specs=pl.BlockSpec((1, tr, width), lambda j, i, c: (j, i, 0)))
    return _pc(body, grid_spec=gs, name="sum_core_pair", out_shape=SDS((n, half, width), bf16),
                          compiler_params=_cp(32))(c_arr, part, got)


def _sum_chips(pieces, c_arr, full, row0, total_rows):
    n, half, width = pieces.shape
    tr = max(t for t in range(16, 257, 16) if half % t == 0 and row0 % t == 0)
    nt = half // tr

    def body(c_ref, p_ref, *rest):
        o_ref = rest[-1]
        acc = p_ref[0].astype(f32)
        for s in range(1, n):
            acc = acc + p_ref[s].astype(f32)
        o_ref[...] = acc

    gs = pltpu.PrefetchScalarGridSpec(
        num_scalar_prefetch=1, grid=(nt,),
        in_specs=[pl.BlockSpec((n, tr, width), lambda i, c: (0, i, 0))] + ([] if full is None else [ANY]),
        out_specs=pl.BlockSpec((tr, width), lambda i, c: (row0 // tr + c[0] * nt + i, 0)))
    args = (c_arr, pieces) if full is None else (c_arr, pieces, full)
    return _pc(body, grid_spec=gs, name="sum_chips", out_shape=SDS((total_rows, width), f32),
                          input_output_aliases={} if full is None else {2: 0}, compiler_params=_cp(32))(*args)


def _adamw_math(w, g, m, v):
    mn = ADAM_B1 * m + (1.0 - ADAM_B1) * g
    vn = ADAM_B2 * v + (1.0 - ADAM_B2) * (g * g)
    m_hat = mn / (1.0 - ADAM_B1 ** ADAM_STEP)
    v_hat = vn / (1.0 - ADAM_B2 ** ADAM_STEP)
    return -ADAM_LR * (m_hat / (jnp.sqrt(v_hat) + ADAM_EPS) + ADAM_WD * w), mn, vn


def _adamw(w, g, m, v, name):
    rows, width = w.shape
    tr = _row_tile(rows, width * 7, target=12 * 1024 * 1024)

    def body(w_ref, g_ref, m_ref, v_ref, d_ref, nm_ref, nv_ref):
        d_ref[...], nm_ref[...], nv_ref[...] = _adamw_math(w_ref[...], g_ref[...], m_ref[...], v_ref[...])

    blk = pl.BlockSpec((tr, width), lambda i: (i, 0))
    return _pc(body, grid=(rows // tr,), name=name, in_specs=[blk] * 4, out_specs=[blk] * 3,
                          out_shape=[SDS((rows, width), f32)] * 3, compiler_params=_cp(40))(w, g, m, v)


def _adamw_rows(w, m, v, gblob, tr, first_tile, name):
    layers, rows, width = w.shape

    def body(w_ref, g_ref, m_ref, v_ref, d_ref, nm_ref, nv_ref):
        d_ref[0], nm_ref[0], nv_ref[0] = _adamw_math(w_ref[0], g_ref[...], m_ref[0], v_ref[0])

    blk = pl.BlockSpec((1, tr, width), lambda l, i: (l, i, 0))
    gblk = pl.BlockSpec((tr, width), lambda l, i: (first_tile(l) + i, 0))
    return _pc(body, grid=(layers, rows // tr), name=name, in_specs=[blk, gblk, blk, blk], out_specs=[blk] * 3,
                          out_shape=[SDS(w.shape, f32)] * 3, compiler_params=_cp(40))(w, gblob, m, v)


ANY = pl.BlockSpec(memory_space=pl.ANY)


def _place():
    x, y, c = lax.axis_index("x"), lax.axis_index("y"), lax.axis_index("c")
    chips = [(1 - x, y), (x, 1 - y), (1 - x, 1 - y)]
    return x, y, c, chips


NQ_ICI = 4
NQ_D2D = 8


def _chunks(rows, want):
    n = max(k for k in range(1, want + 1) if rows % k == 0 and (rows // k) % 16 == 0)
    step = rows // n
    return [(q * step, step) for q in range(n)]


def _scatter_copies(ins, outs, ssem, rsem, lsem):
    x, y, c, chips = _place()
    me = (x, y, c)
    locals_, sends, lands = [], [], []
    for b in range(len(ins)):
        for q, (off, n) in enumerate(_chunks(ins[b].shape[1], NQ_ICI)):
            rows = pl.ds(off, n)
            mine = outs[b].at[2 * x + y, rows, :]
            locals_.append(pltpu.make_async_copy(ins[b].at[2 * x + y, rows, :], mine, lsem.at[b, q]))
            for j, chip in enumerate(chips):
                sends.append(_remote(ins[b].at[2 * chip[0] + chip[1], rows, :], mine, ssem.at[b, j, q], rsem.at[b, j, q],
                                     (*chip, c)))
                slot = outs[b].at[2 * chip[0] + chip[1], rows, :]
                lands.append(_remote(slot, slot, ssem.at[b, j, q], rsem.at[b, j, q], me))
    return locals_, sends, lands


def _scatter_start(ins, outs, ssem, rsem, lsem):
    locals_, sends, _ = _scatter_copies(ins, outs, ssem, rsem, lsem)
    for cp in locals_ + sends:
        cp.start()


def _scatter_finish(ins, outs, ssem, rsem, lsem):
    locals_, sends, lands = _scatter_copies(ins, outs, ssem, rsem, lsem)
    for cp in lands:
        cp.wait_recv()
    for cp in sends:
        cp.wait_send()
    for cp in locals_:
        cp.wait()


def _scatter_sems(nb):
    return [pltpu.SemaphoreType.DMA((nb, 3, NQ_ICI)), pltpu.SemaphoreType.DMA((nb, 3, NQ_ICI)),
            pltpu.SemaphoreType.DMA((nb, NQ_ICI))]


def _remote(src, dst, ssem, rsem, dev):
    return pltpu.make_async_remote_copy(src_ref=src, dst_ref=dst, send_sem=ssem, recv_sem=rsem, device_id=dev,
                                        device_id_type=MESH)


def _all_gather_weights(shards):
    nb = len(shards)

    def body(*refs):
        ins, outs, sems = refs[:nb], refs[nb:2 * nb], refs[2 * nb:]
        _gather_start(ins, outs, *sems)
        _gather_finish(ins, outs, *sems)

    return _pc(
        body, name="all_gather_weights", in_specs=[ANY] * nb, out_specs=[ANY] * nb,
        out_shape=[SDS((NCH,) + s.shape, s.dtype) for s in shards], scratch_shapes=_gather_sems(nb),
    )(*shards)


def _rows_to_move(ref):
    return 3 * FC + 256 if ref.shape[0] == 4 * FC else ref.shape[0]


def _gather_first(ins, outs, ssem, rsem, lsem):
    x, y, c, chips = _place()
    locals_, sends = [], []
    for b in range(len(ins)):
        half = _rows_to_move(ins[b]) // 2
        for q, (off, n) in enumerate(_chunks(half, NQ_ICI)):
            mine = pl.ds(c * half + off, n)
            own = outs[b].at[2 * x + y, mine, :]
            locals_.append(pltpu.make_async_copy(ins[b].at[mine, :], own, lsem.at[b, q]))
            sends.append(_remote(ins[b].at[mine, :], own, ssem.at[b, 0, q], rsem.at[b, 0, q], (x, y, 1 - c)))
            sends += [_remote(ins[b].at[mine, :], own, ssem.at[b, 1 + j, q], rsem.at[b, 1 + j, q], (*chip, c))
                      for j, chip in enumerate(chips)]
    return locals_, sends


def _gather_start(ins, outs, ssem, rsem, lsem):
    locals_, sends = _gather_first(ins, outs, ssem, rsem, lsem)
    for cp in locals_ + sends:
        cp.start()


def _gather_finish(ins, outs, ssem, rsem, lsem):
    x, y, c, chips = _place()
    me, sib = (x, y, c), (x, y, 1 - c)
    locals_, sends = _gather_first(ins, outs, ssem, rsem, lsem)
    for b in range(len(ins)):
        half = _rows_to_move(ins[b]) // 2
        for q, (off, n) in enumerate(_chunks(half, NQ_ICI)):
            mine = pl.ds(c * half + off, n)
            for j, chip in enumerate(chips):
                landed = outs[b].at[2 * chip[0] + chip[1], mine, :]
                _remote(landed, landed, ssem.at[b, 1 + j, q], rsem.at[b, 1 + j, q], me).wait_recv()
                cp = _remote(landed, landed, ssem.at[b, 4 + j, q], rsem.at[b, 4 + j, q], sib)
                cp.start()
                sends.append(cp)
    for b in range(len(ins)):
        half = _rows_to_move(ins[b]) // 2
        for q, (off, n) in enumerate(_chunks(half, NQ_ICI)):
            other = pl.ds((1 - c) * half + off, n)
            theirs = outs[b].at[2 * x + y, other, :]
            _remote(theirs, theirs, ssem.at[b, 0, q], rsem.at[b, 0, q], me).wait_recv()
            for j, chip in enumerate(chips):
                fwd = outs[b].at[2 * chip[0] + chip[1], other, :]
                _remote(fwd, fwd, ssem.at[b, 4 + j, q], rsem.at[b, 4 + j, q], me).wait_recv()
    for cp in sends:
        cp.wait_send()
    for cp in locals_:
        cp.wait()


def _gather_sems(nb):
    return [pltpu.SemaphoreType.DMA((nb, 7, NQ_ICI)), pltpu.SemaphoreType.DMA((nb, 7, NQ_ICI)),
            pltpu.SemaphoreType.DMA((nb, NQ_ICI))]


def _send_sibling_half(parts):
    nb = len(parts)

    def body(*refs):
        ins, gots = refs[:nb], refs[nb:2 * nb]
        ssem, rsem = refs[2 * nb:]
        x, y, c, _ = _place()
        sib = (x, y, 1 - c)
        todo = []
        for b in range(nb):
            half = ins[b].shape[1] // 2
            for q, (off, n) in enumerate(_chunks(half, NQ_D2D)):
                cp = _remote(ins[b].at[:, pl.ds((1 - c) * half + off, n), :], gots[b].at[:, pl.ds(off, n), :],
                             ssem.at[b, q], rsem.at[b, q], sib)
                cp.start()
                todo.append(cp)
        for cp in todo:
            cp.wait()

    return _pc(
        body, name="send_sibling_half", in_specs=[ANY] * nb, out_specs=[ANY] * nb,
        out_shape=[SDS((p.shape[0], p.shape[1] // 2, p.shape[2]), p.dtype) for p in parts],
        scratch_shapes=[pltpu.SemaphoreType.DMA((nb, NQ_D2D)), pltpu.SemaphoreType.DMA((nb, NQ_D2D))],
    )(*parts)


def _scatter_to_chips(parts):
    nb = len(parts)

    def body(*refs):
        ins, outs, sems = refs[:nb], refs[nb:2 * nb], refs[2 * nb:]
        _scatter_start(ins, outs, *sems)
        _scatter_finish(ins, outs, *sems)

    return _pc(
        body, name="scatter_to_chips", in_specs=[ANY] * nb, out_specs=[ANY] * nb,
        out_shape=[SDS(p.shape, p.dtype) for p in parts], scratch_shapes=_scatter_sems(nb),
    )(*parts)


def _scatter_begin(sums):
    hbm = pl.BlockSpec(memory_space=pltpu.HBM)
    sem = pl.BlockSpec(memory_space=pltpu.SEMAPHORE)

    def body(src_ref, land_ref, send_sem, recv_sem, src_thru, land_thru, token):
        x, y, c, chips = _place()
        for chip in chips:
            _remote(src_ref.at[2 * chip[0] + chip[1]], land_ref.at[2 * x + y], send_sem, recv_sem, (*chip, c)).start()
        token[...] = jnp.zeros_like(token)

    pin = lambda a: pltpu.with_memory_space_constraint(a, pltpu.HBM)
    return pl.pallas_call(
        body, name="scatter_begin",
        out_shape=(pltpu.SemaphoreType.DMA(()), pltpu.SemaphoreType.DMA(()), pltpu.HBM(sums.shape, sums.dtype),
                   pltpu.HBM(sums.shape, sums.dtype), SDS((8, 128), f32)),
        in_specs=(hbm, hbm), out_specs=(sem, sem, hbm, hbm, pl.BlockSpec(memory_space=pltpu.VMEM)),
        input_output_aliases={0: 2, 1: 3},
        compiler_params=pltpu.CompilerParams(has_side_effects=pltpu.SideEffectType.DATAFLOW_SIDE_EFFECTING),
    )(pin(sums), pin(sums + jnp.zeros_like(sums)))


def _scatter_end(send_sem, recv_sem, src_thru, land_thru, after):
    hbm = pl.BlockSpec(memory_space=pltpu.HBM)
    sem = pl.BlockSpec(memory_space=pltpu.SEMAPHORE)

    def body(src_ref, land_ref, send_sem, recv_sem, after_ref, src_dead, got_ref):
        x, y, c, _ = _place()
        three = pl.ds(0, 3)
        cp = _remote(src_ref.at[three], land_ref.at[three], send_sem, recv_sem, (x, y, c))
        cp.wait_send()
        cp.wait_recv()

    return pl.pallas_call(
        body, name="scatter_end",
        out_shape=(pltpu.HBM(src_thru.shape, src_thru.dtype), pltpu.HBM(land_thru.shape, land_thru.dtype)),
        in_specs=(hbm, hbm, sem, sem, pl.BlockSpec(memory_space=pl.ANY)), out_specs=(hbm, hbm),
        input_output_aliases={0: 0, 1: 1},
        compiler_params=pltpu.CompilerParams(has_side_effects=pltpu.SideEffectType.DATAFLOW_SIDE_EFFECTING),
    )(src_thru, land_thru, send_sem, recv_sem, after)[1]


def _join_halves(fulls, ranges):
    nb = len(fulls)
    nr = max(len(r) for r in ranges)

    def body(*refs):
        ins, outs = refs[:nb], refs[nb:2 * nb]
        ssem, rsem = refs[2 * nb:]
        x, y, c, _ = _place()
        sib = (x, y, 1 - c)
        sends, lands = [], []
        for b in range(nb):
            for g, (row0, rows) in enumerate(ranges[b]):
                half = rows // 2
                for q, (off, n) in enumerate(_chunks(half, NQ_D2D)):
                    mine = pl.ds(row0 + c * half + off, n)
                    sends.append(_remote(ins[b].at[mine, :], outs[b].at[mine, :], ssem.at[b, g, q], rsem.at[b, g, q], sib))
                    other = outs[b].at[pl.ds(row0 + (1 - c) * half + off, n), :]
                    lands.append(_remote(other, other, ssem.at[b, g, q], rsem.at[b, g, q], sib))
        for cp in sends:
            cp.start()
        for cp in lands:
            cp.wait_recv()
        for cp in sends:
            cp.wait_send()

    return _pc(
        body, name="join_halves", in_specs=[ANY] * nb, out_specs=[ANY] * nb,
        out_shape=[SDS(h.shape, h.dtype) for h in fulls], input_output_aliases={b: b for b in range(nb)},
        scratch_shapes=[pltpu.SemaphoreType.DMA((nb, nr, NQ_D2D)), pltpu.SemaphoreType.DMA((nb, nr, NQ_D2D))],
    )(*fulls)


def _gather_small(vec):
    def body(v_ref, o_ref, ssem, rsem, lsem):
        x, y, c, _ = _place()
        mine = o_ref.at[4 * x + 2 * y + c]
        local = pltpu.make_async_copy(v_ref, mine, lsem)
        local.start()
        sends = []
        for k in range(1, 8):
            peer = (x ^ (k >> 2), y ^ ((k >> 1) & 1), c ^ (k & 1))
            cp = _remote(v_ref, mine, ssem.at[k - 1], rsem.at[k - 1], peer)
            cp.start()
            sends.append(cp)
        for k in range(1, 8):
            px, py, pc = x ^ (k >> 2), y ^ ((k >> 1) & 1), c ^ (k & 1)
            slot = o_ref.at[4 * px + 2 * py + pc]
            _remote(slot, slot, ssem.at[k - 1], rsem.at[k - 1], (x, y, c)).wait_recv()
        for cp in sends:
            cp.wait_send()
        local.wait()

    return _pc(
        body, name="gather_small", in_specs=[ANY], out_specs=ANY, out_shape=SDS((8,) + vec.shape, vec.dtype),
        scratch_shapes=[pltpu.SemaphoreType.DMA((7,)), pltpu.SemaphoreType.DMA((7,)), pltpu.SemaphoreType.DMA],
    )(vec)


def _block_diag(pw):
    return jnp.concatenate([jnp.pad(pw[g], ((0, 0), (64 * g, 192 - 64 * g))) for g in range(4)], axis=0)


def _own_columns(full, chip):
    n = full.shape[-1] // NCH
    parts = full.reshape(full.shape[:-1] + (NCH, n))
    sel = (lax.broadcasted_iota(jnp.int32, (NCH, 1), 0) == chip)
    return jnp.sum(jnp.where(sel, parts, 0.0), axis=-2)


def _at_own_columns(shard, chip):
    n = shard.shape[-1]
    sel = (lax.broadcasted_iota(jnp.int32, (NCH * n,), 0) // n == chip)
    return jnp.where(sel, jnp.tile(shard, NCH), 0.0)


def _pad_rows(a, rows):
    return jnp.pad(a, ((0, rows - a.shape[0]),) + ((0, 0),) * (a.ndim - 1))


def _ffn_block(l, which):
    return 7 * l + 3 * which


def _wout_block(l):
    return 7 * l + 6


class _Weights:
    def __init__(self):
        self.ffn, self.wout, self.w_aug, self.rides = {}, {}, {}, {}

    @classmethod
    def from_blob(cls, blob, w_aug):
        self = cls()
        for l in range(DEPTH):
            self.ffn[(l, 0)], self.ffn[(l, 1)] = (blob, _ffn_block(l, 0)), (blob, _ffn_block(l, 1))
            self.wout[l], self.w_aug[l] = (blob, _wout_block(l)), w_aug[l]
        return self

    def set_w_in(self, l, gathered):
        self.w_aug[l] = jnp.pad(gathered.transpose(1, 0, 2).reshape(D, INW), ((0, 0), (0, INP - INW)))

    def ffn_fwd(self, l, which, x, nw):
        arr, k0 = self.ffn[(l, which)]
        if (l, which) not in self.rides:
            return _ffn_fwd(x, nw, arr, k0)
        shards, landed = self.rides[(l, which)]
        out, *gathered = _ffn_fwd(x, nw, arr, k0, shards)
        landed(gathered)
        return out


def _layer_fwd(l, x0, pos, freq, wts, ws):
    sv = {"x0": x0}
    x1 = ws.ffn_fwd(l, 0, x0, wts["ffn1_norm"][l:l + 1])
    att, att4, att16, pu, dq, dz, dba = _inproj_fwd(x1, wts["mix_norm"][l:l + 1], ws.w_aug[l], pos, freq)
    qkvs = [att, att4.reshape(S, 768), att16.reshape(S, 768)]
    (o1, l1), (o4, l4), (o16, l16) = [_attn_fwd(q, NB // d) for q, d in zip(qkvs, PATTERN_DIL)]
    ols = (o1, l1, o4.reshape(4, S // 4, ATT), l4.reshape(4, S // 4, ATT), o16.reshape(16, S // 16, ATT),
           l16.reshape(16, S // 16, ATT))
    ya = _merge_fwd(*ols)
    yb = _pool_fwd(pu, wts["pool_bd"][l], wts["pool_scale"][l:l + 1])
    c = _conv_fwd(dq, wts["conv_w"][l])
    u, w, qg, kd, intra, aux = _dn_local_fwd(c, dba, wts["dn_par"][l])
    o_dn, states = _dn_rec_fwd(u, w, qg, kd, intra, aux)
    yc = _dn_post_fwd(o_dn, dz, wts["dn_out_norm"][l:l + 1])
    x2 = _outproj_fwd(x1, ya, yb, yc, *ws.wout[l])
    x3 = ws.ffn_fwd(l, 1, x2, wts["ffn2_norm"][l:l + 1])
    sv.update(x1=x1, x2=x2, qkvs=qkvs, ols=ols, ya=ya, yb=yb, yc=yc, pu=pu, dq=dq, dz=dz, dba=dba, c=c,
              u=u, w=w, qg=qg, kd=kd, intra=intra, aux=aux, states=states, o_dn=o_dn)
    return x3, sv


def _wout_part(g):
    return jnp.pad(g.astype(bf16).reshape(NCH, 256, D), ((0, 0), (0, FC - 256), (0, 0)))


def _win_part(g):
    return g[:, :INW].astype(bf16).reshape(D, NCH, INC).transpose(1, 0, 2)


def _layer_bwd(l, dx3, sv, pos, freq, wts, ws, ride=None, prep=None):
    gr = {}
    g2, u2, d2, dh4, *pieces_before = _ffn_bwd(sv["x2"], wts["ffn2_norm"][l:l + 1], *ws.ffn[(l, 1)], dx3, ride)
    gr.update(ffn2_w_gate=g2, ffn2_w_up=u2, ffn2_w_down=d2)
    dx2, gr["ffn2_norm"], dya, dyb, dyc, gr["w_out"] = _outproj_bwd(sv["x2"], wts["ffn2_norm"][l:l + 1], dx3, dh4, sv["ya"],
                                                                     sv["yb"], sv["yc"], *ws.wout[l])
    do_dn, ddz, gr["dn_out_norm"] = _dn_post_bwd(sv["o_dn"], sv["dz"], wts["dn_out_norm"][l:l + 1], dyc)
    du, dw, dqg, dkd, dintra, daux = _dn_rec_bwd(sv["u"], sv["w"], sv["qg"], sv["kd"], sv["intra"], sv["aux"], sv["states"], do_dn)
    dc, ddba, gr["dn_par"] = _dn_local_bwd(sv["c"], sv["dba"], wts["dn_par"][l], du, dw, dqg, dkd, dintra, daux)
    ddq, gr["conv_w"] = _conv_bwd(sv["dq"], wts["conv_w"][l], dc)
    dpu, gr["pool_bd"], gr["pool_scale"] = _pool_bwd(sv["pu"], wts["pool_bd"][l], wts["pool_scale"][l:l + 1], dyb)
    dols = _merge_bwd(*sv["ols"], dya)
    flat = lambda a: a.reshape(S, ATT)
    datts = [_attn_bwd(q, flat(sv["ols"][2 * p]), flat(sv["ols"][2 * p + 1]), flat(dols[2 * p]), flat(dols[2 * p + 1]), NB // d)
             for p, (q, d) in enumerate(zip(sv["qkvs"], PATTERN_DIL))]
    dx1, gr["mix_norm"], gr["w_aug"] = _inproj_bwd(sv["x1"], wts["mix_norm"][l:l + 1], ws.w_aug[l], pos, freq, dx2,
                                                    datts[0], datts[1].reshape(4, S // 4, 768),
                                                    datts[2].reshape(16, S // 16, 768), dpu, ddq, ddz, ddba)
    own = None
    if prep is not None:
        own = prep([jnp.concatenate([g2, u2, d2, _wout_part(gr["w_out"])], axis=1), _win_part(gr["w_aug"])])
    g1, u1, d1, dh4, *pieces_own = _ffn_bwd(sv["x0"], wts["ffn1_norm"][l:l + 1], *ws.ffn[(l, 0)], dx1, own)
    begun, nw1 = None, wts["ffn1_norm"][l:l + 1]
    if prep is not None:
        begun = _scatter_begin(prep([jnp.concatenate([g1, u1, d1], axis=1)])[0])
        nw1 = nw1 + begun[4][0, 0]
    dx0, gr["ffn1_norm"] = _norm_bwd(sv["x0"], nw1, dx1, dh4)
    gr.update(ffn1_w_gate=g1, ffn1_w_up=u1, ffn1_w_down=d1)
    return dx0, gr, pieces_before, pieces_own, begun


def _device_step(x, pos, target, wts, ws, prep=None):
    freq = jnp.tile(ROPE_THETA ** (-jnp.arange(0, EH, 2, dtype=f32) / EH), 2 * ATT // EH).reshape(1, ATT)
    saved = []
    h = x
    for l in range(DEPTH):
        h, sv = _layer_fwd(l, h, pos, freq, wts, ws)
        saved.append(sv)
    dh, g_final, loss = _final(h, wts["final_norm"], target)
    grads = [None] * DEPTH
    dh, grads[1], *_ = _layer_bwd(1, dh, saved[1], pos, freq, wts, ws)
    sums1 = None
    if prep is not None:
        g = grads[1]
        ffn = [g[f"ffn{f}_w_{n}"] for f in (1, 2) for n in ("gate", "up", "down")]
        sums1 = prep([jnp.concatenate(ffn + [_wout_part(g["w_out"])], axis=1), _win_part(g["w_aug"])])
    dh, grads[0], pieces1, pieces0, begun = _layer_bwd(0, dh, saved[0], pos, freq, wts, ws, sums1, prep)
    return loss, dh, g_final, grads, pieces1, pieces0, begun


_SMALL = (("ffn1_norm", (DEPTH, D)), ("mix_norm", (DEPTH, D)), ("pool_w", (DEPTH, 4, 64, 64)), ("pool_scale", (DEPTH, 256)),
          ("dn_conv_w", (DEPTH, 4, CW)), ("dn_a_log", (DEPTH, 4)), ("dn_dt_bias", (DEPTH, 4)), ("dn_out_norm", (DEPTH, 128)),
          ("ffn2_norm", (DEPTH, D)), ("final_norm", (D,)), ("loss", (1,)))


def _pack_small(vals):
    rows = []
    for name, shape in _SMALL:
        flat = vals[name].astype(f32).reshape(-1)
        rows.append(jnp.pad(flat, (0, _small_rows(shape) * 128 - flat.shape[0])).reshape(-1, 128))
    out = jnp.concatenate(rows, axis=0)
    return _pad_rows(out, -(-out.shape[0] // 16) * 16)


def _small_rows(shape):
    return -(-int(np.prod(shape)) // 1024) * 8


def _unpack_small(packed):
    vals, r = {}, 0
    for name, shape in _SMALL:
        size, n = int(np.prod(shape)), _small_rows(shape)
        vals[name] = packed[r:r + n].reshape(-1)[:size].reshape(shape)
        r += n
    return vals


def kernel(x, positions, ffn1_norm, ffn1_w_gate, ffn1_w_up, ffn1_w_down, mix_norm, w_in, pool_w, pool_scale, dn_conv_w, dn_a_log, dn_dt_bias, dn_out_norm, w_out, ffn2_norm, ffn2_w_gate, ffn2_w_up, ffn2_w_down, final_norm, loss_target, m_ffn1_norm, m_ffn1_w_gate, m_ffn1_w_up, m_ffn1_w_down, m_mix_norm, m_w_in, m_pool_w, m_pool_scale, m_dn_conv_w, m_dn_a_log, m_dn_dt_bias, m_dn_out_norm, m_w_out, m_ffn2_norm, m_ffn2_w_gate, m_ffn2_w_up, m_ffn2_w_down, m_final_norm, v_ffn1_norm, v_ffn1_w_gate, v_ffn1_w_up, v_ffn1_w_down, v_mix_norm, v_w_in, v_pool_w, v_pool_scale, v_dn_conv_w, v_dn_a_log, v_dn_dt_bias, v_dn_out_norm, v_w_out, v_ffn2_norm, v_ffn2_w_gate, v_ffn2_w_up, v_ffn2_w_down, v_final_norm):
    names = ["ffn1_norm", "ffn1_w_gate", "ffn1_w_up", "ffn1_w_down", "mix_norm", "w_in", "pool_w", "pool_scale", "dn_conv_w",
             "dn_a_log", "dn_dt_bias", "dn_out_norm", "w_out", "ffn2_norm", "ffn2_w_gate", "ffn2_w_up", "ffn2_w_down", "final_norm"]
    W = dict(zip(names, [ffn1_norm, ffn1_w_gate, ffn1_w_up, ffn1_w_down, mix_norm, w_in, pool_w, pool_scale, dn_conv_w,
                         dn_a_log, dn_dt_bias, dn_out_norm, w_out, ffn2_norm, ffn2_w_gate, ffn2_w_up, ffn2_w_down, final_norm]))
    M = dict(zip(names, [m_ffn1_norm, m_ffn1_w_gate, m_ffn1_w_up, m_ffn1_w_down, m_mix_norm, m_w_in, m_pool_w, m_pool_scale,
                         m_dn_conv_w, m_dn_a_log, m_dn_dt_bias, m_dn_out_norm, m_w_out, m_ffn2_norm, m_ffn2_w_gate, m_ffn2_w_up,
                         m_ffn2_w_down, m_final_norm]))
    V = dict(zip(names, [v_ffn1_norm, v_ffn1_w_gate, v_ffn1_w_up, v_ffn1_w_down, v_mix_norm, v_w_in, v_pool_w, v_pool_scale,
                         v_dn_conv_w, v_dn_a_log, v_dn_dt_bias, v_dn_out_norm, v_w_out, v_ffn2_norm, v_ffn2_w_gate, v_ffn2_w_up,
                         v_ffn2_w_down, v_final_norm]))
    chip = 2 * lax.axis_index("x") + lax.axis_index("y")

    ffn_names = [(f"ffn{f}_w_gate", f"ffn{f}_w_up", f"ffn{f}_w_down") for f in (1, 2)]
    tr = lambda t: jnp.swapaxes(t, -1, -2)
    def ffn_rows(l, which):
        g, u, dn = ffn_names[which]
        return [tr(W[g][l]), tr(W[u][l]), W[dn][l]]

    def second_half(l):
        return jnp.concatenate(ffn_rows(l, 1) + [jnp.pad(W["w_out"][l], ((0, FC - 256), (0, 0)))], axis=0).astype(bf16)

    ws = _Weights()
    first0, = _all_gather_weights([jnp.concatenate(ffn_rows(0, 0), axis=0).astype(bf16)])
    ws.ffn[(0, 0)] = (first0, 0)

    def landed_00(gathered):
        ws.ffn[(0, 1)], ws.wout[0] = (gathered[0], 0), (gathered[0], 3)
        ws.set_w_in(0, gathered[1])

    def landed_01(gathered):
        ws.ffn[(1, 0)] = (gathered[0], 0)
        ws.set_w_in(1, gathered[1])

    def landed_10(gathered):
        ws.ffn[(1, 1)], ws.wout[1] = (gathered[0], 0), (gathered[0], 3)

    ws.rides[(0, 0)] = ([second_half(0), W["w_in"][0].astype(bf16)], landed_00)
    ws.rides[(0, 1)] = ([jnp.concatenate(ffn_rows(1, 0), axis=0).astype(bf16), W["w_in"][1].astype(bf16)], landed_01)
    ws.rides[(1, 0)] = ([second_half(1)], landed_10)
    conv_all = _gather_small(_pad_rows(dn_conv_w.reshape(DEPTH * 4 * (CW // NCH) // 128, 128), 32))
    conv_full = jnp.concatenate([conv_all[2 * j, :DEPTH * 4 * (CW // NCH) // 128].reshape(DEPTH, 4, CW // NCH) for j in range(NCH)],
                                axis=-1)

    par = jnp.pad(jnp.stack([dn_a_log, dn_dt_bias], axis=1), ((0, 0), (0, 6), (4, 120)))
    wts = dict(ffn1_norm=ffn1_norm, mix_norm=mix_norm, ffn2_norm=ffn2_norm, final_norm=final_norm.reshape(1, D),
               pool_bd=jnp.stack([_block_diag(pool_w[l]) for l in range(DEPTH)]).astype(bf16),
               pool_scale=pool_scale, conv_w=jnp.pad(conv_full, ((0, 0), (0, 4), (0, 0))),
               dn_par=par, dn_out_norm=dn_out_norm)

    c_arr = lax.axis_index("c").astype(jnp.int32).reshape(1)

    def prep(parts):
        return [_sum_core_pair(p, g, c_arr) for p, g in zip(parts, _send_sibling_half(parts))]

    loss, dx, g_final, grads, pieces1, pieces0, begun = _device_step(x[0], positions.reshape(S, 1), loss_target[0], wts, ws,
                                                                      prep)

    small = {"loss": loss[0, 0:1], "final_norm": g_final.reshape(D)}
    for n in ("ffn1_norm", "mix_norm", "ffn2_norm", "pool_scale", "dn_out_norm"):
        small[n] = jnp.stack([grads[l][n].reshape(-1) for l in range(DEPTH)])
    small["pool_w"] = jnp.stack([jnp.stack([grads[l]["pool_bd"][64 * g:64 * (g + 1), 64 * g:64 * (g + 1)] for g in range(4)])
                                 for l in range(DEPTH)])
    small["dn_conv_w"] = jnp.stack([grads[l]["conv_w"][0:4] for l in range(DEPTH)])
    small["dn_a_log"] = jnp.stack([grads[l]["dn_par"][0, 4:8] for l in range(DEPTH)])
    small["dn_dt_bias"] = jnp.stack([grads[l]["dn_par"][1, 4:8] for l in range(DEPTH)])
    packed = _pack_small(small)
    g_small = _sum_pieces(_gather_small(packed), f32, "sum_small")
    gs = _unpack_small(g_small)

    full_b = _sum_chips(pieces0[0], c_arr, None, 3 * FC, RB)
    full_b = _sum_chips(pieces1[0], c_arr, full_b, 7 * FC, RB)
    full_b = _sum_chips(_scatter_end(*begun[:4], g_small), c_arr, full_b, 0, RB)
    full_c = _sum_chips(pieces0[1], c_arr, None, 0, RC)
    full_c = _sum_chips(pieces1[1], c_arr, full_c, D, RC)
    full_b, full_c = _join_halves([full_b, full_c], [[(0, 3 * FC), (3 * FC, 4 * FC), (7 * FC, 7 * FC)], [(0, D), (D, D)]])

    transposed = ("ffn1_w_gate", "ffn1_w_up", "ffn2_w_gate", "ffn2_w_up")
    where = {"ffn1_w_gate": (full_b, FC // 2, lambda l: 14 * l), "ffn1_w_up": (full_b, FC // 2, lambda l: 14 * l + 2),
             "ffn1_w_down": (full_b, FC // 2, lambda l: 14 * l + 4), "ffn2_w_gate": (full_b, FC // 2, lambda l: 14 * l + 6),
             "ffn2_w_up": (full_b, FC // 2, lambda l: 14 * l + 8), "ffn2_w_down": (full_b, FC // 2, lambda l: 14 * l + 10),
             "w_out": (full_b, 64, lambda l: (FC // 64) * (7 * l + 6)), "w_in": (full_c, D // 2, lambda l: 2 * l)}
    big_res = {}
    for n, (gblob, tile, first) in where.items():
        t = tr if n in transposed else (lambda a: a)
        big_res[n] = [t(r) for r in _adamw_rows(t(W[n]), t(M[n]), t(V[n]), gblob, tile, first, "adamw_" + n)]

    def small_of(T):
        d = {n: T[n] for n, _ in _SMALL if n not in ("loss", "dn_conv_w")}
        d["loss"] = jnp.zeros((1,), f32)
        d["dn_conv_w"] = _at_own_columns(T["dn_conv_w"], chip)
        return _pack_small(d)

    res_s = _adamw(small_of(W), g_small, small_of(M), small_of(V), "adamw_small")
    small_out = [_unpack_small(r) for r in res_s]

    def split_blobs(b, c):
        out = {}
        b7 = b.reshape(DEPTH, 7, FC, D)
        for k, n in enumerate(n for names3 in ffn_names for n in names3):
            out[n] = tr(b7[:, k]) if n in transposed else b7[:, k]
        out["w_out"] = b7[:, 6, :256]
        out["w_in"] = c.reshape(DEPTH, D, INC)
        return out

    def assemble(big, sm):
        out = []
        for n in names:
            if n in big:
                out.append(big[n])
            elif n == "dn_conv_w":
                out.append(_own_columns(sm[n], chip))
            else:
                out.append(sm[n])
        return out

    grad_list = assemble(split_blobs(full_b, full_c), gs)
    outs = [gs["loss"].reshape(()), dx.reshape(1, S, D)] + grad_list
    for k in range(3):
        outs += assemble({n: r[k] for n, r in big_res.items()}, small_out[k])
    return tuple(outs)
```

```python
import functools
import math

import jax
import jax.numpy as jnp
import numpy as np
from jax import lax
from jax.experimental import pallas as pl
from jax.experimental.pallas import tpu as pltpu

f32 = jnp.float32
bf16 = jnp.bfloat16
SDS = jax.ShapeDtypeStruct
MESH = pl.DeviceIdType.MESH

S = 4096
D = 1024
DEPTH = 2
FF = 2816
NCH = 4
FC = FF // NCH
INW = 3080
INC = INW // NCH
INP = 3200
ATT = 256
EH = 64
NBLK = 128
DNW = 512
DH = 128
CH = 64
NCHUNK = S // CH
EPS = 1e-6
ROPE_THETA = 10000.0
PATTERN_DIL = (1, 4, 16)
ADAM_LR, ADAM_B1, ADAM_B2, ADAM_EPS, ADAM_WD, ADAM_STEP = 0.001, 0.9, 0.999, 1e-08, 0.01, 10
VMEM_BYTES_V7X = 64 * 1024 * 1024
NEG = -1e30

TM = 512
RB, RC = 14 * FC, 2 * D


def _cp(vmem_mb=48, sem=None):
    kw = dict(vmem_limit_bytes=vmem_mb * 1024 * 1024)
    if sem is not None:
        kw["dimension_semantics"] = sem
    return pltpu.CompilerParams(**kw)


def _pc(*args, **kwargs):
    pin = lambda s: pltpu.HBM(s.shape, s.dtype) if isinstance(s, SDS) and jnp.issubdtype(s.dtype, jnp.floating) else s
    out = kwargs["out_shape"]
    kwargs["out_shape"] = [pin(s) for s in out] if isinstance(out, (list, tuple)) else pin(out)
    call = pl.pallas_call(*args, **kwargs)

    def run(*operands):
        pinned = [pltpu.with_memory_space_constraint(o, pltpu.HBM) if jnp.issubdtype(o.dtype, jnp.floating) else o
                  for o in operands]
        return call(*pinned)

    return run


def _dot(a, b):
    return jnp.dot(a, b, preferred_element_type=f32)


def _dot_nt(a, b):
    return lax.dot_general(a, b, (((1,), (1,)), ((), ())), preferred_element_type=f32)


def _dot_tn(a, b):
    return lax.dot_general(a, b, (((0,), (0,)), ((), ())), preferred_element_type=f32)


def _rms(x, w):
    r = lax.rsqrt(jnp.mean(x * x, axis=-1, keepdims=True) + EPS)
    return x * r * w, r


def _rms_bwd(x, w, r, dh):
    xhat = x * r
    dw = jnp.sum(dh * xhat, axis=0, keepdims=True)
    dxh = dh * w
    dx = r * (dxh - xhat * jnp.mean(dxh * xhat, axis=-1, keepdims=True))
    return dx, dw


def _ffn_fwd(x, nw, blob, k0, ride=None):
    kg, ku, kd = k0, k0 + 1, k0 + 2
    nr = 0 if ride is None else len(ride)
    ni = S // TM

    def body(*refs):
        x_ref, nw_ref, wg_ref, wu_ref, wd_ref = refs[:5]
        ride_in = refs[5:5 + nr]
        o_ref = refs[5 + nr]
        ride_out = refs[6 + nr:6 + 2 * nr]
        h_scr, acc_scr = refs[6 + 2 * nr:8 + 2 * nr]
        sems = refs[8 + 2 * nr:]
        i = pl.program_id(0)
        j = pl.program_id(1)

        if nr:
            @pl.when(jnp.logical_and(i == 0, j == 0))
            def _():
                _gather_start(ride_in, ride_out, *sems)

        @pl.when(j == 0)
        def _():
            h, _ = _rms(x_ref[...], nw_ref[...])
            h_scr[...] = h.astype(bf16)
            acc_scr[...] = jnp.zeros_like(acc_scr)

        h = h_scr[...]
        g = _dot_nt(h, wg_ref[0])
        u = _dot_nt(h, wu_ref[0])
        a = (g * jax.nn.sigmoid(g) * u).astype(bf16)
        acc_scr[...] += _dot(a, wd_ref[0])

        @pl.when(j == NCH - 1)
        def _():
            o_ref[...] = x_ref[...] + 0.5 * acc_scr[...]

        if nr:
            @pl.when(jnp.logical_and(i == ni - 1, j == NCH - 1))
            def _():
                _gather_finish(ride_in, ride_out, *sems)

    wspec = lambda k: pl.BlockSpec((1, FC, D), lambda i, j: (j, k, 0))
    rides = [] if ride is None else list(ride)
    res = _pc(
        body, grid=(ni, NCH), name="ffn_fwd_ride" if nr else "ffn_fwd",
        in_specs=[pl.BlockSpec((TM, D), lambda i, j: (i, 0)),
                  pl.BlockSpec((1, D), lambda i, j: (0, 0)),
                  wspec(kg), wspec(ku), wspec(kd)] + [ANY] * nr,
        out_specs=[pl.BlockSpec((TM, D), lambda i, j: (i, 0))] + [ANY] * nr,
        out_shape=[SDS((S, D), f32)] + [SDS((NCH,) + r.shape, r.dtype) for r in rides],
        scratch_shapes=[pltpu.VMEM((TM, D), bf16), pltpu.VMEM((TM, D), f32)] + (_gather_sems(nr) if nr else []),
        compiler_params=_cp(40),
    )(x, nw, blob, blob, blob, *rides)
    return res if nr else res[0]


def _ffn_bwd(x, nw, blob, k0, dy, ride=None):
    nt = S // TM
    kg, ku, kd = k0, k0 + 1, k0 + 2
    nr = 0 if ride is None else len(ride)

    def body(*refs):
        x_ref, nw_ref, wg_ref, wu_ref, wd_ref, dy_ref = refs[:6]
        ride_in = refs[6:6 + nr]
        dwg_ref, dwu_ref, dwd_ref, dh_ref = refs[6 + nr:10 + nr]
        ride_out = refs[10 + nr:10 + 2 * nr]
        ag, au, ad = refs[10 + 2 * nr:13 + 2 * nr]
        sems = refs[13 + 2 * nr:]
        j = pl.program_id(0)
        i = pl.program_id(1)

        if nr:
            @pl.when(jnp.logical_and(j == 0, i == 0))
            def _():
                _scatter_start(ride_in, ride_out, *sems)

        @pl.when(i == 0)
        def _():
            ag[...] = jnp.zeros_like(ag)
            au[...] = jnp.zeros_like(au)
            ad[...] = jnp.zeros_like(ad)

        hf, _ = _rms(x_ref[...], nw_ref[...])
        h = hf.astype(bf16)
        g = _dot_nt(h, wg_ref[0])
        u = _dot_nt(h, wu_ref[0])
        sg = jax.nn.sigmoid(g)
        s = g * sg
        a = (s * u).astype(bf16)
        dyb = (0.5 * dy_ref[...]).astype(bf16)
        da = _dot_nt(dyb, wd_ref[0])
        ad[...] += _dot_tn(a, dyb)
        du = (da * s).astype(bf16)
        dg = (da * u * (sg * (1.0 + g * (1.0 - sg)))).astype(bf16)
        ag[...] += _dot_tn(dg, h)
        au[...] += _dot_tn(du, h)
        dh_ref[0] = (_dot(dg, wg_ref[0]) + _dot(du, wu_ref[0])).astype(bf16)

        @pl.when(i == nt - 1)
        def _():
            dwg_ref[0] = ag[...].astype(bf16)
            dwu_ref[0] = au[...].astype(bf16)
            dwd_ref[0] = ad[...].astype(bf16)

        if nr:
            @pl.when(jnp.logical_and(j == NCH - 1, i == nt - 1))
            def _():
                _scatter_finish(ride_in, ride_out, *sems)

    wspec = lambda k: pl.BlockSpec((1, FC, D), lambda j, i: (j, k, 0))
    gspec = pl.BlockSpec((1, FC, D), lambda j, i: (j, 0, 0))
    rides = [] if ride is None else list(ride)
    return _pc(
        body, grid=(NCH, nt), name="ffn_bwd_ride" if nr else "ffn_bwd",
        in_specs=[pl.BlockSpec((TM, D), lambda j, i: (i, 0)),
                  pl.BlockSpec((1, D), lambda j, i: (0, 0)),
                  wspec(kg), wspec(ku), wspec(kd),
                  pl.BlockSpec((TM, D), lambda j, i: (i, 0))] + [ANY] * nr,
        out_specs=[gspec, gspec, gspec, pl.BlockSpec((1, TM, D), lambda j, i: (j, i, 0))] + [ANY] * nr,
        out_shape=[SDS((NCH, FC, D), bf16)] * 3 + [SDS((NCH, S, D), bf16)] + [SDS(r.shape, r.dtype) for r in rides],
        scratch_shapes=[pltpu.VMEM((FC, D), f32)] * 3 + (_scatter_sems(nr) if nr else []),
        compiler_params=_cp(56),
    )(x, nw, blob, blob, blob, dy, *rides)


def _norm_bwd(x, nw, dres, dh4):
    nt = S // TM
    nparts = dh4.shape[0]

    def body(x_ref, nw_ref, dres_ref, dh_ref, dx_ref, dnw_ref):
        i = pl.program_id(0)
        dh = dh_ref[0].astype(f32)
        for p in range(1, nparts):
            dh = dh + dh_ref[p].astype(f32)
        xv = x_ref[...]
        _, r = _rms(xv, nw_ref[...])
        dx, dw = _rms_bwd(xv, nw_ref[...], r, dh)
        dx_ref[...] = dres_ref[...] + dx

        @pl.when(i == 0)
        def _():
            dnw_ref[...] = jnp.zeros_like(dnw_ref)

        dnw_ref[...] += dw

    return _pc(
        body, grid=(nt,), name="norm_bwd",
        in_specs=[pl.BlockSpec((TM, D), lambda i: (i, 0)),
                  pl.BlockSpec((1, D), lambda i: (0, 0)),
                  pl.BlockSpec((TM, D), lambda i: (i, 0)),
                  pl.BlockSpec((nparts, TM, D), lambda i: (0, i, 0))],
        out_specs=[pl.BlockSpec((TM, D), lambda i: (i, 0)), pl.BlockSpec((1, D), lambda i: (0, 0))],
        out_shape=[SDS((S, D), f32), SDS((1, D), f32)],
        compiler_params=_cp(40),
    )(x, nw, dres, dh4)


def _final(x, nw, target):
    nt = S // TM

    def body(x_ref, nw_ref, t_ref, dx_ref, dnw_ref, loss_ref):
        i = pl.program_id(0)
        xv = x_ref[...]
        y, r = _rms(xv, nw_ref[...])
        err = y - t_ref[...]
        part = 0.5 * jnp.sum(jnp.mean(err * err, axis=-1, keepdims=True), axis=0, keepdims=True)
        dx, dw = _rms_bwd(xv, nw_ref[...], r, err * (1.0 / D))
        dx_ref[...] = dx

        @pl.when(i == 0)
        def _():
            dnw_ref[...] = jnp.zeros_like(dnw_ref)
            loss_ref[...] = jnp.zeros_like(loss_ref)

        dnw_ref[...] += dw
        loss_ref[...] += jnp.broadcast_to(part, loss_ref.shape)

    return _pc(
        body, grid=(nt,), name="final_loss",
        in_specs=[pl.BlockSpec((TM, D), lambda i: (i, 0)),
                  pl.BlockSpec((1, D), lambda i: (0, 0)),
                  pl.BlockSpec((TM, D), lambda i: (i, 0))],
        out_specs=[pl.BlockSpec((TM, D), lambda i: (i, 0)), pl.BlockSpec((1, D), lambda i: (0, 0)),
                   pl.BlockSpec((1, 128), lambda i: (0, 0))],
        out_shape=[SDS((S, D), f32), SDS((1, D), f32), SDS((1, 128), f32)],
        compiler_params=_cp(40),
    )(x, nw, target)


def _rot_half(t):
    lane = lax.broadcasted_iota(jnp.int32, t.shape, 1)
    first = (lane % EH) < (EH // 2)
    return jnp.where(first, -pltpu.roll(t, ATT - EH // 2, 1), pltpu.roll(t, EH // 2, 1))


def _rope_tables(pos_ref, freq_ref):
    ang = pos_ref[...].astype(f32) * freq_ref[...]
    return jnp.cos(ang), jnp.sin(ang)


def _split_residues(val, scr, outs):
    rows, cols = val.shape
    for j in range(cols // 128):
        scr[j] = val[:, 128 * j:128 * (j + 1)]
    for ref, d in outs:
        for j in range(cols // 128):
            for r in range(d):
                ref.at[r][:, 128 * j:128 * (j + 1)] = scr.at[j][pl.ds(r, rows // d, stride=d), :]


def _join_residues(ref, d, scr):
    rows, cols = scr.shape[1], ref.shape[2]
    for j in range(cols // 128):
        for r in range(d):
            scr.at[j][pl.ds(r, rows // d, stride=d), :] = ref.at[r][:, 128 * j:128 * (j + 1)]
    return jnp.concatenate([scr[j] for j in range(cols // 128)], axis=1)


def _res_spec(d, tile, cols):
    return pl.BlockSpec((d, tile // d, cols), lambda i: (0, i, 0))


def _inproj_fwd(x, nw, w_aug, pos, freq):
    TI = 256

    def body(x_ref, nw_ref, w_hbm, pos_ref, freq_ref, att_ref, att4_ref, att16_ref, pu_ref, dq_ref, dz_ref, dba_ref,
             w_scr, r_scr):
        @pl.when(pl.program_id(0) == 0)
        def _():
            pltpu.sync_copy(w_hbm, w_scr)

        h, _ = _rms(x_ref[...], nw_ref[...])
        proj = _dot(h.astype(bf16), w_scr[...])
        cos, sin = _rope_tables(pos_ref, freq_ref)
        q = proj[:, 0:ATT]
        k = proj[:, ATT:2 * ATT]
        att = jnp.concatenate([q * cos + _rot_half(q) * sin, k * cos + _rot_half(k) * sin, proj[:, 2 * ATT:3 * ATT]], axis=1)
        att_ref[...] = att
        _split_residues(att, r_scr, [(att4_ref, 4), (att16_ref, 16)])
        pu_ref[...] = proj[:, 768:1024]
        dq_ref[...] = proj[:, 1024:2560]
        dz_ref[...] = proj[:, 2560:3072]
        dba_ref[...] = proj[:, 3072:3200]

    return _pc(
        body, grid=(S // TI,), name="inproj_fwd",
        in_specs=[pl.BlockSpec((TI, D), lambda i: (i, 0)),
                  pl.BlockSpec((1, D), lambda i: (0, 0)),
                  pl.BlockSpec(memory_space=pl.ANY),
                  pl.BlockSpec((TI, 1), lambda i: (i, 0)),
                  pl.BlockSpec((1, ATT), lambda i: (0, 0))],
        out_specs=[pl.BlockSpec((TI, 768), lambda i: (i, 0)), _res_spec(4, TI, 768), _res_spec(16, TI, 768),
                   pl.BlockSpec((TI, 256), lambda i: (i, 0)),
                   pl.BlockSpec((TI, 1536), lambda i: (i, 0)), pl.BlockSpec((TI, 512), lambda i: (i, 0)),
                   pl.BlockSpec((TI, 128), lambda i: (i, 0))],
        out_shape=[SDS((S, 768), f32), SDS((4, S // 4, 768), f32), SDS((16, S // 16, 768), f32), SDS((S, 256), f32),
                   SDS((S, 1536), f32), SDS((S, 512), f32), SDS((S, 128), f32)],
        scratch_shapes=[pltpu.VMEM((D, INP), bf16), pltpu.VMEM((6, TI, 128), f32)],
        compiler_params=_cp(48),
    )(x, nw, w_aug, pos, freq)


def _inproj_bwd(x, nw, w_aug, pos, freq, dres, datt, datt4, datt16, dpu, ddq, ddz, ddba):
    TI = 256
    nt = S // TI

    def body(x_ref, nw_ref, w_hbm, pos_ref, freq_ref, dres_ref, datt_ref, datt4_ref, datt16_ref, dpu_ref, ddq_ref, ddz_ref,
             ddba_ref, dx_ref, dnw_ref, dw_hbm, w_scr, acc, r_scr):
        i = pl.program_id(0)

        @pl.when(i == 0)
        def _():
            pltpu.sync_copy(w_hbm, w_scr)
            acc[...] = jnp.zeros_like(acc)
            dnw_ref[...] = jnp.zeros_like(dnw_ref)

        xv = x_ref[...]
        hf, r = _rms(xv, nw_ref[...])
        h = hf.astype(bf16)
        cos, sin = _rope_tables(pos_ref, freq_ref)
        datt = datt_ref[...] + _join_residues(datt4_ref, 4, r_scr)
        datt = datt + _join_residues(datt16_ref, 16, r_scr)
        dq = datt[:, 0:ATT]
        dk = datt[:, ATT:2 * ATT]
        dq = dq * cos - _rot_half(dq) * sin
        dk = dk * cos - _rot_half(dk) * sin
        dproj = jnp.concatenate([dq, dk, datt[:, 2 * ATT:3 * ATT], dpu_ref[...], ddq_ref[...], ddz_ref[...], ddba_ref[...]],
                                axis=1).astype(bf16)
        acc[...] += _dot_tn(h, dproj)
        dh = _dot_nt(dproj, w_scr[...])
        dx, dw = _rms_bwd(xv, nw_ref[...], r, dh)
        dx_ref[...] = dres_ref[...] + dx
        dnw_ref[...] += dw

        @pl.when(i == nt - 1)
        def _():
            pltpu.sync_copy(acc, dw_hbm)

    return _pc(
        body, grid=(nt,), name="inproj_bwd",
        in_specs=[pl.BlockSpec((TI, D), lambda i: (i, 0)),
                  pl.BlockSpec((1, D), lambda i: (0, 0)),
                  pl.BlockSpec(memory_space=pl.ANY),
                  pl.BlockSpec((TI, 1), lambda i: (i, 0)),
                  pl.BlockSpec((1, ATT), lambda i: (0, 0)),
                  pl.BlockSpec((TI, D), lambda i: (i, 0)),
                  pl.BlockSpec((TI, 768), lambda i: (i, 0)), _res_spec(4, TI, 768), _res_spec(16, TI, 768),
                  pl.BlockSpec((TI, 256), lambda i: (i, 0)),
                  pl.BlockSpec((TI, 1536), lambda i: (i, 0)),
                  pl.BlockSpec((TI, 512), lambda i: (i, 0)),
                  pl.BlockSpec((TI, 128), lambda i: (i, 0))],
        out_specs=[pl.BlockSpec((TI, D), lambda i: (i, 0)), pl.BlockSpec((1, D), lambda i: (0, 0)),
                   pl.BlockSpec(memory_space=pl.ANY)],
        out_shape=[SDS((S, D), f32), SDS((1, D), f32), SDS((D, INP), f32)],
        scratch_shapes=[pltpu.VMEM((D, INP), bf16), pltpu.VMEM((D, INP), f32), pltpu.VMEM((6, TI, 128), f32)],
        compiler_params=_cp(56),
    )(x, nw, w_aug, pos, freq, dres, datt, datt4, datt16, dpu, ddq, ddz, ddba)


def _outproj_fwd(x, ya, yb, yc, blob_b, kw):
    def body(x_ref, ya_ref, yb_ref, yc_ref, w_ref, o_ref):
        ycat = jnp.concatenate([ya_ref[...], yb_ref[...], yc_ref[...]], axis=1).astype(bf16)
        o_ref[...] = x_ref[...] + _dot(ycat, w_ref[:, 0:256, :].reshape(D, D))

    return _pc(
        body, grid=(S // TM,), name="outproj_fwd",
        in_specs=[pl.BlockSpec((TM, D), lambda i: (i, 0)),
                  pl.BlockSpec((TM, 256), lambda i: (i, 0)),
                  pl.BlockSpec((TM, 256), lambda i: (i, 0)),
                  pl.BlockSpec((TM, 512), lambda i: (i, 0)),
                  pl.BlockSpec((NCH, FC, D), lambda i: (0, kw, 0))],
        out_specs=pl.BlockSpec((TM, D), lambda i: (i, 0)),
        out_shape=SDS((S, D), f32),
        compiler_params=_cp(40),
    )(x, ya, yb, yc, blob_b)


def _outproj_bwd(x, nw, dres, dh4, ya, yb, yc, blob_b, kw):
    nt = S // TM
    nparts = dh4.shape[0]

    def body(x_ref, nw_ref, dres_ref, dh_ref, ya_ref, yb_ref, yc_ref, w_ref, dx_ref, dnw_ref, dya_ref, dyb_ref, dyc_ref, dw_ref):
        i = pl.program_id(0)

        @pl.when(i == 0)
        def _():
            dw_ref[...] = jnp.zeros_like(dw_ref)
            dnw_ref[...] = jnp.zeros_like(dnw_ref)

        dh = dh_ref[0].astype(f32)
        for p in range(1, nparts):
            dh = dh + dh_ref[p].astype(f32)
        xv = x_ref[...]
        _, r = _rms(xv, nw_ref[...])
        dxn, dnw = _rms_bwd(xv, nw_ref[...], r, dh)
        dx = dres_ref[...] + dxn
        dx_ref[...] = dx
        dnw_ref[...] += dnw
        dyv = dx.astype(bf16)
        ycat = jnp.concatenate([ya_ref[...], yb_ref[...], yc_ref[...]], axis=1).astype(bf16)
        dw_ref[...] += _dot_tn(ycat, dyv)
        dcat = _dot_nt(dyv, w_ref[:, 0:256, :].reshape(D, D))
        dya_ref[...] = dcat[:, 0:256]
        dyb_ref[...] = dcat[:, 256:512]
        dyc_ref[...] = dcat[:, 512:1024]

    return _pc(
        body, grid=(nt,), name="outproj_bwd",
        in_specs=[pl.BlockSpec((TM, D), lambda i: (i, 0)),
                  pl.BlockSpec((1, D), lambda i: (0, 0)),
                  pl.BlockSpec((TM, D), lambda i: (i, 0)),
                  pl.BlockSpec((nparts, TM, D), lambda i: (0, i, 0)),
                  pl.BlockSpec((TM, 256), lambda i: (i, 0)),
                  pl.BlockSpec((TM, 256), lambda i: (i, 0)),
                  pl.BlockSpec((TM, 512), lambda i: (i, 0)),
                  pl.BlockSpec((NCH, FC, D), lambda i: (0, kw, 0))],
        out_specs=[pl.BlockSpec((TM, D), lambda i: (i, 0)), pl.BlockSpec((1, D), lambda i: (0, 0)),
                   pl.BlockSpec((TM, 256), lambda i: (i, 0)), pl.BlockSpec((TM, 256), lambda i: (i, 0)),
                   pl.BlockSpec((TM, 512), lambda i: (i, 0)), pl.BlockSpec((D, D), lambda i: (0, 0))],
        out_shape=[SDS((S, D), f32), SDS((1, D), f32), SDS((S, 256), f32), SDS((S, 256), f32), SDS((S, 512), f32),
                   SDS((D, D), f32)],
        compiler_params=_cp(48),
    )(x, nw, dres, dh4, ya, yb, yc, blob_b)


QT = NBLK
NB = S // QT


def _attn_block(q, kp, kc, vp, vc, first):
    kk = jnp.concatenate([kp, kc], axis=0).astype(bf16)
    vv = jnp.concatenate([vp, vc], axis=0).astype(bf16)
    qi = lax.broadcasted_iota(jnp.int32, (4 * QT, NBLK + QT), 0) % QT
    ki = lax.broadcasted_iota(jnp.int32, (4 * QT, NBLK + QT), 1)
    dist = NBLK + qi - ki
    valid = (dist >= 0) & (dist <= NBLK) & (jnp.logical_not(first) | (ki >= NBLK))
    head = lax.broadcasted_iota(jnp.int32, (1, ATT), 1) // EH
    masks = [(head == h).astype(f32) for h in range(4)]
    qs = jnp.concatenate([q * (mh * (1.0 / math.sqrt(EH))) for mh in masks], axis=0).astype(bf16)
    s = _dot_nt(qs, kk)
    s = jnp.where(valid, s, NEG)
    m = lax.stop_gradient(jnp.max(s, axis=-1, keepdims=True))
    p = jnp.exp(s - m)
    den = jnp.sum(p, axis=-1, keepdims=True)
    po = _dot((p * (1.0 / den)).astype(bf16), vv)
    lse = m + jnp.log(den)
    o = jnp.zeros((QT, ATT), f32)
    l = jnp.zeros((QT, ATT), f32)
    for h, mh in enumerate(masks):
        o = o + po[QT * h:QT * (h + 1)] * mh
        l = l + lse[QT * h:QT * (h + 1)] * mh
    return o, l


def _attn_specs(tile):
    own = lambda col: pl.BlockSpec((QT, ATT), lambda s: (tile(s), col))
    prev = lambda col: pl.BlockSpec((NBLK, ATT), lambda s: (jnp.maximum((QT // NBLK) * tile(s) - 1, 0), col))
    return [own(0), prev(1), own(1), prev(2), own(2)]


def _attn_fwd(qkv, per_seq):
    def body(q_ref, kp_ref, kc_ref, vp_ref, vc_ref, o_ref, l_ref):
        first = pl.program_id(0) % per_seq == 0
        o, l = _attn_block(q_ref[...], kp_ref[...], kc_ref[...], vp_ref[...], vc_ref[...], first)
        o_ref[...] = o
        l_ref[...] = l

    blk = pl.BlockSpec((QT, ATT), lambda t: (t, 0))
    return _pc(
        body, grid=(NB,), name="attn_fwd", in_specs=_attn_specs(lambda t: t), out_specs=[blk, blk],
        out_shape=[SDS((S, ATT), f32), SDS((S, ATT), f32)], compiler_params=_cp(32),
    )(qkv, qkv, qkv, qkv, qkv)


def _attn_block_bwd(q, kp, kc, vp, vc, o, l, do, dl, first):
    kk = jnp.concatenate([kp, kc], axis=0).astype(bf16)
    vv = jnp.concatenate([vp, vc], axis=0).astype(bf16)
    qi = lax.broadcasted_iota(jnp.int32, (4 * QT, NBLK + QT), 0) % QT
    ki = lax.broadcasted_iota(jnp.int32, (4 * QT, NBLK + QT), 1)
    dist = NBLK + qi - ki
    valid = (dist >= 0) & (dist <= NBLK) & (jnp.logical_not(first) | (ki >= NBLK))
    head = lax.broadcasted_iota(jnp.int32, (1, ATT), 1) // EH
    masks = [(head == h).astype(f32) for h in range(4)]
    scale = 1.0 / math.sqrt(EH)
    stack = lambda f: jnp.concatenate([f(mh) for mh in masks], axis=0)
    qs = stack(lambda mh: q * (mh * scale)).astype(bf16)
    s = jnp.where(valid, _dot_nt(qs, kk), NEG)
    lse = stack(lambda mh: jnp.max(jnp.where(mh > 0.0, l, NEG), axis=1, keepdims=True))
    p = jnp.exp(s - lse)
    dos = stack(lambda mh: do * mh).astype(bf16)
    dvv = _dot_tn(p.astype(bf16), dos)
    dp = _dot_nt(dos, vv)
    delta = stack(lambda mh: jnp.sum(do * o * mh, axis=1, keepdims=True))
    dlse = stack(lambda mh: jnp.sum(dl * mh, axis=1, keepdims=True))
    ds = (p * (dp - delta + dlse)).astype(bf16)
    dqs = _dot(ds, kk)
    dq = jnp.zeros((QT, ATT), f32)
    for h, mh in enumerate(masks):
        dq = dq + dqs[QT * h:QT * (h + 1)] * (mh * scale)
    dkk = _dot_tn(ds, qs)
    return dq, dkk[:NBLK], dkk[NBLK:], dvv[:NBLK], dvv[NBLK:]


def _attn_bwd(qkv, o, l, do, dl, per_seq):
    def body(q_ref, kp_ref, kc_ref, vp_ref, vc_ref, ofw_ref, lfw_ref, do_ref, dl_ref, o_ref, k_carry, v_carry):
        step = pl.program_id(0)

        @pl.when(step == 0)
        def _():
            k_carry[...] = jnp.zeros_like(k_carry)
            v_carry[...] = jnp.zeros_like(v_carry)

        t = NB - 1 - step
        first = t % per_seq == 0
        last = t % per_seq == per_seq - 1
        dq, dkp, dkc, dvp, dvc = _attn_block_bwd(q_ref[...], kp_ref[...], kc_ref[...], vp_ref[...], vc_ref[...], ofw_ref[...],
                                                 lfw_ref[...], do_ref[...], dl_ref[...], first)
        o_ref[:, 0:ATT] = dq
        o_ref[:, ATT:2 * ATT] = dkc
        o_ref[:, 2 * ATT:3 * ATT] = dvc
        o_ref[QT - NBLK:QT, ATT:2 * ATT] += jnp.where(last, 0.0, k_carry[...])
        o_ref[QT - NBLK:QT, 2 * ATT:3 * ATT] += jnp.where(last, 0.0, v_carry[...])
        k_carry[...] = dkp
        v_carry[...] = dvp

    rev = lambda s: NB - 1 - s
    blk = pl.BlockSpec((QT, ATT), lambda s: (rev(s), 0))
    return _pc(
        body, grid=(NB,), name="attn_bwd", in_specs=_attn_specs(rev) + [blk, blk, blk, blk],
        out_specs=pl.BlockSpec((QT, 768), lambda s: (rev(s), 0)),
        out_shape=SDS((S, 768), f32), scratch_shapes=[pltpu.VMEM((NBLK, ATT), f32)] * 2, compiler_params=_cp(40),
    )(qkv, qkv, qkv, qkv, qkv, o, l, do, dl)


def _merge_weights(l0, l1, l2):
    m = jnp.maximum(jnp.maximum(l0, l1), l2)
    e0, e1, e2 = jnp.exp(l0 - m), jnp.exp(l1 - m), jnp.exp(l2 - m)
    tot = e0 + e1 + e2
    return e0 / tot, e1 / tot, e2 / tot


def _merge_specs():
    nat = pl.BlockSpec((TM, ATT), lambda i: (i, 0))
    return nat, _res_spec(4, TM, ATT), _res_spec(16, TM, ATT)


def _merge_fwd(o1, l1, o4, l4, o16, l16):
    def body(o1_ref, l1_ref, o4_ref, l4_ref, o16_ref, l16_ref, y_ref, scr):
        o4v, l4v = _join_residues(o4_ref, 4, scr), _join_residues(l4_ref, 4, scr)
        o16v, l16v = _join_residues(o16_ref, 16, scr), _join_residues(l16_ref, 16, scr)
        w0, w1, w2 = _merge_weights(l1_ref[...], l4v, l16v)
        y_ref[...] = w0 * o1_ref[...] + w1 * o4v + w2 * o16v

    nat, r4, r16 = _merge_specs()
    return _pc(body, grid=(S // TM,), name="merge_fwd", in_specs=[nat, nat, r4, r4, r16, r16],
                          out_specs=nat, out_shape=SDS((S, ATT), f32), scratch_shapes=[pltpu.VMEM((2, TM, 128), f32)],
                          compiler_params=_cp(32))(o1, l1, o4, l4, o16, l16)


def _merge_bwd(o1, l1, o4, l4, o16, l16, dy):
    def body(o1_ref, l1_ref, o4_ref, l4_ref, o16_ref, l16_ref, dy_ref, do1_ref, dl1_ref, do4_ref, dl4_ref, do16_ref, dl16_ref, scr):
        o4v, l4v = _join_residues(o4_ref, 4, scr), _join_residues(l4_ref, 4, scr)
        o16v, l16v = _join_residues(o16_ref, 16, scr), _join_residues(l16_ref, 16, scr)
        o1v = o1_ref[...]
        w0, w1, w2 = _merge_weights(l1_ref[...], l4v, l16v)
        y = w0 * o1v + w1 * o4v + w2 * o16v
        dyv = dy_ref[...]
        do1_ref[...] = w0 * dyv
        dl1_ref[...] = w0 * (o1v - y) * dyv
        _split_residues(w1 * dyv, scr, [(do4_ref, 4)])
        _split_residues(w1 * (o4v - y) * dyv, scr, [(dl4_ref, 4)])
        _split_residues(w2 * dyv, scr, [(do16_ref, 16)])
        _split_residues(w2 * (o16v - y) * dyv, scr, [(dl16_ref, 16)])

    nat, r4, r16 = _merge_specs()
    return _pc(body, grid=(S // TM,), name="merge_bwd", in_specs=[nat, nat, r4, r4, r16, r16, nat],
                          out_specs=[nat, nat, r4, r4, r16, r16],
                          out_shape=[SDS((S, ATT), f32)] * 2 + [SDS((4, S // 4, ATT), f32)] * 2 + [SDS((16, S // 16, ATT), f32)] * 2,
                          scratch_shapes=[pltpu.VMEM((2, TM, 128), f32)], compiler_params=_cp(32))(o1, l1, o4, l4, o16, l16, dy)


HALO = 16


def _pool_consts(i, rows):
    grp = lax.broadcasted_iota(jnp.int32, (rows, 256), 1) // 64
    t = i * TM + lax.broadcasted_iota(jnp.int32, (rows, 256), 0)
    win = jnp.where(grp == 0, 2, jnp.where(grp == 1, 4, jnp.where(grp == 2, 8, 16)))
    cnt = jnp.minimum(t + 1, win).astype(f32)
    return grp, cnt


def _pool_select(grp, s2, s4, s8, s16):
    return jnp.where(grp == 0, s2, jnp.where(grp == 1, s4, jnp.where(grp == 2, s8, s16)))


def _pooled(i, cur, halo):
    xx = jnp.concatenate([halo, cur], axis=0)
    s2 = xx + pltpu.roll(xx, 1, 0)
    s4 = s2 + pltpu.roll(s2, 2, 0)
    s8 = s4 + pltpu.roll(s4, 4, 0)
    s16 = s8 + pltpu.roll(s8, 8, 0)
    grp, cnt = _pool_consts(i, TM)
    tot = _pool_select(grp, s2[HALO:], s4[HALO:], s8[HALO:], s16[HALO:])
    return tot / cnt - cur


def _pool_fwd(u, wp, scale):
    def body(u_ref, halo_ref, wp_ref, sc_ref, y_ref):
        i = pl.program_id(0)
        halo = halo_ref[...] * (i > 0).astype(f32)
        pooled = _pooled(i, u_ref[...], halo)
        y_ref[...] = _dot(pooled.astype(bf16), wp_ref[...]) * sc_ref[...]

    return _pc(
        body, grid=(S // TM,), name="pool_fwd",
        in_specs=[pl.BlockSpec((TM, 256), lambda i: (i, 0)),
                  pl.BlockSpec((HALO, 256), lambda i: (jnp.maximum(i * (TM // HALO) - 1, 0), 0)),
                  pl.BlockSpec((256, 256), lambda i: (0, 0)),
                  pl.BlockSpec((1, 256), lambda i: (0, 0))],
        out_specs=pl.BlockSpec((TM, 256), lambda i: (i, 0)), out_shape=SDS((S, 256), f32), compiler_params=_cp(32),
    )(u, u, wp, scale)


def _pool_bwd(u, wp, scale, dy):
    nt = S // TM

    def body(u_ref, halo_ref, wp_ref, sc_ref, dy_ref, dyn_ref, du_ref, dwp_ref, dsc_ref):
        i = pl.program_id(0)

        @pl.when(i == 0)
        def _():
            dwp_ref[...] = jnp.zeros_like(dwp_ref)
            dsc_ref[...] = jnp.zeros_like(dsc_ref)

        halo = halo_ref[...] * (i > 0).astype(f32)
        pooled = _pooled(i, u_ref[...], halo).astype(bf16)
        dyv = dy_ref[...]
        dsc_ref[...] += jnp.sum(dyv * _dot(pooled, wp_ref[...]), axis=0, keepdims=True)
        dys = (dyv * sc_ref[...]).astype(bf16)
        dwp_ref[...] += _dot_tn(pooled, dys)
        dpool = _dot_nt(dys, wp_ref[...])
        grp, cnt = _pool_consts(i, TM)
        dyn = ((dyn_ref[...] * (i < nt - 1).astype(f32)) * sc_ref[...]).astype(bf16)
        _, cntn = _pool_consts(i + 1, HALO)
        zn = _dot_nt(dyn, wp_ref[...]) / cntn
        zz = jnp.concatenate([dpool / cnt, zn], axis=0)
        n = TM + HALO
        a2 = zz + pltpu.roll(zz, n - 1, 0)
        a4 = a2 + pltpu.roll(a2, n - 2, 0)
        a8 = a4 + pltpu.roll(a4, n - 4, 0)
        a16 = a8 + pltpu.roll(a8, n - 8, 0)
        du_ref[...] = _pool_select(grp, a2[:TM], a4[:TM], a8[:TM], a16[:TM]) - dpool

    return _pc(
        body, grid=(nt,), name="pool_bwd",
        in_specs=[pl.BlockSpec((TM, 256), lambda i: (i, 0)),
                  pl.BlockSpec((HALO, 256), lambda i: (jnp.maximum(i * (TM // HALO) - 1, 0), 0)),
                  pl.BlockSpec((256, 256), lambda i: (0, 0)),
                  pl.BlockSpec((1, 256), lambda i: (0, 0)),
                  pl.BlockSpec((TM, 256), lambda i: (i, 0)),
                  pl.BlockSpec((HALO, 256), lambda i: (jnp.minimum((i + 1) * (TM // HALO), S // HALO - 1), 0))],
        out_specs=[pl.BlockSpec((TM, 256), lambda i: (i, 0)), pl.BlockSpec((256, 256), lambda i: (0, 0)),
                   pl.BlockSpec((1, 256), lambda i: (0, 0))],
        out_shape=[SDS((S, 256), f32), SDS((256, 256), f32), SDS((1, 256), f32)], compiler_params=_cp(32),
    )(u, u, wp, scale, dy, dy)


CW = 3 * DNW
CHALO = 8
TC = 256


def _conv_fwd(u, w):
    def body(u_ref, halo_ref, w_ref, c_ref):
        i = pl.program_id(0)
        xx = jnp.concatenate([halo_ref[...] * (i > 0).astype(f32), u_ref[...]], axis=0)
        c = (w_ref[3:4, :] * xx + w_ref[2:3, :] * pltpu.roll(xx, 1, 0) + w_ref[1:2, :] * pltpu.roll(xx, 2, 0)
             + w_ref[0:1, :] * pltpu.roll(xx, 3, 0))
        c_ref[...] = c[CHALO:]

    return _pc(
        body, grid=(S // TC,), name="conv_fwd",
        in_specs=[pl.BlockSpec((TC, CW), lambda i: (i, 0)),
                  pl.BlockSpec((CHALO, CW), lambda i: (jnp.maximum(i * (TC // CHALO) - 1, 0), 0)),
                  pl.BlockSpec((8, CW), lambda i: (0, 0))],
        out_specs=pl.BlockSpec((TC, CW), lambda i: (i, 0)), out_shape=SDS((S, CW), f32), compiler_params=_cp(32),
    )(u, u, w)


def _conv_bwd(u, w, dc):
    nt = S // TC

    def body(u_ref, halo_ref, w_ref, dc_ref, dcn_ref, du_ref, dw_ref):
        i = pl.program_id(0)

        @pl.when(i == 0)
        def _():
            dw_ref[...] = jnp.zeros_like(dw_ref)

        dcv = dc_ref[...]
        zz = jnp.concatenate([dcv, dcn_ref[...] * (i < nt - 1).astype(f32)], axis=0)
        n = TC + CHALO
        du = (w_ref[3:4, :] * zz + w_ref[2:3, :] * pltpu.roll(zz, n - 1, 0) + w_ref[1:2, :] * pltpu.roll(zz, n - 2, 0)
              + w_ref[0:1, :] * pltpu.roll(zz, n - 3, 0))
        du_ref[...] = du[:TC]
        xx = jnp.concatenate([halo_ref[...] * (i > 0).astype(f32), u_ref[...]], axis=0)
        for j in range(4):
            shifted = xx if j == 3 else pltpu.roll(xx, 3 - j, 0)
            dw_ref[j:j + 1, :] += jnp.sum(dcv * shifted[CHALO:], axis=0, keepdims=True)

    return _pc(
        body, grid=(nt,), name="conv_bwd",
        in_specs=[pl.BlockSpec((TC, CW), lambda i: (i, 0)),
                  pl.BlockSpec((CHALO, CW), lambda i: (jnp.maximum(i * (TC // CHALO) - 1, 0), 0)),
                  pl.BlockSpec((8, CW), lambda i: (0, 0)),
                  pl.BlockSpec((TC, CW), lambda i: (i, 0)),
                  pl.BlockSpec((CHALO, CW), lambda i: (jnp.minimum((i + 1) * (TC // CHALO), S // CHALO - 1), 0))],
        out_specs=[pl.BlockSpec((TC, CW), lambda i: (i, 0)), pl.BlockSpec((8, CW), lambda i: (0, 0))],
        out_shape=[SDS((S, CW), f32), SDS((8, CW), f32)], compiler_params=_cp(32),
    )(u, u, w, dc, dc)


TL = 512
NCL = TL // CH


def _ein(spec, a, b):
    return jnp.einsum(spec, a.astype(bf16), b.astype(bf16), preferred_element_type=f32)


def _ein_ct(spec, x, y, ct_first):
    ct = x if ct_first else y
    hi = ct.astype(bf16)
    lo = ct - hi.astype(f32)
    if ct_first:
        return _ein(spec, hi, y) + _ein(spec, lo, y)
    return _ein(spec, x, hi) + _ein(spec, x, lo)


def _bf16_dot(spec, grad_a, grad_b):
    @jax.custom_vjp
    def dot(a, b):
        return _ein(spec, a, b)

    def fwd(a, b):
        return _ein(spec, a, b), (a, b)

    def bwd(res, ct):
        a, b = res
        return grad_a(a, b, ct), grad_b(a, b, ct)

    dot.defvjp(fwd, bwd)
    return dot


def _bdot(a, b):
    return _ein('nik,nkj->nij', a, b)


def _bdot_nt(a, b):
    return _ein('nik,njk->nij', a, b)


def _bdot_tn(a, b):
    return _ein('nki,nkj->nij', a, b)


_mm = _bf16_dot('ik,kj->ij', lambda a, b, ct: _ein_ct('ij,kj->ik', ct, b, True), lambda a, b, ct: _ein_ct('ik,ij->kj', a, ct, False))
_mm_tn = _bf16_dot('ki,kj->ij', lambda a, b, ct: _ein_ct('kj,ij->ki', b, ct, False),
                   lambda a, b, ct: _ein_ct('ki,ij->kj', a, ct, False))


@jax.custom_vjp
def _inv_unit_lower(a):
    ii = lax.broadcasted_iota(jnp.int32, (1, CH, CH), 1)
    jj = lax.broadcasted_iota(jnp.int32, (1, CH, CH), 2)
    t = (ii == jj).astype(f32) - a
    p = a
    for _ in range(5):
        p = _bdot(p, p)
        t = t + _bdot(t, p)
    return t


def _inv_unit_lower_fwd(a):
    t = _inv_unit_lower(a)
    return t, t


def _inv_unit_lower_bwd(t, dt):
    return (-_bdot_tn(t, _bdot_nt(dt, t)),)


_inv_unit_lower.defvjp(_inv_unit_lower_fwd, _inv_unit_lower_bwd)


def _dn_local(c, dba, a_row, b_row):
    act = c * jax.nn.sigmoid(c)
    lane = lax.broadcasted_iota(jnp.int32, (TL, 128), 1)
    beta_all = jax.nn.sigmoid(dba)
    xs = dba + b_row
    softplus = jnp.maximum(xs, 0.0) + jnp.log(1.0 + jnp.exp(-jnp.abs(xs)))
    g_all = -jnp.exp(a_row) * softplus
    ii = lax.broadcasted_iota(jnp.int32, (1, CH, CH), 1)
    jj = lax.broadcasted_iota(jnp.int32, (1, CH, CH), 2)
    lower = jj <= ii
    strict = jj < ii
    eye = (ii == jj).astype(f32)
    us, ws, qgs, kds, intras = [], [], [], [], []
    aux = jnp.zeros((TL, 128), f32)
    for h in range(4):
        q = act[:, DH * h:DH * (h + 1)]
        k = act[:, DNW + DH * h:DNW + DH * (h + 1)]
        v = act[:, 2 * DNW + DH * h:2 * DNW + DH * (h + 1)]
        q = q * lax.rsqrt(jnp.sum(q * q, axis=-1, keepdims=True) + EPS) * (DH ** -0.5)
        k = k * lax.rsqrt(jnp.sum(k * k, axis=-1, keepdims=True) + EPS)
        beta = jnp.sum(jnp.where(lane == h, beta_all, 0.0), axis=1, keepdims=True)
        g = jnp.sum(jnp.where(lane == 4 + h, g_all, 0.0), axis=1, keepdims=True)
        q3, k3, v3 = q.reshape(NCL, CH, DH), k.reshape(NCL, CH, DH), v.reshape(NCL, CH, DH)
        beta3, g3 = beta.reshape(NCL, CH, 1), g.reshape(NCL, CH, 1)
        g_row = jnp.sum(eye * g3, axis=1, keepdims=True)
        gc_col = jnp.sum(jnp.where(lower, g_row, 0.0), axis=2, keepdims=True)
        gc_row = jnp.sum(jnp.where(ii <= jj, g3, 0.0), axis=1, keepdims=True)
        diff = gc_col - gc_row
        decay = jnp.where(lower, jnp.exp(jnp.where(lower, diff, 0.0)), 0.0)
        kb = k3 * beta3
        vb = v3 * beta3
        a = jnp.where(strict, _bdot_nt(kb, k3) * decay, 0.0)
        t = _inv_unit_lower(a)
        u3 = _bdot(t, vb)
        w3 = _bdot(t, kb * jnp.exp(gc_col))
        intra = jnp.where(lower, _bdot_nt(q3, k3) * decay, 0.0)
        g_last = jnp.sum(g3, axis=1, keepdims=True)
        us.append(u3.reshape(TL, DH))
        ws.append(w3.reshape(TL, DH))
        qgs.append((q3 * jnp.exp(gc_col)).reshape(TL, DH))
        kds.append((k3 * jnp.exp(g_last - gc_col)).reshape(TL, DH))
        intras.append(intra.reshape(TL, CH))
        e_last = jnp.broadcast_to(jnp.exp(g_last), (NCL, CH, 1)).reshape(TL, 1)
        aux = aux + jnp.where(lane == h, e_last, 0.0)
    cat = lambda xs: jnp.concatenate(xs, axis=1)
    return cat(us), cat(ws), cat(qgs), cat(kds), jnp.stack(intras, axis=0), aux


def _dn_local_fwd(c, dba, par):
    def body(c_ref, dba_ref, par_ref, u_ref, w_ref, qg_ref, kd_ref, in_ref, aux_ref):
        u, w, qg, kd, intra, aux = _dn_local(c_ref[...], dba_ref[...], par_ref[0:1, :], par_ref[1:2, :])
        u_ref[...] = u
        w_ref[...] = w
        qg_ref[...] = qg
        kd_ref[...] = kd
        in_ref[...] = intra
        aux_ref[...] = aux

    wide = pl.BlockSpec((TL, DNW), lambda i: (i, 0))
    return _pc(
        body, grid=(S // TL,), name="dn_local_fwd",
        in_specs=[pl.BlockSpec((TL, CW), lambda i: (i, 0)), pl.BlockSpec((TL, 128), lambda i: (i, 0)),
                  pl.BlockSpec((8, 128), lambda i: (0, 0))],
        out_specs=[wide, wide, wide, wide, pl.BlockSpec((4, TL, CH), lambda i: (0, i, 0)),
                   pl.BlockSpec((TL, 128), lambda i: (i, 0))],
        out_shape=[SDS((S, DNW), f32)] * 4 + [SDS((4, S, CH), f32), SDS((S, 128), f32)], compiler_params=_cp(48),
    )(c, dba, par)


def _dn_local_bwd(c, dba, par, du, dw, dqg, dkd, dintra, daux):
    def body(c_ref, dba_ref, par_ref, du_ref, dw_ref, dqg_ref, dkd_ref, din_ref, daux_ref, dc_ref, ddba_ref, dpar_ref):
        @pl.when(pl.program_id(0) == 0)
        def _():
            dpar_ref[...] = jnp.zeros_like(dpar_ref)

        _, vjp = jax.vjp(_dn_local, c_ref[...], dba_ref[...], par_ref[0:1, :], par_ref[1:2, :])
        dc, ddba, da_row, db_row = vjp((du_ref[...], dw_ref[...], dqg_ref[...], dkd_ref[...], din_ref[...], daux_ref[...]))
        dc_ref[...] = dc
        ddba_ref[...] = ddba
        dpar_ref[0:1, :] += da_row
        dpar_ref[1:2, :] += db_row

    wide = pl.BlockSpec((TL, DNW), lambda i: (i, 0))
    return _pc(
        body, grid=(S // TL,), name="dn_local_bwd",
        in_specs=[pl.BlockSpec((TL, CW), lambda i: (i, 0)), pl.BlockSpec((TL, 128), lambda i: (i, 0)),
                  pl.BlockSpec((8, 128), lambda i: (0, 0)), wide, wide, wide, wide,
                  pl.BlockSpec((4, TL, CH), lambda i: (0, i, 0)), pl.BlockSpec((TL, 128), lambda i: (i, 0))],
        out_specs=[pl.BlockSpec((TL, CW), lambda i: (i, 0)), pl.BlockSpec((TL, 128), lambda i: (i, 0)),
                   pl.BlockSpec((8, 128), lambda i: (0, 0))],
        out_shape=[SDS((S, CW), f32), SDS((S, 128), f32), SDS((8, 128), f32)], compiler_params=_cp(56),
    )(c, dba, par, du, dw, dqg, dkd, dintra, daux)


def _dn_step(state, u, w, qg, kd, intra, aux):
    lane = lax.broadcasted_iota(jnp.int32, (CH, 128), 1)
    row = lax.broadcasted_iota(jnp.int32, (CH, 128), 0)
    outs, states = [], []
    for h in range(4):
        sl = slice(DH * h, DH * (h + 1))
        st = state[h]
        e = jnp.sum(jnp.sum(jnp.where((lane == h) & (row == 0), aux, 0.0), axis=1, keepdims=True), axis=0, keepdims=True)
        v_new = u[:, sl] - _mm(w[:, sl], st)
        outs.append(_mm(qg[:, sl], st) + _mm(intra[h], v_new))
        states.append(st * e + _mm_tn(kd[:, sl], v_new))
    return jnp.concatenate(outs, axis=1), jnp.stack(states, axis=0)


CPS = 8
NSTEP = NCHUNK // CPS


def _dn_rec_specs(index):
    wide = pl.BlockSpec((CPS * CH, DNW), lambda n: (index(n), 0))
    inb = pl.BlockSpec((4, CPS * CH, CH), lambda n: (0, index(n), 0))
    auxb = pl.BlockSpec((CPS * CH, 128), lambda n: (index(n), 0))
    stb = pl.BlockSpec((CPS, 4, DH, DH), lambda n: (index(n), 0, 0, 0))
    return wide, inb, auxb, stb


def _dn_rec_fwd(u, w, qg, kd, intra, aux):
    def body(u_ref, w_ref, qg_ref, kd_ref, in_ref, aux_ref, o_ref, st_ref, st_scr):
        @pl.when(pl.program_id(0) == 0)
        def _():
            st_scr[...] = jnp.zeros_like(st_scr)

        st = st_scr[...]
        for k in range(CPS):
            rows = slice(CH * k, CH * (k + 1))
            st_ref[k] = st
            o, st = _dn_step(st, u_ref[rows, :], w_ref[rows, :], qg_ref[rows, :], kd_ref[rows, :], in_ref[:, rows, :],
                             aux_ref[rows, :])
            o_ref[rows, :] = o
        st_scr[...] = st

    wide, inb, auxb, stb = _dn_rec_specs(lambda n: n)
    return _pc(
        body, grid=(NSTEP,), name="dn_rec_fwd", in_specs=[wide, wide, wide, wide, inb, auxb], out_specs=[wide, stb],
        out_shape=[SDS((S, DNW), f32), SDS((NCHUNK, 4, DH, DH), f32)],
        scratch_shapes=[pltpu.VMEM((4, DH, DH), f32)], compiler_params=_cp(32),
    )(u, w, qg, kd, intra, aux)


def _dn_rec_bwd(u, w, qg, kd, intra, aux, states, do):
    def body(u_ref, w_ref, qg_ref, kd_ref, in_ref, aux_ref, st_ref, do_ref,
             du_ref, dw_ref, dqg_ref, dkd_ref, din_ref, daux_ref, ds_scr):
        @pl.when(pl.program_id(0) == 0)
        def _():
            ds_scr[...] = jnp.zeros_like(ds_scr)

        ds = ds_scr[...]
        for k in reversed(range(CPS)):
            rows = slice(CH * k, CH * (k + 1))
            _, vjp = jax.vjp(_dn_step, st_ref[k], u_ref[rows, :], w_ref[rows, :], qg_ref[rows, :], kd_ref[rows, :],
                             in_ref[:, rows, :], aux_ref[rows, :])
            ds, du, dw, dqg, dkd, din, daux = vjp((do_ref[rows, :], ds))
            du_ref[rows, :] = du
            dw_ref[rows, :] = dw
            dqg_ref[rows, :] = dqg
            dkd_ref[rows, :] = dkd
            din_ref[:, rows, :] = din
            daux_ref[rows, :] = daux
        ds_scr[...] = ds

    wide, inb, auxb, stb = _dn_rec_specs(lambda n: NSTEP - 1 - n)
    return _pc(
        body, grid=(NSTEP,), name="dn_rec_bwd", in_specs=[wide, wide, wide, wide, inb, auxb, stb, wide],
        out_specs=[wide, wide, wide, wide, inb, auxb],
        out_shape=[SDS((S, DNW), f32)] * 4 + [SDS((4, S, CH), f32), SDS((S, 128), f32)],
        scratch_shapes=[pltpu.VMEM((4, DH, DH), f32)], compiler_params=_cp(40),
    )(u, w, qg, kd, intra, aux, states, do)


def _dn_post(o, z, nw):
    parts = []
    for h in range(4):
        sl = slice(DH * h, DH * (h + 1))
        oh = o[:, sl]
        y = oh * lax.rsqrt(jnp.mean(oh * oh, axis=-1, keepdims=True) + EPS) * nw
        zh = z[:, sl]
        parts.append(y * (zh * jax.nn.sigmoid(zh)))
    return jnp.concatenate(parts, axis=1)


def _dn_post_fwd(o, z, nw):
    def body(o_ref, z_ref, nw_ref, y_ref):
        y_ref[...] = _dn_post(o_ref[...], z_ref[...], nw_ref[...])

    wide = pl.BlockSpec((TM, DNW), lambda i: (i, 0))
    return _pc(body, grid=(S // TM,), name="dn_post_fwd",
                          in_specs=[wide, wide, pl.BlockSpec((1, 128), lambda i: (0, 0))], out_specs=wide,
                          out_shape=SDS((S, DNW), f32), compiler_params=_cp(32))(o, z, nw)


def _dn_post_bwd(o, z, nw, dy):
    def body(o_ref, z_ref, nw_ref, dy_ref, do_ref, dz_ref, dnw_ref):
        @pl.when(pl.program_id(0) == 0)
        def _():
            dnw_ref[...] = jnp.zeros_like(dnw_ref)

        _, vjp = jax.vjp(_dn_post, o_ref[...], z_ref[...], nw_ref[...])
        do, dz, dnw = vjp(dy_ref[...])
        do_ref[...] = do
        dz_ref[...] = dz
        dnw_ref[...] += dnw

    wide = pl.BlockSpec((TM, DNW), lambda i: (i, 0))
    one = pl.BlockSpec((1, 128), lambda i: (0, 0))
    return _pc(body, grid=(S // TM,), name="dn_post_bwd", in_specs=[wide, wide, one, wide],
                          out_specs=[wide, wide, one], out_shape=[SDS((S, DNW), f32), SDS((S, DNW), f32), SDS((1, 128), f32)],
                          compiler_params=_cp(32))(o, z, nw, dy)


def _row_tile(rows, width, itemsize=4, target=2 * 1024 * 1024):
    best = None
    for t in range(16, rows + 1, 16):
        if rows % t == 0 and t * width * itemsize <= target:
            best = t
    return best if best is not None else rows


def _sum_pieces(pieces, out_dtype, name):
    n, rows, width = pieces.shape
    tr = _row_tile(rows, width * n)

    def body(p_ref, o_ref):
        acc = p_ref[0].astype(f32)
        for s in range(1, n):
            acc = acc + p_ref[s].astype(f32)
        o_ref[...] = acc.astype(out_dtype)

    return _pc(body, grid=(rows // tr,), name=name,
                          in_specs=[pl.BlockSpec((n, tr, width), lambda i: (0, i, 0))],
                          out_specs=pl.BlockSpec((tr, width), lambda i: (i, 0)),
                          out_shape=SDS((rows, width), out_dtype), compiler_params=_cp(32))(pieces)


def _sum_core_pair(part, got, c_arr):
    n, rows, width = part.shape
    half = rows // 2
    tr = _row_tile(half, width, itemsize=2)
    nt = half // tr

    def body(c_ref, p_ref, g_ref, o_ref):
        o_ref[...] = (p_ref[...].astype(f32) + g_ref[...].astype(f32)).astype(bf16)

    gs = pltpu.PrefetchScalarGridSpec(
        num_scalar_prefetch=1, grid=(n, nt),
        in_specs=[pl.BlockSpec((1, tr, width), lambda j, i, c: (j, c[0] * nt + i, 0)),
                  pl.BlockSpec((1, tr, width), lambda j, i, c: (j, i, 0))],
        out_specs=pl.BlockSpec((1, tr, width), lambda j, i, c: (j, i, 0)))
    return _pc(body, grid_spec=gs, name="sum_core_pair", out_shape=SDS((n, half, width), bf16),
                          compiler_params=_cp(32))(c_arr, part, got)


def _sum_chips(pieces, c_arr, full, row0, total_rows):
    n, half, width = pieces.shape
    tr = max(t for t in range(16, 257, 16) if half % t == 0 and row0 % t == 0)
    nt = half // tr

    def body(c_ref, p_ref, *rest):
        o_ref = rest[-1]
        acc = p_ref[0].astype(f32)
        for s in range(1, n):
            acc = acc + p_ref[s].astype(f32)
        o_ref[...] = acc

    gs = pltpu.PrefetchScalarGridSpec(
        num_scalar_prefetch=1, grid=(nt,),
        in_specs=[pl.BlockSpec((n, tr, width), lambda i, c: (0, i, 0))] + ([] if full is None else [ANY]),
        out_specs=pl.BlockSpec((tr, width), lambda i, c: (row0 // tr + c[0] * nt + i, 0)))
    args = (c_arr, pieces) if full is None else (c_arr, pieces, full)
    return _pc(body, grid_spec=gs, name="sum_chips", out_shape=SDS((total_rows, width), f32),
                          input_output_aliases={} if full is None else {2: 0}, compiler_params=_cp(32))(*args)


def _adamw_math(w, g, m, v):
    mn = ADAM_B1 * m + (1.0 - ADAM_B1) * g
    vn = ADAM_B2 * v + (1.0 - ADAM_B2) * (g * g)
    m_hat = mn / (1.0 - ADAM_B1 ** ADAM_STEP)
    v_hat = vn / (1.0 - ADAM_B2 ** ADAM_STEP)
    return -ADAM_LR * (m_hat / (jnp.sqrt(v_hat) + ADAM_EPS) + ADAM_WD * w), mn, vn


def _adamw(w, g, m, v, name):
    rows, width = w.shape
    tr = _row_tile(rows, width * 7, target=12 * 1024 * 1024)

    def body(w_ref, g_ref, m_ref, v_ref, d_ref, nm_ref, nv_ref):
        d_ref[...], nm_ref[...], nv_ref[...] = _adamw_math(w_ref[...], g_ref[...], m_ref[...], v_ref[...])

    blk = pl.BlockSpec((tr, width), lambda i: (i, 0))
    return _pc(body, grid=(rows // tr,), name=name, in_specs=[blk] * 4, out_specs=[blk] * 3,
                          out_shape=[SDS((rows, width), f32)] * 3, compiler_params=_cp(40))(w, g, m, v)


def _adamw_rows(w, m, v, gblob, tr, first_tile, name):
    layers, rows, width = w.shape

    def body(w_ref, g_ref, m_ref, v_ref, d_ref, nm_ref, nv_ref):
        d_ref[0], nm_ref[0], nv_ref[0] = _adamw_math(w_ref[0], g_ref[...], m_ref[0], v_ref[0])

    blk = pl.BlockSpec((1, tr, width), lambda l, i: (l, i, 0))
    gblk = pl.BlockSpec((tr, width), lambda l, i: (first_tile(l) + i, 0))
    return _pc(body, grid=(layers, rows // tr), name=name, in_specs=[blk, gblk, blk, blk], out_specs=[blk] * 3,
                          out_shape=[SDS(w.shape, f32)] * 3, compiler_params=_cp(40))(w, gblob, m, v)


ANY = pl.BlockSpec(memory_space=pl.ANY)


def _place():
    x, y, c = lax.axis_index("x"), lax.axis_index("y"), lax.axis_index("c")
    chips = [(1 - x, y), (x, 1 - y), (1 - x, 1 - y)]
    return x, y, c, chips


NQ_ICI = 4
NQ_D2D = 8


def _chunks(rows, want):
    n = max(k for k in range(1, want + 1) if rows % k == 0 and (rows // k) % 16 == 0)
    step = rows // n
    return [(q * step, step) for q in range(n)]


def _scatter_copies(ins, outs, ssem, rsem, lsem):
    x, y, c, chips = _place()
    me = (x, y, c)
    locals_, sends, lands = [], [], []
    for b in range(len(ins)):
        for q, (off, n) in enumerate(_chunks(ins[b].shape[1], NQ_ICI)):
            rows = pl.ds(off, n)
            mine = outs[b].at[2 * x + y, rows, :]
            locals_.append(pltpu.make_async_copy(ins[b].at[2 * x + y, rows, :], mine, lsem.at[b, q]))
            for j, chip in enumerate(chips):
                sends.append(_remote(ins[b].at[2 * chip[0] + chip[1], rows, :], mine, ssem.at[b, j, q], rsem.at[b, j, q],
                                     (*chip, c)))
                slot = outs[b].at[2 * chip[0] + chip[1], rows, :]
                lands.append(_remote(slot, slot, ssem.at[b, j, q], rsem.at[b, j, q], me))
    return locals_, sends, lands


def _scatter_start(ins, outs, ssem, rsem, lsem):
    locals_, sends, _ = _scatter_copies(ins, outs, ssem, rsem, lsem)
    for cp in locals_ + sends:
        cp.start()


def _scatter_finish(ins, outs, ssem, rsem, lsem):
    locals_, sends, lands = _scatter_copies(ins, outs, ssem, rsem, lsem)
    for cp in lands:
        cp.wait_recv()
    for cp in sends:
        cp.wait_send()
    for cp in locals_:
        cp.wait()


def _scatter_sems(nb):
    return [pltpu.SemaphoreType.DMA((nb, 3, NQ_ICI)), pltpu.SemaphoreType.DMA((nb, 3, NQ_ICI)),
            pltpu.SemaphoreType.DMA((nb, NQ_ICI))]


def _remote(src, dst, ssem, rsem, dev):
    return pltpu.make_async_remote_copy(src_ref=src, dst_ref=dst, send_sem=ssem, recv_sem=rsem, device_id=dev,
                                        device_id_type=MESH)


def _all_gather_weights(shards):
    nb = len(shards)

    def body(*refs):
        ins, outs, sems = refs[:nb], refs[nb:2 * nb], refs[2 * nb:]
        _gather_start(ins, outs, *sems)
        _gather_finish(ins, outs, *sems)

    return _pc(
        body, name="all_gather_weights", in_specs=[ANY] * nb, out_specs=[ANY] * nb,
        out_shape=[SDS((NCH,) + s.shape, s.dtype) for s in shards], scratch_shapes=_gather_sems(nb),
    )(*shards)


def _rows_to_move(ref):
    return 3 * FC + 256 if ref.shape[0] == 4 * FC else ref.shape[0]


def _gather_first(ins, outs, ssem, rsem, lsem):
    x, y, c, chips = _place()
    locals_, sends = [], []
    for b in range(len(ins)):
        half = _rows_to_move(ins[b]) // 2
        for q, (off, n) in enumerate(_chunks(half, NQ_ICI)):
            mine = pl.ds(c * half + off, n)
            own = outs[b].at[2 * x + y, mine, :]
            locals_.append(pltpu.make_async_copy(ins[b].at[mine, :], own, lsem.at[b, q]))
            sends.append(_remote(ins[b].at[mine, :], own, ssem.at[b, 0, q], rsem.at[b, 0, q], (x, y, 1 - c)))
            sends += [_remote(ins[b].at[mine, :], own, ssem.at[b, 1 + j, q], rsem.at[b, 1 + j, q], (*chip, c))
                      for j, chip in enumerate(chips)]
    return locals_, sends


def _gather_start(ins, outs, ssem, rsem, lsem):
    locals_, sends = _gather_first(ins, outs, ssem, rsem, lsem)
    for cp in locals_ + sends:
        cp.start()


def _gather_finish(ins, outs, ssem, rsem, lsem):
    x, y, c, chips = _place()
    me, sib = (x, y, c), (x, y, 1 - c)
    locals_, sends = _gather_first(ins, outs, ssem, rsem, lsem)
    for b in range(len(ins)):
        half = _rows_to_move(ins[b]) // 2
        for q, (off, n) in enumerate(_chunks(half, NQ_ICI)):
            mine = pl.ds(c * half + off, n)
            for j, chip in enumerate(chips):
                landed = outs[b].at[2 * chip[0] + chip[1], mine, :]
                _remote(landed, landed, ssem.at[b, 1 + j, q], rsem.at[b, 1 + j, q], me).wait_recv()
                cp = _remote(landed, landed, ssem.at[b, 4 + j, q], rsem.at[b, 4 + j, q], sib)
                cp.start()
                sends.append(cp)
    for b in range(len(ins)):
        half = _rows_to_move(ins[b]) // 2
        for q, (off, n) in enumerate(_chunks(half, NQ_ICI)):
            other = pl.ds((1 - c) * half + off, n)
            theirs = outs[b].at[2 * x + y, other, :]
            _remote(theirs, theirs, ssem.at[b, 0, q], rsem.at[b, 0, q], me).wait_recv()
            for j, chip in enumerate(chips):
                fwd = outs[b].at[2 * chip[0] + chip[1], other, :]
                _remote(fwd, fwd, ssem.at[b, 4 + j, q], rsem.at[b, 4 + j, q], me).wait_recv()
    for cp in sends:
        cp.wait_send()
    for cp in locals_:
        cp.wait()


def _gather_sems(nb):
    return [pltpu.SemaphoreType.DMA((nb, 7, NQ_ICI)), pltpu.SemaphoreType.DMA((nb, 7, NQ_ICI)),
            pltpu.SemaphoreType.DMA((nb, NQ_ICI))]


def _send_sibling_half(parts):
    nb = len(parts)

    def body(*refs):
        ins, gots = refs[:nb], refs[nb:2 * nb]
        ssem, rsem = refs[2 * nb:]
        x, y, c, _ = _place()
        sib = (x, y, 1 - c)
        todo = []
        for b in range(nb):
            half = ins[b].shape[1] // 2
            for q, (off, n) in enumerate(_chunks(half, NQ_D2D)):
                cp = _remote(ins[b].at[:, pl.ds((1 - c) * half + off, n), :], gots[b].at[:, pl.ds(off, n), :],
                             ssem.at[b, q], rsem.at[b, q], sib)
                cp.start()
                todo.append(cp)
        for cp in todo:
            cp.wait()

    return _pc(
        body, name="send_sibling_half", in_specs=[ANY] * nb, out_specs=[ANY] * nb,
        out_shape=[SDS((p.shape[0], p.shape[1] // 2, p.shape[2]), p.dtype) for p in parts],
        scratch_shapes=[pltpu.SemaphoreType.DMA((nb, NQ_D2D)), pltpu.SemaphoreType.DMA((nb, NQ_D2D))],
    )(*parts)


def _scatter_to_chips(parts):
    nb = len(parts)

    def body(*refs):
        ins, outs, sems = refs[:nb], refs[nb:2 * nb], refs[2 * nb:]
        _scatter_start(ins, outs, *sems)
        _scatter_finish(ins, outs, *sems)

    return _pc(
        body, name="scatter_to_chips", in_specs=[ANY] * nb, out_specs=[ANY] * nb,
        out_shape=[SDS(p.shape, p.dtype) for p in parts], scratch_shapes=_scatter_sems(nb),
    )(*parts)


def _scatter_begin(sums):
    hbm = pl.BlockSpec(memory_space=pltpu.HBM)
    sem = pl.BlockSpec(memory_space=pltpu.SEMAPHORE)

    def body(src_ref, land_ref, send_sem, recv_sem, src_thru, land_thru, token):
        x, y, c, chips = _place()
        for chip in chips:
            _remote(src_ref.at[2 * chip[0] + chip[1]], land_ref.at[2 * x + y], send_sem, recv_sem, (*chip, c)).start()
        token[...] = jnp.zeros_like(token)

    pin = lambda a: pltpu.with_memory_space_constraint(a, pltpu.HBM)
    return pl.pallas_call(
        body, name="scatter_begin",
        out_shape=(pltpu.SemaphoreType.DMA(()), pltpu.SemaphoreType.DMA(()), pltpu.HBM(sums.shape, sums.dtype),
                   pltpu.HBM(sums.shape, sums.dtype), SDS((8, 128), f32)),
        in_specs=(hbm, hbm), out_specs=(sem, sem, hbm, hbm, pl.BlockSpec(memory_space=pltpu.VMEM)),
        input_output_aliases={0: 2, 1: 3},
        compiler_params=pltpu.CompilerParams(has_side_effects=pltpu.SideEffectType.DATAFLOW_SIDE_EFFECTING),
    )(pin(sums), pin(sums + jnp.zeros_like(sums)))


def _scatter_end(send_sem, recv_sem, src_thru, land_thru, after):
    hbm = pl.BlockSpec(memory_space=pltpu.HBM)
    sem = pl.BlockSpec(memory_space=pltpu.SEMAPHORE)

    def body(src_ref, land_ref, send_sem, recv_sem, after_ref, src_dead, got_ref):
        x, y, c, _ = _place()
        three = pl.ds(0, 3)
        cp = _remote(src_ref.at[three], land_ref.at[three], send_sem, recv_sem, (x, y, c))
        cp.wait_send()
        cp.wait_recv()

    return pl.pallas_call(
        body, name="scatter_end",
        out_shape=(pltpu.HBM(src_thru.shape, src_thru.dtype), pltpu.HBM(land_thru.shape, land_thru.dtype)),
        in_specs=(hbm, hbm, sem, sem, pl.BlockSpec(memory_space=pl.ANY)), out_specs=(hbm, hbm),
        input_output_aliases={0: 0, 1: 1},
        compiler_params=pltpu.CompilerParams(has_side_effects=pltpu.SideEffectType.DATAFLOW_SIDE_EFFECTING),
    )(src_thru, land_thru, send_sem, recv_sem, after)[1]


def _join_halves(fulls, ranges):
    nb = len(fulls)
    nr = max(len(r) for r in ranges)

    def body(*refs):
        ins, outs = refs[:nb], refs[nb:2 * nb]
        ssem, rsem = refs[2 * nb:]
        x, y, c, _ = _place()
        sib = (x, y, 1 - c)
        sends, lands = [], []
        for b in range(nb):
            for g, (row0, rows) in enumerate(ranges[b]):
                half = rows // 2
                for q, (off, n) in enumerate(_chunks(half, NQ_D2D)):
                    mine = pl.ds(row0 + c * half + off, n)
                    sends.append(_remote(ins[b].at[mine, :], outs[b].at[mine, :], ssem.at[b, g, q], rsem.at[b, g, q], sib))
                    other = outs[b].at[pl.ds(row0 + (1 - c) * half + off, n), :]
                    lands.append(_remote(other, other, ssem.at[b, g, q], rsem.at[b, g, q], sib))
        for cp in sends:
            cp.start()
        for cp in lands:
            cp.wait_recv()
        for cp in sends:
            cp.wait_send()

    return _pc(
        body, name="join_halves", in_specs=[ANY] * nb, out_specs=[ANY] * nb,
        out_shape=[SDS(h.shape, h.dtype) for h in fulls], input_output_aliases={b: b for b in range(nb)},
        scratch_shapes=[pltpu.SemaphoreType.DMA((nb, nr, NQ_D2D)), pltpu.SemaphoreType.DMA((nb, nr, NQ_D2D))],
    )(*fulls)


def _gather_small(vec):
    def body(v_ref, o_ref, ssem, rsem, lsem):
        x, y, c, _ = _place()
        mine = o_ref.at[4 * x + 2 * y + c]
        local = pltpu.make_async_copy(v_ref, mine, lsem)
        local.start()
        sends = []
        for k in range(1, 8):
            peer = (x ^ (k >> 2), y ^ ((k >> 1) & 1), c ^ (k & 1))
            cp = _remote(v_ref, mine, ssem.at[k - 1], rsem.at[k - 1], peer)
            cp.start()
            sends.append(cp)
        for k in range(1, 8):
            px, py, pc = x ^ (k >> 2), y ^ ((k >> 1) & 1), c ^ (k & 1)
            slot = o_ref.at[4 * px + 2 * py + pc]
            _remote(slot, slot, ssem.at[k - 1], rsem.at[k - 1], (x, y, c)).wait_recv()
        for cp in sends:
            cp.wait_send()
        local.wait()

    return _pc(
        body, name="gather_small", in_specs=[ANY], out_specs=ANY, out_shape=SDS((8,) + vec.shape, vec.dtype),
        scratch_shapes=[pltpu.SemaphoreType.DMA((7,)), pltpu.SemaphoreType.DMA((7,)), pltpu.SemaphoreType.DMA],
    )(vec)


def _block_diag(pw):
    return jnp.concatenate([jnp.pad(pw[g], ((0, 0), (64 * g, 192 - 64 * g))) for g in range(4)], axis=0)


def _own_columns(full, chip):
    n = full.shape[-1] // NCH
    parts = full.reshape(full.shape[:-1] + (NCH, n))
    sel = (lax.broadcasted_iota(jnp.int32, (NCH, 1), 0) == chip)
    return jnp.sum(jnp.where(sel, parts, 0.0), axis=-2)


def _at_own_columns(shard, chip):
    n = shard.shape[-1]
    sel = (lax.broadcasted_iota(jnp.int32, (NCH * n,), 0) // n == chip)
    return jnp.where(sel, jnp.tile(shard, NCH), 0.0)


def _pad_rows(a, rows):
    return jnp.pad(a, ((0, rows - a.shape[0]),) + ((0, 0),) * (a.ndim - 1))


def _ffn_block(l, which):
    return 7 * l + 3 * which


def _wout_block(l):
    return 7 * l + 6


class _Weights:
    def __init__(self):
        self.ffn, self.wout, self.w_aug, self.rides = {}, {}, {}, {}

    @classmethod
    def from_blob(cls, blob, w_aug):
        self = cls()
        for l in range(DEPTH):
            self.ffn[(l, 0)], self.ffn[(l, 1)] = (blob, _ffn_block(l, 0)), (blob, _ffn_block(l, 1))
            self.wout[l], self.w_aug[l] = (blob, _wout_block(l)), w_aug[l]
        return self

    def set_w_in(self, l, gathered):
        self.w_aug[l] = jnp.pad(gathered.transpose(1, 0, 2).reshape(D, INW), ((0, 0), (0, INP - INW)))

    def ffn_fwd(self, l, which, x, nw):
        arr, k0 = self.ffn[(l, which)]
        if (l, which) not in self.rides:
            return _ffn_fwd(x, nw, arr, k0)
        shards, landed = self.rides[(l, which)]
        out, *gathered = _ffn_fwd(x, nw, arr, k0, shards)
        landed(gathered)
        return out


def _layer_fwd(l, x0, pos, freq, wts, ws):
    sv = {"x0": x0}
    x1 = ws.ffn_fwd(l, 0, x0, wts["ffn1_norm"][l:l + 1])
    att, att4, att16, pu, dq, dz, dba = _inproj_fwd(x1, wts["mix_norm"][l:l + 1], ws.w_aug[l], pos, freq)
    qkvs = [att, att4.reshape(S, 768), att16.reshape(S, 768)]
    (o1, l1), (o4, l4), (o16, l16) = [_attn_fwd(q, NB // d) for q, d in zip(qkvs, PATTERN_DIL)]
    ols = (o1, l1, o4.reshape(4, S // 4, ATT), l4.reshape(4, S // 4, ATT), o16.reshape(16, S // 16, ATT),
           l16.reshape(16, S // 16, ATT))
    ya = _merge_fwd(*ols)
    yb = _pool_fwd(pu, wts["pool_bd"][l], wts["pool_scale"][l:l + 1])
    c = _conv_fwd(dq, wts["conv_w"][l])
    u, w, qg, kd, intra, aux = _dn_local_fwd(c, dba, wts["dn_par"][l])
    o_dn, states = _dn_rec_fwd(u, w, qg, kd, intra, aux)
    yc = _dn_post_fwd(o_dn, dz, wts["dn_out_norm"][l:l + 1])
    x2 = _outproj_fwd(x1, ya, yb, yc, *ws.wout[l])
    x3 = ws.ffn_fwd(l, 1, x2, wts["ffn2_norm"][l:l + 1])
    sv.update(x1=x1, x2=x2, qkvs=qkvs, ols=ols, ya=ya, yb=yb, yc=yc, pu=pu, dq=dq, dz=dz, dba=dba, c=c,
              u=u, w=w, qg=qg, kd=kd, intra=intra, aux=aux, states=states, o_dn=o_dn)
    return x3, sv


def _wout_part(g):
    return jnp.pad(g.astype(bf16).reshape(NCH, 256, D), ((0, 0), (0, FC - 256), (0, 0)))


def _win_part(g):
    return g[:, :INW].astype(bf16).reshape(D, NCH, INC).transpose(1, 0, 2)


def _layer_bwd(l, dx3, sv, pos, freq, wts, ws, ride=None, prep=None):
    gr = {}
    g2, u2, d2, dh4, *pieces_before = _ffn_bwd(sv["x2"], wts["ffn2_norm"][l:l + 1], *ws.ffn[(l, 1)], dx3, ride)
    gr.update(ffn2_w_gate=g2, ffn2_w_up=u2, ffn2_w_down=d2)
    dx2, gr["ffn2_norm"], dya, dyb, dyc, gr["w_out"] = _outproj_bwd(sv["x2"], wts["ffn2_norm"][l:l + 1], dx3, dh4, sv["ya"],
                                                                     sv["yb"], sv["yc"], *ws.wout[l])
    do_dn, ddz, gr["dn_out_norm"] = _dn_post_bwd(sv["o_dn"], sv["dz"], wts["dn_out_norm"][l:l + 1], dyc)
    du, dw, dqg, dkd, dintra, daux = _dn_rec_bwd(sv["u"], sv["w"], sv["qg"], sv["kd"], sv["intra"], sv["aux"], sv["states"], do_dn)
    dc, ddba, gr["dn_par"] = _dn_local_bwd(sv["c"], sv["dba"], wts["dn_par"][l], du, dw, dqg, dkd, dintra, daux)
    ddq, gr["conv_w"] = _conv_bwd(sv["dq"], wts["conv_w"][l], dc)
    dpu, gr["pool_bd"], gr["pool_scale"] = _pool_bwd(sv["pu"], wts["pool_bd"][l], wts["pool_scale"][l:l + 1], dyb)
    dols = _merge_bwd(*sv["ols"], dya)
    flat = lambda a: a.reshape(S, ATT)
    datts = [_attn_bwd(q, flat(sv["ols"][2 * p]), flat(sv["ols"][2 * p + 1]), flat(dols[2 * p]), flat(dols[2 * p + 1]), NB // d)
             for p, (q, d) in enumerate(zip(sv["qkvs"], PATTERN_DIL))]
    dx1, gr["mix_norm"], gr["w_aug"] = _inproj_bwd(sv["x1"], wts["mix_norm"][l:l + 1], ws.w_aug[l], pos, freq, dx2,
                                                    datts[0], datts[1].reshape(4, S // 4, 768),
                                                    datts[2].reshape(16, S // 16, 768), dpu, ddq, ddz, ddba)
    own = None
    if prep is not None:
        own = prep([jnp.concatenate([g2, u2, d2, _wout_part(gr["w_out"])], axis=1), _win_part(gr["w_aug"])])
    g1, u1, d1, dh4, *pieces_own = _ffn_bwd(sv["x0"], wts["ffn1_norm"][l:l + 1], *ws.ffn[(l, 0)], dx1, own)
    begun, nw1 = None, wts["ffn1_norm"][l:l + 1]
    if prep is not None:
        begun = _scatter_begin(prep([jnp.concatenate([g1, u1, d1], axis=1)])[0])
        nw1 = nw1 + begun[4][0, 0]
    dx0, gr["ffn1_norm"] = _norm_bwd(sv["x0"], nw1, dx1, dh4)
    gr.update(ffn1_w_gate=g1, ffn1_w_up=u1, ffn1_w_down=d1)
    return dx0, gr, pieces_before, pieces_own, begun


def _device_step(x, pos, target, wts, ws, prep=None):
    freq = jnp.tile(ROPE_THETA ** (-jnp.arange(0, EH, 2, dtype=f32) / EH), 2 * ATT // EH).reshape(1, ATT)
    saved = []
    h = x
    for l in range(DEPTH):
        h, sv = _layer_fwd(l, h, pos, freq, wts, ws)
        saved.append(sv)
    dh, g_final, loss = _final(h, wts["final_norm"], target)
    grads = [None] * DEPTH
    dh, grads[1], *_ = _layer_bwd(1, dh, saved[1], pos, freq, wts, ws)
    sums1 = None
    if prep is not None:
        g = grads[1]
        ffn = [g[f"ffn{f}_w_{n}"] for f in (1, 2) for n in ("gate", "up", "down")]
        sums1 = prep([jnp.concatenate(ffn + [_wout_part(g["w_out"])], axis=1), _win_part(g["w_aug"])])
    dh, grads[0], pieces1, pieces0, begun = _layer_bwd(0, dh, saved[0], pos, freq, wts, ws, sums1, prep)
    return loss, dh, g_final, grads, pieces1, pieces0, begun


_SMALL = (("ffn1_norm", (DEPTH, D)), ("mix_norm", (DEPTH, D)), ("pool_w", (DEPTH, 4, 64, 64)), ("pool_scale", (DEPTH, 256)),
          ("dn_conv_w", (DEPTH, 4, CW)), ("dn_a_log", (DEPTH, 4)), ("dn_dt_bias", (DEPTH, 4)), ("dn_out_norm", (DEPTH, 128)),
          ("ffn2_norm", (DEPTH, D)), ("final_norm", (D,)), ("loss", (1,)))


def _pack_small(vals):
    rows = []
    for name, shape in _SMALL:
        flat = vals[name].astype(f32).reshape(-1)
        rows.append(jnp.pad(flat, (0, _small_rows(shape) * 128 - flat.shape[0])).reshape(-1, 128))
    out = jnp.concatenate(rows, axis=0)
    return _pad_rows(out, -(-out.shape[0] // 16) * 16)


def _small_rows(shape):
    return -(-int(np.prod(shape)) // 1024) * 8


def _unpack_small(packed):
    vals, r = {}, 0
    for name, shape in _SMALL:
        size, n = int(np.prod(shape)), _small_rows(shape)
        vals[name] = packed[r:r + n].reshape(-1)[:size].reshape(shape)
        r += n
    return vals


def kernel(x, positions, ffn1_norm, ffn1_w_gate, ffn1_w_up, ffn1_w_down, mix_norm, w_in, pool_w, pool_scale, dn_conv_w, dn_a_log, dn_dt_bias, dn_out_norm, w_out, ffn2_norm, ffn2_w_gate, ffn2_w_up, ffn2_w_down, final_norm, loss_target, m_ffn1_norm, m_ffn1_w_gate, m_ffn1_w_up, m_ffn1_w_down, m_mix_norm, m_w_in, m_pool_w, m_pool_scale, m_dn_conv_w, m_dn_a_log, m_dn_dt_bias, m_dn_out_norm, m_w_out, m_ffn2_norm, m_ffn2_w_gate, m_ffn2_w_up, m_ffn2_w_down, m_final_norm, v_ffn1_norm, v_ffn1_w_gate, v_ffn1_w_up, v_ffn1_w_down, v_mix_norm, v_w_in, v_pool_w, v_pool_scale, v_dn_conv_w, v_dn_a_log, v_dn_dt_bias, v_dn_out_norm, v_w_out, v_ffn2_norm, v_ffn2_w_gate, v_ffn2_w_up, v_ffn2_w_down, v_final_norm):
    names = ["ffn1_norm", "ffn1_w_gate", "ffn1_w_up", "ffn1_w_down", "mix_norm", "w_in", "pool_w", "pool_scale", "dn_conv_w",
             "dn_a_log", "dn_dt_bias", "dn_out_norm", "w_out", "ffn2_norm", "ffn2_w_gate", "ffn2_w_up", "ffn2_w_down", "final_norm"]
    W = dict(zip(names, [ffn1_norm, ffn1_w_gate, ffn1_w_up, ffn1_w_down, mix_norm, w_in, pool_w, pool_scale, dn_conv_w,
                         dn_a_log, dn_dt_bias, dn_out_norm, w_out, ffn2_norm, ffn2_w_gate, ffn2_w_up, ffn2_w_down, final_norm]))
    M = dict(zip(names, [m_ffn1_norm, m_ffn1_w_gate, m_ffn1_w_up, m_ffn1_w_down, m_mix_norm, m_w_in, m_pool_w, m_pool_scale,
                         m_dn_conv_w, m_dn_a_log, m_dn_dt_bias, m_dn_out_norm, m_w_out, m_ffn2_norm, m_ffn2_w_gate, m_ffn2_w_up,
                         m_ffn2_w_down, m_final_norm]))
    V = dict(zip(names, [v_ffn1_norm, v_ffn1_w_gate, v_ffn1_w_up, v_ffn1_w_down, v_mix_norm, v_w_in, v_pool_w, v_pool_scale,
                         v_dn_conv_w, v_dn_a_log, v_dn_dt_bias, v_dn_out_norm, v_w_out, v_ffn2_norm, v_ffn2_w_gate, v_ffn2_w_up,
                         v_ffn2_w_down, v_final_norm]))
    chip = 2 * lax.axis_index("x") + lax.axis_index("y")

    ffn_names = [(f"ffn{f}_w_gate", f"ffn{f}_w_up", f"ffn{f}_w_down") for f in (1, 2)]
    tr = lambda t: jnp.swapaxes(t, -1, -2)
    def ffn_rows(l, which):
        g, u, dn = ffn_names[which]
        return [tr(W[g][l]), tr(W[u][l]), W[dn][l]]

    def second_half(l):
        return jnp.concatenate(ffn_rows(l, 1) + [jnp.pad(W["w_out"][l], ((0, FC - 256), (0, 0)))], axis=0).astype(bf16)

    ws = _Weights()
    first0, = _all_gather_weights([jnp.concatenate(ffn_rows(0, 0), axis=0).astype(bf16)])
    ws.ffn[(0, 0)] = (first0, 0)

    def landed_00(gathered):
        ws.ffn[(0, 1)], ws.wout[0] = (gathered[0], 0), (gathered[0], 3)
        ws.set_w_in(0, gathered[1])

    def landed_01(gathered):
        ws.ffn[(1, 0)] = (gathered[0], 0)
        ws.set_w_in(1, gathered[1])

    def landed_10(gathered):
        ws.ffn[(1, 1)], ws.wout[1] = (gathered[0], 0), (gathered[0], 3)

    ws.rides[(0, 0)] = ([second_half(0), W["w_in"][0].astype(bf16)], landed_00)
    ws.rides[(0, 1)] = ([jnp.concatenate(ffn_rows(1, 0), axis=0).astype(bf16), W["w_in"][1].astype(bf16)], landed_01)
    ws.rides[(1, 0)] = ([second_half(1)], landed_10)
    conv_all = _gather_small(_pad_rows(dn_conv_w.reshape(DEPTH * 4 * (CW // NCH) // 128, 128), 32))
    conv_full = jnp.concatenate([conv_all[2 * j, :DEPTH * 4 * (CW // NCH) // 128].reshape(DEPTH, 4, CW // NCH) for j in range(NCH)],
                                axis=-1)

    par = jnp.pad(jnp.stack([dn_a_log, dn_dt_bias], axis=1), ((0, 0), (0, 6), (4, 120)))
    wts = dict(ffn1_norm=ffn1_norm, mix_norm=mix_norm, ffn2_norm=ffn2_norm, final_norm=final_norm.reshape(1, D),
               pool_bd=jnp.stack([_block_diag(pool_w[l]) for l in range(DEPTH)]).astype(bf16),
               pool_scale=pool_scale, conv_w=jnp.pad(conv_full, ((0, 0), (0, 4), (0, 0))),
               dn_par=par, dn_out_norm=dn_out_norm)

    c_arr = lax.axis_index("c").astype(jnp.int32).reshape(1)

    def prep(parts):
        return [_sum_core_pair(p, g, c_arr) for p, g in zip(parts, _send_sibling_half(parts))]

    loss, dx, g_final, grads, pieces1, pieces0, begun = _device_step(x[0], positions.reshape(S, 1), loss_target[0], wts, ws,
                                                                      prep)

    small = {"loss": loss[0, 0:1], "final_norm": g_final.reshape(D)}
    for n in ("ffn1_norm", "mix_norm", "ffn2_norm", "pool_scale", "dn_out_norm"):
        small[n] = jnp.stack([grads[l][n].reshape(-1) for l in range(DEPTH)])
    small["pool_w"] = jnp.stack([jnp.stack([grads[l]["pool_bd"][64 * g:64 * (g + 1), 64 * g:64 * (g + 1)] for g in range(4)])
                                 for l in range(DEPTH)])
    small["dn_conv_w"] = jnp.stack([grads[l]["conv_w"][0:4] for l in range(DEPTH)])
    small["dn_a_log"] = jnp.stack([grads[l]["dn_par"][0, 4:8] for l in range(DEPTH)])
    small["dn_dt_bias"] = jnp.stack([grads[l]["dn_par"][1, 4:8] for l in range(DEPTH)])

    full_b = _sum_chips(pieces0[0], c_arr, None, 3 * FC, RB)
    full_b = _sum_chips(pieces1[0], c_arr, full_b, 7 * FC, RB)
    full_c = _sum_chips(pieces0[1], c_arr, None, 0, RC)
    full_c = _sum_chips(pieces1[1], c_arr, full_c, D, RC)
    full_b, full_c = _join_halves([full_b, full_c], [[(3 * FC, 4 * FC), (7 * FC, 7 * FC)], [(0, D), (D, D)]])

    transposed = ("ffn1_w_gate", "ffn1_w_up", "ffn2_w_gate", "ffn2_w_up")
    where = {"ffn1_w_gate": (FC // 2, lambda l: 14 * l), "ffn1_w_up": (FC // 2, lambda l: 14 * l + 2),
             "ffn1_w_down": (FC // 2, lambda l: 14 * l + 4), "ffn2_w_gate": (FC // 2, lambda l: 14 * l + 6),
             "ffn2_w_up": (FC // 2, lambda l: 14 * l + 8), "ffn2_w_down": (FC // 2, lambda l: 14 * l + 10),
             "w_out": (64, lambda l: (FC // 64) * (7 * l + 6)), "w_in": (D // 2, lambda l: 2 * l)}
    big_res, raw = {}, {}

    def update(n, gblob):
        tile, first = where[n]
        t = tr if n in transposed else (lambda a: a)
        raw[n] = _adamw_rows(t(W[n]), t(M[n]), t(V[n]), gblob, tile, first, "adamw_" + n)
        big_res[n] = [t(r) for r in raw[n]]

    early = ("ffn2_w_gate", "ffn2_w_up", "ffn2_w_down", "w_out")
    for n in early:
        update(n, full_b)
    update("w_in", full_c)
    after = jnp.stack([raw[n][0][0, 0:8, 0:128] for n in early] + [raw["w_in"][0][0, 0:8, 0:128]])
    pieces_c = _scatter_end(*begun[:4], after)
    full_b = _sum_chips(pieces_c, c_arr, full_b, 0, RB)
    full_b, = _join_halves([full_b], [[(0, 3 * FC)]])
    for n in ("ffn1_w_gate", "ffn1_w_up", "ffn1_w_down"):
        update(n, full_b)

    packed = _pack_small(small) + 0.0 * pieces_c[0, 0:1, 0:128].astype(f32)
    g_small = _sum_pieces(_gather_small(packed), f32, "sum_small")
    gs = _unpack_small(g_small)

    def small_of(T):
        d = {n: T[n] for n, _ in _SMALL if n not in ("loss", "dn_conv_w")}
        d["loss"] = jnp.zeros((1,), f32)
        d["dn_conv_w"] = _at_own_columns(T["dn_conv_w"], chip)
        return _pack_small(d)

    res_s = _adamw(small_of(W), g_small, small_of(M), small_of(V), "adamw_small")
    small_out = [_unpack_small(r) for r in res_s]

    def split_blobs(b, c):
        out = {}
        b7 = b.reshape(DEPTH, 7, FC, D)
        for k, n in enumerate(n for names3 in ffn_names for n in names3):
            out[n] = tr(b7[:, k]) if n in transposed else b7[:, k]
        out["w_out"] = b7[:, 6, :256]
        out["w_in"] = c.reshape(DEPTH, D, INC)
        return out

    def assemble(big, sm):
        out = []
        for n in names:
            if n in big:
                out.append(big[n])
            elif n == "dn_conv_w":
                out.append(_own_columns(sm[n], chip))
            else:
                out.append(sm[n])
        return out

    grad_list = assemble(split_blobs(full_b, full_c), gs)
    outs = [gs["loss"].reshape(()), dx.reshape(1, S, D)] + grad_list
    for k in range(3):
        outs += assemble({n: r[k] for n, r in big_res.items()}, small_out[k])
    return tuple(outs)
```

```python
import functools
import math

import jax
import jax.numpy as jnp
import numpy as np
from jax import lax
from jax.experimental import pallas as pl
from jax.experimental.pallas import tpu as pltpu

f32 = jnp.float32
bf16 = jnp.bfloat16
SDS = jax.ShapeDtypeStruct
MESH = pl.DeviceIdType.MESH

S = 4096
D = 1024
DEPTH = 2
FF = 2816
NCH = 4
FC = FF // NCH
INW = 3080
INC = INW // NCH
INP = 3200
ATT = 256
EH = 64
NBLK = 128
DNW = 512
DH = 128
CH = 64
NCHUNK = S // CH
EPS = 1e-6
ROPE_THETA = 10000.0
PATTERN_DIL = (1, 4, 16)
ADAM_LR, ADAM_B1, ADAM_B2, ADAM_EPS, ADAM_WD, ADAM_STEP = 0.001, 0.9, 0.999, 1e-08, 0.01, 10
VMEM_BYTES_V7X = 64 * 1024 * 1024
NEG = -1e30

TM = 512
RB, RC = 14 * FC, 2 * D


def _cp(vmem_mb=48, sem=None):
    kw = dict(vmem_limit_bytes=vmem_mb * 1024 * 1024)
    if sem is not None:
        kw["dimension_semantics"] = sem
    return pltpu.CompilerParams(**kw)


def _pc(*args, **kwargs):
    pin = lambda s: pltpu.HBM(s.shape, s.dtype) if isinstance(s, SDS) and jnp.issubdtype(s.dtype, jnp.floating) else s
    out = kwargs["out_shape"]
    kwargs["out_shape"] = [pin(s) for s in out] if isinstance(out, (list, tuple)) else pin(out)
    call = pl.pallas_call(*args, **kwargs)

    def run(*operands):
        pinned = [pltpu.with_memory_space_constraint(o, pltpu.HBM) if jnp.issubdtype(o.dtype, jnp.floating) else o
                  for o in operands]
        return call(*pinned)

    return run


def _dot(a, b):
    return jnp.dot(a, b, preferred_element_type=f32)


def _dot_nt(a, b):
    return lax.dot_general(a, b, (((1,), (1,)), ((), ())), preferred_element_type=f32)


def _dot_tn(a, b):
    return lax.dot_general(a, b, (((0,), (0,)), ((), ())), preferred_element_type=f32)


def _rms(x, w):
    r = lax.rsqrt(jnp.mean(x * x, axis=-1, keepdims=True) + EPS)
    return x * r * w, r


def _rms_bwd(x, w, r, dh):
    xhat = x * r
    dw = jnp.sum(dh * xhat, axis=0, keepdims=True)
    dxh = dh * w
    dx = r * (dxh - xhat * jnp.mean(dxh * xhat, axis=-1, keepdims=True))
    return dx, dw


def _ffn_fwd(x, nw, blob, k0, ride=None):
    kg, ku, kd = k0, k0 + 1, k0 + 2
    nr = 0 if ride is None else len(ride)
    ni = S // TM

    def body(*refs):
        x_ref, nw_ref, wg_ref, wu_ref, wd_ref = refs[:5]
        ride_in = refs[5:5 + nr]
        o_ref = refs[5 + nr]
        ride_out = refs[6 + nr:6 + 2 * nr]
        h_scr, acc_scr = refs[6 + 2 * nr:8 + 2 * nr]
        sems = refs[8 + 2 * nr:]
        i = pl.program_id(0)
        j = pl.program_id(1)

        if nr:
            @pl.when(jnp.logical_and(i == 0, j == 0))
            def _():
                _gather_start(ride_in, ride_out, *sems)

        @pl.when(j == 0)
        def _():
            h, _ = _rms(x_ref[...], nw_ref[...])
            h_scr[...] = h.astype(bf16)
            acc_scr[...] = jnp.zeros_like(acc_scr)

        h = h_scr[...]
        g = _dot_nt(h, wg_ref[0])
        u = _dot_nt(h, wu_ref[0])
        a = (g * jax.nn.sigmoid(g) * u).astype(bf16)
        acc_scr[...] += _dot(a, wd_ref[0])

        @pl.when(j == NCH - 1)
        def _():
            o_ref[...] = x_ref[...] + 0.5 * acc_scr[...]

        if nr:
            @pl.when(jnp.logical_and(i == ni - 1, j == NCH - 1))
            def _():
                _gather_finish(ride_in, ride_out, *sems)

    wspec = lambda k: pl.BlockSpec((1, FC, D), lambda i, j: (j, k, 0))
    rides = [] if ride is None else list(ride)
    res = _pc(
        body, grid=(ni, NCH), name="ffn_fwd_ride" if nr else "ffn_fwd",
        in_specs=[pl.BlockSpec((TM, D), lambda i, j: (i, 0)),
                  pl.BlockSpec((1, D), lambda i, j: (0, 0)),
                  wspec(kg), wspec(ku), wspec(kd)] + [ANY] * nr,
        out_specs=[pl.BlockSpec((TM, D), lambda i, j: (i, 0))] + [ANY] * nr,
        out_shape=[SDS((S, D), f32)] + [SDS((NCH,) + r.shape, r.dtype) for r in rides],
        scratch_shapes=[pltpu.VMEM((TM, D), bf16), pltpu.VMEM((TM, D), f32)] + (_gather_sems(nr) if nr else []),
        compiler_params=_cp(40),
    )(x, nw, blob, blob, blob, *rides)
    return res if nr else res[0]


def _ffn_bwd(x, nw, blob, k0, dy, ride=None):
    nt = S // TM
    kg, ku, kd = k0, k0 + 1, k0 + 2
    nr = 0 if ride is None else len(ride)

    def body(*refs):
        x_ref, nw_ref, wg_ref, wu_ref, wd_ref, dy_ref = refs[:6]
        ride_in = refs[6:6 + nr]
        dwg_ref, dwu_ref, dwd_ref, dh_ref = refs[6 + nr:10 + nr]
        ride_out = refs[10 + nr:10 + 2 * nr]
        ag, au, ad = refs[10 + 2 * nr:13 + 2 * nr]
        sems = refs[13 + 2 * nr:]
        j = pl.program_id(0)
        i = pl.program_id(1)

        if nr:
            @pl.when(jnp.logical_and(j == 0, i == 0))
            def _():
                _scatter_start(ride_in, ride_out, *sems)

        @pl.when(i == 0)
        def _():
            ag[...] = jnp.zeros_like(ag)
            au[...] = jnp.zeros_like(au)
            ad[...] = jnp.zeros_like(ad)

        hf, _ = _rms(x_ref[...], nw_ref[...])
        h = hf.astype(bf16)
        g = _dot_nt(h, wg_ref[0])
        u = _dot_nt(h, wu_ref[0])
        sg = jax.nn.sigmoid(g)
        s = g * sg
        a = (s * u).astype(bf16)
        dyb = (0.5 * dy_ref[...]).astype(bf16)
        da = _dot_nt(dyb, wd_ref[0])
        ad[...] += _dot_tn(a, dyb)
        du = (da * s).astype(bf16)
        dg = (da * u * (sg * (1.0 + g * (1.0 - sg)))).astype(bf16)
        ag[...] += _dot_tn(dg, h)
        au[...] += _dot_tn(du, h)
        dh_ref[0] = (_dot(dg, wg_ref[0]) + _dot(du, wu_ref[0])).astype(bf16)

        @pl.when(i == nt - 1)
        def _():
            dwg_ref[0] = ag[...].astype(bf16)
            dwu_ref[0] = au[...].astype(bf16)
            dwd_ref[0] = ad[...].astype(bf16)

        if nr:
            @pl.when(jnp.logical_and(j == NCH - 1, i == nt - 1))
            def _():
                _scatter_finish(ride_in, ride_out, *sems)

    wspec = lambda k: pl.BlockSpec((1, FC, D), lambda j, i: (j, k, 0))
    gspec = pl.BlockSpec((1, FC, D), lambda j, i: (j, 0, 0))
    rides = [] if ride is None else list(ride)
    return _pc(
        body, grid=(NCH, nt), name="ffn_bwd_ride" if nr else "ffn_bwd",
        in_specs=[pl.BlockSpec((TM, D), lambda j, i: (i, 0)),
                  pl.BlockSpec((1, D), lambda j, i: (0, 0)),
                  wspec(kg), wspec(ku), wspec(kd),
                  pl.BlockSpec((TM, D), lambda j, i: (i, 0))] + [ANY] * nr,
        out_specs=[gspec, gspec, gspec, pl.BlockSpec((1, TM, D), lambda j, i: (j, i, 0))] + [ANY] * nr,
        out_shape=[SDS((NCH, FC, D), bf16)] * 3 + [SDS((NCH, S, D), bf16)] + [SDS(r.shape, r.dtype) for r in rides],
        scratch_shapes=[pltpu.VMEM((FC, D), f32)] * 3 + (_scatter_sems(nr) if nr else []),
        compiler_params=_cp(56),
    )(x, nw, blob, blob, blob, dy, *rides)


def _norm_bwd(x, nw, dres, dh4):
    nt = S // TM
    nparts = dh4.shape[0]

    def body(x_ref, nw_ref, dres_ref, dh_ref, dx_ref, dnw_ref):
        i = pl.program_id(0)
        dh = dh_ref[0].astype(f32)
        for p in range(1, nparts):
            dh = dh + dh_ref[p].astype(f32)
        xv = x_ref[...]
        _, r = _rms(xv, nw_ref[...])
        dx, dw = _rms_bwd(xv, nw_ref[...], r, dh)
        dx_ref[...] = dres_ref[...] + dx

        @pl.when(i == 0)
        def _():
            dnw_ref[...] = jnp.zeros_like(dnw_ref)

        dnw_ref[...] += dw

    return _pc(
        body, grid=(nt,), name="norm_bwd",
        in_specs=[pl.BlockSpec((TM, D), lambda i: (i, 0)),
                  pl.BlockSpec((1, D), lambda i: (0, 0)),
                  pl.BlockSpec((TM, D), lambda i: (i, 0)),
                  pl.BlockSpec((nparts, TM, D), lambda i: (0, i, 0))],
        out_specs=[pl.BlockSpec((TM, D), lambda i: (i, 0)), pl.BlockSpec((1, D), lambda i: (0, 0))],
        out_shape=[SDS((S, D), f32), SDS((1, D), f32)],
        compiler_params=_cp(40),
    )(x, nw, dres, dh4)


def _final(x, nw, target):
    nt = S // TM

    def body(x_ref, nw_ref, t_ref, dx_ref, dnw_ref, loss_ref):
        i = pl.program_id(0)
        xv = x_ref[...]
        y, r = _rms(xv, nw_ref[...])
        err = y - t_ref[...]
        part = 0.5 * jnp.sum(jnp.mean(err * err, axis=-1, keepdims=True), axis=0, keepdims=True)
        dx, dw = _rms_bwd(xv, nw_ref[...], r, err * (1.0 / D))
        dx_ref[...] = dx

        @pl.when(i == 0)
        def _():
            dnw_ref[...] = jnp.zeros_like(dnw_ref)
            loss_ref[...] = jnp.zeros_like(loss_ref)

        dnw_ref[...] += dw
        loss_ref[...] += jnp.broadcast_to(part, loss_ref.shape)

    return _pc(
        body, grid=(nt,), name="final_loss",
        in_specs=[pl.BlockSpec((TM, D), lambda i: (i, 0)),
                  pl.BlockSpec((1, D), lambda i: (0, 0)),
                  pl.BlockSpec((TM, D), lambda i: (i, 0))],
        out_specs=[pl.BlockSpec((TM, D), lambda i: (i, 0)), pl.BlockSpec((1, D), lambda i: (0, 0)),
                   pl.BlockSpec((1, 128), lambda i: (0, 0))],
        out_shape=[SDS((S, D), f32), SDS((1, D), f32), SDS((1, 128), f32)],
        compiler_params=_cp(40),
    )(x, nw, target)


def _rot_half(t):
    lane = lax.broadcasted_iota(jnp.int32, t.shape, 1)
    first = (lane % EH) < (EH // 2)
    return jnp.where(first, -pltpu.roll(t, ATT - EH // 2, 1), pltpu.roll(t, EH // 2, 1))


def _rope_tables(pos_ref, freq_ref):
    ang = pos_ref[...].astype(f32) * freq_ref[...]
    return jnp.cos(ang), jnp.sin(ang)


def _split_residues(val, scr, outs):
    rows, cols = val.shape
    for j in range(cols // 128):
        scr[j] = val[:, 128 * j:128 * (j + 1)]
    for ref, d in outs:
        for j in range(cols // 128):
            for r in range(d):
                ref.at[r][:, 128 * j:128 * (j + 1)] = scr.at[j][pl.ds(r, rows // d, stride=d), :]


def _join_residues(ref, d, scr):
    rows, cols = scr.shape[1], ref.shape[2]
    for j in range(cols // 128):
        for r in range(d):
            scr.at[j][pl.ds(r, rows // d, stride=d), :] = ref.at[r][:, 128 * j:128 * (j + 1)]
    return jnp.concatenate([scr[j] for j in range(cols // 128)], axis=1)


def _res_spec(d, tile, cols):
    return pl.BlockSpec((d, tile // d, cols), lambda i: (0, i, 0))


def _inproj_fwd(x, nw, w_aug, pos, freq):
    TI = 256

    def body(x_ref, nw_ref, w_hbm, pos_ref, freq_ref, att_ref, att4_ref, att16_ref, pu_ref, dq_ref, dz_ref, dba_ref,
             w_scr, r_scr):
        @pl.when(pl.program_id(0) == 0)
        def _():
            pltpu.sync_copy(w_hbm, w_scr)

        h, _ = _rms(x_ref[...], nw_ref[...])
        proj = _dot(h.astype(bf16), w_scr[...])
        cos, sin = _rope_tables(pos_ref, freq_ref)
        q = proj[:, 0:ATT]
        k = proj[:, ATT:2 * ATT]
        att = jnp.concatenate([q * cos + _rot_half(q) * sin, k * cos + _rot_half(k) * sin, proj[:, 2 * ATT:3 * ATT]], axis=1)
        att_ref[...] = att
        _split_residues(att, r_scr, [(att4_ref, 4), (att16_ref, 16)])
        pu_ref[...] = proj[:, 768:1024]
        dq_ref[...] = proj[:, 1024:2560]
        dz_ref[...] = proj[:, 2560:3072]
        dba_ref[...] = proj[:, 3072:3200]

    return _pc(
        body, grid=(S // TI,), name="inproj_fwd",
        in_specs=[pl.BlockSpec((TI, D), lambda i: (i, 0)),
                  pl.BlockSpec((1, D), lambda i: (0, 0)),
                  pl.BlockSpec(memory_space=pl.ANY),
                  pl.BlockSpec((TI, 1), lambda i: (i, 0)),
                  pl.BlockSpec((1, ATT), lambda i: (0, 0))],
        out_specs=[pl.BlockSpec((TI, 768), lambda i: (i, 0)), _res_spec(4, TI, 768), _res_spec(16, TI, 768),
                   pl.BlockSpec((TI, 256), lambda i: (i, 0)),
                   pl.BlockSpec((TI, 1536), lambda i: (i, 0)), pl.BlockSpec((TI, 512), lambda i: (i, 0)),
                   pl.BlockSpec((TI, 128), lambda i: (i, 0))],
        out_shape=[SDS((S, 768), f32), SDS((4, S // 4, 768), f32), SDS((16, S // 16, 768), f32), SDS((S, 256), f32),
                   SDS((S, 1536), f32), SDS((S, 512), f32), SDS((S, 128), f32)],
        scratch_shapes=[pltpu.VMEM((D, INP), bf16), pltpu.VMEM((6, TI, 128), f32)],
        compiler_params=_cp(48),
    )(x, nw, w_aug, pos, freq)


def _inproj_bwd(x, nw, w_aug, pos, freq, dres, datt, datt4, datt16, dpu, ddq, ddz, ddba):
    TI = 256
    nt = S // TI

    def body(x_ref, nw_ref, w_hbm, pos_ref, freq_ref, dres_ref, datt_ref, datt4_ref, datt16_ref, dpu_ref, ddq_ref, ddz_ref,
             ddba_ref, dx_ref, dnw_ref, dw_hbm, w_scr, acc, r_scr):
        i = pl.program_id(0)

        @pl.when(i == 0)
        def _():
            pltpu.sync_copy(w_hbm, w_scr)
            acc[...] = jnp.zeros_like(acc)
            dnw_ref[...] = jnp.zeros_like(dnw_ref)

        xv = x_ref[...]
        hf, r = _rms(xv, nw_ref[...])
        h = hf.astype(bf16)
        cos, sin = _rope_tables(pos_ref, freq_ref)
        datt = datt_ref[...] + _join_residues(datt4_ref, 4, r_scr)
        datt = datt + _join_residues(datt16_ref, 16, r_scr)
        dq = datt[:, 0:ATT]
        dk = datt[:, ATT:2 * ATT]
        dq = dq * cos - _rot_half(dq) * sin
        dk = dk * cos - _rot_half(dk) * sin
        dproj = jnp.concatenate([dq, dk, datt[:, 2 * ATT:3 * ATT], dpu_ref[...], ddq_ref[...], ddz_ref[...], ddba_ref[...]],
                                axis=1).astype(bf16)
        acc[...] += _dot_tn(h, dproj)
        dh = _dot_nt(dproj, w_scr[...])
        dx, dw = _rms_bwd(xv, nw_ref[...], r, dh)
        dx_ref[...] = dres_ref[...] + dx
        dnw_ref[...] += dw

        @pl.when(i == nt - 1)
        def _():
            pltpu.sync_copy(acc, dw_hbm)

    return _pc(
        body, grid=(nt,), name="inproj_bwd",
        in_specs=[pl.BlockSpec((TI, D), lambda i: (i, 0)),
                  pl.BlockSpec((1, D), lambda i: (0, 0)),
                  pl.BlockSpec(memory_space=pl.ANY),
                  pl.BlockSpec((TI, 1), lambda i: (i, 0)),
                  pl.BlockSpec((1, ATT), lambda i: (0, 0)),
                  pl.BlockSpec((TI, D), lambda i: (i, 0)),
                  pl.BlockSpec((TI, 768), lambda i: (i, 0)), _res_spec(4, TI, 768), _res_spec(16, TI, 768),
                  pl.BlockSpec((TI, 256), lambda i: (i, 0)),
                  pl.BlockSpec((TI, 1536), lambda i: (i, 0)),
                  pl.BlockSpec((TI, 512), lambda i: (i, 0)),
                  pl.BlockSpec((TI, 128), lambda i: (i, 0))],
        out_specs=[pl.BlockSpec((TI, D), lambda i: (i, 0)), pl.BlockSpec((1, D), lambda i: (0, 0)),
                   pl.BlockSpec(memory_space=pl.ANY)],
        out_shape=[SDS((S, D), f32), SDS((1, D), f32), SDS((D, INP), f32)],
        scratch_shapes=[pltpu.VMEM((D, INP), bf16), pltpu.VMEM((D, INP), f32), pltpu.VMEM((6, TI, 128), f32)],
        compiler_params=_cp(56),
    )(x, nw, w_aug, pos, freq, dres, datt, datt4, datt16, dpu, ddq, ddz, ddba)


def _outproj_fwd(x, ya, yb, yc, blob_b, kw):
    def body(x_ref, ya_ref, yb_ref, yc_ref, w_ref, o_ref):
        ycat = jnp.concatenate([ya_ref[...], yb_ref[...], yc_ref[...]], axis=1).astype(bf16)
        o_ref[...] = x_ref[...] + _dot(ycat, w_ref[:, 0:256, :].reshape(D, D))

    return _pc(
        body, grid=(S // TM,), name="outproj_fwd",
        in_specs=[pl.BlockSpec((TM, D), lambda i: (i, 0)),
                  pl.BlockSpec((TM, 256), lambda i: (i, 0)),
                  pl.BlockSpec((TM, 256), lambda i: (i, 0)),
                  pl.BlockSpec((TM, 512), lambda i: (i, 0)),
                  pl.BlockSpec((NCH, FC, D), lambda i: (0, kw, 0))],
        out_specs=pl.BlockSpec((TM, D), lambda i: (i, 0)),
        out_shape=SDS((S, D), f32),
        compiler_params=_cp(40),
    )(x, ya, yb, yc, blob_b)


def _outproj_bwd(x, nw, dres, dh4, ya, yb, yc, blob_b, kw):
    nt = S // TM
    nparts = dh4.shape[0]

    def body(x_ref, nw_ref, dres_ref, dh_ref, ya_ref, yb_ref, yc_ref, w_ref, dx_ref, dnw_ref, dya_ref, dyb_ref, dyc_ref, dw_ref):
        i = pl.program_id(0)

        @pl.when(i == 0)
        def _():
            dw_ref[...] = jnp.zeros_like(dw_ref)
            dnw_ref[...] = jnp.zeros_like(dnw_ref)

        dh = dh_ref[0].astype(f32)
        for p in range(1, nparts):
            dh = dh + dh_ref[p].astype(f32)
        xv = x_ref[...]
        _, r = _rms(xv, nw_ref[...])
        dxn, dnw = _rms_bwd(xv, nw_ref[...], r, dh)
        dx = dres_ref[...] + dxn
        dx_ref[...] = dx
        dnw_ref[...] += dnw
        dyv = dx.astype(bf16)
        ycat = jnp.concatenate([ya_ref[...], yb_ref[...], yc_ref[...]], axis=1).astype(bf16)
        dw_ref[...] += _dot_tn(ycat, dyv)
        dcat = _dot_nt(dyv, w_ref[:, 0:256, :].reshape(D, D))
        dya_ref[...] = dcat[:, 0:256]
        dyb_ref[...] = dcat[:, 256:512]
        dyc_ref[...] = dcat[:, 512:1024]

    return _pc(
        body, grid=(nt,), name="outproj_bwd",
        in_specs=[pl.BlockSpec((TM, D), lambda i: (i, 0)),
                  pl.BlockSpec((1, D), lambda i: (0, 0)),
                  pl.BlockSpec((TM, D), lambda i: (i, 0)),
                  pl.BlockSpec((nparts, TM, D), lambda i: (0, i, 0)),
                  pl.BlockSpec((TM, 256), lambda i: (i, 0)),
                  pl.BlockSpec((TM, 256), lambda i: (i, 0)),
                  pl.BlockSpec((TM, 512), lambda i: (i, 0)),
                  pl.BlockSpec((NCH, FC, D), lambda i: (0, kw, 0))],
        out_specs=[pl.BlockSpec((TM, D), lambda i: (i, 0)), pl.BlockSpec((1, D), lambda i: (0, 0)),
                   pl.BlockSpec((TM, 256), lambda i: (i, 0)), pl.BlockSpec((TM, 256), lambda i: (i, 0)),
                   pl.BlockSpec((TM, 512), lambda i: (i, 0)), pl.BlockSpec((D, D), lambda i: (0, 0))],
        out_shape=[SDS((S, D), f32), SDS((1, D), f32), SDS((S, 256), f32), SDS((S, 256), f32), SDS((S, 512), f32),
                   SDS((D, D), f32)],
        compiler_params=_cp(48),
    )(x, nw, dres, dh4, ya, yb, yc, blob_b)


QT = NBLK
NB = S // QT


def _attn_block(q, kp, kc, vp, vc, first):
    kk = jnp.concatenate([kp, kc], axis=0).astype(bf16)
    vv = jnp.concatenate([vp, vc], axis=0).astype(bf16)
    qi = lax.broadcasted_iota(jnp.int32, (4 * QT, NBLK + QT), 0) % QT
    ki = lax.broadcasted_iota(jnp.int32, (4 * QT, NBLK + QT), 1)
    dist = NBLK + qi - ki
    valid = (dist >= 0) & (dist <= NBLK) & (jnp.logical_not(first) | (ki >= NBLK))
    head = lax.broadcasted_iota(jnp.int32, (1, ATT), 1) // EH
    masks = [(head == h).astype(f32) for h in range(4)]
    qs = jnp.concatenate([q * (mh * (1.0 / math.sqrt(EH))) for mh in masks], axis=0).astype(bf16)
    s = _dot_nt(qs, kk)
    s = jnp.where(valid, s, NEG)
    m = lax.stop_gradient(jnp.max(s, axis=-1, keepdims=True))
    p = jnp.exp(s - m)
    den = jnp.sum(p, axis=-1, keepdims=True)
    po = _dot((p * (1.0 / den)).astype(bf16), vv)
    lse = m + jnp.log(den)
    o = jnp.zeros((QT, ATT), f32)
    l = jnp.zeros((QT, ATT), f32)
    for h, mh in enumerate(masks):
        o = o + po[QT * h:QT * (h + 1)] * mh
        l = l + lse[QT * h:QT * (h + 1)] * mh
    return o, l


def _attn_specs(tile):
    own = lambda col: pl.BlockSpec((QT, ATT), lambda s: (tile(s), col))
    prev = lambda col: pl.BlockSpec((NBLK, ATT), lambda s: (jnp.maximum((QT // NBLK) * tile(s) - 1, 0), col))
    return [own(0), prev(1), own(1), prev(2), own(2)]


def _attn_fwd(qkv, per_seq):
    def body(q_ref, kp_ref, kc_ref, vp_ref, vc_ref, o_ref, l_ref):
        first = pl.program_id(0) % per_seq == 0
        o, l = _attn_block(q_ref[...], kp_ref[...], kc_ref[...], vp_ref[...], vc_ref[...], first)
        o_ref[...] = o
        l_ref[...] = l

    blk = pl.BlockSpec((QT, ATT), lambda t: (t, 0))
    return _pc(
        body, grid=(NB,), name="attn_fwd", in_specs=_attn_specs(lambda t: t), out_specs=[blk, blk],
        out_shape=[SDS((S, ATT), f32), SDS((S, ATT), f32)], compiler_params=_cp(32),
    )(qkv, qkv, qkv, qkv, qkv)


def _attn_block_bwd(q, kp, kc, vp, vc, o, l, do, dl, first):
    kk = jnp.concatenate([kp, kc], axis=0).astype(bf16)
    vv = jnp.concatenate([vp, vc], axis=0).astype(bf16)
    qi = lax.broadcasted_iota(jnp.int32, (4 * QT, NBLK + QT), 0) % QT
    ki = lax.broadcasted_iota(jnp.int32, (4 * QT, NBLK + QT), 1)
    dist = NBLK + qi - ki
    valid = (dist >= 0) & (dist <= NBLK) & (jnp.logical_not(first) | (ki >= NBLK))
    head = lax.broadcasted_iota(jnp.int32, (1, ATT), 1) // EH
    masks = [(head == h).astype(f32) for h in range(4)]
    scale = 1.0 / math.sqrt(EH)
    stack = lambda f: jnp.concatenate([f(mh) for mh in masks], axis=0)
    qs = stack(lambda mh: q * (mh * scale)).astype(bf16)
    s = jnp.where(valid, _dot_nt(qs, kk), NEG)
    lse = stack(lambda mh: jnp.max(jnp.where(mh > 0.0, l, NEG), axis=1, keepdims=True))
    p = jnp.exp(s - lse)
    dos = stack(lambda mh: do * mh).astype(bf16)
    dvv = _dot_tn(p.astype(bf16), dos)
    dp = _dot_nt(dos, vv)
    delta = stack(lambda mh: jnp.sum(do * o * mh, axis=1, keepdims=True))
    dlse = stack(lambda mh: jnp.sum(dl * mh, axis=1, keepdims=True))
    ds = (p * (dp - delta + dlse)).astype(bf16)
    dqs = _dot(ds, kk)
    dq = jnp.zeros((QT, ATT), f32)
    for h, mh in enumerate(masks):
        dq = dq + dqs[QT * h:QT * (h + 1)] * (mh * scale)
    dkk = _dot_tn(ds, qs)
    return dq, dkk[:NBLK], dkk[NBLK:], dvv[:NBLK], dvv[NBLK:]


def _attn_bwd(qkv, o, l, do, dl, per_seq):
    def body(q_ref, kp_ref, kc_ref, vp_ref, vc_ref, ofw_ref, lfw_ref, do_ref, dl_ref, o_ref, k_carry, v_carry):
        step = pl.program_id(0)

        @pl.when(step == 0)
        def _():
            k_carry[...] = jnp.zeros_like(k_carry)
            v_carry[...] = jnp.zeros_like(v_carry)

        t = NB - 1 - step
        first = t % per_seq == 0
        last = t % per_seq == per_seq - 1
        dq, dkp, dkc, dvp, dvc = _attn_block_bwd(q_ref[...], kp_ref[...], kc_ref[...], vp_ref[...], vc_ref[...], ofw_ref[...],
                                                 lfw_ref[...], do_ref[...], dl_ref[...], first)
        o_ref[:, 0:ATT] = dq
        o_ref[:, ATT:2 * ATT] = dkc
        o_ref[:, 2 * ATT:3 * ATT] = dvc
        o_ref[QT - NBLK:QT, ATT:2 * ATT] += jnp.where(last, 0.0, k_carry[...])
        o_ref[QT - NBLK:QT, 2 * ATT:3 * ATT] += jnp.where(last, 0.0, v_carry[...])
        k_carry[...] = dkp
        v_carry[...] = dvp

    rev = lambda s: NB - 1 - s
    blk = pl.BlockSpec((QT, ATT), lambda s: (rev(s), 0))
    return _pc(
        body, grid=(NB,), name="attn_bwd", in_specs=_attn_specs(rev) + [blk, blk, blk, blk],
        out_specs=pl.BlockSpec((QT, 768), lambda s: (rev(s), 0)),
        out_shape=SDS((S, 768), f32), scratch_shapes=[pltpu.VMEM((NBLK, ATT), f32)] * 2, compiler_params=_cp(40),
    )(qkv, qkv, qkv, qkv, qkv, o, l, do, dl)


def _merge_weights(l0, l1, l2):
    m = jnp.maximum(jnp.maximum(l0, l1), l2)
    e0, e1, e2 = jnp.exp(l0 - m), jnp.exp(l1 - m), jnp.exp(l2 - m)
    tot = e0 + e1 + e2
    return e0 / tot, e1 / tot, e2 / tot


def _merge_specs():
    nat = pl.BlockSpec((TM, ATT), lambda i: (i, 0))
    return nat, _res_spec(4, TM, ATT), _res_spec(16, TM, ATT)


def _merge_fwd(o1, l1, o4, l4, o16, l16):
    def body(o1_ref, l1_ref, o4_ref, l4_ref, o16_ref, l16_ref, y_ref, scr):
        o4v, l4v = _join_residues(o4_ref, 4, scr), _join_residues(l4_ref, 4, scr)
        o16v, l16v = _join_residues(o16_ref, 16, scr), _join_residues(l16_ref, 16, scr)
        w0, w1, w2 = _merge_weights(l1_ref[...], l4v, l16v)
        y_ref[...] = w0 * o1_ref[...] + w1 * o4v + w2 * o16v

    nat, r4, r16 = _merge_specs()
    return _pc(body, grid=(S // TM,), name="merge_fwd", in_specs=[nat, nat, r4, r4, r16, r16],
                          out_specs=nat, out_shape=SDS((S, ATT), f32), scratch_shapes=[pltpu.VMEM((2, TM, 128), f32)],
                          compiler_params=_cp(32))(o1, l1, o4, l4, o16, l16)


def _merge_bwd(o1, l1, o4, l4, o16, l16, dy):
    def body(o1_ref, l1_ref, o4_ref, l4_ref, o16_ref, l16_ref, dy_ref, do1_ref, dl1_ref, do4_ref, dl4_ref, do16_ref, dl16_ref, scr):
        o4v, l4v = _join_residues(o4_ref, 4, scr), _join_residues(l4_ref, 4, scr)
        o16v, l16v = _join_residues(o16_ref, 16, scr), _join_residues(l16_ref, 16, scr)
        o1v = o1_ref[...]
        w0, w1, w2 = _merge_weights(l1_ref[...], l4v, l16v)
        y = w0 * o1v + w1 * o4v + w2 * o16v
        dyv = dy_ref[...]
        do1_ref[...] = w0 * dyv
        dl1_ref[...] = w0 * (o1v - y) * dyv
        _split_residues(w1 * dyv, scr, [(do4_ref, 4)])
        _split_residues(w1 * (o4v - y) * dyv, scr, [(dl4_ref, 4)])
        _split_residues(w2 * dyv, scr, [(do16_ref, 16)])
        _split_residues(w2 * (o16v - y) * dyv, scr, [(dl16_ref, 16)])

    nat, r4, r16 = _merge_specs()
    return _pc(body, grid=(S // TM,), name="merge_bwd", in_specs=[nat, nat, r4, r4, r16, r16, nat],
                          out_specs=[nat, nat, r4, r4, r16, r16],
                          out_shape=[SDS((S, ATT), f32)] * 2 + [SDS((4, S // 4, ATT), f32)] * 2 + [SDS((16, S // 16, ATT), f32)] * 2,
                          scratch_shapes=[pltpu.VMEM((2, TM, 128), f32)], compiler_params=_cp(32))(o1, l1, o4, l4, o16, l16, dy)


HALO = 16


def _pool_consts(i, rows):
    grp = lax.broadcasted_iota(jnp.int32, (rows, 256), 1) // 64
    t = i * TM + lax.broadcasted_iota(jnp.int32, (rows, 256), 0)
    win = jnp.where(grp == 0, 2, jnp.where(grp == 1, 4, jnp.where(grp == 2, 8, 16)))
    cnt = jnp.minimum(t + 1, win).astype(f32)
    return grp, cnt


def _pool_select(grp, s2, s4, s8, s16):
    return jnp.where(grp == 0, s2, jnp.where(grp == 1, s4, jnp.where(grp == 2, s8, s16)))


def _pooled(i, cur, halo):
    xx = jnp.concatenate([halo, cur], axis=0)
    s2 = xx + pltpu.roll(xx, 1, 0)
    s4 = s2 + pltpu.roll(s2, 2, 0)
    s8 = s4 + pltpu.roll(s4, 4, 0)
    s16 = s8 + pltpu.roll(s8, 8, 0)
    grp, cnt = _pool_consts(i, TM)
    tot = _pool_select(grp, s2[HALO:], s4[HALO:], s8[HALO:], s16[HALO:])
    return tot / cnt - cur


def _pool_fwd(u, wp, scale):
    def body(u_ref, halo_ref, wp_ref, sc_ref, y_ref):
        i = pl.program_id(0)
        halo = halo_ref[...] * (i > 0).astype(f32)
        pooled = _pooled(i, u_ref[...], halo)
        y_ref[...] = _dot(pooled.astype(bf16), wp_ref[...]) * sc_ref[...]

    return _pc(
        body, grid=(S // TM,), name="pool_fwd",
        in_specs=[pl.BlockSpec((TM, 256), lambda i: (i, 0)),
                  pl.BlockSpec((HALO, 256), lambda i: (jnp.maximum(i * (TM // HALO) - 1, 0), 0)),
                  pl.BlockSpec((256, 256), lambda i: (0, 0)),
                  pl.BlockSpec((1, 256), lambda i: (0, 0))],
        out_specs=pl.BlockSpec((TM, 256), lambda i: (i, 0)), out_shape=SDS((S, 256), f32), compiler_params=_cp(32),
    )(u, u, wp, scale)


def _pool_bwd(u, wp, scale, dy):
    nt = S // TM

    def body(u_ref, halo_ref, wp_ref, sc_ref, dy_ref, dyn_ref, du_ref, dwp_ref, dsc_ref):
        i = pl.program_id(0)

        @pl.when(i == 0)
        def _():
            dwp_ref[...] = jnp.zeros_like(dwp_ref)
            dsc_ref[...] = jnp.zeros_like(dsc_ref)

        halo = halo_ref[...] * (i > 0).astype(f32)
        pooled = _pooled(i, u_ref[...], halo).astype(bf16)
        dyv = dy_ref[...]
        dsc_ref[...] += jnp.sum(dyv * _dot(pooled, wp_ref[...]), axis=0, keepdims=True)
        dys = (dyv * sc_ref[...]).astype(bf16)
        dwp_ref[...] += _dot_tn(pooled, dys)
        dpool = _dot_nt(dys, wp_ref[...])
        grp, cnt = _pool_consts(i, TM)
        dyn = ((dyn_ref[...] * (i < nt - 1).astype(f32)) * sc_ref[...]).astype(bf16)
        _, cntn = _pool_consts(i + 1, HALO)
        zn = _dot_nt(dyn, wp_ref[...]) / cntn
        zz = jnp.concatenate([dpool / cnt, zn], axis=0)
        n = TM + HALO
        a2 = zz + pltpu.roll(zz, n - 1, 0)
        a4 = a2 + pltpu.roll(a2, n - 2, 0)
        a8 = a4 + pltpu.roll(a4, n - 4, 0)
        a16 = a8 + pltpu.roll(a8, n - 8, 0)
        du_ref[...] = _pool_select(grp, a2[:TM], a4[:TM], a8[:TM], a16[:TM]) - dpool

    return _pc(
        body, grid=(nt,), name="pool_bwd",
        in_specs=[pl.BlockSpec((TM, 256), lambda i: (i, 0)),
                  pl.BlockSpec((HALO, 256), lambda i: (jnp.maximum(i * (TM // HALO) - 1, 0), 0)),
                  pl.BlockSpec((256, 256), lambda i: (0, 0)),
                  pl.BlockSpec((1, 256), lambda i: (0, 0)),
                  pl.BlockSpec((TM, 256), lambda i: (i, 0)),
                  pl.BlockSpec((HALO, 256), lambda i: (jnp.minimum((i + 1) * (TM // HALO), S // HALO - 1), 0))],
        out_specs=[pl.BlockSpec((TM, 256), lambda i: (i, 0)), pl.BlockSpec((256, 256), lambda i: (0, 0)),
                   pl.BlockSpec((1, 256), lambda i: (0, 0))],
        out_shape=[SDS((S, 256), f32), SDS((256, 256), f32), SDS((1, 256), f32)], compiler_params=_cp(32),
    )(u, u, wp, scale, dy, dy)


CW = 3 * DNW
CHALO = 8
TC = 256


def _conv_fwd(u, w):
    def body(u_ref, halo_ref, w_ref, c_ref):
        i = pl.program_id(0)
        xx = jnp.concatenate([halo_ref[...] * (i > 0).astype(f32), u_ref[...]], axis=0)
        c = (w_ref[3:4, :] * xx + w_ref[2:3, :] * pltpu.roll(xx, 1, 0) + w_ref[1:2, :] * pltpu.roll(xx, 2, 0)
             + w_ref[0:1, :] * pltpu.roll(xx, 3, 0))
        c_ref[...] = c[CHALO:]

    return _pc(
        body, grid=(S // TC,), name="conv_fwd",
        in_specs=[pl.BlockSpec((TC, CW), lambda i: (i, 0)),
                  pl.BlockSpec((CHALO, CW), lambda i: (jnp.maximum(i * (TC // CHALO) - 1, 0), 0)),
                  pl.BlockSpec((8, CW), lambda i: (0, 0))],
        out_specs=pl.BlockSpec((TC, CW), lambda i: (i, 0)), out_shape=SDS((S, CW), f32), compiler_params=_cp(32),
    )(u, u, w)


def _conv_bwd(u, w, dc):
    nt = S // TC

    def body(u_ref, halo_ref, w_ref, dc_ref, dcn_ref, du_ref, dw_ref):
        i = pl.program_id(0)

        @pl.when(i == 0)
        def _():
            dw_ref[...] = jnp.zeros_like(dw_ref)

        dcv = dc_ref[...]
        zz = jnp.concatenate([dcv, dcn_ref[...] * (i < nt - 1).astype(f32)], axis=0)
        n = TC + CHALO
        du = (w_ref[3:4, :] * zz + w_ref[2:3, :] * pltpu.roll(zz, n - 1, 0) + w_ref[1:2, :] * pltpu.roll(zz, n - 2, 0)
              + w_ref[0:1, :] * pltpu.roll(zz, n - 3, 0))
        du_ref[...] = du[:TC]
        xx = jnp.concatenate([halo_ref[...] * (i > 0).astype(f32), u_ref[...]], axis=0)
        for j in range(4):
            shifted = xx if j == 3 else pltpu.roll(xx, 3 - j, 0)
            dw_ref[j:j + 1, :] += jnp.sum(dcv * shifted[CHALO:], axis=0, keepdims=True)

    return _pc(
        body, grid=(nt,), name="conv_bwd",
        in_specs=[pl.BlockSpec((TC, CW), lambda i: (i, 0)),
                  pl.BlockSpec((CHALO, CW), lambda i: (jnp.maximum(i * (TC // CHALO) - 1, 0), 0)),
                  pl.BlockSpec((8, CW), lambda i: (0, 0)),
                  pl.BlockSpec((TC, CW), lambda i: (i, 0)),
                  pl.BlockSpec((CHALO, CW), lambda i: (jnp.minimum((i + 1) * (TC // CHALO), S // CHALO - 1), 0))],
        out_specs=[pl.BlockSpec((TC, CW), lambda i: (i, 0)), pl.BlockSpec((8, CW), lambda i: (0, 0))],
        out_shape=[SDS((S, CW), f32), SDS((8, CW), f32)], compiler_params=_cp(32),
    )(u, u, w, dc, dc)


TL = 512
NCL = TL // CH


def _ein(spec, a, b):
    return jnp.einsum(spec, a.astype(bf16), b.astype(bf16), preferred_element_type=f32)


def _ein_ct(spec, x, y, ct_first):
    ct = x if ct_first else y
    hi = ct.astype(bf16)
    lo = ct - hi.astype(f32)
    if ct_first:
        return _ein(spec, hi, y) + _ein(spec, lo, y)
    return _ein(spec, x, hi) + _ein(spec, x, lo)


def _bf16_dot(spec, grad_a, grad_b):
    @jax.custom_vjp
    def dot(a, b):
        return _ein(spec, a, b)

    def fwd(a, b):
        return _ein(spec, a, b), (a, b)

    def bwd(res, ct):
        a, b = res
        return grad_a(a, b, ct), grad_b(a, b, ct)

    dot.defvjp(fwd, bwd)
    return dot


def _bdot(a, b):
    return _ein('nik,nkj->nij', a, b)


def _bdot_nt(a, b):
    return _ein('nik,njk->nij', a, b)


def _bdot_tn(a, b):
    return _ein('nki,nkj->nij', a, b)


_mm = _bf16_dot('ik,kj->ij', lambda a, b, ct: _ein_ct('ij,kj->ik', ct, b, True), lambda a, b, ct: _ein_ct('ik,ij->kj', a, ct, False))
_mm_tn = _bf16_dot('ki,kj->ij', lambda a, b, ct: _ein_ct('kj,ij->ki', b, ct, False),
                   lambda a, b, ct: _ein_ct('ki,ij->kj', a, ct, False))


@jax.custom_vjp
def _inv_unit_lower(a):
    ii = lax.broadcasted_iota(jnp.int32, (1, CH, CH), 1)
    jj = lax.broadcasted_iota(jnp.int32, (1, CH, CH), 2)
    t = (ii == jj).astype(f32) - a
    p = a
    for _ in range(5):
        p = _bdot(p, p)
        t = t + _bdot(t, p)
    return t


def _inv_unit_lower_fwd(a):
    t = _inv_unit_lower(a)
    return t, t


def _inv_unit_lower_bwd(t, dt):
    return (-_bdot_tn(t, _bdot_nt(dt, t)),)


_inv_unit_lower.defvjp(_inv_unit_lower_fwd, _inv_unit_lower_bwd)


def _dn_local(c, dba, a_row, b_row):
    act = c * jax.nn.sigmoid(c)
    lane = lax.broadcasted_iota(jnp.int32, (TL, 128), 1)
    beta_all = jax.nn.sigmoid(dba)
    xs = dba + b_row
    softplus = jnp.maximum(xs, 0.0) + jnp.log(1.0 + jnp.exp(-jnp.abs(xs)))
    g_all = -jnp.exp(a_row) * softplus
    ii = lax.broadcasted_iota(jnp.int32, (1, CH, CH), 1)
    jj = lax.broadcasted_iota(jnp.int32, (1, CH, CH), 2)
    lower = jj <= ii
    strict = jj < ii
    eye = (ii == jj).astype(f32)
    us, ws, qgs, kds, intras = [], [], [], [], []
    aux = jnp.zeros((TL, 128), f32)
    for h in range(4):
        q = act[:, DH * h:DH * (h + 1)]
        k = act[:, DNW + DH * h:DNW + DH * (h + 1)]
        v = act[:, 2 * DNW + DH * h:2 * DNW + DH * (h + 1)]
        q = q * lax.rsqrt(jnp.sum(q * q, axis=-1, keepdims=True) + EPS) * (DH ** -0.5)
        k = k * lax.rsqrt(jnp.sum(k * k, axis=-1, keepdims=True) + EPS)
        beta = jnp.sum(jnp.where(lane == h, beta_all, 0.0), axis=1, keepdims=True)
        g = jnp.sum(jnp.where(lane == 4 + h, g_all, 0.0), axis=1, keepdims=True)
        q3, k3, v3 = q.reshape(NCL, CH, DH), k.reshape(NCL, CH, DH), v.reshape(NCL, CH, DH)
        beta3, g3 = beta.reshape(NCL, CH, 1), g.reshape(NCL, CH, 1)
        g_row = jnp.sum(eye * g3, axis=1, keepdims=True)
        gc_col = jnp.sum(jnp.where(lower, g_row, 0.0), axis=2, keepdims=True)
        gc_row = jnp.sum(jnp.where(ii <= jj, g3, 0.0), axis=1, keepdims=True)
        diff = gc_col - gc_row
        decay = jnp.where(lower, jnp.exp(jnp.where(lower, diff, 0.0)), 0.0)
        kb = k3 * beta3
        vb = v3 * beta3
        a = jnp.where(strict, _bdot_nt(kb, k3) * decay, 0.0)
        t = _inv_unit_lower(a)
        u3 = _bdot(t, vb)
        w3 = _bdot(t, kb * jnp.exp(gc_col))
        intra = jnp.where(lower, _bdot_nt(q3, k3) * decay, 0.0)
        g_last = jnp.sum(g3, axis=1, keepdims=True)
        us.append(u3.reshape(TL, DH))
        ws.append(w3.reshape(TL, DH))
        qgs.append((q3 * jnp.exp(gc_col)).reshape(TL, DH))
        kds.append((k3 * jnp.exp(g_last - gc_col)).reshape(TL, DH))
        intras.append(intra.reshape(TL, CH))
        e_last = jnp.broadcast_to(jnp.exp(g_last), (NCL, CH, 1)).reshape(TL, 1)
        aux = aux + jnp.where(lane == h, e_last, 0.0)
    cat = lambda xs: jnp.concatenate(xs, axis=1)
    return cat(us), cat(ws), cat(qgs), cat(kds), jnp.stack(intras, axis=0), aux


def _dn_local_fwd(c, dba, par):
    def body(c_ref, dba_ref, par_ref, u_ref, w_ref, qg_ref, kd_ref, in_ref, aux_ref):
        u, w, qg, kd, intra, aux = _dn_local(c_ref[...], dba_ref[...], par_ref[0:1, :], par_ref[1:2, :])
        u_ref[...] = u
        w_ref[...] = w
        qg_ref[...] = qg
        kd_ref[...] = kd
        in_ref[...] = intra
        aux_ref[...] = aux

    wide = pl.BlockSpec((TL, DNW), lambda i: (i, 0))
    return _pc(
        body, grid=(S // TL,), name="dn_local_fwd",
        in_specs=[pl.BlockSpec((TL, CW), lambda i: (i, 0)), pl.BlockSpec((TL, 128), lambda i: (i, 0)),
                  pl.BlockSpec((8, 128), lambda i: (0, 0))],
        out_specs=[wide, wide, wide, wide, pl.BlockSpec((4, TL, CH), lambda i: (0, i, 0)),
                   pl.BlockSpec((TL, 128), lambda i: (i, 0))],
        out_shape=[SDS((S, DNW), f32)] * 4 + [SDS((4, S, CH), f32), SDS((S, 128), f32)], compiler_params=_cp(48),
    )(c, dba, par)


def _dn_local_bwd(c, dba, par, du, dw, dqg, dkd, dintra, daux):
    def body(c_ref, dba_ref, par_ref, du_ref, dw_ref, dqg_ref, dkd_ref, din_ref, daux_ref, dc_ref, ddba_ref, dpar_ref):
        @pl.when(pl.program_id(0) == 0)
        def _():
            dpar_ref[...] = jnp.zeros_like(dpar_ref)

        _, vjp = jax.vjp(_dn_local, c_ref[...], dba_ref[...], par_ref[0:1, :], par_ref[1:2, :])
        dc, ddba, da_row, db_row = vjp((du_ref[...], dw_ref[...], dqg_ref[...], dkd_ref[...], din_ref[...], daux_ref[...]))
        dc_ref[...] = dc
        ddba_ref[...] = ddba
        dpar_ref[0:1, :] += da_row
        dpar_ref[1:2, :] += db_row

    wide = pl.BlockSpec((TL, DNW), lambda i: (i, 0))
    return _pc(
        body, grid=(S // TL,), name="dn_local_bwd",
        in_specs=[pl.BlockSpec((TL, CW), lambda i: (i, 0)), pl.BlockSpec((TL, 128), lambda i: (i, 0)),
                  pl.BlockSpec((8, 128), lambda i: (0, 0)), wide, wide, wide, wide,
                  pl.BlockSpec((4, TL, CH), lambda i: (0, i, 0)), pl.BlockSpec((TL, 128), lambda i: (i, 0))],
        out_specs=[pl.BlockSpec((TL, CW), lambda i: (i, 0)), pl.BlockSpec((TL, 128), lambda i: (i, 0)),
                   pl.BlockSpec((8, 128), lambda i: (0, 0))],
        out_shape=[SDS((S, CW), f32), SDS((S, 128), f32), SDS((8, 128), f32)], compiler_params=_cp(56),
    )(c, dba, par, du, dw, dqg, dkd, dintra, daux)


def _dn_step(state, u, w, qg, kd, intra, aux):
    lane = lax.broadcasted_iota(jnp.int32, (CH, 128), 1)
    row = lax.broadcasted_iota(jnp.int32, (CH, 128), 0)
    outs, states = [], []
    for h in range(4):
        sl = slice(DH * h, DH * (h + 1))
        st = state[h]
        e = jnp.sum(jnp.sum(jnp.where((lane == h) & (row == 0), aux, 0.0), axis=1, keepdims=True), axis=0, keepdims=True)
        v_new = u[:, sl] - _mm(w[:, sl], st)
        outs.append(_mm(qg[:, sl], st) + _mm(intra[h], v_new))
        states.append(st * e + _mm_tn(kd[:, sl], v_new))
    return jnp.concatenate(outs, axis=1), jnp.stack(states, axis=0)


CPS = 8
NSTEP = NCHUNK // CPS


def _dn_rec_specs(index):
    wide = pl.BlockSpec((CPS * CH, DNW), lambda n: (index(n), 0))
    inb = pl.BlockSpec((4, CPS * CH, CH), lambda n: (0, index(n), 0))
    auxb = pl.BlockSpec((CPS * CH, 128), lambda n: (index(n), 0))
    stb = pl.BlockSpec((CPS, 4, DH, DH), lambda n: (index(n), 0, 0, 0))
    return wide, inb, auxb, stb


def _dn_rec_fwd(u, w, qg, kd, intra, aux):
    def body(u_ref, w_ref, qg_ref, kd_ref, in_ref, aux_ref, o_ref, st_ref, st_scr):
        @pl.when(pl.program_id(0) == 0)
        def _():
            st_scr[...] = jnp.zeros_like(st_scr)

        st = st_scr[...]
        for k in range(CPS):
            rows = slice(CH * k, CH * (k + 1))
            st_ref[k] = st
            o, st = _dn_step(st, u_ref[rows, :], w_ref[rows, :], qg_ref[rows, :], kd_ref[rows, :], in_ref[:, rows, :],
                             aux_ref[rows, :])
            o_ref[rows, :] = o
        st_scr[...] = st

    wide, inb, auxb, stb = _dn_rec_specs(lambda n: n)
    return _pc(
        body, grid=(NSTEP,), name="dn_rec_fwd", in_specs=[wide, wide, wide, wide, inb, auxb], out_specs=[wide, stb],
        out_shape=[SDS((S, DNW), f32), SDS((NCHUNK, 4, DH, DH), f32)],
        scratch_shapes=[pltpu.VMEM((4, DH, DH), f32)], compiler_params=_cp(32),
    )(u, w, qg, kd, intra, aux)


def _dn_rec_bwd(u, w, qg, kd, intra, aux, states, do):
    def body(u_ref, w_ref, qg_ref, kd_ref, in_ref, aux_ref, st_ref, do_ref,
             du_ref, dw_ref, dqg_ref, dkd_ref, din_ref, daux_ref, ds_scr):
        @pl.when(pl.program_id(0) == 0)
        def _():
            ds_scr[...] = jnp.zeros_like(ds_scr)

        ds = ds_scr[...]
        for k in reversed(range(CPS)):
            rows = slice(CH * k, CH * (k + 1))
            _, vjp = jax.vjp(_dn_step, st_ref[k], u_ref[rows, :], w_ref[rows, :], qg_ref[rows, :], kd_ref[rows, :],
                             in_ref[:, rows, :], aux_ref[rows, :])
            ds, du, dw, dqg, dkd, din, daux = vjp((do_ref[rows, :], ds))
            du_ref[rows, :] = du
            dw_ref[rows, :] = dw
            dqg_ref[rows, :] = dqg
            dkd_ref[rows, :] = dkd
            din_ref[:, rows, :] = din
            daux_ref[rows, :] = daux
        ds_scr[...] = ds

    wide, inb, auxb, stb = _dn_rec_specs(lambda n: NSTEP - 1 - n)
    return _pc(
        body, grid=(NSTEP,), name="dn_rec_bwd", in_specs=[wide, wide, wide, wide, inb, auxb, stb, wide],
        out_specs=[wide, wide, wide, wide, inb, auxb],
        out_shape=[SDS((S, DNW), f32)] * 4 + [SDS((4, S, CH), f32), SDS((S, 128), f32)],
        scratch_shapes=[pltpu.VMEM((4, DH, DH), f32)], compiler_params=_cp(40),
    )(u, w, qg, kd, intra, aux, states, do)


def _dn_post(o, z, nw):
    parts = []
    for h in range(4):
        sl = slice(DH * h, DH * (h + 1))
        oh = o[:, sl]
        y = oh * lax.rsqrt(jnp.mean(oh * oh, axis=-1, keepdims=True) + EPS) * nw
        zh = z[:, sl]
        parts.append(y * (zh * jax.nn.sigmoid(zh)))
    return jnp.concatenate(parts, axis=1)


def _dn_post_fwd(o, z, nw):
    def body(o_ref, z_ref, nw_ref, y_ref):
        y_ref[...] = _dn_post(o_ref[...], z_ref[...], nw_ref[...])

    wide = pl.BlockSpec((TM, DNW), lambda i: (i, 0))
    return _pc(body, grid=(S // TM,), name="dn_post_fwd",
                          in_specs=[wide, wide, pl.BlockSpec((1, 128), lambda i: (0, 0))], out_specs=wide,
                          out_shape=SDS((S, DNW), f32), compiler_params=_cp(32))(o, z, nw)


def _dn_post_bwd(o, z, nw, dy):
    def body(o_ref, z_ref, nw_ref, dy_ref, do_ref, dz_ref, dnw_ref):
        @pl.when(pl.program_id(0) == 0)
        def _():
            dnw_ref[...] = jnp.zeros_like(dnw_ref)

        _, vjp = jax.vjp(_dn_post, o_ref[...], z_ref[...], nw_ref[...])
        do, dz, dnw = vjp(dy_ref[...])
        do_ref[...] = do
        dz_ref[...] = dz
        dnw_ref[...] += dnw

    wide = pl.BlockSpec((TM, DNW), lambda i: (i, 0))
    one = pl.BlockSpec((1, 128), lambda i: (0, 0))
    return _pc(body, grid=(S // TM,), name="dn_post_bwd", in_specs=[wide, wide, one, wide],
                          out_specs=[wide, wide, one], out_shape=[SDS((S, DNW), f32), SDS((S, DNW), f32), SDS((1, 128), f32)],
                          compiler_params=_cp(32))(o, z, nw, dy)


def _row_tile(rows, width, itemsize=4, target=2 * 1024 * 1024):
    best = None
    for t in range(16, rows + 1, 16):
        if rows % t == 0 and t * width * itemsize <= target:
            best = t
    return best if best is not None else rows


def _sum_pieces(pieces, out_dtype, name):
    n, rows, width = pieces.shape
    tr = _row_tile(rows, width * n)

    def body(p_ref, o_ref):
        acc = p_ref[0].astype(f32)
        for s in range(1, n):
            acc = acc + p_ref[s].astype(f32)
        o_ref[...] = acc.astype(out_dtype)

    return _pc(body, grid=(rows // tr,), name=name,
                          in_specs=[pl.BlockSpec((n, tr, width), lambda i: (0, i, 0))],
                          out_specs=pl.BlockSpec((tr, width), lambda i: (i, 0)),
                          out_shape=SDS((rows, width), out_dtype), compiler_params=_cp(32))(pieces)


def _sum_core_pair(part, got, c_arr):
    n, rows, width = part.shape
    half = rows // 2
    tr = _row_tile(half, width, itemsize=2)
    nt = half // tr

    def body(c_ref, p_ref, g_ref, o_ref):
        o_ref[...] = (p_ref[...].astype(f32) + g_ref[...].astype(f32)).astype(bf16)

    gs = pltpu.PrefetchScalarGridSpec(
        num_scalar_prefetch=1, grid=(n, nt),
        in_specs=[pl.BlockSpec((1, tr, width), lambda j, i, c: (j, c[0] * nt + i, 0)),
                  pl.BlockSpec((1, tr, width), lambda j, i, c: (j, i, 0))],
        out_specs=pl.BlockSpec((1, tr, width), lambda j, i, c: (j, i, 0)))
    return _pc(body, grid_spec=gs, name="sum_core_pair", out_shape=SDS((n, half, width), bf16),
                          compiler_params=_cp(32))(c_arr, part, got)


def _sum_chips(pieces, c_arr, full, row0, total_rows):
    n, half, width = pieces.shape
    tr = max(t for t in range(16, 257, 16) if half % t == 0 and row0 % t == 0)
    nt = half // tr

    def body(c_ref, p_ref, *rest):
        o_ref = rest[-1]
        acc = p_ref[0].astype(f32)
        for s in range(1, n):
            acc = acc + p_ref[s].astype(f32)
        o_ref[...] = acc

    gs = pltpu.PrefetchScalarGridSpec(
        num_scalar_prefetch=1, grid=(nt,),
        in_specs=[pl.BlockSpec((n, tr, width), lambda i, c: (0, i, 0))] + ([] if full is None else [ANY]),
        out_specs=pl.BlockSpec((tr, width), lambda i, c: (row0 // tr + c[0] * nt + i, 0)))
    args = (c_arr, pieces) if full is None else (c_arr, pieces, full)
    return _pc(body, grid_spec=gs, name="sum_chips", out_shape=SDS((total_rows, width), f32),
                          input_output_aliases={} if full is None else {2: 0}, compiler_params=_cp(32))(*args)


def _adamw_math(w, g, m, v):
    mn = ADAM_B1 * m + (1.0 - ADAM_B1) * g
    vn = ADAM_B2 * v + (1.0 - ADAM_B2) * (g * g)
    m_hat = mn / (1.0 - ADAM_B1 ** ADAM_STEP)
    v_hat = vn / (1.0 - ADAM_B2 ** ADAM_STEP)
    return -ADAM_LR * (m_hat / (jnp.sqrt(v_hat) + ADAM_EPS) + ADAM_WD * w), mn, vn


def _adamw(w, g, m, v, name):
    rows, width = w.shape
    tr = _row_tile(rows, width * 7, target=12 * 1024 * 1024)

    def body(w_ref, g_ref, m_ref, v_ref, d_ref, nm_ref, nv_ref):
        d_ref[...], nm_ref[...], nv_ref[...] = _adamw_math(w_ref[...], g_ref[...], m_ref[...], v_ref[...])

    blk = pl.BlockSpec((tr, width), lambda i: (i, 0))
    return _pc(body, grid=(rows // tr,), name=name, in_specs=[blk] * 4, out_specs=[blk] * 3,
                          out_shape=[SDS((rows, width), f32)] * 3, compiler_params=_cp(40))(w, g, m, v)


def _adamw_rows(w, m, v, gblob, tr, first_tile, name):
    layers, rows, width = w.shape

    def body(w_ref, g_ref, m_ref, v_ref, d_ref, nm_ref, nv_ref):
        d_ref[0], nm_ref[0], nv_ref[0] = _adamw_math(w_ref[0], g_ref[...], m_ref[0], v_ref[0])

    blk = pl.BlockSpec((1, tr, width), lambda l, i: (l, i, 0))
    gblk = pl.BlockSpec((tr, width), lambda l, i: (first_tile(l) + i, 0))
    return _pc(body, grid=(layers, rows // tr), name=name, in_specs=[blk, gblk, blk, blk], out_specs=[blk] * 3,
                          out_shape=[SDS(w.shape, f32)] * 3, compiler_params=_cp(40))(w, gblob, m, v)


ANY = pl.BlockSpec(memory_space=pl.ANY)


def _place():
    x, y, c = lax.axis_index("x"), lax.axis_index("y"), lax.axis_index("c")
    chips = [(1 - x, y), (x, 1 - y), (1 - x, 1 - y)]
    return x, y, c, chips


NQ_ICI = 4
NQ_D2D = 8


def _chunks(rows, want):
    n = max(k for k in range(1, want + 1) if rows % k == 0 and (rows // k) % 16 == 0)
    step = rows // n
    return [(q * step, step) for q in range(n)]


def _scatter_copies(ins, outs, ssem, rsem, lsem):
    x, y, c, chips = _place()
    me = (x, y, c)
    locals_, sends, lands = [], [], []
    for b in range(len(ins)):
        for q, (off, n) in enumerate(_chunks(ins[b].shape[1], NQ_ICI)):
            rows = pl.ds(off, n)
            mine = outs[b].at[2 * x + y, rows, :]
            locals_.append(pltpu.make_async_copy(ins[b].at[2 * x + y, rows, :], mine, lsem.at[b, q]))
            for j, chip in enumerate(chips):
                sends.append(_remote(ins[b].at[2 * chip[0] + chip[1], rows, :], mine, ssem.at[b, j, q], rsem.at[b, j, q],
                                     (*chip, c)))
                slot = outs[b].at[2 * chip[0] + chip[1], rows, :]
                lands.append(_remote(slot, slot, ssem.at[b, j, q], rsem.at[b, j, q], me))
    return locals_, sends, lands


def _scatter_start(ins, outs, ssem, rsem, lsem):
    locals_, sends, _ = _scatter_copies(ins, outs, ssem, rsem, lsem)
    for cp in locals_ + sends:
        cp.start()


def _scatter_finish(ins, outs, ssem, rsem, lsem):
    locals_, sends, lands = _scatter_copies(ins, outs, ssem, rsem, lsem)
    for cp in lands:
        cp.wait_recv()
    for cp in sends:
        cp.wait_send()
    for cp in locals_:
        cp.wait()


def _scatter_sems(nb):
    return [pltpu.SemaphoreType.DMA((nb, 3, NQ_ICI)), pltpu.SemaphoreType.DMA((nb, 3, NQ_ICI)),
            pltpu.SemaphoreType.DMA((nb, NQ_ICI))]


def _remote(src, dst, ssem, rsem, dev):
    return pltpu.make_async_remote_copy(src_ref=src, dst_ref=dst, send_sem=ssem, recv_sem=rsem, device_id=dev,
                                        device_id_type=MESH)


def _all_gather_weights(shards):
    nb = len(shards)

    def body(*refs):
        ins, outs, sems = refs[:nb], refs[nb:2 * nb], refs[2 * nb:]
        _gather_start(ins, outs, *sems)
        _gather_finish(ins, outs, *sems)

    return _pc(
        body, name="all_gather_weights", in_specs=[ANY] * nb, out_specs=[ANY] * nb,
        out_shape=[SDS((NCH,) + s.shape, s.dtype) for s in shards], scratch_shapes=_gather_sems(nb),
    )(*shards)


def _rows_to_move(ref):
    return 3 * FC + 256 if ref.shape[0] == 4 * FC else ref.shape[0]


def _gather_first(ins, outs, ssem, rsem, lsem):
    x, y, c, chips = _place()
    locals_, sends = [], []
    for b in range(len(ins)):
        half = _rows_to_move(ins[b]) // 2
        for q, (off, n) in enumerate(_chunks(half, NQ_ICI)):
            mine = pl.ds(c * half + off, n)
            own = outs[b].at[2 * x + y, mine, :]
            locals_.append(pltpu.make_async_copy(ins[b].at[mine, :], own, lsem.at[b, q]))
            sends.append(_remote(ins[b].at[mine, :], own, ssem.at[b, 0, q], rsem.at[b, 0, q], (x, y, 1 - c)))
            sends += [_remote(ins[b].at[mine, :], own, ssem.at[b, 1 + j, q], rsem.at[b, 1 + j, q], (*chip, c))
                      for j, chip in enumerate(chips)]
    return locals_, sends


def _gather_start(ins, outs, ssem, rsem, lsem):
    locals_, sends = _gather_first(ins, outs, ssem, rsem, lsem)
    for cp in locals_ + sends:
        cp.start()


def _gather_finish(ins, outs, ssem, rsem, lsem):
    x, y, c, chips = _place()
    me, sib = (x, y, c), (x, y, 1 - c)
    locals_, sends = _gather_first(ins, outs, ssem, rsem, lsem)
    for b in range(len(ins)):
        half = _rows_to_move(ins[b]) // 2
        for q, (off, n) in enumerate(_chunks(half, NQ_ICI)):
            mine = pl.ds(c * half + off, n)
            for j, chip in enumerate(chips):
                landed = outs[b].at[2 * chip[0] + chip[1], mine, :]
                _remote(landed, landed, ssem.at[b, 1 + j, q], rsem.at[b, 1 + j, q], me).wait_recv()
                cp = _remote(landed, landed, ssem.at[b, 4 + j, q], rsem.at[b, 4 + j, q], sib)
                cp.start()
                sends.append(cp)
    for b in range(len(ins)):
        half = _rows_to_move(ins[b]) // 2
        for q, (off, n) in enumerate(_chunks(half, NQ_ICI)):
            other = pl.ds((1 - c) * half + off, n)
            theirs = outs[b].at[2 * x + y, other, :]
            _remote(theirs, theirs, ssem.at[b, 0, q], rsem.at[b, 0, q], me).wait_recv()
            for j, chip in enumerate(chips):
                fwd = outs[b].at[2 * chip[0] + chip[1], other, :]
                _remote(fwd, fwd, ssem.at[b, 4 + j, q], rsem.at[b, 4 + j, q], me).wait_recv()
    for cp in sends:
        cp.wait_send()
    for cp in locals_:
        cp.wait()


def _gather_sems(nb):
    return [pltpu.SemaphoreType.DMA((nb, 7, NQ_ICI)), pltpu.SemaphoreType.DMA((nb, 7, NQ_ICI)),
            pltpu.SemaphoreType.DMA((nb, NQ_ICI))]


def _send_sibling_half(parts):
    nb = len(parts)

    def body(*refs):
        ins, gots = refs[:nb], refs[nb:2 * nb]
        ssem, rsem = refs[2 * nb:]
        x, y, c, _ = _place()
        sib = (x, y, 1 - c)
        todo = []
        for b in range(nb):
            half = ins[b].shape[1] // 2
            for q, (off, n) in enumerate(_chunks(half, NQ_D2D)):
                cp = _remote(ins[b].at[:, pl.ds((1 - c) * half + off, n), :], gots[b].at[:, pl.ds(off, n), :],
                             ssem.at[b, q], rsem.at[b, q], sib)
                cp.start()
                todo.append(cp)
        for cp in todo:
            cp.wait()

    return _pc(
        body, name="send_sibling_half", in_specs=[ANY] * nb, out_specs=[ANY] * nb,
        out_shape=[SDS((p.shape[0], p.shape[1] // 2, p.shape[2]), p.dtype) for p in parts],
        scratch_shapes=[pltpu.SemaphoreType.DMA((nb, NQ_D2D)), pltpu.SemaphoreType.DMA((nb, NQ_D2D))],
    )(*parts)


def _scatter_to_chips(parts):
    nb = len(parts)

    def body(*refs):
        ins, outs, sems = refs[:nb], refs[nb:2 * nb], refs[2 * nb:]
        _scatter_start(ins, outs, *sems)
        _scatter_finish(ins, outs, *sems)

    return _pc(
        body, name="scatter_to_chips", in_specs=[ANY] * nb, out_specs=[ANY] * nb,
        out_shape=[SDS(p.shape, p.dtype) for p in parts], scratch_shapes=_scatter_sems(nb),
    )(*parts)


def _scatter_begin(sums):
    hbm = pl.BlockSpec(memory_space=pltpu.HBM)
    sem = pl.BlockSpec(memory_space=pltpu.SEMAPHORE)

    def body(src_ref, land_ref, send_sem, recv_sem, src_thru, land_thru, token):
        x, y, c, chips = _place()
        for chip in chips:
            _remote(src_ref.at[2 * chip[0] + chip[1]], land_ref.at[2 * x + y], send_sem, recv_sem, (*chip, c)).start()
        token[...] = jnp.zeros_like(token)

    pin = lambda a: pltpu.with_memory_space_constraint(a, pltpu.HBM)
    return pl.pallas_call(
        body, name="scatter_begin",
        out_shape=(pltpu.SemaphoreType.DMA(()), pltpu.SemaphoreType.DMA(()), pltpu.HBM(sums.shape, sums.dtype),
                   pltpu.HBM(sums.shape, sums.dtype), SDS((8, 128), f32)),
        in_specs=(hbm, hbm), out_specs=(sem, sem, hbm, hbm, pl.BlockSpec(memory_space=pltpu.VMEM)),
        input_output_aliases={0: 2, 1: 3},
        compiler_params=pltpu.CompilerParams(has_side_effects=pltpu.SideEffectType.DATAFLOW_SIDE_EFFECTING),
    )(pin(sums), pin(sums + jnp.zeros_like(sums)))


def _scatter_end(send_sem, recv_sem, src_thru, land_thru, after):
    hbm = pl.BlockSpec(memory_space=pltpu.HBM)
    sem = pl.BlockSpec(memory_space=pltpu.SEMAPHORE)

    def body(src_ref, land_ref, send_sem, recv_sem, after_ref, src_dead, got_ref):
        x, y, c, _ = _place()
        three = pl.ds(0, 3)
        cp = _remote(src_ref.at[three], land_ref.at[three], send_sem, recv_sem, (x, y, c))
        cp.wait_send()
        cp.wait_recv()

    return pl.pallas_call(
        body, name="scatter_end",
        out_shape=(pltpu.HBM(src_thru.shape, src_thru.dtype), pltpu.HBM(land_thru.shape, land_thru.dtype)),
        in_specs=(hbm, hbm, sem, sem, pl.BlockSpec(memory_space=pl.ANY)), out_specs=(hbm, hbm),
        input_output_aliases={0: 0, 1: 1},
        compiler_params=pltpu.CompilerParams(has_side_effects=pltpu.SideEffectType.DATAFLOW_SIDE_EFFECTING),
    )(src_thru, land_thru, send_sem, recv_sem, after)[1]


def _join_halves(fulls, ranges):
    nb = len(fulls)
    nr = max(len(r) for r in ranges)

    def body(*refs):
        ins, outs = refs[:nb], refs[nb:2 * nb]
        ssem, rsem = refs[2 * nb:]
        x, y, c, _ = _place()
        sib = (x, y, 1 - c)
        sends, lands = [], []
        for b in range(nb):
            for g, (row0, rows) in enumerate(ranges[b]):
                half = rows // 2
                for q, (off, n) in enumerate(_chunks(half, NQ_D2D)):
                    mine = pl.ds(row0 + c * half + off, n)
                    sends.append(_remote(ins[b].at[mine, :], outs[b].at[mine, :], ssem.at[b, g, q], rsem.at[b, g, q], sib))
                    other = outs[b].at[pl.ds(row0 + (1 - c) * half + off, n), :]
                    lands.append(_remote(other, other, ssem.at[b, g, q], rsem.at[b, g, q], sib))
        for cp in sends:
            cp.start()
        for cp in lands:
            cp.wait_recv()
        for cp in sends:
            cp.wait_send()

    return _pc(
        body, name="join_halves", in_specs=[ANY] * nb, out_specs=[ANY] * nb,
        out_shape=[SDS(h.shape, h.dtype) for h in fulls], input_output_aliases={b: b for b in range(nb)},
        scratch_shapes=[pltpu.SemaphoreType.DMA((nb, nr, NQ_D2D)), pltpu.SemaphoreType.DMA((nb, nr, NQ_D2D))],
    )(*fulls)


def _gather_small(vec):
    def body(v_ref, o_ref, ssem, rsem, lsem):
        x, y, c, _ = _place()
        mine = o_ref.at[4 * x + 2 * y + c]
        local = pltpu.make_async_copy(v_ref, mine, lsem)
        local.start()
        sends = []
        for k in range(1, 8):
            peer = (x ^ (k >> 2), y ^ ((k >> 1) & 1), c ^ (k & 1))
            cp = _remote(v_ref, mine, ssem.at[k - 1], rsem.at[k - 1], peer)
            cp.start()
            sends.append(cp)
        for k in range(1, 8):
            px, py, pc = x ^ (k >> 2), y ^ ((k >> 1) & 1), c ^ (k & 1)
            slot = o_ref.at[4 * px + 2 * py + pc]
            _remote(slot, slot, ssem.at[k - 1], rsem.at[k - 1], (x, y, c)).wait_recv()
        for cp in sends:
            cp.wait_send()
        local.wait()

    return _pc(
        body, name="gather_small", in_specs=[ANY], out_specs=ANY, out_shape=SDS((8,) + vec.shape, vec.dtype),
        scratch_shapes=[pltpu.SemaphoreType.DMA((7,)), pltpu.SemaphoreType.DMA((7,)), pltpu.SemaphoreType.DMA],
    )(vec)


def _block_diag(pw):
    return jnp.concatenate([jnp.pad(pw[g], ((0, 0), (64 * g, 192 - 64 * g))) for g in range(4)], axis=0)


def _own_columns(full, chip):
    n = full.shape[-1] // NCH
    parts = full.reshape(full.shape[:-1] + (NCH, n))
    sel = (lax.broadcasted_iota(jnp.int32, (NCH, 1), 0) == chip)
    return jnp.sum(jnp.where(sel, parts, 0.0), axis=-2)


def _at_own_columns(shard, chip):
    n = shard.shape[-1]
    sel = (lax.broadcasted_iota(jnp.int32, (NCH * n,), 0) // n == chip)
    return jnp.where(sel, jnp.tile(shard, NCH), 0.0)


def _pad_rows(a, rows):
    return jnp.pad(a, ((0, rows - a.shape[0]),) + ((0, 0),) * (a.ndim - 1))


def _ffn_block(l, which):
    return 7 * l + 3 * which


def _wout_block(l):
    return 7 * l + 6


class _Weights:
    def __init__(self):
        self.ffn, self.wout, self.w_aug, self.rides = {}, {}, {}, {}

    @classmethod
    def from_blob(cls, blob, w_aug):
        self = cls()
        for l in range(DEPTH):
            self.ffn[(l, 0)], self.ffn[(l, 1)] = (blob, _ffn_block(l, 0)), (blob, _ffn_block(l, 1))
            self.wout[l], self.w_aug[l] = (blob, _wout_block(l)), w_aug[l]
        return self

    def set_w_in(self, l, gathered):
        self.w_aug[l] = jnp.pad(gathered.transpose(1, 0, 2).reshape(D, INW), ((0, 0), (0, INP - INW)))

    def ffn_fwd(self, l, which, x, nw):
        arr, k0 = self.ffn[(l, which)]
        if (l, which) not in self.rides:
            return _ffn_fwd(x, nw, arr, k0)
        shards, landed = self.rides[(l, which)]
        out, *gathered = _ffn_fwd(x, nw, arr, k0, shards)
        landed(gathered)
        return out


def _layer_fwd(l, x0, pos, freq, wts, ws):
    sv = {"x0": x0}
    x1 = ws.ffn_fwd(l, 0, x0, wts["ffn1_norm"][l:l + 1])
    att, att4, att16, pu, dq, dz, dba = _inproj_fwd(x1, wts["mix_norm"][l:l + 1], ws.w_aug[l], pos, freq)
    qkvs = [att, att4.reshape(S, 768), att16.reshape(S, 768)]
    (o1, l1), (o4, l4), (o16, l16) = [_attn_fwd(q, NB // d) for q, d in zip(qkvs, PATTERN_DIL)]
    ols = (o1, l1, o4.reshape(4, S // 4, ATT), l4.reshape(4, S // 4, ATT), o16.reshape(16, S // 16, ATT),
           l16.reshape(16, S // 16, ATT))
    ya = _merge_fwd(*ols)
    yb = _pool_fwd(pu, wts["pool_bd"][l], wts["pool_scale"][l:l + 1])
    c = _conv_fwd(dq, wts["conv_w"][l])
    u, w, qg, kd, intra, aux = _dn_local_fwd(c, dba, wts["dn_par"][l])
    o_dn, states = _dn_rec_fwd(u, w, qg, kd, intra, aux)
    yc = _dn_post_fwd(o_dn, dz, wts["dn_out_norm"][l:l + 1])
    x2 = _outproj_fwd(x1, ya, yb, yc, *ws.wout[l])
    x3 = ws.ffn_fwd(l, 1, x2, wts["ffn2_norm"][l:l + 1])
    sv.update(x1=x1, x2=x2, qkvs=qkvs, ols=ols, ya=ya, yb=yb, yc=yc, pu=pu, dq=dq, dz=dz, dba=dba, c=c,
              u=u, w=w, qg=qg, kd=kd, intra=intra, aux=aux, states=states, o_dn=o_dn)
    return x3, sv


def _wout_part(g):
    return jnp.pad(g.astype(bf16).reshape(NCH, 256, D), ((0, 0), (0, FC - 256), (0, 0)))


def _win_part(g):
    return g[:, :INW].astype(bf16).reshape(D, NCH, INC).transpose(1, 0, 2)


def _layer_bwd(l, dx3, sv, pos, freq, wts, ws, ride=None, prep=None):
    gr = {}
    g2, u2, d2, dh4, *pieces_before = _ffn_bwd(sv["x2"], wts["ffn2_norm"][l:l + 1], *ws.ffn[(l, 1)], dx3, ride)
    gr.update(ffn2_w_gate=g2, ffn2_w_up=u2, ffn2_w_down=d2)
    dx2, gr["ffn2_norm"], dya, dyb, dyc, gr["w_out"] = _outproj_bwd(sv["x2"], wts["ffn2_norm"][l:l + 1], dx3, dh4, sv["ya"],
                                                                     sv["yb"], sv["yc"], *ws.wout[l])
    do_dn, ddz, gr["dn_out_norm"] = _dn_post_bwd(sv["o_dn"], sv["dz"], wts["dn_out_norm"][l:l + 1], dyc)
    du, dw, dqg, dkd, dintra, daux = _dn_rec_bwd(sv["u"], sv["w"], sv["qg"], sv["kd"], sv["intra"], sv["aux"], sv["states"], do_dn)
    dc, ddba, gr["dn_par"] = _dn_local_bwd(sv["c"], sv["dba"], wts["dn_par"][l], du, dw, dqg, dkd, dintra, daux)
    ddq, gr["conv_w"] = _conv_bwd(sv["dq"], wts["conv_w"][l], dc)
    dpu, gr["pool_bd"], gr["pool_scale"] = _pool_bwd(sv["pu"], wts["pool_bd"][l], wts["pool_scale"][l:l + 1], dyb)
    dols = _merge_bwd(*sv["ols"], dya)
    flat = lambda a: a.reshape(S, ATT)
    datts = [_attn_bwd(q, flat(sv["ols"][2 * p]), flat(sv["ols"][2 * p + 1]), flat(dols[2 * p]), flat(dols[2 * p + 1]), NB // d)
             for p, (q, d) in enumerate(zip(sv["qkvs"], PATTERN_DIL))]
    dx1, gr["mix_norm"], gr["w_aug"] = _inproj_bwd(sv["x1"], wts["mix_norm"][l:l + 1], ws.w_aug[l], pos, freq, dx2,
                                                    datts[0], datts[1].reshape(4, S // 4, 768),
                                                    datts[2].reshape(16, S // 16, 768), dpu, ddq, ddz, ddba)
    own = None
    if prep is not None:
        own = prep([jnp.concatenate([g2, u2, d2, _wout_part(gr["w_out"])], axis=1), _win_part(gr["w_aug"])])
    g1, u1, d1, dh4, *pieces_own = _ffn_bwd(sv["x0"], wts["ffn1_norm"][l:l + 1], *ws.ffn[(l, 0)], dx1, own)
    begun, nw1 = None, wts["ffn1_norm"][l:l + 1]
    if prep is not None:
        begun = _scatter_begin(prep([jnp.concatenate([g1, u1, d1], axis=1)])[0])
        nw1 = nw1 + begun[4][0, 0]
    dx0, gr["ffn1_norm"] = _norm_bwd(sv["x0"], nw1, dx1, dh4)
    gr.update(ffn1_w_gate=g1, ffn1_w_up=u1, ffn1_w_down=d1)
    return dx0, gr, pieces_before, pieces_own, begun


def _device_step(x, pos, target, wts, ws, prep=None):
    freq = jnp.tile(ROPE_THETA ** (-jnp.arange(0, EH, 2, dtype=f32) / EH), 2 * ATT // EH).reshape(1, ATT)
    saved = []
    h = x
    for l in range(DEPTH):
        h, sv = _layer_fwd(l, h, pos, freq, wts, ws)
        saved.append(sv)
    dh, g_final, loss = _final(h, wts["final_norm"], target)
    grads = [None] * DEPTH
    dh, grads[1], *_ = _layer_bwd(1, dh, saved[1], pos, freq, wts, ws)
    sums1 = None
    if prep is not None:
        g = grads[1]
        ffn = [g[f"ffn{f}_w_{n}"] for f in (1, 2) for n in ("gate", "up", "down")]
        sums1 = prep([jnp.concatenate(ffn + [_wout_part(g["w_out"])], axis=1), _win_part(g["w_aug"])])
    dh, grads[0], pieces1, pieces0, begun = _layer_bwd(0, dh, saved[0], pos, freq, wts, ws, sums1, prep)
    return loss, dh, g_final, grads, pieces1, pieces0, begun


_SMALL = (("ffn1_norm", (DEPTH, D)), ("mix_norm", (DEPTH, D)), ("pool_w", (DEPTH, 4, 64, 64)), ("pool_scale", (DEPTH, 256)),
          ("dn_conv_w", (DEPTH, 4, CW)), ("dn_a_log", (DEPTH, 4)), ("dn_dt_bias", (DEPTH, 4)), ("dn_out_norm", (DEPTH, 128)),
          ("ffn2_norm", (DEPTH, D)), ("final_norm", (D,)), ("loss", (1,)))


def _pack_small(vals):
    rows = []
    for name, shape in _SMALL:
        flat = vals[name].astype(f32).reshape(-1)
        rows.append(jnp.pad(flat, (0, _small_rows(shape) * 128 - flat.shape[0])).reshape(-1, 128))
    out = jnp.concatenate(rows, axis=0)
    return _pad_rows(out, -(-out.shape[0] // 16) * 16)


def _small_rows(shape):
    return -(-int(np.prod(shape)) // 1024) * 8


def _unpack_small(packed):
    vals, r = {}, 0
    for name, shape in _SMALL:
        size, n = int(np.prod(shape)), _small_rows(shape)
        vals[name] = packed[r:r + n].reshape(-1)[:size].reshape(shape)
        r += n
    return vals


def kernel(x, positions, ffn1_norm, ffn1_w_gate, ffn1_w_up, ffn1_w_down, mix_norm, w_in, pool_w, pool_scale, dn_conv_w, dn_a_log, dn_dt_bias, dn_out_norm, w_out, ffn2_norm, ffn2_w_gate, ffn2_w_up, ffn2_w_down, final_norm, loss_target, m_ffn1_norm, m_ffn1_w_gate, m_ffn1_w_up, m_ffn1_w_down, m_mix_norm, m_w_in, m_pool_w, m_pool_scale, m_dn_conv_w, m_dn_a_log, m_dn_dt_bias, m_dn_out_norm, m_w_out, m_ffn2_norm, m_ffn2_w_gate, m_ffn2_w_up, m_ffn2_w_down, m_final_norm, v_ffn1_norm, v_ffn1_w_gate, v_ffn1_w_up, v_ffn1_w_down, v_mix_norm, v_w_in, v_pool_w, v_pool_scale, v_dn_conv_w, v_dn_a_log, v_dn_dt_bias, v_dn_out_norm, v_w_out, v_ffn2_norm, v_ffn2_w_gate, v_ffn2_w_up, v_ffn2_w_down, v_final_norm):
    names = ["ffn1_norm", "ffn1_w_gate", "ffn1_w_up", "ffn1_w_down", "mix_norm", "w_in", "pool_w", "pool_scale", "dn_conv_w",
             "dn_a_log", "dn_dt_bias", "dn_out_norm", "w_out", "ffn2_norm", "ffn2_w_gate", "ffn2_w_up", "ffn2_w_down", "final_norm"]
    W = dict(zip(names, [ffn1_norm, ffn1_w_gate, ffn1_w_up, ffn1_w_down, mix_norm, w_in, pool_w, pool_scale, dn_conv_w,
                         dn_a_log, dn_dt_bias, dn_out_norm, w_out, ffn2_norm, ffn2_w_gate, ffn2_w_up, ffn2_w_down, final_norm]))
    M = dict(zip(names, [m_ffn1_norm, m_ffn1_w_gate, m_ffn1_w_up, m_ffn1_w_down, m_mix_norm, m_w_in, m_pool_w, m_pool_scale,
                         m_dn_conv_w, m_dn_a_log, m_dn_dt_bias, m_dn_out_norm, m_w_out, m_ffn2_norm, m_ffn2_w_gate, m_ffn2_w_up,
                         m_ffn2_w_down, m_final_norm]))
    V = dict(zip(names, [v_ffn1_norm, v_ffn1_w_gate, v_ffn1_w_up, v_ffn1_w_down, v_mix_norm, v_w_in, v_pool_w, v_pool_scale,
                         v_dn_conv_w, v_dn_a_log, v_dn_dt_bias, v_dn_out_norm, v_w_out, v_ffn2_norm, v_ffn2_w_gate, v_ffn2_w_up,
                         v_ffn2_w_down, v_final_norm]))
    chip = 2 * lax.axis_index("x") + lax.axis_index("y")

    ffn_names = [(f"ffn{f}_w_gate", f"ffn{f}_w_up", f"ffn{f}_w_down") for f in (1, 2)]
    tr = lambda t: jnp.swapaxes(t, -1, -2)
    def ffn_rows(l, which):
        g, u, dn = ffn_names[which]
        return [tr(W[g][l]), tr(W[u][l]), W[dn][l]]

    def second_half(l):
        return jnp.concatenate(ffn_rows(l, 1) + [jnp.pad(W["w_out"][l], ((0, FC - 256), (0, 0)))], axis=0).astype(bf16)

    ws = _Weights()
    first0, = _all_gather_weights([jnp.concatenate(ffn_rows(0, 0), axis=0).astype(bf16)])
    ws.ffn[(0, 0)] = (first0, 0)

    def landed_00(gathered):
        ws.ffn[(0, 1)], ws.wout[0] = (gathered[0], 0), (gathered[0], 3)
        ws.set_w_in(0, gathered[1])

    def landed_01(gathered):
        ws.ffn[(1, 0)] = (gathered[0], 0)
        ws.set_w_in(1, gathered[1])

    def landed_10(gathered):
        ws.ffn[(1, 1)], ws.wout[1] = (gathered[0], 0), (gathered[0], 3)

    ws.rides[(0, 0)] = ([second_half(0), W["w_in"][0].astype(bf16)], landed_00)
    ws.rides[(0, 1)] = ([jnp.concatenate(ffn_rows(1, 0), axis=0).astype(bf16), W["w_in"][1].astype(bf16)], landed_01)
    ws.rides[(1, 0)] = ([second_half(1)], landed_10)
    conv_all = _gather_small(_pad_rows(dn_conv_w.reshape(DEPTH * 4 * (CW // NCH) // 128, 128), 32))
    conv_full = jnp.concatenate([conv_all[2 * j, :DEPTH * 4 * (CW // NCH) // 128].reshape(DEPTH, 4, CW // NCH) for j in range(NCH)],
                                axis=-1)

    par = jnp.pad(jnp.stack([dn_a_log, dn_dt_bias], axis=1), ((0, 0), (0, 6), (4, 120)))
    wts = dict(ffn1_norm=ffn1_norm, mix_norm=mix_norm, ffn2_norm=ffn2_norm, final_norm=final_norm.reshape(1, D),
               pool_bd=jnp.stack([_block_diag(pool_w[l]) for l in range(DEPTH)]).astype(bf16),
               pool_scale=pool_scale, conv_w=jnp.pad(conv_full, ((0, 0), (0, 4), (0, 0))),
               dn_par=par, dn_out_norm=dn_out_norm)

    c_arr = lax.axis_index("c").astype(jnp.int32).reshape(1)

    def prep(parts):
        return [_sum_core_pair(p, g, c_arr) for p, g in zip(parts, _send_sibling_half(parts))]

    loss, dx, g_final, grads, pieces1, pieces0, begun = _device_step(x[0], positions.reshape(S, 1), loss_target[0], wts, ws,
                                                                      prep)

    small = {"loss": loss[0, 0:1], "final_norm": g_final.reshape(D)}
    for n in ("ffn1_norm", "mix_norm", "ffn2_norm", "pool_scale", "dn_out_norm"):
        small[n] = jnp.stack([grads[l][n].reshape(-1) for l in range(DEPTH)])
    small["pool_w"] = jnp.stack([jnp.stack([grads[l]["pool_bd"][64 * g:64 * (g + 1), 64 * g:64 * (g + 1)] for g in range(4)])
                                 for l in range(DEPTH)])
    small["dn_conv_w"] = jnp.stack([grads[l]["conv_w"][0:4] for l in range(DEPTH)])
    small["dn_a_log"] = jnp.stack([grads[l]["dn_par"][0, 4:8] for l in range(DEPTH)])
    small["dn_dt_bias"] = jnp.stack([grads[l]["dn_par"][1, 4:8] for l in range(DEPTH)])
    full_b = _sum_chips(pieces0[0], c_arr, None, 3 * FC, RB)
    full_b = _sum_chips(pieces1[0], c_arr, full_b, 7 * FC, RB)
    full_c = _sum_chips(pieces0[1], c_arr, None, 0, RC)
    full_c = _sum_chips(pieces1[1], c_arr, full_c, D, RC)
    core = lax.axis_index("c")
    own_b = lax.dynamic_slice_in_dim(full_b, 7 * FC + core * (7 * FC // 2), 1, axis=0)[:, 0:128]
    own_c = lax.dynamic_slice_in_dim(full_c, D + core * (D // 2), 1, axis=0)[:, 0:128]
    packed = _pack_small(small) + 0.0 * (own_b + own_c)
    g_small = _sum_pieces(_gather_small(packed), f32, "sum_small")
    gs = _unpack_small(g_small)
    full_b = _sum_chips(_scatter_end(*begun[:4], g_small), c_arr, full_b, 0, RB)
    full_b, full_c = _join_halves([full_b, full_c], [[(0, 3 * FC), (3 * FC, 4 * FC), (7 * FC, 7 * FC)], [(0, D), (D, D)]])

    transposed = ("ffn1_w_gate", "ffn1_w_up", "ffn2_w_gate", "ffn2_w_up")
    where = {"ffn1_w_gate": (full_b, FC // 2, lambda l: 14 * l), "ffn1_w_up": (full_b, FC // 2, lambda l: 14 * l + 2),
             "ffn1_w_down": (full_b, FC // 2, lambda l: 14 * l + 4), "ffn2_w_gate": (full_b, FC // 2, lambda l: 14 * l + 6),
             "ffn2_w_up": (full_b, FC // 2, lambda l: 14 * l + 8), "ffn2_w_down": (full_b, FC // 2, lambda l: 14 * l + 10),
             "w_out": (full_b, 64, lambda l: (FC // 64) * (7 * l + 6)), "w_in": (full_c, D // 2, lambda l: 2 * l)}
    big_res = {}
    for n, (gblob, tile, first) in where.items():
        t = tr if n in transposed else (lambda a: a)
        big_res[n] = [t(r) for r in _adamw_rows(t(W[n]), t(M[n]), t(V[n]), gblob, tile, first, "adamw_" + n)]

    def small_of(T):
        d = {n: T[n] for n, _ in _SMALL if n not in ("loss", "dn_conv_w")}
        d["loss"] = jnp.zeros((1,), f32)
        d["dn_conv_w"] = _at_own_columns(T["dn_conv_w"], chip)
        return _pack_small(d)

    res_s = _adamw(small_of(W), g_small, small_of(M), small_of(V), "adamw_small")
    small_out = [_unpack_small(r) for r in res_s]

    def split_blobs(b, c):
        out = {}
        b7 = b.reshape(DEPTH, 7, FC, D)
        for k, n in enumerate(n for names3 in ffn_names for n in names3):
            out[n] = tr(b7[:, k]) if n in transposed else b7[:, k]
        out["w_out"] = b7[:, 6, :256]
        out["w_in"] = c.reshape(DEPTH, D, INC)
        return out

    def assemble(big, sm):
        out = []
        for n in names:
            if n in big:
                out.append(big[n])
            elif n == "dn_conv_w":
                out.append(_own_columns(sm[n], chip))
            else:
                out.append(sm[n])
        return out

    grad_list = assemble(split_blobs(full_b, full_c), gs)
    outs = [gs["loss"].reshape(()), dx.reshape(1, S, D)] + grad_list
    for k in range(3):
        outs += assemble({n: r[k] for n, r in big_res.items()}, small_out[k])
    return tuple(outs)
```
